```python
import math
import jax, jax.numpy as jnp
from jax import lax
import numpy as np

D_MODEL = 1024
BATCH = 8
SEQ = 4096
DEPTH = 4

MIX_WIDTH = D_MODEL
FOX_WIDTH = MIX_WIDTH // 2
FOX_HEAD_DIM = 64
FOX_HEADS = FOX_WIDTH // FOX_HEAD_DIM
POOL_WIDTH = MIX_WIDTH - FOX_WIDTH
POOL_WINDOWS = (2, 4, 8, 16)
POOL_GROUPS = len(POOL_WINDOWS)
POOL_GROUP_DIM = POOL_WIDTH // POOL_GROUPS
Q_BLOCK = 128
MEM_LEN = 256
X_HEADS = 4
X_HEAD_DIM = D_MODEL // X_HEADS
D_FF = 4 * D_MODEL
EPS = 1e-6
IN_COLS = 3 * FOX_WIDTH + FOX_HEADS + POOL_WIDTH

kernel_name = "fox_pool_hybrid_memory_trunk"


def rms_norm(x, g):
    x32 = x.astype(jnp.float32)
    y = x32 * lax.rsqrt(jnp.mean(x32 * x32, axis=-1, keepdims=True) + EPS)
    return (y * g.astype(jnp.float32)).astype(x.dtype)


def forgetting_attention(q, k, v, fg_logit):
    S = q.shape[1]
    scale = 1.0 / math.sqrt(q.shape[-1])
    log_f = jax.nn.log_sigmoid(fg_logit.astype(jnp.float32))
    c = jnp.transpose(jnp.cumsum(log_f, axis=1), (0, 2, 1))
    outs = []
    for blk in range(S // Q_BLOCK):
        q0, q1 = blk * Q_BLOCK, (blk + 1) * Q_BLOCK
        qb = q[:, q0:q1]
        kb = k[:, :q1]
        vb = v[:, :q1]
        s = jnp.einsum('bqhd,bkhd->bhqk', qb, kb).astype(jnp.float32) * scale
        s = s + c[:, :, q0:q1, None] - c[:, :, None, :q1]
        causal = jnp.arange(q1)[None, :] <= jnp.arange(q0, q1)[:, None]
        s = jnp.where(causal, s, -jnp.inf)
        p = jax.nn.softmax(s, axis=-1).astype(v.dtype)
        outs.append(jnp.einsum('bhqk,bkhd->bqhd', p, vb))
    return jnp.concatenate(outs, axis=1)


def causal_pool_mixer(u, w_groups, scale):
    B, S, _ = u.shape
    ug = u.reshape(B, S, POOL_GROUPS, POOL_GROUP_DIM)
    u32 = ug.astype(jnp.float32)
    csum = jnp.cumsum(u32, axis=1)
    pos = jnp.arange(S)
    pooled = []
    for g, w in enumerate(POOL_WINDOWS):
        cg = csum[:, :, g]
        lag = jnp.concatenate([jnp.zeros((B, w, POOL_GROUP_DIM), jnp.float32), cg[:, :S - w]], axis=1)
        count = jnp.minimum(pos + 1, w).astype(jnp.float32)[None, :, None]
        pooled.append((cg - lag) / count - u32[:, :, g])
    pooled = jnp.stack(pooled, axis=2).astype(u.dtype)
    y = jnp.einsum('bsgc,gcd->bsgd', pooled, w_groups)
    return y.reshape(B, S, POOL_WIDTH) * scale


def memory_cross_attention(h, m, wq, wkv, wo):
    B, S, _ = h.shape
    q = (h @ wq).reshape(B, S, X_HEADS, X_HEAD_DIM)
    kv = m @ wkv
    k = kv[..., :D_MODEL].reshape(B, MEM_LEN, X_HEADS, X_HEAD_DIM)
    v = kv[..., D_MODEL:].reshape(B, MEM_LEN, X_HEADS, X_HEAD_DIM)
    s = jnp.einsum('bqhd,bkhd->bhqk', q, k).astype(jnp.float32) / math.sqrt(X_HEAD_DIM)
    p = jax.nn.softmax(s, axis=-1).astype(v.dtype)
    o = jnp.einsum('bhqk,bkhd->bqhd', p, v).reshape(B, S, D_MODEL)
    return o @ wo


def _fwd_setup_inputs(seed: int = 0) -> dict:
    key = jax.random.key(seed)
    ks = jax.random.split(key, 20)
    f32 = jnp.float32

    def w(k, shape, fan_in):
        return jax.random.normal(k, shape, f32) * (fan_in ** -0.5)

    def gain(k):
        return 1.0 + 0.02 * jax.random.normal(k, (DEPTH, D_MODEL), f32)

    return {
        "x": jax.random.normal(ks[0], (BATCH, SEQ, D_MODEL), f32),
        "mem": jax.random.normal(ks[1], (BATCH, MEM_LEN, D_MODEL), f32),
        "g_mix_pre": gain(ks[2]),
        "w_in": w(ks[3], (DEPTH, D_MODEL, IN_COLS), D_MODEL),
        "b_forget": 2.0 + 0.1 * jax.random.normal(ks[4], (DEPTH, FOX_HEADS), f32),
        "pool_w": w(ks[5], (DEPTH, POOL_GROUPS, POOL_GROUP_DIM, POOL_GROUP_DIM), POOL_GROUP_DIM),
        "pool_scale": 1.0 + 0.02 * jax.random.normal(ks[6], (DEPTH, POOL_WIDTH), f32),
        "w_out": w(ks[7], (DEPTH, MIX_WIDTH, D_MODEL), MIX_WIDTH),
        "g_mix_post": gain(ks[8]),
        "g_x_pre": gain(ks[9]),
        "g_mem": gain(ks[10]),
        "wq_x": w(ks[11], (DEPTH, D_MODEL, D_MODEL), D_MODEL),
        "wkv_x": w(ks[12], (DEPTH, D_MODEL, 2 * D_MODEL), D_MODEL),
        "wo_x": w(ks[13], (DEPTH, D_MODEL, D_MODEL), D_MODEL),
        "g_x_post": gain(ks[14]),
        "g_ffn_pre": gain(ks[15]),
        "w_up": w(ks[16], (DEPTH, D_MODEL, D_FF), D_MODEL),
        "w_down": w(ks[17], (DEPTH, D_FF, D_MODEL), D_FF),
        "g_ffn_post": gain(ks[18]),
    }


def _fwd_reference(x, mem, g_mix_pre, w_in, b_forget, pool_w, pool_scale, w_out, g_mix_post,
              g_x_pre, g_mem, wq_x, wkv_x, wo_x, g_x_post, g_ffn_pre, w_up, w_down, g_ffn_post):
    B, S, _ = x.shape
    o_q, o_k, o_v = 0, FOX_WIDTH, 2 * FOX_WIDTH
    o_f = 3 * FOX_WIDTH
    o_p = o_f + FOX_HEADS
    for l in range(DEPTH):
        h = rms_norm(x, g_mix_pre[l])
        proj = h @ w_in[l]
        q = proj[..., o_q:o_k].reshape(B, S, FOX_HEADS, FOX_HEAD_DIM)
        k = proj[..., o_k:o_v].reshape(B, S, FOX_HEADS, FOX_HEAD_DIM)
        v = proj[..., o_v:o_f].reshape(B, S, FOX_HEADS, FOX_HEAD_DIM)
        fg_logit = proj[..., o_f:o_p] + b_forget[l]
        u = proj[..., o_p:]
        attn_out = forgetting_attention(q, k, v, fg_logit).reshape(B, S, FOX_WIDTH)
        pool_out = causal_pool_mixer(u, pool_w[l], pool_scale[l])
        mix = jnp.concatenate([attn_out, pool_out], axis=-1) @ w_out[l]
        x = x + rms_norm(mix, g_mix_post[l])
        h = rms_norm(x, g_x_pre[l])
        m = rms_norm(mem, g_mem[l])
        xo = memory_cross_attention(h, m, wq_x[l], wkv_x[l], wo_x[l])
        x = x + rms_norm(xo, g_x_post[l])
        h = rms_norm(x, g_ffn_pre[l])
        a = jnp.square(jax.nn.relu(h @ w_up[l]))
        x = x + rms_norm(a @ w_down[l], g_ffn_post[l])
    return x


import jax as _jax
import jax.numpy as _jnp

TWIN_FORMAT = 'train_step'
FWD_PARAMS = ['x', 'mem', 'g_mix_pre', 'w_in', 'b_forget', 'pool_w', 'pool_scale', 'w_out', 'g_mix_post', 'g_x_pre', 'g_mem', 'wq_x', 'wkv_x', 'wo_x', 'g_x_post', 'g_ffn_pre', 'w_up', 'w_down', 'g_ffn_post']
TWIN_WEIGHTS = ['g_mix_pre', 'w_in', 'b_forget', 'pool_w', 'pool_scale', 'w_out', 'g_mix_post', 'g_x_pre', 'g_mem', 'wq_x', 'wkv_x', 'wo_x', 'g_x_post', 'g_ffn_pre', 'w_up', 'w_down', 'g_ffn_post']
TWIN_DIFF_INPUT = 'x'
TWIN_INPUTS = ['x', 'mem', 'g_mix_pre', 'w_in', 'b_forget', 'pool_w', 'pool_scale', 'w_out', 'g_mix_post', 'g_x_pre', 'g_mem', 'wq_x', 'wkv_x', 'wo_x', 'g_x_post', 'g_ffn_pre', 'w_up', 'w_down', 'g_ffn_post', 'loss_target', 'm_g_mix_pre', 'm_w_in', 'm_b_forget', 'm_pool_w', 'm_pool_scale', 'm_w_out', 'm_g_mix_post', 'm_g_x_pre', 'm_g_mem', 'm_wq_x', 'm_wkv_x', 'm_wo_x', 'm_g_x_post', 'm_g_ffn_pre', 'm_w_up', 'm_w_down', 'm_g_ffn_post', 'v_g_mix_pre', 'v_w_in', 'v_b_forget', 'v_pool_w', 'v_pool_scale', 'v_w_out', 'v_g_mix_post', 'v_g_x_pre', 'v_g_mem', 'v_wq_x', 'v_wkv_x', 'v_wo_x', 'v_g_x_post', 'v_g_ffn_pre', 'v_w_up', 'v_w_down', 'v_g_ffn_post']
TWIN_OUTPUTS = ['loss', 'grad_x', 'grad_g_mix_pre', 'grad_w_in', 'grad_b_forget', 'grad_pool_w', 'grad_pool_scale', 'grad_w_out', 'grad_g_mix_post', 'grad_g_x_pre', 'grad_g_mem', 'grad_wq_x', 'grad_wkv_x', 'grad_wo_x', 'grad_g_x_post', 'grad_g_ffn_pre', 'grad_w_up', 'grad_w_down', 'grad_g_ffn_post', 'delta_g_mix_pre', 'delta_w_in', 'delta_b_forget', 'delta_pool_w', 'delta_pool_scale', 'delta_w_out', 'delta_g_mix_post', 'delta_g_x_pre', 'delta_g_mem', 'delta_wq_x', 'delta_wkv_x', 'delta_wo_x', 'delta_g_x_post', 'delta_g_ffn_pre', 'delta_w_up', 'delta_w_down', 'delta_g_ffn_post', 'new_m_g_mix_pre', 'new_m_w_in', 'new_m_b_forget', 'new_m_pool_w', 'new_m_pool_scale', 'new_m_w_out', 'new_m_g_mix_post', 'new_m_g_x_pre', 'new_m_g_mem', 'new_m_wq_x', 'new_m_wkv_x', 'new_m_wo_x', 'new_m_g_x_post', 'new_m_g_ffn_pre', 'new_m_w_up', 'new_m_w_down', 'new_m_g_ffn_post', 'new_v_g_mix_pre', 'new_v_w_in', 'new_v_b_forget', 'new_v_pool_w', 'new_v_pool_scale', 'new_v_w_out', 'new_v_g_mix_post', 'new_v_g_x_pre', 'new_v_g_mem', 'new_v_wq_x', 'new_v_wkv_x', 'new_v_wo_x', 'new_v_g_x_post', 'new_v_g_ffn_pre', 'new_v_w_up', 'new_v_w_down', 'new_v_g_ffn_post']
TWIN_LEAF_KINDS = {'loss': 'loss', 'grad_x': 'grad_x', 'grad_g_mix_pre': 'grad_w', 'grad_w_in': 'grad_w', 'grad_b_forget': 'grad_w', 'grad_pool_w': 'grad_w', 'grad_pool_scale': 'grad_w', 'grad_w_out': 'grad_w', 'grad_g_mix_post': 'grad_w', 'grad_g_x_pre': 'grad_w', 'grad_g_mem': 'grad_w', 'grad_wq_x': 'grad_w', 'grad_wkv_x': 'grad_w', 'grad_wo_x': 'grad_w', 'grad_g_x_post': 'grad_w', 'grad_g_ffn_pre': 'grad_w', 'grad_w_up': 'grad_w', 'grad_w_down': 'grad_w', 'grad_g_ffn_post': 'grad_w', 'delta_g_mix_pre': 'delta_w', 'delta_w_in': 'delta_w', 'delta_b_forget': 'delta_w', 'delta_pool_w': 'delta_w', 'delta_pool_scale': 'delta_w', 'delta_w_out': 'delta_w', 'delta_g_mix_post': 'delta_w', 'delta_g_x_pre': 'delta_w', 'delta_g_mem': 'delta_w', 'delta_wq_x': 'delta_w', 'delta_wkv_x': 'delta_w', 'delta_wo_x': 'delta_w', 'delta_g_x_post': 'delta_w', 'delta_g_ffn_pre': 'delta_w', 'delta_w_up': 'delta_w', 'delta_w_down': 'delta_w', 'delta_g_ffn_post': 'delta_w', 'new_m_g_mix_pre': 'new_m', 'new_m_w_in': 'new_m', 'new_m_b_forget': 'new_m', 'new_m_pool_w': 'new_m', 'new_m_pool_scale': 'new_m', 'new_m_w_out': 'new_m', 'new_m_g_mix_post': 'new_m', 'new_m_g_x_pre': 'new_m', 'new_m_g_mem': 'new_m', 'new_m_wq_x': 'new_m', 'new_m_wkv_x': 'new_m', 'new_m_wo_x': 'new_m', 'new_m_g_x_post': 'new_m', 'new_m_g_ffn_pre': 'new_m', 'new_m_w_up': 'new_m', 'new_m_w_down': 'new_m', 'new_m_g_ffn_post': 'new_m', 'new_v_g_mix_pre': 'new_v', 'new_v_w_in': 'new_v', 'new_v_b_forget': 'new_v', 'new_v_pool_w': 'new_v', 'new_v_pool_scale': 'new_v', 'new_v_w_out': 'new_v', 'new_v_g_mix_post': 'new_v', 'new_v_g_x_pre': 'new_v', 'new_v_g_mem': 'new_v', 'new_v_wq_x': 'new_v', 'new_v_wkv_x': 'new_v', 'new_v_wo_x': 'new_v', 'new_v_g_x_post': 'new_v', 'new_v_g_ffn_pre': 'new_v', 'new_v_w_up': 'new_v', 'new_v_w_down': 'new_v', 'new_v_g_ffn_post': 'new_v'}


def _forward(args):
    return _fwd_reference(*[args[k] for k in FWD_PARAMS])


def _output_shape():
    out = _jax.eval_shape(lambda: _forward(_fwd_setup_inputs(0)))
    return out.shape, out.dtype

N_MICROBATCH = 1
ADAM_LR = 0.001
ADAM_B1 = 0.9
ADAM_B2 = 0.999
ADAM_EPS = 1e-08
ADAM_WD = 0.01
ADAM_STEP = 10
PER_EXAMPLE_BATCH_AXIS = {'x': 0, 'mem': 0, 'loss_target': 0}
SHARED_INPUTS = []
_WEIGHT_DTYPES = {'g_mix_pre': _jnp.float32, 'w_in': _jnp.float32, 'b_forget': _jnp.float32, 'pool_w': _jnp.float32, 'pool_scale': _jnp.float32, 'w_out': _jnp.float32, 'g_mix_post': _jnp.float32, 'g_x_pre': _jnp.float32, 'g_mem': _jnp.float32, 'wq_x': _jnp.float32, 'wkv_x': _jnp.float32, 'wo_x': _jnp.float32, 'g_x_post': _jnp.float32, 'g_ffn_pre': _jnp.float32, 'w_up': _jnp.float32, 'w_down': _jnp.float32, 'g_ffn_post': _jnp.float32}
MOMENT_SCALE = {'g_mix_pre': 1.389654e+01, 'w_in': 9.497744e+00, 'b_forget': 3.167600e+00, 'pool_w': 3.571759e+00, 'pool_scale': 4.065666e+00, 'w_out': 1.377212e+01, 'g_mix_post': 3.527805e+01, 'g_x_pre': 7.515929e+00, 'g_mem': 2.401726e+01, 'wq_x': 7.438479e+00, 'wkv_x': 1.640506e+01, 'wo_x': 2.227757e+01, 'g_x_post': 4.084853e+01, 'g_ffn_pre': 1.113673e+01, 'w_up': 5.645461e+00, 'w_down': 2.074450e+01, 'g_ffn_post': 3.884034e+01}


def _to_microbatches(a, axis):
    t = _jnp.moveaxis(a, axis, 0)
    t = t.reshape((N_MICROBATCH, t.shape[0] // N_MICROBATCH) + t.shape[1:])
    return _jnp.moveaxis(t, 1, axis + 1)


def setup_inputs(seed: int = 0) -> dict:
    inp = _fwd_setup_inputs(seed)
    key = _jax.random.fold_in(_jax.random.key(seed), 7919)
    shape, _ = _output_shape()
    out = dict(inp)
    out["loss_target"] = _jax.random.normal(_jax.random.fold_in(key, 0), shape, _jnp.float32)
    for i, name in enumerate(TWIN_WEIGHTS):
        w = inp[name].astype(_jnp.float32)
        if MOMENT_SCALE is None:
            s = _jnp.sqrt(_jnp.mean(_jnp.square(w)) + 1e-30)
        else:
            s = MOMENT_SCALE[name]
        km, kv = _jax.random.split(_jax.random.fold_in(key, i + 1))
        out[name] = w
        out["m_" + name] = s * _jax.random.normal(km, w.shape, _jnp.float32)
        out["v_" + name] = (s * s) * _jax.random.uniform(kv, w.shape, _jnp.float32, 0.5, 1.5)
    if N_MICROBATCH > 1:
        for name, axis in PER_EXAMPLE_BATCH_AXIS.items():
            out[name] = _to_microbatches(out[name], axis)
    return {'x': out['x'], 'mem': out['mem'], 'g_mix_pre': out['g_mix_pre'], 'w_in': out['w_in'], 'b_forget': out['b_forget'], 'pool_w': out['pool_w'], 'pool_scale': out['pool_scale'], 'w_out': out['w_out'], 'g_mix_post': out['g_mix_post'], 'g_x_pre': out['g_x_pre'], 'g_mem': out['g_mem'], 'wq_x': out['wq_x'], 'wkv_x': out['wkv_x'], 'wo_x': out['wo_x'], 'g_x_post': out['g_x_post'], 'g_ffn_pre': out['g_ffn_pre'], 'w_up': out['w_up'], 'w_down': out['w_down'], 'g_ffn_post': out['g_ffn_post'], 'loss_target': out['loss_target'], 'm_g_mix_pre': out['m_g_mix_pre'], 'm_w_in': out['m_w_in'], 'm_b_forget': out['m_b_forget'], 'm_pool_w': out['m_pool_w'], 'm_pool_scale': out['m_pool_scale'], 'm_w_out': out['m_w_out'], 'm_g_mix_post': out['m_g_mix_post'], 'm_g_x_pre': out['m_g_x_pre'], 'm_g_mem': out['m_g_mem'], 'm_wq_x': out['m_wq_x'], 'm_wkv_x': out['m_wkv_x'], 'm_wo_x': out['m_wo_x'], 'm_g_x_post': out['m_g_x_post'], 'm_g_ffn_pre': out['m_g_ffn_pre'], 'm_w_up': out['m_w_up'], 'm_w_down': out['m_w_down'], 'm_g_ffn_post': out['m_g_ffn_post'], 'v_g_mix_pre': out['v_g_mix_pre'], 'v_w_in': out['v_w_in'], 'v_b_forget': out['v_b_forget'], 'v_pool_w': out['v_pool_w'], 'v_pool_scale': out['v_pool_scale'], 'v_w_out': out['v_w_out'], 'v_g_mix_post': out['v_g_mix_post'], 'v_g_x_pre': out['v_g_x_pre'], 'v_g_mem': out['v_g_mem'], 'v_wq_x': out['v_wq_x'], 'v_wkv_x': out['v_wkv_x'], 'v_wo_x': out['v_wo_x'], 'v_g_x_post': out['v_g_x_post'], 'v_g_ffn_pre': out['v_g_ffn_pre'], 'v_w_up': out['v_w_up'], 'v_w_down': out['v_w_down'], 'v_g_ffn_post': out['v_g_ffn_post']}


def _loss(weights, diff, rest, loss_target):
    with _jax.named_scope("forward"):
        args = {**rest, TWIN_DIFF_INPUT: diff, **{k: w.astype(_WEIGHT_DTYPES[k]) for k, w in weights.items()}}
        y = _forward(args)
    with _jax.named_scope("loss_head"):
        err = _jnp.square(y.astype(_jnp.float32) - loss_target)
        return 0.5 * _jnp.sum(_jnp.mean(err, axis=-1)) if err.ndim else 0.5 * err


def _adamw(w, g, m, v):
    m = ADAM_B1 * m + (1.0 - ADAM_B1) * g
    v = ADAM_B2 * v + (1.0 - ADAM_B2) * _jnp.square(g)
    m_hat = m / (1.0 - ADAM_B1 ** ADAM_STEP)
    v_hat = v / (1.0 - ADAM_B2 ** ADAM_STEP)
    delta = -ADAM_LR * (m_hat / (_jnp.sqrt(v_hat) + ADAM_EPS) + ADAM_WD * w)
    return delta, m, v


def reference(x, mem, g_mix_pre, w_in, b_forget, pool_w, pool_scale, w_out, g_mix_post, g_x_pre, g_mem, wq_x, wkv_x, wo_x, g_x_post, g_ffn_pre, w_up, w_down, g_ffn_post, loss_target, m_g_mix_pre, m_w_in, m_b_forget, m_pool_w, m_pool_scale, m_w_out, m_g_mix_post, m_g_x_pre, m_g_mem, m_wq_x, m_wkv_x, m_wo_x, m_g_x_post, m_g_ffn_pre, m_w_up, m_w_down, m_g_ffn_post, v_g_mix_pre, v_w_in, v_b_forget, v_pool_w, v_pool_scale, v_w_out, v_g_mix_post, v_g_x_pre, v_g_mem, v_wq_x, v_wkv_x, v_wo_x, v_g_x_post, v_g_ffn_pre, v_w_up, v_w_down, v_g_ffn_post):
    given = dict(x=x, mem=mem, g_mix_pre=g_mix_pre, w_in=w_in, b_forget=b_forget, pool_w=pool_w, pool_scale=pool_scale, w_out=w_out, g_mix_post=g_mix_post, g_x_pre=g_x_pre, g_mem=g_mem, wq_x=wq_x, wkv_x=wkv_x, wo_x=wo_x, g_x_post=g_x_post, g_ffn_pre=g_ffn_pre, w_up=w_up, w_down=w_down, g_ffn_post=g_ffn_post, loss_target=loss_target, m_g_mix_pre=m_g_mix_pre, m_w_in=m_w_in, m_b_forget=m_b_forget, m_pool_w=m_pool_w, m_pool_scale=m_pool_scale, m_w_out=m_w_out, m_g_mix_post=m_g_mix_post, m_g_x_pre=m_g_x_pre, m_g_mem=m_g_mem, m_wq_x=m_wq_x, m_wkv_x=m_wkv_x, m_wo_x=m_wo_x, m_g_x_post=m_g_x_post, m_g_ffn_pre=m_g_ffn_pre, m_w_up=m_w_up, m_w_down=m_w_down, m_g_ffn_post=m_g_ffn_post, v_g_mix_pre=v_g_mix_pre, v_w_in=v_w_in, v_b_forget=v_b_forget, v_pool_w=v_pool_w, v_pool_scale=v_pool_scale, v_w_out=v_w_out, v_g_mix_post=v_g_mix_post, v_g_x_pre=v_g_x_pre, v_g_mem=v_g_mem, v_wq_x=v_wq_x, v_wkv_x=v_wkv_x, v_wo_x=v_wo_x, v_g_x_post=v_g_x_post, v_g_ffn_pre=v_g_ffn_pre, v_w_up=v_w_up, v_w_down=v_w_down, v_g_ffn_post=v_g_ffn_post)
    weights = {n: given[n] for n in TWIN_WEIGHTS}
    shared = {n: given[n] for n in SHARED_INPUTS}
    per_example = {n: given[n] for n in ['x', 'mem']}
    grad_fn = _jax.value_and_grad(_loss, argnums=(0, 1))

    def one_microbatch(ex, loss_target):
        ex = dict(ex)
        diff = ex.pop(TWIN_DIFF_INPUT)
        return grad_fn(weights, diff, {**shared, **ex}, loss_target)

    if N_MICROBATCH == 1:
        loss, (grad_w, grad_x) = one_microbatch(per_example, given["loss_target"])
    else:
        def body(carry, xs):
            loss_sum, grad_sum = carry
            l_k, (gw_k, gx_k) = one_microbatch(xs[0], xs[1])
            with _jax.named_scope("update"):
                return (loss_sum + l_k, _jax.tree.map(_jnp.add, grad_sum, gw_k)), gx_k

        init = (_jnp.zeros((), _jnp.float32), _jax.tree.map(_jnp.zeros_like, weights))
        (loss, grad_w), grad_x = _jax.lax.scan(body, init, (per_example, given["loss_target"]))
    with _jax.named_scope("update"):
        delta_w, new_m, new_v = {}, {}, {}
        for n in TWIN_WEIGHTS:
            delta_w[n], new_m[n], new_v[n] = _adamw(weights[n], grad_w[n], given["m_" + n], given["v_" + n])
    return (loss, grad_x, *[grad_w[n] for n in TWIN_WEIGHTS], *[delta_w[n] for n in TWIN_WEIGHTS],
            *[new_m[n] for n in TWIN_WEIGHTS], *[new_v[n] for n in TWIN_WEIGHTS])
```

```python
import math

import jax
import jax.numpy as jnp
from jax import lax
from jax.experimental import pallas as pl
from jax.experimental.pallas import tpu as pltpu

F32 = jnp.float32
BF16 = jnp.bfloat16

D_MODEL = 1024
DEPTH = 4
FOX_WIDTH = 512
FOX_HEADS = 8
FOX_HEAD_DIM = 64
POOL_WIDTH = 512
POOL_WINDOWS = (2, 4, 8, 16)
POOL_GROUP_DIM = 128
POOL_HALO = 16
MEM_LEN = 256
X_HEADS = 4
X_HEAD_DIM = 256
D_FF = 4096
EPS = 1e-6
IN_COLS = 2056
QKV_COLS = 3 * FOX_WIDTH
UF_COLS = 640
INP_COLS = QKV_COLS + UF_COLS
N_DEV = 8
LANES = 128

ADAM_LR = 0.001
ADAM_B1 = 0.9
ADAM_B2 = 0.999
ADAM_EPS = 1e-08
ADAM_WD = 0.01
ADAM_STEP = 10

VMEM_LIMIT = 56 * 1024 * 1024

W_NAMES = ['g_mix_pre', 'w_in', 'b_forget', 'pool_w', 'pool_scale', 'w_out', 'g_mix_post', 'g_x_pre', 'g_mem',
           'wq_x', 'wkv_x', 'wo_x', 'g_x_post', 'g_ffn_pre', 'w_up', 'w_down', 'g_ffn_post']
BIG = ['w_in', 'w_out', 'wq_x', 'wkv_x', 'wo_x', 'w_up', 'w_down']
SMALL = [n for n in W_NAMES if n not in BIG]

NN = (((1,), (0,)), ((), ()))
NT = (((1,), (1,)), ((), ()))
TN = (((0,), (0,)), ((), ()))


def _params(*sem):
    return pltpu.CompilerParams(dimension_semantics=sem, vmem_limit_bytes=VMEM_LIMIT)


def _row_tile(s):
    return min(s, 512)


def _matmul(name, a, b, *, mode, grid, a_spec, b_spec, out_shapes, out_specs, acc_shape, epilogue="none",
            extra=(), extra_specs=()):
    nk = grid[2]
    dn = {"nn": NN, "nt": NT, "tn": TN}[mode]
    n_extra = len(extra)
    n_out = len(out_shapes)

    def finish(r, extra_refs, out_refs):
        if epilogue == "none":
            out_refs[0][...] = r.astype(out_refs[0].dtype)
        elif epilogue == "relu2":
            out_refs[0][...] = r.astype(out_refs[0].dtype)
            rp = jnp.maximum(r, 0.0)
            out_refs[1][...] = (rp * rp).astype(out_refs[1].dtype)
        else:
            up = extra_refs[0][...].astype(F32)
            out_refs[0][...] = (r * (2.0 * jnp.maximum(up, 0.0))).astype(out_refs[0].dtype)

    if nk == 1:
        def body(a_ref, b_ref, *rest):
            r = lax.dot_general(a_ref[...], b_ref[...], dn, preferred_element_type=F32)
            finish(r, rest[:n_extra], rest[n_extra:n_extra + n_out])
        scratch = []
    else:
        def body(a_ref, b_ref, *rest):
            acc = rest[-1]
            k = pl.program_id(2)

            @pl.when(k == 0)
            def _():
                acc[...] = jnp.zeros_like(acc)

            acc[...] += lax.dot_general(a_ref[...], b_ref[...], dn, preferred_element_type=F32)

            @pl.when(k == nk - 1)
            def _():
                finish(acc[...], rest[:n_extra], rest[n_extra:n_extra + n_out])
        scratch = [pltpu.VMEM(acc_shape, F32)]

    outs = pl.pallas_call(
        body, name=name, grid=grid, in_specs=[a_spec, b_spec, *extra_specs], out_specs=list(out_specs),
        out_shape=list(out_shapes), scratch_shapes=scratch,
        compiler_params=_params("parallel", "parallel", "arbitrary"),
    )(a, b, *extra)
    return outs


def _mm_nn(name, a, b, out_dtype, tn, tk=None, tm=None):
    M, K = a.shape
    N = b.shape[1]
    tm = tm or _row_tile(M)
    tk = tk or min(K, 1024)
    grid = (M // tm, N // tn, K // tk)
    return _matmul(
        name, a, b, mode="nn", grid=grid,
        a_spec=pl.BlockSpec((tm, tk), lambda i, j, k: (i, k)),
        b_spec=pl.BlockSpec((tk, tn), lambda i, j, k: (k, j)),
        out_shapes=[jax.ShapeDtypeStruct((M, N), out_dtype)],
        out_specs=[pl.BlockSpec((tm, tn), lambda i, j, k: (i, j))],
        acc_shape=(tm, tn))[0]


def _mm_nn_cols(name, a, b3, out_dtype, epilogue="none"):
    M, K = a.shape
    nb, _, n = b3.shape
    tm = _row_tile(M)
    grid = (M // tm, nb, 1)
    n_out = 2 if epilogue == "relu2" else 1
    outs = _matmul(
        name, a, b3, mode="nn", grid=grid, epilogue=epilogue,
        a_spec=pl.BlockSpec((tm, K), lambda i, j, k: (i, 0)),
        b_spec=pl.BlockSpec((None, K, n), lambda i, j, k: (j, 0, 0)),
        out_shapes=[jax.ShapeDtypeStruct((M, nb * n), out_dtype)] * n_out,
        out_specs=[pl.BlockSpec((tm, n), lambda i, j, k: (i, j))] * n_out,
        acc_shape=(tm, n))
    return outs if n_out == 2 else outs[0]


def _mm_nt(name, a, b, out_dtype, tn, tk=None, epilogue="none", extra=None):
    M, K = a.shape
    N = b.shape[0]
    tm = _row_tile(M)
    tk = tk or min(K, 1024)
    grid = (M // tm, N // tn, K // tk)
    extras = () if extra is None else (extra,)
    especs = () if extra is None else (pl.BlockSpec((tm, tn), lambda i, j, k: (i, j)),)
    return _matmul(
        name, a, b, mode="nt", grid=grid, epilogue=epilogue, extra=extras, extra_specs=especs,
        a_spec=pl.BlockSpec((tm, tk), lambda i, j, k: (i, k)),
        b_spec=pl.BlockSpec((tn, tk), lambda i, j, k: (j, k)),
        out_shapes=[jax.ShapeDtypeStruct((M, N), out_dtype)],
        out_specs=[pl.BlockSpec((tm, tn), lambda i, j, k: (i, j))],
        acc_shape=(tm, tn))[0]


def _mm_nt_cols(name, a, b3, out_dtype):
    M = a.shape[0]
    nb, N, n = b3.shape
    tm = _row_tile(M)
    grid = (M // tm, 1, nb)
    return _matmul(
        name, a, b3, mode="nt", grid=grid,
        a_spec=pl.BlockSpec((tm, n), lambda i, j, k: (i, k)),
        b_spec=pl.BlockSpec((None, N, n), lambda i, j, k: (k, 0, 0)),
        out_shapes=[jax.ShapeDtypeStruct((M, N), out_dtype)],
        out_specs=[pl.BlockSpec((tm, N), lambda i, j, k: (i, 0))],
        acc_shape=(tm, N))[0]


def _mm_tn(name, a, b, out_dtype, tm, tn):
    K, M = a.shape
    N = b.shape[1]
    tk = _row_tile(K)
    grid = (M // tm, N // tn, K // tk)
    return _matmul(
        name, a, b, mode="tn", grid=grid,
        a_spec=pl.BlockSpec((tk, tm), lambda i, j, k: (k, i)),
        b_spec=pl.BlockSpec((tk, tn), lambda i, j, k: (k, j)),
        out_shapes=[jax.ShapeDtypeStruct((M, N), out_dtype)],
        out_specs=[pl.BlockSpec((tm, tn), lambda i, j, k: (i, j))],
        acc_shape=(tm, tn))[0]


def _mm_tn_cols(name, a, b, out_dtype, n):
    K, M = a.shape
    N = b.shape[1]
    nb = N // n
    tk = _row_tile(K)
    grid = (1, nb, K // tk)
    return _matmul(
        name, a, b, mode="tn", grid=grid,
        a_spec=pl.BlockSpec((tk, M), lambda i, j, k: (k, 0)),
        b_spec=pl.BlockSpec((tk, n), lambda i, j, k: (k, j)),
        out_shapes=[jax.ShapeDtypeStruct((nb, M, n), out_dtype)],
        out_specs=[pl.BlockSpec((None, M, n), lambda i, j, k: (j, 0, 0))],
        acc_shape=(M, n))[0]


def _norm_fwd(name, x, g):
    S, Dm = x.shape
    ts = _row_tile(S)

    def body(x_ref, g_ref, h_ref):
        xv = x_ref[...]
        r = lax.rsqrt(jnp.mean(xv * xv, axis=-1, keepdims=True) + EPS)
        h_ref[...] = ((xv * r) * g_ref[...]).astype(BF16)

    return pl.pallas_call(
        body, name=name, grid=(S // ts,),
        in_specs=[pl.BlockSpec((ts, Dm), lambda i: (i, 0)), pl.BlockSpec((1, Dm), lambda i: (0, 0))],
        out_specs=pl.BlockSpec((ts, Dm), lambda i: (i, 0)),
        out_shape=jax.ShapeDtypeStruct((S, Dm), BF16), compiler_params=_params("parallel"))(x, g)


def _resid_norm_fwd(name, x, f, g):
    S, Dm = x.shape
    ts = _row_tile(S)

    def body(x_ref, f_ref, g_ref, o_ref):
        fv = f_ref[...]
        r = lax.rsqrt(jnp.mean(fv * fv, axis=-1, keepdims=True) + EPS)
        o_ref[...] = x_ref[...] + (fv * r) * g_ref[...]

    row = pl.BlockSpec((ts, Dm), lambda i: (i, 0))
    return pl.pallas_call(
        body, name=name, grid=(S // ts,), in_specs=[row, row, pl.BlockSpec((1, Dm), lambda i: (0, 0))],
        out_specs=row, out_shape=jax.ShapeDtypeStruct((S, Dm), F32), compiler_params=_params("parallel"))(x, f, g)


def _norm_bwd(name, dout, y, g, resid, out_dtype):
    S, Dm = y.shape
    ts = _row_tile(S)
    has_resid = resid is not None

    def body(*refs):
        if has_resid:
            do_ref, y_ref, g_ref, r_ref, dy_ref, dg_ref = refs
        else:
            do_ref, y_ref, g_ref, dy_ref, dg_ref = refs
        i = pl.program_id(0)
        yv = y_ref[...]
        dov = do_ref[...]
        r = lax.rsqrt(jnp.mean(yv * yv, axis=-1, keepdims=True) + EPS)
        z = dov * g_ref[...]
        yr = yv * r
        dy = r * (z - yr * jnp.mean(yr * z, axis=-1, keepdims=True))
        if has_resid:
            dy = dy + r_ref[...]
        dy_ref[...] = dy.astype(out_dtype)

        @pl.when(i == 0)
        def _():
            dg_ref[...] = jnp.zeros_like(dg_ref)

        dg_ref[...] += jnp.sum(dov * yr, axis=0, keepdims=True)

    row = pl.BlockSpec((ts, Dm), lambda i: (i, 0))
    vec = pl.BlockSpec((1, Dm), lambda i: (0, 0))
    ins = [dout, y, g] + ([resid] if has_resid else [])
    specs = [row, row, vec] + ([row] if has_resid else [])
    return pl.pallas_call(
        body, name=name, grid=(S // ts,), in_specs=specs, out_specs=[row, vec],
        out_shape=[jax.ShapeDtypeStruct((S, Dm), out_dtype), jax.ShapeDtypeStruct((1, Dm), F32)],
        compiler_params=_params("arbitrary"))(*ins)


def _loss_fwd_bwd(y, t):
    S, Dm = y.shape
    ts = _row_tile(S)

    def body(y_ref, t_ref, dy_ref, acc_ref):
        i = pl.program_id(0)
        e = y_ref[...] - t_ref[...]
        dy_ref[...] = e * (1.0 / Dm)

        @pl.when(i == 0)
        def _():
            acc_ref[...] = jnp.zeros_like(acc_ref)

        s = jnp.sum(jnp.sum(e * e, axis=1, keepdims=True), axis=0, keepdims=True)
        acc_ref[...] += s

    row = pl.BlockSpec((ts, Dm), lambda i: (i, 0))
    return pl.pallas_call(
        body, name="loss", grid=(S // ts,), in_specs=[row, row],
        out_specs=[row, pl.BlockSpec((8, LANES), lambda i: (0, 0))],
        out_shape=[jax.ShapeDtypeStruct((S, Dm), F32), jax.ShapeDtypeStruct((8, LANES), F32)],
        compiler_params=_params("arbitrary"))(y, t)


def _log_sigmoid(x):
    return jnp.minimum(x, 0.0) - jnp.log(1.0 + jnp.exp(-jnp.abs(x)))


def _gate_fwd(uf, bpad):
    S = uf.shape[0]
    T = _row_tile(S)

    def body(f_ref, b_ref, c_ref, carry):
        i = pl.program_id(0)

        @pl.when(i == 0)
        def _():
            carry[...] = jnp.zeros_like(carry)

        lf = _log_sigmoid(f_ref[...] + b_ref[...])
        r = lax.broadcasted_iota(jnp.int32, (T, T), 0)
        cidx = lax.broadcasted_iota(jnp.int32, (T, T), 1)
        tri = (cidx <= r).astype(F32)
        c = lax.dot_general(tri, lf, NN, precision=lax.Precision.HIGHEST, preferred_element_type=F32)
        c_ref[...] = c + carry[0:1, :]
        carry[...] = carry[...] + jnp.sum(lf, axis=0, keepdims=True)

    return pl.pallas_call(
        body, name="gate_fwd", grid=(S // T,),
        in_specs=[pl.BlockSpec((T, LANES), lambda i: (i, 4)), pl.BlockSpec((1, LANES), lambda i: (0, 0))],
        out_specs=pl.BlockSpec((T, LANES), lambda i: (i, 0)),
        out_shape=jax.ShapeDtypeStruct((S, LANES), F32),
        scratch_shapes=[pltpu.VMEM((8, LANES), F32)], compiler_params=_params("arbitrary"))(uf, bpad)


def _gate_bwd(dc, uf, bpad):
    S = uf.shape[0]
    T = _row_tile(S)
    nb = S // T

    def body(dc_ref, f_ref, b_ref, df_ref, db_ref, carry):
        i = pl.program_id(0)

        @pl.when(i == 0)
        def _():
            carry[...] = jnp.zeros_like(carry)
            db_ref[...] = jnp.zeros_like(db_ref)

        dcv = dc_ref[...]
        r = lax.broadcasted_iota(jnp.int32, (T, T), 0)
        cidx = lax.broadcasted_iota(jnp.int32, (T, T), 1)
        tri = (cidx >= r).astype(F32)
        dlf = lax.dot_general(tri, dcv, NN, precision=lax.Precision.HIGHEST, preferred_element_type=F32)
        dlf = dlf + carry[0:1, :]
        carry[...] = carry[...] + jnp.sum(dcv, axis=0, keepdims=True)
        fg = f_ref[...] + b_ref[...]
        dfg = dlf / (1.0 + jnp.exp(fg))
        df_ref[...] = dfg.astype(BF16)
        db_ref[...] += jnp.sum(dfg, axis=0, keepdims=True)

    return pl.pallas_call(
        body, name="gate_bwd", grid=(nb,),
        in_specs=[pl.BlockSpec((T, LANES), lambda i: (nb - 1 - i, 0)),
                  pl.BlockSpec((T, LANES), lambda i: (nb - 1 - i, 4)),
                  pl.BlockSpec((1, LANES), lambda i: (0, 0))],
        out_specs=[pl.BlockSpec((T, LANES), lambda i: (nb - 1 - i, 0)), pl.BlockSpec((1, LANES), lambda i: (0, 0))],
        out_shape=[jax.ShapeDtypeStruct((S, LANES), BF16), jax.ShapeDtypeStruct((1, LANES), F32)],
        scratch_shapes=[pltpu.VMEM((8, LANES), F32)], compiler_params=_params("arbitrary"))(dc, uf, bpad)


def _causal_scores(q, k, c_row, qi, ki, t):
    s = lax.dot_general(q, k, NT, preferred_element_type=F32) * (1.0 / math.sqrt(FOX_HEAD_DIM))
    s = s - c_row
    row = qi * t + lax.broadcasted_iota(jnp.int32, (t, t), 0)
    col = ki * t + lax.broadcasted_iota(jnp.int32, (t, t), 1)
    return jnp.where(col <= row, s, -jnp.inf)


def _fox_fwd(qkvh, cT):
    _, H, S, Dh = qkvh.shape
    t = _row_tile(S)
    n = S // t

    def body(q_ref, k_ref, v_ref, c_ref, o_ref, lse_ref, m_s, l_s, acc_s):
        qi = pl.program_id(1)
        ki = pl.program_id(2)

        @pl.when(ki == 0)
        def _():
            m_s[...] = jnp.full_like(m_s, -jnp.inf)
            l_s[...] = jnp.zeros_like(l_s)
            acc_s[...] = jnp.zeros_like(acc_s)

        @pl.when(ki <= qi)
        def _():
            s = _causal_scores(q_ref[...], k_ref[...], c_ref[...], qi, ki, t)
            m_prev = m_s[...]
            m_new = jnp.maximum(m_prev, jnp.max(s, axis=1, keepdims=True))
            alpha = jnp.exp(m_prev - m_new)
            p = jnp.exp(s - m_new)
            l_s[...] = alpha * l_s[...] + jnp.sum(p, axis=1, keepdims=True)
            p_hi = p.astype(BF16)
            p_lo = (p - p_hi.astype(F32)).astype(BF16)
            pv = (lax.dot_general(p_hi, v_ref[...], NN, preferred_element_type=F32)
                  + lax.dot_general(p_lo, v_ref[...], NN, preferred_element_type=F32))
            acc_s[...] = alpha * acc_s[...] + pv
            m_s[...] = m_new

        @pl.when(ki == qi)
        def _():
            o_ref[...] = acc_s[...] / l_s[...]
            lse_ref[...] = jnp.broadcast_to(m_s[...] + jnp.log(l_s[...]), (t, LANES))

    def kv_spec(which):
        return pl.BlockSpec((None, None, t, Dh), lambda h, i, j: (which, h, jnp.minimum(i, j), 0))

    return pl.pallas_call(
        body, name="fox_fwd", grid=(H, n, n),
        in_specs=[pl.BlockSpec((None, None, t, Dh), lambda h, i, j: (0, h, i, 0)), kv_spec(1), kv_spec(2),
                  pl.BlockSpec((None, 1, t), lambda h, i, j: (h, 0, jnp.minimum(i, j)))],
        out_specs=[pl.BlockSpec((None, t, Dh), lambda h, i, j: (h, i, 0)),
                   pl.BlockSpec((None, t, LANES), lambda h, i, j: (h, i, 0))],
        out_shape=[jax.ShapeDtypeStruct((H, S, Dh), F32), jax.ShapeDtypeStruct((H, S, LANES), F32)],
        scratch_shapes=[pltpu.VMEM((t, 1), F32), pltpu.VMEM((t, 1), F32), pltpu.VMEM((t, Dh), F32)],
        compiler_params=_params("parallel", "parallel", "arbitrary"))(qkvh, qkvh, qkvh, cT)


def _fox_bwd(qkvh, cT, o, lse, do):
    _, H, S, Dh = qkvh.shape
    t = _row_tile(S)
    n = S // t
    scale = 1.0 / math.sqrt(FOX_HEAD_DIM)

    def body(q_ref, k_ref, v_ref, c_ref, o_ref, do_ref, lse_ref, dq_ref, dk_ref, dv_ref, dc_ref, dk_s, dv_s, dc_s):
        ki = pl.program_id(1)
        qi = pl.program_id(2)

        @pl.when((ki == 0) & (qi == 0))
        def _():
            dq_ref[...] = jnp.zeros_like(dq_ref)

        @pl.when(qi == ki)
        def _():
            dk_s[...] = jnp.zeros_like(dk_s)
            dv_s[...] = jnp.zeros_like(dv_s)
            dc_s[...] = jnp.zeros_like(dc_s)

        @pl.when(qi >= ki)
        def _():
            q = q_ref[...]
            k = k_ref[...]
            dov = do_ref[...]
            s = _causal_scores(q, k, c_ref[...], qi, ki, t)
            p = jnp.exp(s - jnp.max(lse_ref[...], axis=1, keepdims=True))
            dv_s[...] += lax.dot_general(p.astype(BF16), dov, TN, preferred_element_type=F32)
            dp = lax.dot_general(dov, v_ref[...], NT, preferred_element_type=F32)
            delta = jnp.sum(dov.astype(F32) * o_ref[...], axis=1, keepdims=True)
            ds = p * (dp - delta)
            dc_s[...] += jnp.sum(ds, axis=0, keepdims=True)
            dsb = ds.astype(BF16)
            rows = pl.ds(pl.multiple_of(qi * t, t), t)
            dq_ref[rows, :] += lax.dot_general(dsb, k, NN, preferred_element_type=F32) * scale
            dk_s[...] += lax.dot_general(dsb, q, TN, preferred_element_type=F32) * scale

        @pl.when(qi == n - 1)
        def _():
            dk_ref[...] = dk_s[...].astype(BF16)
            dv_ref[...] = dv_s[...].astype(BF16)
            dc_ref[...] = -dc_s[...]

    def q_side(last):
        return pl.BlockSpec((None, t, last), lambda h, j, i: (h, jnp.maximum(i, j), 0))

    def kv_spec(which):
        return pl.BlockSpec((None, None, t, Dh), lambda h, j, i: (which, h, j, 0))

    return pl.pallas_call(
        body, name="fox_bwd", grid=(H, n, n),
        in_specs=[pl.BlockSpec((None, None, t, Dh), lambda h, j, i: (0, h, jnp.maximum(i, j), 0)),
                  kv_spec(1), kv_spec(2), pl.BlockSpec((None, 1, t), lambda h, j, i: (h, 0, j)),
                  q_side(Dh), q_side(Dh), q_side(LANES)],
        out_specs=[pl.BlockSpec((None, S, Dh), lambda h, j, i: (h, 0, 0)),
                   pl.BlockSpec((None, t, Dh), lambda h, j, i: (h, j, 0)),
                   pl.BlockSpec((None, t, Dh), lambda h, j, i: (h, j, 0)),
                   pl.BlockSpec((None, 1, t), lambda h, j, i: (h, 0, j))],
        out_shape=[jax.ShapeDtypeStruct((H, S, Dh), F32), jax.ShapeDtypeStruct((H, S, Dh), BF16),
                   jax.ShapeDtypeStruct((H, S, Dh), BF16), jax.ShapeDtypeStruct((H, 1, S), F32)],
        scratch_shapes=[pltpu.VMEM((t, Dh), F32), pltpu.VMEM((t, Dh), F32), pltpu.VMEM((1, t), F32)],
        compiler_params=_params("parallel", "arbitrary", "arbitrary"))(qkvh, qkvh, qkvh, cT, o, do, lse)


def _lanes(g):
    return slice(g * POOL_GROUP_DIM, (g + 1) * POOL_GROUP_DIM)


def _window_sum(e, win, back):
    rows = e.shape[0]
    s = e
    sh = 1
    while sh < win:
        s = s + pltpu.roll(s, sh if back else rows - sh, 0)
        sh *= 2
    return s


def _pooled(u_ref, up_ref, i, g, win, T):
    cur = u_ref[:, _lanes(g)]
    tail = jnp.where(i > 0, up_ref[T - POOL_HALO:T, _lanes(g)], 0.0)
    e = jnp.concatenate([tail, cur], axis=0)
    s = _window_sum(e, win, True)
    t_idx = i * T - POOL_HALO + lax.broadcasted_iota(jnp.int32, (T + POOL_HALO, POOL_GROUP_DIM), 0)
    cnt = jnp.clip(t_idx + 1, 1, win).astype(F32)
    return (s / cnt - e)[POOL_HALO:, :]


def _pool_fwd(uf, pw, ps):
    S = uf.shape[0]
    T = _row_tile(S)

    def body(u_ref, up_ref, w_ref, sc_ref, o_ref):
        i = pl.program_id(0)
        for g, win in enumerate(POOL_WINDOWS):
            pb = _pooled(u_ref, up_ref, i, g, win, T).astype(BF16)
            yv = lax.dot_general(pb, w_ref[g], NN, preferred_element_type=F32)
            o_ref[:, _lanes(g)] = (yv * sc_ref[:, _lanes(g)]).astype(BF16)

    return pl.pallas_call(
        body, name="pool_fwd", grid=(S // T,),
        in_specs=[pl.BlockSpec((T, POOL_WIDTH), lambda i: (i, 0)),
                  pl.BlockSpec((T, POOL_WIDTH), lambda i: (jnp.maximum(i - 1, 0), 0)),
                  pl.BlockSpec((4, POOL_GROUP_DIM, POOL_GROUP_DIM), lambda i: (0, 0, 0)),
                  pl.BlockSpec((1, POOL_WIDTH), lambda i: (0, 0))],
        out_specs=pl.BlockSpec((T, POOL_WIDTH), lambda i: (i, 0)),
        out_shape=jax.ShapeDtypeStruct((S, POOL_WIDTH), BF16), compiler_params=_params("parallel"))(uf, uf, pw, ps)


def _pool_bwd(uf, dcat, pw, ps):
    S = uf.shape[0]
    T = _row_tile(S)
    nb = S // T

    def body(u_ref, up_ref, d_ref, dn_ref, w_ref, sc_ref, du_ref, dw_ref, dsc_ref):
        i = pl.program_id(0)

        @pl.when(i == 0)
        def _():
            dw_ref[...] = jnp.zeros_like(dw_ref)
            dsc_ref[...] = jnp.zeros_like(dsc_ref)

        t_idx = i * T + lax.broadcasted_iota(jnp.int32, (T + POOL_HALO, POOL_GROUP_DIM), 0)
        for g, win in enumerate(POOL_WINDOWS):
            pb = _pooled(u_ref, up_ref, i, g, win, T).astype(BF16)
            w = w_ref[g]
            sc = sc_ref[:, _lanes(g)]
            yv = lax.dot_general(pb, w, NN, preferred_element_type=F32)
            dov = d_ref[:, _lanes(g)]
            dsc_ref[:, _lanes(g)] += jnp.sum(dov * yv, axis=0, keepdims=True)
            head = jnp.where(i < nb - 1, dn_ref[0:POOL_HALO, _lanes(g)], 0.0)
            dyb = (jnp.concatenate([dov, head], axis=0) * sc).astype(BF16)
            dw_ref[g] += lax.dot_general(pb, dyb[:T], TN, preferred_element_type=F32)
            dpool = lax.dot_general(dyb, w, NT, preferred_element_type=F32)
            cnt = jnp.minimum(t_idx + 1, win).astype(F32)
            a = _window_sum(dpool / cnt, win, False)
            du_ref[:, _lanes(g)] = (a - dpool)[:T].astype(BF16)

    return pl.pallas_call(
        body, name="pool_bwd", grid=(nb,),
        in_specs=[pl.BlockSpec((T, POOL_WIDTH), lambda i: (i, 0)),
                  pl.BlockSpec((T, POOL_WIDTH), lambda i: (jnp.maximum(i - 1, 0), 0)),
                  pl.BlockSpec((T, POOL_WIDTH), lambda i: (i, 1)),
                  pl.BlockSpec((T, POOL_WIDTH), lambda i: (jnp.minimum(i + 1, nb - 1), 1)),
                  pl.BlockSpec((4, POOL_GROUP_DIM, POOL_GROUP_DIM), lambda i: (0, 0, 0)),
                  pl.BlockSpec((1, POOL_WIDTH), lambda i: (0, 0))],
        out_specs=[pl.BlockSpec((T, POOL_WIDTH), lambda i: (i, 0)),
                   pl.BlockSpec((4, POOL_GROUP_DIM, POOL_GROUP_DIM), lambda i: (0, 0, 0)),
                   pl.BlockSpec((1, POOL_WIDTH), lambda i: (0, 0))],
        out_shape=[jax.ShapeDtypeStruct((S, POOL_WIDTH), BF16),
                   jax.ShapeDtypeStruct((4, POOL_GROUP_DIM, POOL_GROUP_DIM), F32),
                   jax.ShapeDtypeStruct((1, POOL_WIDTH), F32)],
        compiler_params=_params("arbitrary"))(uf, uf, dcat, dcat, pw, ps)


def _xhead(h):
    return slice(h * X_HEAD_DIM, (h + 1) * X_HEAD_DIM)


def _xvhead(h):
    return slice(D_MODEL + h * X_HEAD_DIM, D_MODEL + (h + 1) * X_HEAD_DIM)


def _x_probs(qh, kh):
    s = lax.dot_general(qh, kh, NT, preferred_element_type=F32) * (1.0 / math.sqrt(X_HEAD_DIM))
    e = jnp.exp(s - jnp.max(s, axis=1, keepdims=True))
    return e / jnp.sum(e, axis=1, keepdims=True)


def _xattn_fwd(q, kv):
    S = q.shape[0]
    t = _row_tile(S)

    def body(q_ref, kv_ref, o_ref):
        for h in range(X_HEADS):
            p = _x_probs(q_ref[:, _xhead(h)], kv_ref[:, _xhead(h)])
            o_ref[:, _xhead(h)] = lax.dot_general(p.astype(BF16), kv_ref[:, _xvhead(h)], NN,
                                                  preferred_element_type=F32).astype(BF16)

    return pl.pallas_call(
        body, name="xattn_fwd", grid=(S // t,),
        in_specs=[pl.BlockSpec((t, D_MODEL), lambda i: (i, 0)), pl.BlockSpec((MEM_LEN, 2 * D_MODEL), lambda i: (0, 0))],
        out_specs=pl.BlockSpec((t, D_MODEL), lambda i: (i, 0)),
        out_shape=jax.ShapeDtypeStruct((S, D_MODEL), BF16), compiler_params=_params("parallel"))(q, kv)


def _xattn_bwd(q, kv, do):
    S = q.shape[0]
    t = _row_tile(S)
    scale = 1.0 / math.sqrt(X_HEAD_DIM)

    def body(q_ref, kv_ref, do_ref, dq_ref, dkv_ref):
        i = pl.program_id(0)

        @pl.when(i == 0)
        def _():
            dkv_ref[...] = jnp.zeros_like(dkv_ref)

        for h in range(X_HEADS):
            qh = q_ref[:, _xhead(h)]
            kh = kv_ref[:, _xhead(h)]
            doh = do_ref[:, _xhead(h)]
            p = _x_probs(qh, kh)
            dkv_ref[:, _xvhead(h)] += lax.dot_general(p.astype(BF16), doh, TN, preferred_element_type=F32)
            dp = lax.dot_general(doh, kv_ref[:, _xvhead(h)], NT, preferred_element_type=F32)
            ds = p * (dp - jnp.sum(dp * p, axis=1, keepdims=True))
            dsb = ds.astype(BF16)
            dq_ref[:, _xhead(h)] = (lax.dot_general(dsb, kh, NN, preferred_element_type=F32) * scale).astype(BF16)
            dkv_ref[:, _xhead(h)] += lax.dot_general(dsb, qh, TN, preferred_element_type=F32) * scale

    row = pl.BlockSpec((t, D_MODEL), lambda i: (i, 0))
    full = pl.BlockSpec((MEM_LEN, 2 * D_MODEL), lambda i: (0, 0))
    return pl.pallas_call(
        body, name="xattn_bwd", grid=(S // t,), in_specs=[row, full, row], out_specs=[row, full],
        out_shape=[jax.ShapeDtypeStruct((S, D_MODEL), BF16), jax.ShapeDtypeStruct((MEM_LEN, 2 * D_MODEL), F32)],
        compiler_params=_params("arbitrary"))(q, kv, do)


def _exchange(name, arrs, gather):
    n = len(arrs)
    out_shapes = [jax.ShapeDtypeStruct((N_DEV,) + tuple(a.shape[-2:]), a.dtype) for a in arrs]

    def body(*refs):
        ins, outs = refs[:n], refs[n:2 * n]
        send_sems, recv_sems, local_sems = refs[2 * n:]
        x, y, c = lax.axis_index("x"), lax.axis_index("y"), lax.axis_index("c")
        me = 4 * x + 2 * y + c
        local = []
        for w in range(n):
            src = ins[w] if gather else ins[w].at[me]
            cp = pltpu.make_async_copy(src, outs[w].at[me], local_sems.at[w])
            cp.start()
            local.append(cp)
        remote = []
        for k in range(1, N_DEV):
            px = 1 - x if k & 4 else x
            py = 1 - y if k & 2 else y
            pc = 1 - c if k & 1 else c
            peer = 4 * px + 2 * py + pc
            for w in range(n):
                src = ins[w] if gather else ins[w].at[peer]
                cp = pltpu.make_async_remote_copy(
                    src_ref=src, dst_ref=outs[w].at[me], send_sem=send_sems.at[w, k - 1],
                    recv_sem=recv_sems.at[w, k - 1], device_id=(px, py, pc), device_id_type=pl.DeviceIdType.MESH)
                cp.start()
                remote.append(cp)
        for cp in remote:
            cp.wait()
        for cp in local:
            cp.wait()

    any_spec = pl.BlockSpec(memory_space=pl.ANY)
    return pl.pallas_call(
        body, name=name, in_specs=[any_spec] * n, out_specs=[any_spec] * n, out_shape=out_shapes,
        scratch_shapes=[pltpu.SemaphoreType.DMA((n, N_DEV - 1)), pltpu.SemaphoreType.DMA((n, N_DEV - 1)),
                        pltpu.SemaphoreType.DMA((n,))],
    )(*arrs)


def _adamw_math(w, g, m, v):
    m = ADAM_B1 * m + (1.0 - ADAM_B1) * g
    v = ADAM_B2 * v + (1.0 - ADAM_B2) * (g * g)
    m_hat = m / (1.0 - ADAM_B1 ** ADAM_STEP)
    v_hat = v / (1.0 - ADAM_B2 ** ADAM_STEP)
    delta = -ADAM_LR * (m_hat / (jnp.sqrt(v_hat) + ADAM_EPS) + ADAM_WD * w)
    return delta, m, v


def _sum_parts(p_ref):
    g = p_ref[0].astype(F32)
    for s in range(1, N_DEV):
        g = g + p_ref[s].astype(F32)
    return g


def _adamw_big(name, w, m, v, parts, tr):
    L, R, C = w.shape

    def body(w_ref, m_ref, v_ref, *rest):
        p_refs = rest[:L]
        g_ref, d_ref, nm_ref, nv_ref = rest[L:]
        layer = pl.program_id(0)
        for j in range(L):
            @pl.when(layer == j)
            def _(j=j):
                g = _sum_parts(p_refs[j])
                delta, nm, nv = _adamw_math(w_ref[...], g, m_ref[...], v_ref[...])
                g_ref[...] = g
                d_ref[...] = delta
                nm_ref[...] = nm
                nv_ref[...] = nv

    blk = pl.BlockSpec((None, tr, C), lambda l, i: (l, i, 0))

    def part_spec(j):
        return pl.BlockSpec((N_DEV, tr, C), lambda l, i: (0, jnp.where(l == j, i, 0), 0))

    shp = jax.ShapeDtypeStruct((L, R, C), F32)
    return pl.pallas_call(
        body, name=name, grid=(L, R // tr), in_specs=[blk, blk, blk] + [part_spec(j) for j in range(L)],
        out_specs=[blk] * 4, out_shape=[shp] * 4, compiler_params=_params("arbitrary", "arbitrary"))(w, m, v, *parts)


def _adamw_small(w, m, v, parts):
    R, C = w.shape

    def body(w_ref, m_ref, v_ref, p_ref, g_ref, d_ref, nm_ref, nv_ref):
        g = _sum_parts(p_ref)
        delta, nm, nv = _adamw_math(w_ref[...], g, m_ref[...], v_ref[...])
        g_ref[...] = g
        d_ref[...] = delta
        nm_ref[...] = nm
        nv_ref[...] = nv

    shp = jax.ShapeDtypeStruct((R, C), F32)
    return pl.pallas_call(body, name="adamw_small", out_shape=[shp] * 4,
                          compiler_params=pltpu.CompilerParams(vmem_limit_bytes=VMEM_LIMIT))(w, m, v, parts)


def _vec(a):
    return a.reshape(1, -1)


def _unpack_w_in(g):
    full = jnp.transpose(g, (1, 0, 2)).reshape(D_MODEL, IN_COLS)
    qkv = full[:, :QKV_COLS]
    f = full[:, QKV_COLS:QKV_COLS + FOX_HEADS]
    u = full[:, QKV_COLS + FOX_HEADS:]
    uf = jnp.concatenate([u, f, jnp.zeros((D_MODEL, UF_COLS - POOL_WIDTH - FOX_HEADS), g.dtype)], axis=1)
    return qkv, uf, jnp.concatenate([qkv, uf], axis=1)


def _pack_dw_in(dwp):
    qkv = dwp[:, :QKV_COLS]
    u = dwp[:, QKV_COLS:QKV_COLS + POOL_WIDTH]
    f = dwp[:, QKV_COLS + POOL_WIDTH:QKV_COLS + POOL_WIDTH + FOX_HEADS]
    full = jnp.concatenate([qkv, f, u], axis=1)
    return jnp.transpose(full.reshape(D_MODEL, N_DEV, IN_COLS // N_DEV), (1, 0, 2))


def _layer_fwd(x0, mem, sp, W):
    S = x0.shape[0]
    sv = {"x0": x0}
    h1 = _norm_fwd("norm_fwd", x0, sp["g_mix_pre"])
    qkv = _mm_nn("mm_qkv", h1, W["qkv"], BF16, tn=512)
    uf = _mm_nn("mm_uf", h1, W["uf"], F32, tn=UF_COLS)
    c = _gate_fwd(uf, sp["b_forget"])
    cT = jnp.transpose(c[:, :FOX_HEADS]).reshape(FOX_HEADS, 1, S)
    qkvh = jnp.transpose(qkv.reshape(S, 3, FOX_HEADS, FOX_HEAD_DIM), (1, 2, 0, 3))
    o, lse = _fox_fwd(qkvh, cT)
    pool = _pool_fwd(uf, sp["pool_w"], sp["pool_scale"])
    cat = jnp.concatenate([jnp.transpose(o, (1, 0, 2)).reshape(S, FOX_WIDTH).astype(BF16), pool], axis=1)
    mix = _mm_nn("mm_out", cat, W["out"], F32, tn=1024)
    x1 = _resid_norm_fwd("resid_norm", x0, mix, sp["g_mix_post"])
    h2 = _norm_fwd("norm_fwd", x1, sp["g_x_pre"])
    mn = _norm_fwd("norm_mem", mem, sp["g_mem"])
    q2 = _mm_nn("mm_q", h2, W["q"], BF16, tn=1024)
    kv = _mm_nn_cols("mm_kv", mn, W["kv"], BF16)
    o2 = _xattn_fwd(q2, kv)
    xo = _mm_nn("mm_o", o2, W["o"], F32, tn=1024)
    x2 = _resid_norm_fwd("resid_norm", x1, xo, sp["g_x_post"])
    h3 = _norm_fwd("norm_fwd", x2, sp["g_ffn_pre"])
    up, act = _mm_nn_cols("mm_up", h3, W["up"], BF16, epilogue="relu2")
    y = _mm_nn("mm_down", act, W["down"], F32, tn=1024)
    x3 = _resid_norm_fwd("resid_norm", x2, y, sp["g_ffn_post"])
    sv.update(h1=h1, uf=uf, cT=cT, qkvh=qkvh, o=o, lse=lse, cat=cat, mix=mix, x1=x1, h2=h2, mn=mn, q2=q2, kv=kv,
              o2=o2, xo=xo, x2=x2, h3=h3, up=up, act=act, y=y)
    return x3, sv


def _layer_bwd(dx3, mem, sv, sp, W):
    S = dx3.shape[0]
    gs = {}
    gb = {}
    dy, gs["g_ffn_post"] = _norm_bwd("norm_bwd_b", dx3, sv["y"], sp["g_ffn_post"], None, BF16)
    dup = _mm_nt("mm_dup", dy, W["down"], BF16, tn=1024, epilogue="drelu2", extra=sv["up"])
    gb["w_down"] = _mm_tn("mm_dw_down", sv["act"], dy, BF16, tm=1024, tn=1024).reshape(N_DEV, D_FF // N_DEV, D_MODEL)
    gb["w_up"] = _mm_tn_cols("mm_dw_up", sv["h3"], dup, BF16, D_FF // N_DEV)
    dh3 = _mm_nt_cols("mm_dh3", dup, W["up"], F32)
    dx2, gs["g_ffn_pre"] = _norm_bwd("norm_bwd_r", dh3, sv["x2"], sp["g_ffn_pre"], dx3, F32)
    dxo, gs["g_x_post"] = _norm_bwd("norm_bwd_b", dx2, sv["xo"], sp["g_x_post"], None, BF16)
    do2 = _mm_nt("mm_do2", dxo, W["o"], BF16, tn=1024)
    gb["wo_x"] = _mm_tn("mm_dw_sq", sv["o2"], dxo, BF16, tm=1024, tn=1024).reshape(N_DEV, D_MODEL // N_DEV, D_MODEL)
    dq2, dkv = _xattn_bwd(sv["q2"], sv["kv"], do2)
    dkvb = dkv.astype(BF16)
    gb["wq_x"] = _mm_tn("mm_dw_sq", sv["h2"], dq2, BF16, tm=1024, tn=1024).reshape(N_DEV, D_MODEL // N_DEV, D_MODEL)
    dh2 = _mm_nt("mm_dh2", dq2, W["q"], F32, tn=1024)
    gb["wkv_x"] = _mm_tn_cols("mm_dw_kv", sv["mn"], dkvb, BF16, 2 * D_MODEL // N_DEV)
    dmn = _mm_nt_cols("mm_dmn", dkvb, W["kv"], F32)
    _, gs["g_mem"] = _norm_bwd("norm_bwd_mem", dmn, mem, sp["g_mem"], None, BF16)
    dx1, gs["g_x_pre"] = _norm_bwd("norm_bwd_r", dh2, sv["x1"], sp["g_x_pre"], dx2, F32)
    dmix, gs["g_mix_post"] = _norm_bwd("norm_bwd_b", dx1, sv["mix"], sp["g_mix_post"], None, BF16)
    dcat = _mm_nt("mm_dcat", dmix, W["out"], F32, tn=1024)
    gb["w_out"] = _mm_tn("mm_dw_sq", sv["cat"], dmix, BF16, tm=1024, tn=1024).reshape(N_DEV, D_MODEL // N_DEV, D_MODEL)
    du, gs["pool_w"], gs["pool_scale"] = _pool_bwd(sv["uf"], dcat, sp["pool_w"], sp["pool_scale"])
    doh = jnp.transpose(dcat[:, :FOX_WIDTH].astype(BF16).reshape(S, FOX_HEADS, FOX_HEAD_DIM), (1, 0, 2))
    dq, dk, dv, dcT = _fox_bwd(sv["qkvh"], sv["cT"], sv["o"], sv["lse"], doh)
    dc = jnp.pad(jnp.transpose(dcT.reshape(FOX_HEADS, S)), ((0, 0), (0, LANES - FOX_HEADS)))
    dfg, db = _gate_bwd(dc, sv["uf"], sp["b_forget"])
    gs["b_forget"] = db[:, :FOX_HEADS]
    dqkv = jnp.transpose(jnp.stack([dq.astype(BF16), dk, dv]), (2, 0, 1, 3)).reshape(S, QKV_COLS)
    dproj = jnp.concatenate([dqkv, du, dfg], axis=1)
    dwp = _mm_tn("mm_dw_in", sv["h1"], dproj, BF16, tm=512, tn=INP_COLS)
    gb["w_in"] = _pack_dw_in(dwp)
    dh1 = _mm_nt("mm_dh1", dproj, W["inp"], F32, tn=1024, tk=INP_COLS)
    dx0, gs["g_mix_pre"] = _norm_bwd("norm_bwd_r", dh1, sv["x0"], sp["g_mix_pre"], dx1, F32)
    return dx0, gb, gs


SMALL_ROWS = 2392


def _pack_small(d):
    flat = jnp.concatenate([d[n].reshape(-1) for n in SMALL])
    return jnp.pad(flat, (0, SMALL_ROWS * LANES - flat.shape[0])).reshape(SMALL_ROWS, LANES)


def _unpack_small(packed, like):
    flat = packed.reshape(-1)
    out = {}
    off = 0
    for n in SMALL:
        size = math.prod(like[n].shape)
        out[n] = flat[off:off + size].reshape(like[n].shape)
        off += size
    return out


def kernel(x, mem, g_mix_pre, w_in, b_forget, pool_w, pool_scale, w_out, g_mix_post, g_x_pre, g_mem, wq_x, wkv_x, wo_x, g_x_post, g_ffn_pre, w_up, w_down, g_ffn_post, loss_target, m_g_mix_pre, m_w_in, m_b_forget, m_pool_w, m_pool_scale, m_w_out, m_g_mix_post, m_g_x_pre, m_g_mem, m_wq_x, m_wkv_x, m_wo_x, m_g_x_post, m_g_ffn_pre, m_w_up, m_w_down, m_g_ffn_post, v_g_mix_pre, v_w_in, v_b_forget, v_pool_w, v_pool_scale, v_w_out, v_g_mix_post, v_g_x_pre, v_g_mem, v_wq_x, v_wkv_x, v_wo_x, v_g_x_post, v_g_ffn_pre, v_w_up, v_w_down, v_g_ffn_post):
    w = dict(g_mix_pre=g_mix_pre, w_in=w_in, b_forget=b_forget, pool_w=pool_w, pool_scale=pool_scale, w_out=w_out,
             g_mix_post=g_mix_post, g_x_pre=g_x_pre, g_mem=g_mem, wq_x=wq_x, wkv_x=wkv_x, wo_x=wo_x,
             g_x_post=g_x_post, g_ffn_pre=g_ffn_pre, w_up=w_up, w_down=w_down, g_ffn_post=g_ffn_post)
    mom = dict(g_mix_pre=m_g_mix_pre, w_in=m_w_in, b_forget=m_b_forget, pool_w=m_pool_w, pool_scale=m_pool_scale,
               w_out=m_w_out, g_mix_post=m_g_mix_post, g_x_pre=m_g_x_pre, g_mem=m_g_mem, wq_x=m_wq_x,
               wkv_x=m_wkv_x, wo_x=m_wo_x, g_x_post=m_g_x_post, g_ffn_pre=m_g_ffn_pre, w_up=m_w_up,
               w_down=m_w_down, g_ffn_post=m_g_ffn_post)
    var = dict(g_mix_pre=v_g_mix_pre, w_in=v_w_in, b_forget=v_b_forget, pool_w=v_pool_w, pool_scale=v_pool_scale,
               w_out=v_w_out, g_mix_post=v_g_mix_post, g_x_pre=v_g_x_pre, g_mem=v_g_mem, wq_x=v_wq_x,
               wkv_x=v_wkv_x, wo_x=v_wo_x, g_x_post=v_g_x_post, g_ffn_pre=v_g_ffn_pre, w_up=v_w_up,
               w_down=v_w_down, g_ffn_post=v_g_ffn_post)
    S = x.shape[1]
    xs = x.reshape(S, D_MODEL)
    mems = mem.reshape(MEM_LEN, D_MODEL)
    target = loss_target.reshape(S, D_MODEL)

    def small_params(l):
        return dict(
            g_mix_pre=_vec(g_mix_pre[l]), g_mix_post=_vec(g_mix_post[l]), g_x_pre=_vec(g_x_pre[l]),
            g_mem=_vec(g_mem[l]), g_x_post=_vec(g_x_post[l]), g_ffn_pre=_vec(g_ffn_pre[l]),
            g_ffn_post=_vec(g_ffn_post[l]), pool_scale=_vec(pool_scale[l]), pool_w=pool_w[l].astype(BF16),
            b_forget=jnp.pad(_vec(b_forget[l]), ((0, 0), (0, LANES - FOX_HEADS))))

    weights = []
    for l in range(DEPTH):
        got = _exchange("gather_weights", [w[n][l].astype(BF16) for n in BIG], True)
        g_in, g_out, g_q, g_kv, g_o, g_up, g_down = got
        qkv, uf, inp = _unpack_w_in(g_in)
        weights.append(dict(qkv=qkv, uf=uf, inp=inp, out=g_out.reshape(D_MODEL, D_MODEL),
                            q=g_q.reshape(D_MODEL, D_MODEL), kv=g_kv, o=g_o.reshape(D_MODEL, D_MODEL),
                            up=g_up, down=g_down.reshape(D_FF, D_MODEL)))

    sps = [small_params(l) for l in range(DEPTH)]
    saved = []
    h = xs
    for l in range(DEPTH):
        h, sv = _layer_fwd(h, mems, sps[l], weights[l])
        saved.append(sv)
    dh, sq = _loss_fwd_bwd(h, target)
    loss = lax.psum(0.5 * sq[0, 0] / D_MODEL, ("x", "y", "c"))

    parts = [None] * DEPTH
    small_grads = [None] * DEPTH
    for l in reversed(range(DEPTH)):
        dh, gb, gs = _layer_bwd(dh, mems, saved[l], sps[l], weights[l])
        parts[l] = _exchange("scatter_grads", [gb[n] for n in BIG], False)
        small_grads[l] = gs
    grad_x = dh.reshape(1, S, D_MODEL)

    grads, deltas, new_m, new_v = {}, {}, {}, {}
    rows = dict(w_in=128, w_out=128, wq_x=128, wkv_x=256, wo_x=128, w_up=256, w_down=128)
    for i, n in enumerate(BIG):
        grads[n], deltas[n], new_m[n], new_v[n] = _adamw_big(
            "adamw_" + n, w[n], mom[n], var[n], [parts[l][i] for l in range(DEPTH)], rows[n])

    sg = {n: jnp.stack([small_grads[l][n].reshape(w[n].shape[1:]) for l in range(DEPTH)]) for n in SMALL}
    (sg_parts,) = _exchange("gather_small_grads", [_pack_small(sg)], True)
    outs = _adamw_small(_pack_small(w), _pack_small(mom), _pack_small(var), sg_parts)
    for d, packed in zip((grads, deltas, new_m, new_v), outs):
        d.update(_unpack_small(packed, w))

    return (loss, grad_x, *[grads[n] for n in W_NAMES], *[deltas[n] for n in W_NAMES],
            *[new_m[n] for n in W_NAMES], *[new_v[n] for n in W_NAMES])
```

```python
import math

import jax
import jax.numpy as jnp
from jax import lax
from jax.experimental import pallas as pl
from jax.experimental.pallas import tpu as pltpu

F32 = jnp.float32
BF16 = jnp.bfloat16

D_MODEL = 1024
DEPTH = 4
FOX_WIDTH = 512
FOX_HEADS = 8
FOX_HEAD_DIM = 64
POOL_WIDTH = 512
POOL_WINDOWS = (2, 4, 8, 16)
POOL_GROUP_DIM = 128
POOL_HALO = 16
MEM_LEN = 256
X_HEADS = 4
X_HEAD_DIM = 256
D_FF = 4096
EPS = 1e-6
IN_COLS = 2056
QKV_COLS = 3 * FOX_WIDTH
UF_COLS = 640
INP_COLS = QKV_COLS + UF_COLS
N_DEV = 8
LANES = 128

ADAM_LR = 0.001
ADAM_B1 = 0.9
ADAM_B2 = 0.999
ADAM_EPS = 1e-08
ADAM_WD = 0.01
ADAM_STEP = 10

VMEM_LIMIT = 56 * 1024 * 1024

W_NAMES = ['g_mix_pre', 'w_in', 'b_forget', 'pool_w', 'pool_scale', 'w_out', 'g_mix_post', 'g_x_pre', 'g_mem',
           'wq_x', 'wkv_x', 'wo_x', 'g_x_post', 'g_ffn_pre', 'w_up', 'w_down', 'g_ffn_post']
BIG = ['w_in', 'w_out', 'wq_x', 'wkv_x', 'wo_x', 'w_up', 'w_down']
SMALL = [n for n in W_NAMES if n not in BIG]

NN = (((1,), (0,)), ((), ()))
NT = (((1,), (1,)), ((), ()))
TN = (((0,), (0,)), ((), ()))


def _params(*sem):
    return pltpu.CompilerParams(dimension_semantics=sem, vmem_limit_bytes=VMEM_LIMIT)


def _row_tile(s):
    return min(s, 512)


def _matmul(name, a, b, *, mode, grid, a_spec, b_spec, out_shapes, out_specs, acc_shape, epilogue="none",
            extra=(), extra_specs=()):
    nk = grid[2]
    dn = {"nn": NN, "nt": NT, "tn": TN}[mode]
    n_extra = len(extra)
    n_out = len(out_shapes)

    def finish(r, extra_refs, out_refs):
        if epilogue == "none":
            out_refs[0][...] = r.astype(out_refs[0].dtype)
        elif epilogue == "relu2":
            out_refs[0][...] = r.astype(out_refs[0].dtype)
            rp = jnp.maximum(r, 0.0)
            out_refs[1][...] = (rp * rp).astype(out_refs[1].dtype)
        else:
            up = extra_refs[0][...].astype(F32)
            out_refs[0][...] = (r * (2.0 * jnp.maximum(up, 0.0))).astype(out_refs[0].dtype)

    if nk == 1:
        def body(a_ref, b_ref, *rest):
            r = lax.dot_general(a_ref[...], b_ref[...], dn, preferred_element_type=F32)
            finish(r, rest[:n_extra], rest[n_extra:n_extra + n_out])
        scratch = []
    else:
        def body(a_ref, b_ref, *rest):
            acc = rest[-1]
            k = pl.program_id(2)

            @pl.when(k == 0)
            def _():
                acc[...] = jnp.zeros_like(acc)

            acc[...] += lax.dot_general(a_ref[...], b_ref[...], dn, preferred_element_type=F32)

            @pl.when(k == nk - 1)
            def _():
                finish(acc[...], rest[:n_extra], rest[n_extra:n_extra + n_out])
        scratch = [pltpu.VMEM(acc_shape, F32)]

    outs = pl.pallas_call(
        body, name=name, grid=grid, in_specs=[a_spec, b_spec, *extra_specs], out_specs=list(out_specs),
        out_shape=list(out_shapes), scratch_shapes=scratch,
        compiler_params=_params("parallel", "parallel", "arbitrary"),
    )(a, b, *extra)
    return outs


def _mm_nn(name, a, b, out_dtype, tn, tk=None, tm=None):
    M, K = a.shape
    N = b.shape[1]
    tm = tm or _row_tile(M)
    tk = tk or min(K, 1024)
    grid = (M // tm, N // tn, K // tk)
    return _matmul(
        name, a, b, mode="nn", grid=grid,
        a_spec=pl.BlockSpec((tm, tk), lambda i, j, k: (i, k)),
        b_spec=pl.BlockSpec((tk, tn), lambda i, j, k: (k, j)),
        out_shapes=[jax.ShapeDtypeStruct((M, N), out_dtype)],
        out_specs=[pl.BlockSpec((tm, tn), lambda i, j, k: (i, j))],
        acc_shape=(tm, tn))[0]


def _mm_nn_cols(name, a, b3, out_dtype, epilogue="none"):
    M, K = a.shape
    nb, _, n = b3.shape
    tm = _row_tile(M)
    grid = (M // tm, nb, 1)
    n_out = 2 if epilogue == "relu2" else 1
    outs = _matmul(
        name, a, b3, mode="nn", grid=grid, epilogue=epilogue,
        a_spec=pl.BlockSpec((tm, K), lambda i, j, k: (i, 0)),
        b_spec=pl.BlockSpec((None, K, n), lambda i, j, k: (j, 0, 0)),
        out_shapes=[jax.ShapeDtypeStruct((M, nb * n), out_dtype)] * n_out,
        out_specs=[pl.BlockSpec((tm, n), lambda i, j, k: (i, j))] * n_out,
        acc_shape=(tm, n))
    return outs if n_out == 2 else outs[0]


def _mm_nt(name, a, b, out_dtype, tn, tk=None, epilogue="none", extra=None):
    M, K = a.shape
    N = b.shape[0]
    tm = _row_tile(M)
    tk = tk or min(K, 1024)
    grid = (M // tm, N // tn, K // tk)
    extras = () if extra is None else (extra,)
    especs = () if extra is None else (pl.BlockSpec((tm, tn), lambda i, j, k: (i, j)),)
    return _matmul(
        name, a, b, mode="nt", grid=grid, epilogue=epilogue, extra=extras, extra_specs=especs,
        a_spec=pl.BlockSpec((tm, tk), lambda i, j, k: (i, k)),
        b_spec=pl.BlockSpec((tn, tk), lambda i, j, k: (j, k)),
        out_shapes=[jax.ShapeDtypeStruct((M, N), out_dtype)],
        out_specs=[pl.BlockSpec((tm, tn), lambda i, j, k: (i, j))],
        acc_shape=(tm, tn))[0]


def _mm_nt_cols(name, a, b3, out_dtype):
    M = a.shape[0]
    nb, N, n = b3.shape
    tm = _row_tile(M)
    grid = (M // tm, 1, nb)
    return _matmul(
        name, a, b3, mode="nt", grid=grid,
        a_spec=pl.BlockSpec((tm, n), lambda i, j, k: (i, k)),
        b_spec=pl.BlockSpec((None, N, n), lambda i, j, k: (k, 0, 0)),
        out_shapes=[jax.ShapeDtypeStruct((M, N), out_dtype)],
        out_specs=[pl.BlockSpec((tm, N), lambda i, j, k: (i, 0))],
        acc_shape=(tm, N))[0]


def _mm_tn(name, a, b, out_dtype, tm, tn):
    K, M = a.shape
    N = b.shape[1]
    tk = _row_tile(K)
    grid = (M // tm, N // tn, K // tk)
    return _matmul(
        name, a, b, mode="tn", grid=grid,
        a_spec=pl.BlockSpec((tk, tm), lambda i, j, k: (k, i)),
        b_spec=pl.BlockSpec((tk, tn), lambda i, j, k: (k, j)),
        out_shapes=[jax.ShapeDtypeStruct((M, N), out_dtype)],
        out_specs=[pl.BlockSpec((tm, tn), lambda i, j, k: (i, j))],
        acc_shape=(tm, tn))[0]


def _mm_tn_cols(name, a, b, out_dtype, n):
    K, M = a.shape
    N = b.shape[1]
    nb = N // n
    tk = _row_tile(K)
    grid = (1, nb, K // tk)
    return _matmul(
        name, a, b, mode="tn", grid=grid,
        a_spec=pl.BlockSpec((tk, M), lambda i, j, k: (k, 0)),
        b_spec=pl.BlockSpec((tk, n), lambda i, j, k: (k, j)),
        out_shapes=[jax.ShapeDtypeStruct((nb, M, n), out_dtype)],
        out_specs=[pl.BlockSpec((None, M, n), lambda i, j, k: (j, 0, 0))],
        acc_shape=(M, n))[0]


def _norm_fwd(name, x, g):
    S, Dm = x.shape
    ts = _row_tile(S)

    def body(x_ref, g_ref, h_ref):
        xv = x_ref[...]
        r = lax.rsqrt(jnp.mean(xv * xv, axis=-1, keepdims=True) + EPS)
        h_ref[...] = ((xv * r) * g_ref[...]).astype(BF16)

    return pl.pallas_call(
        body, name=name, grid=(S // ts,),
        in_specs=[pl.BlockSpec((ts, Dm), lambda i: (i, 0)), pl.BlockSpec((1, Dm), lambda i: (0, 0))],
        out_specs=pl.BlockSpec((ts, Dm), lambda i: (i, 0)),
        out_shape=jax.ShapeDtypeStruct((S, Dm), BF16), compiler_params=_params("parallel"))(x, g)


def _resid_norm_fwd(name, x, f, g):
    S, Dm = x.shape
    ts = _row_tile(S)

    def body(x_ref, f_ref, g_ref, o_ref):
        fv = f_ref[...]
        r = lax.rsqrt(jnp.mean(fv * fv, axis=-1, keepdims=True) + EPS)
        o_ref[...] = x_ref[...] + (fv * r) * g_ref[...]

    row = pl.BlockSpec((ts, Dm), lambda i: (i, 0))
    return pl.pallas_call(
        body, name=name, grid=(S // ts,), in_specs=[row, row, pl.BlockSpec((1, Dm), lambda i: (0, 0))],
        out_specs=row, out_shape=jax.ShapeDtypeStruct((S, Dm), F32), compiler_params=_params("parallel"))(x, f, g)


def _norm_bwd(name, dout, y, g, resid, out_dtype):
    S, Dm = y.shape
    ts = _row_tile(S)
    has_resid = resid is not None

    def body(*refs):
        if has_resid:
            do_ref, y_ref, g_ref, r_ref, dy_ref, dg_ref = refs
        else:
            do_ref, y_ref, g_ref, dy_ref, dg_ref = refs
        i = pl.program_id(0)
        yv = y_ref[...]
        dov = do_ref[...]
        r = lax.rsqrt(jnp.mean(yv * yv, axis=-1, keepdims=True) + EPS)
        z = dov * g_ref[...]
        yr = yv * r
        dy = r * (z - yr * jnp.mean(yr * z, axis=-1, keepdims=True))
        if has_resid:
            dy = dy + r_ref[...]
        dy_ref[...] = dy.astype(out_dtype)

        @pl.when(i == 0)
        def _():
            dg_ref[...] = jnp.zeros_like(dg_ref)

        dg_ref[...] += jnp.sum(dov * yr, axis=0, keepdims=True)

    row = pl.BlockSpec((ts, Dm), lambda i: (i, 0))
    vec = pl.BlockSpec((1, Dm), lambda i: (0, 0))
    ins = [dout, y, g] + ([resid] if has_resid else [])
    specs = [row, row, vec] + ([row] if has_resid else [])
    return pl.pallas_call(
        body, name=name, grid=(S // ts,), in_specs=specs, out_specs=[row, vec],
        out_shape=[jax.ShapeDtypeStruct((S, Dm), out_dtype), jax.ShapeDtypeStruct((1, Dm), F32)],
        compiler_params=_params("arbitrary"))(*ins)


def _loss_fwd_bwd(y, t):
    S, Dm = y.shape
    ts = _row_tile(S)

    def body(y_ref, t_ref, dy_ref, acc_ref):
        i = pl.program_id(0)
        e = y_ref[...] - t_ref[...]
        dy_ref[...] = e * (1.0 / Dm)

        @pl.when(i == 0)
        def _():
            acc_ref[...] = jnp.zeros_like(acc_ref)

        s = jnp.sum(jnp.sum(e * e, axis=1, keepdims=True), axis=0, keepdims=True)
        acc_ref[...] += s

    row = pl.BlockSpec((ts, Dm), lambda i: (i, 0))
    return pl.pallas_call(
        body, name="loss", grid=(S // ts,), in_specs=[row, row],
        out_specs=[row, pl.BlockSpec((8, LANES), lambda i: (0, 0))],
        out_shape=[jax.ShapeDtypeStruct((S, Dm), F32), jax.ShapeDtypeStruct((8, LANES), F32)],
        compiler_params=_params("arbitrary"))(y, t)


def _log_sigmoid(x):
    return jnp.minimum(x, 0.0) - jnp.log(1.0 + jnp.exp(-jnp.abs(x)))


def _gate_fwd(uf, bpad):
    S = uf.shape[0]
    T = _row_tile(S)

    def body(f_ref, b_ref, c_ref, carry):
        i = pl.program_id(0)

        @pl.when(i == 0)
        def _():
            carry[...] = jnp.zeros_like(carry)

        lf = _log_sigmoid(f_ref[...] + b_ref[...])
        r = lax.broadcasted_iota(jnp.int32, (T, T), 0)
        cidx = lax.broadcasted_iota(jnp.int32, (T, T), 1)
        tri = (cidx <= r).astype(F32)
        c = lax.dot_general(tri, lf, NN, precision=lax.Precision.HIGHEST, preferred_element_type=F32)
        c_ref[...] = c + carry[0:1, :]
        carry[...] = carry[...] + jnp.sum(lf, axis=0, keepdims=True)

    return pl.pallas_call(
        body, name="gate_fwd", grid=(S // T,),
        in_specs=[pl.BlockSpec((T, LANES), lambda i: (i, 4)), pl.BlockSpec((1, LANES), lambda i: (0, 0))],
        out_specs=pl.BlockSpec((T, LANES), lambda i: (i, 0)),
        out_shape=jax.ShapeDtypeStruct((S, LANES), F32),
        scratch_shapes=[pltpu.VMEM((8, LANES), F32)], compiler_params=_params("arbitrary"))(uf, bpad)


def _gate_bwd(dc, uf, bpad):
    S = uf.shape[0]
    T = _row_tile(S)
    nb = S // T

    def body(dc_ref, f_ref, b_ref, df_ref, db_ref, carry):
        i = pl.program_id(0)

        @pl.when(i == 0)
        def _():
            carry[...] = jnp.zeros_like(carry)
            db_ref[...] = jnp.zeros_like(db_ref)

        dcv = dc_ref[...]
        r = lax.broadcasted_iota(jnp.int32, (T, T), 0)
        cidx = lax.broadcasted_iota(jnp.int32, (T, T), 1)
        tri = (cidx >= r).astype(F32)
        dlf = lax.dot_general(tri, dcv, NN, precision=lax.Precision.HIGHEST, preferred_element_type=F32)
        dlf = dlf + carry[0:1, :]
        carry[...] = carry[...] + jnp.sum(dcv, axis=0, keepdims=True)
        fg = f_ref[...] + b_ref[...]
        dfg = dlf / (1.0 + jnp.exp(fg))
        df_ref[...] = dfg.astype(BF16)
        db_ref[...] += jnp.sum(dfg, axis=0, keepdims=True)

    return pl.pallas_call(
        body, name="gate_bwd", grid=(nb,),
        in_specs=[pl.BlockSpec((T, LANES), lambda i: (nb - 1 - i, 0)),
                  pl.BlockSpec((T, LANES), lambda i: (nb - 1 - i, 4)),
                  pl.BlockSpec((1, LANES), lambda i: (0, 0))],
        out_specs=[pl.BlockSpec((T, LANES), lambda i: (nb - 1 - i, 0)), pl.BlockSpec((1, LANES), lambda i: (0, 0))],
        out_shape=[jax.ShapeDtypeStruct((S, LANES), BF16), jax.ShapeDtypeStruct((1, LANES), F32)],
        scratch_shapes=[pltpu.VMEM((8, LANES), F32)], compiler_params=_params("arbitrary"))(dc, uf, bpad)


def _scores(q, k, c_row, diagonal, t):
    s = lax.dot_general(q, k, NT, preferred_element_type=F32) - c_row
    if diagonal:
        row = lax.broadcasted_iota(jnp.int32, (t, t), 0)
        col = lax.broadcasted_iota(jnp.int32, (t, t), 1)
        s = jnp.where(col <= row, s, -jnp.inf)
    return s


def _fox_fwd(qkvh, cT, comm):
    _, H, S, Dh = qkvh.shape
    t = _row_tile(S)
    n = S // t
    nc = len(comm)
    scale = 1.0 / math.sqrt(FOX_HEAD_DIM)

    def body(q_ref, k_ref, v_ref, c_ref, *rest):
        comm_in = rest[:nc]
        o_ref, lse_ref = rest[nc:nc + 2]
        comm_out = rest[nc + 2:2 * nc + 2]
        m_s, l_s, acc_s = rest[2 * nc + 2:2 * nc + 5]
        sems = rest[2 * nc + 5:]
        h = pl.program_id(0)
        qi = pl.program_id(1)
        ki = pl.program_id(2)

        if nc:
            @pl.when((h == 0) & (qi == 0) & (ki == 0))
            def _():
                for cp in _comm_copies(comm_in, comm_out, *sems):
                    cp.start()

        @pl.when(ki == 0)
        def _():
            m_s[...] = jnp.full_like(m_s, -jnp.inf)
            l_s[...] = jnp.zeros_like(l_s)
            acc_s[...] = jnp.zeros_like(acc_s)

        def step(diagonal):
            s = _scores(q_ref[...] * scale, k_ref[...], c_ref[...], diagonal, t)
            m_prev = m_s[...]
            m_new = jnp.maximum(m_prev, jnp.max(s, axis=1, keepdims=True))
            alpha = jnp.exp(m_prev - m_new)
            p = jnp.exp(s - m_new)
            l_s[...] = alpha * l_s[...] + jnp.sum(p, axis=1, keepdims=True)
            p_hi = p.astype(BF16)
            p_lo = (p - p_hi.astype(F32)).astype(BF16)
            pv = (lax.dot_general(p_hi, v_ref[...], NN, preferred_element_type=F32)
                  + lax.dot_general(p_lo, v_ref[...], NN, preferred_element_type=F32))
            acc_s[...] = alpha * acc_s[...] + pv
            m_s[...] = m_new

        @pl.when(ki < qi)
        def _():
            step(False)

        @pl.when(ki == qi)
        def _():
            step(True)
            o_ref[...] = acc_s[...] / l_s[...]
            lse_ref[...] = jnp.broadcast_to(m_s[...] + jnp.log(l_s[...]), (t, LANES))

        if nc:
            @pl.when((h == H - 1) & (qi == n - 1) & (ki == n - 1))
            def _():
                for cp in _comm_copies(comm_in, comm_out, *sems):
                    cp.wait()

    def kv_spec(which):
        return pl.BlockSpec((None, None, t, Dh), lambda h, i, j: (which, h, jnp.minimum(i, j), 0))

    any_spec = pl.BlockSpec(memory_space=pl.ANY)
    return pl.pallas_call(
        body, name="fox_fwd", grid=(H, n, n),
        in_specs=[pl.BlockSpec((None, None, t, Dh), lambda h, i, j: (0, h, i, 0)), kv_spec(1), kv_spec(2),
                  pl.BlockSpec((None, 1, t), lambda h, i, j: (h, 0, jnp.minimum(i, j)))] + [any_spec] * nc,
        out_specs=[pl.BlockSpec((None, t, Dh), lambda h, i, j: (h, i, 0)),
                   pl.BlockSpec((None, t, LANES), lambda h, i, j: (h, i, 0))] + [any_spec] * nc,
        out_shape=[jax.ShapeDtypeStruct((H, S, Dh), F32), jax.ShapeDtypeStruct((H, S, LANES), F32)]
        + _comm_shapes(comm),
        scratch_shapes=[pltpu.VMEM((t, 1), F32), pltpu.VMEM((t, 1), F32), pltpu.VMEM((t, Dh), F32)]
        + _comm_scratch(nc),
        compiler_params=_params("arbitrary", "arbitrary", "arbitrary"))(qkvh, qkvh, qkvh, cT, *comm)


def _fox_bwd(qkvh, cT, o, lse, do, comm):
    _, H, S, Dh = qkvh.shape
    t = _row_tile(S)
    n = S // t
    nc = len(comm)
    scale = 1.0 / math.sqrt(FOX_HEAD_DIM)

    def body(q_ref, k_ref, v_ref, c_ref, o_ref, do_ref, lse_ref, *rest):
        comm_in = rest[:nc]
        dq_ref, dk_ref, dv_ref, dc_ref = rest[nc:nc + 4]
        comm_out = rest[nc + 4:2 * nc + 4]
        dk_s, dv_s, dc_s = rest[2 * nc + 4:2 * nc + 7]
        sems = rest[2 * nc + 7:]
        h = pl.program_id(0)
        ki = pl.program_id(1)
        qi = pl.program_id(2)

        if nc:
            @pl.when((h == 0) & (qi == 0) & (ki == 0))
            def _():
                for cp in _comm_copies(comm_in, comm_out, *sems):
                    cp.start()

        @pl.when((ki == 0) & (qi == 0))
        def _():
            dq_ref[...] = jnp.zeros_like(dq_ref)

        @pl.when(qi == ki)
        def _():
            dk_s[...] = jnp.zeros_like(dk_s)
            dv_s[...] = jnp.zeros_like(dv_s)
            dc_s[...] = jnp.zeros_like(dc_s)

        def step(diagonal):
            q = q_ref[...]
            k = k_ref[...]
            dov = do_ref[...]
            s = _scores(q * scale, k, c_ref[...], diagonal, t)
            p = jnp.exp(s - jnp.max(lse_ref[...], axis=1, keepdims=True))
            dv_s[...] += lax.dot_general(p.astype(BF16), dov, TN, preferred_element_type=F32)
            dp = lax.dot_general(dov, v_ref[...], NT, preferred_element_type=F32)
            delta = jnp.sum(dov.astype(F32) * o_ref[...], axis=1, keepdims=True)
            ds = p * (dp - delta)
            dc_s[...] += jnp.sum(ds, axis=0, keepdims=True)
            dsb = ds.astype(BF16)
            rows = pl.ds(pl.multiple_of(qi * t, t), t)
            dq_ref[rows, :] += lax.dot_general(dsb, k, NN, preferred_element_type=F32) * scale
            dk_s[...] += lax.dot_general(dsb, q, TN, preferred_element_type=F32) * scale

        @pl.when(qi > ki)
        def _():
            step(False)

        @pl.when(qi == ki)
        def _():
            step(True)

        @pl.when(qi == n - 1)
        def _():
            dk_ref[...] = dk_s[...].astype(BF16)
            dv_ref[...] = dv_s[...].astype(BF16)
            dc_ref[...] = -dc_s[...]

        if nc:
            @pl.when((h == H - 1) & (qi == n - 1) & (ki == n - 1))
            def _():
                for cp in _comm_copies(comm_in, comm_out, *sems):
                    cp.wait()

    def q_side(last):
        return pl.BlockSpec((None, t, last), lambda h, j, i: (h, jnp.maximum(i, j), 0))

    def kv_spec(which):
        return pl.BlockSpec((None, None, t, Dh), lambda h, j, i: (which, h, j, 0))

    any_spec = pl.BlockSpec(memory_space=pl.ANY)
    return pl.pallas_call(
        body, name="fox_bwd", grid=(H, n, n),
        in_specs=[pl.BlockSpec((None, None, t, Dh), lambda h, j, i: (0, h, jnp.maximum(i, j), 0)),
                  kv_spec(1), kv_spec(2), pl.BlockSpec((None, 1, t), lambda h, j, i: (h, 0, j)),
                  q_side(Dh), q_side(Dh), q_side(LANES)] + [any_spec] * nc,
        out_specs=[pl.BlockSpec((None, S, Dh), lambda h, j, i: (h, 0, 0)),
                   pl.BlockSpec((None, t, Dh), lambda h, j, i: (h, j, 0)),
                   pl.BlockSpec((None, t, Dh), lambda h, j, i: (h, j, 0)),
                   pl.BlockSpec((None, 1, t), lambda h, j, i: (h, 0, j))] + [any_spec] * nc,
        out_shape=[jax.ShapeDtypeStruct((H, S, Dh), F32), jax.ShapeDtypeStruct((H, S, Dh), BF16),
                   jax.ShapeDtypeStruct((H, S, Dh), BF16), jax.ShapeDtypeStruct((H, 1, S), F32)] + _comm_shapes(comm),
        scratch_shapes=[pltpu.VMEM((t, Dh), F32), pltpu.VMEM((t, Dh), F32), pltpu.VMEM((1, t), F32)]
        + _comm_scratch(nc),
        compiler_params=_params("arbitrary", "arbitrary", "arbitrary"))(qkvh, qkvh, qkvh, cT, o, do, lse, *comm)


def _lanes(g):
    return slice(g * POOL_GROUP_DIM, (g + 1) * POOL_GROUP_DIM)


def _window_sum(e, win, back):
    rows = e.shape[0]
    s = e
    sh = 1
    while sh < win:
        s = s + pltpu.roll(s, sh if back else rows - sh, 0)
        sh *= 2
    return s


def _pooled(u_ref, up_ref, i, g, win, T):
    cur = u_ref[:, _lanes(g)]
    tail = jnp.where(i > 0, up_ref[T - POOL_HALO:T, _lanes(g)], 0.0)
    e = jnp.concatenate([tail, cur], axis=0)
    s = _window_sum(e, win, True)
    t_idx = i * T - POOL_HALO + lax.broadcasted_iota(jnp.int32, (T + POOL_HALO, POOL_GROUP_DIM), 0)
    cnt = jnp.clip(t_idx + 1, 1, win).astype(F32)
    return (s / cnt - e)[POOL_HALO:, :]


def _pool_fwd(uf, pw, ps):
    S = uf.shape[0]
    T = _row_tile(S)

    def body(u_ref, up_ref, w_ref, sc_ref, o_ref):
        i = pl.program_id(0)
        for g, win in enumerate(POOL_WINDOWS):
            pb = _pooled(u_ref, up_ref, i, g, win, T).astype(BF16)
            yv = lax.dot_general(pb, w_ref[g], NN, preferred_element_type=F32)
            o_ref[:, _lanes(g)] = (yv * sc_ref[:, _lanes(g)]).astype(BF16)

    return pl.pallas_call(
        body, name="pool_fwd", grid=(S // T,),
        in_specs=[pl.BlockSpec((T, POOL_WIDTH), lambda i: (i, 0)),
                  pl.BlockSpec((T, POOL_WIDTH), lambda i: (jnp.maximum(i - 1, 0), 0)),
                  pl.BlockSpec((4, POOL_GROUP_DIM, POOL_GROUP_DIM), lambda i: (0, 0, 0)),
                  pl.BlockSpec((1, POOL_WIDTH), lambda i: (0, 0))],
        out_specs=pl.BlockSpec((T, POOL_WIDTH), lambda i: (i, 0)),
        out_shape=jax.ShapeDtypeStruct((S, POOL_WIDTH), BF16), compiler_params=_params("parallel"))(uf, uf, pw, ps)


def _pool_bwd(uf, dcat, pw, ps):
    S = uf.shape[0]
    T = _row_tile(S)
    nb = S // T

    def body(u_ref, up_ref, d_ref, dn_ref, w_ref, sc_ref, du_ref, dw_ref, dsc_ref):
        i = pl.program_id(0)

        @pl.when(i == 0)
        def _():
            dw_ref[...] = jnp.zeros_like(dw_ref)
            dsc_ref[...] = jnp.zeros_like(dsc_ref)

        t_idx = i * T + lax.broadcasted_iota(jnp.int32, (T + POOL_HALO, POOL_GROUP_DIM), 0)
        for g, win in enumerate(POOL_WINDOWS):
            pb = _pooled(u_ref, up_ref, i, g, win, T).astype(BF16)
            w = w_ref[g]
            sc = sc_ref[:, _lanes(g)]
            yv = lax.dot_general(pb, w, NN, preferred_element_type=F32)
            dov = d_ref[:, _lanes(g)]
            dsc_ref[:, _lanes(g)] += jnp.sum(dov * yv, axis=0, keepdims=True)
            head = jnp.where(i < nb - 1, dn_ref[0:POOL_HALO, _lanes(g)], 0.0)
            dyb = (jnp.concatenate([dov, head], axis=0) * sc).astype(BF16)
            dw_ref[g] += lax.dot_general(pb, dyb[:T], TN, preferred_element_type=F32)
            dpool = lax.dot_general(dyb, w, NT, preferred_element_type=F32)
            cnt = jnp.minimum(t_idx + 1, win).astype(F32)
            a = _window_sum(dpool / cnt, win, False)
            du_ref[:, _lanes(g)] = (a - dpool)[:T].astype(BF16)

    return pl.pallas_call(
        body, name="pool_bwd", grid=(nb,),
        in_specs=[pl.BlockSpec((T, POOL_WIDTH), lambda i: (i, 0)),
                  pl.BlockSpec((T, POOL_WIDTH), lambda i: (jnp.maximum(i - 1, 0), 0)),
                  pl.BlockSpec((T, POOL_WIDTH), lambda i: (i, 1)),
                  pl.BlockSpec((T, POOL_WIDTH), lambda i: (jnp.minimum(i + 1, nb - 1), 1)),
                  pl.BlockSpec((4, POOL_GROUP_DIM, POOL_GROUP_DIM), lambda i: (0, 0, 0)),
                  pl.BlockSpec((1, POOL_WIDTH), lambda i: (0, 0))],
        out_specs=[pl.BlockSpec((T, POOL_WIDTH), lambda i: (i, 0)),
                   pl.BlockSpec((4, POOL_GROUP_DIM, POOL_GROUP_DIM), lambda i: (0, 0, 0)),
                   pl.BlockSpec((1, POOL_WIDTH), lambda i: (0, 0))],
        out_shape=[jax.ShapeDtypeStruct((S, POOL_WIDTH), BF16),
                   jax.ShapeDtypeStruct((4, POOL_GROUP_DIM, POOL_GROUP_DIM), F32),
                   jax.ShapeDtypeStruct((1, POOL_WIDTH), F32)],
        compiler_params=_params("arbitrary"))(uf, uf, dcat, dcat, pw, ps)


def _xhead(h):
    return slice(h * X_HEAD_DIM, (h + 1) * X_HEAD_DIM)


def _xvhead(h):
    return slice(D_MODEL + h * X_HEAD_DIM, D_MODEL + (h + 1) * X_HEAD_DIM)


def _x_probs(qh, kh):
    s = lax.dot_general(qh, kh, NT, preferred_element_type=F32) * (1.0 / math.sqrt(X_HEAD_DIM))
    e = jnp.exp(s - jnp.max(s, axis=1, keepdims=True))
    return e / jnp.sum(e, axis=1, keepdims=True)


def _xattn_fwd(q, kv):
    S = q.shape[0]
    t = _row_tile(S)

    def body(q_ref, kv_ref, o_ref):
        for h in range(X_HEADS):
            p = _x_probs(q_ref[:, _xhead(h)], kv_ref[:, _xhead(h)])
            o_ref[:, _xhead(h)] = lax.dot_general(p.astype(BF16), kv_ref[:, _xvhead(h)], NN,
                                                  preferred_element_type=F32).astype(BF16)

    return pl.pallas_call(
        body, name="xattn_fwd", grid=(S // t,),
        in_specs=[pl.BlockSpec((t, D_MODEL), lambda i: (i, 0)), pl.BlockSpec((MEM_LEN, 2 * D_MODEL), lambda i: (0, 0))],
        out_specs=pl.BlockSpec((t, D_MODEL), lambda i: (i, 0)),
        out_shape=jax.ShapeDtypeStruct((S, D_MODEL), BF16), compiler_params=_params("parallel"))(q, kv)


def _xattn_bwd(q, kv, do):
    S = q.shape[0]
    t = _row_tile(S)
    scale = 1.0 / math.sqrt(X_HEAD_DIM)

    def body(q_ref, kv_ref, do_ref, dq_ref, dkv_ref):
        i = pl.program_id(0)

        @pl.when(i == 0)
        def _():
            dkv_ref[...] = jnp.zeros_like(dkv_ref)

        for h in range(X_HEADS):
            qh = q_ref[:, _xhead(h)]
            kh = kv_ref[:, _xhead(h)]
            doh = do_ref[:, _xhead(h)]
            p = _x_probs(qh, kh)
            dkv_ref[:, _xvhead(h)] += lax.dot_general(p.astype(BF16), doh, TN, preferred_element_type=F32)
            dp = lax.dot_general(doh, kv_ref[:, _xvhead(h)], NT, preferred_element_type=F32)
            ds = p * (dp - jnp.sum(dp * p, axis=1, keepdims=True))
            dsb = ds.astype(BF16)
            dq_ref[:, _xhead(h)] = (lax.dot_general(dsb, kh, NN, preferred_element_type=F32) * scale).astype(BF16)
            dkv_ref[:, _xhead(h)] += lax.dot_general(dsb, qh, TN, preferred_element_type=F32) * scale

    row = pl.BlockSpec((t, D_MODEL), lambda i: (i, 0))
    full = pl.BlockSpec((MEM_LEN, 2 * D_MODEL), lambda i: (0, 0))
    return pl.pallas_call(
        body, name="xattn_bwd", grid=(S // t,), in_specs=[row, full, row], out_specs=[row, full],
        out_shape=[jax.ShapeDtypeStruct((S, D_MODEL), BF16), jax.ShapeDtypeStruct((MEM_LEN, 2 * D_MODEL), F32)],
        compiler_params=_params("arbitrary"))(q, kv, do)


def _comm_shapes(arrs):
    return [jax.ShapeDtypeStruct((N_DEV,) + tuple(a.shape[-2:]), a.dtype) for a in arrs]


def _comm_scratch(n):
    if n == 0:
        return []
    return [pltpu.SemaphoreType.DMA((n, N_DEV - 1)), pltpu.SemaphoreType.DMA((n, N_DEV - 1)),
            pltpu.SemaphoreType.DMA((n,))]


def _comm_copies(ins, outs, send_sems, recv_sems, local_sems):
    x, y, c = lax.axis_index("x"), lax.axis_index("y"), lax.axis_index("c")
    me = 4 * x + 2 * y + c
    copies = []
    for w in range(len(ins)):
        src = ins[w] if len(ins[w].shape) == 2 else ins[w].at[me]
        copies.append(pltpu.make_async_copy(src, outs[w].at[me], local_sems.at[w]))
    for k in range(1, N_DEV):
        px = 1 - x if k & 4 else x
        py = 1 - y if k & 2 else y
        pc = 1 - c if k & 1 else c
        peer = 4 * px + 2 * py + pc
        for w in range(len(ins)):
            src = ins[w] if len(ins[w].shape) == 2 else ins[w].at[peer]
            copies.append(pltpu.make_async_remote_copy(
                src_ref=src, dst_ref=outs[w].at[me], send_sem=send_sems.at[w, k - 1],
                recv_sem=recv_sems.at[w, k - 1], device_id=(px, py, pc), device_id_type=pl.DeviceIdType.MESH))
    return copies


def _exchange(name, arrs):
    n = len(arrs)

    def body(*refs):
        copies = _comm_copies(refs[:n], refs[n:2 * n], *refs[2 * n:])
        for cp in copies:
            cp.start()
        for cp in copies:
            cp.wait()

    any_spec = pl.BlockSpec(memory_space=pl.ANY)
    return pl.pallas_call(
        body, name=name, in_specs=[any_spec] * n, out_specs=[any_spec] * n, out_shape=_comm_shapes(arrs),
        scratch_shapes=_comm_scratch(n))(*arrs)


def _adamw_math(w, g, m, v):
    m = ADAM_B1 * m + (1.0 - ADAM_B1) * g
    v = ADAM_B2 * v + (1.0 - ADAM_B2) * (g * g)
    m_hat = m / (1.0 - ADAM_B1 ** ADAM_STEP)
    v_hat = v / (1.0 - ADAM_B2 ** ADAM_STEP)
    delta = -ADAM_LR * (m_hat / (jnp.sqrt(v_hat) + ADAM_EPS) + ADAM_WD * w)
    return delta, m, v


def _sum_parts(p_ref):
    g = p_ref[0].astype(F32)
    for s in range(1, N_DEV):
        g = g + p_ref[s].astype(F32)
    return g


def _adamw_big(name, w, m, v, parts, tr):
    L, R, C = w.shape

    def body(w_ref, m_ref, v_ref, *rest):
        p_refs = rest[:L]
        g_ref, d_ref, nm_ref, nv_ref = rest[L:]
        layer = pl.program_id(0)
        for j in range(L):
            @pl.when(layer == j)
            def _(j=j):
                g = _sum_parts(p_refs[j])
                delta, nm, nv = _adamw_math(w_ref[...], g, m_ref[...], v_ref[...])
                g_ref[...] = g
                d_ref[...] = delta
                nm_ref[...] = nm
                nv_ref[...] = nv

    blk = pl.BlockSpec((None, tr, C), lambda l, i: (l, i, 0))

    def part_spec(j):
        return pl.BlockSpec((N_DEV, tr, C), lambda l, i: (0, jnp.where(l == j, i, 0), 0))

    shp = jax.ShapeDtypeStruct((L, R, C), F32)
    return pl.pallas_call(
        body, name=name, grid=(L, R // tr), in_specs=[blk, blk, blk] + [part_spec(j) for j in range(L)],
        out_specs=[blk] * 4, out_shape=[shp] * 4, compiler_params=_params("arbitrary", "arbitrary"))(w, m, v, *parts)


def _adamw_small(w, m, v, parts):
    R, C = w.shape

    def body(w_ref, m_ref, v_ref, p_ref, g_ref, d_ref, nm_ref, nv_ref):
        g = _sum_parts(p_ref)
        delta, nm, nv = _adamw_math(w_ref[...], g, m_ref[...], v_ref[...])
        g_ref[...] = g
        d_ref[...] = delta
        nm_ref[...] = nm
        nv_ref[...] = nv

    shp = jax.ShapeDtypeStruct((R, C), F32)
    return pl.pallas_call(body, name="adamw_small", out_shape=[shp] * 4,
                          compiler_params=pltpu.CompilerParams(vmem_limit_bytes=VMEM_LIMIT))(w, m, v, parts)


def _vec(a):
    return a.reshape(1, -1)


def _unpack_w_in(g):
    full = jnp.transpose(g, (1, 0, 2)).reshape(D_MODEL, IN_COLS)
    qkv = full[:, :QKV_COLS]
    f = full[:, QKV_COLS:QKV_COLS + FOX_HEADS]
    u = full[:, QKV_COLS + FOX_HEADS:]
    uf = jnp.concatenate([u, f, jnp.zeros((D_MODEL, UF_COLS - POOL_WIDTH - FOX_HEADS), g.dtype)], axis=1)
    return qkv, uf, jnp.concatenate([qkv, uf], axis=1)


def _pack_dw_in(dwp):
    qkv = dwp[:, :QKV_COLS]
    u = dwp[:, QKV_COLS:QKV_COLS + POOL_WIDTH]
    f = dwp[:, QKV_COLS + POOL_WIDTH:QKV_COLS + POOL_WIDTH + FOX_HEADS]
    full = jnp.concatenate([qkv, f, u], axis=1)
    return jnp.transpose(full.reshape(D_MODEL, N_DEV, IN_COLS // N_DEV), (1, 0, 2))


REST = ['w_out', 'wq_x', 'wkv_x', 'wo_x', 'w_up', 'w_down']


def _layer_fwd(x0, mem, sp, g_in, shards):
    S = x0.shape[0]
    sv = {"x0": x0}
    w_qkv, w_uf, w_inp = _unpack_w_in(g_in)
    h1 = _norm_fwd("norm_fwd", x0, sp["g_mix_pre"])
    qkv = _mm_nn("mm_qkv", h1, w_qkv, BF16, tn=512)
    uf = _mm_nn("mm_uf", h1, w_uf, F32, tn=UF_COLS)
    c = _gate_fwd(uf, sp["b_forget"])
    cT = jnp.transpose(c[:, :FOX_HEADS]).reshape(FOX_HEADS, 1, S)
    qkvh = jnp.transpose(qkv.reshape(S, 3, FOX_HEADS, FOX_HEAD_DIM), (1, 2, 0, 3))
    o, lse, *got = _fox_fwd(qkvh, cT, shards)
    g_out, g_q, g_kv, g_o, g_up, g_down = got[:6]
    W = dict(inp=w_inp, out=g_out.reshape(D_MODEL, D_MODEL), q=g_q.reshape(D_MODEL, D_MODEL), kv=g_kv,
             o=g_o.reshape(D_MODEL, D_MODEL), up=g_up, down=g_down.reshape(D_FF, D_MODEL))
    pool = _pool_fwd(uf, sp["pool_w"], sp["pool_scale"])
    cat = jnp.concatenate([jnp.transpose(o, (1, 0, 2)).reshape(S, FOX_WIDTH).astype(BF16), pool], axis=1)
    mix = _mm_nn("mm_out", cat, W["out"], F32, tn=1024)
    x1 = _resid_norm_fwd("resid_norm", x0, mix, sp["g_mix_post"])
    h2 = _norm_fwd("norm_fwd", x1, sp["g_x_pre"])
    mn = _norm_fwd("norm_mem", mem, sp["g_mem"])
    q2 = _mm_nn("mm_q", h2, W["q"], BF16, tn=1024)
    kv = _mm_nn_cols("mm_kv", mn, W["kv"], BF16)
    o2 = _xattn_fwd(q2, kv)
    xo = _mm_nn("mm_o", o2, W["o"], F32, tn=1024)
    x2 = _resid_norm_fwd("resid_norm", x1, xo, sp["g_x_post"])
    h3 = _norm_fwd("norm_fwd", x2, sp["g_ffn_pre"])
    up, act = _mm_nn_cols("mm_up", h3, W["up"], BF16, epilogue="relu2")
    y = _mm_nn("mm_down", act, W["down"], F32, tn=1024)
    x3 = _resid_norm_fwd("resid_norm", x2, y, sp["g_ffn_post"])
    sv.update(h1=h1, uf=uf, cT=cT, qkvh=qkvh, o=o, lse=lse, cat=cat, mix=mix, x1=x1, h2=h2, mn=mn, q2=q2, kv=kv,
              o2=o2, xo=xo, x2=x2, h3=h3, up=up, act=act, y=y)
    return x3, sv, W, (got[6] if len(got) > 6 else None)


def _layer_bwd(dx3, mem, sv, sp, W, carried):
    S = dx3.shape[0]
    gs = {}
    gb = {}
    dy, gs["g_ffn_post"] = _norm_bwd("norm_bwd_b", dx3, sv["y"], sp["g_ffn_post"], None, BF16)
    dup = _mm_nt("mm_dup", dy, W["down"], BF16, tn=1024, epilogue="drelu2", extra=sv["up"])
    gb["w_down"] = _mm_tn("mm_dw_down", sv["act"], dy, BF16, tm=1024, tn=1024).reshape(N_DEV, D_FF // N_DEV, D_MODEL)
    gb["w_up"] = _mm_tn_cols("mm_dw_up", sv["h3"], dup, BF16, D_FF // N_DEV)
    dh3 = _mm_nt_cols("mm_dh3", dup, W["up"], F32)
    dx2, gs["g_ffn_pre"] = _norm_bwd("norm_bwd_r", dh3, sv["x2"], sp["g_ffn_pre"], dx3, F32)
    dxo, gs["g_x_post"] = _norm_bwd("norm_bwd_b", dx2, sv["xo"], sp["g_x_post"], None, BF16)
    do2 = _mm_nt("mm_do2", dxo, W["o"], BF16, tn=1024)
    gb["wo_x"] = _mm_tn("mm_dw_sq", sv["o2"], dxo, BF16, tm=1024, tn=1024).reshape(N_DEV, D_MODEL // N_DEV, D_MODEL)
    dq2, dkv = _xattn_bwd(sv["q2"], sv["kv"], do2)
    dkvb = dkv.astype(BF16)
    gb["wq_x"] = _mm_tn("mm_dw_sq", sv["h2"], dq2, BF16, tm=1024, tn=1024).reshape(N_DEV, D_MODEL // N_DEV, D_MODEL)
    dh2 = _mm_nt("mm_dh2", dq2, W["q"], F32, tn=1024)
    gb["wkv_x"] = _mm_tn_cols("mm_dw_kv", sv["mn"], dkvb, BF16, 2 * D_MODEL // N_DEV)
    dmn = _mm_nt_cols("mm_dmn", dkvb, W["kv"], F32)
    _, gs["g_mem"] = _norm_bwd("norm_bwd_mem", dmn, mem, sp["g_mem"], None, BF16)
    dx1, gs["g_x_pre"] = _norm_bwd("norm_bwd_r", dh2, sv["x1"], sp["g_x_pre"], dx2, F32)
    dmix, gs["g_mix_post"] = _norm_bwd("norm_bwd_b", dx1, sv["mix"], sp["g_mix_post"], None, BF16)
    dcat = _mm_nt("mm_dcat", dmix, W["out"], F32, tn=1024)
    gb["w_out"] = _mm_tn("mm_dw_sq", sv["cat"], dmix, BF16, tm=1024, tn=1024).reshape(N_DEV, D_MODEL // N_DEV, D_MODEL)
    du, gs["pool_w"], gs["pool_scale"] = _pool_bwd(sv["uf"], dcat, sp["pool_w"], sp["pool_scale"])
    doh = jnp.transpose(dcat[:, :FOX_WIDTH].astype(BF16).reshape(S, FOX_HEADS, FOX_HEAD_DIM), (1, 0, 2))
    dq, dk, dv, dcT, *got = _fox_bwd(sv["qkvh"], sv["cT"], sv["o"], sv["lse"], doh, [gb[n] for n in REST] + carried)
    dc = jnp.pad(jnp.transpose(dcT.reshape(FOX_HEADS, S)), ((0, 0), (0, LANES - FOX_HEADS)))
    dfg, db = _gate_bwd(dc, sv["uf"], sp["b_forget"])
    gs["b_forget"] = db[:, :FOX_HEADS]
    dqkv = jnp.transpose(jnp.stack([dq.astype(BF16), dk, dv]), (2, 0, 1, 3)).reshape(S, QKV_COLS)
    dproj = jnp.concatenate([dqkv, du, dfg], axis=1)
    dwp = _mm_tn("mm_dw_in", sv["h1"], dproj, BF16, tm=512, tn=INP_COLS)
    dh1 = _mm_nt("mm_dh1", dproj, W["inp"], F32, tn=1024, tk=INP_COLS)
    dx0, gs["g_mix_pre"] = _norm_bwd("norm_bwd_r", dh1, sv["x0"], sp["g_mix_pre"], dx1, F32)
    return dx0, dict(zip(REST, got[:6])), got[6:], _pack_dw_in(dwp), gs


SMALL_ROWS = 2392


def _pack_small(d):
    flat = jnp.concatenate([d[n].reshape(-1) for n in SMALL])
    return jnp.pad(flat, (0, SMALL_ROWS * LANES - flat.shape[0])).reshape(SMALL_ROWS, LANES)


def _unpack_small(packed, like):
    flat = packed.reshape(-1)
    out = {}
    off = 0
    for n in SMALL:
        size = math.prod(like[n].shape)
        out[n] = flat[off:off + size].reshape(like[n].shape)
        off += size
    return out


def kernel(x, mem, g_mix_pre, w_in, b_forget, pool_w, pool_scale, w_out, g_mix_post, g_x_pre, g_mem, wq_x, wkv_x, wo_x, g_x_post, g_ffn_pre, w_up, w_down, g_ffn_post, loss_target, m_g_mix_pre, m_w_in, m_b_forget, m_pool_w, m_pool_scale, m_w_out, m_g_mix_post, m_g_x_pre, m_g_mem, m_wq_x, m_wkv_x, m_wo_x, m_g_x_post, m_g_ffn_pre, m_w_up, m_w_down, m_g_ffn_post, v_g_mix_pre, v_w_in, v_b_forget, v_pool_w, v_pool_scale, v_w_out, v_g_mix_post, v_g_x_pre, v_g_mem, v_wq_x, v_wkv_x, v_wo_x, v_g_x_post, v_g_ffn_pre, v_w_up, v_w_down, v_g_ffn_post):
    w = dict(g_mix_pre=g_mix_pre, w_in=w_in, b_forget=b_forget, pool_w=pool_w, pool_scale=pool_scale, w_out=w_out,
             g_mix_post=g_mix_post, g_x_pre=g_x_pre, g_mem=g_mem, wq_x=wq_x, wkv_x=wkv_x, wo_x=wo_x,
             g_x_post=g_x_post, g_ffn_pre=g_ffn_pre, w_up=w_up, w_down=w_down, g_ffn_post=g_ffn_post)
    mom = dict(g_mix_pre=m_g_mix_pre, w_in=m_w_in, b_forget=m_b_forget, pool_w=m_pool_w, pool_scale=m_pool_scale,
               w_out=m_w_out, g_mix_post=m_g_mix_post, g_x_pre=m_g_x_pre, g_mem=m_g_mem, wq_x=m_wq_x,
               wkv_x=m_wkv_x, wo_x=m_wo_x, g_x_post=m_g_x_post, g_ffn_pre=m_g_ffn_pre, w_up=m_w_up,
               w_down=m_w_down, g_ffn_post=m_g_ffn_post)
    var = dict(g_mix_pre=v_g_mix_pre, w_in=v_w_in, b_forget=v_b_forget, pool_w=v_pool_w, pool_scale=v_pool_scale,
               w_out=v_w_out, g_mix_post=v_g_mix_post, g_x_pre=v_g_x_pre, g_mem=v_g_mem, wq_x=v_wq_x,
               wkv_x=v_wkv_x, wo_x=v_wo_x, g_x_post=v_g_x_post, g_ffn_pre=v_g_ffn_pre, w_up=v_w_up,
               w_down=v_w_down, g_ffn_post=v_g_ffn_post)
    S = x.shape[1]
    xs = x.reshape(S, D_MODEL)
    mems = mem.reshape(MEM_LEN, D_MODEL)
    target = loss_target.reshape(S, D_MODEL)

    def small_params(l):
        return dict(
            g_mix_pre=_vec(g_mix_pre[l]), g_mix_post=_vec(g_mix_post[l]), g_x_pre=_vec(g_x_pre[l]),
            g_mem=_vec(g_mem[l]), g_x_post=_vec(g_x_post[l]), g_ffn_pre=_vec(g_ffn_pre[l]),
            g_ffn_post=_vec(g_ffn_post[l]), pool_scale=_vec(pool_scale[l]), pool_w=pool_w[l].astype(BF16),
            b_forget=jnp.pad(_vec(b_forget[l]), ((0, 0), (0, LANES - FOX_HEADS))))

    shard = {n: [w[n][l].astype(BF16) for l in range(DEPTH)] for n in BIG}
    sps = [small_params(l) for l in range(DEPTH)]
    saved, weights = [], []
    h = xs
    (g_in,) = _exchange("gather_w_in", [shard["w_in"][0]])
    for l in range(DEPTH):
        travelling = [shard[n][l] for n in REST] + ([shard["w_in"][l + 1]] if l + 1 < DEPTH else [])
        h, sv, W, g_in = _layer_fwd(h, mems, sps[l], g_in, travelling)
        saved.append(sv)
        weights.append(W)
    dh, sq = _loss_fwd_bwd(h, target)
    loss = lax.psum(0.5 * sq[0, 0] / D_MODEL, ("x", "y", "c"))

    parts = [dict() for _ in range(DEPTH)]
    small_grads = [None] * DEPTH
    carried = []
    for l in reversed(range(DEPTH)):
        dh, got, got_carried, dw_in, gs = _layer_bwd(dh, mems, saved[l], sps[l], weights[l], carried)
        parts[l].update(got)
        if got_carried:
            parts[l + 1]["w_in"] = got_carried[0]
        carried = [dw_in]
        small_grads[l] = gs
    (parts[0]["w_in"],) = _exchange("scatter_dw_in", carried)
    grad_x = dh.reshape(1, S, D_MODEL)

    grads, deltas, new_m, new_v = {}, {}, {}, {}
    rows = dict(w_in=128, w_out=128, wq_x=128, wkv_x=256, wo_x=128, w_up=256, w_down=128)
    for n in BIG:
        grads[n], deltas[n], new_m[n], new_v[n] = _adamw_big(
            "adamw_" + n, w[n], mom[n], var[n], [parts[l][n] for l in range(DEPTH)], rows[n])

    sg = {n: jnp.stack([small_grads[l][n].reshape(w[n].shape[1:]) for l in range(DEPTH)]) for n in SMALL}
    (sg_parts,) = _exchange("gather_small_grads", [_pack_small(sg)])
    outs = _adamw_small(_pack_small(w), _pack_small(mom), _pack_small(var), sg_parts)
    for d, packed in zip((grads, deltas, new_m, new_v), outs):
        d.update(_unpack_small(packed, w))

    return (loss, grad_x, *[grads[n] for n in W_NAMES], *[deltas[n] for n in W_NAMES],
            *[new_m[n] for n in W_NAMES], *[new_v[n] for n in W_NAMES])
```

```python
import math

import jax
import jax.numpy as jnp
from jax import lax
from jax.experimental import pallas as pl
from jax.experimental.pallas import tpu as pltpu

F32 = jnp.float32
BF16 = jnp.bfloat16

D_MODEL = 1024
DEPTH = 4
FOX_WIDTH = 512
FOX_HEADS = 8
FOX_HEAD_DIM = 64
POOL_WIDTH = 512
POOL_WINDOWS = (2, 4, 8, 16)
POOL_GROUP_DIM = 128
POOL_HALO = 16
MEM_LEN = 256
X_HEADS = 4
X_HEAD_DIM = 256
D_FF = 4096
EPS = 1e-6
IN_COLS = 2056
QKV_COLS = 3 * FOX_WIDTH
UF_COLS = 640
INP_COLS = QKV_COLS + UF_COLS
N_DEV = 8
LANES = 128

ADAM_LR = 0.001
ADAM_B1 = 0.9
ADAM_B2 = 0.999
ADAM_EPS = 1e-08
ADAM_WD = 0.01
ADAM_STEP = 10

VMEM_LIMIT = 56 * 1024 * 1024

W_NAMES = ['g_mix_pre', 'w_in', 'b_forget', 'pool_w', 'pool_scale', 'w_out', 'g_mix_post', 'g_x_pre', 'g_mem',
           'wq_x', 'wkv_x', 'wo_x', 'g_x_post', 'g_ffn_pre', 'w_up', 'w_down', 'g_ffn_post']
BIG = ['w_in', 'w_out', 'wq_x', 'wkv_x', 'wo_x', 'w_up', 'w_down']
SMALL = [n for n in W_NAMES if n not in BIG]

NN = (((1,), (0,)), ((), ()))
NT = (((1,), (1,)), ((), ()))
TN = (((0,), (0,)), ((), ()))


def _params(*sem):
    return pltpu.CompilerParams(dimension_semantics=sem, vmem_limit_bytes=VMEM_LIMIT)


def _row_tile(s):
    return min(s, 512)


def _mm_rows(name, terms, outs, extra=None, piece=1024):
    M = terms[0][0].shape[0]
    tm = _row_tile(M)
    nterm = len(terms)
    n_extra = 0 if extra is None else 1
    groups = {}
    for idx, (_, c0, width, fn) in enumerate(outs):
        groups.setdefault((c0, width), []).append((idx, fn))

    def product(a_ref, w_ref, w, kind, c0, pw):
        cols = slice(c0, c0 + pw)
        if kind == "nn":
            return lax.dot_general(a_ref[...], w_ref[:, cols], NN, preferred_element_type=F32)
        if kind == "nt":
            return lax.dot_general(a_ref[...], w_ref[cols, :], NT, preferred_element_type=F32)
        n = w.shape[2]
        if kind == "nn3":
            assert pw == n and c0 % n == 0
            return lax.dot_general(a_ref[...], w_ref[c0 // n], NN, preferred_element_type=F32)
        r = None
        for j in range(w.shape[0]):
            part = lax.dot_general(a_ref[:, j * n:(j + 1) * n], w_ref[j, cols, :], NT, preferred_element_type=F32)
            r = part if r is None else r + part
        return r

    def body(*refs):
        a_refs = refs[0:2 * nterm:2]
        w_refs = refs[1:2 * nterm:2]
        extra_refs = refs[2 * nterm:2 * nterm + n_extra]
        out_refs = refs[2 * nterm + n_extra:]
        for (g0, gw), members in groups.items():
            for c0 in range(g0, g0 + gw, piece):
                pw = min(piece, g0 + gw - c0)
                r = None
                for a_ref, w_ref, (_, w, kind) in zip(a_refs, w_refs, terms):
                    part = product(a_ref, w_ref, w, kind, c0, pw)
                    r = part if r is None else r + part
                dst = slice(c0 - g0, c0 - g0 + pw)
                for idx, fn in members:
                    if fn == "relu2":
                        rp = jnp.maximum(r, 0.0)
                        val = rp * rp
                    elif fn == "drelu2":
                        val = r * (2.0 * jnp.maximum(extra_refs[0][:, dst].astype(F32), 0.0))
                    else:
                        val = r
                    out_refs[idx][:, dst] = val.astype(out_refs[idx].dtype)

    in_specs, ins = [], []
    for a, w, _ in terms:
        in_specs.append(pl.BlockSpec((tm, a.shape[1]), lambda i: (i, 0)))
        in_specs.append(pl.BlockSpec(w.shape, lambda i, nd=w.ndim: (0,) * nd))
        ins += [a, w]
    if extra is not None:
        in_specs.append(pl.BlockSpec((tm, extra.shape[1]), lambda i: (i, 0)))
        ins.append(extra)
    res = pl.pallas_call(
        body, name=name, grid=(M // tm,), in_specs=in_specs,
        out_specs=[pl.BlockSpec((tm, width), lambda i: (i, 0)) for _, _, width, _ in outs],
        out_shape=[jax.ShapeDtypeStruct((M, width), dt) for dt, _, width, _ in outs],
        compiler_params=_params("parallel"))(*ins)
    return res


def _mm1(name, a, w, kind, n_cols, dtype, piece=1024):
    return _mm_rows(name, [(a, w, kind)], [(dtype, 0, n_cols, "id")], piece=piece)[0]


def _matmul_tn(name, a, b, grid, a_spec, b_spec, out_shape, out_spec, acc_shape):
    nk = grid[2]

    def body(a_ref, b_ref, o_ref, acc):
        k = pl.program_id(2)

        @pl.when(k == 0)
        def _():
            acc[...] = jnp.zeros_like(acc)

        acc[...] += lax.dot_general(a_ref[...], b_ref[...], TN, preferred_element_type=F32)

        @pl.when(k == nk - 1)
        def _():
            o_ref[...] = acc[...].astype(o_ref.dtype)

    return pl.pallas_call(
        body, name=name, grid=grid, in_specs=[a_spec, b_spec], out_specs=out_spec, out_shape=out_shape,
        scratch_shapes=[pltpu.VMEM(acc_shape, F32)],
        compiler_params=_params("parallel", "parallel", "arbitrary"))(a, b)


def _mm_tn(name, a, b, out_dtype, tm, tn):
    K, M = a.shape
    N = b.shape[1]
    tk = _row_tile(K)
    return _matmul_tn(
        name, a, b, (M // tm, N // tn, K // tk),
        pl.BlockSpec((tk, tm), lambda i, j, k: (k, i)), pl.BlockSpec((tk, tn), lambda i, j, k: (k, j)),
        jax.ShapeDtypeStruct((M, N), out_dtype), pl.BlockSpec((tm, tn), lambda i, j, k: (i, j)), (tm, tn))


def _mm_tn_cols(name, a, b, out_dtype, n):
    K, M = a.shape
    nb = b.shape[1] // n
    tk = _row_tile(K)
    return _matmul_tn(
        name, a, b, (1, nb, K // tk),
        pl.BlockSpec((tk, M), lambda i, j, k: (k, 0)), pl.BlockSpec((tk, n), lambda i, j, k: (k, j)),
        jax.ShapeDtypeStruct((nb, M, n), out_dtype), pl.BlockSpec((None, M, n), lambda i, j, k: (j, 0, 0)), (M, n))


def _norm_fwd(name, x, g):
    S, Dm = x.shape
    ts = _row_tile(S)

    def body(x_ref, g_ref, h_ref):
        xv = x_ref[...]
        r = lax.rsqrt(jnp.mean(xv * xv, axis=-1, keepdims=True) + EPS)
        h_ref[...] = ((xv * r) * g_ref[...]).astype(BF16)

    return pl.pallas_call(
        body, name=name, grid=(S // ts,),
        in_specs=[pl.BlockSpec((ts, Dm), lambda i: (i, 0)), pl.BlockSpec((1, Dm), lambda i: (0, 0))],
        out_specs=pl.BlockSpec((ts, Dm), lambda i: (i, 0)),
        out_shape=jax.ShapeDtypeStruct((S, Dm), BF16), compiler_params=_params("parallel"))(x, g)


def _resid_norm_fwd(name, x, f, g):
    S, Dm = x.shape
    ts = _row_tile(S)

    def body(x_ref, f_ref, g_ref, o_ref):
        fv = f_ref[...]
        r = lax.rsqrt(jnp.mean(fv * fv, axis=-1, keepdims=True) + EPS)
        o_ref[...] = x_ref[...] + (fv * r) * g_ref[...]

    row = pl.BlockSpec((ts, Dm), lambda i: (i, 0))
    return pl.pallas_call(
        body, name=name, grid=(S // ts,), in_specs=[row, row, pl.BlockSpec((1, Dm), lambda i: (0, 0))],
        out_specs=row, out_shape=jax.ShapeDtypeStruct((S, Dm), F32), compiler_params=_params("parallel"))(x, f, g)


def _norm_bwd(name, dout, y, g, resid, out_dtype):
    S, Dm = y.shape
    ts = _row_tile(S)
    has_resid = resid is not None

    def body(*refs):
        if has_resid:
            do_ref, y_ref, g_ref, r_ref, dy_ref, dg_ref = refs
        else:
            do_ref, y_ref, g_ref, dy_ref, dg_ref = refs
        i = pl.program_id(0)
        yv = y_ref[...]
        dov = do_ref[...]
        r = lax.rsqrt(jnp.mean(yv * yv, axis=-1, keepdims=True) + EPS)
        z = dov * g_ref[...]
        yr = yv * r
        dy = r * (z - yr * jnp.mean(yr * z, axis=-1, keepdims=True))
        if has_resid:
            dy = dy + r_ref[...]
        dy_ref[...] = dy.astype(out_dtype)

        @pl.when(i == 0)
        def _():
            dg_ref[...] = jnp.zeros_like(dg_ref)

        dg_ref[...] += jnp.sum(dov * yr, axis=0, keepdims=True)

    row = pl.BlockSpec((ts, Dm), lambda i: (i, 0))
    vec = pl.BlockSpec((1, Dm), lambda i: (0, 0))
    ins = [dout, y, g] + ([resid] if has_resid else [])
    specs = [row, row, vec] + ([row] if has_resid else [])
    return pl.pallas_call(
        body, name=name, grid=(S // ts,), in_specs=specs, out_specs=[row, vec],
        out_shape=[jax.ShapeDtypeStruct((S, Dm), out_dtype), jax.ShapeDtypeStruct((1, Dm), F32)],
        compiler_params=_params("arbitrary"))(*ins)


def _loss_fwd_bwd(y, t):
    S, Dm = y.shape
    ts = _row_tile(S)

    def body(y_ref, t_ref, dy_ref, acc_ref):
        i = pl.program_id(0)
        e = y_ref[...] - t_ref[...]
        dy_ref[...] = e * (1.0 / Dm)

        @pl.when(i == 0)
        def _():
            acc_ref[...] = jnp.zeros_like(acc_ref)

        s = jnp.sum(jnp.sum(e * e, axis=1, keepdims=True), axis=0, keepdims=True)
        acc_ref[...] += s

    row = pl.BlockSpec((ts, Dm), lambda i: (i, 0))
    return pl.pallas_call(
        body, name="loss", grid=(S // ts,), in_specs=[row, row],
        out_specs=[row, pl.BlockSpec((8, LANES), lambda i: (0, 0))],
        out_shape=[jax.ShapeDtypeStruct((S, Dm), F32), jax.ShapeDtypeStruct((8, LANES), F32)],
        compiler_params=_params("arbitrary"))(y, t)


def _log_sigmoid(x):
    return jnp.minimum(x, 0.0) - jnp.log(1.0 + jnp.exp(-jnp.abs(x)))


def _gate_fwd(uf, bpad):
    S = uf.shape[0]
    T = _row_tile(S)

    def body(f_ref, b_ref, c_ref, carry):
        i = pl.program_id(0)

        @pl.when(i == 0)
        def _():
            carry[...] = jnp.zeros_like(carry)

        lf = _log_sigmoid(f_ref[...] + b_ref[...])
        r = lax.broadcasted_iota(jnp.int32, (T, T), 0)
        cidx = lax.broadcasted_iota(jnp.int32, (T, T), 1)
        tri = (cidx <= r).astype(F32)
        c = lax.dot_general(tri, lf, NN, precision=lax.Precision.HIGHEST, preferred_element_type=F32)
        c_ref[...] = c + carry[0:1, :]
        carry[...] = carry[...] + jnp.sum(lf, axis=0, keepdims=True)

    return pl.pallas_call(
        body, name="gate_fwd", grid=(S // T,),
        in_specs=[pl.BlockSpec((T, LANES), lambda i: (i, 4)), pl.BlockSpec((1, LANES), lambda i: (0, 0))],
        out_specs=pl.BlockSpec((T, LANES), lambda i: (i, 0)),
        out_shape=jax.ShapeDtypeStruct((S, LANES), F32),
        scratch_shapes=[pltpu.VMEM((8, LANES), F32)], compiler_params=_params("arbitrary"))(uf, bpad)


def _gate_bwd(dc, uf, bpad):
    S = uf.shape[0]
    T = _row_tile(S)
    nb = S // T

    def body(dc_ref, f_ref, b_ref, df_ref, db_ref, carry):
        i = pl.program_id(0)

        @pl.when(i == 0)
        def _():
            carry[...] = jnp.zeros_like(carry)
            db_ref[...] = jnp.zeros_like(db_ref)

        dcv = dc_ref[...]
        r = lax.broadcasted_iota(jnp.int32, (T, T), 0)
        cidx = lax.broadcasted_iota(jnp.int32, (T, T), 1)
        tri = (cidx >= r).astype(F32)
        dlf = lax.dot_general(tri, dcv, NN, precision=lax.Precision.HIGHEST, preferred_element_type=F32)
        dlf = dlf + carry[0:1, :]
        carry[...] = carry[...] + jnp.sum(dcv, axis=0, keepdims=True)
        fg = f_ref[...] + b_ref[...]
        dfg = dlf / (1.0 + jnp.exp(fg))
        df_ref[...] = dfg.astype(BF16)
        db_ref[...] += jnp.sum(dfg, axis=0, keepdims=True)

    return pl.pallas_call(
        body, name="gate_bwd", grid=(nb,),
        in_specs=[pl.BlockSpec((T, LANES), lambda i: (nb - 1 - i, 0)),
                  pl.BlockSpec((T, LANES), lambda i: (nb - 1 - i, 4)),
                  pl.BlockSpec((1, LANES), lambda i: (0, 0))],
        out_specs=[pl.BlockSpec((T, LANES), lambda i: (nb - 1 - i, 0)), pl.BlockSpec((1, LANES), lambda i: (0, 0))],
        out_shape=[jax.ShapeDtypeStruct((S, LANES), BF16), jax.ShapeDtypeStruct((1, LANES), F32)],
        scratch_shapes=[pltpu.VMEM((8, LANES), F32)], compiler_params=_params("arbitrary"))(dc, uf, bpad)


FOX_CHUNK = 64
HEAD_PAIRS = FOX_HEADS // 2
PAIR = 2


def _masked(s, row0, diagonal):
    if diagonal:
        row = row0 + lax.broadcasted_iota(jnp.int32, s.shape, 0)
        col = lax.broadcasted_iota(jnp.int32, s.shape, 1)
        s = jnp.where(col <= row, s, -jnp.inf)
    return s


def _head_lanes(hh):
    lane = lax.broadcasted_iota(jnp.int32, (1, LANES), 1)
    return (lane < FOX_HEAD_DIM) if hh == 0 else (lane >= FOX_HEAD_DIM)


def _pick(first_head, a, b):
    return jnp.where(first_head, a, b)


def _fox_fwd(qkv, cT, comm):
    S = qkv.shape[0]
    t = _row_tile(S)
    n = S // t
    nc = len(comm)
    scale = 1.0 / math.sqrt(FOX_HEAD_DIM)
    chunk = min(FOX_CHUNK, t)
    per_head = 7

    def body(q_ref, k_ref, v_ref, c_ref, *rest):
        comm_in = rest[:nc]
        o_ref, ob_ref, lse_ref = rest[nc:nc + 3]
        comm_out = rest[nc + 3:2 * nc + 3]
        scr = rest[2 * nc + 3:2 * nc + 3 + PAIR * per_head]
        sems = rest[2 * nc + 3 + PAIR * per_head:]
        hp = pl.program_id(0)
        qi = pl.program_id(1)
        ki = pl.program_id(2)

        if nc:
            @pl.when((hp == 0) & (qi == 0) & (ki == 0))
            def _():
                for cp in _comm_copies(comm_in, comm_out, *sems):
                    cp.start()

        @pl.when(ki == 0)
        def _():
            for hh in range(PAIR):
                m_s, l_s, a_s, acc_s = scr[hh * per_head:hh * per_head + 4]
                m_s[...] = jnp.full_like(m_s, -jnp.inf)
                l_s[...] = jnp.zeros_like(l_s)
                acc_s[...] = jnp.zeros_like(acc_s)

        def step(diagonal):
            q2 = q_ref[...] * scale
            k2 = k_ref[...]
            v2 = v_ref[...]
            for hh in range(PAIR):
                m_s, l_s, a_s, acc_s, s_s, ph_s, pl_s = scr[hh * per_head:(hh + 1) * per_head]
                qm = jnp.where(_head_lanes(hh), q2, jnp.zeros_like(q2))
                s_s[...] = lax.dot_general(qm, k2, NT, preferred_element_type=F32)
                c_row = c_ref[hh]
                for r in range(t // chunk):
                    rows = slice(r * chunk, (r + 1) * chunk)
                    s = _masked(s_s[rows, :] - c_row, r * chunk, diagonal)
                    m_prev = m_s[rows, :]
                    m_new = jnp.maximum(m_prev, jnp.max(s, axis=1, keepdims=True))
                    alpha = jnp.exp(m_prev - m_new)
                    p = jnp.exp(s - m_new)
                    l_s[rows, :] = alpha * l_s[rows, :] + jnp.sum(p, axis=1, keepdims=True)
                    m_s[rows, :] = m_new
                    a_s[rows, :] = alpha
                    p_hi = p.astype(BF16)
                    ph_s[rows, :] = p_hi
                    pl_s[rows, :] = (p - p_hi.astype(F32)).astype(BF16)
                pv = (lax.dot_general(ph_s[...], v2, NN, preferred_element_type=F32)
                      + lax.dot_general(pl_s[...], v2, NN, preferred_element_type=F32))
                acc_s[...] = a_s[...] * acc_s[...] + pv

        @pl.when(ki < qi)
        def _():
            step(False)

        @pl.when(ki == qi)
        def _():
            step(True)
            heads = []
            for hh in range(PAIR):
                m_s, l_s, a_s, acc_s = scr[hh * per_head:hh * per_head + 4]
                heads.append(acc_s[...] / l_s[...])
                lse_ref[hh] = jnp.broadcast_to(m_s[...] + jnp.log(l_s[...]), (t, LANES))
            o2 = _pick(_head_lanes(0), heads[0], heads[1])
            o_ref[...] = o2
            ob_ref[...] = o2.astype(BF16)

        if nc:
            @pl.when((hp == HEAD_PAIRS - 1) & (qi == n - 1) & (ki == n - 1))
            def _():
                for cp in _comm_copies(comm_in, comm_out, *sems):
                    cp.wait()

    def col_spec(first_block, causal):
        if causal:
            return pl.BlockSpec((t, LANES), lambda h, i, j: (jnp.minimum(i, j), first_block + h))
        return pl.BlockSpec((t, LANES), lambda h, i, j: (i, first_block + h))

    any_spec = pl.BlockSpec(memory_space=pl.ANY)
    head_scratch = [pltpu.VMEM((t, 1), F32), pltpu.VMEM((t, 1), F32), pltpu.VMEM((t, 1), F32),
                    pltpu.VMEM((t, LANES), F32), pltpu.VMEM((t, t), F32), pltpu.VMEM((t, t), BF16),
                    pltpu.VMEM((t, t), BF16)]
    return pl.pallas_call(
        body, name="fox_fwd", grid=(HEAD_PAIRS, n, n),
        in_specs=[col_spec(0, False), col_spec(HEAD_PAIRS, True), col_spec(2 * HEAD_PAIRS, True),
                  pl.BlockSpec((PAIR, 1, t), lambda h, i, j: (h, 0, jnp.minimum(i, j)))] + [any_spec] * nc,
        out_specs=[col_spec(0, False), col_spec(0, False),
                   pl.BlockSpec((PAIR, t, LANES), lambda h, i, j: (h, i, 0))] + [any_spec] * nc,
        out_shape=[jax.ShapeDtypeStruct((S, FOX_WIDTH), F32), jax.ShapeDtypeStruct((S, FOX_WIDTH), BF16),
                   jax.ShapeDtypeStruct((FOX_HEADS, S, LANES), F32)] + _comm_shapes(comm),
        scratch_shapes=head_scratch * PAIR + _comm_scratch(nc),
        compiler_params=_params("arbitrary", "arbitrary", "arbitrary"))(qkv, qkv, qkv, cT, *comm)


def _fox_bwd(qkv, cT, o, lse, do, comm):
    S = qkv.shape[0]
    t = _row_tile(S)
    n = S // t
    nc = len(comm)
    scale = 1.0 / math.sqrt(FOX_HEAD_DIM)
    chunk = min(FOX_CHUNK, t)
    per_head = 7

    def body(q_ref, k_ref, v_ref, c_ref, o_ref, do_ref, lse_ref, *rest):
        comm_in = rest[:nc]
        dq_ref, dk_ref, dv_ref, dc_ref = rest[nc:nc + 4]
        comm_out = rest[nc + 4:2 * nc + 4]
        dq_s, dk_s, dv_s = rest[2 * nc + 4:2 * nc + 7]
        scr = rest[2 * nc + 7:2 * nc + 7 + PAIR * per_head]
        sems = rest[2 * nc + 7 + PAIR * per_head:]
        hp = pl.program_id(0)
        ki = pl.program_id(1)
        qi = pl.program_id(2)

        if nc:
            @pl.when((hp == 0) & (qi == 0) & (ki == 0))
            def _():
                for cp in _comm_copies(comm_in, comm_out, *sems):
                    cp.start()

        @pl.when((ki == 0) & (qi == 0))
        def _():
            dq_s[...] = jnp.zeros_like(dq_s)

        @pl.when(qi == ki)
        def _():
            dk_s[...] = jnp.zeros_like(dk_s)
            dv_s[...] = jnp.zeros_like(dv_s)
            for hh in range(PAIR):
                dc_s = scr[hh * per_head]
                dc_s[...] = jnp.zeros_like(dc_s)

        def step(diagonal):
            q2 = q_ref[...]
            k2 = k_ref[...]
            v2 = v_ref[...]
            do2 = do_ref[...]
            prod = do2.astype(F32) * o_ref[...]
            grads = []
            for hh in range(PAIR):
                dc_s, lse_s, delta_s, s_s, dp_s, p_s, ds_s = scr[hh * per_head:(hh + 1) * per_head]
                mine = _head_lanes(hh)
                s_s[...] = lax.dot_general(jnp.where(mine, q2 * scale, jnp.zeros_like(q2)), k2, NT,
                                           preferred_element_type=F32)
                dp_s[...] = lax.dot_general(jnp.where(mine, do2, jnp.zeros_like(do2)), v2, NT,
                                            preferred_element_type=F32)
                lse_s[...] = jnp.max(lse_ref[hh], axis=1, keepdims=True)
                delta_s[...] = jnp.sum(jnp.where(mine, prod, 0.0), axis=1, keepdims=True)
                c_row = c_ref[hh]
                dc8 = jnp.zeros((8, t), F32)
                for r in range(t // chunk):
                    rows = slice(r * chunk, (r + 1) * chunk)
                    s = _masked(s_s[rows, :] - c_row, r * chunk, diagonal)
                    p = jnp.exp(s - lse_s[rows, :])
                    ds = p * (dp_s[rows, :] - delta_s[rows, :])
                    p_s[rows, :] = p.astype(BF16)
                    ds_s[rows, :] = ds.astype(BF16)
                    dc8 = dc8 + jnp.sum(ds.reshape(chunk // 8, 8, t), axis=0)
                dc_s[...] += jnp.sum(dc8, axis=0, keepdims=True)
                dsb = ds_s[...]
                grads.append((lax.dot_general(p_s[...], do2, TN, preferred_element_type=F32),
                              lax.dot_general(dsb, k2, NN, preferred_element_type=F32),
                              lax.dot_general(dsb, q2, TN, preferred_element_type=F32)))
            first = _head_lanes(0)
            dv_s[...] += _pick(first, grads[0][0], grads[1][0])
            q_rows = pl.ds(pl.multiple_of(qi * t, t), t)
            dq_s[q_rows, :] += _pick(first, grads[0][1], grads[1][1]) * scale
            dk_s[...] += _pick(first, grads[0][2], grads[1][2]) * scale

        @pl.when(qi > ki)
        def _():
            step(False)

        @pl.when(qi == ki)
        def _():
            step(True)

        @pl.when(qi == n - 1)
        def _():
            dk_ref[...] = dk_s[...].astype(BF16)
            dv_ref[...] = dv_s[...].astype(BF16)
            for hh in range(PAIR):
                dc_ref[hh] = -scr[hh * per_head][...]

        @pl.when((ki == n - 1) & (qi == n - 1))
        def _():
            dq_ref[...] = dq_s[...].astype(BF16)

        if nc:
            @pl.when((hp == HEAD_PAIRS - 1) & (qi == n - 1) & (ki == n - 1))
            def _():
                for cp in _comm_copies(comm_in, comm_out, *sems):
                    cp.wait()

    def q_side(first_block):
        return pl.BlockSpec((t, LANES), lambda h, j, i: (jnp.maximum(i, j), first_block + h))

    def k_side(first_block):
        return pl.BlockSpec((t, LANES), lambda h, j, i: (j, first_block + h))

    any_spec = pl.BlockSpec(memory_space=pl.ANY)
    head_scratch = [pltpu.VMEM((1, t), F32), pltpu.VMEM((t, 1), F32), pltpu.VMEM((t, 1), F32),
                    pltpu.VMEM((t, t), F32), pltpu.VMEM((t, t), F32), pltpu.VMEM((t, t), BF16),
                    pltpu.VMEM((t, t), BF16)]
    grad_shape = jax.ShapeDtypeStruct((S, FOX_WIDTH), BF16)
    return pl.pallas_call(
        body, name="fox_bwd", grid=(HEAD_PAIRS, n, n),
        in_specs=[q_side(0), k_side(HEAD_PAIRS), k_side(2 * HEAD_PAIRS),
                  pl.BlockSpec((PAIR, 1, t), lambda h, j, i: (h, 0, j)), q_side(0), q_side(0),
                  pl.BlockSpec((PAIR, t, LANES), lambda h, j, i: (h, jnp.maximum(i, j), 0))] + [any_spec] * nc,
        out_specs=[pl.BlockSpec((S, LANES), lambda h, j, i: (0, h)), k_side(0), k_side(0),
                   pl.BlockSpec((PAIR, 1, t), lambda h, j, i: (h, 0, j))] + [any_spec] * nc,
        out_shape=[grad_shape, grad_shape, grad_shape, jax.ShapeDtypeStruct((FOX_HEADS, 1, S), F32)]
        + _comm_shapes(comm),
        scratch_shapes=[pltpu.VMEM((S, LANES), F32), pltpu.VMEM((t, LANES), F32), pltpu.VMEM((t, LANES), F32)]
        + head_scratch * PAIR + _comm_scratch(nc),
        compiler_params=_params("arbitrary", "arbitrary", "arbitrary"))(qkv, qkv, qkv, cT, o, do, lse, *comm)


def _lanes(g):
    return slice(g * POOL_GROUP_DIM, (g + 1) * POOL_GROUP_DIM)


def _window_sum(e, win, back):
    rows = e.shape[0]
    s = e
    sh = 1
    while sh < win:
        s = s + pltpu.roll(s, sh if back else rows - sh, 0)
        sh *= 2
    return s


def _pooled(u_ref, up_ref, i, g, win, T):
    cur = u_ref[:, _lanes(g)]
    tail = jnp.where(i > 0, up_ref[T - POOL_HALO:T, _lanes(g)], 0.0)
    e = jnp.concatenate([tail, cur], axis=0)
    s = _window_sum(e, win, True)
    t_idx = i * T - POOL_HALO + lax.broadcasted_iota(jnp.int32, (T + POOL_HALO, POOL_GROUP_DIM), 0)
    cnt = jnp.clip(t_idx + 1, 1, win).astype(F32)
    return (s / cnt - e)[POOL_HALO:, :]


def _pool_fwd(uf, pw, ps):
    S = uf.shape[0]
    T = _row_tile(S)

    def body(u_ref, up_ref, w_ref, sc_ref, o_ref):
        i = pl.program_id(0)
        for g, win in enumerate(POOL_WINDOWS):
            pb = _pooled(u_ref, up_ref, i, g, win, T).astype(BF16)
            yv = lax.dot_general(pb, w_ref[g], NN, preferred_element_type=F32)
            o_ref[:, _lanes(g)] = (yv * sc_ref[:, _lanes(g)]).astype(BF16)

    return pl.pallas_call(
        body, name="pool_fwd", grid=(S // T,),
        in_specs=[pl.BlockSpec((T, POOL_WIDTH), lambda i: (i, 0)),
                  pl.BlockSpec((T, POOL_WIDTH), lambda i: (jnp.maximum(i - 1, 0), 0)),
                  pl.BlockSpec((4, POOL_GROUP_DIM, POOL_GROUP_DIM), lambda i: (0, 0, 0)),
                  pl.BlockSpec((1, POOL_WIDTH), lambda i: (0, 0))],
        out_specs=pl.BlockSpec((T, POOL_WIDTH), lambda i: (i, 0)),
        out_shape=jax.ShapeDtypeStruct((S, POOL_WIDTH), BF16), compiler_params=_params("parallel"))(uf, uf, pw, ps)


def _pool_bwd(uf, dpool, pw, ps):
    S = uf.shape[0]
    T = _row_tile(S)
    nb = S // T

    def body(u_ref, up_ref, d_ref, dn_ref, w_ref, sc_ref, du_ref, dw_ref, dsc_ref):
        i = pl.program_id(0)

        @pl.when(i == 0)
        def _():
            dw_ref[...] = jnp.zeros_like(dw_ref)
            dsc_ref[...] = jnp.zeros_like(dsc_ref)

        t_idx = i * T + lax.broadcasted_iota(jnp.int32, (T + POOL_HALO, POOL_GROUP_DIM), 0)
        for g, win in enumerate(POOL_WINDOWS):
            pb = _pooled(u_ref, up_ref, i, g, win, T).astype(BF16)
            w = w_ref[g]
            sc = sc_ref[:, _lanes(g)]
            yv = lax.dot_general(pb, w, NN, preferred_element_type=F32)
            dov = d_ref[:, _lanes(g)]
            dsc_ref[:, _lanes(g)] += jnp.sum(dov * yv, axis=0, keepdims=True)
            head = jnp.where(i < nb - 1, dn_ref[0:POOL_HALO, _lanes(g)], 0.0)
            dyb = (jnp.concatenate([dov, head], axis=0) * sc).astype(BF16)
            dw_ref[g] += lax.dot_general(pb, dyb[:T], TN, preferred_element_type=F32)
            dpl = lax.dot_general(dyb, w, NT, preferred_element_type=F32)
            cnt = jnp.minimum(t_idx + 1, win).astype(F32)
            a = _window_sum(dpl / cnt, win, False)
            du_ref[:, _lanes(g)] = (a - dpl)[:T].astype(BF16)

    return pl.pallas_call(
        body, name="pool_bwd", grid=(nb,),
        in_specs=[pl.BlockSpec((T, POOL_WIDTH), lambda i: (i, 0)),
                  pl.BlockSpec((T, POOL_WIDTH), lambda i: (jnp.maximum(i - 1, 0), 0)),
                  pl.BlockSpec((T, POOL_WIDTH), lambda i: (i, 0)),
                  pl.BlockSpec((T, POOL_WIDTH), lambda i: (jnp.minimum(i + 1, nb - 1), 0)),
                  pl.BlockSpec((4, POOL_GROUP_DIM, POOL_GROUP_DIM), lambda i: (0, 0, 0)),
                  pl.BlockSpec((1, POOL_WIDTH), lambda i: (0, 0))],
        out_specs=[pl.BlockSpec((T, POOL_WIDTH), lambda i: (i, 0)),
                   pl.BlockSpec((4, POOL_GROUP_DIM, POOL_GROUP_DIM), lambda i: (0, 0, 0)),
                   pl.BlockSpec((1, POOL_WIDTH), lambda i: (0, 0))],
        out_shape=[jax.ShapeDtypeStruct((S, POOL_WIDTH), BF16),
                   jax.ShapeDtypeStruct((4, POOL_GROUP_DIM, POOL_GROUP_DIM), F32),
                   jax.ShapeDtypeStruct((1, POOL_WIDTH), F32)],
        compiler_params=_params("arbitrary"))(uf, uf, dpool, dpool, pw, ps)


def _xhead(h):
    return slice(h * X_HEAD_DIM, (h + 1) * X_HEAD_DIM)


def _xvhead(h):
    return slice(D_MODEL + h * X_HEAD_DIM, D_MODEL + (h + 1) * X_HEAD_DIM)


def _x_probs(qh, kh):
    s = lax.dot_general(qh, kh, NT, preferred_element_type=F32) * (1.0 / math.sqrt(X_HEAD_DIM))
    e = jnp.exp(s - jnp.max(s, axis=1, keepdims=True))
    return e / jnp.sum(e, axis=1, keepdims=True)


def _xattn_fwd(q, kv):
    S = q.shape[0]
    t = _row_tile(S)

    def body(q_ref, kv_ref, o_ref):
        for h in range(X_HEADS):
            p = _x_probs(q_ref[:, _xhead(h)], kv_ref[:, _xhead(h)])
            o_ref[:, _xhead(h)] = lax.dot_general(p.astype(BF16), kv_ref[:, _xvhead(h)], NN,
                                                  preferred_element_type=F32).astype(BF16)

    return pl.pallas_call(
        body, name="xattn_fwd", grid=(S // t,),
        in_specs=[pl.BlockSpec((t, D_MODEL), lambda i: (i, 0)), pl.BlockSpec((MEM_LEN, 2 * D_MODEL), lambda i: (0, 0))],
        out_specs=pl.BlockSpec((t, D_MODEL), lambda i: (i, 0)),
        out_shape=jax.ShapeDtypeStruct((S, D_MODEL), BF16), compiler_params=_params("parallel"))(q, kv)


def _xattn_bwd(q, kv, do):
    S = q.shape[0]
    t = _row_tile(S)
    nb = S // t
    scale = 1.0 / math.sqrt(X_HEAD_DIM)

    def body(q_ref, kv_ref, do_ref, dq_ref, dkv_ref, acc):
        i = pl.program_id(0)

        @pl.when(i == 0)
        def _():
            acc[...] = jnp.zeros_like(acc)

        for h in range(X_HEADS):
            qh = q_ref[:, _xhead(h)]
            kh = kv_ref[:, _xhead(h)]
            doh = do_ref[:, _xhead(h)]
            p = _x_probs(qh, kh)
            acc[:, _xvhead(h)] += lax.dot_general(p.astype(BF16), doh, TN, preferred_element_type=F32)
            dp = lax.dot_general(doh, kv_ref[:, _xvhead(h)], NT, preferred_element_type=F32)
            ds = p * (dp - jnp.sum(dp * p, axis=1, keepdims=True))
            dsb = ds.astype(BF16)
            dq_ref[:, _xhead(h)] = (lax.dot_general(dsb, kh, NN, preferred_element_type=F32) * scale).astype(BF16)
            acc[:, _xhead(h)] += lax.dot_general(dsb, qh, TN, preferred_element_type=F32) * scale

        @pl.when(i == nb - 1)
        def _():
            dkv_ref[...] = acc[...].astype(BF16)

    row = pl.BlockSpec((t, D_MODEL), lambda i: (i, 0))
    full = pl.BlockSpec((MEM_LEN, 2 * D_MODEL), lambda i: (0, 0))
    return pl.pallas_call(
        body, name="xattn_bwd", grid=(nb,), in_specs=[row, full, row], out_specs=[row, full],
        out_shape=[jax.ShapeDtypeStruct((S, D_MODEL), BF16), jax.ShapeDtypeStruct((MEM_LEN, 2 * D_MODEL), BF16)],
        scratch_shapes=[pltpu.VMEM((MEM_LEN, 2 * D_MODEL), F32)],
        compiler_params=_params("arbitrary"))(q, kv, do)


def _comm_shapes(arrs):
    return [jax.ShapeDtypeStruct((N_DEV,) + tuple(a.shape[-2:]), a.dtype) for a in arrs]


def _comm_scratch(n):
    if n == 0:
        return []
    return [pltpu.SemaphoreType.DMA((n, N_DEV - 1)), pltpu.SemaphoreType.DMA((n, N_DEV - 1)),
            pltpu.SemaphoreType.DMA((n,))]


def _comm_copies(ins, outs, send_sems, recv_sems, local_sems):
    x, y, c = lax.axis_index("x"), lax.axis_index("y"), lax.axis_index("c")
    me = 4 * x + 2 * y + c
    copies = []
    for w in range(len(ins)):
        src = ins[w] if len(ins[w].shape) == 2 else ins[w].at[me]
        copies.append(pltpu.make_async_copy(src, outs[w].at[me], local_sems.at[w]))
    for k in range(1, N_DEV):
        px = 1 - x if k & 4 else x
        py = 1 - y if k & 2 else y
        pc = 1 - c if k & 1 else c
        peer = 4 * px + 2 * py + pc
        for w in range(len(ins)):
            src = ins[w] if len(ins[w].shape) == 2 else ins[w].at[peer]
            copies.append(pltpu.make_async_remote_copy(
                src_ref=src, dst_ref=outs[w].at[me], send_sem=send_sems.at[w, k - 1],
                recv_sem=recv_sems.at[w, k - 1], device_id=(px, py, pc), device_id_type=pl.DeviceIdType.MESH))
    return copies


def _exchange(name, arrs):
    n = len(arrs)

    def body(*refs):
        copies = _comm_copies(refs[:n], refs[n:2 * n], *refs[2 * n:])
        for cp in copies:
            cp.start()
        for cp in copies:
            cp.wait()

    any_spec = pl.BlockSpec(memory_space=pl.ANY)
    return pl.pallas_call(
        body, name=name, in_specs=[any_spec] * n, out_specs=[any_spec] * n, out_shape=_comm_shapes(arrs),
        scratch_shapes=_comm_scratch(n))(*arrs)


def _adamw_math(w, g, m, v):
    m = ADAM_B1 * m + (1.0 - ADAM_B1) * g
    v = ADAM_B2 * v + (1.0 - ADAM_B2) * (g * g)
    m_hat = m / (1.0 - ADAM_B1 ** ADAM_STEP)
    v_hat = v / (1.0 - ADAM_B2 ** ADAM_STEP)
    delta = -ADAM_LR * (m_hat / (jnp.sqrt(v_hat) + ADAM_EPS) + ADAM_WD * w)
    return delta, m, v


def _sum_parts(p_ref):
    g = p_ref[0].astype(F32)
    for s in range(1, N_DEV):
        g = g + p_ref[s].astype(F32)
    return g


def _adamw_big(name, w, m, v, parts, tr):
    L, R, C = w.shape

    def body(w_ref, m_ref, v_ref, *rest):
        p_refs = rest[:L]
        g_ref, d_ref, nm_ref, nv_ref = rest[L:]
        layer = pl.program_id(0)
        for j in range(L):
            @pl.when(layer == j)
            def _(j=j):
                g = _sum_parts(p_refs[j])
                delta, nm, nv = _adamw_math(w_ref[...], g, m_ref[...], v_ref[...])
                g_ref[...] = g
                d_ref[...] = delta
                nm_ref[...] = nm
                nv_ref[...] = nv

    blk = pl.BlockSpec((None, tr, C), lambda l, i: (l, i, 0))

    def part_spec(j):
        return pl.BlockSpec((N_DEV, tr, C), lambda l, i: (0, jnp.where(l == j, i, 0), 0))

    shp = jax.ShapeDtypeStruct((L, R, C), F32)
    return pl.pallas_call(
        body, name=name, grid=(L, R // tr), in_specs=[blk, blk, blk] + [part_spec(j) for j in range(L)],
        out_specs=[blk] * 4, out_shape=[shp] * 4, compiler_params=_params("arbitrary", "arbitrary"))(w, m, v, *parts)


def _adamw_small(w, m, v, parts):
    R, C = w.shape

    def body(w_ref, m_ref, v_ref, p_ref, g_ref, d_ref, nm_ref, nv_ref):
        g = _sum_parts(p_ref)
        delta, nm, nv = _adamw_math(w_ref[...], g, m_ref[...], v_ref[...])
        g_ref[...] = g
        d_ref[...] = delta
        nm_ref[...] = nm
        nv_ref[...] = nv

    shp = jax.ShapeDtypeStruct((R, C), F32)
    return pl.pallas_call(body, name="adamw_small", out_shape=[shp] * 4,
                          compiler_params=pltpu.CompilerParams(vmem_limit_bytes=VMEM_LIMIT))(w, m, v, parts)


def _vec(a):
    return a.reshape(1, -1)


W_IN_SHARD = IN_COLS // N_DEV
W_IN_ROWS = 272


def _w_in_travel(a):
    pad = [(0, 0)] * (a.ndim - 2) + [(0, W_IN_ROWS - W_IN_SHARD), (0, 0)]
    return jnp.pad(jnp.swapaxes(a, -1, -2), pad)


def _unpack_w_in(g):
    full = jnp.transpose(g[:, :W_IN_SHARD, :], (2, 0, 1)).reshape(D_MODEL, IN_COLS)
    qkv = full[:, :QKV_COLS]
    f = full[:, QKV_COLS:QKV_COLS + FOX_HEADS]
    u = full[:, QKV_COLS + FOX_HEADS:]
    uf = jnp.concatenate([u, f, jnp.zeros((D_MODEL, UF_COLS - POOL_WIDTH - FOX_HEADS), g.dtype)], axis=1)
    return qkv, uf, jnp.concatenate([qkv, uf], axis=1)


def _pack_dw_in(dwp):
    qkv = dwp[:, :QKV_COLS]
    u = dwp[:, QKV_COLS:QKV_COLS + POOL_WIDTH]
    f = dwp[:, QKV_COLS + POOL_WIDTH:QKV_COLS + POOL_WIDTH + FOX_HEADS]
    full = jnp.concatenate([qkv, f, u], axis=1)
    return _w_in_travel(jnp.transpose(full.reshape(D_MODEL, N_DEV, W_IN_SHARD), (1, 0, 2)))


REST = ['w_out', 'wq_x', 'wkv_x', 'wo_x', 'w_up', 'w_down']


def _layer_fwd(x0, mem, sp, g_in, shards):
    S = x0.shape[0]
    sv = {"x0": x0}
    w_qkv, w_uf, w_inp = _unpack_w_in(g_in)
    h1 = _norm_fwd("norm_fwd", x0, sp["g_mix_pre"])
    qkv, uf = _mm_rows("mm_in", [(h1, w_inp, "nn")],
                       [(BF16, 0, QKV_COLS, "id"), (F32, QKV_COLS, UF_COLS, "id")], piece=UF_COLS)
    c = _gate_fwd(uf, sp["b_forget"])
    cT = jnp.transpose(c[:, :FOX_HEADS]).reshape(FOX_HEADS, 1, S)
    o, ob, lse, *got = _fox_fwd(qkv, cT, shards)
    g_out, g_q, g_kv, g_o, g_up, g_down = got[:6]
    W = dict(inp=w_inp, out=g_out.reshape(D_MODEL, D_MODEL), q=g_q.reshape(D_MODEL, D_MODEL), kv=g_kv,
             o=g_o.reshape(D_MODEL, D_MODEL), up=g_up, down=g_down.reshape(D_FF, D_MODEL))
    pool = _pool_fwd(uf, sp["pool_w"], sp["pool_scale"])
    cat = jnp.concatenate([ob, pool], axis=1)
    mix = _mm1("mm_sq", cat, W["out"], "nn", D_MODEL, F32)
    x1 = _resid_norm_fwd("resid_norm", x0, mix, sp["g_mix_post"])
    h2 = _norm_fwd("norm_fwd", x1, sp["g_x_pre"])
    mn = _norm_fwd("norm_mem", mem, sp["g_mem"])
    q2 = _mm1("mm_q", h2, W["q"], "nn", D_MODEL, BF16)
    kv = _mm1("mm_kv", mn, W["kv"], "nn3", 2 * D_MODEL, BF16, piece=2 * D_MODEL // N_DEV)
    o2 = _xattn_fwd(q2, kv)
    xo = _mm1("mm_sq", o2, W["o"], "nn", D_MODEL, F32)
    x2 = _resid_norm_fwd("resid_norm", x1, xo, sp["g_x_post"])
    h3 = _norm_fwd("norm_fwd", x2, sp["g_ffn_pre"])
    up, act = _mm_rows("mm_up", [(h3, W["up"], "nn3")], [(BF16, 0, D_FF, "id"), (BF16, 0, D_FF, "relu2")],
                       piece=D_FF // N_DEV)
    y = _mm1("mm_down", act, W["down"], "nn", D_MODEL, F32)
    x3 = _resid_norm_fwd("resid_norm", x2, y, sp["g_ffn_post"])
    sv.update(h1=h1, uf=uf, cT=cT, qkv=qkv, o=o, lse=lse, cat=cat, mix=mix, x1=x1, h2=h2, mn=mn, q2=q2, kv=kv,
              o2=o2, xo=xo, x2=x2, h3=h3, up=up, act=act, y=y)
    return x3, sv, W, (got[6] if len(got) > 6 else None)


def _layer_bwd(dx3, mem, sv, sp, W, carried):
    S = dx3.shape[0]
    gs = {}
    gb = {}
    dy, gs["g_ffn_post"] = _norm_bwd("norm_bwd_b", dx3, sv["y"], sp["g_ffn_post"], None, BF16)
    (dup,) = _mm_rows("mm_dup", [(dy, W["down"], "nt")], [(BF16, 0, D_FF, "drelu2")], extra=sv["up"])
    gb["w_down"] = _mm_tn("mm_dw_down", sv["act"], dy, BF16, tm=1024, tn=1024).reshape(N_DEV, D_FF // N_DEV, D_MODEL)
    gb["w_up"] = _mm_tn_cols("mm_dw_up", sv["h3"], dup, BF16, D_FF // N_DEV)
    dh3 = _mm1("mm_dh3", dup, W["up"], "nt3", D_MODEL, F32)
    dx2, gs["g_ffn_pre"] = _norm_bwd("norm_bwd_r", dh3, sv["x2"], sp["g_ffn_pre"], dx3, F32)
    dxo, gs["g_x_post"] = _norm_bwd("norm_bwd_b", dx2, sv["xo"], sp["g_x_post"], None, BF16)
    do2 = _mm1("mm_sq_t", dxo, W["o"], "nt", D_MODEL, BF16)
    gb["wo_x"] = _mm_tn("mm_dw_sq", sv["o2"], dxo, BF16, tm=1024, tn=1024).reshape(N_DEV, D_MODEL // N_DEV, D_MODEL)
    dq2, dkvb = _xattn_bwd(sv["q2"], sv["kv"], do2)
    gb["wq_x"] = _mm_tn("mm_dw_sq", sv["h2"], dq2, BF16, tm=1024, tn=1024).reshape(N_DEV, D_MODEL // N_DEV, D_MODEL)
    dh2 = _mm1("mm_dh2", dq2, W["q"], "nt", D_MODEL, F32)
    gb["wkv_x"] = _mm_tn_cols("mm_dw_kv", sv["mn"], dkvb, BF16, 2 * D_MODEL // N_DEV)
    dmn = _mm1("mm_dmn", dkvb, W["kv"], "nt3", D_MODEL, F32)
    _, gs["g_mem"] = _norm_bwd("norm_bwd_mem", dmn, mem, sp["g_mem"], None, BF16)
    dx1, gs["g_x_pre"] = _norm_bwd("norm_bwd_r", dh2, sv["x1"], sp["g_x_pre"], dx2, F32)
    dmix, gs["g_mix_post"] = _norm_bwd("norm_bwd_b", dx1, sv["mix"], sp["g_mix_post"], None, BF16)
    doh, dpool = _mm_rows("mm_dcat", [(dmix, W["out"], "nt")],
                          [(BF16, 0, FOX_WIDTH, "id"), (F32, FOX_WIDTH, POOL_WIDTH, "id")])
    gb["w_out"] = _mm_tn("mm_dw_sq", sv["cat"], dmix, BF16, tm=1024, tn=1024).reshape(N_DEV, D_MODEL // N_DEV, D_MODEL)
    du, gs["pool_w"], gs["pool_scale"] = _pool_bwd(sv["uf"], dpool, sp["pool_w"], sp["pool_scale"])
    dq, dk, dv, dcT, *got = _fox_bwd(sv["qkv"], sv["cT"], sv["o"], sv["lse"], doh, [gb[n] for n in REST] + carried)
    dc = jnp.pad(jnp.transpose(dcT.reshape(FOX_HEADS, S)), ((0, 0), (0, LANES - FOX_HEADS)))
    dfg, db = _gate_bwd(dc, sv["uf"], sp["b_forget"])
    gs["b_forget"] = db[:, :FOX_HEADS]
    dproj = jnp.concatenate([dq, dk, dv, du, dfg], axis=1)
    dwp = _mm_tn("mm_dw_in", sv["h1"], dproj, BF16, tm=512, tn=INP_COLS)
    dh1 = _mm1("mm_dh1", dproj, W["inp"], "nt", D_MODEL, F32)
    dx0, gs["g_mix_pre"] = _norm_bwd("norm_bwd_r", dh1, sv["x0"], sp["g_mix_pre"], dx1, F32)
    return dx0, dict(zip(REST, got[:6])), got[6:], _pack_dw_in(dwp), gs


SMALL_ROWS = 2392


def _pack_small(d):
    flat = jnp.concatenate([d[n].reshape(-1) for n in SMALL])
    return jnp.pad(flat, (0, SMALL_ROWS * LANES - flat.shape[0])).reshape(SMALL_ROWS, LANES)


def _unpack_small(packed, like):
    flat = packed.reshape(-1)
    out = {}
    off = 0
    for n in SMALL:
        size = math.prod(like[n].shape)
        out[n] = flat[off:off + size].reshape(like[n].shape)
        off += size
    return out


def kernel(x, mem, g_mix_pre, w_in, b_forget, pool_w, pool_scale, w_out, g_mix_post, g_x_pre, g_mem, wq_x, wkv_x, wo_x, g_x_post, g_ffn_pre, w_up, w_down, g_ffn_post, loss_target, m_g_mix_pre, m_w_in, m_b_forget, m_pool_w, m_pool_scale, m_w_out, m_g_mix_post, m_g_x_pre, m_g_mem, m_wq_x, m_wkv_x, m_wo_x, m_g_x_post, m_g_ffn_pre, m_w_up, m_w_down, m_g_ffn_post, v_g_mix_pre, v_w_in, v_b_forget, v_pool_w, v_pool_scale, v_w_out, v_g_mix_post, v_g_x_pre, v_g_mem, v_wq_x, v_wkv_x, v_wo_x, v_g_x_post, v_g_ffn_pre, v_w_up, v_w_down, v_g_ffn_post):
    w = dict(g_mix_pre=g_mix_pre, w_in=w_in, b_forget=b_forget, pool_w=pool_w, pool_scale=pool_scale, w_out=w_out,
             g_mix_post=g_mix_post, g_x_pre=g_x_pre, g_mem=g_mem, wq_x=wq_x, wkv_x=wkv_x, wo_x=wo_x,
             g_x_post=g_x_post, g_ffn_pre=g_ffn_pre, w_up=w_up, w_down=w_down, g_ffn_post=g_ffn_post)
    mom = dict(g_mix_pre=m_g_mix_pre, w_in=m_w_in, b_forget=m_b_forget, pool_w=m_pool_w, pool_scale=m_pool_scale,
               w_out=m_w_out, g_mix_post=m_g_mix_post, g_x_pre=m_g_x_pre, g_mem=m_g_mem, wq_x=m_wq_x,
               wkv_x=m_wkv_x, wo_x=m_wo_x, g_x_post=m_g_x_post, g_ffn_pre=m_g_ffn_pre, w_up=m_w_up,
               w_down=m_w_down, g_ffn_post=m_g_ffn_post)
    var = dict(g_mix_pre=v_g_mix_pre, w_in=v_w_in, b_forget=v_b_forget, pool_w=v_pool_w, pool_scale=v_pool_scale,
               w_out=v_w_out, g_mix_post=v_g_mix_post, g_x_pre=v_g_x_pre, g_mem=v_g_mem, wq_x=v_wq_x,
               wkv_x=v_wkv_x, wo_x=v_wo_x, g_x_post=v_g_x_post, g_ffn_pre=v_g_ffn_pre, w_up=v_w_up,
               w_down=v_w_down, g_ffn_post=v_g_ffn_post)
    S = x.shape[1]
    xs = x.reshape(S, D_MODEL)
    mems = mem.reshape(MEM_LEN, D_MODEL)
    target = loss_target.reshape(S, D_MODEL)

    def small_params(l):
        return dict(
            g_mix_pre=_vec(g_mix_pre[l]), g_mix_post=_vec(g_mix_post[l]), g_x_pre=_vec(g_x_pre[l]),
            g_mem=_vec(g_mem[l]), g_x_post=_vec(g_x_post[l]), g_ffn_pre=_vec(g_ffn_pre[l]),
            g_ffn_post=_vec(g_ffn_post[l]), pool_scale=_vec(pool_scale[l]), pool_w=pool_w[l].astype(BF16),
            b_forget=jnp.pad(_vec(b_forget[l]), ((0, 0), (0, LANES - FOX_HEADS))))

    shard = {n: [w[n][l].astype(BF16) for l in range(DEPTH)] for n in REST}
    shard["w_in"] = [_w_in_travel(w_in[l].astype(BF16)) for l in range(DEPTH)]
    sps = [small_params(l) for l in range(DEPTH)]
    saved, weights = [], []
    h = xs
    (g_in,) = _exchange("gather_w_in", [shard["w_in"][0]])
    for l in range(DEPTH):
        travelling = [shard[n][l] for n in REST] + ([shard["w_in"][l + 1]] if l + 1 < DEPTH else [])
        h, sv, W, g_in = _layer_fwd(h, mems, sps[l], g_in, travelling)
        saved.append(sv)
        weights.append(W)
    dh, sq = _loss_fwd_bwd(h, target)
    loss = lax.psum(0.5 * sq[0, 0] / D_MODEL, ("x", "y", "c"))

    parts = [dict() for _ in range(DEPTH)]
    small_grads = [None] * DEPTH
    carried = []
    for l in reversed(range(DEPTH)):
        dh, got, got_carried, dw_in, gs = _layer_bwd(dh, mems, saved[l], sps[l], weights[l], carried)
        parts[l].update(got)
        if got_carried:
            parts[l + 1]["w_in"] = got_carried[0]
        carried = [dw_in]
        small_grads[l] = gs
    (parts[0]["w_in"],) = _exchange("scatter_dw_in", carried)
    grad_x = dh.reshape(1, S, D_MODEL)

    grads, deltas, new_m, new_v = {}, {}, {}, {}
    rows = dict(w_in=128, w_out=128, wq_x=128, wkv_x=256, wo_x=128, w_up=256, w_down=128)
    for l in range(DEPTH):
        parts[l]["w_in"] = jnp.swapaxes(parts[l]["w_in"][:, :W_IN_SHARD, :], 1, 2)
    for n in BIG:
        grads[n], deltas[n], new_m[n], new_v[n] = _adamw_big(
            "adamw_" + n, w[n], mom[n], var[n], [parts[l][n] for l in range(DEPTH)], rows[n])

    sg = {n: jnp.stack([small_grads[l][n].reshape(w[n].shape[1:]) for l in range(DEPTH)]) for n in SMALL}
    (sg_parts,) = _exchange("gather_small_grads", [_pack_small(sg)])
    outs = _adamw_small(_pack_small(w), _pack_small(mom), _pack_small(var), sg_parts)
    for d, packed in zip((grads, deltas, new_m, new_v), outs):
        d.update(_unpack_small(packed, w))

    return (loss, grad_x, *[grads[n] for n in W_NAMES], *[deltas[n] for n in W_NAMES],
            *[new_m[n] for n in W_NAMES], *[new_v[n] for n in W_NAMES])
```

```python
import math

import jax
import jax.numpy as jnp
from jax import lax
from jax.experimental import pallas as pl
from jax.experimental.pallas import tpu as pltpu

F32 = jnp.float32
BF16 = jnp.bfloat16

D_MODEL = 1024
DEPTH = 4
FOX_WIDTH = 512
FOX_HEADS = 8
FOX_HEAD_DIM = 64
POOL_WIDTH = 512
POOL_WINDOWS = (2, 4, 8, 16)
POOL_GROUP_DIM = 128
POOL_HALO = 16
MEM_LEN = 256
X_HEADS = 4
X_HEAD_DIM = 256
D_FF = 4096
EPS = 1e-6
IN_COLS = 2056
QKV_COLS = 3 * FOX_WIDTH
UF_COLS = 640
INP_COLS = QKV_COLS + UF_COLS
N_DEV = 8
LANES = 128

ADAM_LR = 0.001
ADAM_B1 = 0.9
ADAM_B2 = 0.999
ADAM_EPS = 1e-08
ADAM_WD = 0.01
ADAM_STEP = 10

VMEM_LIMIT = 56 * 1024 * 1024

W_NAMES = ['g_mix_pre', 'w_in', 'b_forget', 'pool_w', 'pool_scale', 'w_out', 'g_mix_post', 'g_x_pre', 'g_mem',
           'wq_x', 'wkv_x', 'wo_x', 'g_x_post', 'g_ffn_pre', 'w_up', 'w_down', 'g_ffn_post']
BIG = ['w_in', 'w_out', 'wq_x', 'wkv_x', 'wo_x', 'w_up', 'w_down']
SMALL = [n for n in W_NAMES if n not in BIG]

NN = (((1,), (0,)), ((), ()))
NT = (((1,), (1,)), ((), ()))
TN = (((0,), (0,)), ((), ()))


def _params(*sem):
    return pltpu.CompilerParams(dimension_semantics=sem, vmem_limit_bytes=VMEM_LIMIT)


def _row_tile(s):
    return min(s, 512)


def _mm_rows(name, terms, outs, extra=None, piece=1024):
    M = terms[0][0].shape[0]
    tm = _row_tile(M)
    nterm = len(terms)
    n_extra = 0 if extra is None else 1
    groups = {}
    for idx, (_, c0, width, fn) in enumerate(outs):
        groups.setdefault((c0, width), []).append((idx, fn))

    def product(a_ref, w_ref, w, kind, c0, pw):
        cols = slice(c0, c0 + pw)
        if kind == "nn":
            return lax.dot_general(a_ref[...], w_ref[:, cols], NN, preferred_element_type=F32)
        if kind == "nt":
            return lax.dot_general(a_ref[...], w_ref[cols, :], NT, preferred_element_type=F32)
        n = w.shape[2]
        if kind == "nn3":
            assert pw == n and c0 % n == 0
            return lax.dot_general(a_ref[...], w_ref[c0 // n], NN, preferred_element_type=F32)
        r = None
        for j in range(w.shape[0]):
            part = lax.dot_general(a_ref[:, j * n:(j + 1) * n], w_ref[j, cols, :], NT, preferred_element_type=F32)
            r = part if r is None else r + part
        return r

    def body(*refs):
        a_refs = refs[0:2 * nterm:2]
        w_refs = refs[1:2 * nterm:2]
        extra_refs = refs[2 * nterm:2 * nterm + n_extra]
        out_refs = refs[2 * nterm + n_extra:]
        for (g0, gw), members in groups.items():
            for c0 in range(g0, g0 + gw, piece):
                pw = min(piece, g0 + gw - c0)
                r = None
                for a_ref, w_ref, (_, w, kind) in zip(a_refs, w_refs, terms):
                    part = product(a_ref, w_ref, w, kind, c0, pw)
                    r = part if r is None else r + part
                dst = slice(c0 - g0, c0 - g0 + pw)
                for idx, fn in members:
                    if fn == "relu2":
                        rp = jnp.maximum(r, 0.0)
                        val = rp * rp
                    elif fn == "drelu2":
                        val = r * (2.0 * jnp.maximum(extra_refs[0][:, dst].astype(F32), 0.0))
                    else:
                        val = r
                    out_refs[idx][:, dst] = val.astype(out_refs[idx].dtype)

    in_specs, ins = [], []
    for a, w, _ in terms:
        in_specs.append(pl.BlockSpec((tm, a.shape[1]), lambda i: (i, 0)))
        in_specs.append(pl.BlockSpec(w.shape, lambda i, nd=w.ndim: (0,) * nd))
        ins += [a, w]
    if extra is not None:
        in_specs.append(pl.BlockSpec((tm, extra.shape[1]), lambda i: (i, 0)))
        ins.append(extra)
    res = pl.pallas_call(
        body, name=name, grid=(M // tm,), in_specs=in_specs,
        out_specs=[pl.BlockSpec((tm, width), lambda i: (i, 0)) for _, _, width, _ in outs],
        out_shape=[jax.ShapeDtypeStruct((M, width), dt) for dt, _, width, _ in outs],
        compiler_params=_params("parallel"))(*ins)
    return res


def _mm1(name, a, w, kind, n_cols, dtype, piece=1024):
    return _mm_rows(name, [(a, w, kind)], [(dtype, 0, n_cols, "id")], piece=piece)[0]


def _mm_tn(name, a, b, out_dtype, shard_cols=None, piece=512):
    K, M = a.shape
    N = b.shape[1]
    tk = _row_tile(K)
    nk = K // tk
    piece = shard_cols or min(piece, N)

    def body(a_ref, b_ref, o_ref, acc, a_t):
        k = pl.program_id(0)

        @pl.when(k == 0)
        def _():
            acc[...] = jnp.zeros_like(acc)

        a_t[...] = jnp.transpose(a_ref[...])
        for c0 in range(0, N, piece):
            cols = slice(c0, min(c0 + piece, N))
            acc[:, cols] += lax.dot_general(a_t[...], b_ref[:, cols], NN, preferred_element_type=F32)

        @pl.when(k == nk - 1)
        def _():
            for c0 in range(0, N, piece):
                cols = slice(c0, min(c0 + piece, N))
                if shard_cols:
                    o_ref[c0 // piece] = acc[:, cols].astype(o_ref.dtype)
                else:
                    o_ref[:, cols] = acc[:, cols].astype(o_ref.dtype)

    out_dims = (N // shard_cols, M, shard_cols) if shard_cols else (M, N)
    return pl.pallas_call(
        body, name=name, grid=(nk,),
        in_specs=[pl.BlockSpec((tk, M), lambda k: (k, 0)), pl.BlockSpec((tk, N), lambda k: (k, 0))],
        out_specs=pl.BlockSpec(out_dims, lambda k, nd=len(out_dims): (0,) * nd),
        out_shape=jax.ShapeDtypeStruct(out_dims, out_dtype),
        scratch_shapes=[pltpu.VMEM((M, N), F32), pltpu.VMEM((M, tk), a.dtype)],
        compiler_params=_params("arbitrary"))(a, b)


def _norm_fwd(name, x, g):
    S, Dm = x.shape
    ts = _row_tile(S)

    def body(x_ref, g_ref, h_ref):
        xv = x_ref[...]
        r = lax.rsqrt(jnp.mean(xv * xv, axis=-1, keepdims=True) + EPS)
        h_ref[...] = ((xv * r) * g_ref[...]).astype(BF16)

    return pl.pallas_call(
        body, name=name, grid=(S // ts,),
        in_specs=[pl.BlockSpec((ts, Dm), lambda i: (i, 0)), pl.BlockSpec((1, Dm), lambda i: (0, 0))],
        out_specs=pl.BlockSpec((ts, Dm), lambda i: (i, 0)),
        out_shape=jax.ShapeDtypeStruct((S, Dm), BF16), compiler_params=_params("parallel"))(x, g)


def _resid_norm_fwd(name, x, f, g):
    S, Dm = x.shape
    ts = _row_tile(S)

    def body(x_ref, f_ref, g_ref, o_ref):
        fv = f_ref[...]
        r = lax.rsqrt(jnp.mean(fv * fv, axis=-1, keepdims=True) + EPS)
        o_ref[...] = x_ref[...] + (fv * r) * g_ref[...]

    row = pl.BlockSpec((ts, Dm), lambda i: (i, 0))
    return pl.pallas_call(
        body, name=name, grid=(S // ts,), in_specs=[row, row, pl.BlockSpec((1, Dm), lambda i: (0, 0))],
        out_specs=row, out_shape=jax.ShapeDtypeStruct((S, Dm), F32), compiler_params=_params("parallel"))(x, f, g)


def _norm_bwd(name, dout, y, g, resid, out_dtype):
    S, Dm = y.shape
    ts = _row_tile(S)
    has_resid = resid is not None

    def body(*refs):
        if has_resid:
            do_ref, y_ref, g_ref, r_ref, dy_ref, dg_ref = refs
        else:
            do_ref, y_ref, g_ref, dy_ref, dg_ref = refs
        i = pl.program_id(0)
        yv = y_ref[...]
        dov = do_ref[...]
        r = lax.rsqrt(jnp.mean(yv * yv, axis=-1, keepdims=True) + EPS)
        z = dov * g_ref[...]
        yr = yv * r
        dy = r * (z - yr * jnp.mean(yr * z, axis=-1, keepdims=True))
        if has_resid:
            dy = dy + r_ref[...]
        dy_ref[...] = dy.astype(out_dtype)

        @pl.when(i == 0)
        def _():
            dg_ref[...] = jnp.zeros_like(dg_ref)

        dg_ref[...] += jnp.sum(dov * yr, axis=0, keepdims=True)

    row = pl.BlockSpec((ts, Dm), lambda i: (i, 0))
    vec = pl.BlockSpec((1, Dm), lambda i: (0, 0))
    ins = [dout, y, g] + ([resid] if has_resid else [])
    specs = [row, row, vec] + ([row] if has_resid else [])
    return pl.pallas_call(
        body, name=name, grid=(S // ts,), in_specs=specs, out_specs=[row, vec],
        out_shape=[jax.ShapeDtypeStruct((S, Dm), out_dtype), jax.ShapeDtypeStruct((1, Dm), F32)],
        compiler_params=_params("arbitrary"))(*ins)


def _loss_fwd_bwd(y, t):
    S, Dm = y.shape
    ts = _row_tile(S)

    def body(y_ref, t_ref, dy_ref, acc_ref):
        i = pl.program_id(0)
        e = y_ref[...] - t_ref[...]
        dy_ref[...] = e * (1.0 / Dm)

        @pl.when(i == 0)
        def _():
            acc_ref[...] = jnp.zeros_like(acc_ref)

        s = jnp.sum(jnp.sum(e * e, axis=1, keepdims=True), axis=0, keepdims=True)
        acc_ref[...] += s

    row = pl.BlockSpec((ts, Dm), lambda i: (i, 0))
    return pl.pallas_call(
        body, name="loss", grid=(S // ts,), in_specs=[row, row],
        out_specs=[row, pl.BlockSpec((8, LANES), lambda i: (0, 0))],
        out_shape=[jax.ShapeDtypeStruct((S, Dm), F32), jax.ShapeDtypeStruct((8, LANES), F32)],
        compiler_params=_params("arbitrary"))(y, t)


def _log_sigmoid(x):
    return jnp.minimum(x, 0.0) - jnp.log(1.0 + jnp.exp(-jnp.abs(x)))


def _gate_fwd(uf, bpad):
    S = uf.shape[0]
    T = _row_tile(S)

    def body(f_ref, b_ref, c_ref, carry):
        i = pl.program_id(0)

        @pl.when(i == 0)
        def _():
            carry[...] = jnp.zeros_like(carry)

        lf = _log_sigmoid(f_ref[...] + b_ref[...])
        r = lax.broadcasted_iota(jnp.int32, (T, T), 0)
        cidx = lax.broadcasted_iota(jnp.int32, (T, T), 1)
        tri = (cidx <= r).astype(F32)
        c = lax.dot_general(tri, lf, NN, precision=lax.Precision.HIGHEST, preferred_element_type=F32)
        c_ref[...] = c + carry[0:1, :]
        carry[...] = carry[...] + jnp.sum(lf, axis=0, keepdims=True)

    return pl.pallas_call(
        body, name="gate_fwd", grid=(S // T,),
        in_specs=[pl.BlockSpec((T, LANES), lambda i: (i, 4)), pl.BlockSpec((1, LANES), lambda i: (0, 0))],
        out_specs=pl.BlockSpec((T, LANES), lambda i: (i, 0)),
        out_shape=jax.ShapeDtypeStruct((S, LANES), F32),
        scratch_shapes=[pltpu.VMEM((8, LANES), F32)], compiler_params=_params("arbitrary"))(uf, bpad)


def _gate_bwd(dc, uf, bpad):
    S = uf.shape[0]
    T = _row_tile(S)
    nb = S // T

    def body(dc_ref, f_ref, b_ref, df_ref, db_ref, carry):
        i = pl.program_id(0)

        @pl.when(i == 0)
        def _():
            carry[...] = jnp.zeros_like(carry)
            db_ref[...] = jnp.zeros_like(db_ref)

        dcv = dc_ref[...]
        r = lax.broadcasted_iota(jnp.int32, (T, T), 0)
        cidx = lax.broadcasted_iota(jnp.int32, (T, T), 1)
        tri = (cidx >= r).astype(F32)
        dlf = lax.dot_general(tri, dcv, NN, precision=lax.Precision.HIGHEST, preferred_element_type=F32)
        dlf = dlf + carry[0:1, :]
        carry[...] = carry[...] + jnp.sum(dcv, axis=0, keepdims=True)
        fg = f_ref[...] + b_ref[...]
        dfg = dlf / (1.0 + jnp.exp(fg))
        df_ref[...] = dfg.astype(BF16)
        db_ref[...] += jnp.sum(dfg, axis=0, keepdims=True)

    return pl.pallas_call(
        body, name="gate_bwd", grid=(nb,),
        in_specs=[pl.BlockSpec((T, LANES), lambda i: (nb - 1 - i, 0)),
                  pl.BlockSpec((T, LANES), lambda i: (nb - 1 - i, 4)),
                  pl.BlockSpec((1, LANES), lambda i: (0, 0))],
        out_specs=[pl.BlockSpec((T, LANES), lambda i: (nb - 1 - i, 0)), pl.BlockSpec((1, LANES), lambda i: (0, 0))],
        out_shape=[jax.ShapeDtypeStruct((S, LANES), BF16), jax.ShapeDtypeStruct((1, LANES), F32)],
        scratch_shapes=[pltpu.VMEM((8, LANES), F32)], compiler_params=_params("arbitrary"))(dc, uf, bpad)


FOX_CHUNK = 32
FOX_CHUNK_BWD = 64
HEAD_PAIRS = FOX_HEADS // 2
PAIR = 2


def _masked(s, row0, col0, diagonal):
    if diagonal:
        row = row0 + lax.broadcasted_iota(jnp.int32, s.shape, 0)
        col = col0 + lax.broadcasted_iota(jnp.int32, s.shape, 1)
        s = jnp.where(col <= row, s, -jnp.inf)
    return s


def _lane_block(b):
    return slice(b * LANES, (b + 1) * LANES)


def _fold(op, xs):
    acc = xs[0]
    for x in xs[1:]:
        acc = op(acc, x)
    return acc


def _head_lanes(hh):
    lane = lax.broadcasted_iota(jnp.int32, (1, LANES), 1)
    return (lane < FOX_HEAD_DIM) if hh == 0 else (lane >= FOX_HEAD_DIM)


def _pick(first_head, a, b):
    return jnp.where(first_head, a, b)


def _fox_fwd(qkv, cT, comm):
    S = qkv.shape[0]
    t = _row_tile(S)
    n = S // t
    nc = len(comm)
    scale = 1.0 / math.sqrt(FOX_HEAD_DIM)
    chunk = min(FOX_CHUNK, t)
    per_head = 7

    def body(q_ref, k_ref, v_ref, c_ref, *rest):
        comm_in = rest[:nc]
        o_ref, ob_ref, lse_ref = rest[nc:nc + 3]
        comm_out = rest[nc + 3:2 * nc + 3]
        scr = rest[2 * nc + 3:2 * nc + 3 + PAIR * per_head]
        sems = rest[2 * nc + 3 + PAIR * per_head:]
        hp = pl.program_id(0)
        qi = pl.program_id(1)
        ki = pl.program_id(2)

        if nc:
            @pl.when((hp == 0) & (qi == 0) & (ki == 0))
            def _():
                _Gather(comm_in, comm_out, *sems).start()

            @pl.when((hp == HEAD_PAIRS - 1) & (qi == 0) & (ki == 0))
            def _():
                _Gather(comm_in, comm_out, *sems).pass_on()

        @pl.when(ki == 0)
        def _():
            for hh in range(PAIR):
                m_s, l_s, a_s, acc_s = scr[hh * per_head:hh * per_head + 4]
                m_s[...] = jnp.full_like(m_s, -jnp.inf)
                l_s[...] = jnp.zeros_like(l_s)
                acc_s[...] = jnp.zeros_like(acc_s)

        def step(diagonal):
            q2 = q_ref[...] * scale
            k2 = k_ref[...]
            v2 = v_ref[...]
            for hh in range(PAIR):
                m_s, l_s, a_s, acc_s, s_s, ph_s, pl_s = scr[hh * per_head:(hh + 1) * per_head]
                qm = jnp.where(_head_lanes(hh), q2, jnp.zeros_like(q2))
                s_s[...] = lax.dot_general(qm, k2, NT, preferred_element_type=F32)
                for r in range(t // chunk):
                    rows = slice(r * chunk, (r + 1) * chunk)
                    blocks = [_masked(s_s[rows, _lane_block(b)] - c_ref[hh, :, _lane_block(b)], r * chunk,
                                      b * LANES, diagonal) for b in range(t // LANES)]
                    m_prev = m_s[rows, :]
                    m_new = jnp.maximum(m_prev, jnp.max(_fold(jnp.maximum, blocks), axis=1, keepdims=True))
                    alpha = jnp.exp(m_prev - m_new)
                    ps = [jnp.exp(blk - m_new) for blk in blocks]
                    l_s[rows, :] = alpha * l_s[rows, :] + jnp.sum(_fold(jnp.add, ps), axis=1, keepdims=True)
                    m_s[rows, :] = m_new
                    a_s[rows, :] = alpha
                    for b, p in enumerate(ps):
                        p_hi = p.astype(BF16)
                        ph_s[rows, _lane_block(b)] = p_hi
                        pl_s[rows, _lane_block(b)] = (p - p_hi.astype(F32)).astype(BF16)
                pv = (lax.dot_general(ph_s[...], v2, NN, preferred_element_type=F32)
                      + lax.dot_general(pl_s[...], v2, NN, preferred_element_type=F32))
                acc_s[...] = a_s[...] * acc_s[...] + pv

        @pl.when(ki < qi)
        def _():
            step(False)

        @pl.when(ki == qi)
        def _():
            step(True)
            heads = []
            for hh in range(PAIR):
                m_s, l_s, a_s, acc_s = scr[hh * per_head:hh * per_head + 4]
                heads.append(acc_s[...] / l_s[...])
                lse_ref[hh] = m_s[...] + jnp.log(l_s[...])
            o2 = _pick(_head_lanes(0), heads[0], heads[1])
            o_ref[...] = o2
            ob_ref[...] = o2.astype(BF16)

        if nc:
            @pl.when((hp == HEAD_PAIRS - 1) & (qi == n - 1) & (ki == n - 1))
            def _():
                _Gather(comm_in, comm_out, *sems).finish()

    def col_spec(first_block, causal):
        if causal:
            return pl.BlockSpec((t, LANES), lambda h, i, j: (jnp.minimum(i, j), first_block + h))
        return pl.BlockSpec((t, LANES), lambda h, i, j: (i, first_block + h))

    any_spec = pl.BlockSpec(memory_space=pl.ANY)
    head_scratch = [pltpu.VMEM((t, LANES), F32), pltpu.VMEM((t, LANES), F32), pltpu.VMEM((t, LANES), F32),
                    pltpu.VMEM((t, LANES), F32), pltpu.VMEM((t, t), F32), pltpu.VMEM((t, t), BF16),
                    pltpu.VMEM((t, t), BF16)]
    return pl.pallas_call(
        body, name="fox_fwd", grid=(HEAD_PAIRS, n, n),
        in_specs=[col_spec(0, False), col_spec(HEAD_PAIRS, True), col_spec(2 * HEAD_PAIRS, True),
                  pl.BlockSpec((PAIR, 1, t), lambda h, i, j: (h, 0, jnp.minimum(i, j)))] + [any_spec] * nc,
        out_specs=[col_spec(0, False), col_spec(0, False),
                   pl.BlockSpec((PAIR, t, LANES), lambda h, i, j: (h, i, 0))] + [any_spec] * nc,
        out_shape=[jax.ShapeDtypeStruct((S, FOX_WIDTH), F32), jax.ShapeDtypeStruct((S, FOX_WIDTH), BF16),
                   jax.ShapeDtypeStruct((FOX_HEADS, S, LANES), F32)] + _comm_shapes(comm),
        scratch_shapes=head_scratch * PAIR + _comm_scratch(nc),
        compiler_params=_params("arbitrary", "arbitrary", "arbitrary"))(qkv, qkv, qkv, cT, *comm)


def _fox_bwd(qkv, cT, o, lse, do, comm):
    S = qkv.shape[0]
    t = _row_tile(S)
    n = S // t
    nc = len(comm)
    scale = 1.0 / math.sqrt(FOX_HEAD_DIM)
    chunk = min(FOX_CHUNK_BWD, t)
    per_head = 6

    def body(q_ref, k_ref, v_ref, c_ref, o_ref, do_ref, lse_ref, *rest):
        comm_in = rest[:nc]
        dq_ref, dk_ref, dv_ref, dc_ref = rest[nc:nc + 4]
        comm_out = rest[nc + 4:2 * nc + 4]
        dq_s, dk_s, dv_s = rest[2 * nc + 4:2 * nc + 7]
        scr = rest[2 * nc + 7:2 * nc + 7 + PAIR * per_head]
        sems = rest[2 * nc + 7 + PAIR * per_head:]
        hp = pl.program_id(0)
        ki = pl.program_id(1)
        qi = pl.program_id(2)

        if nc:
            @pl.when((hp == 0) & (qi == 0) & (ki == 0))
            def _():
                for cp in _comm_copies(comm_in, comm_out, *sems):
                    cp.start()

        @pl.when((ki == 0) & (qi == 0))
        def _():
            dq_s[...] = jnp.zeros_like(dq_s)

        @pl.when(qi == ki)
        def _():
            dk_s[...] = jnp.zeros_like(dk_s)
            dv_s[...] = jnp.zeros_like(dv_s)
            for hh in range(PAIR):
                dc_s = scr[hh * per_head]
                dc_s[...] = jnp.zeros_like(dc_s)

        def step(diagonal):
            q2 = q_ref[...]
            k2 = k_ref[...]
            v2 = v_ref[...]
            do2 = do_ref[...]
            prod = do2.astype(F32) * o_ref[...]
            grads = []
            for hh in range(PAIR):
                dc_s, delta_s, s_s, dp_s, p_s, ds_s = scr[hh * per_head:(hh + 1) * per_head]
                mine = _head_lanes(hh)
                s_s[...] = lax.dot_general(jnp.where(mine, q2 * scale, jnp.zeros_like(q2)), k2, NT,
                                           preferred_element_type=F32)
                dp_s[...] = lax.dot_general(jnp.where(mine, do2, jnp.zeros_like(do2)), v2, NT,
                                            preferred_element_type=F32)
                delta_s[...] = jnp.broadcast_to(jnp.sum(jnp.where(mine, prod, 0.0), axis=1, keepdims=True),
                                                (t, LANES))
                dc8 = [jnp.zeros((8, LANES), F32) for _ in range(t // LANES)]
                for r in range(t // chunk):
                    rows = slice(r * chunk, (r + 1) * chunk)
                    lse = lse_ref[hh, rows, :]
                    delta = delta_s[rows, :]
                    for b in range(t // LANES):
                        s = _masked(s_s[rows, _lane_block(b)] - c_ref[hh, :, _lane_block(b)], r * chunk, b * LANES,
                                    diagonal)
                        p = jnp.exp(s - lse)
                        ds = p * (dp_s[rows, _lane_block(b)] - delta)
                        p_s[rows, _lane_block(b)] = p.astype(BF16)
                        ds_s[rows, _lane_block(b)] = ds.astype(BF16)
                        dc8[b] = dc8[b] + jnp.sum(ds.reshape(chunk // 8, 8, LANES), axis=0)
                for b in range(t // LANES):
                    dc_s[:, _lane_block(b)] += jnp.sum(dc8[b], axis=0, keepdims=True)
                dsb = ds_s[...]
                grads.append((lax.dot_general(p_s[...], do2, TN, preferred_element_type=F32),
                              lax.dot_general(dsb, k2, NN, preferred_element_type=F32),
                              lax.dot_general(dsb, q2, TN, preferred_element_type=F32)))
            first = _head_lanes(0)
            dv_s[...] += _pick(first, grads[0][0], grads[1][0])
            q_rows = pl.ds(pl.multiple_of(qi * t, t), t)
            dq_s[q_rows, :] += _pick(first, grads[0][1], grads[1][1]) * scale
            dk_s[...] += _pick(first, grads[0][2], grads[1][2]) * scale

        @pl.when(qi > ki)
        def _():
            step(False)

        @pl.when(qi == ki)
        def _():
            step(True)

        @pl.when(qi == n - 1)
        def _():
            dk_ref[...] = dk_s[...].astype(BF16)
            dv_ref[...] = dv_s[...].astype(BF16)
            for hh in range(PAIR):
                dc_ref[hh] = -scr[hh * per_head][...]

        @pl.when((ki == n - 1) & (qi == n - 1))
        def _():
            dq_ref[...] = dq_s[...].astype(BF16)

        if nc:
            @pl.when((hp == HEAD_PAIRS - 1) & (qi == n - 1) & (ki == n - 1))
            def _():
                for cp in _comm_copies(comm_in, comm_out, *sems):
                    cp.wait()

    def q_side(first_block):
        return pl.BlockSpec((t, LANES), lambda h, j, i: (jnp.maximum(i, j), first_block + h))

    def k_side(first_block):
        return pl.BlockSpec((t, LANES), lambda h, j, i: (j, first_block + h))

    any_spec = pl.BlockSpec(memory_space=pl.ANY)
    head_scratch = [pltpu.VMEM((1, t), F32), pltpu.VMEM((t, LANES), F32),
                    pltpu.VMEM((t, t), F32), pltpu.VMEM((t, t), F32), pltpu.VMEM((t, t), BF16),
                    pltpu.VMEM((t, t), BF16)]
    grad_shape = jax.ShapeDtypeStruct((S, FOX_WIDTH), BF16)
    return pl.pallas_call(
        body, name="fox_bwd", grid=(HEAD_PAIRS, n, n),
        in_specs=[q_side(0), k_side(HEAD_PAIRS), k_side(2 * HEAD_PAIRS),
                  pl.BlockSpec((PAIR, 1, t), lambda h, j, i: (h, 0, j)), q_side(0), q_side(0),
                  pl.BlockSpec((PAIR, t, LANES), lambda h, j, i: (h, jnp.maximum(i, j), 0))] + [any_spec] * nc,
        out_specs=[pl.BlockSpec((S, LANES), lambda h, j, i: (0, h)), k_side(0), k_side(0),
                   pl.BlockSpec((PAIR, 1, t), lambda h, j, i: (h, 0, j))] + [any_spec] * nc,
        out_shape=[grad_shape, grad_shape, grad_shape, jax.ShapeDtypeStruct((FOX_HEADS, 1, S), F32)]
        + _comm_shapes(comm),
        scratch_shapes=[pltpu.VMEM((S, LANES), F32), pltpu.VMEM((t, LANES), F32), pltpu.VMEM((t, LANES), F32)]
        + head_scratch * PAIR + _comm_scratch(nc),
        compiler_params=_params("arbitrary", "arbitrary", "arbitrary"))(qkv, qkv, qkv, cT, o, do, lse, *comm)


def _lanes(g):
    return slice(g * POOL_GROUP_DIM, (g + 1) * POOL_GROUP_DIM)


def _window_sum(e, win, back):
    rows = e.shape[0]
    s = e
    sh = 1
    while sh < win:
        s = s + pltpu.roll(s, sh if back else rows - sh, 0)
        sh *= 2
    return s


def _pooled(u_ref, up_ref, i, g, win, T):
    cur = u_ref[:, _lanes(g)]
    tail = jnp.where(i > 0, up_ref[T - POOL_HALO:T, _lanes(g)], 0.0)
    e = jnp.concatenate([tail, cur], axis=0)
    s = _window_sum(e, win, True)
    t_idx = i * T - POOL_HALO + lax.broadcasted_iota(jnp.int32, (T + POOL_HALO, POOL_GROUP_DIM), 0)
    cnt = jnp.clip(t_idx + 1, 1, win).astype(F32)
    return (s / cnt - e)[POOL_HALO:, :]


def _pool_fwd(uf, pw, ps):
    S = uf.shape[0]
    T = _row_tile(S)

    def body(u_ref, up_ref, w_ref, sc_ref, o_ref):
        i = pl.program_id(0)
        for g, win in enumerate(POOL_WINDOWS):
            pb = _pooled(u_ref, up_ref, i, g, win, T).astype(BF16)
            yv = lax.dot_general(pb, w_ref[g], NN, preferred_element_type=F32)
            o_ref[:, _lanes(g)] = (yv * sc_ref[:, _lanes(g)]).astype(BF16)

    return pl.pallas_call(
        body, name="pool_fwd", grid=(S // T,),
        in_specs=[pl.BlockSpec((T, POOL_WIDTH), lambda i: (i, 0)),
                  pl.BlockSpec((T, POOL_WIDTH), lambda i: (jnp.maximum(i - 1, 0), 0)),
                  pl.BlockSpec((4, POOL_GROUP_DIM, POOL_GROUP_DIM), lambda i: (0, 0, 0)),
                  pl.BlockSpec((1, POOL_WIDTH), lambda i: (0, 0))],
        out_specs=pl.BlockSpec((T, POOL_WIDTH), lambda i: (i, 0)),
        out_shape=jax.ShapeDtypeStruct((S, POOL_WIDTH), BF16), compiler_params=_params("parallel"))(uf, uf, pw, ps)


def _pool_bwd(uf, dpool, pw, ps):
    S = uf.shape[0]
    T = _row_tile(S)
    nb = S // T

    def body(u_ref, up_ref, d_ref, dn_ref, w_ref, sc_ref, du_ref, dw_ref, dsc_ref):
        i = pl.program_id(0)

        @pl.when(i == 0)
        def _():
            dw_ref[...] = jnp.zeros_like(dw_ref)
            dsc_ref[...] = jnp.zeros_like(dsc_ref)

        t_idx = i * T + lax.broadcasted_iota(jnp.int32, (T + POOL_HALO, POOL_GROUP_DIM), 0)
        for g, win in enumerate(POOL_WINDOWS):
            pb = _pooled(u_ref, up_ref, i, g, win, T).astype(BF16)
            w = w_ref[g]
            sc = sc_ref[:, _lanes(g)]
            yv = lax.dot_general(pb, w, NN, preferred_element_type=F32)
            dov = d_ref[:, _lanes(g)]
            dsc_ref[:, _lanes(g)] += jnp.sum(dov * yv, axis=0, keepdims=True)
            head = jnp.where(i < nb - 1, dn_ref[0:POOL_HALO, _lanes(g)], 0.0)
            dyb = (jnp.concatenate([dov, head], axis=0) * sc).astype(BF16)
            dw_ref[g] += lax.dot_general(pb, dyb[:T], TN, preferred_element_type=F32)
            dpl = lax.dot_general(dyb, w, NT, preferred_element_type=F32)
            cnt = jnp.minimum(t_idx + 1, win).astype(F32)
            a = _window_sum(dpl / cnt, win, False)
            du_ref[:, _lanes(g)] = (a - dpl)[:T].astype(BF16)

    return pl.pallas_call(
        body, name="pool_bwd", grid=(nb,),
        in_specs=[pl.BlockSpec((T, POOL_WIDTH), lambda i: (i, 0)),
                  pl.BlockSpec((T, POOL_WIDTH), lambda i: (jnp.maximum(i - 1, 0), 0)),
                  pl.BlockSpec((T, POOL_WIDTH), lambda i: (i, 0)),
                  pl.BlockSpec((T, POOL_WIDTH), lambda i: (jnp.minimum(i + 1, nb - 1), 0)),
                  pl.BlockSpec((4, POOL_GROUP_DIM, POOL_GROUP_DIM), lambda i: (0, 0, 0)),
                  pl.BlockSpec((1, POOL_WIDTH), lambda i: (0, 0))],
        out_specs=[pl.BlockSpec((T, POOL_WIDTH), lambda i: (i, 0)),
                   pl.BlockSpec((4, POOL_GROUP_DIM, POOL_GROUP_DIM), lambda i: (0, 0, 0)),
                   pl.BlockSpec((1, POOL_WIDTH), lambda i: (0, 0))],
        out_shape=[jax.ShapeDtypeStruct((S, POOL_WIDTH), BF16),
                   jax.ShapeDtypeStruct((4, POOL_GROUP_DIM, POOL_GROUP_DIM), F32),
                   jax.ShapeDtypeStruct((1, POOL_WIDTH), F32)],
        compiler_params=_params("arbitrary"))(uf, uf, dpool, dpool, pw, ps)


def _xhead(h):
    return slice(h * X_HEAD_DIM, (h + 1) * X_HEAD_DIM)


def _xvhead(h):
    return slice(D_MODEL + h * X_HEAD_DIM, D_MODEL + (h + 1) * X_HEAD_DIM)


def _x_probs(qh, kh):
    s = lax.dot_general(qh, kh, NT, preferred_element_type=F32) * (1.0 / math.sqrt(X_HEAD_DIM))
    e = jnp.exp(s - jnp.max(s, axis=1, keepdims=True))
    return e / jnp.sum(e, axis=1, keepdims=True)


def _xattn_fwd(q, kv):
    S = q.shape[0]
    t = _row_tile(S)

    def body(q_ref, kv_ref, o_ref):
        for h in range(X_HEADS):
            p = _x_probs(q_ref[:, _xhead(h)], kv_ref[:, _xhead(h)])
            o_ref[:, _xhead(h)] = lax.dot_general(p.astype(BF16), kv_ref[:, _xvhead(h)], NN,
                                                  preferred_element_type=F32).astype(BF16)

    return pl.pallas_call(
        body, name="xattn_fwd", grid=(S // t,),
        in_specs=[pl.BlockSpec((t, D_MODEL), lambda i: (i, 0)), pl.BlockSpec((MEM_LEN, 2 * D_MODEL), lambda i: (0, 0))],
        out_specs=pl.BlockSpec((t, D_MODEL), lambda i: (i, 0)),
        out_shape=jax.ShapeDtypeStruct((S, D_MODEL), BF16), compiler_params=_params("parallel"))(q, kv)


def _xattn_bwd(q, kv, do):
    S = q.shape[0]
    t = _row_tile(S)
    nb = S // t
    scale = 1.0 / math.sqrt(X_HEAD_DIM)

    def body(q_ref, kv_ref, do_ref, dq_ref, dkv_ref, acc):
        i = pl.program_id(0)

        @pl.when(i == 0)
        def _():
            acc[...] = jnp.zeros_like(acc)

        for h in range(X_HEADS):
            qh = q_ref[:, _xhead(h)]
            kh = kv_ref[:, _xhead(h)]
            doh = do_ref[:, _xhead(h)]
            p = _x_probs(qh, kh)
            acc[:, _xvhead(h)] += lax.dot_general(p.astype(BF16), doh, TN, preferred_element_type=F32)
            dp = lax.dot_general(doh, kv_ref[:, _xvhead(h)], NT, preferred_element_type=F32)
            ds = p * (dp - jnp.sum(dp * p, axis=1, keepdims=True))
            dsb = ds.astype(BF16)
            dq_ref[:, _xhead(h)] = (lax.dot_general(dsb, kh, NN, preferred_element_type=F32) * scale).astype(BF16)
            acc[:, _xhead(h)] += lax.dot_general(dsb, qh, TN, preferred_element_type=F32) * scale

        @pl.when(i == nb - 1)
        def _():
            dkv_ref[...] = acc[...].astype(BF16)

    row = pl.BlockSpec((t, D_MODEL), lambda i: (i, 0))
    full = pl.BlockSpec((MEM_LEN, 2 * D_MODEL), lambda i: (0, 0))
    return pl.pallas_call(
        body, name="xattn_bwd", grid=(nb,), in_specs=[row, full, row], out_specs=[row, full],
        out_shape=[jax.ShapeDtypeStruct((S, D_MODEL), BF16), jax.ShapeDtypeStruct((MEM_LEN, 2 * D_MODEL), BF16)],
        scratch_shapes=[pltpu.VMEM((MEM_LEN, 2 * D_MODEL), F32)],
        compiler_params=_params("arbitrary"))(q, kv, do)


def _comm_shapes(arrs):
    return [jax.ShapeDtypeStruct((N_DEV,) + tuple(a.shape[-2:]), a.dtype) for a in arrs]


def _comm_scratch(n):
    if n == 0:
        return []
    return [pltpu.SemaphoreType.DMA((n, N_DEV - 1)), pltpu.SemaphoreType.DMA((n, N_DEV - 1)),
            pltpu.SemaphoreType.DMA((n,))]


def _comm_copies(ins, outs, send_sems, recv_sems, local_sems):
    x, y, c = lax.axis_index("x"), lax.axis_index("y"), lax.axis_index("c")
    me = 4 * x + 2 * y + c
    copies = []
    for w in range(len(ins)):
        src = ins[w] if len(ins[w].shape) == 2 else ins[w].at[me]
        copies.append(pltpu.make_async_copy(src, outs[w].at[me], local_sems.at[w]))
    for k in range(1, N_DEV):
        px = 1 - x if k & 4 else x
        py = 1 - y if k & 2 else y
        pc = 1 - c if k & 1 else c
        peer = 4 * px + 2 * py + pc
        for w in range(len(ins)):
            src = ins[w] if len(ins[w].shape) == 2 else ins[w].at[peer]
            copies.append(pltpu.make_async_remote_copy(
                src_ref=src, dst_ref=outs[w].at[me], send_sem=send_sems.at[w, k - 1],
                recv_sem=recv_sems.at[w, k - 1], device_id=(px, py, pc), device_id_type=pl.DeviceIdType.MESH))
    return copies


class _Gather:
    def __init__(self, ins, outs, send_sems, recv_sems, local_sems):
        x, y, c = lax.axis_index("x"), lax.axis_index("y"), lax.axis_index("c")
        me = 4 * x + 2 * y + c
        sibling = (x, y, 1 - c)
        self.local, self.mine, self.passed = [], [], []
        for w in range(len(ins)):
            def remote(idx, src, slot, dev, w=w):
                return pltpu.make_async_remote_copy(
                    src_ref=src, dst_ref=outs[w].at[slot], send_sem=send_sems.at[w, idx],
                    recv_sem=recv_sems.at[w, idx], device_id=dev, device_id_type=pl.DeviceIdType.MESH)

            self.local.append(pltpu.make_async_copy(ins[w], outs[w].at[me], local_sems.at[w]))
            mine, passed = [remote(0, ins[w], me, sibling)], []
            for j, (fx, fy) in enumerate(((0, 1), (1, 0), (1, 1))):
                px = 1 - x if fx else x
                py = 1 - y if fy else y
                slot = 4 * px + 2 * py + c
                mine.append(remote(1 + j, ins[w], me, (px, py, c)))
                passed.append(remote(4 + j, outs[w].at[slot], slot, sibling))
            self.mine.append(mine)
            self.passed.append(passed)

    def start(self):
        for cp in self.local:
            cp.start()
        for mine in self.mine:
            for cp in mine:
                cp.start()

    def pass_on(self):
        for mine, passed in zip(self.mine, self.passed):
            for j, cp in enumerate(passed):
                mine[1 + j].wait_recv()
                cp.start()

    def finish(self):
        for mine, passed in zip(self.mine, self.passed):
            mine[0].wait_recv()
            for cp in passed:
                cp.wait_recv()
            for cp in mine + passed:
                cp.wait_send()
        for cp in self.local:
            cp.wait()


def _exchange(name, arrs):
    n = len(arrs)
    gather = all(a.ndim == 2 for a in arrs)

    def body(*refs):
        if gather:
            g = _Gather(refs[:n], refs[n:2 * n], *refs[2 * n:])
            g.start()
            g.pass_on()
            g.finish()
            return
        copies = _comm_copies(refs[:n], refs[n:2 * n], *refs[2 * n:])
        for cp in copies:
            cp.start()
        for cp in copies:
            cp.wait()

    any_spec = pl.BlockSpec(memory_space=pl.ANY)
    return pl.pallas_call(
        body, name=name, in_specs=[any_spec] * n, out_specs=[any_spec] * n, out_shape=_comm_shapes(arrs),
        scratch_shapes=_comm_scratch(n))(*arrs)


def _adamw_math(w, g, m, v):
    m = ADAM_B1 * m + (1.0 - ADAM_B1) * g
    v = ADAM_B2 * v + (1.0 - ADAM_B2) * (g * g)
    m_hat = m / (1.0 - ADAM_B1 ** ADAM_STEP)
    v_hat = v / (1.0 - ADAM_B2 ** ADAM_STEP)
    delta = -ADAM_LR * (m_hat / (jnp.sqrt(v_hat) + ADAM_EPS) + ADAM_WD * w)
    return delta, m, v


def _sum_parts(p_ref):
    g = p_ref[0].astype(F32)
    for s in range(1, N_DEV):
        g = g + p_ref[s].astype(F32)
    return g


def _adamw_big(name, w, m, v, parts, tr):
    L, R, C = w.shape

    def body(w_ref, m_ref, v_ref, *rest):
        p_refs = rest[:L]
        g_ref, d_ref, nm_ref, nv_ref = rest[L:]
        layer = pl.program_id(0)
        for j in range(L):
            @pl.when(layer == j)
            def _(j=j):
                g = _sum_parts(p_refs[j])
                delta, nm, nv = _adamw_math(w_ref[...], g, m_ref[...], v_ref[...])
                g_ref[...] = g
                d_ref[...] = delta
                nm_ref[...] = nm
                nv_ref[...] = nv

    blk = pl.BlockSpec((None, tr, C), lambda l, i: (l, i, 0))

    def part_spec(j):
        return pl.BlockSpec((N_DEV, tr, C), lambda l, i: (0, jnp.where(l == j, i, 0), 0))

    shp = jax.ShapeDtypeStruct((L, R, C), F32)
    return pl.pallas_call(
        body, name=name, grid=(L, R // tr), in_specs=[blk, blk, blk] + [part_spec(j) for j in range(L)],
        out_specs=[blk] * 4, out_shape=[shp] * 4, compiler_params=_params("arbitrary", "arbitrary"))(w, m, v, *parts)


def _adamw_small(w, m, v, parts):
    R, C = w.shape

    def body(w_ref, m_ref, v_ref, p_ref, g_ref, d_ref, nm_ref, nv_ref):
        g = _sum_parts(p_ref)
        delta, nm, nv = _adamw_math(w_ref[...], g, m_ref[...], v_ref[...])
        g_ref[...] = g
        d_ref[...] = delta
        nm_ref[...] = nm
        nv_ref[...] = nv

    shp = jax.ShapeDtypeStruct((R, C), F32)
    return pl.pallas_call(body, name="adamw_small", out_shape=[shp] * 4,
                          compiler_params=pltpu.CompilerParams(vmem_limit_bytes=VMEM_LIMIT))(w, m, v, parts)


def _vec(a):
    return a.reshape(1, -1)


W_IN_SHARD = IN_COLS // N_DEV
W_IN_ROWS = 272


def _w_in_travel(a):
    pad = [(0, 0)] * (a.ndim - 2) + [(0, W_IN_ROWS - W_IN_SHARD), (0, 0)]
    return jnp.pad(jnp.swapaxes(a, -1, -2), pad)


def _unpack_w_in(g):
    full = jnp.transpose(g[:, :W_IN_SHARD, :], (2, 0, 1)).reshape(D_MODEL, IN_COLS)
    qkv = full[:, :QKV_COLS]
    f = full[:, QKV_COLS:QKV_COLS + FOX_HEADS]
    u = full[:, QKV_COLS + FOX_HEADS:]
    uf = jnp.concatenate([u, f, jnp.zeros((D_MODEL, UF_COLS - POOL_WIDTH - FOX_HEADS), g.dtype)], axis=1)
    return qkv, uf, jnp.concatenate([qkv, uf], axis=1)


def _pack_dw_in(dwp):
    qkv = dwp[:, :QKV_COLS]
    u = dwp[:, QKV_COLS:QKV_COLS + POOL_WIDTH]
    f = dwp[:, QKV_COLS + POOL_WIDTH:QKV_COLS + POOL_WIDTH + FOX_HEADS]
    full = jnp.concatenate([qkv, f, u], axis=1)
    return _w_in_travel(jnp.transpose(full.reshape(D_MODEL, N_DEV, W_IN_SHARD), (1, 0, 2)))


REST = ['w_out', 'wq_x', 'wkv_x', 'wo_x', 'w_up', 'w_down']


def _layer_fwd(x0, mem, sp, g_in, shards):
    S = x0.shape[0]
    sv = {"x0": x0}
    w_qkv, w_uf, w_inp = _unpack_w_in(g_in)
    h1 = _norm_fwd("norm_fwd", x0, sp["g_mix_pre"])
    qkv, uf = _mm_rows("mm_in", [(h1, w_inp, "nn")],
                       [(BF16, 0, QKV_COLS, "id"), (F32, QKV_COLS, UF_COLS, "id")], piece=UF_COLS)
    c = _gate_fwd(uf, sp["b_forget"])
    cT = jnp.transpose(c[:, :FOX_HEADS]).reshape(FOX_HEADS, 1, S)
    o, ob, lse, *got = _fox_fwd(qkv, cT, shards)
    g_out, g_q, g_kv, g_o, g_up, g_down = got[:6]
    W = dict(inp=w_inp, out=g_out.reshape(D_MODEL, D_MODEL), q=g_q.reshape(D_MODEL, D_MODEL), kv=g_kv,
             o=g_o.reshape(D_MODEL, D_MODEL), up=g_up, down=g_down.reshape(D_FF, D_MODEL))
    pool = _pool_fwd(uf, sp["pool_w"], sp["pool_scale"])
    cat = jnp.concatenate([ob, pool], axis=1)
    mix = _mm1("mm_sq", cat, W["out"], "nn", D_MODEL, F32)
    x1 = _resid_norm_fwd("resid_norm", x0, mix, sp["g_mix_post"])
    h2 = _norm_fwd("norm_fwd", x1, sp["g_x_pre"])
    mn = _norm_fwd("norm_mem", mem, sp["g_mem"])
    q2 = _mm1("mm_q", h2, W["q"], "nn", D_MODEL, BF16)
    kv = _mm1("mm_kv", mn, W["kv"], "nn3", 2 * D_MODEL, BF16, piece=2 * D_MODEL // N_DEV)
    o2 = _xattn_fwd(q2, kv)
    xo = _mm1("mm_sq", o2, W["o"], "nn", D_MODEL, F32)
    x2 = _resid_norm_fwd("resid_norm", x1, xo, sp["g_x_post"])
    h3 = _norm_fwd("norm_fwd", x2, sp["g_ffn_pre"])
    up, act = _mm_rows("mm_up", [(h3, W["up"], "nn3")], [(BF16, 0, D_FF, "id"), (BF16, 0, D_FF, "relu2")],
                       piece=D_FF // N_DEV)
    y = _mm1("mm_down", act, W["down"], "nn", D_MODEL, F32)
    x3 = _resid_norm_fwd("resid_norm", x2, y, sp["g_ffn_post"])
    sv.update(h1=h1, uf=uf, cT=cT, qkv=qkv, o=o, lse=lse, cat=cat, mix=mix, x1=x1, h2=h2, mn=mn, q2=q2, kv=kv,
              o2=o2, xo=xo, x2=x2, h3=h3, up=up, act=act, y=y)
    return x3, sv, W, (got[6] if len(got) > 6 else None)


def _layer_bwd(dx3, mem, sv, sp, W, carried):
    S = dx3.shape[0]
    gs = {}
    gb = {}
    dy, gs["g_ffn_post"] = _norm_bwd("norm_bwd_b", dx3, sv["y"], sp["g_ffn_post"], None, BF16)
    (dup,) = _mm_rows("mm_dup", [(dy, W["down"], "nt")], [(BF16, 0, D_FF, "drelu2")], extra=sv["up"])
    gb["w_down"] = _mm_tn("mm_dw_down", sv["act"], dy, BF16).reshape(N_DEV, D_FF // N_DEV, D_MODEL)
    gb["w_up"] = _mm_tn("mm_dw_up", sv["h3"], dup, BF16, shard_cols=D_FF // N_DEV)
    dh3 = _mm1("mm_dh3", dup, W["up"], "nt3", D_MODEL, F32)
    dx2, gs["g_ffn_pre"] = _norm_bwd("norm_bwd_r", dh3, sv["x2"], sp["g_ffn_pre"], dx3, F32)
    dxo, gs["g_x_post"] = _norm_bwd("norm_bwd_b", dx2, sv["xo"], sp["g_x_post"], None, BF16)
    do2 = _mm1("mm_sq_t", dxo, W["o"], "nt", D_MODEL, BF16)
    gb["wo_x"] = _mm_tn("mm_dw_sq", sv["o2"], dxo, BF16).reshape(N_DEV, D_MODEL // N_DEV, D_MODEL)
    dq2, dkvb = _xattn_bwd(sv["q2"], sv["kv"], do2)
    gb["wq_x"] = _mm_tn("mm_dw_sq", sv["h2"], dq2, BF16).reshape(N_DEV, D_MODEL // N_DEV, D_MODEL)
    dh2 = _mm1("mm_dh2", dq2, W["q"], "nt", D_MODEL, F32)
    gb["wkv_x"] = _mm_tn("mm_dw_kv", sv["mn"], dkvb, BF16, shard_cols=2 * D_MODEL // N_DEV)
    dmn = _mm1("mm_dmn", dkvb, W["kv"], "nt3", D_MODEL, F32)
    _, gs["g_mem"] = _norm_bwd("norm_bwd_mem", dmn, mem, sp["g_mem"], None, BF16)
    dx1, gs["g_x_pre"] = _norm_bwd("norm_bwd_r", dh2, sv["x1"], sp["g_x_pre"], dx2, F32)
    dmix, gs["g_mix_post"] = _norm_bwd("norm_bwd_b", dx1, sv["mix"], sp["g_mix_post"], None, BF16)
    doh, dpool = _mm_rows("mm_dcat", [(dmix, W["out"], "nt")],
                          [(BF16, 0, FOX_WIDTH, "id"), (F32, FOX_WIDTH, POOL_WIDTH, "id")])
    gb["w_out"] = _mm_tn("mm_dw_sq", sv["cat"], dmix, BF16).reshape(N_DEV, D_MODEL // N_DEV, D_MODEL)
    du, gs["pool_w"], gs["pool_scale"] = _pool_bwd(sv["uf"], dpool, sp["pool_w"], sp["pool_scale"])
    dq, dk, dv, dcT, *got = _fox_bwd(sv["qkv"], sv["cT"], sv["o"], sv["lse"], doh, [gb[n] for n in REST] + carried)
    dc = jnp.pad(jnp.transpose(dcT.reshape(FOX_HEADS, S)), ((0, 0), (0, LANES - FOX_HEADS)))
    dfg, db = _gate_bwd(dc, sv["uf"], sp["b_forget"])
    gs["b_forget"] = db[:, :FOX_HEADS]
    dproj = jnp.concatenate([dq, dk, dv, du, dfg], axis=1)
    dwp = _mm_tn("mm_dw_in", sv["h1"], dproj, BF16, piece=UF_COLS)
    dh1 = _mm1("mm_dh1", dproj, W["inp"], "nt", D_MODEL, F32)
    dx0, gs["g_mix_pre"] = _norm_bwd("norm_bwd_r", dh1, sv["x0"], sp["g_mix_pre"], dx1, F32)
    return dx0, dict(zip(REST, got[:6])), got[6:], _pack_dw_in(dwp), gs


SMALL_ROWS = 2392


def _pack_small(d):
    flat = jnp.concatenate([d[n].reshape(-1) for n in SMALL])
    return jnp.pad(flat, (0, SMALL_ROWS * LANES - flat.shape[0])).reshape(SMALL_ROWS, LANES)


def _unpack_small(packed, like):
    flat = packed.reshape(-1)
    out = {}
    off = 0
    for n in SMALL:
        size = math.prod(like[n].shape)
        out[n] = flat[off:off + size].reshape(like[n].shape)
        off += size
    return out


def kernel(x, mem, g_mix_pre, w_in, b_forget, pool_w, pool_scale, w_out, g_mix_post, g_x_pre, g_mem, wq_x, wkv_x, wo_x, g_x_post, g_ffn_pre, w_up, w_down, g_ffn_post, loss_target, m_g_mix_pre, m_w_in, m_b_forget, m_pool_w, m_pool_scale, m_w_out, m_g_mix_post, m_g_x_pre, m_g_mem, m_wq_x, m_wkv_x, m_wo_x, m_g_x_post, m_g_ffn_pre, m_w_up, m_w_down, m_g_ffn_post, v_g_mix_pre, v_w_in, v_b_forget, v_pool_w, v_pool_scale, v_w_out, v_g_mix_post, v_g_x_pre, v_g_mem, v_wq_x, v_wkv_x, v_wo_x, v_g_x_post, v_g_ffn_pre, v_w_up, v_w_down, v_g_ffn_post):
    w = dict(g_mix_pre=g_mix_pre, w_in=w_in, b_forget=b_forget, pool_w=pool_w, pool_scale=pool_scale, w_out=w_out,
             g_mix_post=g_mix_post, g_x_pre=g_x_pre, g_mem=g_mem, wq_x=wq_x, wkv_x=wkv_x, wo_x=wo_x,
             g_x_post=g_x_post, g_ffn_pre=g_ffn_pre, w_up=w_up, w_down=w_down, g_ffn_post=g_ffn_post)
    mom = dict(g_mix_pre=m_g_mix_pre, w_in=m_w_in, b_forget=m_b_forget, pool_w=m_pool_w, pool_scale=m_pool_scale,
               w_out=m_w_out, g_mix_post=m_g_mix_post, g_x_pre=m_g_x_pre, g_mem=m_g_mem, wq_x=m_wq_x,
               wkv_x=m_wkv_x, wo_x=m_wo_x, g_x_post=m_g_x_post, g_ffn_pre=m_g_ffn_pre, w_up=m_w_up,
               w_down=m_w_down, g_ffn_post=m_g_ffn_post)
    var = dict(g_mix_pre=v_g_mix_pre, w_in=v_w_in, b_forget=v_b_forget, pool_w=v_pool_w, pool_scale=v_pool_scale,
               w_out=v_w_out, g_mix_post=v_g_mix_post, g_x_pre=v_g_x_pre, g_mem=v_g_mem, wq_x=v_wq_x,
               wkv_x=v_wkv_x, wo_x=v_wo_x, g_x_post=v_g_x_post, g_ffn_pre=v_g_ffn_pre, w_up=v_w_up,
               w_down=v_w_down, g_ffn_post=v_g_ffn_post)
    S = x.shape[1]
    xs = x.reshape(S, D_MODEL)
    mems = mem.reshape(MEM_LEN, D_MODEL)
    target = loss_target.reshape(S, D_MODEL)

    def small_params(l):
        return dict(
            g_mix_pre=_vec(g_mix_pre[l]), g_mix_post=_vec(g_mix_post[l]), g_x_pre=_vec(g_x_pre[l]),
            g_mem=_vec(g_mem[l]), g_x_post=_vec(g_x_post[l]), g_ffn_pre=_vec(g_ffn_pre[l]),
            g_ffn_post=_vec(g_ffn_post[l]), pool_scale=_vec(pool_scale[l]), pool_w=pool_w[l].astype(BF16),
            b_forget=jnp.pad(_vec(b_forget[l]), ((0, 0), (0, LANES - FOX_HEADS))))

    shard = {n: [w[n][l].astype(BF16) for l in range(DEPTH)] for n in REST}
    shard["w_in"] = [_w_in_travel(w_in[l].astype(BF16)) for l in range(DEPTH)]
    sps = [small_params(l) for l in range(DEPTH)]
    saved, weights = [], []
    h = xs
    (g_in,) = _exchange("gather_w_in", [shard["w_in"][0]])
    for l in range(DEPTH):
        travelling = [shard[n][l] for n in REST] + ([shard["w_in"][l + 1]] if l + 1 < DEPTH else [])
        h, sv, W, g_in = _layer_fwd(h, mems, sps[l], g_in, travelling)
        saved.append(sv)
        weights.append(W)
    dh, sq = _loss_fwd_bwd(h, target)
    loss = lax.psum(0.5 * sq[0, 0] / D_MODEL, ("x", "y", "c"))

    parts = [dict() for _ in range(DEPTH)]
    small_grads = [None] * DEPTH
    carried = []
    for l in reversed(range(DEPTH)):
        dh, got, got_carried, dw_in, gs = _layer_bwd(dh, mems, saved[l], sps[l], weights[l], carried)
        parts[l].update(got)
        if got_carried:
            parts[l + 1]["w_in"] = got_carried[0]
        carried = [dw_in]
        small_grads[l] = gs
    (parts[0]["w_in"],) = _exchange("scatter_dw_in", carried)
    grad_x = dh.reshape(1, S, D_MODEL)

    grads, deltas, new_m, new_v = {}, {}, {}, {}
    rows = dict(w_in=128, w_out=128, wq_x=128, wkv_x=256, wo_x=128, w_up=256, w_down=128)
    for l in range(DEPTH):
        parts[l]["w_in"] = jnp.swapaxes(parts[l]["w_in"][:, :W_IN_SHARD, :], 1, 2)
    for n in BIG:
        grads[n], deltas[n], new_m[n], new_v[n] = _adamw_big(
            "adamw_" + n, w[n], mom[n], var[n], [parts[l][n] for l in range(DEPTH)], rows[n])

    sg = {n: jnp.stack([small_grads[l][n].reshape(w[n].shape[1:]) for l in range(DEPTH)]) for n in SMALL}
    (sg_parts,) = _exchange("gather_small_grads", [_pack_small(sg)])
    outs = _adamw_small(_pack_small(w), _pack_small(mom), _pack_small(var), sg_parts)
    for d, packed in zip((grads, deltas, new_m, new_v), outs):
        d.update(_unpack_small(packed, w))

    return (loss, grad_x, *[grads[n] for n in W_NAMES], *[deltas[n] for n in W_NAMES],
            *[new_m[n] for n in W_NAMES], *[new_v[n] for n in W_NAMES])
```

```python
import math

import jax
import jax.numpy as jnp
from jax import lax
from jax.experimental import pallas as pl
from jax.experimental.pallas import tpu as pltpu

F32 = jnp.float32
BF16 = jnp.bfloat16

D_MODEL = 1024
DEPTH = 4
FOX_WIDTH = 512
FOX_HEADS = 8
FOX_HEAD_DIM = 64
POOL_WIDTH = 512
POOL_WINDOWS = (2, 4, 8, 16)
POOL_GROUP_DIM = 128
POOL_HALO = 16
MEM_LEN = 256
X_HEADS = 4
X_HEAD_DIM = 256
D_FF = 4096
EPS = 1e-6
IN_COLS = 2056
QKV_COLS = 3 * FOX_WIDTH
UF_COLS = 640
INP_COLS = QKV_COLS + UF_COLS
N_DEV = 8
LANES = 128

ADAM_LR = 0.001
ADAM_B1 = 0.9
ADAM_B2 = 0.999
ADAM_EPS = 1e-08
ADAM_WD = 0.01
ADAM_STEP = 10

VMEM_LIMIT = 56 * 1024 * 1024

W_NAMES = ['g_mix_pre', 'w_in', 'b_forget', 'pool_w', 'pool_scale', 'w_out', 'g_mix_post', 'g_x_pre', 'g_mem',
           'wq_x', 'wkv_x', 'wo_x', 'g_x_post', 'g_ffn_pre', 'w_up', 'w_down', 'g_ffn_post']
BIG = ['w_in', 'w_out', 'wq_x', 'wkv_x', 'wo_x', 'w_up', 'w_down']
SMALL = [n for n in W_NAMES if n not in BIG]

NN = (((1,), (0,)), ((), ()))
NT = (((1,), (1,)), ((), ()))
TN = (((0,), (0,)), ((), ()))


def _params(*sem):
    return pltpu.CompilerParams(dimension_semantics=sem, vmem_limit_bytes=VMEM_LIMIT)


def _row_tile(s):
    return min(s, 512)


def _mm_rows(name, terms, outs, extra=None, piece=1024):
    M = terms[0][0].shape[0]
    tm = _row_tile(M)
    nterm = len(terms)
    n_extra = 0 if extra is None else 1
    groups = {}
    for idx, (_, c0, width, fn) in enumerate(outs):
        groups.setdefault((c0, width), []).append((idx, fn))

    def product(a_ref, w_ref, w, kind, c0, pw):
        cols = slice(c0, c0 + pw)
        if kind == "nn":
            return lax.dot_general(a_ref[...], w_ref[:, cols], NN, preferred_element_type=F32)
        if kind == "nt":
            return lax.dot_general(a_ref[...], w_ref[cols, :], NT, preferred_element_type=F32)
        n = w.shape[2]
        if kind == "nn3":
            assert pw == n and c0 % n == 0
            return lax.dot_general(a_ref[...], w_ref[c0 // n], NN, preferred_element_type=F32)
        r = None
        for j in range(w.shape[0]):
            part = lax.dot_general(a_ref[:, j * n:(j + 1) * n], w_ref[j, cols, :], NT, preferred_element_type=F32)
            r = part if r is None else r + part
        return r

    def body(*refs):
        a_refs = refs[0:2 * nterm:2]
        w_refs = refs[1:2 * nterm:2]
        extra_refs = refs[2 * nterm:2 * nterm + n_extra]
        out_refs = refs[2 * nterm + n_extra:]
        for (g0, gw), members in groups.items():
            for c0 in range(g0, g0 + gw, piece):
                pw = min(piece, g0 + gw - c0)
                r = None
                for a_ref, w_ref, (_, w, kind) in zip(a_refs, w_refs, terms):
                    part = product(a_ref, w_ref, w, kind, c0, pw)
                    r = part if r is None else r + part
                dst = slice(c0 - g0, c0 - g0 + pw)
                for idx, fn in members:
                    if fn == "relu2":
                        rp = jnp.maximum(r, 0.0)
                        val = rp * rp
                    elif fn == "drelu2":
                        val = r * (2.0 * jnp.maximum(extra_refs[0][:, dst].astype(F32), 0.0))
                    else:
                        val = r
                    out_refs[idx][:, dst] = val.astype(out_refs[idx].dtype)

    in_specs, ins = [], []
    for a, w, _ in terms:
        in_specs.append(pl.BlockSpec((tm, a.shape[1]), lambda i: (i, 0)))
        in_specs.append(pl.BlockSpec(w.shape, lambda i, nd=w.ndim: (0,) * nd))
        ins += [a, w]
    if extra is not None:
        in_specs.append(pl.BlockSpec((tm, extra.shape[1]), lambda i: (i, 0)))
        ins.append(extra)
    res = pl.pallas_call(
        body, name=name, grid=(M // tm,), in_specs=in_specs,
        out_specs=[pl.BlockSpec((tm, width), lambda i: (i, 0)) for _, _, width, _ in outs],
        out_shape=[jax.ShapeDtypeStruct((M, width), dt) for dt, _, width, _ in outs],
        compiler_params=_params("parallel"))(*ins)
    return res


def _mm1(name, a, w, kind, n_cols, dtype, piece=1024):
    return _mm_rows(name, [(a, w, kind)], [(dtype, 0, n_cols, "id")], piece=piece)[0]


def _mm_tn(name, a, b, out_dtype, shard_cols=None, piece=512):
    K, M = a.shape
    N = b.shape[1]
    tk = _row_tile(K)
    nk = K // tk
    piece = shard_cols or min(piece, N)

    def body(a_ref, b_ref, o_ref, acc, a_t):
        k = pl.program_id(0)

        @pl.when(k == 0)
        def _():
            acc[...] = jnp.zeros_like(acc)

        a_t[...] = jnp.transpose(a_ref[...])
        for c0 in range(0, N, piece):
            cols = slice(c0, min(c0 + piece, N))
            acc[:, cols] += lax.dot_general(a_t[...], b_ref[:, cols], NN, preferred_element_type=F32)

        @pl.when(k == nk - 1)
        def _():
            for c0 in range(0, N, piece):
                cols = slice(c0, min(c0 + piece, N))
                if shard_cols:
                    o_ref[c0 // piece] = acc[:, cols].astype(o_ref.dtype)
                else:
                    o_ref[:, cols] = acc[:, cols].astype(o_ref.dtype)

    out_dims = (N // shard_cols, M, shard_cols) if shard_cols else (M, N)
    return pl.pallas_call(
        body, name=name, grid=(nk,),
        in_specs=[pl.BlockSpec((tk, M), lambda k: (k, 0)), pl.BlockSpec((tk, N), lambda k: (k, 0))],
        out_specs=pl.BlockSpec(out_dims, lambda k, nd=len(out_dims): (0,) * nd),
        out_shape=jax.ShapeDtypeStruct(out_dims, out_dtype),
        scratch_shapes=[pltpu.VMEM((M, N), F32), pltpu.VMEM((M, tk), a.dtype)],
        compiler_params=_params("arbitrary"))(a, b)


def _norm_fwd(name, x, g):
    S, Dm = x.shape
    ts = _row_tile(S)

    def body(x_ref, g_ref, h_ref):
        xv = x_ref[...]
        r = lax.rsqrt(jnp.mean(xv * xv, axis=-1, keepdims=True) + EPS)
        h_ref[...] = ((xv * r) * g_ref[...]).astype(BF16)

    return pl.pallas_call(
        body, name=name, grid=(S // ts,),
        in_specs=[pl.BlockSpec((ts, Dm), lambda i: (i, 0)), pl.BlockSpec((1, Dm), lambda i: (0, 0))],
        out_specs=pl.BlockSpec((ts, Dm), lambda i: (i, 0)),
        out_shape=jax.ShapeDtypeStruct((S, Dm), BF16), compiler_params=_params("parallel"))(x, g)


def _resid_norm_fwd(name, x, f, g, g_next):
    S, Dm = x.shape
    ts = _row_tile(S)
    has_next = g_next is not None

    def body(x_ref, f_ref, g_ref, *rest):
        fv = f_ref[...]
        r = lax.rsqrt(jnp.mean(fv * fv, axis=-1, keepdims=True) + EPS)
        xn = x_ref[...] + (fv * r) * g_ref[...]
        if has_next:
            gn_ref, o_ref, h_ref = rest
            rn = lax.rsqrt(jnp.mean(xn * xn, axis=-1, keepdims=True) + EPS)
            h_ref[...] = ((xn * rn) * gn_ref[...]).astype(BF16)
        else:
            (o_ref,) = rest
        o_ref[...] = xn

    row = pl.BlockSpec((ts, Dm), lambda i: (i, 0))
    vec = pl.BlockSpec((1, Dm), lambda i: (0, 0))
    ins = [x, f, g] + ([g_next] if has_next else [])
    res = pl.pallas_call(
        body, name=name, grid=(S // ts,), in_specs=[row, row, vec] + ([vec] if has_next else []),
        out_specs=[row, row] if has_next else [row],
        out_shape=[jax.ShapeDtypeStruct((S, Dm), F32)] + ([jax.ShapeDtypeStruct((S, Dm), BF16)] if has_next else []),
        compiler_params=_params("parallel"))(*ins)
    return (res[0], res[1]) if has_next else (res[0], None)


def _rms_bwd(dov, yv, g):
    r = lax.rsqrt(jnp.mean(yv * yv, axis=-1, keepdims=True) + EPS)
    z = dov * g
    yr = yv * r
    return r * (z - yr * jnp.mean(yr * z, axis=-1, keepdims=True)), jnp.sum(dov * yr, axis=0, keepdims=True)


def _norm_bwd(name, dout, y, g, resid, out_dtype, below=None):
    S, Dm = y.shape
    ts = _row_tile(S)
    has_resid = resid is not None
    chained = below is not None

    def body(*refs):
        refs = list(refs)
        do_ref, y_ref, g_ref = refs[:3]
        pos = 3
        r_ref = refs[pos] if has_resid else None
        pos += has_resid
        if chained:
            f_ref, gf_ref = refs[pos:pos + 2]
            pos += 2
        dy_ref, dg_ref = refs[pos:pos + 2]
        i = pl.program_id(0)
        dy, dg = _rms_bwd(do_ref[...], y_ref[...], g_ref[...])
        if has_resid:
            dy = dy + r_ref[...]
        dy_ref[...] = dy.astype(out_dtype)

        @pl.when(i == 0)
        def _():
            for ref in refs[pos + 1::2]:
                ref[...] = jnp.zeros_like(ref)

        dg_ref[...] += dg
        if chained:
            df_ref, dgf_ref = refs[pos + 2:pos + 4]
            df, dgf = _rms_bwd(dy, f_ref[...], gf_ref[...])
            df_ref[...] = df.astype(BF16)
            dgf_ref[...] += dgf

    row = pl.BlockSpec((ts, Dm), lambda i: (i, 0))
    vec = pl.BlockSpec((1, Dm), lambda i: (0, 0))
    ins = [dout, y, g] + ([resid] if has_resid else []) + (list(below) if chained else [])
    specs = [row, row, vec] + ([row] if has_resid else []) + ([row, vec] if chained else [])
    vec_shape = jax.ShapeDtypeStruct((1, Dm), F32)
    return pl.pallas_call(
        body, name=name, grid=(S // ts,), in_specs=specs, out_specs=[row, vec] + ([row, vec] if chained else []),
        out_shape=[jax.ShapeDtypeStruct((S, Dm), out_dtype), vec_shape]
        + ([jax.ShapeDtypeStruct((S, Dm), BF16), vec_shape] if chained else []),
        compiler_params=_params("arbitrary"))(*ins)


def _loss_fwd_bwd(y, t):
    S, Dm = y.shape
    ts = _row_tile(S)

    def body(y_ref, t_ref, dy_ref, acc_ref):
        i = pl.program_id(0)
        e = y_ref[...] - t_ref[...]
        dy_ref[...] = e * (1.0 / Dm)

        @pl.when(i == 0)
        def _():
            acc_ref[...] = jnp.zeros_like(acc_ref)

        s = jnp.sum(jnp.sum(e * e, axis=1, keepdims=True), axis=0, keepdims=True)
        acc_ref[...] += s

    row = pl.BlockSpec((ts, Dm), lambda i: (i, 0))
    return pl.pallas_call(
        body, name="loss", grid=(S // ts,), in_specs=[row, row],
        out_specs=[row, pl.BlockSpec((8, LANES), lambda i: (0, 0))],
        out_shape=[jax.ShapeDtypeStruct((S, Dm), F32), jax.ShapeDtypeStruct((8, LANES), F32)],
        compiler_params=_params("arbitrary"))(y, t)


def _log_sigmoid(x):
    return jnp.minimum(x, 0.0) - jnp.log(1.0 + jnp.exp(-jnp.abs(x)))


def _gate_fwd(uf, bpad):
    S = uf.shape[0]
    T = _row_tile(S)

    def body(f_ref, b_ref, c_ref, carry):
        i = pl.program_id(0)

        @pl.when(i == 0)
        def _():
            carry[...] = jnp.zeros_like(carry)

        lf = _log_sigmoid(f_ref[...] + b_ref[...])
        r = lax.broadcasted_iota(jnp.int32, (T, T), 0)
        cidx = lax.broadcasted_iota(jnp.int32, (T, T), 1)
        tri = (cidx <= r).astype(F32)
        c = lax.dot_general(tri, lf, NN, precision=lax.Precision.HIGHEST, preferred_element_type=F32)
        c_ref[...] = c + carry[0:1, :]
        carry[...] = carry[...] + jnp.sum(lf, axis=0, keepdims=True)

    return pl.pallas_call(
        body, name="gate_fwd", grid=(S // T,),
        in_specs=[pl.BlockSpec((T, LANES), lambda i: (i, 4)), pl.BlockSpec((1, LANES), lambda i: (0, 0))],
        out_specs=pl.BlockSpec((T, LANES), lambda i: (i, 0)),
        out_shape=jax.ShapeDtypeStruct((S, LANES), F32),
        scratch_shapes=[pltpu.VMEM((8, LANES), F32)], compiler_params=_params("arbitrary"))(uf, bpad)


def _gate_bwd(dc, uf, bpad):
    S = uf.shape[0]
    T = _row_tile(S)
    nb = S // T

    def body(dc_ref, f_ref, b_ref, df_ref, db_ref, carry):
        i = pl.program_id(0)

        @pl.when(i == 0)
        def _():
            carry[...] = jnp.zeros_like(carry)
            db_ref[...] = jnp.zeros_like(db_ref)

        dcv = dc_ref[...]
        r = lax.broadcasted_iota(jnp.int32, (T, T), 0)
        cidx = lax.broadcasted_iota(jnp.int32, (T, T), 1)
        tri = (cidx >= r).astype(F32)
        dlf = lax.dot_general(tri, dcv, NN, precision=lax.Precision.HIGHEST, preferred_element_type=F32)
        dlf = dlf + carry[0:1, :]
        carry[...] = carry[...] + jnp.sum(dcv, axis=0, keepdims=True)
        fg = f_ref[...] + b_ref[...]
        dfg = dlf / (1.0 + jnp.exp(fg))
        df_ref[...] = dfg.astype(BF16)
        db_ref[...] += jnp.sum(dfg, axis=0, keepdims=True)

    return pl.pallas_call(
        body, name="gate_bwd", grid=(nb,),
        in_specs=[pl.BlockSpec((T, LANES), lambda i: (nb - 1 - i, 0)),
                  pl.BlockSpec((T, LANES), lambda i: (nb - 1 - i, 4)),
                  pl.BlockSpec((1, LANES), lambda i: (0, 0))],
        out_specs=[pl.BlockSpec((T, LANES), lambda i: (nb - 1 - i, 0)), pl.BlockSpec((1, LANES), lambda i: (0, 0))],
        out_shape=[jax.ShapeDtypeStruct((S, LANES), BF16), jax.ShapeDtypeStruct((1, LANES), F32)],
        scratch_shapes=[pltpu.VMEM((8, LANES), F32)], compiler_params=_params("arbitrary"))(dc, uf, bpad)


FOX_CHUNK = 32
FOX_CHUNK_BWD = 64
HEAD_PAIRS = FOX_HEADS // 2
PAIR = 2


def _masked(s, row0, col0, diagonal):
    if diagonal:
        row = row0 + lax.broadcasted_iota(jnp.int32, s.shape, 0)
        col = col0 + lax.broadcasted_iota(jnp.int32, s.shape, 1)
        s = jnp.where(col <= row, s, -jnp.inf)
    return s


def _causal_pairs(n, query_major):
    if query_major:
        pairs = [(q, k) for q in range(n) for k in range(q + 1)]
    else:
        pairs = [(q, k) for k in range(n) for q in range(k, n)]
    return (jnp.asarray([p[0] for p in pairs], jnp.int32), jnp.asarray([p[1] for p in pairs], jnp.int32))


def _lane_block(b):
    return slice(b * LANES, (b + 1) * LANES)


def _fold(op, xs):
    acc = xs[0]
    for x in xs[1:]:
        acc = op(acc, x)
    return acc


def _head_lanes(hh):
    lane = lax.broadcasted_iota(jnp.int32, (1, LANES), 1)
    return (lane < FOX_HEAD_DIM) if hh == 0 else (lane >= FOX_HEAD_DIM)


def _pick(first_head, a, b):
    return jnp.where(first_head, a, b)


def _fox_fwd(qkv, cT, comm):
    S = qkv.shape[0]
    t = _row_tile(S)
    n = S // t
    nc = len(comm)
    scale = 1.0 / math.sqrt(FOX_HEAD_DIM)
    chunk = min(FOX_CHUNK, t)
    per_head = 7
    q_tab, k_tab = _causal_pairs(n, True)
    steps = q_tab.shape[0]

    def body(qt_ref, kt_ref, q_ref, k_ref, v_ref, c_ref, *rest):
        comm_in = rest[:nc]
        o_ref, ob_ref, lse_ref = rest[nc:nc + 3]
        comm_out = rest[nc + 3:2 * nc + 3]
        scr = rest[2 * nc + 3:2 * nc + 3 + PAIR * per_head]
        sems = rest[2 * nc + 3 + PAIR * per_head:]
        hp = pl.program_id(0)
        step_id = pl.program_id(1)
        qi = qt_ref[step_id]
        ki = kt_ref[step_id]

        if nc:
            @pl.when((hp == 0) & (step_id == 0))
            def _():
                _Gather(comm_in, comm_out, *sems).start()

            @pl.when((hp == HEAD_PAIRS - 1) & (step_id == 0))
            def _():
                _Gather(comm_in, comm_out, *sems).pass_on()

        @pl.when(ki == 0)
        def _():
            for hh in range(PAIR):
                m_s, l_s, a_s, acc_s = scr[hh * per_head:hh * per_head + 4]
                m_s[...] = jnp.full_like(m_s, -jnp.inf)
                l_s[...] = jnp.zeros_like(l_s)
                acc_s[...] = jnp.zeros_like(acc_s)

        def step(diagonal):
            q2 = q_ref[...] * scale
            k2 = k_ref[...]
            v2 = v_ref[...]
            for hh in range(PAIR):
                m_s, l_s, a_s, acc_s, s_s, ph_s, pl_s = scr[hh * per_head:(hh + 1) * per_head]
                qm = jnp.where(_head_lanes(hh), q2, jnp.zeros_like(q2))
                s_s[...] = lax.dot_general(qm, k2, NT, preferred_element_type=F32)
                for r in range(t // chunk):
                    rows = slice(r * chunk, (r + 1) * chunk)
                    blocks = [_masked(s_s[rows, _lane_block(b)] - c_ref[hh, :, _lane_block(b)], r * chunk,
                                      b * LANES, diagonal) for b in range(t // LANES)]
                    m_prev = m_s[rows, :]
                    m_new = jnp.maximum(m_prev, jnp.max(_fold(jnp.maximum, blocks), axis=1, keepdims=True))
                    alpha = jnp.exp(m_prev - m_new)
                    ps = [jnp.exp(blk - m_new) for blk in blocks]
                    l_s[rows, :] = alpha * l_s[rows, :] + jnp.sum(_fold(jnp.add, ps), axis=1, keepdims=True)
                    m_s[rows, :] = m_new
                    a_s[rows, :] = alpha
                    for b, p in enumerate(ps):
                        p_hi = p.astype(BF16)
                        ph_s[rows, _lane_block(b)] = p_hi
                        pl_s[rows, _lane_block(b)] = (p - p_hi.astype(F32)).astype(BF16)
                pv = (lax.dot_general(ph_s[...], v2, NN, preferred_element_type=F32)
                      + lax.dot_general(pl_s[...], v2, NN, preferred_element_type=F32))
                acc_s[...] = a_s[...] * acc_s[...] + pv

        @pl.when(ki < qi)
        def _():
            step(False)

        @pl.when(ki == qi)
        def _():
            step(True)
            heads = []
            for hh in range(PAIR):
                m_s, l_s, a_s, acc_s = scr[hh * per_head:hh * per_head + 4]
                heads.append(acc_s[...] / l_s[...])
                lse_ref[hh] = m_s[...] + jnp.log(l_s[...])
            o2 = _pick(_head_lanes(0), heads[0], heads[1])
            o_ref[...] = o2
            ob_ref[...] = o2.astype(BF16)

        if nc:
            @pl.when((hp == HEAD_PAIRS - 1) & (step_id == steps - 1))
            def _():
                _Gather(comm_in, comm_out, *sems).finish()

    def q_cols(first_block):
        return pl.BlockSpec((t, LANES), lambda h, s, qt, kt: (qt[s], first_block + h))

    def k_cols(first_block):
        return pl.BlockSpec((t, LANES), lambda h, s, qt, kt: (kt[s], first_block + h))

    any_spec = pl.BlockSpec(memory_space=pl.ANY)
    head_scratch = [pltpu.VMEM((t, LANES), F32), pltpu.VMEM((t, LANES), F32), pltpu.VMEM((t, LANES), F32),
                    pltpu.VMEM((t, LANES), F32), pltpu.VMEM((t, t), F32), pltpu.VMEM((t, t), BF16),
                    pltpu.VMEM((t, t), BF16)]
    grid_spec = pltpu.PrefetchScalarGridSpec(
        num_scalar_prefetch=2, grid=(HEAD_PAIRS, steps),
        in_specs=[q_cols(0), k_cols(HEAD_PAIRS), k_cols(2 * HEAD_PAIRS),
                  pl.BlockSpec((PAIR, 1, t), lambda h, s, qt, kt: (h, 0, kt[s]))] + [any_spec] * nc,
        out_specs=[q_cols(0), q_cols(0),
                   pl.BlockSpec((PAIR, t, LANES), lambda h, s, qt, kt: (h, qt[s], 0))] + [any_spec] * nc,
        scratch_shapes=head_scratch * PAIR + _comm_scratch(nc))
    return pl.pallas_call(
        body, name="fox_fwd", grid_spec=grid_spec,
        out_shape=[jax.ShapeDtypeStruct((S, FOX_WIDTH), F32), jax.ShapeDtypeStruct((S, FOX_WIDTH), BF16),
                   jax.ShapeDtypeStruct((FOX_HEADS, S, LANES), F32)] + _comm_shapes(comm),
        compiler_params=_params("arbitrary", "arbitrary"))(q_tab, k_tab, qkv, qkv, qkv, cT, *comm)


def _fox_bwd(qkv, cT, o, lse, do, comm):
    S = qkv.shape[0]
    t = _row_tile(S)
    n = S // t
    nc = len(comm)
    scale = 1.0 / math.sqrt(FOX_HEAD_DIM)
    chunk = min(FOX_CHUNK_BWD, t)
    per_head = 6
    q_tab, k_tab = _causal_pairs(n, False)
    steps = q_tab.shape[0]

    def body(qt_ref, kt_ref, q_ref, k_ref, v_ref, c_ref, o_ref, do_ref, lse_ref, *rest):
        comm_in = rest[:nc]
        dq_ref, dk_ref, dv_ref, dc_ref = rest[nc:nc + 4]
        comm_out = rest[nc + 4:2 * nc + 4]
        dq_s, dk_s, dv_s = rest[2 * nc + 4:2 * nc + 7]
        scr = rest[2 * nc + 7:2 * nc + 7 + PAIR * per_head]
        sems = rest[2 * nc + 7 + PAIR * per_head:]
        hp = pl.program_id(0)
        step_id = pl.program_id(1)
        qi = qt_ref[step_id]
        ki = kt_ref[step_id]

        if nc:
            @pl.when((hp == 0) & (step_id == 0))
            def _():
                for cp in _comm_copies(comm_in, comm_out, *sems):
                    cp.start()

        @pl.when(step_id == 0)
        def _():
            dq_s[...] = jnp.zeros_like(dq_s)

        @pl.when(qi == ki)
        def _():
            dk_s[...] = jnp.zeros_like(dk_s)
            dv_s[...] = jnp.zeros_like(dv_s)
            for hh in range(PAIR):
                dc_s = scr[hh * per_head]
                dc_s[...] = jnp.zeros_like(dc_s)

        def step(diagonal):
            q2 = q_ref[...]
            k2 = k_ref[...]
            v2 = v_ref[...]
            do2 = do_ref[...]
            prod = do2.astype(F32) * o_ref[...]
            grads = []
            for hh in range(PAIR):
                dc_s, delta_s, s_s, dp_s, p_s, ds_s = scr[hh * per_head:(hh + 1) * per_head]
                mine = _head_lanes(hh)
                s_s[...] = lax.dot_general(jnp.where(mine, q2 * scale, jnp.zeros_like(q2)), k2, NT,
                                           preferred_element_type=F32)
                dp_s[...] = lax.dot_general(jnp.where(mine, do2, jnp.zeros_like(do2)), v2, NT,
                                            preferred_element_type=F32)
                delta_s[...] = jnp.broadcast_to(jnp.sum(jnp.where(mine, prod, 0.0), axis=1, keepdims=True),
                                                (t, LANES))
                dc8 = [jnp.zeros((8, LANES), F32) for _ in range(t // LANES)]
                for r in range(t // chunk):
                    rows = slice(r * chunk, (r + 1) * chunk)
                    lse = lse_ref[hh, rows, :]
                    delta = delta_s[rows, :]
                    for b in range(t // LANES):
                        s = _masked(s_s[rows, _lane_block(b)] - c_ref[hh, :, _lane_block(b)], r * chunk, b * LANES,
                                    diagonal)
                        p = jnp.exp(s - lse)
                        ds = p * (dp_s[rows, _lane_block(b)] - delta)
                        p_s[rows, _lane_block(b)] = p.astype(BF16)
                        ds_s[rows, _lane_block(b)] = ds.astype(BF16)
                        dc8[b] = dc8[b] + jnp.sum(ds.reshape(chunk // 8, 8, LANES), axis=0)
                for b in range(t // LANES):
                    dc_s[:, _lane_block(b)] += jnp.sum(dc8[b], axis=0, keepdims=True)
                dsb = ds_s[...]
                grads.append((lax.dot_general(p_s[...], do2, TN, preferred_element_type=F32),
                              lax.dot_general(dsb, k2, NN, preferred_element_type=F32),
                              lax.dot_general(dsb, q2, TN, preferred_element_type=F32)))
            first = _head_lanes(0)
            dv_s[...] += _pick(first, grads[0][0], grads[1][0])
            q_rows = pl.ds(pl.multiple_of(qi * t, t), t)
            dq_s[q_rows, :] += _pick(first, grads[0][1], grads[1][1]) * scale
            dk_s[...] += _pick(first, grads[0][2], grads[1][2]) * scale

        @pl.when(qi > ki)
        def _():
            step(False)

        @pl.when(qi == ki)
        def _():
            step(True)

        @pl.when(qi == n - 1)
        def _():
            dk_ref[...] = dk_s[...].astype(BF16)
            dv_ref[...] = dv_s[...].astype(BF16)
            for hh in range(PAIR):
                dc_ref[hh] = -scr[hh * per_head][...]

        @pl.when(step_id == steps - 1)
        def _():
            dq_ref[...] = dq_s[...].astype(BF16)

        if nc:
            @pl.when((hp == HEAD_PAIRS - 1) & (step_id == steps - 1))
            def _():
                for cp in _comm_copies(comm_in, comm_out, *sems):
                    cp.wait()

    def q_side(first_block):
        return pl.BlockSpec((t, LANES), lambda h, s, qt, kt: (qt[s], first_block + h))

    def k_side(first_block):
        return pl.BlockSpec((t, LANES), lambda h, s, qt, kt: (kt[s], first_block + h))

    any_spec = pl.BlockSpec(memory_space=pl.ANY)
    head_scratch = [pltpu.VMEM((1, t), F32), pltpu.VMEM((t, LANES), F32),
                    pltpu.VMEM((t, t), F32), pltpu.VMEM((t, t), F32), pltpu.VMEM((t, t), BF16),
                    pltpu.VMEM((t, t), BF16)]
    grad_shape = jax.ShapeDtypeStruct((S, FOX_WIDTH), BF16)
    grid_spec = pltpu.PrefetchScalarGridSpec(
        num_scalar_prefetch=2, grid=(HEAD_PAIRS, steps),
        in_specs=[q_side(0), k_side(HEAD_PAIRS), k_side(2 * HEAD_PAIRS),
                  pl.BlockSpec((PAIR, 1, t), lambda h, s, qt, kt: (h, 0, kt[s])), q_side(0), q_side(0),
                  pl.BlockSpec((PAIR, t, LANES), lambda h, s, qt, kt: (h, qt[s], 0))] + [any_spec] * nc,
        out_specs=[pl.BlockSpec((S, LANES), lambda h, s, qt, kt: (0, h)), k_side(0), k_side(0),
                   pl.BlockSpec((PAIR, 1, t), lambda h, s, qt, kt: (h, 0, kt[s]))] + [any_spec] * nc,
        scratch_shapes=[pltpu.VMEM((S, LANES), F32), pltpu.VMEM((t, LANES), F32), pltpu.VMEM((t, LANES), F32)]
        + head_scratch * PAIR + _comm_scratch(nc))
    return pl.pallas_call(
        body, name="fox_bwd", grid_spec=grid_spec,
        out_shape=[grad_shape, grad_shape, grad_shape, jax.ShapeDtypeStruct((FOX_HEADS, 1, S), F32)]
        + _comm_shapes(comm),
        compiler_params=_params("arbitrary", "arbitrary"))(q_tab, k_tab, qkv, qkv, qkv, cT, o, do, lse, *comm)


def _lanes(g):
    return slice(g * POOL_GROUP_DIM, (g + 1) * POOL_GROUP_DIM)


def _window_sum(e, win, back):
    rows = e.shape[0]
    s = e
    sh = 1
    while sh < win:
        s = s + pltpu.roll(s, sh if back else rows - sh, 0)
        sh *= 2
    return s


def _pooled(u_ref, up_ref, i, g, win, T):
    cur = u_ref[:, _lanes(g)]
    tail = jnp.where(i > 0, up_ref[T - POOL_HALO:T, _lanes(g)], 0.0)
    e = jnp.concatenate([tail, cur], axis=0)
    s = _window_sum(e, win, True)
    t_idx = i * T - POOL_HALO + lax.broadcasted_iota(jnp.int32, (T + POOL_HALO, POOL_GROUP_DIM), 0)
    cnt = jnp.clip(t_idx + 1, 1, win).astype(F32)
    return (s / cnt - e)[POOL_HALO:, :]


def _pool_fwd(uf, pw, ps):
    S = uf.shape[0]
    T = _row_tile(S)

    def body(u_ref, up_ref, w_ref, sc_ref, o_ref):
        i = pl.program_id(0)
        for g, win in enumerate(POOL_WINDOWS):
            pb = _pooled(u_ref, up_ref, i, g, win, T).astype(BF16)
            yv = lax.dot_general(pb, w_ref[g], NN, preferred_element_type=F32)
            o_ref[:, _lanes(g)] = (yv * sc_ref[:, _lanes(g)]).astype(BF16)

    return pl.pallas_call(
        body, name="pool_fwd", grid=(S // T,),
        in_specs=[pl.BlockSpec((T, POOL_WIDTH), lambda i: (i, 0)),
                  pl.BlockSpec((T, POOL_WIDTH), lambda i: (jnp.maximum(i - 1, 0), 0)),
                  pl.BlockSpec((4, POOL_GROUP_DIM, POOL_GROUP_DIM), lambda i: (0, 0, 0)),
                  pl.BlockSpec((1, POOL_WIDTH), lambda i: (0, 0))],
        out_specs=pl.BlockSpec((T, POOL_WIDTH), lambda i: (i, 0)),
        out_shape=jax.ShapeDtypeStruct((S, POOL_WIDTH), BF16), compiler_params=_params("parallel"))(uf, uf, pw, ps)


def _pool_bwd(uf, dpool, pw, ps):
    S = uf.shape[0]
    T = _row_tile(S)
    nb = S // T

    def body(u_ref, up_ref, d_ref, dn_ref, w_ref, sc_ref, du_ref, dw_ref, dsc_ref):
        i = pl.program_id(0)

        @pl.when(i == 0)
        def _():
            dw_ref[...] = jnp.zeros_like(dw_ref)
            dsc_ref[...] = jnp.zeros_like(dsc_ref)

        t_idx = i * T + lax.broadcasted_iota(jnp.int32, (T + POOL_HALO, POOL_GROUP_DIM), 0)
        for g, win in enumerate(POOL_WINDOWS):
            pb = _pooled(u_ref, up_ref, i, g, win, T).astype(BF16)
            w = w_ref[g]
            sc = sc_ref[:, _lanes(g)]
            yv = lax.dot_general(pb, w, NN, preferred_element_type=F32)
            dov = d_ref[:, _lanes(g)]
            dsc_ref[:, _lanes(g)] += jnp.sum(dov * yv, axis=0, keepdims=True)
            head = jnp.where(i < nb - 1, dn_ref[0:POOL_HALO, _lanes(g)], 0.0)
            dyb = (jnp.concatenate([dov, head], axis=0) * sc).astype(BF16)
            dw_ref[g] += lax.dot_general(pb, dyb[:T], TN, preferred_element_type=F32)
            dpl = lax.dot_general(dyb, w, NT, preferred_element_type=F32)
            cnt = jnp.minimum(t_idx + 1, win).astype(F32)
            a = _window_sum(dpl / cnt, win, False)
            du_ref[:, _lanes(g)] = (a - dpl)[:T].astype(BF16)

    return pl.pallas_call(
        body, name="pool_bwd", grid=(nb,),
        in_specs=[pl.BlockSpec((T, POOL_WIDTH), lambda i: (i, 0)),
                  pl.BlockSpec((T, POOL_WIDTH), lambda i: (jnp.maximum(i - 1, 0), 0)),
                  pl.BlockSpec((T, POOL_WIDTH), lambda i: (i, 0)),
                  pl.BlockSpec((T, POOL_WIDTH), lambda i: (jnp.minimum(i + 1, nb - 1), 0)),
                  pl.BlockSpec((4, POOL_GROUP_DIM, POOL_GROUP_DIM), lambda i: (0, 0, 0)),
                  pl.BlockSpec((1, POOL_WIDTH), lambda i: (0, 0))],
        out_specs=[pl.BlockSpec((T, POOL_WIDTH), lambda i: (i, 0)),
                   pl.BlockSpec((4, POOL_GROUP_DIM, POOL_GROUP_DIM), lambda i: (0, 0, 0)),
                   pl.BlockSpec((1, POOL_WIDTH), lambda i: (0, 0))],
        out_shape=[jax.ShapeDtypeStruct((S, POOL_WIDTH), BF16),
                   jax.ShapeDtypeStruct((4, POOL_GROUP_DIM, POOL_GROUP_DIM), F32),
                   jax.ShapeDtypeStruct((1, POOL_WIDTH), F32)],
        compiler_params=_params("arbitrary"))(uf, uf, dpool, dpool, pw, ps)


def _xhead(h):
    return slice(h * X_HEAD_DIM, (h + 1) * X_HEAD_DIM)


def _xvhead(h):
    return slice(D_MODEL + h * X_HEAD_DIM, D_MODEL + (h + 1) * X_HEAD_DIM)


def _x_probs(qh, kh):
    s = lax.dot_general(qh, kh, NT, preferred_element_type=F32) * (1.0 / math.sqrt(X_HEAD_DIM))
    e = jnp.exp(s - jnp.max(s, axis=1, keepdims=True))
    return e / jnp.sum(e, axis=1, keepdims=True)


def _xattn_fwd(q, kv):
    S = q.shape[0]
    t = _row_tile(S)

    def body(q_ref, kv_ref, o_ref):
        for h in range(X_HEADS):
            p = _x_probs(q_ref[:, _xhead(h)], kv_ref[:, _xhead(h)])
            o_ref[:, _xhead(h)] = lax.dot_general(p.astype(BF16), kv_ref[:, _xvhead(h)], NN,
                                                  preferred_element_type=F32).astype(BF16)

    return pl.pallas_call(
        body, name="xattn_fwd", grid=(S // t,),
        in_specs=[pl.BlockSpec((t, D_MODEL), lambda i: (i, 0)), pl.BlockSpec((MEM_LEN, 2 * D_MODEL), lambda i: (0, 0))],
        out_specs=pl.BlockSpec((t, D_MODEL), lambda i: (i, 0)),
        out_shape=jax.ShapeDtypeStruct((S, D_MODEL), BF16), compiler_params=_params("parallel"))(q, kv)


def _xattn_bwd(q, kv, do):
    S = q.shape[0]
    t = _row_tile(S)
    nb = S // t
    scale = 1.0 / math.sqrt(X_HEAD_DIM)

    def body(q_ref, kv_ref, do_ref, dq_ref, dkv_ref, acc):
        i = pl.program_id(0)

        @pl.when(i == 0)
        def _():
            acc[...] = jnp.zeros_like(acc)

        for h in range(X_HEADS):
            qh = q_ref[:, _xhead(h)]
            kh = kv_ref[:, _xhead(h)]
            doh = do_ref[:, _xhead(h)]
            p = _x_probs(qh, kh)
            acc[:, _xvhead(h)] += lax.dot_general(p.astype(BF16), doh, TN, preferred_element_type=F32)
            dp = lax.dot_general(doh, kv_ref[:, _xvhead(h)], NT, preferred_element_type=F32)
            ds = p * (dp - jnp.sum(dp * p, axis=1, keepdims=True))
            dsb = ds.astype(BF16)
            dq_ref[:, _xhead(h)] = (lax.dot_general(dsb, kh, NN, preferred_element_type=F32) * scale).astype(BF16)
            acc[:, _xhead(h)] += lax.dot_general(dsb, qh, TN, preferred_element_type=F32) * scale

        @pl.when(i == nb - 1)
        def _():
            dkv_ref[...] = acc[...].astype(BF16)

    row = pl.BlockSpec((t, D_MODEL), lambda i: (i, 0))
    full = pl.BlockSpec((MEM_LEN, 2 * D_MODEL), lambda i: (0, 0))
    return pl.pallas_call(
        body, name="xattn_bwd", grid=(nb,), in_specs=[row, full, row], out_specs=[row, full],
        out_shape=[jax.ShapeDtypeStruct((S, D_MODEL), BF16), jax.ShapeDtypeStruct((MEM_LEN, 2 * D_MODEL), BF16)],
        scratch_shapes=[pltpu.VMEM((MEM_LEN, 2 * D_MODEL), F32)],
        compiler_params=_params("arbitrary"))(q, kv, do)


def _comm_shapes(arrs):
    return [jax.ShapeDtypeStruct((N_DEV,) + tuple(a.shape[-2:]), a.dtype) for a in arrs]


def _comm_scratch(n):
    if n == 0:
        return []
    return [pltpu.SemaphoreType.DMA((n, N_DEV - 1)), pltpu.SemaphoreType.DMA((n, N_DEV - 1)),
            pltpu.SemaphoreType.DMA((n,))]


def _comm_copies(ins, outs, send_sems, recv_sems, local_sems):
    x, y, c = lax.axis_index("x"), lax.axis_index("y"), lax.axis_index("c")
    me = 4 * x + 2 * y + c
    copies = []
    for w in range(len(ins)):
        src = ins[w] if len(ins[w].shape) == 2 else ins[w].at[me]
        copies.append(pltpu.make_async_copy(src, outs[w].at[me], local_sems.at[w]))
    for k in range(1, N_DEV):
        px = 1 - x if k & 4 else x
        py = 1 - y if k & 2 else y
        pc = 1 - c if k & 1 else c
        peer = 4 * px + 2 * py + pc
        for w in range(len(ins)):
            src = ins[w] if len(ins[w].shape) == 2 else ins[w].at[peer]
            copies.append(pltpu.make_async_remote_copy(
                src_ref=src, dst_ref=outs[w].at[me], send_sem=send_sems.at[w, k - 1],
                recv_sem=recv_sems.at[w, k - 1], device_id=(px, py, pc), device_id_type=pl.DeviceIdType.MESH))
    return copies


class _Gather:
    def __init__(self, ins, outs, send_sems, recv_sems, local_sems):
        x, y, c = lax.axis_index("x"), lax.axis_index("y"), lax.axis_index("c")
        me = 4 * x + 2 * y + c
        sibling = (x, y, 1 - c)
        self.local, self.mine, self.passed = [], [], []
        for w in range(len(ins)):
            def remote(idx, src, slot, dev, w=w):
                return pltpu.make_async_remote_copy(
                    src_ref=src, dst_ref=outs[w].at[slot], send_sem=send_sems.at[w, idx],
                    recv_sem=recv_sems.at[w, idx], device_id=dev, device_id_type=pl.DeviceIdType.MESH)

            self.local.append(pltpu.make_async_copy(ins[w], outs[w].at[me], local_sems.at[w]))
            mine, passed = [remote(0, ins[w], me, sibling)], []
            for j, (fx, fy) in enumerate(((0, 1), (1, 0), (1, 1))):
                px = 1 - x if fx else x
                py = 1 - y if fy else y
                slot = 4 * px + 2 * py + c
                mine.append(remote(1 + j, ins[w], me, (px, py, c)))
                passed.append(remote(4 + j, outs[w].at[slot], slot, sibling))
            self.mine.append(mine)
            self.passed.append(passed)

    def start(self):
        for cp in self.local:
            cp.start()
        for mine in self.mine:
            for cp in mine:
                cp.start()

    def pass_on(self):
        for mine, passed in zip(self.mine, self.passed):
            for j, cp in enumerate(passed):
                mine[1 + j].wait_recv()
                cp.start()

    def finish(self):
        for mine, passed in zip(self.mine, self.passed):
            mine[0].wait_recv()
            for cp in passed:
                cp.wait_recv()
            for cp in mine + passed:
                cp.wait_send()
        for cp in self.local:
            cp.wait()


def _exchange(name, arrs):
    n = len(arrs)
    gather = all(a.ndim == 2 for a in arrs)

    def body(*refs):
        if gather:
            g = _Gather(refs[:n], refs[n:2 * n], *refs[2 * n:])
            g.start()
            g.pass_on()
            g.finish()
            return
        copies = _comm_copies(refs[:n], refs[n:2 * n], *refs[2 * n:])
        for cp in copies:
            cp.start()
        for cp in copies:
            cp.wait()

    any_spec = pl.BlockSpec(memory_space=pl.ANY)
    return pl.pallas_call(
        body, name=name, in_specs=[any_spec] * n, out_specs=[any_spec] * n, out_shape=_comm_shapes(arrs),
        scratch_shapes=_comm_scratch(n))(*arrs)


def _adamw_math(w, g, m, v):
    m = ADAM_B1 * m + (1.0 - ADAM_B1) * g
    v = ADAM_B2 * v + (1.0 - ADAM_B2) * (g * g)
    m_hat = m / (1.0 - ADAM_B1 ** ADAM_STEP)
    v_hat = v / (1.0 - ADAM_B2 ** ADAM_STEP)
    delta = -ADAM_LR * (m_hat / (jnp.sqrt(v_hat) + ADAM_EPS) + ADAM_WD * w)
    return delta, m, v


def _sum_parts(p_ref):
    g = p_ref[0].astype(F32)
    for s in range(1, N_DEV):
        g = g + p_ref[s].astype(F32)
    return g


def _adamw_big(name, w, m, v, parts, tr):
    L, R, C = w.shape

    def body(w_ref, m_ref, v_ref, *rest):
        p_refs = rest[:L]
        g_ref, d_ref, nm_ref, nv_ref = rest[L:]
        layer = pl.program_id(0)
        for j in range(L):
            @pl.when(layer == j)
            def _(j=j):
                g = _sum_parts(p_refs[j])
                delta, nm, nv = _adamw_math(w_ref[...], g, m_ref[...], v_ref[...])
                g_ref[...] = g
                d_ref[...] = delta
                nm_ref[...] = nm
                nv_ref[...] = nv

    blk = pl.BlockSpec((None, tr, C), lambda l, i: (l, i, 0))

    def part_spec(j):
        return pl.BlockSpec((N_DEV, tr, C), lambda l, i: (0, jnp.where(l == j, i, 0), 0))

    shp = jax.ShapeDtypeStruct((L, R, C), F32)
    return pl.pallas_call(
        body, name=name, grid=(L, R // tr), in_specs=[blk, blk, blk] + [part_spec(j) for j in range(L)],
        out_specs=[blk] * 4, out_shape=[shp] * 4, compiler_params=_params("arbitrary", "arbitrary"))(w, m, v, *parts)


def _adamw_small(w, m, v, parts):
    R, C = w.shape

    def body(w_ref, m_ref, v_ref, p_ref, g_ref, d_ref, nm_ref, nv_ref):
        g = _sum_parts(p_ref)
        delta, nm, nv = _adamw_math(w_ref[...], g, m_ref[...], v_ref[...])
        g_ref[...] = g
        d_ref[...] = delta
        nm_ref[...] = nm
        nv_ref[...] = nv

    shp = jax.ShapeDtypeStruct((R, C), F32)
    return pl.pallas_call(body, name="adamw_small", out_shape=[shp] * 4,
                          compiler_params=pltpu.CompilerParams(vmem_limit_bytes=VMEM_LIMIT))(w, m, v, parts)


def _vec(a):
    return a.reshape(1, -1)


W_IN_SHARD = IN_COLS // N_DEV
W_IN_ROWS = 272


def _w_in_travel(a):
    pad = [(0, 0)] * (a.ndim - 2) + [(0, W_IN_ROWS - W_IN_SHARD), (0, 0)]
    return jnp.pad(jnp.swapaxes(a, -1, -2), pad)


def _unpack_w_in(g):
    full = jnp.transpose(g[:, :W_IN_SHARD, :], (2, 0, 1)).reshape(D_MODEL, IN_COLS)
    qkv = full[:, :QKV_COLS]
    f = full[:, QKV_COLS:QKV_COLS + FOX_HEADS]
    u = full[:, QKV_COLS + FOX_HEADS:]
    uf = jnp.concatenate([u, f, jnp.zeros((D_MODEL, UF_COLS - POOL_WIDTH - FOX_HEADS), g.dtype)], axis=1)
    return jnp.concatenate([qkv, uf], axis=1)


def _pack_dw_in(dwp):
    qkv = dwp[:, :QKV_COLS]
    u = dwp[:, QKV_COLS:QKV_COLS + POOL_WIDTH]
    f = dwp[:, QKV_COLS + POOL_WIDTH:QKV_COLS + POOL_WIDTH + FOX_HEADS]
    full = jnp.concatenate([qkv, f, u], axis=1)
    return _w_in_travel(jnp.transpose(full.reshape(D_MODEL, N_DEV, W_IN_SHARD), (1, 0, 2)))


REST = ['w_out', 'wq_x', 'wkv_x', 'wo_x', 'w_up', 'w_down']


def _layer_fwd(x0, h1, mem, sp, g_in, shards, g_next):
    S = x0.shape[0]
    sv = {"x0": x0}
    w_inp = _unpack_w_in(g_in)
    qkv, uf = _mm_rows("mm_in", [(h1, w_inp, "nn")],
                       [(BF16, 0, QKV_COLS, "id"), (F32, QKV_COLS, UF_COLS, "id")], piece=UF_COLS)
    c = _gate_fwd(uf, sp["b_forget"])
    cT = jnp.transpose(c[:, :FOX_HEADS]).reshape(FOX_HEADS, 1, S)
    o, ob, lse, *got = _fox_fwd(qkv, cT, shards)
    g_out, g_q, g_kv, g_o, g_up, g_down = got[:6]
    W = dict(inp=w_inp, out=g_out.reshape(D_MODEL, D_MODEL), q=g_q.reshape(D_MODEL, D_MODEL), kv=g_kv,
             o=g_o.reshape(D_MODEL, D_MODEL), up=g_up, down=g_down.reshape(D_FF, D_MODEL))
    pool = _pool_fwd(uf, sp["pool_w"], sp["pool_scale"])
    cat = jnp.concatenate([ob, pool], axis=1)
    mix = _mm1("mm_sq", cat, W["out"], "nn", D_MODEL, F32)
    x1, h2 = _resid_norm_fwd("resid_norm", x0, mix, sp["g_mix_post"], sp["g_x_pre"])
    mn = _norm_fwd("norm_mem", mem, sp["g_mem"])
    q2 = _mm1("mm_q", h2, W["q"], "nn", D_MODEL, BF16)
    kv = _mm1("mm_kv", mn, W["kv"], "nn3", 2 * D_MODEL, BF16, piece=2 * D_MODEL // N_DEV)
    o2 = _xattn_fwd(q2, kv)
    xo = _mm1("mm_sq", o2, W["o"], "nn", D_MODEL, F32)
    x2, h3 = _resid_norm_fwd("resid_norm", x1, xo, sp["g_x_post"], sp["g_ffn_pre"])
    up, act = _mm_rows("mm_up", [(h3, W["up"], "nn3")], [(BF16, 0, D_FF, "id"), (BF16, 0, D_FF, "relu2")],
                       piece=D_FF // N_DEV)
    y = _mm1("mm_down", act, W["down"], "nn", D_MODEL, F32)
    x3, h_next = _resid_norm_fwd("resid_norm" if g_next is not None else "resid_norm_last", x2, y, sp["g_ffn_post"],
                                 g_next)
    sv.update(h1=h1, uf=uf, cT=cT, qkv=qkv, o=o, lse=lse, cat=cat, mix=mix, x1=x1, h2=h2, mn=mn, q2=q2, kv=kv,
              o2=o2, xo=xo, x2=x2, h3=h3, up=up, act=act, y=y)
    return x3, h_next, sv, W, (got[6] if len(got) > 6 else None)


def _layer_bwd(dx3, dy, mem, sv, sp, W, carried, below):
    S = dx3.shape[0]
    gs = {}
    gb = {}
    (dup,) = _mm_rows("mm_dup", [(dy, W["down"], "nt")], [(BF16, 0, D_FF, "drelu2")], extra=sv["up"])
    gb["w_down"] = _mm_tn("mm_dw_down", sv["act"], dy, BF16).reshape(N_DEV, D_FF // N_DEV, D_MODEL)
    gb["w_up"] = _mm_tn("mm_dw_up", sv["h3"], dup, BF16, shard_cols=D_FF // N_DEV)
    dh3 = _mm1("mm_dh3", dup, W["up"], "nt3", D_MODEL, F32)
    dx2, gs["g_ffn_pre"], dxo, gs["g_x_post"] = _norm_bwd(
        "norm_bwd_c", dh3, sv["x2"], sp["g_ffn_pre"], dx3, F32, below=(sv["xo"], sp["g_x_post"]))
    do2 = _mm1("mm_sq_t", dxo, W["o"], "nt", D_MODEL, BF16)
    gb["wo_x"] = _mm_tn("mm_dw_sq", sv["o2"], dxo, BF16).reshape(N_DEV, D_MODEL // N_DEV, D_MODEL)
    dq2, dkvb = _xattn_bwd(sv["q2"], sv["kv"], do2)
    gb["wq_x"] = _mm_tn("mm_dw_sq", sv["h2"], dq2, BF16).reshape(N_DEV, D_MODEL // N_DEV, D_MODEL)
    dh2 = _mm1("mm_dh2", dq2, W["q"], "nt", D_MODEL, F32)
    gb["wkv_x"] = _mm_tn("mm_dw_kv", sv["mn"], dkvb, BF16, shard_cols=2 * D_MODEL // N_DEV)
    dmn = _mm1("mm_dmn", dkvb, W["kv"], "nt3", D_MODEL, F32)
    _, gs["g_mem"] = _norm_bwd("norm_bwd_mem", dmn, mem, sp["g_mem"], None, BF16)
    dx1, gs["g_x_pre"], dmix, gs["g_mix_post"] = _norm_bwd(
        "norm_bwd_c", dh2, sv["x1"], sp["g_x_pre"], dx2, F32, below=(sv["mix"], sp["g_mix_post"]))
    doh, dpool = _mm_rows("mm_dcat", [(dmix, W["out"], "nt")],
                          [(BF16, 0, FOX_WIDTH, "id"), (F32, FOX_WIDTH, POOL_WIDTH, "id")])
    gb["w_out"] = _mm_tn("mm_dw_sq", sv["cat"], dmix, BF16).reshape(N_DEV, D_MODEL // N_DEV, D_MODEL)
    du, gs["pool_w"], gs["pool_scale"] = _pool_bwd(sv["uf"], dpool, sp["pool_w"], sp["pool_scale"])
    dq, dk, dv, dcT, *got = _fox_bwd(sv["qkv"], sv["cT"], sv["o"], sv["lse"], doh, [gb[n] for n in REST] + carried)
    dc = jnp.pad(jnp.transpose(dcT.reshape(FOX_HEADS, S)), ((0, 0), (0, LANES - FOX_HEADS)))
    dfg, db = _gate_bwd(dc, sv["uf"], sp["b_forget"])
    gs["b_forget"] = db[:, :FOX_HEADS]
    dproj = jnp.concatenate([dq, dk, dv, du, dfg], axis=1)
    dwp = _mm_tn("mm_dw_in", sv["h1"], dproj, BF16, piece=UF_COLS)
    dh1 = _mm1("mm_dh1", dproj, W["inp"], "nt", D_MODEL, F32)
    if below is None:
        dx0, gs["g_mix_pre"] = _norm_bwd("norm_bwd_r", dh1, sv["x0"], sp["g_mix_pre"], dx1, F32)
        lower = None
    else:
        dx0, gs["g_mix_pre"], *lower = _norm_bwd("norm_bwd_c", dh1, sv["x0"], sp["g_mix_pre"], dx1, F32, below=below)
    return dx0, lower, dict(zip(REST, got[:6])), got[6:], _pack_dw_in(dwp), gs


SMALL_ROWS = 2392


def _pack_small(d):
    flat = jnp.concatenate([d[n].reshape(-1) for n in SMALL])
    return jnp.pad(flat, (0, SMALL_ROWS * LANES - flat.shape[0])).reshape(SMALL_ROWS, LANES)


def _unpack_small(packed, like):
    flat = packed.reshape(-1)
    out = {}
    off = 0
    for n in SMALL:
        size = math.prod(like[n].shape)
        out[n] = flat[off:off + size].reshape(like[n].shape)
        off += size
    return out


def kernel(x, mem, g_mix_pre, w_in, b_forget, pool_w, pool_scale, w_out, g_mix_post, g_x_pre, g_mem, wq_x, wkv_x, wo_x, g_x_post, g_ffn_pre, w_up, w_down, g_ffn_post, loss_target, m_g_mix_pre, m_w_in, m_b_forget, m_pool_w, m_pool_scale, m_w_out, m_g_mix_post, m_g_x_pre, m_g_mem, m_wq_x, m_wkv_x, m_wo_x, m_g_x_post, m_g_ffn_pre, m_w_up, m_w_down, m_g_ffn_post, v_g_mix_pre, v_w_in, v_b_forget, v_pool_w, v_pool_scale, v_w_out, v_g_mix_post, v_g_x_pre, v_g_mem, v_wq_x, v_wkv_x, v_wo_x, v_g_x_post, v_g_ffn_pre, v_w_up, v_w_down, v_g_ffn_post):
    w = dict(g_mix_pre=g_mix_pre, w_in=w_in, b_forget=b_forget, pool_w=pool_w, pool_scale=pool_scale, w_out=w_out,
             g_mix_post=g_mix_post, g_x_pre=g_x_pre, g_mem=g_mem, wq_x=wq_x, wkv_x=wkv_x, wo_x=wo_x,
             g_x_post=g_x_post, g_ffn_pre=g_ffn_pre, w_up=w_up, w_down=w_down, g_ffn_post=g_ffn_post)
    mom = dict(g_mix_pre=m_g_mix_pre, w_in=m_w_in, b_forget=m_b_forget, pool_w=m_pool_w, pool_scale=m_pool_scale,
               w_out=m_w_out, g_mix_post=m_g_mix_post, g_x_pre=m_g_x_pre, g_mem=m_g_mem, wq_x=m_wq_x,
               wkv_x=m_wkv_x, wo_x=m_wo_x, g_x_post=m_g_x_post, g_ffn_pre=m_g_ffn_pre, w_up=m_w_up,
               w_down=m_w_down, g_ffn_post=m_g_ffn_post)
    var = dict(g_mix_pre=v_g_mix_pre, w_in=v_w_in, b_forget=v_b_forget, pool_w=v_pool_w, pool_scale=v_pool_scale,
               w_out=v_w_out, g_mix_post=v_g_mix_post, g_x_pre=v_g_x_pre, g_mem=v_g_mem, wq_x=v_wq_x,
               wkv_x=v_wkv_x, wo_x=v_wo_x, g_x_post=v_g_x_post, g_ffn_pre=v_g_ffn_pre, w_up=v_w_up,
               w_down=v_w_down, g_ffn_post=v_g_ffn_post)
    S = x.shape[1]
    xs = x.reshape(S, D_MODEL)
    mems = mem.reshape(MEM_LEN, D_MODEL)
    target = loss_target.reshape(S, D_MODEL)

    def small_params(l):
        return dict(
            g_mix_pre=_vec(g_mix_pre[l]), g_mix_post=_vec(g_mix_post[l]), g_x_pre=_vec(g_x_pre[l]),
            g_mem=_vec(g_mem[l]), g_x_post=_vec(g_x_post[l]), g_ffn_pre=_vec(g_ffn_pre[l]),
            g_ffn_post=_vec(g_ffn_post[l]), pool_scale=_vec(pool_scale[l]), pool_w=pool_w[l].astype(BF16),
            b_forget=jnp.pad(_vec(b_forget[l]), ((0, 0), (0, LANES - FOX_HEADS))))

    shard = {n: [w[n][l].astype(BF16) for l in range(DEPTH)] for n in REST}
    shard["w_in"] = [_w_in_travel(w_in[l].astype(BF16)) for l in range(DEPTH)]
    sps = [small_params(l) for l in range(DEPTH)]
    saved, weights = [], []
    h = xs
    (g_in,) = _exchange("gather_w_in", [shard["w_in"][0]])
    hn = _norm_fwd("norm_fwd", xs, sps[0]["g_mix_pre"])
    for l in range(DEPTH):
        travelling = [shard[n][l] for n in REST] + ([shard["w_in"][l + 1]] if l + 1 < DEPTH else [])
        g_next = sps[l + 1]["g_mix_pre"] if l + 1 < DEPTH else None
        h, hn, sv, W, g_in = _layer_fwd(h, hn, mems, sps[l], g_in, travelling, g_next)
        saved.append(sv)
        weights.append(W)
    dh, sq = _loss_fwd_bwd(h, target)
    loss = lax.psum(0.5 * sq[0, 0] / D_MODEL, ("x", "y", "c"))

    parts = [dict() for _ in range(DEPTH)]
    small_grads = [None] * DEPTH
    carried = []
    lower = _norm_bwd("norm_bwd_b", dh, saved[-1]["y"], sps[-1]["g_ffn_post"], None, BF16)
    for l in reversed(range(DEPTH)):
        dy, dg_ffn_post = lower
        below = (saved[l - 1]["y"], sps[l - 1]["g_ffn_post"]) if l > 0 else None
        dh, lower, got, got_carried, dw_in, gs = _layer_bwd(dh, dy, mems, saved[l], sps[l], weights[l], carried, below)
        gs["g_ffn_post"] = dg_ffn_post
        parts[l].update(got)
        if got_carried:
            parts[l + 1]["w_in"] = got_carried[0]
        carried = [dw_in]
        small_grads[l] = gs
    (parts[0]["w_in"],) = _exchange("scatter_dw_in", carried)
    grad_x = dh.reshape(1, S, D_MODEL)

    grads, deltas, new_m, new_v = {}, {}, {}, {}
    rows = dict(w_in=128, w_out=128, wq_x=128, wkv_x=256, wo_x=128, w_up=256, w_down=128)
    for l in range(DEPTH):
        parts[l]["w_in"] = jnp.swapaxes(parts[l]["w_in"][:, :W_IN_SHARD, :], 1, 2)
    for n in BIG:
        grads[n], deltas[n], new_m[n], new_v[n] = _adamw_big(
            "adamw_" + n, w[n], mom[n], var[n], [parts[l][n] for l in range(DEPTH)], rows[n])

    sg = {n: jnp.stack([small_grads[l][n].reshape(w[n].shape[1:]) for l in range(DEPTH)]) for n in SMALL}
    (sg_parts,) = _exchange("gather_small_grads", [_pack_small(sg)])
    outs = _adamw_small(_pack_small(w), _pack_small(mom), _pack_small(var), sg_parts)
    for d, packed in zip((grads, deltas, new_m, new_v), outs):
        d.update(_unpack_small(packed, w))

    return (loss, grad_x, *[grads[n] for n in W_NAMES], *[deltas[n] for n in W_NAMES],
            *[new_m[n] for n in W_NAMES], *[new_v[n] for n in W_NAMES])
```

```python
import math

import jax
import jax.numpy as jnp
from jax import lax
from jax.experimental import pallas as pl
from jax.experimental.pallas import tpu as pltpu

F32 = jnp.float32
BF16 = jnp.bfloat16

D_MODEL = 1024
DEPTH = 4
FOX_WIDTH = 512
FOX_HEADS = 8
FOX_HEAD_DIM = 64
POOL_WIDTH = 512
POOL_WINDOWS = (2, 4, 8, 16)
POOL_GROUP_DIM = 128
POOL_HALO = 16
MEM_LEN = 256
X_HEADS = 4
X_HEAD_DIM = 256
D_FF = 4096
EPS = 1e-6
IN_COLS = 2056
QKV_COLS = 3 * FOX_WIDTH
UF_COLS = 640
INP_COLS = QKV_COLS + UF_COLS
N_DEV = 8
LANES = 128

ADAM_LR = 0.001
ADAM_B1 = 0.9
ADAM_B2 = 0.999
ADAM_EPS = 1e-08
ADAM_WD = 0.01
ADAM_STEP = 10

VMEM_LIMIT = 56 * 1024 * 1024

W_NAMES = ['g_mix_pre', 'w_in', 'b_forget', 'pool_w', 'pool_scale', 'w_out', 'g_mix_post', 'g_x_pre', 'g_mem',
           'wq_x', 'wkv_x', 'wo_x', 'g_x_post', 'g_ffn_pre', 'w_up', 'w_down', 'g_ffn_post']
BIG = ['w_in', 'w_out', 'wq_x', 'wkv_x', 'wo_x', 'w_up', 'w_down']
SMALL = [n for n in W_NAMES if n not in BIG]

NN = (((1,), (0,)), ((), ()))
NT = (((1,), (1,)), ((), ()))
TN = (((0,), (0,)), ((), ()))


def _params(*sem):
    return pltpu.CompilerParams(dimension_semantics=sem, vmem_limit_bytes=VMEM_LIMIT)


def _row_tile(s):
    return min(s, 512)


def _product(a_ref, w_ref, kind, c0, pw):
    cols = slice(c0, c0 + pw)
    if kind == "nn":
        return lax.dot_general(a_ref[...], w_ref[:, cols], NN, preferred_element_type=F32)
    if kind == "nt":
        return lax.dot_general(a_ref[...], w_ref[cols, :], NT, preferred_element_type=F32)
    n = w_ref.shape[2]
    if kind == "nn3":
        assert pw == n and c0 % n == 0
        return lax.dot_general(a_ref[...], w_ref[c0 // n], NN, preferred_element_type=F32)
    r = None
    for j in range(w_ref.shape[0]):
        part = lax.dot_general(a_ref[:, j * n:(j + 1) * n], w_ref[j, cols, :], NT, preferred_element_type=F32)
        r = part if r is None else r + part
    return r


def _resident(w):
    return pl.BlockSpec(w.shape, lambda i, nd=w.ndim: (0,) * nd)


def _mm_rows(name, terms, outs, extra=None, piece=1024):
    M = terms[0][0].shape[0]
    tm = _row_tile(M)
    nterm = len(terms)
    n_extra = 0 if extra is None else 1
    groups = {}
    for idx, (_, c0, width, fn) in enumerate(outs):
        groups.setdefault((c0, width), []).append((idx, fn))

    def body(*refs):
        a_refs = refs[0:2 * nterm:2]
        w_refs = refs[1:2 * nterm:2]
        extra_refs = refs[2 * nterm:2 * nterm + n_extra]
        out_refs = refs[2 * nterm + n_extra:]
        for (g0, gw), members in groups.items():
            for c0 in range(g0, g0 + gw, piece):
                pw = min(piece, g0 + gw - c0)
                r = None
                for a_ref, w_ref, (_, w, kind) in zip(a_refs, w_refs, terms):
                    part = _product(a_ref, w_ref, kind, c0, pw)
                    r = part if r is None else r + part
                dst = slice(c0 - g0, c0 - g0 + pw)
                for idx, fn in members:
                    if fn == "relu2":
                        rp = jnp.maximum(r, 0.0)
                        val = rp * rp
                    elif fn == "drelu2":
                        val = r * (2.0 * jnp.maximum(extra_refs[0][:, dst].astype(F32), 0.0))
                    else:
                        val = r
                    out_refs[idx][:, dst] = val.astype(out_refs[idx].dtype)

    in_specs, ins = [], []
    for a, w, _ in terms:
        in_specs.append(pl.BlockSpec((tm, a.shape[1]), lambda i: (i, 0)))
        in_specs.append(pl.BlockSpec(w.shape, lambda i, nd=w.ndim: (0,) * nd))
        ins += [a, w]
    if extra is not None:
        in_specs.append(pl.BlockSpec((tm, extra.shape[1]), lambda i: (i, 0)))
        ins.append(extra)
    res = pl.pallas_call(
        body, name=name, grid=(M // tm,), in_specs=in_specs,
        out_specs=[pl.BlockSpec((tm, width), lambda i: (i, 0)) for _, _, width, _ in outs],
        out_shape=[jax.ShapeDtypeStruct((M, width), dt) for dt, _, width, _ in outs],
        compiler_params=_params("parallel"))(*ins)
    return res


def _mm1(name, a, w, kind, n_cols, dtype, piece=1024):
    return _mm_rows(name, [(a, w, kind)], [(dtype, 0, n_cols, "id")], piece=piece)[0]


def _mm_tn(name, a, b, out_dtype, shard_cols=None, piece=512):
    K, M = a.shape
    N = b.shape[1]
    tk = _row_tile(K)
    nk = K // tk
    piece = shard_cols or min(piece, N)

    def body(a_ref, b_ref, o_ref, acc, a_t):
        k = pl.program_id(0)

        @pl.when(k == 0)
        def _():
            acc[...] = jnp.zeros_like(acc)

        a_t[...] = jnp.transpose(a_ref[...])
        for c0 in range(0, N, piece):
            cols = slice(c0, min(c0 + piece, N))
            acc[:, cols] += lax.dot_general(a_t[...], b_ref[:, cols], NN, preferred_element_type=F32)

        @pl.when(k == nk - 1)
        def _():
            for c0 in range(0, N, piece):
                cols = slice(c0, min(c0 + piece, N))
                if shard_cols:
                    o_ref[c0 // piece] = acc[:, cols].astype(o_ref.dtype)
                else:
                    o_ref[:, cols] = acc[:, cols].astype(o_ref.dtype)

    out_dims = (N // shard_cols, M, shard_cols) if shard_cols else (M, N)
    return pl.pallas_call(
        body, name=name, grid=(nk,),
        in_specs=[pl.BlockSpec((tk, M), lambda k: (k, 0)), pl.BlockSpec((tk, N), lambda k: (k, 0))],
        out_specs=pl.BlockSpec(out_dims, lambda k, nd=len(out_dims): (0,) * nd),
        out_shape=jax.ShapeDtypeStruct(out_dims, out_dtype),
        scratch_shapes=[pltpu.VMEM((M, N), F32), pltpu.VMEM((M, tk), a.dtype)],
        compiler_params=_params("arbitrary"))(a, b)


def _norm_fwd(name, x, g):
    S, Dm = x.shape
    ts = _row_tile(S)

    def body(x_ref, g_ref, h_ref):
        xv = x_ref[...]
        r = lax.rsqrt(jnp.mean(xv * xv, axis=-1, keepdims=True) + EPS)
        h_ref[...] = ((xv * r) * g_ref[...]).astype(BF16)

    return pl.pallas_call(
        body, name=name, grid=(S // ts,),
        in_specs=[pl.BlockSpec((ts, Dm), lambda i: (i, 0)), pl.BlockSpec((1, Dm), lambda i: (0, 0))],
        out_specs=pl.BlockSpec((ts, Dm), lambda i: (i, 0)),
        out_shape=jax.ShapeDtypeStruct((S, Dm), BF16), compiler_params=_params("parallel"))(x, g)


def _mm_resid_norm(name, a, w, x, g, g_next):
    S, Dm = x.shape
    ts = _row_tile(S)
    has_next = g_next is not None

    def body(a_ref, w_ref, x_ref, g_ref, *rest):
        fv = _product(a_ref, w_ref, "nn", 0, Dm)
        r = lax.rsqrt(jnp.mean(fv * fv, axis=-1, keepdims=True) + EPS)
        xn = x_ref[...] + (fv * r) * g_ref[...]
        if has_next:
            gn_ref, f_ref, o_ref, h_ref = rest
            rn = lax.rsqrt(jnp.mean(xn * xn, axis=-1, keepdims=True) + EPS)
            h_ref[...] = ((xn * rn) * gn_ref[...]).astype(BF16)
        else:
            f_ref, o_ref = rest
        f_ref[...] = fv
        o_ref[...] = xn

    row = pl.BlockSpec((ts, Dm), lambda i: (i, 0))
    vec = pl.BlockSpec((1, Dm), lambda i: (0, 0))
    ins = [a, w, x, g] + ([g_next] if has_next else [])
    f32_rows = jax.ShapeDtypeStruct((S, Dm), F32)
    res = pl.pallas_call(
        body, name=name, grid=(S // ts,),
        in_specs=[pl.BlockSpec((ts, a.shape[1]), lambda i: (i, 0)), _resident(w), row, vec] + ([vec] if has_next else []),
        out_specs=[row, row] + ([row] if has_next else []),
        out_shape=[f32_rows, f32_rows] + ([jax.ShapeDtypeStruct((S, Dm), BF16)] if has_next else []),
        compiler_params=_params("parallel"))(*ins)
    return (res[0], res[1], res[2]) if has_next else (res[0], res[1], None)


def _rms_bwd(dov, yv, g):
    r = lax.rsqrt(jnp.mean(yv * yv, axis=-1, keepdims=True) + EPS)
    z = dov * g
    yr = yv * r
    return r * (z - yr * jnp.mean(yr * z, axis=-1, keepdims=True)), jnp.sum(dov * yr, axis=0, keepdims=True)


def _norm_bwd(name, dout, y, g, resid, out_dtype, below=None):
    S, Dm = y.shape
    ts = _row_tile(S)
    has_resid = resid is not None
    chained = below is not None
    produced = isinstance(dout, tuple)
    kind = dout[2] if produced else None

    def body(*refs):
        refs = list(refs)
        if produced:
            dov = _product(refs[0], refs[1], kind, 0, Dm)
            refs = refs[1:]
        else:
            dov = refs[0][...]
        y_ref, g_ref = refs[1:3]
        pos = 3
        r_ref = refs[pos] if has_resid else None
        pos += has_resid
        if chained:
            f_ref, gf_ref = refs[pos:pos + 2]
            pos += 2
        dy_ref, dg_ref = refs[pos:pos + 2]
        i = pl.program_id(0)
        dy, dg = _rms_bwd(dov, y_ref[...], g_ref[...])
        if has_resid:
            dy = dy + r_ref[...]
        dy_ref[...] = dy.astype(out_dtype)

        @pl.when(i == 0)
        def _():
            for ref in refs[pos + 1::2]:
                ref[...] = jnp.zeros_like(ref)

        dg_ref[...] += dg
        if chained:
            df_ref, dgf_ref = refs[pos + 2:pos + 4]
            df, dgf = _rms_bwd(dy, f_ref[...], gf_ref[...])
            df_ref[...] = df.astype(BF16)
            dgf_ref[...] += dgf

    row = pl.BlockSpec((ts, Dm), lambda i: (i, 0))
    vec = pl.BlockSpec((1, Dm), lambda i: (0, 0))
    if produced:
        ins = [dout[0], dout[1]]
        specs = [pl.BlockSpec((ts, dout[0].shape[1]), lambda i: (i, 0)), _resident(dout[1])]
    else:
        ins = [dout]
        specs = [row]
    ins += [y, g] + ([resid] if has_resid else []) + (list(below) if chained else [])
    specs += [row, vec] + ([row] if has_resid else []) + ([row, vec] if chained else [])
    vec_shape = jax.ShapeDtypeStruct((1, Dm), F32)
    return pl.pallas_call(
        body, name=name, grid=(S // ts,), in_specs=specs, out_specs=[row, vec] + ([row, vec] if chained else []),
        out_shape=[jax.ShapeDtypeStruct((S, Dm), out_dtype), vec_shape]
        + ([jax.ShapeDtypeStruct((S, Dm), BF16), vec_shape] if chained else []),
        compiler_params=_params("arbitrary"))(*ins)


def _loss_fwd_bwd(y, t):
    S, Dm = y.shape
    ts = _row_tile(S)

    def body(y_ref, t_ref, dy_ref, acc_ref):
        i = pl.program_id(0)
        e = y_ref[...] - t_ref[...]
        dy_ref[...] = e * (1.0 / Dm)

        @pl.when(i == 0)
        def _():
            acc_ref[...] = jnp.zeros_like(acc_ref)

        s = jnp.sum(jnp.sum(e * e, axis=1, keepdims=True), axis=0, keepdims=True)
        acc_ref[...] += s

    row = pl.BlockSpec((ts, Dm), lambda i: (i, 0))
    return pl.pallas_call(
        body, name="loss", grid=(S // ts,), in_specs=[row, row],
        out_specs=[row, pl.BlockSpec((8, LANES), lambda i: (0, 0))],
        out_shape=[jax.ShapeDtypeStruct((S, Dm), F32), jax.ShapeDtypeStruct((8, LANES), F32)],
        compiler_params=_params("arbitrary"))(y, t)


def _log_sigmoid(x):
    return jnp.minimum(x, 0.0) - jnp.log(1.0 + jnp.exp(-jnp.abs(x)))


def _gate_fwd(uf, bpad):
    S = uf.shape[0]
    T = _row_tile(S)

    def body(f_ref, b_ref, c_ref, carry):
        i = pl.program_id(0)

        @pl.when(i == 0)
        def _():
            carry[...] = jnp.zeros_like(carry)

        lf = _log_sigmoid(f_ref[...] + b_ref[...])
        r = lax.broadcasted_iota(jnp.int32, (T, T), 0)
        cidx = lax.broadcasted_iota(jnp.int32, (T, T), 1)
        tri = (cidx <= r).astype(F32)
        c = lax.dot_general(tri, lf, NN, precision=lax.Precision.HIGHEST, preferred_element_type=F32)
        c_ref[...] = c + carry[0:1, :]
        carry[...] = carry[...] + jnp.sum(lf, axis=0, keepdims=True)

    return pl.pallas_call(
        body, name="gate_fwd", grid=(S // T,),
        in_specs=[pl.BlockSpec((T, LANES), lambda i: (i, 4)), pl.BlockSpec((1, LANES), lambda i: (0, 0))],
        out_specs=pl.BlockSpec((T, LANES), lambda i: (i, 0)),
        out_shape=jax.ShapeDtypeStruct((S, LANES), F32),
        scratch_shapes=[pltpu.VMEM((8, LANES), F32)], compiler_params=_params("arbitrary"))(uf, bpad)


def _gate_bwd(dc, uf, bpad):
    S = uf.shape[0]
    T = _row_tile(S)
    nb = S // T

    def body(dc_ref, f_ref, b_ref, df_ref, db_ref, carry):
        i = pl.program_id(0)

        @pl.when(i == 0)
        def _():
            carry[...] = jnp.zeros_like(carry)
            db_ref[...] = jnp.zeros_like(db_ref)

        dcv = dc_ref[...]
        r = lax.broadcasted_iota(jnp.int32, (T, T), 0)
        cidx = lax.broadcasted_iota(jnp.int32, (T, T), 1)
        tri = (cidx >= r).astype(F32)
        dlf = lax.dot_general(tri, dcv, NN, precision=lax.Precision.HIGHEST, preferred_element_type=F32)
        dlf = dlf + carry[0:1, :]
        carry[...] = carry[...] + jnp.sum(dcv, axis=0, keepdims=True)
        fg = f_ref[...] + b_ref[...]
        dfg = dlf / (1.0 + jnp.exp(fg))
        df_ref[...] = dfg.astype(BF16)
        db_ref[...] += jnp.sum(dfg, axis=0, keepdims=True)

    return pl.pallas_call(
        body, name="gate_bwd", grid=(nb,),
        in_specs=[pl.BlockSpec((T, LANES), lambda i: (nb - 1 - i, 0)),
                  pl.BlockSpec((T, LANES), lambda i: (nb - 1 - i, 4)),
                  pl.BlockSpec((1, LANES), lambda i: (0, 0))],
        out_specs=[pl.BlockSpec((T, LANES), lambda i: (nb - 1 - i, 0)), pl.BlockSpec((1, LANES), lambda i: (0, 0))],
        out_shape=[jax.ShapeDtypeStruct((S, LANES), BF16), jax.ShapeDtypeStruct((1, LANES), F32)],
        scratch_shapes=[pltpu.VMEM((8, LANES), F32)], compiler_params=_params("arbitrary"))(dc, uf, bpad)


FOX_CHUNK = 32
FOX_CHUNK_BWD = 64
HEAD_PAIRS = FOX_HEADS // 2
PAIR = 2


def _masked(s, row0, col0, diagonal):
    if diagonal:
        row = row0 + lax.broadcasted_iota(jnp.int32, s.shape, 0)
        col = col0 + lax.broadcasted_iota(jnp.int32, s.shape, 1)
        s = jnp.where(col <= row, s, -jnp.inf)
    return s


def _causal_pairs(n, query_major):
    if query_major:
        pairs = [(q, k) for q in range(n) for k in range(q + 1)]
    else:
        pairs = [(q, k) for k in range(n) for q in range(k, n)]
    return (jnp.asarray([p[0] for p in pairs], jnp.int32), jnp.asarray([p[1] for p in pairs], jnp.int32))


def _lane_block(b):
    return slice(b * LANES, (b + 1) * LANES)


def _fold(op, xs):
    acc = xs[0]
    for x in xs[1:]:
        acc = op(acc, x)
    return acc


def _head_lanes(hh):
    lane = lax.broadcasted_iota(jnp.int32, (1, LANES), 1)
    return (lane < FOX_HEAD_DIM) if hh == 0 else (lane >= FOX_HEAD_DIM)


def _pick(first_head, a, b):
    return jnp.where(first_head, a, b)


def _fox_fwd(qkv, cT, comm):
    S = qkv.shape[0]
    t = _row_tile(S)
    n = S // t
    nc = len(comm)
    scale = 1.0 / math.sqrt(FOX_HEAD_DIM)
    chunk = min(FOX_CHUNK, t)
    per_head = 7
    q_tab, k_tab = _causal_pairs(n, True)
    steps = q_tab.shape[0]

    def body(qt_ref, kt_ref, q_ref, k_ref, v_ref, c_ref, *rest):
        comm_in = rest[:nc]
        o_ref, ob_ref, lse_ref = rest[nc:nc + 3]
        comm_out = rest[nc + 3:2 * nc + 3]
        scr = rest[2 * nc + 3:2 * nc + 3 + PAIR * per_head]
        sems = rest[2 * nc + 3 + PAIR * per_head:]
        hp = pl.program_id(0)
        step_id = pl.program_id(1)
        qi = qt_ref[step_id]
        ki = kt_ref[step_id]

        if nc:
            @pl.when((hp == 0) & (step_id == 0))
            def _():
                _Gather(comm_in, comm_out, *sems).start()

            @pl.when((hp == HEAD_PAIRS - 1) & (step_id == 0))
            def _():
                _Gather(comm_in, comm_out, *sems).pass_on()

        @pl.when(ki == 0)
        def _():
            for hh in range(PAIR):
                m_s, l_s, a_s, acc_s = scr[hh * per_head:hh * per_head + 4]
                m_s[...] = jnp.full_like(m_s, -jnp.inf)
                l_s[...] = jnp.zeros_like(l_s)
                acc_s[...] = jnp.zeros_like(acc_s)

        def step(diagonal):
            q2 = q_ref[...] * scale
            k2 = k_ref[...]
            v2 = v_ref[...]
            for hh in range(PAIR):
                s_s = scr[hh * per_head + 4]
                qm = jnp.where(_head_lanes(hh), q2, jnp.zeros_like(q2))
                s_s[...] = lax.dot_general(qm, k2, NT, preferred_element_type=F32)
            for hh in range(PAIR):
                m_s, l_s, a_s, acc_s, s_s, ph_s, pl_s = scr[hh * per_head:(hh + 1) * per_head]
                for r in range(t // chunk):
                    rows = slice(r * chunk, (r + 1) * chunk)
                    blocks = [_masked(s_s[rows, _lane_block(b)] - c_ref[hh, :, _lane_block(b)], r * chunk,
                                      b * LANES, diagonal) for b in range(t // LANES)]
                    m_prev = m_s[rows, :]
                    m_new = jnp.maximum(m_prev, jnp.max(_fold(jnp.maximum, blocks), axis=1, keepdims=True))
                    alpha = jnp.exp(m_prev - m_new)
                    ps = [jnp.exp(blk - m_new) for blk in blocks]
                    l_s[rows, :] = alpha * l_s[rows, :] + jnp.sum(_fold(jnp.add, ps), axis=1, keepdims=True)
                    m_s[rows, :] = m_new
                    a_s[rows, :] = alpha
                    for b, p in enumerate(ps):
                        p_hi = p.astype(BF16)
                        ph_s[rows, _lane_block(b)] = p_hi
                        pl_s[rows, _lane_block(b)] = (p - p_hi.astype(F32)).astype(BF16)
                pv = (lax.dot_general(ph_s[...], v2, NN, preferred_element_type=F32)
                      + lax.dot_general(pl_s[...], v2, NN, preferred_element_type=F32))
                acc_s[...] = a_s[...] * acc_s[...] + pv

        @pl.when(ki < qi)
        def _():
            step(False)

        @pl.when(ki == qi)
        def _():
            step(True)
            heads = []
            for hh in range(PAIR):
                m_s, l_s, a_s, acc_s = scr[hh * per_head:hh * per_head + 4]
                heads.append(acc_s[...] / l_s[...])
                lse_ref[hh] = m_s[...] + jnp.log(l_s[...])
            o2 = _pick(_head_lanes(0), heads[0], heads[1])
            o_ref[...] = o2
            ob_ref[...] = o2.astype(BF16)

        if nc:
            @pl.when((hp == HEAD_PAIRS - 1) & (step_id == steps - 1))
            def _():
                _Gather(comm_in, comm_out, *sems).finish()

    def q_cols(first_block):
        return pl.BlockSpec((t, LANES), lambda h, s, qt, kt: (qt[s], first_block + h))

    def k_cols(first_block):
        return pl.BlockSpec((t, LANES), lambda h, s, qt, kt: (kt[s], first_block + h))

    any_spec = pl.BlockSpec(memory_space=pl.ANY)
    head_scratch = [pltpu.VMEM((t, LANES), F32), pltpu.VMEM((t, LANES), F32), pltpu.VMEM((t, LANES), F32),
                    pltpu.VMEM((t, LANES), F32), pltpu.VMEM((t, t), F32), pltpu.VMEM((t, t), BF16),
                    pltpu.VMEM((t, t), BF16)]
    grid_spec = pltpu.PrefetchScalarGridSpec(
        num_scalar_prefetch=2, grid=(HEAD_PAIRS, steps),
        in_specs=[q_cols(0), k_cols(HEAD_PAIRS), k_cols(2 * HEAD_PAIRS),
                  pl.BlockSpec((PAIR, 1, t), lambda h, s, qt, kt: (h, 0, kt[s]))] + [any_spec] * nc,
        out_specs=[q_cols(0), q_cols(0),
                   pl.BlockSpec((PAIR, t, LANES), lambda h, s, qt, kt: (h, qt[s], 0))] + [any_spec] * nc,
        scratch_shapes=head_scratch * PAIR + _comm_scratch(nc))
    return pl.pallas_call(
        body, name="fox_fwd", grid_spec=grid_spec,
        out_shape=[jax.ShapeDtypeStruct((S, FOX_WIDTH), F32), jax.ShapeDtypeStruct((S, FOX_WIDTH), BF16),
                   jax.ShapeDtypeStruct((FOX_HEADS, S, LANES), F32)] + _comm_shapes(comm),
        compiler_params=_params("arbitrary", "arbitrary"))(q_tab, k_tab, qkv, qkv, qkv, cT, *comm)


def _fox_bwd(qkv, cT, o, lse, do, comm):
    S = qkv.shape[0]
    t = _row_tile(S)
    n = S // t
    nc = len(comm)
    scale = 1.0 / math.sqrt(FOX_HEAD_DIM)
    chunk = min(FOX_CHUNK_BWD, t)
    per_head = 6
    q_tab, k_tab = _causal_pairs(n, False)
    steps = q_tab.shape[0]

    def body(qt_ref, kt_ref, q_ref, k_ref, v_ref, c_ref, o_ref, do_ref, lse_ref, *rest):
        comm_in = rest[:nc]
        dq_ref, dk_ref, dv_ref, dc_ref = rest[nc:nc + 4]
        comm_out = rest[nc + 4:2 * nc + 4]
        dq_s, dk_s, dv_s = rest[2 * nc + 4:2 * nc + 7]
        scr = rest[2 * nc + 7:2 * nc + 7 + PAIR * per_head]
        sems = rest[2 * nc + 7 + PAIR * per_head:]
        hp = pl.program_id(0)
        step_id = pl.program_id(1)
        qi = qt_ref[step_id]
        ki = kt_ref[step_id]

        if nc:
            @pl.when((hp == 0) & (step_id == 0))
            def _():
                for cp in _comm_copies(comm_in, comm_out, *sems):
                    cp.start()

        @pl.when(step_id == 0)
        def _():
            dq_s[...] = jnp.zeros_like(dq_s)

        @pl.when(qi == ki)
        def _():
            dk_s[...] = jnp.zeros_like(dk_s)
            dv_s[...] = jnp.zeros_like(dv_s)
            for hh in range(PAIR):
                dc_s = scr[hh * per_head]
                dc_s[...] = jnp.zeros_like(dc_s)

        def step(diagonal):
            q2 = q_ref[...]
            k2 = k_ref[...]
            v2 = v_ref[...]
            do2 = do_ref[...]
            prod = do2.astype(F32) * o_ref[...]
            grads = []
            for hh in range(PAIR):
                dc_s, delta_s, s_s, dp_s, p_s, ds_s = scr[hh * per_head:(hh + 1) * per_head]
                mine = _head_lanes(hh)
                s_s[...] = lax.dot_general(jnp.where(mine, q2 * scale, jnp.zeros_like(q2)), k2, NT,
                                           preferred_element_type=F32)
                dp_s[...] = lax.dot_general(jnp.where(mine, do2, jnp.zeros_like(do2)), v2, NT,
                                            preferred_element_type=F32)
                delta_s[...] = jnp.broadcast_to(jnp.sum(jnp.where(mine, prod, 0.0), axis=1, keepdims=True),
                                                (t, LANES))
                dc8 = [jnp.zeros((8, LANES), F32) for _ in range(t // LANES)]
                for r in range(t // chunk):
                    rows = slice(r * chunk, (r + 1) * chunk)
                    lse = lse_ref[hh, rows, :]
                    delta = delta_s[rows, :]
                    for b in range(t // LANES):
                        s = _masked(s_s[rows, _lane_block(b)] - c_ref[hh, :, _lane_block(b)], r * chunk, b * LANES,
                                    diagonal)
                        p = jnp.exp(s - lse)
                        ds = p * (dp_s[rows, _lane_block(b)] - delta)
                        p_s[rows, _lane_block(b)] = p.astype(BF16)
                        ds_s[rows, _lane_block(b)] = ds.astype(BF16)
                        dc8[b] = dc8[b] + jnp.sum(ds.reshape(chunk // 8, 8, LANES), axis=0)
                for b in range(t // LANES):
                    dc_s[:, _lane_block(b)] += jnp.sum(dc8[b], axis=0, keepdims=True)
                dsb = ds_s[...]
                grads.append((lax.dot_general(p_s[...], do2, TN, preferred_element_type=F32),
                              lax.dot_general(dsb, k2, NN, preferred_element_type=F32),
                              lax.dot_general(dsb, q2, TN, preferred_element_type=F32)))
            first = _head_lanes(0)
            dv_s[...] += _pick(first, grads[0][0], grads[1][0])
            q_rows = pl.ds(pl.multiple_of(qi * t, t), t)
            dq_s[q_rows, :] += _pick(first, grads[0][1], grads[1][1]) * scale
            dk_s[...] += _pick(first, grads[0][2], grads[1][2]) * scale

        @pl.when(qi > ki)
        def _():
            step(False)

        @pl.when(qi == ki)
        def _():
            step(True)

        @pl.when(qi == n - 1)
        def _():
            dk_ref[...] = dk_s[...].astype(BF16)
            dv_ref[...] = dv_s[...].astype(BF16)
            for hh in range(PAIR):
                dc_ref[hh] = -scr[hh * per_head][...]

        @pl.when(step_id == steps - 1)
        def _():
            dq_ref[...] = dq_s[...].astype(BF16)

        if nc:
            @pl.when((hp == HEAD_PAIRS - 1) & (step_id == steps - 1))
            def _():
                for cp in _comm_copies(comm_in, comm_out, *sems):
                    cp.wait()

    def q_side(first_block):
        return pl.BlockSpec((t, LANES), lambda h, s, qt, kt: (qt[s], first_block + h))

    def k_side(first_block):
        return pl.BlockSpec((t, LANES), lambda h, s, qt, kt: (kt[s], first_block + h))

    any_spec = pl.BlockSpec(memory_space=pl.ANY)
    head_scratch = [pltpu.VMEM((1, t), F32), pltpu.VMEM((t, LANES), F32),
                    pltpu.VMEM((t, t), F32), pltpu.VMEM((t, t), F32), pltpu.VMEM((t, t), BF16),
                    pltpu.VMEM((t, t), BF16)]
    grad_shape = jax.ShapeDtypeStruct((S, FOX_WIDTH), BF16)
    grid_spec = pltpu.PrefetchScalarGridSpec(
        num_scalar_prefetch=2, grid=(HEAD_PAIRS, steps),
        in_specs=[q_side(0), k_side(HEAD_PAIRS), k_side(2 * HEAD_PAIRS),
                  pl.BlockSpec((PAIR, 1, t), lambda h, s, qt, kt: (h, 0, kt[s])), q_side(0), q_side(0),
                  pl.BlockSpec((PAIR, t, LANES), lambda h, s, qt, kt: (h, qt[s], 0))] + [any_spec] * nc,
        out_specs=[pl.BlockSpec((S, LANES), lambda h, s, qt, kt: (0, h)), k_side(0), k_side(0),
                   pl.BlockSpec((PAIR, 1, t), lambda h, s, qt, kt: (h, 0, kt[s]))] + [any_spec] * nc,
        scratch_shapes=[pltpu.VMEM((S, LANES), F32), pltpu.VMEM((t, LANES), F32), pltpu.VMEM((t, LANES), F32)]
        + head_scratch * PAIR + _comm_scratch(nc))
    return pl.pallas_call(
        body, name="fox_bwd", grid_spec=grid_spec,
        out_shape=[grad_shape, grad_shape, grad_shape, jax.ShapeDtypeStruct((FOX_HEADS, 1, S), F32)]
        + _comm_shapes(comm),
        compiler_params=_params("arbitrary", "arbitrary"))(q_tab, k_tab, qkv, qkv, qkv, cT, o, do, lse, *comm)


def _lanes(g):
    return slice(g * POOL_GROUP_DIM, (g + 1) * POOL_GROUP_DIM)


def _window_sum(e, win, back):
    rows = e.shape[0]
    s = e
    sh = 1
    while sh < win:
        s = s + pltpu.roll(s, sh if back else rows - sh, 0)
        sh *= 2
    return s


def _pooled(u_ref, up_ref, i, g, win, T):
    cur = u_ref[:, _lanes(g)]
    tail = jnp.where(i > 0, up_ref[T - POOL_HALO:T, _lanes(g)], 0.0)
    e = jnp.concatenate([tail, cur], axis=0)
    s = _window_sum(e, win, True)
    t_idx = i * T - POOL_HALO + lax.broadcasted_iota(jnp.int32, (T + POOL_HALO, POOL_GROUP_DIM), 0)
    cnt = jnp.clip(t_idx + 1, 1, win).astype(F32)
    return (s / cnt - e)[POOL_HALO:, :]


def _pool_fwd(uf, pw, ps):
    S = uf.shape[0]
    T = _row_tile(S)

    def body(u_ref, up_ref, w_ref, sc_ref, o_ref):
        i = pl.program_id(0)
        for g, win in enumerate(POOL_WINDOWS):
            pb = _pooled(u_ref, up_ref, i, g, win, T).astype(BF16)
            yv = lax.dot_general(pb, w_ref[g], NN, preferred_element_type=F32)
            o_ref[:, _lanes(g)] = (yv * sc_ref[:, _lanes(g)]).astype(BF16)

    return pl.pallas_call(
        body, name="pool_fwd", grid=(S // T,),
        in_specs=[pl.BlockSpec((T, POOL_WIDTH), lambda i: (i, 0)),
                  pl.BlockSpec((T, POOL_WIDTH), lambda i: (jnp.maximum(i - 1, 0), 0)),
                  pl.BlockSpec((4, POOL_GROUP_DIM, POOL_GROUP_DIM), lambda i: (0, 0, 0)),
                  pl.BlockSpec((1, POOL_WIDTH), lambda i: (0, 0))],
        out_specs=pl.BlockSpec((T, POOL_WIDTH), lambda i: (i, 0)),
        out_shape=jax.ShapeDtypeStruct((S, POOL_WIDTH), BF16), compiler_params=_params("parallel"))(uf, uf, pw, ps)


def _pool_bwd(uf, dpool, pw, ps):
    S = uf.shape[0]
    T = _row_tile(S)
    nb = S // T

    def body(u_ref, up_ref, d_ref, dn_ref, w_ref, sc_ref, du_ref, dw_ref, dsc_ref):
        i = pl.program_id(0)

        @pl.when(i == 0)
        def _():
            dw_ref[...] = jnp.zeros_like(dw_ref)
            dsc_ref[...] = jnp.zeros_like(dsc_ref)

        t_idx = i * T + lax.broadcasted_iota(jnp.int32, (T + POOL_HALO, POOL_GROUP_DIM), 0)
        for g, win in enumerate(POOL_WINDOWS):
            pb = _pooled(u_ref, up_ref, i, g, win, T).astype(BF16)
            w = w_ref[g]
            sc = sc_ref[:, _lanes(g)]
            yv = lax.dot_general(pb, w, NN, preferred_element_type=F32)
            dov = d_ref[:, _lanes(g)]
            dsc_ref[:, _lanes(g)] += jnp.sum(dov * yv, axis=0, keepdims=True)
            head = jnp.where(i < nb - 1, dn_ref[0:POOL_HALO, _lanes(g)], 0.0)
            dyb = (jnp.concatenate([dov, head], axis=0) * sc).astype(BF16)
            dw_ref[g] += lax.dot_general(pb, dyb[:T], TN, preferred_element_type=F32)
            dpl = lax.dot_general(dyb, w, NT, preferred_element_type=F32)
            cnt = jnp.minimum(t_idx + 1, win).astype(F32)
            a = _window_sum(dpl / cnt, win, False)
            du_ref[:, _lanes(g)] = (a - dpl)[:T].astype(BF16)

    return pl.pallas_call(
        body, name="pool_bwd", grid=(nb,),
        in_specs=[pl.BlockSpec((T, POOL_WIDTH), lambda i: (i, 0)),
                  pl.BlockSpec((T, POOL_WIDTH), lambda i: (jnp.maximum(i - 1, 0), 0)),
                  pl.BlockSpec((T, POOL_WIDTH), lambda i: (i, 0)),
                  pl.BlockSpec((T, POOL_WIDTH), lambda i: (jnp.minimum(i + 1, nb - 1), 0)),
                  pl.BlockSpec((4, POOL_GROUP_DIM, POOL_GROUP_DIM), lambda i: (0, 0, 0)),
                  pl.BlockSpec((1, POOL_WIDTH), lambda i: (0, 0))],
        out_specs=[pl.BlockSpec((T, POOL_WIDTH), lambda i: (i, 0)),
                   pl.BlockSpec((4, POOL_GROUP_DIM, POOL_GROUP_DIM), lambda i: (0, 0, 0)),
                   pl.BlockSpec((1, POOL_WIDTH), lambda i: (0, 0))],
        out_shape=[jax.ShapeDtypeStruct((S, POOL_WIDTH), BF16),
                   jax.ShapeDtypeStruct((4, POOL_GROUP_DIM, POOL_GROUP_DIM), F32),
                   jax.ShapeDtypeStruct((1, POOL_WIDTH), F32)],
        compiler_params=_params("arbitrary"))(uf, uf, dpool, dpool, pw, ps)


def _xhead(h):
    return slice(h * X_HEAD_DIM, (h + 1) * X_HEAD_DIM)


def _xvhead(h):
    return slice(D_MODEL + h * X_HEAD_DIM, D_MODEL + (h + 1) * X_HEAD_DIM)


def _x_probs(qh, kh):
    s = lax.dot_general(qh, kh, NT, preferred_element_type=F32) * (1.0 / math.sqrt(X_HEAD_DIM))
    e = jnp.exp(s - jnp.max(s, axis=1, keepdims=True))
    return e / jnp.sum(e, axis=1, keepdims=True)


def _xattn_fwd(q, kv):
    S = q.shape[0]
    t = _row_tile(S)

    def body(q_ref, kv_ref, o_ref):
        for h in range(X_HEADS):
            p = _x_probs(q_ref[:, _xhead(h)], kv_ref[:, _xhead(h)])
            o_ref[:, _xhead(h)] = lax.dot_general(p.astype(BF16), kv_ref[:, _xvhead(h)], NN,
                                                  preferred_element_type=F32).astype(BF16)

    return pl.pallas_call(
        body, name="xattn_fwd", grid=(S // t,),
        in_specs=[pl.BlockSpec((t, D_MODEL), lambda i: (i, 0)), pl.BlockSpec((MEM_LEN, 2 * D_MODEL), lambda i: (0, 0))],
        out_specs=pl.BlockSpec((t, D_MODEL), lambda i: (i, 0)),
        out_shape=jax.ShapeDtypeStruct((S, D_MODEL), BF16), compiler_params=_params("parallel"))(q, kv)


def _xattn_bwd(q, kv, do):
    S = q.shape[0]
    t = _row_tile(S)
    nb = S // t
    scale = 1.0 / math.sqrt(X_HEAD_DIM)

    def body(q_ref, kv_ref, do_ref, dq_ref, dkv_ref, acc):
        i = pl.program_id(0)

        @pl.when(i == 0)
        def _():
            acc[...] = jnp.zeros_like(acc)

        for h in range(X_HEADS):
            qh = q_ref[:, _xhead(h)]
            kh = kv_ref[:, _xhead(h)]
            doh = do_ref[:, _xhead(h)]
            p = _x_probs(qh, kh)
            acc[:, _xvhead(h)] += lax.dot_general(p.astype(BF16), doh, TN, preferred_element_type=F32)
            dp = lax.dot_general(doh, kv_ref[:, _xvhead(h)], NT, preferred_element_type=F32)
            ds = p * (dp - jnp.sum(dp * p, axis=1, keepdims=True))
            dsb = ds.astype(BF16)
            dq_ref[:, _xhead(h)] = (lax.dot_general(dsb, kh, NN, preferred_element_type=F32) * scale).astype(BF16)
            acc[:, _xhead(h)] += lax.dot_general(dsb, qh, TN, preferred_element_type=F32) * scale

        @pl.when(i == nb - 1)
        def _():
            dkv_ref[...] = acc[...].astype(BF16)

    row = pl.BlockSpec((t, D_MODEL), lambda i: (i, 0))
    full = pl.BlockSpec((MEM_LEN, 2 * D_MODEL), lambda i: (0, 0))
    return pl.pallas_call(
        body, name="xattn_bwd", grid=(nb,), in_specs=[row, full, row], out_specs=[row, full],
        out_shape=[jax.ShapeDtypeStruct((S, D_MODEL), BF16), jax.ShapeDtypeStruct((MEM_LEN, 2 * D_MODEL), BF16)],
        scratch_shapes=[pltpu.VMEM((MEM_LEN, 2 * D_MODEL), F32)],
        compiler_params=_params("arbitrary"))(q, kv, do)


def _comm_shapes(arrs):
    return [jax.ShapeDtypeStruct((N_DEV,) + tuple(a.shape[-2:]), a.dtype) for a in arrs]


def _comm_scratch(n):
    if n == 0:
        return []
    return [pltpu.SemaphoreType.DMA((n, N_DEV - 1)), pltpu.SemaphoreType.DMA((n, N_DEV - 1)),
            pltpu.SemaphoreType.DMA((n,))]


def _comm_copies(ins, outs, send_sems, recv_sems, local_sems):
    x, y, c = lax.axis_index("x"), lax.axis_index("y"), lax.axis_index("c")
    me = 4 * x + 2 * y + c
    copies = []
    for w in range(len(ins)):
        src = ins[w] if len(ins[w].shape) == 2 else ins[w].at[me]
        copies.append(pltpu.make_async_copy(src, outs[w].at[me], local_sems.at[w]))
    for k in range(1, N_DEV):
        px = 1 - x if k & 4 else x
        py = 1 - y if k & 2 else y
        pc = 1 - c if k & 1 else c
        peer = 4 * px + 2 * py + pc
        for w in range(len(ins)):
            src = ins[w] if len(ins[w].shape) == 2 else ins[w].at[peer]
            copies.append(pltpu.make_async_remote_copy(
                src_ref=src, dst_ref=outs[w].at[me], send_sem=send_sems.at[w, k - 1],
                recv_sem=recv_sems.at[w, k - 1], device_id=(px, py, pc), device_id_type=pl.DeviceIdType.MESH))
    return copies


class _Gather:
    def __init__(self, ins, outs, send_sems, recv_sems, local_sems):
        x, y, c = lax.axis_index("x"), lax.axis_index("y"), lax.axis_index("c")
        me = 4 * x + 2 * y + c
        sibling = (x, y, 1 - c)
        self.local, self.mine, self.passed = [], [], []
        for w in range(len(ins)):
            def remote(idx, src, slot, dev, w=w):
                return pltpu.make_async_remote_copy(
                    src_ref=src, dst_ref=outs[w].at[slot], send_sem=send_sems.at[w, idx],
                    recv_sem=recv_sems.at[w, idx], device_id=dev, device_id_type=pl.DeviceIdType.MESH)

            self.local.append(pltpu.make_async_copy(ins[w], outs[w].at[me], local_sems.at[w]))
            mine, passed = [remote(0, ins[w], me, sibling)], []
            for j, (fx, fy) in enumerate(((0, 1), (1, 0), (1, 1))):
                px = 1 - x if fx else x
                py = 1 - y if fy else y
                slot = 4 * px + 2 * py + c
                mine.append(remote(1 + j, ins[w], me, (px, py, c)))
                passed.append(remote(4 + j, outs[w].at[slot], slot, sibling))
            self.mine.append(mine)
            self.passed.append(passed)

    def start(self):
        for cp in self.local:
            cp.start()
        for mine in self.mine:
            for cp in mine:
                cp.start()

    def pass_on(self):
        for mine, passed in zip(self.mine, self.passed):
            for j, cp in enumerate(passed):
                mine[1 + j].wait_recv()
                cp.start()

    def finish(self):
        for mine, passed in zip(self.mine, self.passed):
            mine[0].wait_recv()
            for cp in passed:
                cp.wait_recv()
            for cp in mine + passed:
                cp.wait_send()
        for cp in self.local:
            cp.wait()


def _exchange(name, arrs):
    n = len(arrs)
    gather = all(a.ndim == 2 for a in arrs)

    def body(*refs):
        if gather:
            g = _Gather(refs[:n], refs[n:2 * n], *refs[2 * n:])
            g.start()
            g.pass_on()
            g.finish()
            return
        copies = _comm_copies(refs[:n], refs[n:2 * n], *refs[2 * n:])
        for cp in copies:
            cp.start()
        for cp in copies:
            cp.wait()

    any_spec = pl.BlockSpec(memory_space=pl.ANY)
    return pl.pallas_call(
        body, name=name, in_specs=[any_spec] * n, out_specs=[any_spec] * n, out_shape=_comm_shapes(arrs),
        scratch_shapes=_comm_scratch(n))(*arrs)


def _adamw_math(w, g, m, v):
    m = ADAM_B1 * m + (1.0 - ADAM_B1) * g
    v = ADAM_B2 * v + (1.0 - ADAM_B2) * (g * g)
    m_hat = m / (1.0 - ADAM_B1 ** ADAM_STEP)
    v_hat = v / (1.0 - ADAM_B2 ** ADAM_STEP)
    delta = -ADAM_LR * (m_hat / (jnp.sqrt(v_hat) + ADAM_EPS) + ADAM_WD * w)
    return delta, m, v


def _sum_parts(p_ref):
    g = p_ref[0].astype(F32)
    for s in range(1, N_DEV):
        g = g + p_ref[s].astype(F32)
    return g


def _adamw_big(name, w, m, v, parts, tr):
    L, R, C = w.shape

    def body(w_ref, m_ref, v_ref, *rest):
        p_refs = rest[:L]
        g_ref, d_ref, nm_ref, nv_ref = rest[L:]
        layer = pl.program_id(0)
        for j in range(L):
            @pl.when(layer == j)
            def _(j=j):
                g = _sum_parts(p_refs[j])
                delta, nm, nv = _adamw_math(w_ref[...], g, m_ref[...], v_ref[...])
                g_ref[...] = g
                d_ref[...] = delta
                nm_ref[...] = nm
                nv_ref[...] = nv

    blk = pl.BlockSpec((None, tr, C), lambda l, i: (l, i, 0))

    def part_spec(j):
        return pl.BlockSpec((N_DEV, tr, C), lambda l, i: (0, jnp.where(l == j, i, 0), 0))

    shp = jax.ShapeDtypeStruct((L, R, C), F32)
    return pl.pallas_call(
        body, name=name, grid=(L, R // tr), in_specs=[blk, blk, blk] + [part_spec(j) for j in range(L)],
        out_specs=[blk] * 4, out_shape=[shp] * 4, compiler_params=_params("arbitrary", "arbitrary"))(w, m, v, *parts)


def _adamw_small(w, m, v, parts):
    R, C = w.shape

    def body(w_ref, m_ref, v_ref, p_ref, g_ref, d_ref, nm_ref, nv_ref):
        g = _sum_parts(p_ref)
        delta, nm, nv = _adamw_math(w_ref[...], g, m_ref[...], v_ref[...])
        g_ref[...] = g
        d_ref[...] = delta
        nm_ref[...] = nm
        nv_ref[...] = nv

    shp = jax.ShapeDtypeStruct((R, C), F32)
    return pl.pallas_call(body, name="adamw_small", out_shape=[shp] * 4,
                          compiler_params=pltpu.CompilerParams(vmem_limit_bytes=VMEM_LIMIT))(w, m, v, parts)


def _vec(a):
    return a.reshape(1, -1)


W_IN_SHARD = IN_COLS // N_DEV
W_IN_ROWS = 272


def _w_in_travel(a):
    pad = [(0, 0)] * (a.ndim - 2) + [(0, W_IN_ROWS - W_IN_SHARD), (0, 0)]
    return jnp.pad(jnp.swapaxes(a, -1, -2), pad)


def _unpack_w_in(g):
    full = jnp.transpose(g[:, :W_IN_SHARD, :], (2, 0, 1)).reshape(D_MODEL, IN_COLS)
    qkv = full[:, :QKV_COLS]
    f = full[:, QKV_COLS:QKV_COLS + FOX_HEADS]
    u = full[:, QKV_COLS + FOX_HEADS:]
    uf = jnp.concatenate([u, f, jnp.zeros((D_MODEL, UF_COLS - POOL_WIDTH - FOX_HEADS), g.dtype)], axis=1)
    return jnp.concatenate([qkv, uf], axis=1)


def _pack_dw_in(dwp):
    qkv = dwp[:, :QKV_COLS]
    u = dwp[:, QKV_COLS:QKV_COLS + POOL_WIDTH]
    f = dwp[:, QKV_COLS + POOL_WIDTH:QKV_COLS + POOL_WIDTH + FOX_HEADS]
    full = jnp.concatenate([qkv, f, u], axis=1)
    return _w_in_travel(jnp.transpose(full.reshape(D_MODEL, N_DEV, W_IN_SHARD), (1, 0, 2)))


REST = ['w_out', 'wq_x', 'wkv_x', 'wo_x', 'w_up', 'w_down']


def _layer_fwd(x0, h1, mem, sp, g_in, shards, g_next):
    S = x0.shape[0]
    sv = {"x0": x0}
    w_inp = _unpack_w_in(g_in)
    qkv, uf = _mm_rows("mm_in", [(h1, w_inp, "nn")],
                       [(BF16, 0, QKV_COLS, "id"), (F32, QKV_COLS, UF_COLS, "id")], piece=UF_COLS)
    c = _gate_fwd(uf, sp["b_forget"])
    cT = jnp.transpose(c[:, :FOX_HEADS]).reshape(FOX_HEADS, 1, S)
    o, ob, lse, *got = _fox_fwd(qkv, cT, shards)
    g_out, g_q, g_kv, g_o, g_up, g_down = got[:6]
    W = dict(inp=w_inp, out=g_out.reshape(D_MODEL, D_MODEL), q=g_q.reshape(D_MODEL, D_MODEL), kv=g_kv,
             o=g_o.reshape(D_MODEL, D_MODEL), up=g_up, down=g_down.reshape(D_FF, D_MODEL))
    pool = _pool_fwd(uf, sp["pool_w"], sp["pool_scale"])
    cat = jnp.concatenate([ob, pool], axis=1)
    mix, x1, h2 = _mm_resid_norm("mm_sq_norm", cat, W["out"], x0, sp["g_mix_post"], sp["g_x_pre"])
    mn = _norm_fwd("norm_mem", mem, sp["g_mem"])
    q2 = _mm1("mm_q", h2, W["q"], "nn", D_MODEL, BF16)
    kv = _mm1("mm_kv", mn, W["kv"], "nn3", 2 * D_MODEL, BF16, piece=2 * D_MODEL // N_DEV)
    o2 = _xattn_fwd(q2, kv)
    xo, x2, h3 = _mm_resid_norm("mm_sq_norm", o2, W["o"], x1, sp["g_x_post"], sp["g_ffn_pre"])
    up, act = _mm_rows("mm_up", [(h3, W["up"], "nn3")], [(BF16, 0, D_FF, "id"), (BF16, 0, D_FF, "relu2")],
                       piece=D_FF // N_DEV)
    y, x3, h_next = _mm_resid_norm("mm_down_norm" if g_next is not None else "mm_down_norm_last", act, W["down"], x2,
                                   sp["g_ffn_post"], g_next)
    sv.update(h1=h1, uf=uf, cT=cT, qkv=qkv, o=o, lse=lse, cat=cat, mix=mix, x1=x1, h2=h2, mn=mn, q2=q2, kv=kv,
              o2=o2, xo=xo, x2=x2, h3=h3, up=up, act=act, y=y)
    return x3, h_next, sv, W, (got[6] if len(got) > 6 else None)


def _layer_bwd(dx3, dy, mem, sv, sp, W, carried, below):
    S = dx3.shape[0]
    gs = {}
    gb = {}
    (dup,) = _mm_rows("mm_dup", [(dy, W["down"], "nt")], [(BF16, 0, D_FF, "drelu2")], extra=sv["up"])
    gb["w_down"] = _mm_tn("mm_dw_down", sv["act"], dy, BF16).reshape(N_DEV, D_FF // N_DEV, D_MODEL)
    gb["w_up"] = _mm_tn("mm_dw_up", sv["h3"], dup, BF16, shard_cols=D_FF // N_DEV)
    dx2, gs["g_ffn_pre"], dxo, gs["g_x_post"] = _norm_bwd(
        "mm_dh3_norm_bwd", (dup, W["up"], "nt3"), sv["x2"], sp["g_ffn_pre"], dx3, F32,
        below=(sv["xo"], sp["g_x_post"]))
    do2 = _mm1("mm_sq_t", dxo, W["o"], "nt", D_MODEL, BF16)
    gb["wo_x"] = _mm_tn("mm_dw_sq", sv["o2"], dxo, BF16).reshape(N_DEV, D_MODEL // N_DEV, D_MODEL)
    dq2, dkvb = _xattn_bwd(sv["q2"], sv["kv"], do2)
    gb["wq_x"] = _mm_tn("mm_dw_sq", sv["h2"], dq2, BF16).reshape(N_DEV, D_MODEL // N_DEV, D_MODEL)
    gb["wkv_x"] = _mm_tn("mm_dw_kv", sv["mn"], dkvb, BF16, shard_cols=2 * D_MODEL // N_DEV)
    dmn = _mm1("mm_dmn", dkvb, W["kv"], "nt3", D_MODEL, F32)
    _, gs["g_mem"] = _norm_bwd("norm_bwd_mem", dmn, mem, sp["g_mem"], None, BF16)
    dx1, gs["g_x_pre"], dmix, gs["g_mix_post"] = _norm_bwd(
        "mm_dh2_norm_bwd", (dq2, W["q"], "nt"), sv["x1"], sp["g_x_pre"], dx2, F32,
        below=(sv["mix"], sp["g_mix_post"]))
    doh, dpool = _mm_rows("mm_dcat", [(dmix, W["out"], "nt")],
                          [(BF16, 0, FOX_WIDTH, "id"), (F32, FOX_WIDTH, POOL_WIDTH, "id")])
    gb["w_out"] = _mm_tn("mm_dw_sq", sv["cat"], dmix, BF16).reshape(N_DEV, D_MODEL // N_DEV, D_MODEL)
    du, gs["pool_w"], gs["pool_scale"] = _pool_bwd(sv["uf"], dpool, sp["pool_w"], sp["pool_scale"])
    dq, dk, dv, dcT, *got = _fox_bwd(sv["qkv"], sv["cT"], sv["o"], sv["lse"], doh, [gb[n] for n in REST] + carried)
    dc = jnp.pad(jnp.transpose(dcT.reshape(FOX_HEADS, S)), ((0, 0), (0, LANES - FOX_HEADS)))
    dfg, db = _gate_bwd(dc, sv["uf"], sp["b_forget"])
    gs["b_forget"] = db[:, :FOX_HEADS]
    dproj = jnp.concatenate([dq, dk, dv, du, dfg], axis=1)
    dwp = _mm_tn("mm_dw_in", sv["h1"], dproj, BF16, piece=UF_COLS)
    dh1 = (dproj, W["inp"], "nt")
    if below is None:
        dx0, gs["g_mix_pre"] = _norm_bwd("mm_dh1_norm_bwd_first", dh1, sv["x0"], sp["g_mix_pre"], dx1, F32)
        lower = None
    else:
        dx0, gs["g_mix_pre"], *lower = _norm_bwd("mm_dh1_norm_bwd", dh1, sv["x0"], sp["g_mix_pre"], dx1, F32,
                                                 below=below)
    return dx0, lower, dict(zip(REST, got[:6])), got[6:], _pack_dw_in(dwp), gs


def _small_rows(shape):
    return -(-math.prod(shape) // (8 * LANES)) * 8


def _pack_small(d):
    blocks = []
    for n in SMALL:
        rows = _small_rows(d[n].shape)
        if d[n].shape[-1] == LANES:
            blocks.append(d[n].reshape(rows, LANES))
        else:
            flat = d[n].reshape(-1)
            blocks.append(jnp.pad(flat, (0, rows * LANES - flat.shape[0])).reshape(rows, LANES))
    return jnp.concatenate(blocks, axis=0)


def _unpack_small(packed, like):
    out = {}
    row = 0
    for n in SMALL:
        shape = like[n].shape
        rows = _small_rows(shape)
        block = packed[row:row + rows]
        out[n] = block.reshape(shape) if shape[-1] == LANES else block.reshape(-1)[:math.prod(shape)].reshape(shape)
        row += rows
    return out


def kernel(x, mem, g_mix_pre, w_in, b_forget, pool_w, pool_scale, w_out, g_mix_post, g_x_pre, g_mem, wq_x, wkv_x, wo_x, g_x_post, g_ffn_pre, w_up, w_down, g_ffn_post, loss_target, m_g_mix_pre, m_w_in, m_b_forget, m_pool_w, m_pool_scale, m_w_out, m_g_mix_post, m_g_x_pre, m_g_mem, m_wq_x, m_wkv_x, m_wo_x, m_g_x_post, m_g_ffn_pre, m_w_up, m_w_down, m_g_ffn_post, v_g_mix_pre, v_w_in, v_b_forget, v_pool_w, v_pool_scale, v_w_out, v_g_mix_post, v_g_x_pre, v_g_mem, v_wq_x, v_wkv_x, v_wo_x, v_g_x_post, v_g_ffn_pre, v_w_up, v_w_down, v_g_ffn_post):
    w = dict(g_mix_pre=g_mix_pre, w_in=w_in, b_forget=b_forget, pool_w=pool_w, pool_scale=pool_scale, w_out=w_out,
             g_mix_post=g_mix_post, g_x_pre=g_x_pre, g_mem=g_mem, wq_x=wq_x, wkv_x=wkv_x, wo_x=wo_x,
             g_x_post=g_x_post, g_ffn_pre=g_ffn_pre, w_up=w_up, w_down=w_down, g_ffn_post=g_ffn_post)
    mom = dict(g_mix_pre=m_g_mix_pre, w_in=m_w_in, b_forget=m_b_forget, pool_w=m_pool_w, pool_scale=m_pool_scale,
               w_out=m_w_out, g_mix_post=m_g_mix_post, g_x_pre=m_g_x_pre, g_mem=m_g_mem, wq_x=m_wq_x,
               wkv_x=m_wkv_x, wo_x=m_wo_x, g_x_post=m_g_x_post, g_ffn_pre=m_g_ffn_pre, w_up=m_w_up,
               w_down=m_w_down, g_ffn_post=m_g_ffn_post)
    var = dict(g_mix_pre=v_g_mix_pre, w_in=v_w_in, b_forget=v_b_forget, pool_w=v_pool_w, pool_scale=v_pool_scale,
               w_out=v_w_out, g_mix_post=v_g_mix_post, g_x_pre=v_g_x_pre, g_mem=v_g_mem, wq_x=v_wq_x,
               wkv_x=v_wkv_x, wo_x=v_wo_x, g_x_post=v_g_x_post, g_ffn_pre=v_g_ffn_pre, w_up=v_w_up,
               w_down=v_w_down, g_ffn_post=v_g_ffn_post)
    S = x.shape[1]
    xs = x.reshape(S, D_MODEL)
    mems = mem.reshape(MEM_LEN, D_MODEL)
    target = loss_target.reshape(S, D_MODEL)

    def small_params(l):
        return dict(
            g_mix_pre=_vec(g_mix_pre[l]), g_mix_post=_vec(g_mix_post[l]), g_x_pre=_vec(g_x_pre[l]),
            g_mem=_vec(g_mem[l]), g_x_post=_vec(g_x_post[l]), g_ffn_pre=_vec(g_ffn_pre[l]),
            g_ffn_post=_vec(g_ffn_post[l]), pool_scale=_vec(pool_scale[l]), pool_w=pool_w[l].astype(BF16),
            b_forget=jnp.pad(_vec(b_forget[l]), ((0, 0), (0, LANES - FOX_HEADS))))

    shard = {n: [w[n][l].astype(BF16) for l in range(DEPTH)] for n in REST}
    shard["w_in"] = [_w_in_travel(w_in[l].astype(BF16)) for l in range(DEPTH)]
    sps = [small_params(l) for l in range(DEPTH)]
    saved, weights = [], []
    h = xs
    (g_in,) = _exchange("gather_w_in", [shard["w_in"][0]])
    hn = _norm_fwd("norm_fwd", xs, sps[0]["g_mix_pre"])
    for l in range(DEPTH):
        travelling = [shard[n][l] for n in REST] + ([shard["w_in"][l + 1]] if l + 1 < DEPTH else [])
        g_next = sps[l + 1]["g_mix_pre"] if l + 1 < DEPTH else None
        h, hn, sv, W, g_in = _layer_fwd(h, hn, mems, sps[l], g_in, travelling, g_next)
        saved.append(sv)
        weights.append(W)
    dh, sq = _loss_fwd_bwd(h, target)
    loss = lax.psum(0.5 * sq[0, 0] / D_MODEL, ("x", "y", "c"))

    parts = [dict() for _ in range(DEPTH)]
    small_grads = [None] * DEPTH
    carried = []
    lower = _norm_bwd("norm_bwd_b", dh, saved[-1]["y"], sps[-1]["g_ffn_post"], None, BF16)
    for l in reversed(range(DEPTH)):
        dy, dg_ffn_post = lower
        below = (saved[l - 1]["y"], sps[l - 1]["g_ffn_post"]) if l > 0 else None
        dh, lower, got, got_carried, dw_in, gs = _layer_bwd(dh, dy, mems, saved[l], sps[l], weights[l], carried, below)
        gs["g_ffn_post"] = dg_ffn_post
        parts[l].update(got)
        if got_carried:
            parts[l + 1]["w_in"] = got_carried[0]
        carried = [dw_in]
        small_grads[l] = gs
    (parts[0]["w_in"],) = _exchange("scatter_dw_in", carried)
    grad_x = dh.reshape(1, S, D_MODEL)

    grads, deltas, new_m, new_v = {}, {}, {}, {}
    rows = dict(w_in=128, w_out=128, wq_x=128, wkv_x=256, wo_x=128, w_up=256, w_down=128)
    for l in range(DEPTH):
        parts[l]["w_in"] = jnp.swapaxes(parts[l]["w_in"][:, :W_IN_SHARD, :], 1, 2)
    for n in BIG:
        grads[n], deltas[n], new_m[n], new_v[n] = _adamw_big(
            "adamw_" + n, w[n], mom[n], var[n], [parts[l][n] for l in range(DEPTH)], rows[n])

    sg = {n: jnp.stack([small_grads[l][n].reshape(w[n].shape[1:]) for l in range(DEPTH)]) for n in SMALL}
    (sg_parts,) = _exchange("gather_small_grads", [_pack_small(sg)])
    outs = _adamw_small(_pack_small(w), _pack_small(mom), _pack_small(var), sg_parts)
    for d, packed in zip((grads, deltas, new_m, new_v), outs):
        d.update(_unpack_small(packed, w))

    return (loss, grad_x, *[grads[n] for n in W_NAMES], *[deltas[n] for n in W_NAMES],
            *[new_m[n] for n in W_NAMES], *[new_v[n] for n in W_NAMES])
```

```python
import math

import jax
import jax.numpy as jnp
from jax import lax
from jax.experimental import pallas as pl
from jax.experimental.pallas import tpu as pltpu

F32 = jnp.float32
BF16 = jnp.bfloat16

D_MODEL = 1024
DEPTH = 4
FOX_WIDTH = 512
FOX_HEADS = 8
FOX_HEAD_DIM = 64
POOL_WIDTH = 512
POOL_WINDOWS = (2, 4, 8, 16)
POOL_GROUP_DIM = 128
POOL_HALO = 16
MEM_LEN = 256
X_HEADS = 4
X_HEAD_DIM = 256
D_FF = 4096
EPS = 1e-6
IN_COLS = 2056
QKV_COLS = 3 * FOX_WIDTH
UF_COLS = 640
INP_COLS = QKV_COLS + UF_COLS
N_DEV = 8
LANES = 128

ADAM_LR = 0.001
ADAM_B1 = 0.9
ADAM_B2 = 0.999
ADAM_EPS = 1e-08
ADAM_WD = 0.01
ADAM_STEP = 10

VMEM_LIMIT = 56 * 1024 * 1024

W_NAMES = ['g_mix_pre', 'w_in', 'b_forget', 'pool_w', 'pool_scale', 'w_out', 'g_mix_post', 'g_x_pre', 'g_mem',
           'wq_x', 'wkv_x', 'wo_x', 'g_x_post', 'g_ffn_pre', 'w_up', 'w_down', 'g_ffn_post']
BIG = ['w_in', 'w_out', 'wq_x', 'wkv_x', 'wo_x', 'w_up', 'w_down']
SMALL = [n for n in W_NAMES if n not in BIG]

NN = (((1,), (0,)), ((), ()))
NT = (((1,), (1,)), ((), ()))
TN = (((0,), (0,)), ((), ()))


def _params(*sem):
    return pltpu.CompilerParams(dimension_semantics=sem, vmem_limit_bytes=VMEM_LIMIT)


def _row_tile(s):
    return min(s, 512)


def _product(a_ref, w_ref, kind, c0, pw):
    cols = slice(c0, c0 + pw)
    if kind == "nn":
        return lax.dot_general(a_ref[...], w_ref[:, cols], NN, preferred_element_type=F32)
    if kind == "nt":
        return lax.dot_general(a_ref[...], w_ref[cols, :], NT, preferred_element_type=F32)
    n = w_ref.shape[2]
    if kind == "nn3":
        assert pw == n and c0 % n == 0
        return lax.dot_general(a_ref[...], w_ref[c0 // n], NN, preferred_element_type=F32)
    r = None
    for j in range(w_ref.shape[0]):
        part = lax.dot_general(a_ref[:, j * n:(j + 1) * n], w_ref[j, cols, :], NT, preferred_element_type=F32)
        r = part if r is None else r + part
    return r


def _resident(w):
    return pl.BlockSpec(w.shape, lambda i, nd=w.ndim: (0,) * nd)


def _mm_rows(name, terms, outs, extra=None, piece=1024):
    M = terms[0][0].shape[0]
    tm = _row_tile(M)
    nterm = len(terms)
    n_extra = 0 if extra is None else 1
    groups = {}
    for idx, (_, c0, width, fn) in enumerate(outs):
        groups.setdefault((c0, width), []).append((idx, fn))

    def body(*refs):
        a_refs = refs[0:2 * nterm:2]
        w_refs = refs[1:2 * nterm:2]
        extra_refs = refs[2 * nterm:2 * nterm + n_extra]
        out_refs = refs[2 * nterm + n_extra:]
        for (g0, gw), members in groups.items():
            for c0 in range(g0, g0 + gw, piece):
                pw = min(piece, g0 + gw - c0)
                r = None
                for a_ref, w_ref, (_, w, kind) in zip(a_refs, w_refs, terms):
                    part = _product(a_ref, w_ref, kind, c0, pw)
                    r = part if r is None else r + part
                dst = slice(c0 - g0, c0 - g0 + pw)
                for idx, fn in members:
                    if fn == "relu2":
                        rp = jnp.maximum(r, 0.0)
                        val = rp * rp
                    elif fn == "drelu2":
                        val = r * (2.0 * jnp.maximum(extra_refs[0][:, dst].astype(F32), 0.0))
                    else:
                        val = r
                    out_refs[idx][:, dst] = val.astype(out_refs[idx].dtype)

    in_specs, ins = [], []
    for a, w, _ in terms:
        in_specs.append(pl.BlockSpec((tm, a.shape[1]), lambda i: (i, 0)))
        in_specs.append(pl.BlockSpec(w.shape, lambda i, nd=w.ndim: (0,) * nd))
        ins += [a, w]
    if extra is not None:
        in_specs.append(pl.BlockSpec((tm, extra.shape[1]), lambda i: (i, 0)))
        ins.append(extra)
    res = pl.pallas_call(
        body, name=name, grid=(M // tm,), in_specs=in_specs,
        out_specs=[pl.BlockSpec((tm, width), lambda i: (i, 0)) for _, _, width, _ in outs],
        out_shape=[jax.ShapeDtypeStruct((M, width), dt) for dt, _, width, _ in outs],
        compiler_params=_params("parallel"))(*ins)
    return res


def _mm1(name, a, w, kind, n_cols, dtype, piece=1024):
    return _mm_rows(name, [(a, w, kind)], [(dtype, 0, n_cols, "id")], piece=piece)[0]


def _mm_tn(name, a, b, out_dtype, shard_cols=None, piece=512):
    K, M = a.shape
    N = b.shape[1]
    tk = _row_tile(K)
    nk = K // tk
    piece = shard_cols or min(piece, N)

    def body(a_ref, b_ref, o_ref, acc, a_t):
        k = pl.program_id(0)

        @pl.when(k == 0)
        def _():
            acc[...] = jnp.zeros_like(acc)

        a_t[...] = jnp.transpose(a_ref[...])
        for c0 in range(0, N, piece):
            cols = slice(c0, min(c0 + piece, N))
            acc[:, cols] += lax.dot_general(a_t[...], b_ref[:, cols], NN, preferred_element_type=F32)

        @pl.when(k == nk - 1)
        def _():
            for c0 in range(0, N, piece):
                cols = slice(c0, min(c0 + piece, N))
                if shard_cols:
                    o_ref[c0 // piece] = acc[:, cols].astype(o_ref.dtype)
                else:
                    o_ref[:, cols] = acc[:, cols].astype(o_ref.dtype)

    out_dims = (N // shard_cols, M, shard_cols) if shard_cols else (M, N)
    return pl.pallas_call(
        body, name=name, grid=(nk,),
        in_specs=[pl.BlockSpec((tk, M), lambda k: (k, 0)), pl.BlockSpec((tk, N), lambda k: (k, 0))],
        out_specs=pl.BlockSpec(out_dims, lambda k, nd=len(out_dims): (0,) * nd),
        out_shape=jax.ShapeDtypeStruct(out_dims, out_dtype),
        scratch_shapes=[pltpu.VMEM((M, N), F32), pltpu.VMEM((M, tk), a.dtype)],
        compiler_params=_params("arbitrary"))(a, b)


def _norm_fwd(name, x, g):
    S, Dm = x.shape
    ts = _row_tile(S)

    def body(x_ref, g_ref, h_ref):
        xv = x_ref[...]
        r = lax.rsqrt(jnp.mean(xv * xv, axis=-1, keepdims=True) + EPS)
        h_ref[...] = ((xv * r) * g_ref[...]).astype(BF16)

    return pl.pallas_call(
        body, name=name, grid=(S // ts,),
        in_specs=[pl.BlockSpec((ts, Dm), lambda i: (i, 0)), pl.BlockSpec((1, Dm), lambda i: (0, 0))],
        out_specs=pl.BlockSpec((ts, Dm), lambda i: (i, 0)),
        out_shape=jax.ShapeDtypeStruct((S, Dm), BF16), compiler_params=_params("parallel"))(x, g)


def _mm_resid_norm(name, a, w, x, g, g_next):
    S, Dm = x.shape
    ts = _row_tile(S)
    has_next = g_next is not None

    def body(a_ref, w_ref, x_ref, g_ref, *rest):
        fv = _product(a_ref, w_ref, "nn", 0, Dm)
        r = lax.rsqrt(jnp.mean(fv * fv, axis=-1, keepdims=True) + EPS)
        xn = x_ref[...] + (fv * r) * g_ref[...]
        if has_next:
            gn_ref, f_ref, o_ref, h_ref = rest
            rn = lax.rsqrt(jnp.mean(xn * xn, axis=-1, keepdims=True) + EPS)
            h_ref[...] = ((xn * rn) * gn_ref[...]).astype(BF16)
        else:
            f_ref, o_ref = rest
        f_ref[...] = fv
        o_ref[...] = xn

    row = pl.BlockSpec((ts, Dm), lambda i: (i, 0))
    vec = pl.BlockSpec((1, Dm), lambda i: (0, 0))
    ins = [a, w, x, g] + ([g_next] if has_next else [])
    f32_rows = jax.ShapeDtypeStruct((S, Dm), F32)
    res = pl.pallas_call(
        body, name=name, grid=(S // ts,),
        in_specs=[pl.BlockSpec((ts, a.shape[1]), lambda i: (i, 0)), _resident(w), row, vec] + ([vec] if has_next else []),
        out_specs=[row, row] + ([row] if has_next else []),
        out_shape=[f32_rows, f32_rows] + ([jax.ShapeDtypeStruct((S, Dm), BF16)] if has_next else []),
        compiler_params=_params("parallel"))(*ins)
    return (res[0], res[1], res[2]) if has_next else (res[0], res[1], None)


def _rms_bwd(dov, yv, g):
    r = lax.rsqrt(jnp.mean(yv * yv, axis=-1, keepdims=True) + EPS)
    z = dov * g
    yr = yv * r
    return r * (z - yr * jnp.mean(yr * z, axis=-1, keepdims=True)), jnp.sum(dov * yr, axis=0, keepdims=True)


def _norm_bwd(name, dout, y, g, resid, out_dtype, below=None):
    S, Dm = y.shape
    ts = _row_tile(S)
    has_resid = resid is not None
    chained = below is not None
    produced = isinstance(dout, tuple)
    kind = dout[2] if produced else None

    def body(*refs):
        refs = list(refs)
        if produced:
            dov = _product(refs[0], refs[1], kind, 0, Dm)
            refs = refs[1:]
        else:
            dov = refs[0][...]
        y_ref, g_ref = refs[1:3]
        pos = 3
        r_ref = refs[pos] if has_resid else None
        pos += has_resid
        if chained:
            f_ref, gf_ref = refs[pos:pos + 2]
            pos += 2
        dy_ref, dg_ref = refs[pos:pos + 2]
        i = pl.program_id(0)
        dy, dg = _rms_bwd(dov, y_ref[...], g_ref[...])
        if has_resid:
            dy = dy + r_ref[...]
        dy_ref[...] = dy.astype(out_dtype)

        @pl.when(i == 0)
        def _():
            for ref in refs[pos + 1::2]:
                ref[...] = jnp.zeros_like(ref)

        dg_ref[...] += dg
        if chained:
            df_ref, dgf_ref = refs[pos + 2:pos + 4]
            df, dgf = _rms_bwd(dy, f_ref[...], gf_ref[...])
            df_ref[...] = df.astype(BF16)
            dgf_ref[...] += dgf

    row = pl.BlockSpec((ts, Dm), lambda i: (i, 0))
    vec = pl.BlockSpec((1, Dm), lambda i: (0, 0))
    if produced:
        ins = [dout[0], dout[1]]
        specs = [pl.BlockSpec((ts, dout[0].shape[1]), lambda i: (i, 0)), _resident(dout[1])]
    else:
        ins = [dout]
        specs = [row]
    ins += [y, g] + ([resid] if has_resid else []) + (list(below) if chained else [])
    specs += [row, vec] + ([row] if has_resid else []) + ([row, vec] if chained else [])
    vec_shape = jax.ShapeDtypeStruct((1, Dm), F32)
    return pl.pallas_call(
        body, name=name, grid=(S // ts,), in_specs=specs, out_specs=[row, vec] + ([row, vec] if chained else []),
        out_shape=[jax.ShapeDtypeStruct((S, Dm), out_dtype), vec_shape]
        + ([jax.ShapeDtypeStruct((S, Dm), BF16), vec_shape] if chained else []),
        compiler_params=_params("arbitrary"))(*ins)


def _loss_fwd_bwd(y, t):
    S, Dm = y.shape
    ts = _row_tile(S)

    def body(y_ref, t_ref, dy_ref, acc_ref):
        i = pl.program_id(0)
        e = y_ref[...] - t_ref[...]
        dy_ref[...] = e * (1.0 / Dm)

        @pl.when(i == 0)
        def _():
            acc_ref[...] = jnp.zeros_like(acc_ref)

        s = jnp.sum(jnp.sum(e * e, axis=1, keepdims=True), axis=0, keepdims=True)
        acc_ref[...] += s

    row = pl.BlockSpec((ts, Dm), lambda i: (i, 0))
    return pl.pallas_call(
        body, name="loss", grid=(S // ts,), in_specs=[row, row],
        out_specs=[row, pl.BlockSpec((8, LANES), lambda i: (0, 0))],
        out_shape=[jax.ShapeDtypeStruct((S, Dm), F32), jax.ShapeDtypeStruct((8, LANES), F32)],
        compiler_params=_params("arbitrary"))(y, t)


def _log_sigmoid(x):
    return jnp.minimum(x, 0.0) - jnp.log(1.0 + jnp.exp(-jnp.abs(x)))


def _gate_fwd(uf, bpad):
    S = uf.shape[0]
    T = _row_tile(S)

    def body(f_ref, b_ref, c_ref, carry):
        i = pl.program_id(0)

        @pl.when(i == 0)
        def _():
            carry[...] = jnp.zeros_like(carry)

        lf = _log_sigmoid(f_ref[...] + b_ref[...])
        r = lax.broadcasted_iota(jnp.int32, (T, T), 0)
        cidx = lax.broadcasted_iota(jnp.int32, (T, T), 1)
        tri = (cidx <= r).astype(F32)
        c = lax.dot_general(tri, lf, NN, precision=lax.Precision.HIGHEST, preferred_element_type=F32)
        c_ref[...] = c + carry[0:1, :]
        carry[...] = carry[...] + jnp.sum(lf, axis=0, keepdims=True)

    return pl.pallas_call(
        body, name="gate_fwd", grid=(S // T,),
        in_specs=[pl.BlockSpec((T, LANES), lambda i: (i, 4)), pl.BlockSpec((1, LANES), lambda i: (0, 0))],
        out_specs=pl.BlockSpec((T, LANES), lambda i: (i, 0)),
        out_shape=jax.ShapeDtypeStruct((S, LANES), F32),
        scratch_shapes=[pltpu.VMEM((8, LANES), F32)], compiler_params=_params("arbitrary"))(uf, bpad)


def _gate_bwd(dc, uf, bpad):
    S = uf.shape[0]
    T = _row_tile(S)
    nb = S // T

    def body(dc_ref, f_ref, b_ref, df_ref, db_ref, carry):
        i = pl.program_id(0)

        @pl.when(i == 0)
        def _():
            carry[...] = jnp.zeros_like(carry)
            db_ref[...] = jnp.zeros_like(db_ref)

        dcv = dc_ref[...]
        r = lax.broadcasted_iota(jnp.int32, (T, T), 0)
        cidx = lax.broadcasted_iota(jnp.int32, (T, T), 1)
        tri = (cidx >= r).astype(F32)
        dlf = lax.dot_general(tri, dcv, NN, precision=lax.Precision.HIGHEST, preferred_element_type=F32)
        dlf = dlf + carry[0:1, :]
        carry[...] = carry[...] + jnp.sum(dcv, axis=0, keepdims=True)
        fg = f_ref[...] + b_ref[...]
        dfg = dlf / (1.0 + jnp.exp(fg))
        df_ref[...] = dfg.astype(BF16)
        db_ref[...] += jnp.sum(dfg, axis=0, keepdims=True)

    return pl.pallas_call(
        body, name="gate_bwd", grid=(nb,),
        in_specs=[pl.BlockSpec((T, LANES), lambda i: (nb - 1 - i, 0)),
                  pl.BlockSpec((T, LANES), lambda i: (nb - 1 - i, 4)),
                  pl.BlockSpec((1, LANES), lambda i: (0, 0))],
        out_specs=[pl.BlockSpec((T, LANES), lambda i: (nb - 1 - i, 0)), pl.BlockSpec((1, LANES), lambda i: (0, 0))],
        out_shape=[jax.ShapeDtypeStruct((S, LANES), BF16), jax.ShapeDtypeStruct((1, LANES), F32)],
        scratch_shapes=[pltpu.VMEM((8, LANES), F32)], compiler_params=_params("arbitrary"))(dc, uf, bpad)


FOX_CHUNK = 32
FOX_CHUNK_BWD = 64
HEAD_PAIRS = FOX_HEADS // 2
PAIR = 2


def _masked(s, row0, col0, diagonal):
    if diagonal:
        row = row0 + lax.broadcasted_iota(jnp.int32, s.shape, 0)
        col = col0 + lax.broadcasted_iota(jnp.int32, s.shape, 1)
        s = jnp.where(col <= row, s, -jnp.inf)
    return s


def _causal_pairs(n, query_major):
    if query_major:
        pairs = [(q, k) for q in range(n) for k in range(q + 1)]
    else:
        pairs = [(q, k) for k in range(n) for q in range(k, n)]
    return (jnp.asarray([p[0] for p in pairs], jnp.int32), jnp.asarray([p[1] for p in pairs], jnp.int32))


def _lane_block(b):
    return slice(b * LANES, (b + 1) * LANES)


def _live_blocks(r, chunk, t, diagonal):
    n = t // LANES
    return list(range(min(n, ((r + 1) * chunk - 1) // LANES + 1))) if diagonal else list(range(n))


def _fold(op, xs):
    acc = xs[0]
    for x in xs[1:]:
        acc = op(acc, x)
    return acc


def _head_lanes(hh):
    lane = lax.broadcasted_iota(jnp.int32, (1, LANES), 1)
    return (lane < FOX_HEAD_DIM) if hh == 0 else (lane >= FOX_HEAD_DIM)


def _pick(first_head, a, b):
    return jnp.where(first_head, a, b)


def _fox_fwd(qkv, cT, comm):
    S = qkv.shape[0]
    t = _row_tile(S)
    n = S // t
    nc = len(comm)
    scale = 1.0 / math.sqrt(FOX_HEAD_DIM)
    chunk = min(FOX_CHUNK, t)
    per_head = 7
    q_tab, k_tab = _causal_pairs(n, True)
    steps = q_tab.shape[0]

    def body(qt_ref, kt_ref, q_ref, k_ref, v_ref, c_ref, *rest):
        comm_in = rest[:nc]
        o_ref, ob_ref, lse_ref = rest[nc:nc + 3]
        comm_out = rest[nc + 3:2 * nc + 3]
        scr = rest[2 * nc + 3:2 * nc + 3 + PAIR * per_head]
        sems = rest[2 * nc + 3 + PAIR * per_head:]
        hp = pl.program_id(0)
        step_id = pl.program_id(1)
        qi = qt_ref[step_id]
        ki = kt_ref[step_id]

        if nc:
            @pl.when((hp == 0) & (step_id == 0))
            def _():
                _Gather(comm_in, comm_out, *sems).start()

            @pl.when((hp == HEAD_PAIRS - 1) & (step_id == 0))
            def _():
                _Gather(comm_in, comm_out, *sems).pass_on()

        @pl.when(ki == 0)
        def _():
            for hh in range(PAIR):
                m_s, l_s, a_s, acc_s = scr[hh * per_head:hh * per_head + 4]
                m_s[...] = jnp.full_like(m_s, -jnp.inf)
                l_s[...] = jnp.zeros_like(l_s)
                acc_s[...] = jnp.zeros_like(acc_s)

        def step(diagonal):
            q2 = q_ref[...] * scale
            k2 = k_ref[...]
            v2 = v_ref[...]
            for hh in range(PAIR):
                s_s = scr[hh * per_head + 4]
                qm = jnp.where(_head_lanes(hh), q2, jnp.zeros_like(q2))
                s_s[...] = lax.dot_general(qm, k2, NT, preferred_element_type=F32)
            for hh in range(PAIR):
                m_s, l_s, a_s, acc_s, s_s, ph_s, pl_s = scr[hh * per_head:(hh + 1) * per_head]
                for r in range(t // chunk):
                    rows = slice(r * chunk, (r + 1) * chunk)
                    live = _live_blocks(r, chunk, t, diagonal)
                    blocks = [_masked(s_s[rows, _lane_block(b)] - c_ref[hh, :, _lane_block(b)], r * chunk,
                                      b * LANES, diagonal) for b in live]
                    m_prev = m_s[rows, :]
                    m_new = jnp.maximum(m_prev, jnp.max(_fold(jnp.maximum, blocks), axis=1, keepdims=True))
                    alpha = jnp.exp(m_prev - m_new)
                    ps = [jnp.exp(blk - m_new) for blk in blocks]
                    l_s[rows, :] = alpha * l_s[rows, :] + jnp.sum(_fold(jnp.add, ps), axis=1, keepdims=True)
                    m_s[rows, :] = m_new
                    a_s[rows, :] = alpha
                    for b, p in zip(live, ps):
                        p_hi = p.astype(BF16)
                        ph_s[rows, _lane_block(b)] = p_hi
                        pl_s[rows, _lane_block(b)] = (p - p_hi.astype(F32)).astype(BF16)
                    for b in range(len(live), t // LANES):
                        ph_s[rows, _lane_block(b)] = jnp.zeros((chunk, LANES), BF16)
                        pl_s[rows, _lane_block(b)] = jnp.zeros((chunk, LANES), BF16)
                pv = (lax.dot_general(ph_s[...], v2, NN, preferred_element_type=F32)
                      + lax.dot_general(pl_s[...], v2, NN, preferred_element_type=F32))
                acc_s[...] = a_s[...] * acc_s[...] + pv

        @pl.when(ki < qi)
        def _():
            step(False)

        @pl.when(ki == qi)
        def _():
            step(True)
            heads = []
            for hh in range(PAIR):
                m_s, l_s, a_s, acc_s = scr[hh * per_head:hh * per_head + 4]
                heads.append(acc_s[...] / l_s[...])
                lse_ref[hh] = m_s[...] + jnp.log(l_s[...])
            o2 = _pick(_head_lanes(0), heads[0], heads[1])
            o_ref[...] = o2
            ob_ref[...] = o2.astype(BF16)

        if nc:
            @pl.when((hp == HEAD_PAIRS - 1) & (step_id == steps - 1))
            def _():
                _Gather(comm_in, comm_out, *sems).finish()

    def q_cols(first_block):
        return pl.BlockSpec((t, LANES), lambda h, s, qt, kt: (qt[s], first_block + h))

    def k_cols(first_block):
        return pl.BlockSpec((t, LANES), lambda h, s, qt, kt: (kt[s], first_block + h))

    any_spec = pl.BlockSpec(memory_space=pl.ANY)
    head_scratch = [pltpu.VMEM((t, LANES), F32), pltpu.VMEM((t, LANES), F32), pltpu.VMEM((t, LANES), F32),
                    pltpu.VMEM((t, LANES), F32), pltpu.VMEM((t, t), F32), pltpu.VMEM((t, t), BF16),
                    pltpu.VMEM((t, t), BF16)]
    grid_spec = pltpu.PrefetchScalarGridSpec(
        num_scalar_prefetch=2, grid=(HEAD_PAIRS, steps),
        in_specs=[q_cols(0), k_cols(HEAD_PAIRS), k_cols(2 * HEAD_PAIRS),
                  pl.BlockSpec((PAIR, 1, t), lambda h, s, qt, kt: (h, 0, kt[s]))] + [any_spec] * nc,
        out_specs=[q_cols(0), q_cols(0),
                   pl.BlockSpec((PAIR, t, LANES), lambda h, s, qt, kt: (h, qt[s], 0))] + [any_spec] * nc,
        scratch_shapes=head_scratch * PAIR + _comm_scratch(nc))
    return pl.pallas_call(
        body, name="fox_fwd", grid_spec=grid_spec,
        out_shape=[jax.ShapeDtypeStruct((S, FOX_WIDTH), F32), jax.ShapeDtypeStruct((S, FOX_WIDTH), BF16),
                   jax.ShapeDtypeStruct((FOX_HEADS, S, LANES), F32)] + _comm_shapes(comm),
        compiler_params=_params("arbitrary", "arbitrary"))(q_tab, k_tab, qkv, qkv, qkv, cT, *comm)


def _fox_bwd(qkv, cT, o, lse, do, comm):
    S = qkv.shape[0]
    t = _row_tile(S)
    n = S // t
    nc = len(comm)
    scale = 1.0 / math.sqrt(FOX_HEAD_DIM)
    chunk = min(FOX_CHUNK_BWD, t)
    per_head = 6
    q_tab, k_tab = _causal_pairs(n, False)
    steps = q_tab.shape[0]

    def body(qt_ref, kt_ref, q_ref, k_ref, v_ref, c_ref, o_ref, do_ref, lse_ref, *rest):
        comm_in = rest[:nc]
        dq_ref, dk_ref, dv_ref, dc_ref = rest[nc:nc + 4]
        comm_out = rest[nc + 4:2 * nc + 4]
        dq_s, dk_s, dv_s = rest[2 * nc + 4:2 * nc + 7]
        scr = rest[2 * nc + 7:2 * nc + 7 + PAIR * per_head]
        sems = rest[2 * nc + 7 + PAIR * per_head:]
        hp = pl.program_id(0)
        step_id = pl.program_id(1)
        qi = qt_ref[step_id]
        ki = kt_ref[step_id]

        if nc:
            @pl.when((hp == 0) & (step_id == 0))
            def _():
                for cp in _comm_copies(comm_in, comm_out, *sems):
                    cp.start()

        @pl.when(step_id == 0)
        def _():
            dq_s[...] = jnp.zeros_like(dq_s)

        @pl.when(qi == ki)
        def _():
            dk_s[...] = jnp.zeros_like(dk_s)
            dv_s[...] = jnp.zeros_like(dv_s)
            for hh in range(PAIR):
                dc_s = scr[hh * per_head]
                dc_s[...] = jnp.zeros_like(dc_s)

        def step(diagonal):
            q2 = q_ref[...]
            k2 = k_ref[...]
            v2 = v_ref[...]
            do2 = do_ref[...]
            prod = do2.astype(F32) * o_ref[...]
            grads = []
            for hh in range(PAIR):
                dc_s, delta_s, s_s, dp_s, p_s, ds_s = scr[hh * per_head:(hh + 1) * per_head]
                mine = _head_lanes(hh)
                s_s[...] = lax.dot_general(jnp.where(mine, q2 * scale, jnp.zeros_like(q2)), k2, NT,
                                           preferred_element_type=F32)
                dp_s[...] = lax.dot_general(jnp.where(mine, do2, jnp.zeros_like(do2)), v2, NT,
                                            preferred_element_type=F32)
                delta_s[...] = jnp.broadcast_to(jnp.sum(jnp.where(mine, prod, 0.0), axis=1, keepdims=True),
                                                (t, LANES))
                dc8 = [jnp.zeros((8, LANES), F32) for _ in range(t // LANES)]
                for r in range(t // chunk):
                    rows = slice(r * chunk, (r + 1) * chunk)
                    lse = lse_ref[hh, rows, :]
                    delta = delta_s[rows, :]
                    live = _live_blocks(r, chunk, t, diagonal)
                    for b in live:
                        s = _masked(s_s[rows, _lane_block(b)] - c_ref[hh, :, _lane_block(b)], r * chunk, b * LANES,
                                    diagonal)
                        p = jnp.exp(s - lse)
                        ds = p * (dp_s[rows, _lane_block(b)] - delta)
                        p_s[rows, _lane_block(b)] = p.astype(BF16)
                        ds_s[rows, _lane_block(b)] = ds.astype(BF16)
                        dc8[b] = dc8[b] + jnp.sum(ds.reshape(chunk // 8, 8, LANES), axis=0)
                    for b in range(len(live), t // LANES):
                        p_s[rows, _lane_block(b)] = jnp.zeros((chunk, LANES), BF16)
                        ds_s[rows, _lane_block(b)] = jnp.zeros((chunk, LANES), BF16)
                for b in range(t // LANES):
                    dc_s[:, _lane_block(b)] += jnp.sum(dc8[b], axis=0, keepdims=True)
                dsb = ds_s[...]
                grads.append((lax.dot_general(p_s[...], do2, TN, preferred_element_type=F32),
                              lax.dot_general(dsb, k2, NN, preferred_element_type=F32),
                              lax.dot_general(dsb, q2, TN, preferred_element_type=F32)))
            first = _head_lanes(0)
            dv_s[...] += _pick(first, grads[0][0], grads[1][0])
            q_rows = pl.ds(pl.multiple_of(qi * t, t), t)
            dq_s[q_rows, :] += _pick(first, grads[0][1], grads[1][1]) * scale
            dk_s[...] += _pick(first, grads[0][2], grads[1][2]) * scale

        @pl.when(qi > ki)
        def _():
            step(False)

        @pl.when(qi == ki)
        def _():
            step(True)

        @pl.when(qi == n - 1)
        def _():
            dk_ref[...] = dk_s[...].astype(BF16)
            dv_ref[...] = dv_s[...].astype(BF16)
            for hh in range(PAIR):
                dc_ref[hh] = -scr[hh * per_head][...]

        @pl.when(step_id == steps - 1)
        def _():
            dq_ref[...] = dq_s[...].astype(BF16)

        if nc:
            @pl.when((hp == HEAD_PAIRS - 1) & (step_id == steps - 1))
            def _():
                for cp in _comm_copies(comm_in, comm_out, *sems):
                    cp.wait()

    def q_side(first_block):
        return pl.BlockSpec((t, LANES), lambda h, s, qt, kt: (qt[s], first_block + h))

    def k_side(first_block):
        return pl.BlockSpec((t, LANES), lambda h, s, qt, kt: (kt[s], first_block + h))

    any_spec = pl.BlockSpec(memory_space=pl.ANY)
    head_scratch = [pltpu.VMEM((1, t), F32), pltpu.VMEM((t, LANES), F32),
                    pltpu.VMEM((t, t), F32), pltpu.VMEM((t, t), F32), pltpu.VMEM((t, t), BF16),
                    pltpu.VMEM((t, t), BF16)]
    grad_shape = jax.ShapeDtypeStruct((S, FOX_WIDTH), BF16)
    grid_spec = pltpu.PrefetchScalarGridSpec(
        num_scalar_prefetch=2, grid=(HEAD_PAIRS, steps),
        in_specs=[q_side(0), k_side(HEAD_PAIRS), k_side(2 * HEAD_PAIRS),
                  pl.BlockSpec((PAIR, 1, t), lambda h, s, qt, kt: (h, 0, kt[s])), q_side(0), q_side(0),
                  pl.BlockSpec((PAIR, t, LANES), lambda h, s, qt, kt: (h, qt[s], 0))] + [any_spec] * nc,
        out_specs=[pl.BlockSpec((S, LANES), lambda h, s, qt, kt: (0, h)), k_side(0), k_side(0),
                   pl.BlockSpec((PAIR, 1, t), lambda h, s, qt, kt: (h, 0, kt[s]))] + [any_spec] * nc,
        scratch_shapes=[pltpu.VMEM((S, LANES), F32), pltpu.VMEM((t, LANES), F32), pltpu.VMEM((t, LANES), F32)]
        + head_scratch * PAIR + _comm_scratch(nc))
    return pl.pallas_call(
        body, name="fox_bwd", grid_spec=grid_spec,
        out_shape=[grad_shape, grad_shape, grad_shape, jax.ShapeDtypeStruct((FOX_HEADS, 1, S), F32)]
        + _comm_shapes(comm),
        compiler_params=_params("arbitrary", "arbitrary"))(q_tab, k_tab, qkv, qkv, qkv, cT, o, do, lse, *comm)


def _lanes(g):
    return slice(g * POOL_GROUP_DIM, (g + 1) * POOL_GROUP_DIM)


def _window_sum(e, win, back):
    rows = e.shape[0]
    s = e
    sh = 1
    while sh < win:
        s = s + pltpu.roll(s, sh if back else rows - sh, 0)
        sh *= 2
    return s


def _pooled(u_ref, up_ref, i, g, win, T):
    cur = u_ref[:, _lanes(g)]
    tail = jnp.where(i > 0, up_ref[T - POOL_HALO:T, _lanes(g)], 0.0)
    e = jnp.concatenate([tail, cur], axis=0)
    s = _window_sum(e, win, True)
    t_idx = i * T - POOL_HALO + lax.broadcasted_iota(jnp.int32, (T + POOL_HALO, POOL_GROUP_DIM), 0)
    cnt = jnp.clip(t_idx + 1, 1, win).astype(F32)
    return (s / cnt - e)[POOL_HALO:, :]


def _pool_fwd(uf, pw, ps):
    S = uf.shape[0]
    T = _row_tile(S)

    def body(u_ref, up_ref, w_ref, sc_ref, o_ref):
        i = pl.program_id(0)
        for g, win in enumerate(POOL_WINDOWS):
            pb = _pooled(u_ref, up_ref, i, g, win, T).astype(BF16)
            yv = lax.dot_general(pb, w_ref[g], NN, preferred_element_type=F32)
            o_ref[:, _lanes(g)] = (yv * sc_ref[:, _lanes(g)]).astype(BF16)

    return pl.pallas_call(
        body, name="pool_fwd", grid=(S // T,),
        in_specs=[pl.BlockSpec((T, POOL_WIDTH), lambda i: (i, 0)),
                  pl.BlockSpec((T, POOL_WIDTH), lambda i: (jnp.maximum(i - 1, 0), 0)),
                  pl.BlockSpec((4, POOL_GROUP_DIM, POOL_GROUP_DIM), lambda i: (0, 0, 0)),
                  pl.BlockSpec((1, POOL_WIDTH), lambda i: (0, 0))],
        out_specs=pl.BlockSpec((T, POOL_WIDTH), lambda i: (i, 0)),
        out_shape=jax.ShapeDtypeStruct((S, POOL_WIDTH), BF16), compiler_params=_params("parallel"))(uf, uf, pw, ps)


def _pool_bwd(uf, dpool, pw, ps):
    S = uf.shape[0]
    T = _row_tile(S)
    nb = S // T

    def body(u_ref, up_ref, d_ref, dn_ref, w_ref, sc_ref, du_ref, dw_ref, dsc_ref):
        i = pl.program_id(0)

        @pl.when(i == 0)
        def _():
            dw_ref[...] = jnp.zeros_like(dw_ref)
            dsc_ref[...] = jnp.zeros_like(dsc_ref)

        t_idx = i * T + lax.broadcasted_iota(jnp.int32, (T + POOL_HALO, POOL_GROUP_DIM), 0)
        for g, win in enumerate(POOL_WINDOWS):
            pb = _pooled(u_ref, up_ref, i, g, win, T).astype(BF16)
            w = w_ref[g]
            sc = sc_ref[:, _lanes(g)]
            yv = lax.dot_general(pb, w, NN, preferred_element_type=F32)
            dov = d_ref[:, _lanes(g)]
            dsc_ref[:, _lanes(g)] += jnp.sum(dov * yv, axis=0, keepdims=True)
            head = jnp.where(i < nb - 1, dn_ref[0:POOL_HALO, _lanes(g)], 0.0)
            dyb = (jnp.concatenate([dov, head], axis=0) * sc).astype(BF16)
            dw_ref[g] += lax.dot_general(pb, dyb[:T], TN, preferred_element_type=F32)
            dpl = lax.dot_general(dyb, w, NT, preferred_element_type=F32)
            cnt = jnp.minimum(t_idx + 1, win).astype(F32)
            a = _window_sum(dpl / cnt, win, False)
            du_ref[:, _lanes(g)] = (a - dpl)[:T].astype(BF16)

    return pl.pallas_call(
        body, name="pool_bwd", grid=(nb,),
        in_specs=[pl.BlockSpec((T, POOL_WIDTH), lambda i: (i, 0)),
                  pl.BlockSpec((T, POOL_WIDTH), lambda i: (jnp.maximum(i - 1, 0), 0)),
                  pl.BlockSpec((T, POOL_WIDTH), lambda i: (i, 0)),
                  pl.BlockSpec((T, POOL_WIDTH), lambda i: (jnp.minimum(i + 1, nb - 1), 0)),
                  pl.BlockSpec((4, POOL_GROUP_DIM, POOL_GROUP_DIM), lambda i: (0, 0, 0)),
                  pl.BlockSpec((1, POOL_WIDTH), lambda i: (0, 0))],
        out_specs=[pl.BlockSpec((T, POOL_WIDTH), lambda i: (i, 0)),
                   pl.BlockSpec((4, POOL_GROUP_DIM, POOL_GROUP_DIM), lambda i: (0, 0, 0)),
                   pl.BlockSpec((1, POOL_WIDTH), lambda i: (0, 0))],
        out_shape=[jax.ShapeDtypeStruct((S, POOL_WIDTH), BF16),
                   jax.ShapeDtypeStruct((4, POOL_GROUP_DIM, POOL_GROUP_DIM), F32),
                   jax.ShapeDtypeStruct((1, POOL_WIDTH), F32)],
        compiler_params=_params("arbitrary"))(uf, uf, dpool, dpool, pw, ps)


def _xhead(h):
    return slice(h * X_HEAD_DIM, (h + 1) * X_HEAD_DIM)


def _xvhead(h):
    return slice(D_MODEL + h * X_HEAD_DIM, D_MODEL + (h + 1) * X_HEAD_DIM)


X_CHUNK = 32


def _x_probs(s_ref, rows):
    blocks = [s_ref[rows, _lane_block(b)] * (1.0 / math.sqrt(X_HEAD_DIM)) for b in range(MEM_LEN // LANES)]
    m = jnp.max(_fold(jnp.maximum, blocks), axis=1, keepdims=True)
    es = [jnp.exp(blk - m) for blk in blocks]
    den = jnp.sum(_fold(jnp.add, es), axis=1, keepdims=True)
    return [e / den for e in es]


def _xattn_fwd(q, kv):
    S = q.shape[0]
    t = _row_tile(S)
    chunk = min(X_CHUNK, t)

    def body(q_ref, kv_ref, o_ref, s_s, p_s):
        for h in range(X_HEADS):
            s_s[...] = lax.dot_general(q_ref[:, _xhead(h)], kv_ref[:, _xhead(h)], NT, preferred_element_type=F32)
            for r in range(t // chunk):
                rows = slice(r * chunk, (r + 1) * chunk)
                for b, p in enumerate(_x_probs(s_s, rows)):
                    p_s[rows, _lane_block(b)] = p.astype(BF16)
            o_ref[:, _xhead(h)] = lax.dot_general(p_s[...], kv_ref[:, _xvhead(h)], NN,
                                                  preferred_element_type=F32).astype(BF16)

    return pl.pallas_call(
        body, name="xattn_fwd", grid=(S // t,),
        in_specs=[pl.BlockSpec((t, D_MODEL), lambda i: (i, 0)), pl.BlockSpec((MEM_LEN, 2 * D_MODEL), lambda i: (0, 0))],
        out_specs=pl.BlockSpec((t, D_MODEL), lambda i: (i, 0)),
        out_shape=jax.ShapeDtypeStruct((S, D_MODEL), BF16),
        scratch_shapes=[pltpu.VMEM((t, MEM_LEN), F32), pltpu.VMEM((t, MEM_LEN), BF16)],
        compiler_params=_params("parallel"))(q, kv)


def _xattn_bwd(q, kv, do):
    S = q.shape[0]
    t = _row_tile(S)
    nb = S // t
    scale = 1.0 / math.sqrt(X_HEAD_DIM)
    chunk = min(X_CHUNK, t)

    def body(q_ref, kv_ref, do_ref, dq_ref, dkv_ref, acc, s_s, dp_s, p_s, ds_s):
        i = pl.program_id(0)

        @pl.when(i == 0)
        def _():
            acc[...] = jnp.zeros_like(acc)

        for h in range(X_HEADS):
            qh = q_ref[:, _xhead(h)]
            kh = kv_ref[:, _xhead(h)]
            doh = do_ref[:, _xhead(h)]
            s_s[...] = lax.dot_general(qh, kh, NT, preferred_element_type=F32)
            dp_s[...] = lax.dot_general(doh, kv_ref[:, _xvhead(h)], NT, preferred_element_type=F32)
            for r in range(t // chunk):
                rows = slice(r * chunk, (r + 1) * chunk)
                ps = _x_probs(s_s, rows)
                dps = [dp_s[rows, _lane_block(b)] for b in range(len(ps))]
                inner = jnp.sum(_fold(jnp.add, [dp * p for dp, p in zip(dps, ps)]), axis=1, keepdims=True)
                for b, (dp, p) in enumerate(zip(dps, ps)):
                    p_s[rows, _lane_block(b)] = p.astype(BF16)
                    ds_s[rows, _lane_block(b)] = (p * (dp - inner)).astype(BF16)
            dsb = ds_s[...]
            acc[:, _xvhead(h)] += lax.dot_general(p_s[...], doh, TN, preferred_element_type=F32)
            dq_ref[:, _xhead(h)] = (lax.dot_general(dsb, kh, NN, preferred_element_type=F32) * scale).astype(BF16)
            acc[:, _xhead(h)] += lax.dot_general(dsb, qh, TN, preferred_element_type=F32) * scale

        @pl.when(i == nb - 1)
        def _():
            dkv_ref[...] = acc[...].astype(BF16)

    row = pl.BlockSpec((t, D_MODEL), lambda i: (i, 0))
    full = pl.BlockSpec((MEM_LEN, 2 * D_MODEL), lambda i: (0, 0))
    return pl.pallas_call(
        body, name="xattn_bwd", grid=(nb,), in_specs=[row, full, row], out_specs=[row, full],
        out_shape=[jax.ShapeDtypeStruct((S, D_MODEL), BF16), jax.ShapeDtypeStruct((MEM_LEN, 2 * D_MODEL), BF16)],
        scratch_shapes=[pltpu.VMEM((MEM_LEN, 2 * D_MODEL), F32), pltpu.VMEM((t, MEM_LEN), F32),
                        pltpu.VMEM((t, MEM_LEN), F32), pltpu.VMEM((t, MEM_LEN), BF16),
                        pltpu.VMEM((t, MEM_LEN), BF16)],
        compiler_params=_params("arbitrary"))(q, kv, do)


def _comm_shapes(arrs):
    return [jax.ShapeDtypeStruct((N_DEV,) + tuple(a.shape[-2:]), a.dtype) for a in arrs]


def _comm_scratch(n):
    if n == 0:
        return []
    return [pltpu.SemaphoreType.DMA((n, N_DEV - 1)), pltpu.SemaphoreType.DMA((n, N_DEV - 1)),
            pltpu.SemaphoreType.DMA((n,))]


def _comm_copies(ins, outs, send_sems, recv_sems, local_sems):
    x, y, c = lax.axis_index("x"), lax.axis_index("y"), lax.axis_index("c")
    me = 4 * x + 2 * y + c
    copies = []
    for w in range(len(ins)):
        src = ins[w] if len(ins[w].shape) == 2 else ins[w].at[me]
        copies.append(pltpu.make_async_copy(src, outs[w].at[me], local_sems.at[w]))
    for k in range(1, N_DEV):
        px = 1 - x if k & 4 else x
        py = 1 - y if k & 2 else y
        pc = 1 - c if k & 1 else c
        peer = 4 * px + 2 * py + pc
        for w in range(len(ins)):
            src = ins[w] if len(ins[w].shape) == 2 else ins[w].at[peer]
            copies.append(pltpu.make_async_remote_copy(
                src_ref=src, dst_ref=outs[w].at[me], send_sem=send_sems.at[w, k - 1],
                recv_sem=recv_sems.at[w, k - 1], device_id=(px, py, pc), device_id_type=pl.DeviceIdType.MESH))
    return copies


class _Gather:
    def __init__(self, ins, outs, send_sems, recv_sems, local_sems):
        x, y, c = lax.axis_index("x"), lax.axis_index("y"), lax.axis_index("c")
        me = 4 * x + 2 * y + c
        sibling = (x, y, 1 - c)
        self.local, self.mine, self.passed = [], [], []
        for w in range(len(ins)):
            def remote(idx, src, slot, dev, w=w):
                return pltpu.make_async_remote_copy(
                    src_ref=src, dst_ref=outs[w].at[slot], send_sem=send_sems.at[w, idx],
                    recv_sem=recv_sems.at[w, idx], device_id=dev, device_id_type=pl.DeviceIdType.MESH)

            self.local.append(pltpu.make_async_copy(ins[w], outs[w].at[me], local_sems.at[w]))
            mine, passed = [remote(0, ins[w], me, sibling)], []
            for j, (fx, fy) in enumerate(((0, 1), (1, 0), (1, 1))):
                px = 1 - x if fx else x
                py = 1 - y if fy else y
                slot = 4 * px + 2 * py + c
                mine.append(remote(1 + j, ins[w], me, (px, py, c)))
                passed.append(remote(4 + j, outs[w].at[slot], slot, sibling))
            self.mine.append(mine)
            self.passed.append(passed)

    def start(self):
        for cp in self.local:
            cp.start()
        for mine in self.mine:
            for cp in mine:
                cp.start()

    def pass_on(self):
        for mine, passed in zip(self.mine, self.passed):
            for j, cp in enumerate(passed):
                mine[1 + j].wait_recv()
                cp.start()

    def finish(self):
        for mine, passed in zip(self.mine, self.passed):
            mine[0].wait_recv()
            for cp in passed:
                cp.wait_recv()
            for cp in mine + passed:
                cp.wait_send()
        for cp in self.local:
            cp.wait()


def _exchange(name, arrs):
    n = len(arrs)
    gather = all(a.ndim == 2 for a in arrs)

    def body(*refs):
        if gather:
            g = _Gather(refs[:n], refs[n:2 * n], *refs[2 * n:])
            g.start()
            g.pass_on()
            g.finish()
            return
        copies = _comm_copies(refs[:n], refs[n:2 * n], *refs[2 * n:])
        for cp in copies:
            cp.start()
        for cp in copies:
            cp.wait()

    any_spec = pl.BlockSpec(memory_space=pl.ANY)
    return pl.pallas_call(
        body, name=name, in_specs=[any_spec] * n, out_specs=[any_spec] * n, out_shape=_comm_shapes(arrs),
        scratch_shapes=_comm_scratch(n))(*arrs)


def _adamw_math(w, g, m, v):
    m = ADAM_B1 * m + (1.0 - ADAM_B1) * g
    v = ADAM_B2 * v + (1.0 - ADAM_B2) * (g * g)
    m_hat = m / (1.0 - ADAM_B1 ** ADAM_STEP)
    v_hat = v / (1.0 - ADAM_B2 ** ADAM_STEP)
    delta = -ADAM_LR * (m_hat / (jnp.sqrt(v_hat) + ADAM_EPS) + ADAM_WD * w)
    return delta, m, v


def _sum_parts(p_ref):
    g = p_ref[0].astype(F32)
    for s in range(1, N_DEV):
        g = g + p_ref[s].astype(F32)
    return g


def _adamw_big(name, w, m, v, parts, tr):
    L, R, C = w.shape

    def body(w_ref, m_ref, v_ref, *rest):
        p_refs = rest[:L]
        g_ref, d_ref, nm_ref, nv_ref = rest[L:]
        layer = pl.program_id(0)
        for j in range(L):
            @pl.when(layer == j)
            def _(j=j):
                g = _sum_parts(p_refs[j])
                delta, nm, nv = _adamw_math(w_ref[...], g, m_ref[...], v_ref[...])
                g_ref[...] = g
                d_ref[...] = delta
                nm_ref[...] = nm
                nv_ref[...] = nv

    blk = pl.BlockSpec((None, tr, C), lambda l, i: (l, i, 0))

    def part_spec(j):
        return pl.BlockSpec((N_DEV, tr, C), lambda l, i: (0, jnp.where(l == j, i, 0), 0))

    shp = jax.ShapeDtypeStruct((L, R, C), F32)
    return pl.pallas_call(
        body, name=name, grid=(L, R // tr), in_specs=[blk, blk, blk] + [part_spec(j) for j in range(L)],
        out_specs=[blk] * 4, out_shape=[shp] * 4, compiler_params=_params("arbitrary", "arbitrary"))(w, m, v, *parts)


def _adamw_small(w, m, v, parts):
    R, C = w.shape

    def body(w_ref, m_ref, v_ref, p_ref, g_ref, d_ref, nm_ref, nv_ref):
        g = _sum_parts(p_ref)
        delta, nm, nv = _adamw_math(w_ref[...], g, m_ref[...], v_ref[...])
        g_ref[...] = g
        d_ref[...] = delta
        nm_ref[...] = nm
        nv_ref[...] = nv

    shp = jax.ShapeDtypeStruct((R, C), F32)
    return pl.pallas_call(body, name="adamw_small", out_shape=[shp] * 4,
                          compiler_params=pltpu.CompilerParams(vmem_limit_bytes=VMEM_LIMIT))(w, m, v, parts)


def _vec(a):
    return a.reshape(1, -1)


W_IN_SHARD = IN_COLS // N_DEV
W_IN_ROWS = 272


def _w_in_travel(a):
    pad = [(0, 0)] * (a.ndim - 2) + [(0, W_IN_ROWS - W_IN_SHARD), (0, 0)]
    return jnp.pad(jnp.swapaxes(a, -1, -2), pad)


def _unpack_w_in(g):
    full = jnp.transpose(g[:, :W_IN_SHARD, :], (2, 0, 1)).reshape(D_MODEL, IN_COLS)
    qkv = full[:, :QKV_COLS]
    f = full[:, QKV_COLS:QKV_COLS + FOX_HEADS]
    u = full[:, QKV_COLS + FOX_HEADS:]
    uf = jnp.concatenate([u, f, jnp.zeros((D_MODEL, UF_COLS - POOL_WIDTH - FOX_HEADS), g.dtype)], axis=1)
    return jnp.concatenate([qkv, uf], axis=1)


def _pack_dw_in(dwp):
    qkv = dwp[:, :QKV_COLS]
    u = dwp[:, QKV_COLS:QKV_COLS + POOL_WIDTH]
    f = dwp[:, QKV_COLS + POOL_WIDTH:QKV_COLS + POOL_WIDTH + FOX_HEADS]
    full = jnp.concatenate([qkv, f, u], axis=1)
    return _w_in_travel(jnp.transpose(full.reshape(D_MODEL, N_DEV, W_IN_SHARD), (1, 0, 2)))


REST = ['w_out', 'wq_x', 'wkv_x', 'wo_x', 'w_up', 'w_down']


def _layer_fwd(x0, h1, mem, sp, g_in, shards, g_next):
    S = x0.shape[0]
    sv = {"x0": x0}
    w_inp = _unpack_w_in(g_in)
    qkv, uf = _mm_rows("mm_in", [(h1, w_inp, "nn")],
                       [(BF16, 0, QKV_COLS, "id"), (F32, QKV_COLS, UF_COLS, "id")], piece=UF_COLS)
    c = _gate_fwd(uf, sp["b_forget"])
    cT = jnp.transpose(c[:, :FOX_HEADS]).reshape(FOX_HEADS, 1, S)
    o, ob, lse, *got = _fox_fwd(qkv, cT, shards)
    g_out, g_q, g_kv, g_o, g_up, g_down = got[:6]
    W = dict(inp=w_inp, out=g_out.reshape(D_MODEL, D_MODEL), q=g_q.reshape(D_MODEL, D_MODEL), kv=g_kv,
             o=g_o.reshape(D_MODEL, D_MODEL), up=g_up, down=g_down.reshape(D_FF, D_MODEL))
    pool = _pool_fwd(uf, sp["pool_w"], sp["pool_scale"])
    cat = jnp.concatenate([ob, pool], axis=1)
    mix, x1, h2 = _mm_resid_norm("mm_sq_norm", cat, W["out"], x0, sp["g_mix_post"], sp["g_x_pre"])
    mn = _norm_fwd("norm_mem", mem, sp["g_mem"])
    q2 = _mm1("mm_q", h2, W["q"], "nn", D_MODEL, BF16)
    kv = _mm1("mm_kv", mn, W["kv"], "nn3", 2 * D_MODEL, BF16, piece=2 * D_MODEL // N_DEV)
    o2 = _xattn_fwd(q2, kv)
    xo, x2, h3 = _mm_resid_norm("mm_sq_norm", o2, W["o"], x1, sp["g_x_post"], sp["g_ffn_pre"])
    up, act = _mm_rows("mm_up", [(h3, W["up"], "nn3")], [(BF16, 0, D_FF, "id"), (BF16, 0, D_FF, "relu2")],
                       piece=D_FF // N_DEV)
    y, x3, h_next = _mm_resid_norm("mm_down_norm" if g_next is not None else "mm_down_norm_last", act, W["down"], x2,
                                   sp["g_ffn_post"], g_next)
    sv.update(h1=h1, uf=uf, cT=cT, qkv=qkv, o=o, lse=lse, cat=cat, mix=mix, x1=x1, h2=h2, mn=mn, q2=q2, kv=kv,
              o2=o2, xo=xo, x2=x2, h3=h3, up=up, act=act, y=y)
    return x3, h_next, sv, W, (got[6] if len(got) > 6 else None)


def _layer_bwd(dx3, dy, mem, sv, sp, W, carried, below):
    S = dx3.shape[0]
    gs = {}
    gb = {}
    (dup,) = _mm_rows("mm_dup", [(dy, W["down"], "nt")], [(BF16, 0, D_FF, "drelu2")], extra=sv["up"])
    gb["w_down"] = _mm_tn("mm_dw_down", sv["act"], dy, BF16).reshape(N_DEV, D_FF // N_DEV, D_MODEL)
    gb["w_up"] = _mm_tn("mm_dw_up", sv["h3"], dup, BF16, shard_cols=D_FF // N_DEV)
    dx2, gs["g_ffn_pre"], dxo, gs["g_x_post"] = _norm_bwd(
        "mm_dh3_norm_bwd", (dup, W["up"], "nt3"), sv["x2"], sp["g_ffn_pre"], dx3, F32,
        below=(sv["xo"], sp["g_x_post"]))
    do2 = _mm1("mm_sq_t", dxo, W["o"], "nt", D_MODEL, BF16)
    gb["wo_x"] = _mm_tn("mm_dw_sq", sv["o2"], dxo, BF16).reshape(N_DEV, D_MODEL // N_DEV, D_MODEL)
    dq2, dkvb = _xattn_bwd(sv["q2"], sv["kv"], do2)
    gb["wq_x"] = _mm_tn("mm_dw_sq", sv["h2"], dq2, BF16).reshape(N_DEV, D_MODEL // N_DEV, D_MODEL)
    gb["wkv_x"] = _mm_tn("mm_dw_kv", sv["mn"], dkvb, BF16, shard_cols=2 * D_MODEL // N_DEV)
    dmn = _mm1("mm_dmn", dkvb, W["kv"], "nt3", D_MODEL, F32)
    _, gs["g_mem"] = _norm_bwd("norm_bwd_mem", dmn, mem, sp["g_mem"], None, BF16)
    dx1, gs["g_x_pre"], dmix, gs["g_mix_post"] = _norm_bwd(
        "mm_dh2_norm_bwd", (dq2, W["q"], "nt"), sv["x1"], sp["g_x_pre"], dx2, F32,
        below=(sv["mix"], sp["g_mix_post"]))
    doh, dpool = _mm_rows("mm_dcat", [(dmix, W["out"], "nt")],
                          [(BF16, 0, FOX_WIDTH, "id"), (F32, FOX_WIDTH, POOL_WIDTH, "id")])
    gb["w_out"] = _mm_tn("mm_dw_sq", sv["cat"], dmix, BF16).reshape(N_DEV, D_MODEL // N_DEV, D_MODEL)
    du, gs["pool_w"], gs["pool_scale"] = _pool_bwd(sv["uf"], dpool, sp["pool_w"], sp["pool_scale"])
    dq, dk, dv, dcT, *got = _fox_bwd(sv["qkv"], sv["cT"], sv["o"], sv["lse"], doh, [gb[n] for n in REST] + carried)
    dc = jnp.pad(jnp.transpose(dcT.reshape(FOX_HEADS, S)), ((0, 0), (0, LANES - FOX_HEADS)))
    dfg, db = _gate_bwd(dc, sv["uf"], sp["b_forget"])
    gs["b_forget"] = db[:, :FOX_HEADS]
    dproj = jnp.concatenate([dq, dk, dv, du, dfg], axis=1)
    dwp = _mm_tn("mm_dw_in", sv["h1"], dproj, BF16, piece=UF_COLS)
    dh1 = (dproj, W["inp"], "nt")
    if below is None:
        dx0, gs["g_mix_pre"] = _norm_bwd("mm_dh1_norm_bwd_first", dh1, sv["x0"], sp["g_mix_pre"], dx1, F32)
        lower = None
    else:
        dx0, gs["g_mix_pre"], *lower = _norm_bwd("mm_dh1_norm_bwd", dh1, sv["x0"], sp["g_mix_pre"], dx1, F32,
                                                 below=below)
    return dx0, lower, dict(zip(REST, got[:6])), got[6:], _pack_dw_in(dwp), gs


def _small_rows(shape):
    return -(-math.prod(shape) // (8 * LANES)) * 8


def _pack_small(d):
    blocks = []
    for n in SMALL:
        rows = _small_rows(d[n].shape)
        if d[n].shape[-1] == LANES:
            blocks.append(d[n].reshape(rows, LANES))
        else:
            flat = d[n].reshape(-1)
            blocks.append(jnp.pad(flat, (0, rows * LANES - flat.shape[0])).reshape(rows, LANES))
    return jnp.concatenate(blocks, axis=0)


def _unpack_small(packed, like):
    out = {}
    row = 0
    for n in SMALL:
        shape = like[n].shape
        rows = _small_rows(shape)
        block = packed[row:row + rows]
        out[n] = block.reshape(shape) if shape[-1] == LANES else block.reshape(-1)[:math.prod(shape)].reshape(shape)
        row += rows
    return out


def kernel(x, mem, g_mix_pre, w_in, b_forget, pool_w, pool_scale, w_out, g_mix_post, g_x_pre, g_mem, wq_x, wkv_x, wo_x, g_x_post, g_ffn_pre, w_up, w_down, g_ffn_post, loss_target, m_g_mix_pre, m_w_in, m_b_forget, m_pool_w, m_pool_scale, m_w_out, m_g_mix_post, m_g_x_pre, m_g_mem, m_wq_x, m_wkv_x, m_wo_x, m_g_x_post, m_g_ffn_pre, m_w_up, m_w_down, m_g_ffn_post, v_g_mix_pre, v_w_in, v_b_forget, v_pool_w, v_pool_scale, v_w_out, v_g_mix_post, v_g_x_pre, v_g_mem, v_wq_x, v_wkv_x, v_wo_x, v_g_x_post, v_g_ffn_pre, v_w_up, v_w_down, v_g_ffn_post):
    w = dict(g_mix_pre=g_mix_pre, w_in=w_in, b_forget=b_forget, pool_w=pool_w, pool_scale=pool_scale, w_out=w_out,
             g_mix_post=g_mix_post, g_x_pre=g_x_pre, g_mem=g_mem, wq_x=wq_x, wkv_x=wkv_x, wo_x=wo_x,
             g_x_post=g_x_post, g_ffn_pre=g_ffn_pre, w_up=w_up, w_down=w_down, g_ffn_post=g_ffn_post)
    mom = dict(g_mix_pre=m_g_mix_pre, w_in=m_w_in, b_forget=m_b_forget, pool_w=m_pool_w, pool_scale=m_pool_scale,
               w_out=m_w_out, g_mix_post=m_g_mix_post, g_x_pre=m_g_x_pre, g_mem=m_g_mem, wq_x=m_wq_x,
               wkv_x=m_wkv_x, wo_x=m_wo_x, g_x_post=m_g_x_post, g_ffn_pre=m_g_ffn_pre, w_up=m_w_up,
               w_down=m_w_down, g_ffn_post=m_g_ffn_post)
    var = dict(g_mix_pre=v_g_mix_pre, w_in=v_w_in, b_forget=v_b_forget, pool_w=v_pool_w, pool_scale=v_pool_scale,
               w_out=v_w_out, g_mix_post=v_g_mix_post, g_x_pre=v_g_x_pre, g_mem=v_g_mem, wq_x=v_wq_x,
               wkv_x=v_wkv_x, wo_x=v_wo_x, g_x_post=v_g_x_post, g_ffn_pre=v_g_ffn_pre, w_up=v_w_up,
               w_down=v_w_down, g_ffn_post=v_g_ffn_post)
    S = x.shape[1]
    xs = x.reshape(S, D_MODEL)
    mems = mem.reshape(MEM_LEN, D_MODEL)
    target = loss_target.reshape(S, D_MODEL)

    def small_params(l):
        return dict(
            g_mix_pre=_vec(g_mix_pre[l]), g_mix_post=_vec(g_mix_post[l]), g_x_pre=_vec(g_x_pre[l]),
            g_mem=_vec(g_mem[l]), g_x_post=_vec(g_x_post[l]), g_ffn_pre=_vec(g_ffn_pre[l]),
            g_ffn_post=_vec(g_ffn_post[l]), pool_scale=_vec(pool_scale[l]), pool_w=pool_w[l].astype(BF16),
            b_forget=jnp.pad(_vec(b_forget[l]), ((0, 0), (0, LANES - FOX_HEADS))))

    shard = {n: [w[n][l].astype(BF16) for l in range(DEPTH)] for n in REST}
    shard["w_in"] = [_w_in_travel(w_in[l].astype(BF16)) for l in range(DEPTH)]
    sps = [small_params(l) for l in range(DEPTH)]
    saved, weights = [], []
    h = xs
    (g_in,) = _exchange("gather_w_in", [shard["w_in"][0]])
    hn = _norm_fwd("norm_fwd", xs, sps[0]["g_mix_pre"])
    for l in range(DEPTH):
        travelling = [shard[n][l] for n in REST] + ([shard["w_in"][l + 1]] if l + 1 < DEPTH else [])
        g_next = sps[l + 1]["g_mix_pre"] if l + 1 < DEPTH else None
        h, hn, sv, W, g_in = _layer_fwd(h, hn, mems, sps[l], g_in, travelling, g_next)
        saved.append(sv)
        weights.append(W)
    dh, sq = _loss_fwd_bwd(h, target)
    loss = lax.psum(0.5 * sq[0, 0] / D_MODEL, ("x", "y", "c"))

    parts = [dict() for _ in range(DEPTH)]
    small_grads = [None] * DEPTH
    carried = []
    lower = _norm_bwd("norm_bwd_b", dh, saved[-1]["y"], sps[-1]["g_ffn_post"], None, BF16)
    for l in reversed(range(DEPTH)):
        dy, dg_ffn_post = lower
        below = (saved[l - 1]["y"], sps[l - 1]["g_ffn_post"]) if l > 0 else None
        dh, lower, got, got_carried, dw_in, gs = _layer_bwd(dh, dy, mems, saved[l], sps[l], weights[l], carried, below)
        gs["g_ffn_post"] = dg_ffn_post
        parts[l].update(got)
        if got_carried:
            parts[l + 1]["w_in"] = got_carried[0]
        carried = [dw_in]
        small_grads[l] = gs
    (parts[0]["w_in"],) = _exchange("scatter_dw_in", carried)
    grad_x = dh.reshape(1, S, D_MODEL)

    grads, deltas, new_m, new_v = {}, {}, {}, {}
    rows = dict(w_in=128, w_out=128, wq_x=128, wkv_x=256, wo_x=128, w_up=256, w_down=128)
    for l in range(DEPTH):
        parts[l]["w_in"] = jnp.swapaxes(parts[l]["w_in"][:, :W_IN_SHARD, :], 1, 2)
    for n in BIG:
        grads[n], deltas[n], new_m[n], new_v[n] = _adamw_big(
            "adamw_" + n, w[n], mom[n], var[n], [parts[l][n] for l in range(DEPTH)], rows[n])

    sg = {n: jnp.stack([small_grads[l][n].reshape(w[n].shape[1:]) for l in range(DEPTH)]) for n in SMALL}
    (sg_parts,) = _exchange("gather_small_grads", [_pack_small(sg)])
    outs = _adamw_small(_pack_small(w), _pack_small(mom), _pack_small(var), sg_parts)
    for d, packed in zip((grads, deltas, new_m, new_v), outs):
        d.update(_unpack_small(packed, w))

    return (loss, grad_x, *[grads[n] for n in W_NAMES], *[deltas[n] for n in W_NAMES],
            *[new_m[n] for n in W_NAMES], *[new_v[n] for n in W_NAMES])
```

```python
import math

import jax
import jax.numpy as jnp
from jax import lax
from jax.experimental import pallas as pl
from jax.experimental.pallas import tpu as pltpu

F32 = jnp.float32
BF16 = jnp.bfloat16

D_MODEL = 1024
DEPTH = 4
FOX_WIDTH = 512
FOX_HEADS = 8
FOX_HEAD_DIM = 64
POOL_WIDTH = 512
POOL_WINDOWS = (2, 4, 8, 16)
POOL_GROUP_DIM = 128
POOL_HALO = 16
MEM_LEN = 256
X_HEADS = 4
X_HEAD_DIM = 256
D_FF = 4096
EPS = 1e-6
IN_COLS = 2056
QKV_COLS = 3 * FOX_WIDTH
UF_COLS = 640
INP_COLS = QKV_COLS + UF_COLS
N_DEV = 8
LANES = 128

ADAM_LR = 0.001
ADAM_B1 = 0.9
ADAM_B2 = 0.999
ADAM_EPS = 1e-08
ADAM_WD = 0.01
ADAM_STEP = 10

VMEM_LIMIT = 56 * 1024 * 1024

W_NAMES = ['g_mix_pre', 'w_in', 'b_forget', 'pool_w', 'pool_scale', 'w_out', 'g_mix_post', 'g_x_pre', 'g_mem',
           'wq_x', 'wkv_x', 'wo_x', 'g_x_post', 'g_ffn_pre', 'w_up', 'w_down', 'g_ffn_post']
BIG = ['w_in', 'w_out', 'wq_x', 'wkv_x', 'wo_x', 'w_up', 'w_down']
SMALL = [n for n in W_NAMES if n not in BIG]

NN = (((1,), (0,)), ((), ()))
NT = (((1,), (1,)), ((), ()))
TN = (((0,), (0,)), ((), ()))


def _params(*sem):
    return pltpu.CompilerParams(dimension_semantics=sem, vmem_limit_bytes=VMEM_LIMIT)


def _row_tile(s):
    return min(s, 512)


def _product(a_ref, w_ref, kind, c0, pw):
    cols = slice(c0, c0 + pw)
    if kind == "nn":
        return lax.dot_general(a_ref[...], w_ref[:, cols], NN, preferred_element_type=F32)
    if kind == "nt":
        return lax.dot_general(a_ref[...], w_ref[cols, :], NT, preferred_element_type=F32)
    n = w_ref.shape[2]
    if kind == "nn3":
        assert pw == n and c0 % n == 0
        return lax.dot_general(a_ref[...], w_ref[c0 // n], NN, preferred_element_type=F32)
    r = None
    for j in range(w_ref.shape[0]):
        part = lax.dot_general(a_ref[:, j * n:(j + 1) * n], w_ref[j, cols, :], NT, preferred_element_type=F32)
        r = part if r is None else r + part
    return r


def _resident(w):
    return pl.BlockSpec(w.shape, lambda i, nd=w.ndim: (0,) * nd)


def _mm_rows(name, terms, outs, extra=None, piece=1024):
    M = terms[0][0].shape[0]
    tm = _row_tile(M)
    nterm = len(terms)
    n_extra = 0 if extra is None else 1
    groups = {}
    for idx, (_, c0, width, fn) in enumerate(outs):
        groups.setdefault((c0, width), []).append((idx, fn))

    def body(*refs):
        a_refs = refs[0:2 * nterm:2]
        w_refs = refs[1:2 * nterm:2]
        extra_refs = refs[2 * nterm:2 * nterm + n_extra]
        out_refs = refs[2 * nterm + n_extra:]
        for (g0, gw), members in groups.items():
            for c0 in range(g0, g0 + gw, piece):
                pw = min(piece, g0 + gw - c0)
                r = None
                for a_ref, w_ref, (_, w, kind) in zip(a_refs, w_refs, terms):
                    part = _product(a_ref, w_ref, kind, c0, pw)
                    r = part if r is None else r + part
                dst = slice(c0 - g0, c0 - g0 + pw)
                for idx, fn in members:
                    if fn == "relu2":
                        rp = jnp.maximum(r, 0.0)
                        val = rp * rp
                    elif fn == "drelu2":
                        val = r * (2.0 * jnp.maximum(extra_refs[0][:, dst].astype(F32), 0.0))
                    else:
                        val = r
                    out_refs[idx][:, dst] = val.astype(out_refs[idx].dtype)

    in_specs, ins = [], []
    for a, w, _ in terms:
        in_specs.append(pl.BlockSpec((tm, a.shape[1]), lambda i: (i, 0)))
        in_specs.append(pl.BlockSpec(w.shape, lambda i, nd=w.ndim: (0,) * nd))
        ins += [a, w]
    if extra is not None:
        in_specs.append(pl.BlockSpec((tm, extra.shape[1]), lambda i: (i, 0)))
        ins.append(extra)
    res = pl.pallas_call(
        body, name=name, grid=(M // tm,), in_specs=in_specs,
        out_specs=[pl.BlockSpec((tm, width), lambda i: (i, 0)) for _, _, width, _ in outs],
        out_shape=[jax.ShapeDtypeStruct((M, width), dt) for dt, _, width, _ in outs],
        compiler_params=_params("parallel"))(*ins)
    return res


def _mm1(name, a, w, kind, n_cols, dtype, piece=1024):
    return _mm_rows(name, [(a, w, kind)], [(dtype, 0, n_cols, "id")], piece=piece)[0]


def _mm_tn(name, a, b, out_dtype, shard_cols=None, piece=512):
    K, M = a.shape
    N = b.shape[1]
    tk = _row_tile(K)
    nk = K // tk
    piece = shard_cols or min(piece, N)

    def body(a_ref, b_ref, o_ref, acc, a_t):
        k = pl.program_id(0)

        @pl.when(k == 0)
        def _():
            acc[...] = jnp.zeros_like(acc)

        a_t[...] = jnp.transpose(a_ref[...])
        for c0 in range(0, N, piece):
            cols = slice(c0, min(c0 + piece, N))
            acc[:, cols] += lax.dot_general(a_t[...], b_ref[:, cols], NN, preferred_element_type=F32)

        @pl.when(k == nk - 1)
        def _():
            for c0 in range(0, N, piece):
                cols = slice(c0, min(c0 + piece, N))
                if shard_cols:
                    o_ref[c0 // piece] = acc[:, cols].astype(o_ref.dtype)
                else:
                    o_ref[:, cols] = acc[:, cols].astype(o_ref.dtype)

    out_dims = (N // shard_cols, M, shard_cols) if shard_cols else (M, N)
    return pl.pallas_call(
        body, name=name, grid=(nk,),
        in_specs=[pl.BlockSpec((tk, M), lambda k: (k, 0)), pl.BlockSpec((tk, N), lambda k: (k, 0))],
        out_specs=pl.BlockSpec(out_dims, lambda k, nd=len(out_dims): (0,) * nd),
        out_shape=jax.ShapeDtypeStruct(out_dims, out_dtype),
        scratch_shapes=[pltpu.VMEM((M, N), F32), pltpu.VMEM((M, tk), a.dtype)],
        compiler_params=_params("arbitrary"))(a, b)


def _mm_tn_whole(name, a, b, out_dtype, piece, shard_out=False, transpose_out=False):
    K, M = a.shape
    N = b.shape[1]
    tk = _row_tile(K)

    def body(a_ref, b_ref, o_ref, a_t):
        @pl.when(pl.program_id(0) == 0)
        def _():
            for k0 in range(0, K, tk):
                a_t[:, k0:k0 + tk] = jnp.transpose(a_ref[k0:k0 + tk, :])

        r = lax.dot_general(a_t[...], b_ref[...], NN, preferred_element_type=F32)
        o_ref[...] = (jnp.transpose(r) if transpose_out else r).astype(o_ref.dtype)

    if shard_out:
        out_dims, out_spec = (N // piece, M, piece), pl.BlockSpec((None, M, piece), lambda j: (j, 0, 0))
    elif transpose_out:
        out_dims, out_spec = (N, M), pl.BlockSpec((piece, M), lambda j: (j, 0))
    else:
        out_dims, out_spec = (M, N), pl.BlockSpec((M, piece), lambda j: (0, j))
    return pl.pallas_call(
        body, name=name, grid=(N // piece,),
        in_specs=[pl.BlockSpec((K, M), lambda j: (0, 0)), pl.BlockSpec((K, piece), lambda j: (0, j))],
        out_specs=out_spec, out_shape=jax.ShapeDtypeStruct(out_dims, out_dtype),
        scratch_shapes=[pltpu.VMEM((M, K), a.dtype)], compiler_params=_params("arbitrary"))(a, b)


def _norm_fwd(name, x, g):
    S, Dm = x.shape
    ts = _row_tile(S)

    def body(x_ref, g_ref, h_ref):
        xv = x_ref[...]
        r = lax.rsqrt(jnp.mean(xv * xv, axis=-1, keepdims=True) + EPS)
        h_ref[...] = ((xv * r) * g_ref[...]).astype(BF16)

    return pl.pallas_call(
        body, name=name, grid=(S // ts,),
        in_specs=[pl.BlockSpec((ts, Dm), lambda i: (i, 0)), pl.BlockSpec((1, Dm), lambda i: (0, 0))],
        out_specs=pl.BlockSpec((ts, Dm), lambda i: (i, 0)),
        out_shape=jax.ShapeDtypeStruct((S, Dm), BF16), compiler_params=_params("parallel"))(x, g)


def _mm_resid_norm(name, a, w, x, g, g_next):
    S, Dm = x.shape
    ts = _row_tile(S)
    has_next = g_next is not None

    def body(a_ref, w_ref, x_ref, g_ref, *rest):
        fv = _product(a_ref, w_ref, "nn", 0, Dm)
        r = lax.rsqrt(jnp.mean(fv * fv, axis=-1, keepdims=True) + EPS)
        xn = x_ref[...] + (fv * r) * g_ref[...]
        if has_next:
            gn_ref, f_ref, o_ref, h_ref = rest
            rn = lax.rsqrt(jnp.mean(xn * xn, axis=-1, keepdims=True) + EPS)
            h_ref[...] = ((xn * rn) * gn_ref[...]).astype(BF16)
        else:
            f_ref, o_ref = rest
        f_ref[...] = fv
        o_ref[...] = xn

    row = pl.BlockSpec((ts, Dm), lambda i: (i, 0))
    vec = pl.BlockSpec((1, Dm), lambda i: (0, 0))
    ins = [a, w, x, g] + ([g_next] if has_next else [])
    f32_rows = jax.ShapeDtypeStruct((S, Dm), F32)
    res = pl.pallas_call(
        body, name=name, grid=(S // ts,),
        in_specs=[pl.BlockSpec((ts, a.shape[1]), lambda i: (i, 0)), _resident(w), row, vec] + ([vec] if has_next else []),
        out_specs=[row, row] + ([row] if has_next else []),
        out_shape=[f32_rows, f32_rows] + ([jax.ShapeDtypeStruct((S, Dm), BF16)] if has_next else []),
        compiler_params=_params("parallel"))(*ins)
    return (res[0], res[1], res[2]) if has_next else (res[0], res[1], None)


def _rms_bwd(dov, yv, g):
    r = lax.rsqrt(jnp.mean(yv * yv, axis=-1, keepdims=True) + EPS)
    z = dov * g
    yr = yv * r
    return r * (z - yr * jnp.mean(yr * z, axis=-1, keepdims=True)), jnp.sum(dov * yr, axis=0, keepdims=True)


def _norm_bwd(name, dout, y, g, resid, out_dtype, below=None):
    S, Dm = y.shape
    ts = _row_tile(S)
    has_resid = resid is not None
    chained = below is not None
    produced = isinstance(dout, tuple)
    kind = dout[2] if produced else None

    def body(*refs):
        refs = list(refs)
        if produced:
            dov = _product(refs[0], refs[1], kind, 0, Dm)
            refs = refs[1:]
        else:
            dov = refs[0][...]
        y_ref, g_ref = refs[1:3]
        pos = 3
        r_ref = refs[pos] if has_resid else None
        pos += has_resid
        if chained:
            f_ref, gf_ref = refs[pos:pos + 2]
            pos += 2
        dy_ref, dg_ref = refs[pos:pos + 2]
        i = pl.program_id(0)
        dy, dg = _rms_bwd(dov, y_ref[...], g_ref[...])
        if has_resid:
            dy = dy + r_ref[...]
        dy_ref[...] = dy.astype(out_dtype)

        @pl.when(i == 0)
        def _():
            for ref in refs[pos + 1::2]:
                ref[...] = jnp.zeros_like(ref)

        dg_ref[...] += dg
        if chained:
            df_ref, dgf_ref = refs[pos + 2:pos + 4]
            df, dgf = _rms_bwd(dy, f_ref[...], gf_ref[...])
            df_ref[...] = df.astype(BF16)
            dgf_ref[...] += dgf

    row = pl.BlockSpec((ts, Dm), lambda i: (i, 0))
    vec = pl.BlockSpec((1, Dm), lambda i: (0, 0))
    if produced:
        ins = [dout[0], dout[1]]
        specs = [pl.BlockSpec((ts, dout[0].shape[1]), lambda i: (i, 0)), _resident(dout[1])]
    else:
        ins = [dout]
        specs = [row]
    ins += [y, g] + ([resid] if has_resid else []) + (list(below) if chained else [])
    specs += [row, vec] + ([row] if has_resid else []) + ([row, vec] if chained else [])
    vec_shape = jax.ShapeDtypeStruct((1, Dm), F32)
    return pl.pallas_call(
        body, name=name, grid=(S // ts,), in_specs=specs, out_specs=[row, vec] + ([row, vec] if chained else []),
        out_shape=[jax.ShapeDtypeStruct((S, Dm), out_dtype), vec_shape]
        + ([jax.ShapeDtypeStruct((S, Dm), BF16), vec_shape] if chained else []),
        compiler_params=_params("arbitrary"))(*ins)


def _loss_fwd_bwd(y, t):
    S, Dm = y.shape
    ts = _row_tile(S)

    def body(y_ref, t_ref, dy_ref, acc_ref):
        i = pl.program_id(0)
        e = y_ref[...] - t_ref[...]
        dy_ref[...] = e * (1.0 / Dm)

        @pl.when(i == 0)
        def _():
            acc_ref[...] = jnp.zeros_like(acc_ref)

        s = jnp.sum(jnp.sum(e * e, axis=1, keepdims=True), axis=0, keepdims=True)
        acc_ref[...] += s

    row = pl.BlockSpec((ts, Dm), lambda i: (i, 0))
    return pl.pallas_call(
        body, name="loss", grid=(S // ts,), in_specs=[row, row],
        out_specs=[row, pl.BlockSpec((8, LANES), lambda i: (0, 0))],
        out_shape=[jax.ShapeDtypeStruct((S, Dm), F32), jax.ShapeDtypeStruct((8, LANES), F32)],
        compiler_params=_params("arbitrary"))(y, t)


def _log_sigmoid(x):
    return jnp.minimum(x, 0.0) - jnp.log(1.0 + jnp.exp(-jnp.abs(x)))


def _gate_fwd(uf, bpad):
    S = uf.shape[0]
    T = _row_tile(S)

    def body(f_ref, b_ref, c_ref, carry):
        i = pl.program_id(0)

        @pl.when(i == 0)
        def _():
            carry[...] = jnp.zeros_like(carry)

        lf = _log_sigmoid(f_ref[...] + b_ref[...])
        r = lax.broadcasted_iota(jnp.int32, (T, T), 0)
        cidx = lax.broadcasted_iota(jnp.int32, (T, T), 1)
        tri = (cidx <= r).astype(F32)
        c = lax.dot_general(tri, lf, NN, precision=lax.Precision.HIGHEST, preferred_element_type=F32)
        c_ref[...] = c + carry[0:1, :]
        carry[...] = carry[...] + jnp.sum(lf, axis=0, keepdims=True)

    return pl.pallas_call(
        body, name="gate_fwd", grid=(S // T,),
        in_specs=[pl.BlockSpec((T, LANES), lambda i: (i, 4)), pl.BlockSpec((1, LANES), lambda i: (0, 0))],
        out_specs=pl.BlockSpec((T, LANES), lambda i: (i, 0)),
        out_shape=jax.ShapeDtypeStruct((S, LANES), F32),
        scratch_shapes=[pltpu.VMEM((8, LANES), F32)], compiler_params=_params("arbitrary"))(uf, bpad)


def _gate_bwd(dc, uf, bpad):
    S = uf.shape[0]
    T = _row_tile(S)
    nb = S // T

    def body(dc_ref, f_ref, b_ref, df_ref, db_ref, carry):
        i = pl.program_id(0)

        @pl.when(i == 0)
        def _():
            carry[...] = jnp.zeros_like(carry)
            db_ref[...] = jnp.zeros_like(db_ref)

        dcv = dc_ref[...]
        r = lax.broadcasted_iota(jnp.int32, (T, T), 0)
        cidx = lax.broadcasted_iota(jnp.int32, (T, T), 1)
        tri = (cidx >= r).astype(F32)
        dlf = lax.dot_general(tri, dcv, NN, precision=lax.Precision.HIGHEST, preferred_element_type=F32)
        dlf = dlf + carry[0:1, :]
        carry[...] = carry[...] + jnp.sum(dcv, axis=0, keepdims=True)
        fg = f_ref[...] + b_ref[...]
        dfg = dlf / (1.0 + jnp.exp(fg))
        df_ref[...] = dfg.astype(BF16)
        db_ref[...] += jnp.sum(dfg, axis=0, keepdims=True)

    return pl.pallas_call(
        body, name="gate_bwd", grid=(nb,),
        in_specs=[pl.BlockSpec((T, LANES), lambda i: (nb - 1 - i, 0)),
                  pl.BlockSpec((T, LANES), lambda i: (nb - 1 - i, 4)),
                  pl.BlockSpec((1, LANES), lambda i: (0, 0))],
        out_specs=[pl.BlockSpec((T, LANES), lambda i: (nb - 1 - i, 0)), pl.BlockSpec((1, LANES), lambda i: (0, 0))],
        out_shape=[jax.ShapeDtypeStruct((S, LANES), BF16), jax.ShapeDtypeStruct((1, LANES), F32)],
        scratch_shapes=[pltpu.VMEM((8, LANES), F32)], compiler_params=_params("arbitrary"))(dc, uf, bpad)


FOX_CHUNK = 32
FOX_CHUNK_BWD = 64
HEAD_PAIRS = FOX_HEADS // 2
PAIR = 2


def _masked(s, row0, col0, diagonal):
    if diagonal:
        row = row0 + lax.broadcasted_iota(jnp.int32, s.shape, 0)
        col = col0 + lax.broadcasted_iota(jnp.int32, s.shape, 1)
        s = jnp.where(col <= row, s, -jnp.inf)
    return s


def _causal_pairs(n, query_major):
    if query_major:
        pairs = [(q, k) for q in range(n) for k in range(q + 1)]
    else:
        pairs = [(q, k) for k in range(n) for q in range(k, n)]
    return (jnp.asarray([p[0] for p in pairs], jnp.int32), jnp.asarray([p[1] for p in pairs], jnp.int32))


def _lane_block(b):
    return slice(b * LANES, (b + 1) * LANES)


def _fold(op, xs):
    acc = xs[0]
    for x in xs[1:]:
        acc = op(acc, x)
    return acc


def _head_lanes(hh):
    lane = lax.broadcasted_iota(jnp.int32, (1, LANES), 1)
    return (lane < FOX_HEAD_DIM) if hh == 0 else (lane >= FOX_HEAD_DIM)


def _pick(first_head, a, b):
    return jnp.where(first_head, a, b)


def _fox_fwd(qkv, cT, comm):
    S = qkv.shape[0]
    t = _row_tile(S)
    n = S // t
    nc = len(comm)
    scale = 1.0 / math.sqrt(FOX_HEAD_DIM)
    chunk = min(FOX_CHUNK, t)
    per_head = 4
    q_tab, k_tab = _causal_pairs(n, True)
    steps = q_tab.shape[0]

    def body(qt_ref, kt_ref, q_ref, k_ref, v_ref, c_ref, *rest):
        comm_in = rest[:nc]
        o_ref, ob_ref, lse_ref = rest[nc:nc + 3]
        comm_out = rest[nc + 3:2 * nc + 3]
        scr = rest[2 * nc + 3:2 * nc + 3 + PAIR * per_head]
        sems = rest[2 * nc + 3 + PAIR * per_head:]
        hp = pl.program_id(0)
        step_id = pl.program_id(1)
        qi = qt_ref[step_id]
        ki = kt_ref[step_id]

        if nc:
            @pl.when((hp == 0) & (step_id == 0))
            def _():
                _Gather(comm_in, comm_out, *sems).start()

            @pl.when((hp == HEAD_PAIRS - 1) & (step_id == 0))
            def _():
                _Gather(comm_in, comm_out, *sems).pass_on()

        @pl.when(ki == 0)
        def _():
            for hh in range(PAIR):
                m_s, l_s, a_s, acc_s = scr[hh * per_head:hh * per_head + 4]
                m_s[...] = jnp.full_like(m_s, -jnp.inf)
                l_s[...] = jnp.zeros_like(l_s)
                acc_s[...] = jnp.zeros_like(acc_s)

        def step(diagonal):
            q2 = q_ref[...] * scale
            k2 = k_ref[...]
            v2 = v_ref[...]
            scores = []
            for hh in range(PAIR):
                qm = jnp.where(_head_lanes(hh), q2, jnp.zeros_like(q2))
                scores.append(lax.dot_general(qm, k2, NT, preferred_element_type=F32))
            for hh in range(PAIR):
                m_s, l_s, a_s, acc_s = scr[hh * per_head:(hh + 1) * per_head]
                s_s = scores[hh]
                hi_rows, lo_rows = [], []
                for r in range(t // chunk):
                    rows = slice(r * chunk, (r + 1) * chunk)
                    blocks = [_masked(s_s[rows, _lane_block(b)] - c_ref[hh, :, _lane_block(b)], r * chunk,
                                      b * LANES, diagonal) for b in range(t // LANES)]
                    m_prev = m_s[rows, :]
                    m_new = jnp.maximum(m_prev, jnp.max(_fold(jnp.maximum, blocks), axis=1, keepdims=True))
                    alpha = jnp.exp(m_prev - m_new)
                    ps = [jnp.exp(blk - m_new) for blk in blocks]
                    l_s[rows, :] = alpha * l_s[rows, :] + jnp.sum(_fold(jnp.add, ps), axis=1, keepdims=True)
                    m_s[rows, :] = m_new
                    a_s[rows, :] = alpha
                    his = [p.astype(BF16) for p in ps]
                    hi_rows.append(jnp.concatenate(his, axis=1))
                    lo_rows.append(jnp.concatenate([(p - h.astype(F32)).astype(BF16) for p, h in zip(ps, his)],
                                                   axis=1))
                pv = (lax.dot_general(jnp.concatenate(hi_rows, axis=0), v2, NN, preferred_element_type=F32)
                      + lax.dot_general(jnp.concatenate(lo_rows, axis=0), v2, NN, preferred_element_type=F32))
                acc_s[...] = a_s[...] * acc_s[...] + pv

        @pl.when(ki < qi)
        def _():
            step(False)

        @pl.when(ki == qi)
        def _():
            step(True)
            heads = []
            for hh in range(PAIR):
                m_s, l_s, a_s, acc_s = scr[hh * per_head:hh * per_head + 4]
                heads.append(acc_s[...] / l_s[...])
                lse_ref[hh] = m_s[...] + jnp.log(l_s[...])
            o2 = _pick(_head_lanes(0), heads[0], heads[1])
            o_ref[...] = o2
            ob_ref[...] = o2.astype(BF16)

        if nc:
            @pl.when((hp == HEAD_PAIRS - 1) & (step_id == steps - 1))
            def _():
                _Gather(comm_in, comm_out, *sems).finish()

    def q_cols(first_block):
        return pl.BlockSpec((t, LANES), lambda h, s, qt, kt: (qt[s], first_block + h))

    def k_cols(first_block):
        return pl.BlockSpec((t, LANES), lambda h, s, qt, kt: (kt[s], first_block + h))

    any_spec = pl.BlockSpec(memory_space=pl.ANY)
    head_scratch = [pltpu.VMEM((t, LANES), F32)] * per_head
    grid_spec = pltpu.PrefetchScalarGridSpec(
        num_scalar_prefetch=2, grid=(HEAD_PAIRS, steps),
        in_specs=[q_cols(0), k_cols(HEAD_PAIRS), k_cols(2 * HEAD_PAIRS),
                  pl.BlockSpec((PAIR, 1, t), lambda h, s, qt, kt: (h, 0, kt[s]))] + [any_spec] * nc,
        out_specs=[q_cols(0), q_cols(0),
                   pl.BlockSpec((PAIR, t, LANES), lambda h, s, qt, kt: (h, qt[s], 0))] + [any_spec] * nc,
        scratch_shapes=head_scratch * PAIR + _comm_scratch(nc))
    return pl.pallas_call(
        body, name="fox_fwd", grid_spec=grid_spec,
        out_shape=[jax.ShapeDtypeStruct((S, FOX_WIDTH), F32), jax.ShapeDtypeStruct((S, FOX_WIDTH), BF16),
                   jax.ShapeDtypeStruct((FOX_HEADS, S, LANES), F32)] + _comm_shapes(comm),
        compiler_params=_params("arbitrary", "arbitrary"))(q_tab, k_tab, qkv, qkv, qkv, cT, *comm)


def _fox_bwd(qkv, cT, o, lse, do, comm):
    S = qkv.shape[0]
    t = _row_tile(S)
    n = S // t
    nc = len(comm)
    scale = 1.0 / math.sqrt(FOX_HEAD_DIM)
    chunk = min(FOX_CHUNK_BWD, t)
    per_head = 6
    q_tab, k_tab = _causal_pairs(n, False)
    steps = q_tab.shape[0]

    def body(qt_ref, kt_ref, q_ref, k_ref, v_ref, c_ref, o_ref, do_ref, lse_ref, *rest):
        comm_in = rest[:nc]
        dq_ref, dk_ref, dv_ref, dc_ref = rest[nc:nc + 4]
        comm_out = rest[nc + 4:2 * nc + 4]
        dq_s, dk_s, dv_s = rest[2 * nc + 4:2 * nc + 7]
        scr = rest[2 * nc + 7:2 * nc + 7 + PAIR * per_head]
        sems = rest[2 * nc + 7 + PAIR * per_head:]
        hp = pl.program_id(0)
        step_id = pl.program_id(1)
        qi = qt_ref[step_id]
        ki = kt_ref[step_id]

        if nc:
            @pl.when((hp == 0) & (step_id == 0))
            def _():
                for cp in _comm_copies(comm_in, comm_out, *sems):
                    cp.start()

        @pl.when(step_id == 0)
        def _():
            dq_s[...] = jnp.zeros_like(dq_s)

        @pl.when(qi == ki)
        def _():
            dk_s[...] = jnp.zeros_like(dk_s)
            dv_s[...] = jnp.zeros_like(dv_s)
            for hh in range(PAIR):
                dc_s = scr[hh * per_head]
                dc_s[...] = jnp.zeros_like(dc_s)

        def step(diagonal):
            q2 = q_ref[...]
            k2 = k_ref[...]
            v2 = v_ref[...]
            do2 = do_ref[...]
            prod = do2.astype(F32) * o_ref[...]
            grads = []
            for hh in range(PAIR):
                dc_s, delta_s, s_s, dp_s, p_s, ds_s = scr[hh * per_head:(hh + 1) * per_head]
                mine = _head_lanes(hh)
                s_s[...] = lax.dot_general(jnp.where(mine, q2 * scale, jnp.zeros_like(q2)), k2, NT,
                                           preferred_element_type=F32)
                dp_s[...] = lax.dot_general(jnp.where(mine, do2, jnp.zeros_like(do2)), v2, NT,
                                            preferred_element_type=F32)
                delta_s[...] = jnp.broadcast_to(jnp.sum(jnp.where(mine, prod, 0.0), axis=1, keepdims=True),
                                                (t, LANES))
                dc8 = [jnp.zeros((8, LANES), F32) for _ in range(t // LANES)]
                for r in range(t // chunk):
                    rows = slice(r * chunk, (r + 1) * chunk)
                    lse = lse_ref[hh, rows, :]
                    delta = delta_s[rows, :]
                    for b in range(t // LANES):
                        s = _masked(s_s[rows, _lane_block(b)] - c_ref[hh, :, _lane_block(b)], r * chunk, b * LANES,
                                    diagonal)
                        p = jnp.exp(s - lse)
                        ds = p * (dp_s[rows, _lane_block(b)] - delta)
                        p_s[rows, _lane_block(b)] = p.astype(BF16)
                        ds_s[rows, _lane_block(b)] = ds.astype(BF16)
                        dc8[b] = dc8[b] + jnp.sum(ds.reshape(chunk // 8, 8, LANES), axis=0)
                for b in range(t // LANES):
                    dc_s[:, _lane_block(b)] += jnp.sum(dc8[b], axis=0, keepdims=True)
                dsb = ds_s[...]
                grads.append((lax.dot_general(p_s[...], do2, TN, preferred_element_type=F32),
                              lax.dot_general(dsb, k2, NN, preferred_element_type=F32),
                              lax.dot_general(dsb, q2, TN, preferred_element_type=F32)))
            first = _head_lanes(0)
            dv_s[...] += _pick(first, grads[0][0], grads[1][0])
            q_rows = pl.ds(pl.multiple_of(qi * t, t), t)
            dq_s[q_rows, :] += _pick(first, grads[0][1], grads[1][1]) * scale
            dk_s[...] += _pick(first, grads[0][2], grads[1][2]) * scale

        @pl.when(qi > ki)
        def _():
            step(False)

        @pl.when(qi == ki)
        def _():
            step(True)

        @pl.when(qi == n - 1)
        def _():
            dk_ref[...] = dk_s[...].astype(BF16)
            dv_ref[...] = dv_s[...].astype(BF16)
            for hh in range(PAIR):
                dc_ref[hh] = -scr[hh * per_head][...]

        @pl.when(step_id == steps - 1)
        def _():
            dq_ref[...] = dq_s[...].astype(BF16)

        if nc:
            @pl.when((hp == HEAD_PAIRS - 1) & (step_id == steps - 1))
            def _():
                for cp in _comm_copies(comm_in, comm_out, *sems):
                    cp.wait()

    def q_side(first_block):
        return pl.BlockSpec((t, LANES), lambda h, s, qt, kt: (qt[s], first_block + h))

    def k_side(first_block):
        return pl.BlockSpec((t, LANES), lambda h, s, qt, kt: (kt[s], first_block + h))

    any_spec = pl.BlockSpec(memory_space=pl.ANY)
    head_scratch = [pltpu.VMEM((1, t), F32), pltpu.VMEM((t, LANES), F32),
                    pltpu.VMEM((t, t), F32), pltpu.VMEM((t, t), F32), pltpu.VMEM((t, t), BF16),
                    pltpu.VMEM((t, t), BF16)]
    grad_shape = jax.ShapeDtypeStruct((S, FOX_WIDTH), BF16)
    grid_spec = pltpu.PrefetchScalarGridSpec(
        num_scalar_prefetch=2, grid=(HEAD_PAIRS, steps),
        in_specs=[q_side(0), k_side(HEAD_PAIRS), k_side(2 * HEAD_PAIRS),
                  pl.BlockSpec((PAIR, 1, t), lambda h, s, qt, kt: (h, 0, kt[s])), q_side(0), q_side(0),
                  pl.BlockSpec((PAIR, t, LANES), lambda h, s, qt, kt: (h, qt[s], 0))] + [any_spec] * nc,
        out_specs=[pl.BlockSpec((S, LANES), lambda h, s, qt, kt: (0, h)), k_side(0), k_side(0),
                   pl.BlockSpec((PAIR, 1, t), lambda h, s, qt, kt: (h, 0, kt[s]))] + [any_spec] * nc,
        scratch_shapes=[pltpu.VMEM((S, LANES), F32), pltpu.VMEM((t, LANES), F32), pltpu.VMEM((t, LANES), F32)]
        + head_scratch * PAIR + _comm_scratch(nc))
    return pl.pallas_call(
        body, name="fox_bwd", grid_spec=grid_spec,
        out_shape=[grad_shape, grad_shape, grad_shape, jax.ShapeDtypeStruct((FOX_HEADS, 1, S), F32)]
        + _comm_shapes(comm),
        compiler_params=_params("arbitrary", "arbitrary"))(q_tab, k_tab, qkv, qkv, qkv, cT, o, do, lse, *comm)


def _lanes(g):
    return slice(g * POOL_GROUP_DIM, (g + 1) * POOL_GROUP_DIM)


def _window_sum(e, win, back):
    rows = e.shape[0]
    s = e
    sh = 1
    while sh < win:
        s = s + pltpu.roll(s, sh if back else rows - sh, 0)
        sh *= 2
    return s


def _pooled(u_ref, up_ref, i, g, win, T):
    cur = u_ref[:, _lanes(g)]
    tail = jnp.where(i > 0, up_ref[T - POOL_HALO:T, _lanes(g)], 0.0)
    e = jnp.concatenate([tail, cur], axis=0)
    s = _window_sum(e, win, True)
    t_idx = i * T - POOL_HALO + lax.broadcasted_iota(jnp.int32, (T + POOL_HALO, POOL_GROUP_DIM), 0)
    cnt = jnp.clip(t_idx + 1, 1, win).astype(F32)
    return (s / cnt - e)[POOL_HALO:, :]


def _pool_fwd(uf, pw, ps):
    S = uf.shape[0]
    T = _row_tile(S)

    def body(u_ref, up_ref, w_ref, sc_ref, o_ref):
        i = pl.program_id(0)
        for g, win in enumerate(POOL_WINDOWS):
            pb = _pooled(u_ref, up_ref, i, g, win, T).astype(BF16)
            yv = lax.dot_general(pb, w_ref[g], NN, preferred_element_type=F32)
            o_ref[:, _lanes(g)] = (yv * sc_ref[:, _lanes(g)]).astype(BF16)

    return pl.pallas_call(
        body, name="pool_fwd", grid=(S // T,),
        in_specs=[pl.BlockSpec((T, POOL_WIDTH), lambda i: (i, 0)),
                  pl.BlockSpec((T, POOL_WIDTH), lambda i: (jnp.maximum(i - 1, 0), 0)),
                  pl.BlockSpec((4, POOL_GROUP_DIM, POOL_GROUP_DIM), lambda i: (0, 0, 0)),
                  pl.BlockSpec((1, POOL_WIDTH), lambda i: (0, 0))],
        out_specs=pl.BlockSpec((T, POOL_WIDTH), lambda i: (i, 0)),
        out_shape=jax.ShapeDtypeStruct((S, POOL_WIDTH), BF16), compiler_params=_params("parallel"))(uf, uf, pw, ps)


def _pool_bwd(uf, dpool, pw, ps):
    S = uf.shape[0]
    T = _row_tile(S)
    nb = S // T

    def body(u_ref, up_ref, d_ref, dn_ref, w_ref, sc_ref, du_ref, dw_ref, dsc_ref):
        i = pl.program_id(0)

        @pl.when(i == 0)
        def _():
            dw_ref[...] = jnp.zeros_like(dw_ref)
            dsc_ref[...] = jnp.zeros_like(dsc_ref)

        t_idx = i * T + lax.broadcasted_iota(jnp.int32, (T + POOL_HALO, POOL_GROUP_DIM), 0)
        for g, win in enumerate(POOL_WINDOWS):
            pb = _pooled(u_ref, up_ref, i, g, win, T).astype(BF16)
            w = w_ref[g]
            sc = sc_ref[:, _lanes(g)]
            yv = lax.dot_general(pb, w, NN, preferred_element_type=F32)
            dov = d_ref[:, _lanes(g)]
            dsc_ref[:, _lanes(g)] += jnp.sum(dov * yv, axis=0, keepdims=True)
            head = jnp.where(i < nb - 1, dn_ref[0:POOL_HALO, _lanes(g)], 0.0)
            dyb = (jnp.concatenate([dov, head], axis=0) * sc).astype(BF16)
            dw_ref[g] += lax.dot_general(pb, dyb[:T], TN, preferred_element_type=F32)
            dpl = lax.dot_general(dyb, w, NT, preferred_element_type=F32)
            cnt = jnp.minimum(t_idx + 1, win).astype(F32)
            a = _window_sum(dpl / cnt, win, False)
            du_ref[:, _lanes(g)] = (a - dpl)[:T].astype(BF16)

    return pl.pallas_call(
        body, name="pool_bwd", grid=(nb,),
        in_specs=[pl.BlockSpec((T, POOL_WIDTH), lambda i: (i, 0)),
                  pl.BlockSpec((T, POOL_WIDTH), lambda i: (jnp.maximum(i - 1, 0), 0)),
                  pl.BlockSpec((T, POOL_WIDTH), lambda i: (i, 0)),
                  pl.BlockSpec((T, POOL_WIDTH), lambda i: (jnp.minimum(i + 1, nb - 1), 0)),
                  pl.BlockSpec((4, POOL_GROUP_DIM, POOL_GROUP_DIM), lambda i: (0, 0, 0)),
                  pl.BlockSpec((1, POOL_WIDTH), lambda i: (0, 0))],
        out_specs=[pl.BlockSpec((T, POOL_WIDTH), lambda i: (i, 0)),
                   pl.BlockSpec((4, POOL_GROUP_DIM, POOL_GROUP_DIM), lambda i: (0, 0, 0)),
                   pl.BlockSpec((1, POOL_WIDTH), lambda i: (0, 0))],
        out_shape=[jax.ShapeDtypeStruct((S, POOL_WIDTH), BF16),
                   jax.ShapeDtypeStruct((4, POOL_GROUP_DIM, POOL_GROUP_DIM), F32),
                   jax.ShapeDtypeStruct((1, POOL_WIDTH), F32)],
        compiler_params=_params("arbitrary"))(uf, uf, dpool, dpool, pw, ps)


def _xhead(h):
    return slice(h * X_HEAD_DIM, (h + 1) * X_HEAD_DIM)


def _xvhead(h):
    return slice(D_MODEL + h * X_HEAD_DIM, D_MODEL + (h + 1) * X_HEAD_DIM)


X_CHUNK = 32


def _x_probs(s_ref, rows):
    blocks = [s_ref[rows, _lane_block(b)] * (1.0 / math.sqrt(X_HEAD_DIM)) for b in range(MEM_LEN // LANES)]
    m = jnp.max(_fold(jnp.maximum, blocks), axis=1, keepdims=True)
    es = [jnp.exp(blk - m) for blk in blocks]
    den = jnp.sum(_fold(jnp.add, es), axis=1, keepdims=True)
    return [e / den for e in es]


def _xattn_fwd(q, kv):
    S = q.shape[0]
    t = _row_tile(S)
    chunk = min(X_CHUNK, t)

    def body(q_ref, kv_ref, o_ref, s_s, p_s):
        for h in range(X_HEADS):
            s_s[...] = lax.dot_general(q_ref[:, _xhead(h)], kv_ref[:, _xhead(h)], NT, preferred_element_type=F32)
            for r in range(t // chunk):
                rows = slice(r * chunk, (r + 1) * chunk)
                for b, p in enumerate(_x_probs(s_s, rows)):
                    p_s[rows, _lane_block(b)] = p.astype(BF16)
            o_ref[:, _xhead(h)] = lax.dot_general(p_s[...], kv_ref[:, _xvhead(h)], NN,
                                                  preferred_element_type=F32).astype(BF16)

    return pl.pallas_call(
        body, name="xattn_fwd", grid=(S // t,),
        in_specs=[pl.BlockSpec((t, D_MODEL), lambda i: (i, 0)), pl.BlockSpec((MEM_LEN, 2 * D_MODEL), lambda i: (0, 0))],
        out_specs=pl.BlockSpec((t, D_MODEL), lambda i: (i, 0)),
        out_shape=jax.ShapeDtypeStruct((S, D_MODEL), BF16),
        scratch_shapes=[pltpu.VMEM((t, MEM_LEN), F32), pltpu.VMEM((t, MEM_LEN), BF16)],
        compiler_params=_params("parallel"))(q, kv)


def _xattn_bwd(q, kv, do):
    S = q.shape[0]
    t = _row_tile(S)
    nb = S // t
    scale = 1.0 / math.sqrt(X_HEAD_DIM)
    chunk = min(X_CHUNK, t)

    def body(q_ref, kv_ref, do_ref, dq_ref, dkv_ref, acc, s_s, dp_s, p_s, ds_s):
        i = pl.program_id(0)

        @pl.when(i == 0)
        def _():
            acc[...] = jnp.zeros_like(acc)

        for h in range(X_HEADS):
            qh = q_ref[:, _xhead(h)]
            kh = kv_ref[:, _xhead(h)]
            doh = do_ref[:, _xhead(h)]
            s_s[...] = lax.dot_general(qh, kh, NT, preferred_element_type=F32)
            dp_s[...] = lax.dot_general(doh, kv_ref[:, _xvhead(h)], NT, preferred_element_type=F32)
            for r in range(t // chunk):
                rows = slice(r * chunk, (r + 1) * chunk)
                ps = _x_probs(s_s, rows)
                dps = [dp_s[rows, _lane_block(b)] for b in range(len(ps))]
                inner = jnp.sum(_fold(jnp.add, [dp * p for dp, p in zip(dps, ps)]), axis=1, keepdims=True)
                for b, (dp, p) in enumerate(zip(dps, ps)):
                    p_s[rows, _lane_block(b)] = p.astype(BF16)
                    ds_s[rows, _lane_block(b)] = (p * (dp - inner)).astype(BF16)
            dsb = ds_s[...]
            acc[:, _xvhead(h)] += lax.dot_general(p_s[...], doh, TN, preferred_element_type=F32)
            dq_ref[:, _xhead(h)] = (lax.dot_general(dsb, kh, NN, preferred_element_type=F32) * scale).astype(BF16)
            acc[:, _xhead(h)] += lax.dot_general(dsb, qh, TN, preferred_element_type=F32) * scale

        @pl.when(i == nb - 1)
        def _():
            dkv_ref[...] = acc[...].astype(BF16)

    row = pl.BlockSpec((t, D_MODEL), lambda i: (i, 0))
    full = pl.BlockSpec((MEM_LEN, 2 * D_MODEL), lambda i: (0, 0))
    return pl.pallas_call(
        body, name="xattn_bwd", grid=(nb,), in_specs=[row, full, row], out_specs=[row, full],
        out_shape=[jax.ShapeDtypeStruct((S, D_MODEL), BF16), jax.ShapeDtypeStruct((MEM_LEN, 2 * D_MODEL), BF16)],
        scratch_shapes=[pltpu.VMEM((MEM_LEN, 2 * D_MODEL), F32), pltpu.VMEM((t, MEM_LEN), F32),
                        pltpu.VMEM((t, MEM_LEN), F32), pltpu.VMEM((t, MEM_LEN), BF16),
                        pltpu.VMEM((t, MEM_LEN), BF16)],
        compiler_params=_params("arbitrary"))(q, kv, do)


def _comm_shapes(arrs):
    return [jax.ShapeDtypeStruct((N_DEV,) + tuple(a.shape[-2:]), a.dtype) for a in arrs]


def _comm_scratch(n):
    if n == 0:
        return []
    return [pltpu.SemaphoreType.DMA((n, N_DEV - 1)), pltpu.SemaphoreType.DMA((n, N_DEV - 1)),
            pltpu.SemaphoreType.DMA((n,))]


def _comm_copies(ins, outs, send_sems, recv_sems, local_sems):
    x, y, c = lax.axis_index("x"), lax.axis_index("y"), lax.axis_index("c")
    me = 4 * x + 2 * y + c
    copies = []
    for w in range(len(ins)):
        src = ins[w] if len(ins[w].shape) == 2 else ins[w].at[me]
        copies.append(pltpu.make_async_copy(src, outs[w].at[me], local_sems.at[w]))
    for k in range(1, N_DEV):
        px = 1 - x if k & 4 else x
        py = 1 - y if k & 2 else y
        pc = 1 - c if k & 1 else c
        peer = 4 * px + 2 * py + pc
        for w in range(len(ins)):
            src = ins[w] if len(ins[w].shape) == 2 else ins[w].at[peer]
            copies.append(pltpu.make_async_remote_copy(
                src_ref=src, dst_ref=outs[w].at[me], send_sem=send_sems.at[w, k - 1],
                recv_sem=recv_sems.at[w, k - 1], device_id=(px, py, pc), device_id_type=pl.DeviceIdType.MESH))
    return copies


class _Gather:
    def __init__(self, ins, outs, send_sems, recv_sems, local_sems):
        x, y, c = lax.axis_index("x"), lax.axis_index("y"), lax.axis_index("c")
        me = 4 * x + 2 * y + c
        sibling = (x, y, 1 - c)
        self.local, self.mine, self.passed = [], [], []
        for w in range(len(ins)):
            def remote(idx, src, slot, dev, w=w):
                return pltpu.make_async_remote_copy(
                    src_ref=src, dst_ref=outs[w].at[slot], send_sem=send_sems.at[w, idx],
                    recv_sem=recv_sems.at[w, idx], device_id=dev, device_id_type=pl.DeviceIdType.MESH)

            self.local.append(pltpu.make_async_copy(ins[w], outs[w].at[me], local_sems.at[w]))
            mine, passed = [remote(0, ins[w], me, sibling)], []
            for j, (fx, fy) in enumerate(((0, 1), (1, 0), (1, 1))):
                px = 1 - x if fx else x
                py = 1 - y if fy else y
                slot = 4 * px + 2 * py + c
                mine.append(remote(1 + j, ins[w], me, (px, py, c)))
                passed.append(remote(4 + j, outs[w].at[slot], slot, sibling))
            self.mine.append(mine)
            self.passed.append(passed)

    def start(self):
        for cp in self.local:
            cp.start()
        for mine in self.mine:
            for cp in mine:
                cp.start()

    def pass_on(self):
        for mine, passed in zip(self.mine, self.passed):
            for j, cp in enumerate(passed):
                mine[1 + j].wait_recv()
                cp.start()

    def finish(self):
        for mine, passed in zip(self.mine, self.passed):
            mine[0].wait_recv()
            for cp in passed:
                cp.wait_recv()
            for cp in mine + passed:
                cp.wait_send()
        for cp in self.local:
            cp.wait()


def _exchange(name, arrs):
    n = len(arrs)
    gather = all(a.ndim == 2 for a in arrs)

    def body(*refs):
        if gather:
            g = _Gather(refs[:n], refs[n:2 * n], *refs[2 * n:])
            g.start()
            g.pass_on()
            g.finish()
            return
        copies = _comm_copies(refs[:n], refs[n:2 * n], *refs[2 * n:])
        for cp in copies:
            cp.start()
        for cp in copies:
            cp.wait()

    any_spec = pl.BlockSpec(memory_space=pl.ANY)
    return pl.pallas_call(
        body, name=name, in_specs=[any_spec] * n, out_specs=[any_spec] * n, out_shape=_comm_shapes(arrs),
        scratch_shapes=_comm_scratch(n))(*arrs)


def _adamw_math(w, g, m, v):
    m = ADAM_B1 * m + (1.0 - ADAM_B1) * g
    v = ADAM_B2 * v + (1.0 - ADAM_B2) * (g * g)
    m_hat = m / (1.0 - ADAM_B1 ** ADAM_STEP)
    v_hat = v / (1.0 - ADAM_B2 ** ADAM_STEP)
    delta = -ADAM_LR * (m_hat / (jnp.sqrt(v_hat) + ADAM_EPS) + ADAM_WD * w)
    return delta, m, v


def _sum_parts(p_ref):
    g = p_ref[0].astype(F32)
    for s in range(1, N_DEV):
        g = g + p_ref[s].astype(F32)
    return g


def _adamw_big(name, w, m, v, parts, tr):
    L, R, C = w.shape

    def body(w_ref, m_ref, v_ref, *rest):
        p_refs = rest[:L]
        g_ref, d_ref, nm_ref, nv_ref = rest[L:]
        layer = pl.program_id(0)
        for j in range(L):
            @pl.when(layer == j)
            def _(j=j):
                g = _sum_parts(p_refs[j])
                delta, nm, nv = _adamw_math(w_ref[...], g, m_ref[...], v_ref[...])
                g_ref[...] = g
                d_ref[...] = delta
                nm_ref[...] = nm
                nv_ref[...] = nv

    blk = pl.BlockSpec((None, tr, C), lambda l, i: (l, i, 0))

    def part_spec(j):
        return pl.BlockSpec((N_DEV, tr, C), lambda l, i: (0, jnp.where(l == j, i, 0), 0))

    shp = jax.ShapeDtypeStruct((L, R, C), F32)
    return pl.pallas_call(
        body, name=name, grid=(L, R // tr), in_specs=[blk, blk, blk] + [part_spec(j) for j in range(L)],
        out_specs=[blk] * 4, out_shape=[shp] * 4, compiler_params=_params("arbitrary", "arbitrary"))(w, m, v, *parts)


def _adamw_small(w, m, v, parts):
    R, C = w.shape

    def body(w_ref, m_ref, v_ref, p_ref, g_ref, d_ref, nm_ref, nv_ref):
        g = _sum_parts(p_ref)
        delta, nm, nv = _adamw_math(w_ref[...], g, m_ref[...], v_ref[...])
        g_ref[...] = g
        d_ref[...] = delta
        nm_ref[...] = nm
        nv_ref[...] = nv

    shp = jax.ShapeDtypeStruct((R, C), F32)
    return pl.pallas_call(body, name="adamw_small", out_shape=[shp] * 4,
                          compiler_params=pltpu.CompilerParams(vmem_limit_bytes=VMEM_LIMIT))(w, m, v, parts)


def _vec(a):
    return a.reshape(1, -1)


W_IN_SHARD = IN_COLS // N_DEV
W_IN_ROWS = 272


def _w_in_travel(a):
    pad = [(0, 0)] * (a.ndim - 2) + [(0, W_IN_ROWS - W_IN_SHARD), (0, 0)]
    return jnp.pad(jnp.swapaxes(a, -1, -2), pad)


def _unpack_w_in(g):
    full = jnp.transpose(g[:, :W_IN_SHARD, :], (2, 0, 1)).reshape(D_MODEL, IN_COLS)
    qkv = full[:, :QKV_COLS]
    f = full[:, QKV_COLS:QKV_COLS + FOX_HEADS]
    u = full[:, QKV_COLS + FOX_HEADS:]
    uf = jnp.concatenate([u, f, jnp.zeros((D_MODEL, UF_COLS - POOL_WIDTH - FOX_HEADS), g.dtype)], axis=1)
    return jnp.concatenate([qkv, uf], axis=1)


def _pack_dw_in(dwp):
    qkv = dwp[:, :QKV_COLS]
    u = dwp[:, QKV_COLS:QKV_COLS + POOL_WIDTH]
    f = dwp[:, QKV_COLS + POOL_WIDTH:QKV_COLS + POOL_WIDTH + FOX_HEADS]
    full = jnp.concatenate([qkv, f, u], axis=1)
    return _w_in_travel(jnp.transpose(full.reshape(D_MODEL, N_DEV, W_IN_SHARD), (1, 0, 2)))


REST = ['w_out', 'wq_x', 'wkv_x', 'wo_x', 'w_up', 'w_down']


def _layer_fwd(x0, h1, mem, sp, g_in, shards, g_next):
    S = x0.shape[0]
    sv = {"x0": x0}
    w_inp = _unpack_w_in(g_in)
    qkv, uf = _mm_rows("mm_in", [(h1, w_inp, "nn")],
                       [(BF16, 0, QKV_COLS, "id"), (F32, QKV_COLS, UF_COLS, "id")], piece=UF_COLS)
    c = _gate_fwd(uf, sp["b_forget"])
    cT = jnp.transpose(c[:, :FOX_HEADS]).reshape(FOX_HEADS, 1, S)
    o, ob, lse, *got = _fox_fwd(qkv, cT, shards)
    g_out, g_q, g_kv, g_o, g_up, g_down = got[:6]
    W = dict(inp=w_inp, out=g_out.reshape(D_MODEL, D_MODEL), q=g_q.reshape(D_MODEL, D_MODEL), kv=g_kv,
             o=g_o.reshape(D_MODEL, D_MODEL), up=g_up, down=g_down.reshape(D_FF, D_MODEL))
    pool = _pool_fwd(uf, sp["pool_w"], sp["pool_scale"])
    cat = jnp.concatenate([ob, pool], axis=1)
    mix, x1, h2 = _mm_resid_norm("mm_sq_norm", cat, W["out"], x0, sp["g_mix_post"], sp["g_x_pre"])
    mn = _norm_fwd("norm_mem", mem, sp["g_mem"])
    q2 = _mm1("mm_q", h2, W["q"], "nn", D_MODEL, BF16)
    kv = _mm1("mm_kv", mn, W["kv"], "nn3", 2 * D_MODEL, BF16, piece=2 * D_MODEL // N_DEV)
    o2 = _xattn_fwd(q2, kv)
    xo, x2, h3 = _mm_resid_norm("mm_sq_norm", o2, W["o"], x1, sp["g_x_post"], sp["g_ffn_pre"])
    up, act = _mm_rows("mm_up", [(h3, W["up"], "nn3")], [(BF16, 0, D_FF, "id"), (BF16, 0, D_FF, "relu2")],
                       piece=D_FF // N_DEV)
    y, x3, h_next = _mm_resid_norm("mm_down_norm" if g_next is not None else "mm_down_norm_last", act, W["down"], x2,
                                   sp["g_ffn_post"], g_next)
    sv.update(h1=h1, uf=uf, cT=cT, qkv=qkv, o=o, lse=lse, cat=cat, mix=mix, x1=x1, h2=h2, mn=mn, q2=q2, kv=kv,
              o2=o2, xo=xo, x2=x2, h3=h3, up=up, act=act, y=y)
    return x3, h_next, sv, W, (got[6] if len(got) > 6 else None)


def _layer_bwd(dx3, dy, mem, sv, sp, W, carried, below):
    S = dx3.shape[0]
    gs = {}
    gb = {}
    (dup,) = _mm_rows("mm_dup", [(dy, W["down"], "nt")], [(BF16, 0, D_FF, "drelu2")], extra=sv["up"])
    gb["w_down"] = _mm_tn_whole("mm_dw_down", dy, sv["act"], BF16, D_FF // N_DEV, transpose_out=True).reshape(
        N_DEV, D_FF // N_DEV, D_MODEL)
    gb["w_up"] = _mm_tn_whole("mm_dw_up", sv["h3"], dup, BF16, D_FF // N_DEV, shard_out=True)
    dx2, gs["g_ffn_pre"], dxo, gs["g_x_post"] = _norm_bwd(
        "mm_dh3_norm_bwd", (dup, W["up"], "nt3"), sv["x2"], sp["g_ffn_pre"], dx3, F32,
        below=(sv["xo"], sp["g_x_post"]))
    do2 = _mm1("mm_sq_t", dxo, W["o"], "nt", D_MODEL, BF16)
    gb["wo_x"] = _mm_tn_whole("mm_dw_sq", sv["o2"], dxo, BF16, 512).reshape(N_DEV, D_MODEL // N_DEV, D_MODEL)
    dq2, dkvb = _xattn_bwd(sv["q2"], sv["kv"], do2)
    gb["wq_x"] = _mm_tn_whole("mm_dw_sq", sv["h2"], dq2, BF16, 512).reshape(N_DEV, D_MODEL // N_DEV, D_MODEL)
    gb["wkv_x"] = _mm_tn("mm_dw_kv", sv["mn"], dkvb, BF16, shard_cols=2 * D_MODEL // N_DEV)
    dmn = _mm1("mm_dmn", dkvb, W["kv"], "nt3", D_MODEL, F32)
    _, gs["g_mem"] = _norm_bwd("norm_bwd_mem", dmn, mem, sp["g_mem"], None, BF16)
    dx1, gs["g_x_pre"], dmix, gs["g_mix_post"] = _norm_bwd(
        "mm_dh2_norm_bwd", (dq2, W["q"], "nt"), sv["x1"], sp["g_x_pre"], dx2, F32,
        below=(sv["mix"], sp["g_mix_post"]))
    doh, dpool = _mm_rows("mm_dcat", [(dmix, W["out"], "nt")],
                          [(BF16, 0, FOX_WIDTH, "id"), (F32, FOX_WIDTH, POOL_WIDTH, "id")])
    gb["w_out"] = _mm_tn_whole("mm_dw_sq", sv["cat"], dmix, BF16, 512).reshape(N_DEV, D_MODEL // N_DEV, D_MODEL)
    du, gs["pool_w"], gs["pool_scale"] = _pool_bwd(sv["uf"], dpool, sp["pool_w"], sp["pool_scale"])
    dq, dk, dv, dcT, *got = _fox_bwd(sv["qkv"], sv["cT"], sv["o"], sv["lse"], doh, [gb[n] for n in REST] + carried)
    dc = jnp.pad(jnp.transpose(dcT.reshape(FOX_HEADS, S)), ((0, 0), (0, LANES - FOX_HEADS)))
    dfg, db = _gate_bwd(dc, sv["uf"], sp["b_forget"])
    gs["b_forget"] = db[:, :FOX_HEADS]
    dproj = jnp.concatenate([dq, dk, dv, du, dfg], axis=1)
    dwp = _mm_tn("mm_dw_in", sv["h1"], dproj, BF16, piece=UF_COLS)
    dh1 = (dproj, W["inp"], "nt")
    if below is None:
        dx0, gs["g_mix_pre"] = _norm_bwd("mm_dh1_norm_bwd_first", dh1, sv["x0"], sp["g_mix_pre"], dx1, F32)
        lower = None
    else:
        dx0, gs["g_mix_pre"], *lower = _norm_bwd("mm_dh1_norm_bwd", dh1, sv["x0"], sp["g_mix_pre"], dx1, F32,
                                                 below=below)
    return dx0, lower, dict(zip(REST, got[:6])), got[6:], _pack_dw_in(dwp), gs


def _small_rows(shape):
    return -(-math.prod(shape) // (8 * LANES)) * 8


def _pack_small(d):
    blocks = []
    for n in SMALL:
        rows = _small_rows(d[n].shape)
        if d[n].shape[-1] == LANES:
            blocks.append(d[n].reshape(rows, LANES))
        else:
            flat = d[n].reshape(-1)
            blocks.append(jnp.pad(flat, (0, rows * LANES - flat.shape[0])).reshape(rows, LANES))
    return jnp.concatenate(blocks, axis=0)


def _unpack_small(packed, like):
    out = {}
    row = 0
    for n in SMALL:
        shape = like[n].shape
        rows = _small_rows(shape)
        block = packed[row:row + rows]
        out[n] = block.reshape(shape) if shape[-1] == LANES else block.reshape(-1)[:math.prod(shape)].reshape(shape)
        row += rows
    return out


def kernel(x, mem, g_mix_pre, w_in, b_forget, pool_w, pool_scale, w_out, g_mix_post, g_x_pre, g_mem, wq_x, wkv_x, wo_x, g_x_post, g_ffn_pre, w_up, w_down, g_ffn_post, loss_target, m_g_mix_pre, m_w_in, m_b_forget, m_pool_w, m_pool_scale, m_w_out, m_g_mix_post, m_g_x_pre, m_g_mem, m_wq_x, m_wkv_x, m_wo_x, m_g_x_post, m_g_ffn_pre, m_w_up, m_w_down, m_g_ffn_post, v_g_mix_pre, v_w_in, v_b_forget, v_pool_w, v_pool_scale, v_w_out, v_g_mix_post, v_g_x_pre, v_g_mem, v_wq_x, v_wkv_x, v_wo_x, v_g_x_post, v_g_ffn_pre, v_w_up, v_w_down, v_g_ffn_post):
    w = dict(g_mix_pre=g_mix_pre, w_in=w_in, b_forget=b_forget, pool_w=pool_w, pool_scale=pool_scale, w_out=w_out,
             g_mix_post=g_mix_post, g_x_pre=g_x_pre, g_mem=g_mem, wq_x=wq_x, wkv_x=wkv_x, wo_x=wo_x,
             g_x_post=g_x_post, g_ffn_pre=g_ffn_pre, w_up=w_up, w_down=w_down, g_ffn_post=g_ffn_post)
    mom = dict(g_mix_pre=m_g_mix_pre, w_in=m_w_in, b_forget=m_b_forget, pool_w=m_pool_w, pool_scale=m_pool_scale,
               w_out=m_w_out, g_mix_post=m_g_mix_post, g_x_pre=m_g_x_pre, g_mem=m_g_mem, wq_x=m_wq_x,
               wkv_x=m_wkv_x, wo_x=m_wo_x, g_x_post=m_g_x_post, g_ffn_pre=m_g_ffn_pre, w_up=m_w_up,
               w_down=m_w_down, g_ffn_post=m_g_ffn_post)
    var = dict(g_mix_pre=v_g_mix_pre, w_in=v_w_in, b_forget=v_b_forget, pool_w=v_pool_w, pool_scale=v_pool_scale,
               w_out=v_w_out, g_mix_post=v_g_mix_post, g_x_pre=v_g_x_pre, g_mem=v_g_mem, wq_x=v_wq_x,
               wkv_x=v_wkv_x, wo_x=v_wo_x, g_x_post=v_g_x_post, g_ffn_pre=v_g_ffn_pre, w_up=v_w_up,
               w_down=v_w_down, g_ffn_post=v_g_ffn_post)
    S = x.shape[1]
    xs = x.reshape(S, D_MODEL)
    mems = mem.reshape(MEM_LEN, D_MODEL)
    target = loss_target.reshape(S, D_MODEL)

    def small_params(l):
        return dict(
            g_mix_pre=_vec(g_mix_pre[l]), g_mix_post=_vec(g_mix_post[l]), g_x_pre=_vec(g_x_pre[l]),
            g_mem=_vec(g_mem[l]), g_x_post=_vec(g_x_post[l]), g_ffn_pre=_vec(g_ffn_pre[l]),
            g_ffn_post=_vec(g_ffn_post[l]), pool_scale=_vec(pool_scale[l]), pool_w=pool_w[l].astype(BF16),
            b_forget=jnp.pad(_vec(b_forget[l]), ((0, 0), (0, LANES - FOX_HEADS))))

    shard = {n: [w[n][l].astype(BF16) for l in range(DEPTH)] for n in REST}
    shard["w_in"] = [_w_in_travel(w_in[l].astype(BF16)) for l in range(DEPTH)]
    sps = [small_params(l) for l in range(DEPTH)]
    saved, weights = [], []
    h = xs
    (g_in,) = _exchange("gather_w_in", [shard["w_in"][0]])
    hn = _norm_fwd("norm_fwd", xs, sps[0]["g_mix_pre"])
    for l in range(DEPTH):
        travelling = [shard[n][l] for n in REST] + ([shard["w_in"][l + 1]] if l + 1 < DEPTH else [])
        g_next = sps[l + 1]["g_mix_pre"] if l + 1 < DEPTH else None
        h, hn, sv, W, g_in = _layer_fwd(h, hn, mems, sps[l], g_in, travelling, g_next)
        saved.append(sv)
        weights.append(W)
    dh, sq = _loss_fwd_bwd(h, target)
    loss = lax.psum(0.5 * sq[0, 0] / D_MODEL, ("x", "y", "c"))

    parts = [dict() for _ in range(DEPTH)]
    small_grads = [None] * DEPTH
    carried = []
    lower = _norm_bwd("norm_bwd_b", dh, saved[-1]["y"], sps[-1]["g_ffn_post"], None, BF16)
    for l in reversed(range(DEPTH)):
        dy, dg_ffn_post = lower
        below = (saved[l - 1]["y"], sps[l - 1]["g_ffn_post"]) if l > 0 else None
        dh, lower, got, got_carried, dw_in, gs = _layer_bwd(dh, dy, mems, saved[l], sps[l], weights[l], carried, below)
        gs["g_ffn_post"] = dg_ffn_post
        parts[l].update(got)
        if got_carried:
            parts[l + 1]["w_in"] = got_carried[0]
        carried = [dw_in]
        small_grads[l] = gs
    (parts[0]["w_in"],) = _exchange("scatter_dw_in", carried)
    grad_x = dh.reshape(1, S, D_MODEL)

    grads, deltas, new_m, new_v = {}, {}, {}, {}
    rows = dict(w_in=128, w_out=128, wq_x=128, wkv_x=256, wo_x=128, w_up=256, w_down=128)
    for l in range(DEPTH):
        parts[l]["w_in"] = jnp.swapaxes(parts[l]["w_in"][:, :W_IN_SHARD, :], 1, 2)
    for n in BIG:
        grads[n], deltas[n], new_m[n], new_v[n] = _adamw_big(
            "adamw_" + n, w[n], mom[n], var[n], [parts[l][n] for l in range(DEPTH)], rows[n])

    sg = {n: jnp.stack([small_grads[l][n].reshape(w[n].shape[1:]) for l in range(DEPTH)]) for n in SMALL}
    (sg_parts,) = _exchange("gather_small_grads", [_pack_small(sg)])
    outs = _adamw_small(_pack_small(w), _pack_small(mom), _pack_small(var), sg_parts)
    for d, packed in zip((grads, deltas, new_m, new_v), outs):
        d.update(_unpack_small(packed, w))

    return (loss, grad_x, *[grads[n] for n in W_NAMES], *[deltas[n] for n in W_NAMES],
            *[new_m[n] for n in W_NAMES], *[new_v[n] for n in W_NAMES])
```

```python
import math

import jax
import jax.numpy as jnp
from jax import lax
from jax.experimental import pallas as pl
from jax.experimental.pallas import tpu as pltpu

F32 = jnp.float32
BF16 = jnp.bfloat16

D_MODEL = 1024
DEPTH = 4
FOX_WIDTH = 512
FOX_HEADS = 8
FOX_HEAD_DIM = 64
POOL_WIDTH = 512
POOL_WINDOWS = (2, 4, 8, 16)
POOL_GROUP_DIM = 128
POOL_HALO = 16
MEM_LEN = 256
X_HEADS = 4
X_HEAD_DIM = 256
D_FF = 4096
EPS = 1e-6
IN_COLS = 2056
QKV_COLS = 3 * FOX_WIDTH
UF_COLS = 640
INP_COLS = QKV_COLS + UF_COLS
N_DEV = 8
LANES = 128

ADAM_LR = 0.001
ADAM_B1 = 0.9
ADAM_B2 = 0.999
ADAM_EPS = 1e-08
ADAM_WD = 0.01
ADAM_STEP = 10

VMEM_LIMIT = 56 * 1024 * 1024

W_NAMES = ['g_mix_pre', 'w_in', 'b_forget', 'pool_w', 'pool_scale', 'w_out', 'g_mix_post', 'g_x_pre', 'g_mem',
           'wq_x', 'wkv_x', 'wo_x', 'g_x_post', 'g_ffn_pre', 'w_up', 'w_down', 'g_ffn_post']
BIG = ['w_in', 'w_out', 'wq_x', 'wkv_x', 'wo_x', 'w_up', 'w_down']
SMALL = [n for n in W_NAMES if n not in BIG]

NN = (((1,), (0,)), ((), ()))
NT = (((1,), (1,)), ((), ()))
TN = (((0,), (0,)), ((), ()))


def _params(*sem):
    return pltpu.CompilerParams(dimension_semantics=sem, vmem_limit_bytes=VMEM_LIMIT)


def _row_tile(s):
    return min(s, 512)


def _product(a_ref, w_ref, kind, c0, pw):
    cols = slice(c0, c0 + pw)
    if kind == "nn":
        return lax.dot_general(a_ref[...], w_ref[:, cols], NN, preferred_element_type=F32)
    if kind == "nt":
        return lax.dot_general(a_ref[...], w_ref[cols, :], NT, preferred_element_type=F32)
    n = w_ref.shape[2]
    if kind == "nn3":
        assert pw == n and c0 % n == 0
        return lax.dot_general(a_ref[...], w_ref[c0 // n], NN, preferred_element_type=F32)
    r = None
    for j in range(w_ref.shape[0]):
        part = lax.dot_general(a_ref[:, j * n:(j + 1) * n], w_ref[j, cols, :], NT, preferred_element_type=F32)
        r = part if r is None else r + part
    return r


def _resident(w):
    return pl.BlockSpec(w.shape, lambda i, nd=w.ndim: (0,) * nd)


def _mm_rows(name, terms, outs, extra=None, piece=1024):
    M = terms[0][0].shape[0]
    tm = _row_tile(M)
    nterm = len(terms)
    n_extra = 0 if extra is None else 1
    groups = {}
    for idx, (_, c0, width, fn) in enumerate(outs):
        groups.setdefault((c0, width), []).append((idx, fn))

    def body(*refs):
        a_refs = refs[0:2 * nterm:2]
        w_refs = refs[1:2 * nterm:2]
        extra_refs = refs[2 * nterm:2 * nterm + n_extra]
        out_refs = refs[2 * nterm + n_extra:]
        for (g0, gw), members in groups.items():
            for c0 in range(g0, g0 + gw, piece):
                pw = min(piece, g0 + gw - c0)
                r = None
                for a_ref, w_ref, (_, w, kind) in zip(a_refs, w_refs, terms):
                    part = _product(a_ref, w_ref, kind, c0, pw)
                    r = part if r is None else r + part
                dst = slice(c0 - g0, c0 - g0 + pw)
                for idx, fn in members:
                    if fn == "relu2":
                        rp = jnp.maximum(r, 0.0)
                        val = rp * rp
                    elif fn == "drelu2":
                        val = r * (2.0 * jnp.maximum(extra_refs[0][:, dst].astype(F32), 0.0))
                    else:
                        val = r
                    out_refs[idx][:, dst] = val.astype(out_refs[idx].dtype)

    in_specs, ins = [], []
    for a, w, _ in terms:
        in_specs.append(pl.BlockSpec((tm, a.shape[1]), lambda i: (i, 0)))
        in_specs.append(pl.BlockSpec(w.shape, lambda i, nd=w.ndim: (0,) * nd))
        ins += [a, w]
    if extra is not None:
        in_specs.append(pl.BlockSpec((tm, extra.shape[1]), lambda i: (i, 0)))
        ins.append(extra)
    res = pl.pallas_call(
        body, name=name, grid=(M // tm,), in_specs=in_specs,
        out_specs=[pl.BlockSpec((tm, width), lambda i: (i, 0)) for _, _, width, _ in outs],
        out_shape=[jax.ShapeDtypeStruct((M, width), dt) for dt, _, width, _ in outs],
        compiler_params=_params("parallel"))(*ins)
    return res


def _mm1(name, a, w, kind, n_cols, dtype, piece=1024):
    return _mm_rows(name, [(a, w, kind)], [(dtype, 0, n_cols, "id")], piece=piece)[0]


def _mm_tn(name, a, b, out_dtype, shard_cols=None, piece=512):
    K, M = a.shape
    N = b.shape[1]
    tk = _row_tile(K)
    nk = K // tk
    piece = shard_cols or min(piece, N)

    def body(a_ref, b_ref, o_ref, acc):
        k = pl.program_id(0)

        @pl.when(k == 0)
        def _():
            acc[...] = jnp.zeros_like(acc)

        a_t = jnp.transpose(a_ref[...])
        for c0 in range(0, N, piece):
            cols = slice(c0, min(c0 + piece, N))
            acc[:, cols] += lax.dot_general(a_t, b_ref[:, cols], NN, preferred_element_type=F32)

        @pl.when(k == nk - 1)
        def _():
            for c0 in range(0, N, piece):
                cols = slice(c0, min(c0 + piece, N))
                if shard_cols:
                    o_ref[c0 // piece] = acc[:, cols].astype(o_ref.dtype)
                else:
                    o_ref[:, cols] = acc[:, cols].astype(o_ref.dtype)

    out_dims = (N // shard_cols, M, shard_cols) if shard_cols else (M, N)
    return pl.pallas_call(
        body, name=name, grid=(nk,),
        in_specs=[pl.BlockSpec((tk, M), lambda k: (k, 0)), pl.BlockSpec((tk, N), lambda k: (k, 0))],
        out_specs=pl.BlockSpec(out_dims, lambda k, nd=len(out_dims): (0,) * nd),
        out_shape=jax.ShapeDtypeStruct(out_dims, out_dtype),
        scratch_shapes=[pltpu.VMEM((M, N), F32)],
        compiler_params=_params("arbitrary"))(a, b)


def _norm_fwd(name, x, g):
    S, Dm = x.shape
    ts = _row_tile(S)

    def body(x_ref, g_ref, h_ref):
        xv = x_ref[...]
        r = lax.rsqrt(jnp.mean(xv * xv, axis=-1, keepdims=True) + EPS)
        h_ref[...] = ((xv * r) * g_ref[...]).astype(BF16)

    return pl.pallas_call(
        body, name=name, grid=(S // ts,),
        in_specs=[pl.BlockSpec((ts, Dm), lambda i: (i, 0)), pl.BlockSpec((1, Dm), lambda i: (0, 0))],
        out_specs=pl.BlockSpec((ts, Dm), lambda i: (i, 0)),
        out_shape=jax.ShapeDtypeStruct((S, Dm), BF16), compiler_params=_params("parallel"))(x, g)


def _mm_resid_norm(name, a, w, x, g, g_next):
    S, Dm = x.shape
    ts = _row_tile(S)
    has_next = g_next is not None

    def body(a_ref, w_ref, x_ref, g_ref, *rest):
        fv = _product(a_ref, w_ref, "nn", 0, Dm)
        r = lax.rsqrt(jnp.mean(fv * fv, axis=-1, keepdims=True) + EPS)
        xn = x_ref[...] + (fv * r) * g_ref[...]
        if has_next:
            gn_ref, f_ref, o_ref, h_ref = rest
            rn = lax.rsqrt(jnp.mean(xn * xn, axis=-1, keepdims=True) + EPS)
            h_ref[...] = ((xn * rn) * gn_ref[...]).astype(BF16)
        else:
            f_ref, o_ref = rest
        f_ref[...] = fv
        o_ref[...] = xn

    row = pl.BlockSpec((ts, Dm), lambda i: (i, 0))
    vec = pl.BlockSpec((1, Dm), lambda i: (0, 0))
    ins = [a, w, x, g] + ([g_next] if has_next else [])
    f32_rows = jax.ShapeDtypeStruct((S, Dm), F32)
    res = pl.pallas_call(
        body, name=name, grid=(S // ts,),
        in_specs=[pl.BlockSpec((ts, a.shape[1]), lambda i: (i, 0)), _resident(w), row, vec] + ([vec] if has_next else []),
        out_specs=[row, row] + ([row] if has_next else []),
        out_shape=[f32_rows, f32_rows] + ([jax.ShapeDtypeStruct((S, Dm), BF16)] if has_next else []),
        compiler_params=_params("parallel"))(*ins)
    return (res[0], res[1], res[2]) if has_next else (res[0], res[1], None)


def _rms_bwd(dov, yv, g):
    r = lax.rsqrt(jnp.mean(yv * yv, axis=-1, keepdims=True) + EPS)
    z = dov * g
    yr = yv * r
    return r * (z - yr * jnp.mean(yr * z, axis=-1, keepdims=True)), jnp.sum(dov * yr, axis=0, keepdims=True)


def _norm_bwd(name, dout, y, g, resid, out_dtype, below=None):
    S, Dm = y.shape
    ts = _row_tile(S)
    has_resid = resid is not None
    chained = below is not None
    produced = isinstance(dout, tuple)
    kind = dout[2] if produced else None

    def body(*refs):
        refs = list(refs)
        if produced:
            dov = _product(refs[0], refs[1], kind, 0, Dm)
            refs = refs[1:]
        else:
            dov = refs[0][...]
        y_ref, g_ref = refs[1:3]
        pos = 3
        r_ref = refs[pos] if has_resid else None
        pos += has_resid
        if chained:
            f_ref, gf_ref = refs[pos:pos + 2]
            pos += 2
        dy_ref, dg_ref = refs[pos:pos + 2]
        i = pl.program_id(0)
        dy, dg = _rms_bwd(dov, y_ref[...], g_ref[...])
        if has_resid:
            dy = dy + r_ref[...]
        dy_ref[...] = dy.astype(out_dtype)

        @pl.when(i == 0)
        def _():
            for ref in refs[pos + 1::2]:
                ref[...] = jnp.zeros_like(ref)

        dg_ref[...] += dg
        if chained:
            df_ref, dgf_ref = refs[pos + 2:pos + 4]
            df, dgf = _rms_bwd(dy, f_ref[...], gf_ref[...])
            df_ref[...] = df.astype(BF16)
            dgf_ref[...] += dgf

    row = pl.BlockSpec((ts, Dm), lambda i: (i, 0))
    vec = pl.BlockSpec((1, Dm), lambda i: (0, 0))
    if produced:
        ins = [dout[0], dout[1]]
        specs = [pl.BlockSpec((ts, dout[0].shape[1]), lambda i: (i, 0)), _resident(dout[1])]
    else:
        ins = [dout]
        specs = [row]
    ins += [y, g] + ([resid] if has_resid else []) + (list(below) if chained else [])
    specs += [row, vec] + ([row] if has_resid else []) + ([row, vec] if chained else [])
    vec_shape = jax.ShapeDtypeStruct((1, Dm), F32)
    return pl.pallas_call(
        body, name=name, grid=(S // ts,), in_specs=specs, out_specs=[row, vec] + ([row, vec] if chained else []),
        out_shape=[jax.ShapeDtypeStruct((S, Dm), out_dtype), vec_shape]
        + ([jax.ShapeDtypeStruct((S, Dm), BF16), vec_shape] if chained else []),
        compiler_params=_params("arbitrary"))(*ins)


def _loss_fwd_bwd(y, t):
    S, Dm = y.shape
    ts = _row_tile(S)

    def body(y_ref, t_ref, dy_ref, acc_ref):
        i = pl.program_id(0)
        e = y_ref[...] - t_ref[...]
        dy_ref[...] = e * (1.0 / Dm)

        @pl.when(i == 0)
        def _():
            acc_ref[...] = jnp.zeros_like(acc_ref)

        s = jnp.sum(jnp.sum(e * e, axis=1, keepdims=True), axis=0, keepdims=True)
        acc_ref[...] += s

    row = pl.BlockSpec((ts, Dm), lambda i: (i, 0))
    return pl.pallas_call(
        body, name="loss", grid=(S // ts,), in_specs=[row, row],
        out_specs=[row, pl.BlockSpec((8, LANES), lambda i: (0, 0))],
        out_shape=[jax.ShapeDtypeStruct((S, Dm), F32), jax.ShapeDtypeStruct((8, LANES), F32)],
        compiler_params=_params("arbitrary"))(y, t)


def _log_sigmoid(x):
    return jnp.minimum(x, 0.0) - jnp.log(1.0 + jnp.exp(-jnp.abs(x)))


def _gate_fwd(uf, bpad):
    S = uf.shape[0]
    T = _row_tile(S)

    def body(f_ref, b_ref, c_ref, carry):
        i = pl.program_id(0)

        @pl.when(i == 0)
        def _():
            carry[...] = jnp.zeros_like(carry)

        lf = _log_sigmoid(f_ref[...] + b_ref[...])
        r = lax.broadcasted_iota(jnp.int32, (T, T), 0)
        cidx = lax.broadcasted_iota(jnp.int32, (T, T), 1)
        tri = (cidx <= r).astype(F32)
        c = lax.dot_general(tri, lf, NN, precision=lax.Precision.HIGHEST, preferred_element_type=F32)
        c_ref[...] = c + carry[0:1, :]
        carry[...] = carry[...] + jnp.sum(lf, axis=0, keepdims=True)

    return pl.pallas_call(
        body, name="gate_fwd", grid=(S // T,),
        in_specs=[pl.BlockSpec((T, LANES), lambda i: (i, 4)), pl.BlockSpec((1, LANES), lambda i: (0, 0))],
        out_specs=pl.BlockSpec((T, LANES), lambda i: (i, 0)),
        out_shape=jax.ShapeDtypeStruct((S, LANES), F32),
        scratch_shapes=[pltpu.VMEM((8, LANES), F32)], compiler_params=_params("arbitrary"))(uf, bpad)


def _gate_bwd(dc, uf, bpad):
    S = uf.shape[0]
    T = _row_tile(S)
    nb = S // T

    def body(dc_ref, f_ref, b_ref, df_ref, db_ref, carry):
        i = pl.program_id(0)

        @pl.when(i == 0)
        def _():
            carry[...] = jnp.zeros_like(carry)
            db_ref[...] = jnp.zeros_like(db_ref)

        dcv = dc_ref[...]
        r = lax.broadcasted_iota(jnp.int32, (T, T), 0)
        cidx = lax.broadcasted_iota(jnp.int32, (T, T), 1)
        tri = (cidx >= r).astype(F32)
        dlf = lax.dot_general(tri, dcv, NN, precision=lax.Precision.HIGHEST, preferred_element_type=F32)
        dlf = dlf + carry[0:1, :]
        carry[...] = carry[...] + jnp.sum(dcv, axis=0, keepdims=True)
        fg = f_ref[...] + b_ref[...]
        dfg = dlf / (1.0 + jnp.exp(fg))
        df_ref[...] = dfg.astype(BF16)
        db_ref[...] += jnp.sum(dfg, axis=0, keepdims=True)

    return pl.pallas_call(
        body, name="gate_bwd", grid=(nb,),
        in_specs=[pl.BlockSpec((T, LANES), lambda i: (nb - 1 - i, 0)),
                  pl.BlockSpec((T, LANES), lambda i: (nb - 1 - i, 4)),
                  pl.BlockSpec((1, LANES), lambda i: (0, 0))],
        out_specs=[pl.BlockSpec((T, LANES), lambda i: (nb - 1 - i, 0)), pl.BlockSpec((1, LANES), lambda i: (0, 0))],
        out_shape=[jax.ShapeDtypeStruct((S, LANES), BF16), jax.ShapeDtypeStruct((1, LANES), F32)],
        scratch_shapes=[pltpu.VMEM((8, LANES), F32)], compiler_params=_params("arbitrary"))(dc, uf, bpad)


FOX_CHUNK = 32
FOX_CHUNK_BWD = 64
HEAD_PAIRS = FOX_HEADS // 2
PAIR = 2


def _masked(s, row0, col0, diagonal):
    if diagonal:
        row = row0 + lax.broadcasted_iota(jnp.int32, s.shape, 0)
        col = col0 + lax.broadcasted_iota(jnp.int32, s.shape, 1)
        s = jnp.where(col <= row, s, -jnp.inf)
    return s


def _causal_pairs(n, query_major):
    if query_major:
        pairs = [(q, k) for q in range(n) for k in range(q + 1)]
    else:
        pairs = [(q, k) for k in range(n) for q in range(k, n)]
    return (jnp.asarray([p[0] for p in pairs], jnp.int32), jnp.asarray([p[1] for p in pairs], jnp.int32))


def _lane_block(b):
    return slice(b * LANES, (b + 1) * LANES)


def _fold(op, xs):
    acc = xs[0]
    for x in xs[1:]:
        acc = op(acc, x)
    return acc


def _head_lanes(hh):
    lane = lax.broadcasted_iota(jnp.int32, (1, LANES), 1)
    return (lane < FOX_HEAD_DIM) if hh == 0 else (lane >= FOX_HEAD_DIM)


def _pick(first_head, a, b):
    return jnp.where(first_head, a, b)


def _fox_fwd(qkv, cT, comm):
    S = qkv.shape[0]
    t = _row_tile(S)
    n = S // t
    nc = len(comm)
    scale = 1.0 / math.sqrt(FOX_HEAD_DIM)
    chunk = min(FOX_CHUNK, t)
    per_head = 4
    q_tab, k_tab = _causal_pairs(n, True)
    steps = q_tab.shape[0]

    def body(qt_ref, kt_ref, q_ref, k_ref, v_ref, c_ref, *rest):
        comm_in = rest[:nc]
        o_ref, ob_ref, lse_ref = rest[nc:nc + 3]
        comm_out = rest[nc + 3:2 * nc + 3]
        scr = rest[2 * nc + 3:2 * nc + 3 + PAIR * per_head]
        sems = rest[2 * nc + 3 + PAIR * per_head:]
        hp = pl.program_id(0)
        step_id = pl.program_id(1)
        qi = qt_ref[step_id]
        ki = kt_ref[step_id]

        if nc:
            @pl.when((hp == 0) & (step_id == 0))
            def _():
                _Gather(comm_in, comm_out, *sems).start()

            @pl.when((hp == HEAD_PAIRS - 1) & (step_id == 0))
            def _():
                _Gather(comm_in, comm_out, *sems).pass_on()

        @pl.when(ki == 0)
        def _():
            for hh in range(PAIR):
                m_s, l_s, a_s, acc_s = scr[hh * per_head:hh * per_head + 4]
                m_s[...] = jnp.full_like(m_s, -jnp.inf)
                l_s[...] = jnp.zeros_like(l_s)
                acc_s[...] = jnp.zeros_like(acc_s)

        def step(diagonal):
            q2 = q_ref[...] * scale
            k2 = k_ref[...]
            v2 = v_ref[...]
            scores = []
            for hh in range(PAIR):
                qm = jnp.where(_head_lanes(hh), q2, jnp.zeros_like(q2))
                scores.append(lax.dot_general(qm, k2, NT, preferred_element_type=F32))
            for hh in range(PAIR):
                m_s, l_s, a_s, acc_s = scr[hh * per_head:(hh + 1) * per_head]
                s_s = scores[hh]
                hi_rows, lo_rows = [], []
                for r in range(t // chunk):
                    rows = slice(r * chunk, (r + 1) * chunk)
                    blocks = [_masked(s_s[rows, _lane_block(b)] - c_ref[hh, :, _lane_block(b)], r * chunk,
                                      b * LANES, diagonal) for b in range(t // LANES)]
                    m_prev = m_s[rows, :]
                    m_new = jnp.maximum(m_prev, jnp.max(_fold(jnp.maximum, blocks), axis=1, keepdims=True))
                    alpha = jnp.exp(m_prev - m_new)
                    ps = [jnp.exp(blk - m_new) for blk in blocks]
                    l_s[rows, :] = alpha * l_s[rows, :] + jnp.sum(_fold(jnp.add, ps), axis=1, keepdims=True)
                    m_s[rows, :] = m_new
                    a_s[rows, :] = alpha
                    his = [p.astype(BF16) for p in ps]
                    hi_rows.append(jnp.concatenate(his, axis=1))
                    lo_rows.append(jnp.concatenate([(p - h.astype(F32)).astype(BF16) for p, h in zip(ps, his)],
                                                   axis=1))
                pv = (lax.dot_general(jnp.concatenate(hi_rows, axis=0), v2, NN, preferred_element_type=F32)
                      + lax.dot_general(jnp.concatenate(lo_rows, axis=0), v2, NN, preferred_element_type=F32))
                acc_s[...] = a_s[...] * acc_s[...] + pv

        @pl.when(ki < qi)
        def _():
            step(False)

        @pl.when(ki == qi)
        def _():
            step(True)
            heads = []
            for hh in range(PAIR):
                m_s, l_s, a_s, acc_s = scr[hh * per_head:hh * per_head + 4]
                heads.append(acc_s[...] / l_s[...])
                lse_ref[hh] = m_s[...] + jnp.log(l_s[...])
            o2 = _pick(_head_lanes(0), heads[0], heads[1])
            o_ref[...] = o2
            ob_ref[...] = o2.astype(BF16)

        if nc:
            @pl.when((hp == HEAD_PAIRS - 1) & (step_id == steps - 1))
            def _():
                _Gather(comm_in, comm_out, *sems).finish()

    def q_cols(first_block):
        return pl.BlockSpec((t, LANES), lambda h, s, qt, kt: (qt[s], first_block + h))

    def k_cols(first_block):
        return pl.BlockSpec((t, LANES), lambda h, s, qt, kt: (kt[s], first_block + h))

    any_spec = pl.BlockSpec(memory_space=pl.ANY)
    head_scratch = [pltpu.VMEM((t, LANES), F32)] * per_head
    grid_spec = pltpu.PrefetchScalarGridSpec(
        num_scalar_prefetch=2, grid=(HEAD_PAIRS, steps),
        in_specs=[q_cols(0), k_cols(HEAD_PAIRS), k_cols(2 * HEAD_PAIRS),
                  pl.BlockSpec((PAIR, 1, t), lambda h, s, qt, kt: (h, 0, kt[s]))] + [any_spec] * nc,
        out_specs=[q_cols(0), q_cols(0),
                   pl.BlockSpec((PAIR, t, LANES), lambda h, s, qt, kt: (h, qt[s], 0))] + [any_spec] * nc,
        scratch_shapes=head_scratch * PAIR + _comm_scratch(nc))
    return pl.pallas_call(
        body, name="fox_fwd", grid_spec=grid_spec,
        out_shape=[jax.ShapeDtypeStruct((S, FOX_WIDTH), F32), jax.ShapeDtypeStruct((S, FOX_WIDTH), BF16),
                   jax.ShapeDtypeStruct((FOX_HEADS, S, LANES), F32)] + _comm_shapes(comm),
        compiler_params=_params("arbitrary", "arbitrary"))(q_tab, k_tab, qkv, qkv, qkv, cT, *comm)


def _fox_bwd(qkv, cT, o, lse, do, comm):
    S = qkv.shape[0]
    t = _row_tile(S)
    n = S // t
    nc = len(comm)
    scale = 1.0 / math.sqrt(FOX_HEAD_DIM)
    chunk = min(FOX_CHUNK_BWD, t)
    per_head = 2
    q_tab, k_tab = _causal_pairs(n, False)
    steps = q_tab.shape[0]

    def body(qt_ref, kt_ref, q_ref, k_ref, v_ref, c_ref, o_ref, do_ref, lse_ref, *rest):
        comm_in = rest[:nc]
        dq_ref, dk_ref, dv_ref, dc_ref = rest[nc:nc + 4]
        comm_out = rest[nc + 4:2 * nc + 4]
        dq_s, dk_s, dv_s = rest[2 * nc + 4:2 * nc + 7]
        scr = rest[2 * nc + 7:2 * nc + 7 + PAIR * per_head]
        sems = rest[2 * nc + 7 + PAIR * per_head:]
        hp = pl.program_id(0)
        step_id = pl.program_id(1)
        qi = qt_ref[step_id]
        ki = kt_ref[step_id]

        if nc:
            @pl.when((hp == 0) & (step_id == 0))
            def _():
                for cp in _comm_copies(comm_in, comm_out, *sems):
                    cp.start()

        @pl.when(step_id == 0)
        def _():
            dq_s[...] = jnp.zeros_like(dq_s)

        @pl.when(qi == ki)
        def _():
            dk_s[...] = jnp.zeros_like(dk_s)
            dv_s[...] = jnp.zeros_like(dv_s)
            for hh in range(PAIR):
                dc_s = scr[hh * per_head]
                dc_s[...] = jnp.zeros_like(dc_s)

        def step(diagonal):
            q2 = q_ref[...]
            k2 = k_ref[...]
            v2 = v_ref[...]
            do2 = do_ref[...]
            prod = do2.astype(F32) * o_ref[...]
            grads = []
            for hh in range(PAIR):
                dc_s, delta_s = scr[hh * per_head:(hh + 1) * per_head]
                mine = _head_lanes(hh)
                s_s = lax.dot_general(jnp.where(mine, q2 * scale, jnp.zeros_like(q2)), k2, NT,
                                      preferred_element_type=F32)
                dp_s = lax.dot_general(jnp.where(mine, do2, jnp.zeros_like(do2)), v2, NT, preferred_element_type=F32)
                delta_s[...] = jnp.broadcast_to(jnp.sum(jnp.where(mine, prod, 0.0), axis=1, keepdims=True),
                                                (t, LANES))
                dc8 = [jnp.zeros((8, LANES), F32) for _ in range(t // LANES)]
                p_rows, ds_rows = [], []
                for r in range(t // chunk):
                    rows = slice(r * chunk, (r + 1) * chunk)
                    lse = lse_ref[hh, rows, :]
                    delta = delta_s[rows, :]
                    p_blocks, ds_blocks = [], []
                    for b in range(t // LANES):
                        s = _masked(s_s[rows, _lane_block(b)] - c_ref[hh, :, _lane_block(b)], r * chunk, b * LANES,
                                    diagonal)
                        p = jnp.exp(s - lse)
                        ds = p * (dp_s[rows, _lane_block(b)] - delta)
                        p_blocks.append(p.astype(BF16))
                        ds_blocks.append(ds.astype(BF16))
                        dc8[b] = dc8[b] + jnp.sum(ds.reshape(chunk // 8, 8, LANES), axis=0)
                    p_rows.append(jnp.concatenate(p_blocks, axis=1))
                    ds_rows.append(jnp.concatenate(ds_blocks, axis=1))
                for b in range(t // LANES):
                    dc_s[:, _lane_block(b)] += jnp.sum(dc8[b], axis=0, keepdims=True)
                dsb = jnp.concatenate(ds_rows, axis=0)
                grads.append((lax.dot_general(jnp.concatenate(p_rows, axis=0), do2, TN, preferred_element_type=F32),
                              lax.dot_general(dsb, k2, NN, preferred_element_type=F32),
                              lax.dot_general(dsb, q2, TN, preferred_element_type=F32)))
            first = _head_lanes(0)
            dv_s[...] += _pick(first, grads[0][0], grads[1][0])
            q_rows = pl.ds(pl.multiple_of(qi * t, t), t)
            dq_s[q_rows, :] += _pick(first, grads[0][1], grads[1][1]) * scale
            dk_s[...] += _pick(first, grads[0][2], grads[1][2]) * scale

        @pl.when(qi > ki)
        def _():
            step(False)

        @pl.when(qi == ki)
        def _():
            step(True)

        @pl.when(qi == n - 1)
        def _():
            dk_ref[...] = dk_s[...].astype(BF16)
            dv_ref[...] = dv_s[...].astype(BF16)
            for hh in range(PAIR):
                dc_ref[hh] = -scr[hh * per_head][...]

        @pl.when(step_id == steps - 1)
        def _():
            dq_ref[...] = dq_s[...].astype(BF16)

        if nc:
            @pl.when((hp == HEAD_PAIRS - 1) & (step_id == steps - 1))
            def _():
                for cp in _comm_copies(comm_in, comm_out, *sems):
                    cp.wait()

    def q_side(first_block):
        return pl.BlockSpec((t, LANES), lambda h, s, qt, kt: (qt[s], first_block + h))

    def k_side(first_block):
        return pl.BlockSpec((t, LANES), lambda h, s, qt, kt: (kt[s], first_block + h))

    any_spec = pl.BlockSpec(memory_space=pl.ANY)
    head_scratch = [pltpu.VMEM((1, t), F32), pltpu.VMEM((t, LANES), F32)]
    grad_shape = jax.ShapeDtypeStruct((S, FOX_WIDTH), BF16)
    grid_spec = pltpu.PrefetchScalarGridSpec(
        num_scalar_prefetch=2, grid=(HEAD_PAIRS, steps),
        in_specs=[q_side(0), k_side(HEAD_PAIRS), k_side(2 * HEAD_PAIRS),
                  pl.BlockSpec((PAIR, 1, t), lambda h, s, qt, kt: (h, 0, kt[s])), q_side(0), q_side(0),
                  pl.BlockSpec((PAIR, t, LANES), lambda h, s, qt, kt: (h, qt[s], 0))] + [any_spec] * nc,
        out_specs=[pl.BlockSpec((S, LANES), lambda h, s, qt, kt: (0, h)), k_side(0), k_side(0),
                   pl.BlockSpec((PAIR, 1, t), lambda h, s, qt, kt: (h, 0, kt[s]))] + [any_spec] * nc,
        scratch_shapes=[pltpu.VMEM((S, LANES), F32), pltpu.VMEM((t, LANES), F32), pltpu.VMEM((t, LANES), F32)]
        + head_scratch * PAIR + _comm_scratch(nc))
    return pl.pallas_call(
        body, name="fox_bwd", grid_spec=grid_spec,
        out_shape=[grad_shape, grad_shape, grad_shape, jax.ShapeDtypeStruct((FOX_HEADS, 1, S), F32)]
        + _comm_shapes(comm),
        compiler_params=_params("arbitrary", "arbitrary"))(q_tab, k_tab, qkv, qkv, qkv, cT, o, do, lse, *comm)


def _lanes(g):
    return slice(g * POOL_GROUP_DIM, (g + 1) * POOL_GROUP_DIM)


def _window_sum(e, win, back):
    rows = e.shape[0]
    s = e
    sh = 1
    while sh < win:
        s = s + pltpu.roll(s, sh if back else rows - sh, 0)
        sh *= 2
    return s


def _pooled(u_ref, up_ref, i, g, win, T):
    cur = u_ref[:, _lanes(g)]
    tail = jnp.where(i > 0, up_ref[T - POOL_HALO:T, _lanes(g)], 0.0)
    e = jnp.concatenate([tail, cur], axis=0)
    s = _window_sum(e, win, True)
    t_idx = i * T - POOL_HALO + lax.broadcasted_iota(jnp.int32, (T + POOL_HALO, POOL_GROUP_DIM), 0)
    cnt = jnp.clip(t_idx + 1, 1, win).astype(F32)
    return (s / cnt - e)[POOL_HALO:, :]


def _pool_fwd(uf, pw, ps):
    S = uf.shape[0]
    T = _row_tile(S)

    def body(u_ref, up_ref, w_ref, sc_ref, o_ref):
        i = pl.program_id(0)
        for g, win in enumerate(POOL_WINDOWS):
            pb = _pooled(u_ref, up_ref, i, g, win, T).astype(BF16)
            yv = lax.dot_general(pb, w_ref[g], NN, preferred_element_type=F32)
            o_ref[:, _lanes(g)] = (yv * sc_ref[:, _lanes(g)]).astype(BF16)

    return pl.pallas_call(
        body, name="pool_fwd", grid=(S // T,),
        in_specs=[pl.BlockSpec((T, POOL_WIDTH), lambda i: (i, 0)),
                  pl.BlockSpec((T, POOL_WIDTH), lambda i: (jnp.maximum(i - 1, 0), 0)),
                  pl.BlockSpec((4, POOL_GROUP_DIM, POOL_GROUP_DIM), lambda i: (0, 0, 0)),
                  pl.BlockSpec((1, POOL_WIDTH), lambda i: (0, 0))],
        out_specs=pl.BlockSpec((T, POOL_WIDTH), lambda i: (i, 0)),
        out_shape=jax.ShapeDtypeStruct((S, POOL_WIDTH), BF16), compiler_params=_params("parallel"))(uf, uf, pw, ps)


def _pool_bwd(uf, dpool, pw, ps):
    S = uf.shape[0]
    T = _row_tile(S)
    nb = S // T

    def body(u_ref, up_ref, d_ref, dn_ref, w_ref, sc_ref, du_ref, dw_ref, dsc_ref):
        i = pl.program_id(0)

        @pl.when(i == 0)
        def _():
            dw_ref[...] = jnp.zeros_like(dw_ref)
            dsc_ref[...] = jnp.zeros_like(dsc_ref)

        t_idx = i * T + lax.broadcasted_iota(jnp.int32, (T + POOL_HALO, POOL_GROUP_DIM), 0)
        for g, win in enumerate(POOL_WINDOWS):
            pb = _pooled(u_ref, up_ref, i, g, win, T).astype(BF16)
            w = w_ref[g]
            sc = sc_ref[:, _lanes(g)]
            yv = lax.dot_general(pb, w, NN, preferred_element_type=F32)
            dov = d_ref[:, _lanes(g)]
            dsc_ref[:, _lanes(g)] += jnp.sum(dov * yv, axis=0, keepdims=True)
            head = jnp.where(i < nb - 1, dn_ref[0:POOL_HALO, _lanes(g)], 0.0)
            dyb = (jnp.concatenate([dov, head], axis=0) * sc).astype(BF16)
            dw_ref[g] += lax.dot_general(pb, dyb[:T], TN, preferred_element_type=F32)
            dpl = lax.dot_general(dyb, w, NT, preferred_element_type=F32)
            cnt = jnp.minimum(t_idx + 1, win).astype(F32)
            a = _window_sum(dpl / cnt, win, False)
            du_ref[:, _lanes(g)] = (a - dpl)[:T].astype(BF16)

    return pl.pallas_call(
        body, name="pool_bwd", grid=(nb,),
        in_specs=[pl.BlockSpec((T, POOL_WIDTH), lambda i: (i, 0)),
                  pl.BlockSpec((T, POOL_WIDTH), lambda i: (jnp.maximum(i - 1, 0), 0)),
                  pl.BlockSpec((T, POOL_WIDTH), lambda i: (i, 0)),
                  pl.BlockSpec((T, POOL_WIDTH), lambda i: (jnp.minimum(i + 1, nb - 1), 0)),
                  pl.BlockSpec((4, POOL_GROUP_DIM, POOL_GROUP_DIM), lambda i: (0, 0, 0)),
                  pl.BlockSpec((1, POOL_WIDTH), lambda i: (0, 0))],
        out_specs=[pl.BlockSpec((T, POOL_WIDTH), lambda i: (i, 0)),
                   pl.BlockSpec((4, POOL_GROUP_DIM, POOL_GROUP_DIM), lambda i: (0, 0, 0)),
                   pl.BlockSpec((1, POOL_WIDTH), lambda i: (0, 0))],
        out_shape=[jax.ShapeDtypeStruct((S, POOL_WIDTH), BF16),
                   jax.ShapeDtypeStruct((4, POOL_GROUP_DIM, POOL_GROUP_DIM), F32),
                   jax.ShapeDtypeStruct((1, POOL_WIDTH), F32)],
        compiler_params=_params("arbitrary"))(uf, uf, dpool, dpool, pw, ps)


def _xhead(h):
    return slice(h * X_HEAD_DIM, (h + 1) * X_HEAD_DIM)


def _xvhead(h):
    return slice(D_MODEL + h * X_HEAD_DIM, D_MODEL + (h + 1) * X_HEAD_DIM)


X_CHUNK = 32


def _x_probs(s_ref, rows):
    blocks = [s_ref[rows, _lane_block(b)] * (1.0 / math.sqrt(X_HEAD_DIM)) for b in range(MEM_LEN // LANES)]
    m = jnp.max(_fold(jnp.maximum, blocks), axis=1, keepdims=True)
    es = [jnp.exp(blk - m) for blk in blocks]
    den = jnp.sum(_fold(jnp.add, es), axis=1, keepdims=True)
    return [e / den for e in es]


def _xattn_fwd(q, kv):
    S = q.shape[0]
    t = _row_tile(S)
    chunk = min(X_CHUNK, t)

    def body(q_ref, kv_ref, o_ref):
        for h in range(X_HEADS):
            s = lax.dot_general(q_ref[:, _xhead(h)], kv_ref[:, _xhead(h)], NT, preferred_element_type=F32)
            p_rows = []
            for r in range(t // chunk):
                rows = slice(r * chunk, (r + 1) * chunk)
                p_rows.append(jnp.concatenate([p.astype(BF16) for p in _x_probs(s, rows)], axis=1))
            o_ref[:, _xhead(h)] = lax.dot_general(jnp.concatenate(p_rows, axis=0), kv_ref[:, _xvhead(h)], NN,
                                                  preferred_element_type=F32).astype(BF16)

    return pl.pallas_call(
        body, name="xattn_fwd", grid=(S // t,),
        in_specs=[pl.BlockSpec((t, D_MODEL), lambda i: (i, 0)), pl.BlockSpec((MEM_LEN, 2 * D_MODEL), lambda i: (0, 0))],
        out_specs=pl.BlockSpec((t, D_MODEL), lambda i: (i, 0)),
        out_shape=jax.ShapeDtypeStruct((S, D_MODEL), BF16), compiler_params=_params("parallel"))(q, kv)


def _xattn_bwd(q, kv, do):
    S = q.shape[0]
    t = _row_tile(S)
    nb = S // t
    scale = 1.0 / math.sqrt(X_HEAD_DIM)
    chunk = min(X_CHUNK, t)

    def body(q_ref, kv_ref, do_ref, dq_ref, dkv_ref, acc):
        i = pl.program_id(0)

        @pl.when(i == 0)
        def _():
            acc[...] = jnp.zeros_like(acc)

        for h in range(X_HEADS):
            qh = q_ref[:, _xhead(h)]
            kh = kv_ref[:, _xhead(h)]
            doh = do_ref[:, _xhead(h)]
            s_s = lax.dot_general(qh, kh, NT, preferred_element_type=F32)
            dp_s = lax.dot_general(doh, kv_ref[:, _xvhead(h)], NT, preferred_element_type=F32)
            p_rows, ds_rows = [], []
            for r in range(t // chunk):
                rows = slice(r * chunk, (r + 1) * chunk)
                ps = _x_probs(s_s, rows)
                dps = [dp_s[rows, _lane_block(b)] for b in range(len(ps))]
                inner = jnp.sum(_fold(jnp.add, [dp * p for dp, p in zip(dps, ps)]), axis=1, keepdims=True)
                p_rows.append(jnp.concatenate([p.astype(BF16) for p in ps], axis=1))
                ds_rows.append(jnp.concatenate([(p * (dp - inner)).astype(BF16) for dp, p in zip(dps, ps)], axis=1))
            dsb = jnp.concatenate(ds_rows, axis=0)
            acc[:, _xvhead(h)] += lax.dot_general(jnp.concatenate(p_rows, axis=0), doh, TN,
                                                  preferred_element_type=F32)
            dq_ref[:, _xhead(h)] = (lax.dot_general(dsb, kh, NN, preferred_element_type=F32) * scale).astype(BF16)
            acc[:, _xhead(h)] += lax.dot_general(dsb, qh, TN, preferred_element_type=F32) * scale

        @pl.when(i == nb - 1)
        def _():
            dkv_ref[...] = acc[...].astype(BF16)

    row = pl.BlockSpec((t, D_MODEL), lambda i: (i, 0))
    full = pl.BlockSpec((MEM_LEN, 2 * D_MODEL), lambda i: (0, 0))
    return pl.pallas_call(
        body, name="xattn_bwd", grid=(nb,), in_specs=[row, full, row], out_specs=[row, full],
        out_shape=[jax.ShapeDtypeStruct((S, D_MODEL), BF16), jax.ShapeDtypeStruct((MEM_LEN, 2 * D_MODEL), BF16)],
        scratch_shapes=[pltpu.VMEM((MEM_LEN, 2 * D_MODEL), F32)],
        compiler_params=_params("arbitrary"))(q, kv, do)


def _comm_shapes(arrs):
    return [jax.ShapeDtypeStruct((N_DEV,) + tuple(a.shape[-2:]), a.dtype) for a in arrs]


def _comm_scratch(n):
    if n == 0:
        return []
    return [pltpu.SemaphoreType.DMA((n, N_DEV - 1)), pltpu.SemaphoreType.DMA((n, N_DEV - 1)),
            pltpu.SemaphoreType.DMA((n,))]


def _comm_copies(ins, outs, send_sems, recv_sems, local_sems):
    x, y, c = lax.axis_index("x"), lax.axis_index("y"), lax.axis_index("c")
    me = 4 * x + 2 * y + c
    copies = []
    for w in range(len(ins)):
        src = ins[w] if len(ins[w].shape) == 2 else ins[w].at[me]
        copies.append(pltpu.make_async_copy(src, outs[w].at[me], local_sems.at[w]))
    for k in range(1, N_DEV):
        px = 1 - x if k & 4 else x
        py = 1 - y if k & 2 else y
        pc = 1 - c if k & 1 else c
        peer = 4 * px + 2 * py + pc
        for w in range(len(ins)):
            src = ins[w] if len(ins[w].shape) == 2 else ins[w].at[peer]
            copies.append(pltpu.make_async_remote_copy(
                src_ref=src, dst_ref=outs[w].at[me], send_sem=send_sems.at[w, k - 1],
                recv_sem=recv_sems.at[w, k - 1], device_id=(px, py, pc), device_id_type=pl.DeviceIdType.MESH))
    return copies


class _Gather:
    def __init__(self, ins, outs, send_sems, recv_sems, local_sems):
        x, y, c = lax.axis_index("x"), lax.axis_index("y"), lax.axis_index("c")
        me = 4 * x + 2 * y + c
        sibling = (x, y, 1 - c)
        self.local, self.mine, self.passed = [], [], []
        for w in range(len(ins)):
            def remote(idx, src, slot, dev, w=w):
                return pltpu.make_async_remote_copy(
                    src_ref=src, dst_ref=outs[w].at[slot], send_sem=send_sems.at[w, idx],
                    recv_sem=recv_sems.at[w, idx], device_id=dev, device_id_type=pl.DeviceIdType.MESH)

            self.local.append(pltpu.make_async_copy(ins[w], outs[w].at[me], local_sems.at[w]))
            mine, passed = [remote(0, ins[w], me, sibling)], []
            for j, (fx, fy) in enumerate(((0, 1), (1, 0), (1, 1))):
                px = 1 - x if fx else x
                py = 1 - y if fy else y
                slot = 4 * px + 2 * py + c
                mine.append(remote(1 + j, ins[w], me, (px, py, c)))
                passed.append(remote(4 + j, outs[w].at[slot], slot, sibling))
            self.mine.append(mine)
            self.passed.append(passed)

    def start(self):
        for cp in self.local:
            cp.start()
        for mine in self.mine:
            for cp in mine:
                cp.start()

    def pass_on(self):
        for mine, passed in zip(self.mine, self.passed):
            for j, cp in enumerate(passed):
                mine[1 + j].wait_recv()
                cp.start()

    def finish(self):
        for mine, passed in zip(self.mine, self.passed):
            mine[0].wait_recv()
            for cp in passed:
                cp.wait_recv()
            for cp in mine + passed:
                cp.wait_send()
        for cp in self.local:
            cp.wait()


def _exchange(name, arrs):
    n = len(arrs)
    gather = all(a.ndim == 2 for a in arrs)

    def body(*refs):
        if gather:
            g = _Gather(refs[:n], refs[n:2 * n], *refs[2 * n:])
            g.start()
            g.pass_on()
            g.finish()
            return
        copies = _comm_copies(refs[:n], refs[n:2 * n], *refs[2 * n:])
        for cp in copies:
            cp.start()
        for cp in copies:
            cp.wait()

    any_spec = pl.BlockSpec(memory_space=pl.ANY)
    return pl.pallas_call(
        body, name=name, in_specs=[any_spec] * n, out_specs=[any_spec] * n, out_shape=_comm_shapes(arrs),
        scratch_shapes=_comm_scratch(n))(*arrs)


def _adamw_math(w, g, m, v):
    m = ADAM_B1 * m + (1.0 - ADAM_B1) * g
    v = ADAM_B2 * v + (1.0 - ADAM_B2) * (g * g)
    m_hat = m / (1.0 - ADAM_B1 ** ADAM_STEP)
    v_hat = v / (1.0 - ADAM_B2 ** ADAM_STEP)
    delta = -ADAM_LR * (m_hat / (jnp.sqrt(v_hat) + ADAM_EPS) + ADAM_WD * w)
    return delta, m, v


def _sum_parts(p_ref):
    g = p_ref[0].astype(F32)
    for s in range(1, N_DEV):
        g = g + p_ref[s].astype(F32)
    return g


def _adamw_big(name, w, m, v, parts, tr):
    L, R, C = w.shape

    def body(w_ref, m_ref, v_ref, *rest):
        p_refs = rest[:L]
        g_ref, d_ref, nm_ref, nv_ref = rest[L:]
        layer = pl.program_id(0)
        for j in range(L):
            @pl.when(layer == j)
            def _(j=j):
                g = _sum_parts(p_refs[j])
                delta, nm, nv = _adamw_math(w_ref[...], g, m_ref[...], v_ref[...])
                g_ref[...] = g
                d_ref[...] = delta
                nm_ref[...] = nm
                nv_ref[...] = nv

    blk = pl.BlockSpec((None, tr, C), lambda l, i: (l, i, 0))

    def part_spec(j):
        return pl.BlockSpec((N_DEV, tr, C), lambda l, i: (0, jnp.where(l == j, i, 0), 0))

    shp = jax.ShapeDtypeStruct((L, R, C), F32)
    return pl.pallas_call(
        body, name=name, grid=(L, R // tr), in_specs=[blk, blk, blk] + [part_spec(j) for j in range(L)],
        out_specs=[blk] * 4, out_shape=[shp] * 4, compiler_params=_params("arbitrary", "arbitrary"))(w, m, v, *parts)


def _adamw_small(w, m, v, parts):
    R, C = w.shape

    def body(w_ref, m_ref, v_ref, p_ref, g_ref, d_ref, nm_ref, nv_ref):
        g = _sum_parts(p_ref)
        delta, nm, nv = _adamw_math(w_ref[...], g, m_ref[...], v_ref[...])
        g_ref[...] = g
        d_ref[...] = delta
        nm_ref[...] = nm
        nv_ref[...] = nv

    shp = jax.ShapeDtypeStruct((R, C), F32)
    return pl.pallas_call(body, name="adamw_small", out_shape=[shp] * 4,
                          compiler_params=pltpu.CompilerParams(vmem_limit_bytes=VMEM_LIMIT))(w, m, v, parts)


def _vec(a):
    return a.reshape(1, -1)


W_IN_SHARD = IN_COLS // N_DEV
W_IN_ROWS = 272


def _w_in_travel(a):
    pad = [(0, 0)] * (a.ndim - 2) + [(0, W_IN_ROWS - W_IN_SHARD), (0, 0)]
    return jnp.pad(jnp.swapaxes(a, -1, -2), pad)


def _unpack_w_in(g):
    full = jnp.transpose(g[:, :W_IN_SHARD, :], (2, 0, 1)).reshape(D_MODEL, IN_COLS)
    qkv = full[:, :QKV_COLS]
    f = full[:, QKV_COLS:QKV_COLS + FOX_HEADS]
    u = full[:, QKV_COLS + FOX_HEADS:]
    uf = jnp.concatenate([u, f, jnp.zeros((D_MODEL, UF_COLS - POOL_WIDTH - FOX_HEADS), g.dtype)], axis=1)
    return jnp.concatenate([qkv, uf], axis=1)


def _pack_dw_in(dwp):
    qkv = dwp[:, :QKV_COLS]
    u = dwp[:, QKV_COLS:QKV_COLS + POOL_WIDTH]
    f = dwp[:, QKV_COLS + POOL_WIDTH:QKV_COLS + POOL_WIDTH + FOX_HEADS]
    full = jnp.concatenate([qkv, f, u], axis=1)
    return _w_in_travel(jnp.transpose(full.reshape(D_MODEL, N_DEV, W_IN_SHARD), (1, 0, 2)))


REST = ['w_out', 'wq_x', 'wkv_x', 'wo_x', 'w_up', 'w_down']


def _layer_fwd(x0, h1, mem, sp, g_in, shards, g_next):
    S = x0.shape[0]
    sv = {"x0": x0}
    w_inp = _unpack_w_in(g_in)
    qkv, uf = _mm_rows("mm_in", [(h1, w_inp, "nn")],
                       [(BF16, 0, QKV_COLS, "id"), (F32, QKV_COLS, UF_COLS, "id")], piece=UF_COLS)
    c = _gate_fwd(uf, sp["b_forget"])
    cT = jnp.transpose(c[:, :FOX_HEADS]).reshape(FOX_HEADS, 1, S)
    o, ob, lse, *got = _fox_fwd(qkv, cT, shards)
    g_out, g_q, g_kv, g_o, g_up, g_down = got[:6]
    W = dict(inp=w_inp, out=g_out.reshape(D_MODEL, D_MODEL), q=g_q.reshape(D_MODEL, D_MODEL), kv=g_kv,
             o=g_o.reshape(D_MODEL, D_MODEL), up=g_up, down=g_down.reshape(D_FF, D_MODEL))
    pool = _pool_fwd(uf, sp["pool_w"], sp["pool_scale"])
    cat = jnp.concatenate([ob, pool], axis=1)
    mix, x1, h2 = _mm_resid_norm("mm_sq_norm", cat, W["out"], x0, sp["g_mix_post"], sp["g_x_pre"])
    mn = _norm_fwd("norm_mem", mem, sp["g_mem"])
    q2 = _mm1("mm_q", h2, W["q"], "nn", D_MODEL, BF16)
    kv = _mm1("mm_kv", mn, W["kv"], "nn3", 2 * D_MODEL, BF16, piece=2 * D_MODEL // N_DEV)
    o2 = _xattn_fwd(q2, kv)
    xo, x2, h3 = _mm_resid_norm("mm_sq_norm", o2, W["o"], x1, sp["g_x_post"], sp["g_ffn_pre"])
    up, act = _mm_rows("mm_up", [(h3, W["up"], "nn3")], [(BF16, 0, D_FF, "id"), (BF16, 0, D_FF, "relu2")],
                       piece=D_FF // N_DEV)
    y, x3, h_next = _mm_resid_norm("mm_down_norm" if g_next is not None else "mm_down_norm_last", act, W["down"], x2,
                                   sp["g_ffn_post"], g_next)
    sv.update(h1=h1, uf=uf, cT=cT, qkv=qkv, o=o, lse=lse, cat=cat, mix=mix, x1=x1, h2=h2, mn=mn, q2=q2, kv=kv,
              o2=o2, xo=xo, x2=x2, h3=h3, up=up, act=act, y=y)
    return x3, h_next, sv, W, (got[6] if len(got) > 6 else None)


def _layer_bwd(dx3, dy, mem, sv, sp, W, carried, below):
    S = dx3.shape[0]
    gs = {}
    gb = {}
    (dup,) = _mm_rows("mm_dup", [(dy, W["down"], "nt")], [(BF16, 0, D_FF, "drelu2")], extra=sv["up"])
    gb["w_down"] = _mm_tn("mm_dw_down", sv["act"], dy, BF16).reshape(N_DEV, D_FF // N_DEV, D_MODEL)
    gb["w_up"] = _mm_tn("mm_dw_up", sv["h3"], dup, BF16, shard_cols=D_FF // N_DEV)
    dx2, gs["g_ffn_pre"], dxo, gs["g_x_post"] = _norm_bwd(
        "mm_dh3_norm_bwd", (dup, W["up"], "nt3"), sv["x2"], sp["g_ffn_pre"], dx3, F32,
        below=(sv["xo"], sp["g_x_post"]))
    do2 = _mm1("mm_sq_t", dxo, W["o"], "nt", D_MODEL, BF16)
    gb["wo_x"] = _mm_tn("mm_dw_sq", sv["o2"], dxo, BF16).reshape(N_DEV, D_MODEL // N_DEV, D_MODEL)
    dq2, dkvb = _xattn_bwd(sv["q2"], sv["kv"], do2)
    gb["wq_x"] = _mm_tn("mm_dw_sq", sv["h2"], dq2, BF16).reshape(N_DEV, D_MODEL // N_DEV, D_MODEL)
    gb["wkv_x"] = _mm_tn("mm_dw_kv", sv["mn"], dkvb, BF16, shard_cols=2 * D_MODEL // N_DEV)
    dmn = _mm1("mm_dmn", dkvb, W["kv"], "nt3", D_MODEL, F32)
    _, gs["g_mem"] = _norm_bwd("norm_bwd_mem", dmn, mem, sp["g_mem"], None, BF16)
    dx1, gs["g_x_pre"], dmix, gs["g_mix_post"] = _norm_bwd(
        "mm_dh2_norm_bwd", (dq2, W["q"], "nt"), sv["x1"], sp["g_x_pre"], dx2, F32,
        below=(sv["mix"], sp["g_mix_post"]))
    doh, dpool = _mm_rows("mm_dcat", [(dmix, W["out"], "nt")],
                          [(BF16, 0, FOX_WIDTH, "id"), (F32, FOX_WIDTH, POOL_WIDTH, "id")])
    gb["w_out"] = _mm_tn("mm_dw_sq", sv["cat"], dmix, BF16).reshape(N_DEV, D_MODEL // N_DEV, D_MODEL)
    du, gs["pool_w"], gs["pool_scale"] = _pool_bwd(sv["uf"], dpool, sp["pool_w"], sp["pool_scale"])
    dq, dk, dv, dcT, *got = _fox_bwd(sv["qkv"], sv["cT"], sv["o"], sv["lse"], doh, [gb[n] for n in REST] + carried)
    dc = jnp.pad(jnp.transpose(dcT.reshape(FOX_HEADS, S)), ((0, 0), (0, LANES - FOX_HEADS)))
    dfg, db = _gate_bwd(dc, sv["uf"], sp["b_forget"])
    gs["b_forget"] = db[:, :FOX_HEADS]
    dproj = jnp.concatenate([dq, dk, dv, du, dfg], axis=1)
    dwp = _mm_tn("mm_dw_in", sv["h1"], dproj, BF16, piece=UF_COLS)
    dh1 = (dproj, W["inp"], "nt")
    if below is None:
        dx0, gs["g_mix_pre"] = _norm_bwd("mm_dh1_norm_bwd_first", dh1, sv["x0"], sp["g_mix_pre"], dx1, F32)
        lower = None
    else:
        dx0, gs["g_mix_pre"], *lower = _norm_bwd("mm_dh1_norm_bwd", dh1, sv["x0"], sp["g_mix_pre"], dx1, F32,
                                                 below=below)
    return dx0, lower, dict(zip(REST, got[:6])), got[6:], _pack_dw_in(dwp), gs


def _small_rows(shape):
    return -(-math.prod(shape) // (8 * LANES)) * 8


def _pack_small(d):
    blocks = []
    for n in SMALL:
        rows = _small_rows(d[n].shape)
        if d[n].shape[-1] == LANES:
            blocks.append(d[n].reshape(rows, LANES))
        else:
            flat = d[n].reshape(-1)
            blocks.append(jnp.pad(flat, (0, rows * LANES - flat.shape[0])).reshape(rows, LANES))
    return jnp.concatenate(blocks, axis=0)


def _unpack_small(packed, like):
    out = {}
    row = 0
    for n in SMALL:
        shape = like[n].shape
        rows = _small_rows(shape)
        block = packed[row:row + rows]
        out[n] = block.reshape(shape) if shape[-1] == LANES else block.reshape(-1)[:math.prod(shape)].reshape(shape)
        row += rows
    return out


def kernel(x, mem, g_mix_pre, w_in, b_forget, pool_w, pool_scale, w_out, g_mix_post, g_x_pre, g_mem, wq_x, wkv_x, wo_x, g_x_post, g_ffn_pre, w_up, w_down, g_ffn_post, loss_target, m_g_mix_pre, m_w_in, m_b_forget, m_pool_w, m_pool_scale, m_w_out, m_g_mix_post, m_g_x_pre, m_g_mem, m_wq_x, m_wkv_x, m_wo_x, m_g_x_post, m_g_ffn_pre, m_w_up, m_w_down, m_g_ffn_post, v_g_mix_pre, v_w_in, v_b_forget, v_pool_w, v_pool_scale, v_w_out, v_g_mix_post, v_g_x_pre, v_g_mem, v_wq_x, v_wkv_x, v_wo_x, v_g_x_post, v_g_ffn_pre, v_w_up, v_w_down, v_g_ffn_post):
    w = dict(g_mix_pre=g_mix_pre, w_in=w_in, b_forget=b_forget, pool_w=pool_w, pool_scale=pool_scale, w_out=w_out,
             g_mix_post=g_mix_post, g_x_pre=g_x_pre, g_mem=g_mem, wq_x=wq_x, wkv_x=wkv_x, wo_x=wo_x,
             g_x_post=g_x_post, g_ffn_pre=g_ffn_pre, w_up=w_up, w_down=w_down, g_ffn_post=g_ffn_post)
    mom = dict(g_mix_pre=m_g_mix_pre, w_in=m_w_in, b_forget=m_b_forget, pool_w=m_pool_w, pool_scale=m_pool_scale,
               w_out=m_w_out, g_mix_post=m_g_mix_post, g_x_pre=m_g_x_pre, g_mem=m_g_mem, wq_x=m_wq_x,
               wkv_x=m_wkv_x, wo_x=m_wo_x, g_x_post=m_g_x_post, g_ffn_pre=m_g_ffn_pre, w_up=m_w_up,
               w_down=m_w_down, g_ffn_post=m_g_ffn_post)
    var = dict(g_mix_pre=v_g_mix_pre, w_in=v_w_in, b_forget=v_b_forget, pool_w=v_pool_w, pool_scale=v_pool_scale,
               w_out=v_w_out, g_mix_post=v_g_mix_post, g_x_pre=v_g_x_pre, g_mem=v_g_mem, wq_x=v_wq_x,
               wkv_x=v_wkv_x, wo_x=v_wo_x, g_x_post=v_g_x_post, g_ffn_pre=v_g_ffn_pre, w_up=v_w_up,
               w_down=v_w_down, g_ffn_post=v_g_ffn_post)
    S = x.shape[1]
    xs = x.reshape(S, D_MODEL)
    mems = mem.reshape(MEM_LEN, D_MODEL)
    target = loss_target.reshape(S, D_MODEL)

    def small_params(l):
        return dict(
            g_mix_pre=_vec(g_mix_pre[l]), g_mix_post=_vec(g_mix_post[l]), g_x_pre=_vec(g_x_pre[l]),
            g_mem=_vec(g_mem[l]), g_x_post=_vec(g_x_post[l]), g_ffn_pre=_vec(g_ffn_pre[l]),
            g_ffn_post=_vec(g_ffn_post[l]), pool_scale=_vec(pool_scale[l]), pool_w=pool_w[l].astype(BF16),
            b_forget=jnp.pad(_vec(b_forget[l]), ((0, 0), (0, LANES - FOX_HEADS))))

    shard = {n: [w[n][l].astype(BF16) for l in range(DEPTH)] for n in REST}
    shard["w_in"] = [_w_in_travel(w_in[l].astype(BF16)) for l in range(DEPTH)]
    sps = [small_params(l) for l in range(DEPTH)]
    saved, weights = [], []
    h = xs
    (g_in,) = _exchange("gather_w_in", [shard["w_in"][0]])
    hn = _norm_fwd("norm_fwd", xs, sps[0]["g_mix_pre"])
    for l in range(DEPTH):
        travelling = [shard[n][l] for n in REST] + ([shard["w_in"][l + 1]] if l + 1 < DEPTH else [])
        g_next = sps[l + 1]["g_mix_pre"] if l + 1 < DEPTH else None
        h, hn, sv, W, g_in = _layer_fwd(h, hn, mems, sps[l], g_in, travelling, g_next)
        saved.append(sv)
        weights.append(W)
    dh, sq = _loss_fwd_bwd(h, target)
    loss = lax.psum(0.5 * sq[0, 0] / D_MODEL, ("x", "y", "c"))

    parts = [dict() for _ in range(DEPTH)]
    small_grads = [None] * DEPTH
    carried = []
    lower = _norm_bwd("norm_bwd_b", dh, saved[-1]["y"], sps[-1]["g_ffn_post"], None, BF16)
    for l in reversed(range(DEPTH)):
        dy, dg_ffn_post = lower
        below = (saved[l - 1]["y"], sps[l - 1]["g_ffn_post"]) if l > 0 else None
        dh, lower, got, got_carried, dw_in, gs = _layer_bwd(dh, dy, mems, saved[l], sps[l], weights[l], carried, below)
        gs["g_ffn_post"] = dg_ffn_post
        parts[l].update(got)
        if got_carried:
            parts[l + 1]["w_in"] = got_carried[0]
        carried = [dw_in]
        small_grads[l] = gs
    (parts[0]["w_in"],) = _exchange("scatter_dw_in", carried)
    grad_x = dh.reshape(1, S, D_MODEL)

    grads, deltas, new_m, new_v = {}, {}, {}, {}
    rows = dict(w_in=128, w_out=128, wq_x=128, wkv_x=256, wo_x=128, w_up=256, w_down=128)
    for l in range(DEPTH):
        parts[l]["w_in"] = jnp.swapaxes(parts[l]["w_in"][:, :W_IN_SHARD, :], 1, 2)
    for n in BIG:
        grads[n], deltas[n], new_m[n], new_v[n] = _adamw_big(
            "adamw_" + n, w[n], mom[n], var[n], [parts[l][n] for l in range(DEPTH)], rows[n])

    sg = {n: jnp.stack([small_grads[l][n].reshape(w[n].shape[1:]) for l in range(DEPTH)]) for n in SMALL}
    (sg_parts,) = _exchange("gather_small_grads", [_pack_small(sg)])
    outs = _adamw_small(_pack_small(w), _pack_small(mom), _pack_small(var), sg_parts)
    for d, packed in zip((grads, deltas, new_m, new_v), outs):
        d.update(_unpack_small(packed, w))

    return (loss, grad_x, *[grads[n] for n in W_NAMES], *[deltas[n] for n in W_NAMES],
            *[new_m[n] for n in W_NAMES], *[new_v[n] for n in W_NAMES])
```

```python
import math

import jax
import jax.numpy as jnp
from jax import lax
from jax.experimental import pallas as pl
from jax.experimental.pallas import tpu as pltpu

F32 = jnp.float32
BF16 = jnp.bfloat16

D_MODEL = 1024
DEPTH = 4
FOX_WIDTH = 512
FOX_HEADS = 8
FOX_HEAD_DIM = 64
POOL_WIDTH = 512
POOL_WINDOWS = (2, 4, 8, 16)
POOL_GROUP_DIM = 128
POOL_HALO = 16
MEM_LEN = 256
X_HEADS = 4
X_HEAD_DIM = 256
D_FF = 4096
EPS = 1e-6
IN_COLS = 2056
QKV_COLS = 3 * FOX_WIDTH
UF_COLS = 640
INP_COLS = QKV_COLS + UF_COLS
N_DEV = 8
LANES = 128

ADAM_LR = 0.001
ADAM_B1 = 0.9
ADAM_B2 = 0.999
ADAM_EPS = 1e-08
ADAM_WD = 0.01
ADAM_STEP = 10

VMEM_LIMIT = 56 * 1024 * 1024

W_NAMES = ['g_mix_pre', 'w_in', 'b_forget', 'pool_w', 'pool_scale', 'w_out', 'g_mix_post', 'g_x_pre', 'g_mem',
           'wq_x', 'wkv_x', 'wo_x', 'g_x_post', 'g_ffn_pre', 'w_up', 'w_down', 'g_ffn_post']
BIG = ['w_in', 'w_out', 'wq_x', 'wkv_x', 'wo_x', 'w_up', 'w_down']
SMALL = [n for n in W_NAMES if n not in BIG]

NN = (((1,), (0,)), ((), ()))
NT = (((1,), (1,)), ((), ()))
TN = (((0,), (0,)), ((), ()))


def _params(*sem):
    return pltpu.CompilerParams(dimension_semantics=sem, vmem_limit_bytes=VMEM_LIMIT)


def _row_tile(s):
    return min(s, 512)


def _product(a_ref, w_ref, kind, c0, pw):
    cols = slice(c0, c0 + pw)
    if kind == "nn":
        return lax.dot_general(a_ref[...], w_ref[:, cols], NN, preferred_element_type=F32)
    if kind == "nt":
        return lax.dot_general(a_ref[...], w_ref[cols, :], NT, preferred_element_type=F32)
    n = w_ref.shape[2]
    if kind == "nn3":
        assert pw == n and c0 % n == 0
        return lax.dot_general(a_ref[...], w_ref[c0 // n], NN, preferred_element_type=F32)
    r = None
    for j in range(w_ref.shape[0]):
        part = lax.dot_general(a_ref[:, j * n:(j + 1) * n], w_ref[j, cols, :], NT, preferred_element_type=F32)
        r = part if r is None else r + part
    return r


def _resident(w):
    return pl.BlockSpec(w.shape, lambda i, nd=w.ndim: (0,) * nd)


def _mm_rows(name, terms, outs, extra=None, piece=1024):
    M = terms[0][0].shape[0]
    tm = _row_tile(M)
    nterm = len(terms)
    n_extra = 0 if extra is None else 1
    groups = {}
    for idx, (_, c0, width, fn) in enumerate(outs):
        groups.setdefault((c0, width), []).append((idx, fn))

    def body(*refs):
        a_refs = refs[0:2 * nterm:2]
        w_refs = refs[1:2 * nterm:2]
        extra_refs = refs[2 * nterm:2 * nterm + n_extra]
        out_refs = refs[2 * nterm + n_extra:]
        for (g0, gw), members in groups.items():
            for c0 in range(g0, g0 + gw, piece):
                pw = min(piece, g0 + gw - c0)
                r = None
                for a_ref, w_ref, (_, w, kind) in zip(a_refs, w_refs, terms):
                    part = _product(a_ref, w_ref, kind, c0, pw)
                    r = part if r is None else r + part
                dst = slice(c0 - g0, c0 - g0 + pw)
                for idx, fn in members:
                    if fn == "relu2":
                        rp = jnp.maximum(r, 0.0)
                        val = rp * rp
                    elif fn == "drelu2":
                        val = r * (2.0 * jnp.maximum(extra_refs[0][:, dst].astype(F32), 0.0))
                    else:
                        val = r
                    out_refs[idx][:, dst] = val.astype(out_refs[idx].dtype)

    in_specs, ins = [], []
    for a, w, _ in terms:
        in_specs.append(pl.BlockSpec((tm, a.shape[1]), lambda i: (i, 0)))
        in_specs.append(pl.BlockSpec(w.shape, lambda i, nd=w.ndim: (0,) * nd))
        ins += [a, w]
    if extra is not None:
        in_specs.append(pl.BlockSpec((tm, extra.shape[1]), lambda i: (i, 0)))
        ins.append(extra)
    res = pl.pallas_call(
        body, name=name, grid=(M // tm,), in_specs=in_specs,
        out_specs=[pl.BlockSpec((tm, width), lambda i: (i, 0)) for _, _, width, _ in outs],
        out_shape=[jax.ShapeDtypeStruct((M, width), dt) for dt, _, width, _ in outs],
        compiler_params=_params("parallel"))(*ins)
    return res


def _mm1(name, a, w, kind, n_cols, dtype, piece=1024):
    return _mm_rows(name, [(a, w, kind)], [(dtype, 0, n_cols, "id")], piece=piece)[0]


def _mm_tn(name, a, b, out_dtype, shard_cols=None, piece=512):
    K, M = a.shape
    N = b.shape[1]
    tk = _row_tile(K)
    nk = K // tk
    piece = shard_cols or min(piece, N)

    def body(a_ref, b_ref, o_ref, acc):
        k = pl.program_id(0)

        @pl.when(k == 0)
        def _():
            acc[...] = jnp.zeros_like(acc)

        a_t = jnp.transpose(a_ref[...])
        for c0 in range(0, N, piece):
            cols = slice(c0, min(c0 + piece, N))
            acc[:, cols] += lax.dot_general(a_t, b_ref[:, cols], NN, preferred_element_type=F32)

        @pl.when(k == nk - 1)
        def _():
            for c0 in range(0, N, piece):
                cols = slice(c0, min(c0 + piece, N))
                if shard_cols:
                    o_ref[c0 // piece] = acc[:, cols].astype(o_ref.dtype)
                else:
                    o_ref[:, cols] = acc[:, cols].astype(o_ref.dtype)

    out_dims = (N // shard_cols, M, shard_cols) if shard_cols else (M, N)
    return pl.pallas_call(
        body, name=name, grid=(nk,),
        in_specs=[pl.BlockSpec((tk, M), lambda k: (k, 0)), pl.BlockSpec((tk, N), lambda k: (k, 0))],
        out_specs=pl.BlockSpec(out_dims, lambda k, nd=len(out_dims): (0,) * nd),
        out_shape=jax.ShapeDtypeStruct(out_dims, out_dtype),
        scratch_shapes=[pltpu.VMEM((M, N), F32)],
        compiler_params=_params("arbitrary"))(a, b)


def _norm_fwd(name, x, g):
    S, Dm = x.shape
    ts = _row_tile(S)

    def body(x_ref, g_ref, h_ref):
        xv = x_ref[...]
        r = lax.rsqrt(jnp.mean(xv * xv, axis=-1, keepdims=True) + EPS)
        h_ref[...] = ((xv * r) * g_ref[...]).astype(BF16)

    return pl.pallas_call(
        body, name=name, grid=(S // ts,),
        in_specs=[pl.BlockSpec((ts, Dm), lambda i: (i, 0)), pl.BlockSpec((1, Dm), lambda i: (0, 0))],
        out_specs=pl.BlockSpec((ts, Dm), lambda i: (i, 0)),
        out_shape=jax.ShapeDtypeStruct((S, Dm), BF16), compiler_params=_params("parallel"))(x, g)


def _mm_resid_norm(name, a, w, x, g, g_next):
    S, Dm = x.shape
    ts = _row_tile(S)
    has_next = g_next is not None

    def body(a_ref, w_ref, x_ref, g_ref, *rest):
        fv = _product(a_ref, w_ref, "nn", 0, Dm)
        r = lax.rsqrt(jnp.mean(fv * fv, axis=-1, keepdims=True) + EPS)
        xn = x_ref[...] + (fv * r) * g_ref[...]
        if has_next:
            gn_ref, f_ref, o_ref, h_ref = rest
            rn = lax.rsqrt(jnp.mean(xn * xn, axis=-1, keepdims=True) + EPS)
            h_ref[...] = ((xn * rn) * gn_ref[...]).astype(BF16)
        else:
            f_ref, o_ref = rest
        f_ref[...] = fv
        o_ref[...] = xn

    row = pl.BlockSpec((ts, Dm), lambda i: (i, 0))
    vec = pl.BlockSpec((1, Dm), lambda i: (0, 0))
    ins = [a, w, x, g] + ([g_next] if has_next else [])
    f32_rows = jax.ShapeDtypeStruct((S, Dm), F32)
    res = pl.pallas_call(
        body, name=name, grid=(S // ts,),
        in_specs=[pl.BlockSpec((ts, a.shape[1]), lambda i: (i, 0)), _resident(w), row, vec] + ([vec] if has_next else []),
        out_specs=[row, row] + ([row] if has_next else []),
        out_shape=[f32_rows, f32_rows] + ([jax.ShapeDtypeStruct((S, Dm), BF16)] if has_next else []),
        compiler_params=_params("parallel"))(*ins)
    return (res[0], res[1], res[2]) if has_next else (res[0], res[1], None)


def _rms_bwd(dov, yv, g):
    r = lax.rsqrt(jnp.mean(yv * yv, axis=-1, keepdims=True) + EPS)
    z = dov * g
    yr = yv * r
    return r * (z - yr * jnp.mean(yr * z, axis=-1, keepdims=True)), jnp.sum(dov * yr, axis=0, keepdims=True)


def _norm_bwd(name, dout, y, g, resid, out_dtype, below=None):
    S, Dm = y.shape
    ts = _row_tile(S)
    has_resid = resid is not None
    chained = below is not None
    produced = isinstance(dout, tuple)
    kind = dout[2] if produced else None

    def body(*refs):
        refs = list(refs)
        if produced:
            dov = _product(refs[0], refs[1], kind, 0, Dm)
            refs = refs[1:]
        else:
            dov = refs[0][...]
        y_ref, g_ref = refs[1:3]
        pos = 3
        r_ref = refs[pos] if has_resid else None
        pos += has_resid
        if chained:
            f_ref, gf_ref = refs[pos:pos + 2]
            pos += 2
        dy_ref, dg_ref = refs[pos:pos + 2]
        i = pl.program_id(0)
        dy, dg = _rms_bwd(dov, y_ref[...], g_ref[...])
        if has_resid:
            dy = dy + r_ref[...]
        dy_ref[...] = dy.astype(out_dtype)

        @pl.when(i == 0)
        def _():
            for ref in refs[pos + 1::2]:
                ref[...] = jnp.zeros_like(ref)

        dg_ref[...] += dg
        if chained:
            df_ref, dgf_ref = refs[pos + 2:pos + 4]
            df, dgf = _rms_bwd(dy, f_ref[...], gf_ref[...])
            df_ref[...] = df.astype(BF16)
            dgf_ref[...] += dgf

    row = pl.BlockSpec((ts, Dm), lambda i: (i, 0))
    vec = pl.BlockSpec((1, Dm), lambda i: (0, 0))
    if produced:
        ins = [dout[0], dout[1]]
        specs = [pl.BlockSpec((ts, dout[0].shape[1]), lambda i: (i, 0)), _resident(dout[1])]
    else:
        ins = [dout]
        specs = [row]
    ins += [y, g] + ([resid] if has_resid else []) + (list(below) if chained else [])
    specs += [row, vec] + ([row] if has_resid else []) + ([row, vec] if chained else [])
    vec_shape = jax.ShapeDtypeStruct((1, Dm), F32)
    return pl.pallas_call(
        body, name=name, grid=(S // ts,), in_specs=specs, out_specs=[row, vec] + ([row, vec] if chained else []),
        out_shape=[jax.ShapeDtypeStruct((S, Dm), out_dtype), vec_shape]
        + ([jax.ShapeDtypeStruct((S, Dm), BF16), vec_shape] if chained else []),
        compiler_params=_params("arbitrary"))(*ins)


def _loss_fwd_bwd(y, t):
    S, Dm = y.shape
    ts = _row_tile(S)

    def body(y_ref, t_ref, dy_ref, acc_ref):
        i = pl.program_id(0)
        e = y_ref[...] - t_ref[...]
        dy_ref[...] = e * (1.0 / Dm)

        @pl.when(i == 0)
        def _():
            acc_ref[...] = jnp.zeros_like(acc_ref)

        s = jnp.sum(jnp.sum(e * e, axis=1, keepdims=True), axis=0, keepdims=True)
        acc_ref[...] += s

    row = pl.BlockSpec((ts, Dm), lambda i: (i, 0))
    return pl.pallas_call(
        body, name="loss", grid=(S // ts,), in_specs=[row, row],
        out_specs=[row, pl.BlockSpec((8, LANES), lambda i: (0, 0))],
        out_shape=[jax.ShapeDtypeStruct((S, Dm), F32), jax.ShapeDtypeStruct((8, LANES), F32)],
        compiler_params=_params("arbitrary"))(y, t)


def _log_sigmoid(x):
    return jnp.minimum(x, 0.0) - jnp.log(1.0 + jnp.exp(-jnp.abs(x)))


def _gate_fwd(uf, bpad):
    S = uf.shape[0]
    T = _row_tile(S)

    def body(f_ref, b_ref, c_ref, carry):
        i = pl.program_id(0)

        @pl.when(i == 0)
        def _():
            carry[...] = jnp.zeros_like(carry)

        lf = _log_sigmoid(f_ref[...] + b_ref[...])
        r = lax.broadcasted_iota(jnp.int32, (T, T), 0)
        cidx = lax.broadcasted_iota(jnp.int32, (T, T), 1)
        tri = (cidx <= r).astype(F32)
        c = lax.dot_general(tri, lf, NN, precision=lax.Precision.HIGHEST, preferred_element_type=F32)
        c_ref[...] = c + carry[0:1, :]
        carry[...] = carry[...] + jnp.sum(lf, axis=0, keepdims=True)

    return pl.pallas_call(
        body, name="gate_fwd", grid=(S // T,),
        in_specs=[pl.BlockSpec((T, LANES), lambda i: (i, 4)), pl.BlockSpec((1, LANES), lambda i: (0, 0))],
        out_specs=pl.BlockSpec((T, LANES), lambda i: (i, 0)),
        out_shape=jax.ShapeDtypeStruct((S, LANES), F32),
        scratch_shapes=[pltpu.VMEM((8, LANES), F32)], compiler_params=_params("arbitrary"))(uf, bpad)


def _gate_bwd(dc, uf, bpad):
    S = uf.shape[0]
    T = _row_tile(S)
    nb = S // T

    def body(dc_ref, f_ref, b_ref, df_ref, db_ref, carry):
        i = pl.program_id(0)

        @pl.when(i == 0)
        def _():
            carry[...] = jnp.zeros_like(carry)
            db_ref[...] = jnp.zeros_like(db_ref)

        dcv = dc_ref[...]
        r = lax.broadcasted_iota(jnp.int32, (T, T), 0)
        cidx = lax.broadcasted_iota(jnp.int32, (T, T), 1)
        tri = (cidx >= r).astype(F32)
        dlf = lax.dot_general(tri, dcv, NN, precision=lax.Precision.HIGHEST, preferred_element_type=F32)
        dlf = dlf + carry[0:1, :]
        carry[...] = carry[...] + jnp.sum(dcv, axis=0, keepdims=True)
        fg = f_ref[...] + b_ref[...]
        dfg = dlf / (1.0 + jnp.exp(fg))
        df_ref[...] = dfg.astype(BF16)
        db_ref[...] += jnp.sum(dfg, axis=0, keepdims=True)

    return pl.pallas_call(
        body, name="gate_bwd", grid=(nb,),
        in_specs=[pl.BlockSpec((T, LANES), lambda i: (nb - 1 - i, 0)),
                  pl.BlockSpec((T, LANES), lambda i: (nb - 1 - i, 4)),
                  pl.BlockSpec((1, LANES), lambda i: (0, 0))],
        out_specs=[pl.BlockSpec((T, LANES), lambda i: (nb - 1 - i, 0)), pl.BlockSpec((1, LANES), lambda i: (0, 0))],
        out_shape=[jax.ShapeDtypeStruct((S, LANES), BF16), jax.ShapeDtypeStruct((1, LANES), F32)],
        scratch_shapes=[pltpu.VMEM((8, LANES), F32)], compiler_params=_params("arbitrary"))(dc, uf, bpad)


FOX_CHUNK = 32
FOX_CHUNK_BWD = 64
HEAD_PAIRS = FOX_HEADS // 2
PAIR = 2


def _masked(s, row0, col0, diagonal):
    if diagonal:
        row = row0 + lax.broadcasted_iota(jnp.int32, s.shape, 0)
        col = col0 + lax.broadcasted_iota(jnp.int32, s.shape, 1)
        s = jnp.where(col <= row, s, -jnp.inf)
    return s


def _causal_pairs(n, query_major):
    if query_major:
        pairs = [(q, k) for q in range(n) for k in range(q + 1)]
    else:
        pairs = [(q, k) for k in range(n) for q in range(k, n)]
    return (jnp.asarray([p[0] for p in pairs], jnp.int32), jnp.asarray([p[1] for p in pairs], jnp.int32))


def _lane_block(b):
    return slice(b * LANES, (b + 1) * LANES)


def _fold(op, xs):
    acc = xs[0]
    for x in xs[1:]:
        acc = op(acc, x)
    return acc


def _head_lanes(hh):
    lane = lax.broadcasted_iota(jnp.int32, (1, LANES), 1)
    return (lane < FOX_HEAD_DIM) if hh == 0 else (lane >= FOX_HEAD_DIM)


def _pick(first_head, a, b):
    return jnp.where(first_head, a, b)


def _fox_fwd(qkv, cT, comm):
    S = qkv.shape[0]
    t = _row_tile(S)
    n = S // t
    nc = len(comm)
    scale = 1.0 / math.sqrt(FOX_HEAD_DIM)
    chunk = min(FOX_CHUNK, t)
    per_head = 4
    q_tab, k_tab = _causal_pairs(n, True)
    steps = q_tab.shape[0]

    def body(qt_ref, kt_ref, q_ref, k_ref, v_ref, c_ref, *rest):
        comm_in = rest[:nc]
        o_ref, ob_ref, lse_ref = rest[nc:nc + 3]
        comm_out = rest[nc + 3:2 * nc + 3]
        scr = rest[2 * nc + 3:2 * nc + 3 + PAIR * per_head]
        sems = rest[2 * nc + 3 + PAIR * per_head:]
        hp = pl.program_id(0)
        step_id = pl.program_id(1)
        qi = qt_ref[step_id]
        ki = kt_ref[step_id]

        if nc:
            @pl.when((hp == 0) & (step_id == 0))
            def _():
                _Gather(comm_in, comm_out, *sems).start()

            @pl.when((hp == HEAD_PAIRS - 1) & (step_id == 0))
            def _():
                _Gather(comm_in, comm_out, *sems).pass_on()

        @pl.when(ki == 0)
        def _():
            for hh in range(PAIR):
                m_s, l_s, a_s, acc_s = scr[hh * per_head:hh * per_head + 4]
                m_s[...] = jnp.full_like(m_s, -jnp.inf)
                l_s[...] = jnp.zeros_like(l_s)
                acc_s[...] = jnp.zeros_like(acc_s)

        def step(diagonal):
            q2 = q_ref[...] * scale
            k2 = k_ref[...]
            v2 = v_ref[...]
            scores = []
            for hh in range(PAIR):
                qm = jnp.where(_head_lanes(hh), q2, jnp.zeros_like(q2))
                scores.append(lax.dot_general(qm, k2, NT, preferred_element_type=F32))
            for hh in range(PAIR):
                m_s, l_s, a_s, acc_s = scr[hh * per_head:(hh + 1) * per_head]
                s_s = scores[hh]
                hi_rows, lo_rows = [], []
                for r in range(t // chunk):
                    rows = slice(r * chunk, (r + 1) * chunk)
                    blocks = [_masked(s_s[rows, _lane_block(b)] - c_ref[hh, :, _lane_block(b)], r * chunk,
                                      b * LANES, diagonal) for b in range(t // LANES)]
                    m_prev = m_s[rows, :]
                    m_new = jnp.maximum(m_prev, jnp.max(_fold(jnp.maximum, blocks), axis=1, keepdims=True))
                    alpha = jnp.exp(m_prev - m_new)
                    ps = [jnp.exp(blk - m_new) for blk in blocks]
                    l_s[rows, :] = alpha * l_s[rows, :] + jnp.sum(_fold(jnp.add, ps), axis=1, keepdims=True)
                    m_s[rows, :] = m_new
                    a_s[rows, :] = alpha
                    his = [p.astype(BF16) for p in ps]
                    hi_rows.append(jnp.concatenate(his, axis=1))
                    lo_rows.append(jnp.concatenate([(p - h.astype(F32)).astype(BF16) for p, h in zip(ps, his)],
                                                   axis=1))
                pv = (lax.dot_general(jnp.concatenate(hi_rows, axis=0), v2, NN, preferred_element_type=F32)
                      + lax.dot_general(jnp.concatenate(lo_rows, axis=0), v2, NN, preferred_element_type=F32))
                acc_s[...] = a_s[...] * acc_s[...] + pv

        @pl.when(ki < qi)
        def _():
            step(False)

        @pl.when(ki == qi)
        def _():
            step(True)
            heads = []
            for hh in range(PAIR):
                m_s, l_s, a_s, acc_s = scr[hh * per_head:hh * per_head + 4]
                heads.append(acc_s[...] / l_s[...])
                lse_ref[hh] = m_s[...] + jnp.log(l_s[...])
            o2 = _pick(_head_lanes(0), heads[0], heads[1])
            o_ref[...] = o2
            ob_ref[...] = o2.astype(BF16)

        if nc:
            @pl.when((hp == HEAD_PAIRS - 1) & (step_id == steps - 1))
            def _():
                _Gather(comm_in, comm_out, *sems).finish()

    def q_cols(first_block):
        return pl.BlockSpec((t, LANES), lambda h, s, qt, kt: (qt[s], first_block + h))

    def k_cols(first_block):
        return pl.BlockSpec((t, LANES), lambda h, s, qt, kt: (kt[s], first_block + h))

    any_spec = pl.BlockSpec(memory_space=pl.ANY)
    head_scratch = [pltpu.VMEM((t, LANES), F32)] * per_head
    grid_spec = pltpu.PrefetchScalarGridSpec(
        num_scalar_prefetch=2, grid=(HEAD_PAIRS, steps),
        in_specs=[q_cols(0), k_cols(HEAD_PAIRS), k_cols(2 * HEAD_PAIRS),
                  pl.BlockSpec((PAIR, 1, t), lambda h, s, qt, kt: (h, 0, kt[s]))] + [any_spec] * nc,
        out_specs=[q_cols(0), q_cols(0),
                   pl.BlockSpec((PAIR, t, LANES), lambda h, s, qt, kt: (h, qt[s], 0))] + [any_spec] * nc,
        scratch_shapes=head_scratch * PAIR + _comm_scratch(nc))
    return pl.pallas_call(
        body, name="fox_fwd", grid_spec=grid_spec,
        out_shape=[jax.ShapeDtypeStruct((S, FOX_WIDTH), F32), jax.ShapeDtypeStruct((S, FOX_WIDTH), BF16),
                   jax.ShapeDtypeStruct((FOX_HEADS, S, LANES), F32)] + _comm_shapes(comm),
        compiler_params=_params("arbitrary", "arbitrary"))(q_tab, k_tab, qkv, qkv, qkv, cT, *comm)


def _fox_bwd(qkv, cT, o, lse, do, comm):
    S = qkv.shape[0]
    t = _row_tile(S)
    n = S // t
    nc = len(comm)
    scale = 1.0 / math.sqrt(FOX_HEAD_DIM)
    chunk = min(FOX_CHUNK_BWD, t)
    per_head = 2
    q_tab, k_tab = _causal_pairs(n, False)
    steps = q_tab.shape[0]

    def body(qt_ref, kt_ref, q_ref, k_ref, v_ref, c_ref, o_ref, do_ref, lse_ref, *rest):
        comm_in = rest[:nc]
        dq_ref, dk_ref, dv_ref, dc_ref = rest[nc:nc + 4]
        comm_out = rest[nc + 4:2 * nc + 4]
        dq_s, dk_s, dv_s = rest[2 * nc + 4:2 * nc + 7]
        scr = rest[2 * nc + 7:2 * nc + 7 + PAIR * per_head]
        sems = rest[2 * nc + 7 + PAIR * per_head:]
        hp = pl.program_id(0)
        step_id = pl.program_id(1)
        qi = qt_ref[step_id]
        ki = kt_ref[step_id]

        if nc:
            @pl.when((hp == 0) & (step_id == 0))
            def _():
                for cp in _comm_copies(comm_in, comm_out, *sems):
                    cp.start()

        @pl.when(step_id == 0)
        def _():
            dq_s[...] = jnp.zeros_like(dq_s)

        @pl.when(qi == ki)
        def _():
            dk_s[...] = jnp.zeros_like(dk_s)
            dv_s[...] = jnp.zeros_like(dv_s)
            for hh in range(PAIR):
                dc_s = scr[hh * per_head]
                dc_s[...] = jnp.zeros_like(dc_s)

        def step(diagonal):
            q2 = q_ref[...]
            k2 = k_ref[...]
            v2 = v_ref[...]
            do2 = do_ref[...]
            prod = do2.astype(F32) * o_ref[...]
            grads = []
            for hh in range(PAIR):
                dc_s, delta_s = scr[hh * per_head:(hh + 1) * per_head]
                mine = _head_lanes(hh)
                s_s = lax.dot_general(jnp.where(mine, q2 * scale, jnp.zeros_like(q2)), k2, NT,
                                      preferred_element_type=F32)
                dp_s = lax.dot_general(jnp.where(mine, do2, jnp.zeros_like(do2)), v2, NT, preferred_element_type=F32)
                delta_s[...] = jnp.broadcast_to(jnp.sum(jnp.where(mine, prod, 0.0), axis=1, keepdims=True),
                                                (t, LANES))
                dc8 = [jnp.zeros((8, LANES), F32) for _ in range(t // LANES)]
                p_rows, ds_rows = [], []
                for r in range(t // chunk):
                    rows = slice(r * chunk, (r + 1) * chunk)
                    lse = lse_ref[hh, rows, :]
                    delta = delta_s[rows, :]
                    p_blocks, ds_blocks = [], []
                    for b in range(t // LANES):
                        s = _masked(s_s[rows, _lane_block(b)] - c_ref[hh, :, _lane_block(b)], r * chunk, b * LANES,
                                    diagonal)
                        p = jnp.exp(s - lse)
                        ds = p * (dp_s[rows, _lane_block(b)] - delta)
                        p_blocks.append(p.astype(BF16))
                        ds_blocks.append(ds.astype(BF16))
                        dc8[b] = dc8[b] + jnp.sum(ds.reshape(chunk // 8, 8, LANES), axis=0)
                    p_rows.append(jnp.concatenate(p_blocks, axis=1))
                    ds_rows.append(jnp.concatenate(ds_blocks, axis=1))
                for b in range(t // LANES):
                    dc_s[:, _lane_block(b)] += jnp.sum(dc8[b], axis=0, keepdims=True)
                dsb = jnp.concatenate(ds_rows, axis=0)
                grads.append((lax.dot_general(jnp.concatenate(p_rows, axis=0), do2, TN, preferred_element_type=F32),
                              lax.dot_general(dsb, k2, NN, preferred_element_type=F32),
                              lax.dot_general(dsb, q2, TN, preferred_element_type=F32)))
            first = _head_lanes(0)
            dv_s[...] += _pick(first, grads[0][0], grads[1][0])
            q_rows = pl.ds(pl.multiple_of(qi * t, t), t)
            dq_s[q_rows, :] += _pick(first, grads[0][1], grads[1][1]) * scale
            dk_s[...] += _pick(first, grads[0][2], grads[1][2]) * scale

        @pl.when(qi > ki)
        def _():
            step(False)

        @pl.when(qi == ki)
        def _():
            step(True)

        @pl.when(qi == n - 1)
        def _():
            dk_ref[...] = dk_s[...].astype(BF16)
            dv_ref[...] = dv_s[...].astype(BF16)
            for hh in range(PAIR):
                dc_ref[hh] = -scr[hh * per_head][...]

        @pl.when(step_id == steps - 1)
        def _():
            dq_ref[...] = dq_s[...].astype(BF16)

        if nc:
            @pl.when((hp == HEAD_PAIRS - 1) & (step_id == steps - 1))
            def _():
                for cp in _comm_copies(comm_in, comm_out, *sems):
                    cp.wait()

    def q_side(first_block):
        return pl.BlockSpec((t, LANES), lambda h, s, qt, kt: (qt[s], first_block + h))

    def k_side(first_block):
        return pl.BlockSpec((t, LANES), lambda h, s, qt, kt: (kt[s], first_block + h))

    any_spec = pl.BlockSpec(memory_space=pl.ANY)
    head_scratch = [pltpu.VMEM((1, t), F32), pltpu.VMEM((t, LANES), F32)]
    grad_shape = jax.ShapeDtypeStruct((S, FOX_WIDTH), BF16)
    grid_spec = pltpu.PrefetchScalarGridSpec(
        num_scalar_prefetch=2, grid=(HEAD_PAIRS, steps),
        in_specs=[q_side(0), k_side(HEAD_PAIRS), k_side(2 * HEAD_PAIRS),
                  pl.BlockSpec((PAIR, 1, t), lambda h, s, qt, kt: (h, 0, kt[s])), q_side(0), q_side(0),
                  pl.BlockSpec((PAIR, t, LANES), lambda h, s, qt, kt: (h, qt[s], 0))] + [any_spec] * nc,
        out_specs=[pl.BlockSpec((S, LANES), lambda h, s, qt, kt: (0, h)), k_side(0), k_side(0),
                   pl.BlockSpec((PAIR, 1, t), lambda h, s, qt, kt: (h, 0, kt[s]))] + [any_spec] * nc,
        scratch_shapes=[pltpu.VMEM((S, LANES), F32), pltpu.VMEM((t, LANES), F32), pltpu.VMEM((t, LANES), F32)]
        + head_scratch * PAIR + _comm_scratch(nc))
    return pl.pallas_call(
        body, name="fox_bwd", grid_spec=grid_spec,
        out_shape=[grad_shape, grad_shape, grad_shape, jax.ShapeDtypeStruct((FOX_HEADS, 1, S), F32)]
        + _comm_shapes(comm),
        compiler_params=_params("arbitrary", "arbitrary"))(q_tab, k_tab, qkv, qkv, qkv, cT, o, do, lse, *comm)


def _lanes(g):
    return slice(g * POOL_GROUP_DIM, (g + 1) * POOL_GROUP_DIM)


def _window_sum(e, win, back):
    rows = e.shape[0]
    s = e
    sh = 1
    while sh < win:
        s = s + pltpu.roll(s, sh if back else rows - sh, 0)
        sh *= 2
    return s


def _pooled(u_ref, up_ref, i, g, win, T):
    cur = u_ref[:, _lanes(g)]
    tail = jnp.where(i > 0, up_ref[T - POOL_HALO:T, _lanes(g)], 0.0)
    e = jnp.concatenate([tail, cur], axis=0)
    s = _window_sum(e, win, True)
    t_idx = i * T - POOL_HALO + lax.broadcasted_iota(jnp.int32, (T + POOL_HALO, POOL_GROUP_DIM), 0)
    cnt = jnp.clip(t_idx + 1, 1, win).astype(F32)
    return (s / cnt - e)[POOL_HALO:, :]


def _pool_fwd(uf, pw, ps):
    S = uf.shape[0]
    T = _row_tile(S)

    def body(u_ref, up_ref, w_ref, sc_ref, o_ref):
        i = pl.program_id(0)
        for g, win in enumerate(POOL_WINDOWS):
            pb = _pooled(u_ref, up_ref, i, g, win, T).astype(BF16)
            yv = lax.dot_general(pb, w_ref[g], NN, preferred_element_type=F32)
            o_ref[:, _lanes(g)] = (yv * sc_ref[:, _lanes(g)]).astype(BF16)

    return pl.pallas_call(
        body, name="pool_fwd", grid=(S // T,),
        in_specs=[pl.BlockSpec((T, POOL_WIDTH), lambda i: (i, 0)),
                  pl.BlockSpec((T, POOL_WIDTH), lambda i: (jnp.maximum(i - 1, 0), 0)),
                  pl.BlockSpec((4, POOL_GROUP_DIM, POOL_GROUP_DIM), lambda i: (0, 0, 0)),
                  pl.BlockSpec((1, POOL_WIDTH), lambda i: (0, 0))],
        out_specs=pl.BlockSpec((T, POOL_WIDTH), lambda i: (i, 0)),
        out_shape=jax.ShapeDtypeStruct((S, POOL_WIDTH), BF16), compiler_params=_params("parallel"))(uf, uf, pw, ps)


def _pool_bwd(uf, dpool, pw, ps):
    S = uf.shape[0]
    T = _row_tile(S)
    nb = S // T

    def body(u_ref, up_ref, d_ref, dn_ref, w_ref, sc_ref, du_ref, dw_ref, dsc_ref):
        i = pl.program_id(0)

        @pl.when(i == 0)
        def _():
            dw_ref[...] = jnp.zeros_like(dw_ref)
            dsc_ref[...] = jnp.zeros_like(dsc_ref)

        t_idx = i * T + lax.broadcasted_iota(jnp.int32, (T + POOL_HALO, POOL_GROUP_DIM), 0)
        for g, win in enumerate(POOL_WINDOWS):
            pb = _pooled(u_ref, up_ref, i, g, win, T).astype(BF16)
            w = w_ref[g]
            sc = sc_ref[:, _lanes(g)]
            yv = lax.dot_general(pb, w, NN, preferred_element_type=F32)
            dov = d_ref[:, _lanes(g)]
            dsc_ref[:, _lanes(g)] += jnp.sum(dov * yv, axis=0, keepdims=True)
            head = jnp.where(i < nb - 1, dn_ref[0:POOL_HALO, _lanes(g)], 0.0)
            dyb = (jnp.concatenate([dov, head], axis=0) * sc).astype(BF16)
            dw_ref[g] += lax.dot_general(pb, dyb[:T], TN, preferred_element_type=F32)
            dpl = lax.dot_general(dyb, w, NT, preferred_element_type=F32)
            cnt = jnp.minimum(t_idx + 1, win).astype(F32)
            a = _window_sum(dpl / cnt, win, False)
            du_ref[:, _lanes(g)] = (a - dpl)[:T].astype(BF16)

    return pl.pallas_call(
        body, name="pool_bwd", grid=(nb,),
        in_specs=[pl.BlockSpec((T, POOL_WIDTH), lambda i: (i, 0)),
                  pl.BlockSpec((T, POOL_WIDTH), lambda i: (jnp.maximum(i - 1, 0), 0)),
                  pl.BlockSpec((T, POOL_WIDTH), lambda i: (i, 0)),
                  pl.BlockSpec((T, POOL_WIDTH), lambda i: (jnp.minimum(i + 1, nb - 1), 0)),
                  pl.BlockSpec((4, POOL_GROUP_DIM, POOL_GROUP_DIM), lambda i: (0, 0, 0)),
                  pl.BlockSpec((1, POOL_WIDTH), lambda i: (0, 0))],
        out_specs=[pl.BlockSpec((T, POOL_WIDTH), lambda i: (i, 0)),
                   pl.BlockSpec((4, POOL_GROUP_DIM, POOL_GROUP_DIM), lambda i: (0, 0, 0)),
                   pl.BlockSpec((1, POOL_WIDTH), lambda i: (0, 0))],
        out_shape=[jax.ShapeDtypeStruct((S, POOL_WIDTH), BF16),
                   jax.ShapeDtypeStruct((4, POOL_GROUP_DIM, POOL_GROUP_DIM), F32),
                   jax.ShapeDtypeStruct((1, POOL_WIDTH), F32)],
        compiler_params=_params("arbitrary"))(uf, uf, dpool, dpool, pw, ps)


def _xhead(h):
    return slice(h * X_HEAD_DIM, (h + 1) * X_HEAD_DIM)


def _xvhead(h):
    return slice(D_MODEL + h * X_HEAD_DIM, D_MODEL + (h + 1) * X_HEAD_DIM)


X_CHUNK = 32


def _x_probs(s_ref, rows):
    blocks = [s_ref[rows, _lane_block(b)] * (1.0 / math.sqrt(X_HEAD_DIM)) for b in range(MEM_LEN // LANES)]
    m = jnp.max(_fold(jnp.maximum, blocks), axis=1, keepdims=True)
    es = [jnp.exp(blk - m) for blk in blocks]
    den = jnp.sum(_fold(jnp.add, es), axis=1, keepdims=True)
    return [e / den for e in es]


def _xattn_fwd(q, kv):
    S = q.shape[0]
    t = _row_tile(S)
    chunk = min(X_CHUNK, t)

    def body(q_ref, kv_ref, o_ref):
        for h in range(X_HEADS):
            s = lax.dot_general(q_ref[:, _xhead(h)], kv_ref[:, _xhead(h)], NT, preferred_element_type=F32)
            p_rows = []
            for r in range(t // chunk):
                rows = slice(r * chunk, (r + 1) * chunk)
                p_rows.append(jnp.concatenate([p.astype(BF16) for p in _x_probs(s, rows)], axis=1))
            o_ref[:, _xhead(h)] = lax.dot_general(jnp.concatenate(p_rows, axis=0), kv_ref[:, _xvhead(h)], NN,
                                                  preferred_element_type=F32).astype(BF16)

    return pl.pallas_call(
        body, name="xattn_fwd", grid=(S // t,),
        in_specs=[pl.BlockSpec((t, D_MODEL), lambda i: (i, 0)), pl.BlockSpec((MEM_LEN, 2 * D_MODEL), lambda i: (0, 0))],
        out_specs=pl.BlockSpec((t, D_MODEL), lambda i: (i, 0)),
        out_shape=jax.ShapeDtypeStruct((S, D_MODEL), BF16), compiler_params=_params("parallel"))(q, kv)


def _xattn_bwd(q, kv, do):
    S = q.shape[0]
    t = _row_tile(S)
    nb = S // t
    scale = 1.0 / math.sqrt(X_HEAD_DIM)
    chunk = min(X_CHUNK, t)

    def body(q_ref, kv_ref, do_ref, dq_ref, dkv_ref, acc):
        i = pl.program_id(0)

        @pl.when(i == 0)
        def _():
            acc[...] = jnp.zeros_like(acc)

        for h in range(X_HEADS):
            qh = q_ref[:, _xhead(h)]
            kh = kv_ref[:, _xhead(h)]
            doh = do_ref[:, _xhead(h)]
            s_s = lax.dot_general(qh, kh, NT, preferred_element_type=F32)
            dp_s = lax.dot_general(doh, kv_ref[:, _xvhead(h)], NT, preferred_element_type=F32)
            p_rows, ds_rows = [], []
            for r in range(t // chunk):
                rows = slice(r * chunk, (r + 1) * chunk)
                ps = _x_probs(s_s, rows)
                dps = [dp_s[rows, _lane_block(b)] for b in range(len(ps))]
                inner = jnp.sum(_fold(jnp.add, [dp * p for dp, p in zip(dps, ps)]), axis=1, keepdims=True)
                p_rows.append(jnp.concatenate([p.astype(BF16) for p in ps], axis=1))
                ds_rows.append(jnp.concatenate([(p * (dp - inner)).astype(BF16) for dp, p in zip(dps, ps)], axis=1))
            dsb = jnp.concatenate(ds_rows, axis=0)
            acc[:, _xvhead(h)] += lax.dot_general(jnp.concatenate(p_rows, axis=0), doh, TN,
                                                  preferred_element_type=F32)
            dq_ref[:, _xhead(h)] = (lax.dot_general(dsb, kh, NN, preferred_element_type=F32) * scale).astype(BF16)
            acc[:, _xhead(h)] += lax.dot_general(dsb, qh, TN, preferred_element_type=F32) * scale

        @pl.when(i == nb - 1)
        def _():
            dkv_ref[...] = acc[...].astype(BF16)

    row = pl.BlockSpec((t, D_MODEL), lambda i: (i, 0))
    full = pl.BlockSpec((MEM_LEN, 2 * D_MODEL), lambda i: (0, 0))
    return pl.pallas_call(
        body, name="xattn_bwd", grid=(nb,), in_specs=[row, full, row], out_specs=[row, full],
        out_shape=[jax.ShapeDtypeStruct((S, D_MODEL), BF16), jax.ShapeDtypeStruct((MEM_LEN, 2 * D_MODEL), BF16)],
        scratch_shapes=[pltpu.VMEM((MEM_LEN, 2 * D_MODEL), F32)],
        compiler_params=_params("arbitrary"))(q, kv, do)


def _comm_shapes(arrs):
    return [jax.ShapeDtypeStruct((N_DEV,) + tuple(a.shape[-2:]), a.dtype) for a in arrs]


def _comm_scratch(n):
    if n == 0:
        return []
    return [pltpu.SemaphoreType.DMA((n, N_DEV - 1)), pltpu.SemaphoreType.DMA((n, N_DEV - 1)),
            pltpu.SemaphoreType.DMA((n,))]


def _comm_copies(ins, outs, send_sems, recv_sems, local_sems):
    x, y, c = lax.axis_index("x"), lax.axis_index("y"), lax.axis_index("c")
    me = 4 * x + 2 * y + c
    copies = []
    for w in range(len(ins)):
        src = ins[w] if len(ins[w].shape) == 2 else ins[w].at[me]
        copies.append(pltpu.make_async_copy(src, outs[w].at[me], local_sems.at[w]))
    for k in range(1, N_DEV):
        px = 1 - x if k & 4 else x
        py = 1 - y if k & 2 else y
        pc = 1 - c if k & 1 else c
        peer = 4 * px + 2 * py + pc
        for w in range(len(ins)):
            src = ins[w] if len(ins[w].shape) == 2 else ins[w].at[peer]
            copies.append(pltpu.make_async_remote_copy(
                src_ref=src, dst_ref=outs[w].at[me], send_sem=send_sems.at[w, k - 1],
                recv_sem=recv_sems.at[w, k - 1], device_id=(px, py, pc), device_id_type=pl.DeviceIdType.MESH))
    return copies


class _Gather:
    def __init__(self, ins, outs, send_sems, recv_sems, local_sems):
        x, y, c = lax.axis_index("x"), lax.axis_index("y"), lax.axis_index("c")
        me = 4 * x + 2 * y + c
        sibling = (x, y, 1 - c)
        self.local, self.mine, self.passed = [], [], []
        for w in range(len(ins)):
            def remote(idx, src, slot, dev, w=w):
                return pltpu.make_async_remote_copy(
                    src_ref=src, dst_ref=outs[w].at[slot], send_sem=send_sems.at[w, idx],
                    recv_sem=recv_sems.at[w, idx], device_id=dev, device_id_type=pl.DeviceIdType.MESH)

            self.local.append(pltpu.make_async_copy(ins[w], outs[w].at[me], local_sems.at[w]))
            mine, passed = [remote(0, ins[w], me, sibling)], []
            for j, (fx, fy) in enumerate(((0, 1), (1, 0), (1, 1))):
                px = 1 - x if fx else x
                py = 1 - y if fy else y
                slot = 4 * px + 2 * py + c
                mine.append(remote(1 + j, ins[w], me, (px, py, c)))
                passed.append(remote(4 + j, outs[w].at[slot], slot, sibling))
            self.mine.append(mine)
            self.passed.append(passed)

    def start(self):
        for cp in self.local:
            cp.start()
        for mine in self.mine:
            for cp in mine:
                cp.start()

    def pass_on(self):
        for mine, passed in zip(self.mine, self.passed):
            for j, cp in enumerate(passed):
                mine[1 + j].wait_recv()
                cp.start()

    def finish(self):
        for mine, passed in zip(self.mine, self.passed):
            mine[0].wait_recv()
            for cp in passed:
                cp.wait_recv()
            for cp in mine + passed:
                cp.wait_send()
        for cp in self.local:
            cp.wait()


def _exchange(name, arrs):
    n = len(arrs)
    gather = all(a.ndim == 2 for a in arrs)

    def body(*refs):
        if gather:
            g = _Gather(refs[:n], refs[n:2 * n], *refs[2 * n:])
            g.start()
            g.pass_on()
            g.finish()
            return
        copies = _comm_copies(refs[:n], refs[n:2 * n], *refs[2 * n:])
        for cp in copies:
            cp.start()
        for cp in copies:
            cp.wait()

    any_spec = pl.BlockSpec(memory_space=pl.ANY)
    return pl.pallas_call(
        body, name=name, in_specs=[any_spec] * n, out_specs=[any_spec] * n, out_shape=_comm_shapes(arrs),
        scratch_shapes=_comm_scratch(n))(*arrs)


def _adamw_math(w, g, m, v):
    m = ADAM_B1 * m + (1.0 - ADAM_B1) * g
    v = ADAM_B2 * v + (1.0 - ADAM_B2) * (g * g)
    m_hat = m / (1.0 - ADAM_B1 ** ADAM_STEP)
    v_hat = v / (1.0 - ADAM_B2 ** ADAM_STEP)
    delta = -ADAM_LR * (m_hat / (jnp.sqrt(v_hat) + ADAM_EPS) + ADAM_WD * w)
    return delta, m, v


def _sum_parts(p_ref):
    g = p_ref[0].astype(F32)
    for s in range(1, N_DEV):
        g = g + p_ref[s].astype(F32)
    return g


def _adamw_big(name, w, m, v, parts, tr, comm=()):
    L, R, C = w.shape
    nc = len(comm)
    gather = all(a.ndim == 2 for a in comm)
    nr = R // tr

    def exchange(comm_in, comm_out, sems, begin):
        if gather:
            g = _Gather(comm_in, comm_out, *sems)
            if begin:
                g.start()
            else:
                g.pass_on()
                g.finish()
        else:
            for cp in _comm_copies(comm_in, comm_out, *sems):
                cp.start() if begin else cp.wait()

    def body(w_ref, m_ref, v_ref, *rest):
        p_refs = rest[:L]
        comm_in = rest[L:L + nc]
        g_ref, d_ref, nm_ref, nv_ref = rest[L + nc:L + nc + 4]
        comm_out = rest[L + nc + 4:L + 2 * nc + 4]
        sems = rest[L + 2 * nc + 4:]
        layer = pl.program_id(0)
        if nc:
            @pl.when((layer == 0) & (pl.program_id(1) == 0))
            def _():
                exchange(comm_in, comm_out, sems, True)

        for j in range(L):
            @pl.when(layer == j)
            def _(j=j):
                g = _sum_parts(p_refs[j])
                delta, nm, nv = _adamw_math(w_ref[...], g, m_ref[...], v_ref[...])
                g_ref[...] = g
                d_ref[...] = delta
                nm_ref[...] = nm
                nv_ref[...] = nv

        if nc:
            @pl.when((layer == L - 1) & (pl.program_id(1) == nr - 1))
            def _():
                exchange(comm_in, comm_out, sems, False)

    blk = pl.BlockSpec((None, tr, C), lambda l, i: (l, i, 0))

    def part_spec(j):
        return pl.BlockSpec((N_DEV, tr, C), lambda l, i: (0, jnp.where(l == j, i, 0), 0))

    shp = jax.ShapeDtypeStruct((L, R, C), F32)
    any_spec = pl.BlockSpec(memory_space=pl.ANY)
    return pl.pallas_call(
        body, name=name, grid=(L, nr),
        in_specs=[blk, blk, blk] + [part_spec(j) for j in range(L)] + [any_spec] * nc,
        out_specs=[blk] * 4 + [any_spec] * nc, out_shape=[shp] * 4 + _comm_shapes(comm),
        scratch_shapes=_comm_scratch(nc),
        compiler_params=_params("arbitrary", "arbitrary"))(w, m, v, *parts, *comm)


def _adamw_small(w, m, v, parts):
    R, C = w.shape

    def body(w_ref, m_ref, v_ref, p_ref, g_ref, d_ref, nm_ref, nv_ref):
        g = _sum_parts(p_ref)
        delta, nm, nv = _adamw_math(w_ref[...], g, m_ref[...], v_ref[...])
        g_ref[...] = g
        d_ref[...] = delta
        nm_ref[...] = nm
        nv_ref[...] = nv

    shp = jax.ShapeDtypeStruct((R, C), F32)
    return pl.pallas_call(body, name="adamw_small", out_shape=[shp] * 4,
                          compiler_params=pltpu.CompilerParams(vmem_limit_bytes=VMEM_LIMIT))(w, m, v, parts)


def _vec(a):
    return a.reshape(1, -1)


W_IN_SHARD = IN_COLS // N_DEV
W_IN_ROWS = 272


def _w_in_travel(a):
    pad = [(0, 0)] * (a.ndim - 2) + [(0, W_IN_ROWS - W_IN_SHARD), (0, 0)]
    return jnp.pad(jnp.swapaxes(a, -1, -2), pad)


def _unpack_w_in(g):
    full = jnp.transpose(g[:, :W_IN_SHARD, :], (2, 0, 1)).reshape(D_MODEL, IN_COLS)
    qkv = full[:, :QKV_COLS]
    f = full[:, QKV_COLS:QKV_COLS + FOX_HEADS]
    u = full[:, QKV_COLS + FOX_HEADS:]
    uf = jnp.concatenate([u, f, jnp.zeros((D_MODEL, UF_COLS - POOL_WIDTH - FOX_HEADS), g.dtype)], axis=1)
    return jnp.concatenate([qkv, uf], axis=1)


def _pack_dw_in(dwp):
    qkv = dwp[:, :QKV_COLS]
    u = dwp[:, QKV_COLS:QKV_COLS + POOL_WIDTH]
    f = dwp[:, QKV_COLS + POOL_WIDTH:QKV_COLS + POOL_WIDTH + FOX_HEADS]
    full = jnp.concatenate([qkv, f, u], axis=1)
    return _w_in_travel(jnp.transpose(full.reshape(D_MODEL, N_DEV, W_IN_SHARD), (1, 0, 2)))


REST = ['w_out', 'wq_x', 'wkv_x', 'wo_x', 'w_up', 'w_down']


def _layer_fwd(x0, h1, mem, sp, g_in, shards, g_next):
    S = x0.shape[0]
    sv = {"x0": x0}
    w_inp = _unpack_w_in(g_in)
    qkv, uf = _mm_rows("mm_in", [(h1, w_inp, "nn")],
                       [(BF16, 0, QKV_COLS, "id"), (F32, QKV_COLS, UF_COLS, "id")], piece=UF_COLS)
    c = _gate_fwd(uf, sp["b_forget"])
    cT = jnp.transpose(c[:, :FOX_HEADS]).reshape(FOX_HEADS, 1, S)
    o, ob, lse, *got = _fox_fwd(qkv, cT, shards)
    g_out, g_q, g_kv, g_o, g_up, g_down = got[:6]
    W = dict(inp=w_inp, out=g_out.reshape(D_MODEL, D_MODEL), q=g_q.reshape(D_MODEL, D_MODEL), kv=g_kv,
             o=g_o.reshape(D_MODEL, D_MODEL), up=g_up, down=g_down.reshape(D_FF, D_MODEL))
    pool = _pool_fwd(uf, sp["pool_w"], sp["pool_scale"])
    cat = jnp.concatenate([ob, pool], axis=1)
    mix, x1, h2 = _mm_resid_norm("mm_sq_norm", cat, W["out"], x0, sp["g_mix_post"], sp["g_x_pre"])
    mn = _norm_fwd("norm_mem", mem, sp["g_mem"])
    q2 = _mm1("mm_q", h2, W["q"], "nn", D_MODEL, BF16)
    kv = _mm1("mm_kv", mn, W["kv"], "nn3", 2 * D_MODEL, BF16, piece=2 * D_MODEL // N_DEV)
    o2 = _xattn_fwd(q2, kv)
    xo, x2, h3 = _mm_resid_norm("mm_sq_norm", o2, W["o"], x1, sp["g_x_post"], sp["g_ffn_pre"])
    up, act = _mm_rows("mm_up", [(h3, W["up"], "nn3")], [(BF16, 0, D_FF, "id"), (BF16, 0, D_FF, "relu2")],
                       piece=D_FF // N_DEV)
    y, x3, h_next = _mm_resid_norm("mm_down_norm" if g_next is not None else "mm_down_norm_last", act, W["down"], x2,
                                   sp["g_ffn_post"], g_next)
    sv.update(h1=h1, uf=uf, cT=cT, qkv=qkv, o=o, lse=lse, cat=cat, mix=mix, x1=x1, h2=h2, mn=mn, q2=q2, kv=kv,
              o2=o2, xo=xo, x2=x2, h3=h3, up=up, act=act, y=y)
    return x3, h_next, sv, W, (got[6] if len(got) > 6 else None)


def _layer_bwd(dx3, dy, mem, sv, sp, W, carried, below):
    S = dx3.shape[0]
    gs = {}
    gb = {}
    (dup,) = _mm_rows("mm_dup", [(dy, W["down"], "nt")], [(BF16, 0, D_FF, "drelu2")], extra=sv["up"])
    gb["w_down"] = _mm_tn("mm_dw_down", sv["act"], dy, BF16).reshape(N_DEV, D_FF // N_DEV, D_MODEL)
    gb["w_up"] = _mm_tn("mm_dw_up", sv["h3"], dup, BF16, shard_cols=D_FF // N_DEV)
    dx2, gs["g_ffn_pre"], dxo, gs["g_x_post"] = _norm_bwd(
        "mm_dh3_norm_bwd", (dup, W["up"], "nt3"), sv["x2"], sp["g_ffn_pre"], dx3, F32,
        below=(sv["xo"], sp["g_x_post"]))
    do2 = _mm1("mm_sq_t", dxo, W["o"], "nt", D_MODEL, BF16)
    gb["wo_x"] = _mm_tn("mm_dw_sq", sv["o2"], dxo, BF16).reshape(N_DEV, D_MODEL // N_DEV, D_MODEL)
    dq2, dkvb = _xattn_bwd(sv["q2"], sv["kv"], do2)
    gb["wq_x"] = _mm_tn("mm_dw_sq", sv["h2"], dq2, BF16).reshape(N_DEV, D_MODEL // N_DEV, D_MODEL)
    gb["wkv_x"] = _mm_tn("mm_dw_kv", sv["mn"], dkvb, BF16, shard_cols=2 * D_MODEL // N_DEV)
    dmn = _mm1("mm_dmn", dkvb, W["kv"], "nt3", D_MODEL, F32)
    _, gs["g_mem"] = _norm_bwd("norm_bwd_mem", dmn, mem, sp["g_mem"], None, BF16)
    dx1, gs["g_x_pre"], dmix, gs["g_mix_post"] = _norm_bwd(
        "mm_dh2_norm_bwd", (dq2, W["q"], "nt"), sv["x1"], sp["g_x_pre"], dx2, F32,
        below=(sv["mix"], sp["g_mix_post"]))
    doh, dpool = _mm_rows("mm_dcat", [(dmix, W["out"], "nt")],
                          [(BF16, 0, FOX_WIDTH, "id"), (F32, FOX_WIDTH, POOL_WIDTH, "id")])
    gb["w_out"] = _mm_tn("mm_dw_sq", sv["cat"], dmix, BF16).reshape(N_DEV, D_MODEL // N_DEV, D_MODEL)
    du, gs["pool_w"], gs["pool_scale"] = _pool_bwd(sv["uf"], dpool, sp["pool_w"], sp["pool_scale"])
    dq, dk, dv, dcT, *got = _fox_bwd(sv["qkv"], sv["cT"], sv["o"], sv["lse"], doh, [gb[n] for n in REST] + carried)
    dc = jnp.pad(jnp.transpose(dcT.reshape(FOX_HEADS, S)), ((0, 0), (0, LANES - FOX_HEADS)))
    dfg, db = _gate_bwd(dc, sv["uf"], sp["b_forget"])
    gs["b_forget"] = db[:, :FOX_HEADS]
    dproj = jnp.concatenate([dq, dk, dv, du, dfg], axis=1)
    dwp = _mm_tn("mm_dw_in", sv["h1"], dproj, BF16, piece=UF_COLS)
    dh1 = (dproj, W["inp"], "nt")
    if below is None:
        dx0, gs["g_mix_pre"] = _norm_bwd("mm_dh1_norm_bwd_first", dh1, sv["x0"], sp["g_mix_pre"], dx1, F32)
        lower = None
    else:
        dx0, gs["g_mix_pre"], *lower = _norm_bwd("mm_dh1_norm_bwd", dh1, sv["x0"], sp["g_mix_pre"], dx1, F32,
                                                 below=below)
    return dx0, lower, dict(zip(REST, got[:6])), got[6:], _pack_dw_in(dwp), gs


def _small_rows(shape):
    return -(-math.prod(shape) // (8 * LANES)) * 8


def _pack_small(d):
    blocks = []
    for n in SMALL:
        rows = _small_rows(d[n].shape)
        if d[n].shape[-1] == LANES:
            blocks.append(d[n].reshape(rows, LANES))
        else:
            flat = d[n].reshape(-1)
            blocks.append(jnp.pad(flat, (0, rows * LANES - flat.shape[0])).reshape(rows, LANES))
    return jnp.concatenate(blocks, axis=0)


def _unpack_small(packed, like):
    out = {}
    row = 0
    for n in SMALL:
        shape = like[n].shape
        rows = _small_rows(shape)
        block = packed[row:row + rows]
        out[n] = block.reshape(shape) if shape[-1] == LANES else block.reshape(-1)[:math.prod(shape)].reshape(shape)
        row += rows
    return out


def kernel(x, mem, g_mix_pre, w_in, b_forget, pool_w, pool_scale, w_out, g_mix_post, g_x_pre, g_mem, wq_x, wkv_x, wo_x, g_x_post, g_ffn_pre, w_up, w_down, g_ffn_post, loss_target, m_g_mix_pre, m_w_in, m_b_forget, m_pool_w, m_pool_scale, m_w_out, m_g_mix_post, m_g_x_pre, m_g_mem, m_wq_x, m_wkv_x, m_wo_x, m_g_x_post, m_g_ffn_pre, m_w_up, m_w_down, m_g_ffn_post, v_g_mix_pre, v_w_in, v_b_forget, v_pool_w, v_pool_scale, v_w_out, v_g_mix_post, v_g_x_pre, v_g_mem, v_wq_x, v_wkv_x, v_wo_x, v_g_x_post, v_g_ffn_pre, v_w_up, v_w_down, v_g_ffn_post):
    w = dict(g_mix_pre=g_mix_pre, w_in=w_in, b_forget=b_forget, pool_w=pool_w, pool_scale=pool_scale, w_out=w_out,
             g_mix_post=g_mix_post, g_x_pre=g_x_pre, g_mem=g_mem, wq_x=wq_x, wkv_x=wkv_x, wo_x=wo_x,
             g_x_post=g_x_post, g_ffn_pre=g_ffn_pre, w_up=w_up, w_down=w_down, g_ffn_post=g_ffn_post)
    mom = dict(g_mix_pre=m_g_mix_pre, w_in=m_w_in, b_forget=m_b_forget, pool_w=m_pool_w, pool_scale=m_pool_scale,
               w_out=m_w_out, g_mix_post=m_g_mix_post, g_x_pre=m_g_x_pre, g_mem=m_g_mem, wq_x=m_wq_x,
               wkv_x=m_wkv_x, wo_x=m_wo_x, g_x_post=m_g_x_post, g_ffn_pre=m_g_ffn_pre, w_up=m_w_up,
               w_down=m_w_down, g_ffn_post=m_g_ffn_post)
    var = dict(g_mix_pre=v_g_mix_pre, w_in=v_w_in, b_forget=v_b_forget, pool_w=v_pool_w, pool_scale=v_pool_scale,
               w_out=v_w_out, g_mix_post=v_g_mix_post, g_x_pre=v_g_x_pre, g_mem=v_g_mem, wq_x=v_wq_x,
               wkv_x=v_wkv_x, wo_x=v_wo_x, g_x_post=v_g_x_post, g_ffn_pre=v_g_ffn_pre, w_up=v_w_up,
               w_down=v_w_down, g_ffn_post=v_g_ffn_post)
    S = x.shape[1]
    xs = x.reshape(S, D_MODEL)
    mems = mem.reshape(MEM_LEN, D_MODEL)
    target = loss_target.reshape(S, D_MODEL)

    def small_params(l):
        return dict(
            g_mix_pre=_vec(g_mix_pre[l]), g_mix_post=_vec(g_mix_post[l]), g_x_pre=_vec(g_x_pre[l]),
            g_mem=_vec(g_mem[l]), g_x_post=_vec(g_x_post[l]), g_ffn_pre=_vec(g_ffn_pre[l]),
            g_ffn_post=_vec(g_ffn_post[l]), pool_scale=_vec(pool_scale[l]), pool_w=pool_w[l].astype(BF16),
            b_forget=jnp.pad(_vec(b_forget[l]), ((0, 0), (0, LANES - FOX_HEADS))))

    shard = {n: [w[n][l].astype(BF16) for l in range(DEPTH)] for n in REST}
    shard["w_in"] = [_w_in_travel(w_in[l].astype(BF16)) for l in range(DEPTH)]
    sps = [small_params(l) for l in range(DEPTH)]
    saved, weights = [], []
    h = xs
    (g_in,) = _exchange("gather_w_in", [shard["w_in"][0]])
    hn = _norm_fwd("norm_fwd", xs, sps[0]["g_mix_pre"])
    for l in range(DEPTH):
        travelling = [shard[n][l] for n in REST] + ([shard["w_in"][l + 1]] if l + 1 < DEPTH else [])
        g_next = sps[l + 1]["g_mix_pre"] if l + 1 < DEPTH else None
        h, hn, sv, W, g_in = _layer_fwd(h, hn, mems, sps[l], g_in, travelling, g_next)
        saved.append(sv)
        weights.append(W)
    dh, sq = _loss_fwd_bwd(h, target)
    loss = lax.psum(0.5 * sq[0, 0] / D_MODEL, ("x", "y", "c"))

    parts = [dict() for _ in range(DEPTH)]
    small_grads = [None] * DEPTH
    carried = []
    lower = _norm_bwd("norm_bwd_b", dh, saved[-1]["y"], sps[-1]["g_ffn_post"], None, BF16)
    for l in reversed(range(DEPTH)):
        dy, dg_ffn_post = lower
        below = (saved[l - 1]["y"], sps[l - 1]["g_ffn_post"]) if l > 0 else None
        dh, lower, got, got_carried, dw_in, gs = _layer_bwd(dh, dy, mems, saved[l], sps[l], weights[l], carried, below)
        gs["g_ffn_post"] = dg_ffn_post
        parts[l].update(got)
        if got_carried:
            parts[l + 1]["w_in"] = got_carried[0]
        carried = [dw_in]
        small_grads[l] = gs
    grad_x = dh.reshape(1, S, D_MODEL)

    grads, deltas, new_m, new_v = {}, {}, {}, {}
    rows = dict(w_in=128, w_out=128, wq_x=128, wkv_x=256, wo_x=128, w_up=256, w_down=128)
    sg = {n: jnp.stack([small_grads[l][n].reshape(w[n].shape[1:]) for l in range(DEPTH)]) for n in SMALL}
    riders = dict(w_down=carried, w_up=[_pack_small(sg)])
    for n in ["w_down", "w_up", "w_out", "wq_x", "wkv_x", "wo_x", "w_in"]:
        if n == "w_in":
            for l in range(DEPTH):
                parts[l]["w_in"] = jnp.swapaxes(parts[l]["w_in"][:, :W_IN_SHARD, :], 1, 2)
        grads[n], deltas[n], new_m[n], new_v[n], *got = _adamw_big(
            "adamw_" + n, w[n], mom[n], var[n], [parts[l][n] for l in range(DEPTH)], rows[n], riders.get(n, ()))
        if n == "w_down":
            (parts[0]["w_in"],) = got
        elif n == "w_up":
            (sg_parts,) = got
    outs = _adamw_small(_pack_small(w), _pack_small(mom), _pack_small(var), sg_parts)
    for d, packed in zip((grads, deltas, new_m, new_v), outs):
        d.update(_unpack_small(packed, w))

    return (loss, grad_x, *[grads[n] for n in W_NAMES], *[deltas[n] for n in W_NAMES],
            *[new_m[n] for n in W_NAMES], *[new_v[n] for n in W_NAMES])
```

```python
import math

import jax
import jax.numpy as jnp
from jax import lax
from jax.experimental import pallas as pl
from jax.experimental.pallas import tpu as pltpu

F32 = jnp.float32
BF16 = jnp.bfloat16

D_MODEL = 1024
DEPTH = 4
FOX_WIDTH = 512
FOX_HEADS = 8
FOX_HEAD_DIM = 64
POOL_WIDTH = 512
POOL_WINDOWS = (2, 4, 8, 16)
POOL_GROUP_DIM = 128
POOL_HALO = 16
MEM_LEN = 256
X_HEADS = 4
X_HEAD_DIM = 256
D_FF = 4096
EPS = 1e-6
IN_COLS = 2056
QKV_COLS = 3 * FOX_WIDTH
UF_COLS = 640
INP_COLS = QKV_COLS + UF_COLS
N_DEV = 8
LANES = 128

ADAM_LR = 0.001
ADAM_B1 = 0.9
ADAM_B2 = 0.999
ADAM_EPS = 1e-08
ADAM_WD = 0.01
ADAM_STEP = 10

VMEM_LIMIT = 56 * 1024 * 1024

W_NAMES = ['g_mix_pre', 'w_in', 'b_forget', 'pool_w', 'pool_scale', 'w_out', 'g_mix_post', 'g_x_pre', 'g_mem',
           'wq_x', 'wkv_x', 'wo_x', 'g_x_post', 'g_ffn_pre', 'w_up', 'w_down', 'g_ffn_post']
BIG = ['w_in', 'w_out', 'wq_x', 'wkv_x', 'wo_x', 'w_up', 'w_down']
SMALL = [n for n in W_NAMES if n not in BIG]

NN = (((1,), (0,)), ((), ()))
NT = (((1,), (1,)), ((), ()))
TN = (((0,), (0,)), ((), ()))


def _params(*sem):
    return pltpu.CompilerParams(dimension_semantics=sem, vmem_limit_bytes=VMEM_LIMIT)


def _row_tile(s):
    return min(s, 512)


def _product(a_ref, w_ref, kind, c0, pw):
    cols = slice(c0, c0 + pw)
    if kind == "nn":
        return lax.dot_general(a_ref[...], w_ref[:, cols], NN, preferred_element_type=F32)
    if kind == "nt":
        return lax.dot_general(a_ref[...], w_ref[cols, :], NT, preferred_element_type=F32)
    n = w_ref.shape[2]
    if kind == "nn3":
        assert pw == n and c0 % n == 0
        return lax.dot_general(a_ref[...], w_ref[c0 // n], NN, preferred_element_type=F32)
    r = None
    for j in range(w_ref.shape[0]):
        part = lax.dot_general(a_ref[:, j * n:(j + 1) * n], w_ref[j, cols, :], NT, preferred_element_type=F32)
        r = part if r is None else r + part
    return r


def _resident(w):
    return pl.BlockSpec(w.shape, lambda i, nd=w.ndim: (0,) * nd)


def _mm_rows(name, terms, outs, extra=None, piece=1024):
    M = terms[0][0].shape[0]
    tm = _row_tile(M)
    nterm = len(terms)
    n_extra = 0 if extra is None else 1
    groups = {}
    for idx, (_, c0, width, fn) in enumerate(outs):
        groups.setdefault((c0, width), []).append((idx, fn))

    def body(*refs):
        a_refs = refs[0:2 * nterm:2]
        w_refs = refs[1:2 * nterm:2]
        extra_refs = refs[2 * nterm:2 * nterm + n_extra]
        out_refs = refs[2 * nterm + n_extra:]
        for (g0, gw), members in groups.items():
            for c0 in range(g0, g0 + gw, piece):
                pw = min(piece, g0 + gw - c0)
                r = None
                for a_ref, w_ref, (_, w, kind) in zip(a_refs, w_refs, terms):
                    part = _product(a_ref, w_ref, kind, c0, pw)
                    r = part if r is None else r + part
                dst = slice(c0 - g0, c0 - g0 + pw)
                for idx, fn in members:
                    if fn == "relu2":
                        rp = jnp.maximum(r, 0.0)
                        val = rp * rp
                    elif fn == "drelu2":
                        val = r * (2.0 * jnp.maximum(extra_refs[0][:, dst].astype(F32), 0.0))
                    else:
                        val = r
                    out_refs[idx][:, dst] = val.astype(out_refs[idx].dtype)

    in_specs, ins = [], []
    for a, w, _ in terms:
        in_specs.append(pl.BlockSpec((tm, a.shape[1]), lambda i: (i, 0)))
        in_specs.append(pl.BlockSpec(w.shape, lambda i, nd=w.ndim: (0,) * nd))
        ins += [a, w]
    if extra is not None:
        in_specs.append(pl.BlockSpec((tm, extra.shape[1]), lambda i: (i, 0)))
        ins.append(extra)
    res = pl.pallas_call(
        body, name=name, grid=(M // tm,), in_specs=in_specs,
        out_specs=[pl.BlockSpec((tm, width), lambda i: (i, 0)) for _, _, width, _ in outs],
        out_shape=[jax.ShapeDtypeStruct((M, width), dt) for dt, _, width, _ in outs],
        compiler_params=_params("parallel"))(*ins)
    return res


def _mm1(name, a, w, kind, n_cols, dtype, piece=1024):
    return _mm_rows(name, [(a, w, kind)], [(dtype, 0, n_cols, "id")], piece=piece)[0]


def _mm_tn(name, a, b, out_dtype, shard_cols=None, piece=512):
    K, M = a.shape
    b_parts = b if isinstance(b, list) else [b]
    nb = len(b_parts)
    N = sum(p.shape[1] for p in b_parts)
    tk = _row_tile(K)
    nk = K // tk
    piece = shard_cols or min(piece, N)

    def body(a_ref, *rest):
        b_refs = rest[:nb]
        o_ref, acc = rest[nb:]
        k = pl.program_id(0)

        @pl.when(k == 0)
        def _():
            acc[...] = jnp.zeros_like(acc)

        a_t = jnp.transpose(a_ref[...])
        if nb == 1:
            for c0 in range(0, N, piece):
                cols = slice(c0, min(c0 + piece, N))
                acc[:, cols] += lax.dot_general(a_t, b_refs[0][:, cols], NN, preferred_element_type=F32)
        else:
            c0 = 0
            for b_ref in b_refs:
                cols = slice(c0, c0 + b_ref.shape[1])
                acc[:, cols] += lax.dot_general(a_t, b_ref[...], NN, preferred_element_type=F32)
                c0 += b_ref.shape[1]

        @pl.when(k == nk - 1)
        def _():
            for c0 in range(0, N, piece):
                cols = slice(c0, min(c0 + piece, N))
                if shard_cols:
                    o_ref[c0 // piece] = acc[:, cols].astype(o_ref.dtype)
                else:
                    o_ref[:, cols] = acc[:, cols].astype(o_ref.dtype)

    out_dims = (N // shard_cols, M, shard_cols) if shard_cols else (M, N)
    return pl.pallas_call(
        body, name=name, grid=(nk,),
        in_specs=[pl.BlockSpec((tk, M), lambda k: (k, 0))]
        + [pl.BlockSpec((tk, p.shape[1]), lambda k: (k, 0)) for p in b_parts],
        out_specs=pl.BlockSpec(out_dims, lambda k, nd=len(out_dims): (0,) * nd),
        out_shape=jax.ShapeDtypeStruct(out_dims, out_dtype),
        scratch_shapes=[pltpu.VMEM((M, N), F32)],
        compiler_params=_params("arbitrary"))(a, *b_parts)


def _norm_fwd(name, x, g):
    S, Dm = x.shape
    ts = _row_tile(S)

    def body(x_ref, g_ref, h_ref):
        xv = x_ref[...]
        r = lax.rsqrt(jnp.mean(xv * xv, axis=-1, keepdims=True) + EPS)
        h_ref[...] = ((xv * r) * g_ref[...]).astype(BF16)

    return pl.pallas_call(
        body, name=name, grid=(S // ts,),
        in_specs=[pl.BlockSpec((ts, Dm), lambda i: (i, 0)), pl.BlockSpec((1, Dm), lambda i: (0, 0))],
        out_specs=pl.BlockSpec((ts, Dm), lambda i: (i, 0)),
        out_shape=jax.ShapeDtypeStruct((S, Dm), BF16), compiler_params=_params("parallel"))(x, g)


def _mm_resid_norm(name, a, w, x, g, g_next):
    S, Dm = x.shape
    ts = _row_tile(S)
    has_next = g_next is not None

    def body(a_ref, w_ref, x_ref, g_ref, *rest):
        fv = _product(a_ref, w_ref, "nn", 0, Dm)
        r = lax.rsqrt(jnp.mean(fv * fv, axis=-1, keepdims=True) + EPS)
        xn = x_ref[...] + (fv * r) * g_ref[...]
        if has_next:
            gn_ref, f_ref, o_ref, h_ref = rest
            rn = lax.rsqrt(jnp.mean(xn * xn, axis=-1, keepdims=True) + EPS)
            h_ref[...] = ((xn * rn) * gn_ref[...]).astype(BF16)
        else:
            f_ref, o_ref = rest
        f_ref[...] = fv
        o_ref[...] = xn

    row = pl.BlockSpec((ts, Dm), lambda i: (i, 0))
    vec = pl.BlockSpec((1, Dm), lambda i: (0, 0))
    ins = [a, w, x, g] + ([g_next] if has_next else [])
    f32_rows = jax.ShapeDtypeStruct((S, Dm), F32)
    res = pl.pallas_call(
        body, name=name, grid=(S // ts,),
        in_specs=[pl.BlockSpec((ts, a.shape[1]), lambda i: (i, 0)), _resident(w), row, vec] + ([vec] if has_next else []),
        out_specs=[row, row] + ([row] if has_next else []),
        out_shape=[f32_rows, f32_rows] + ([jax.ShapeDtypeStruct((S, Dm), BF16)] if has_next else []),
        compiler_params=_params("parallel"))(*ins)
    return (res[0], res[1], res[2]) if has_next else (res[0], res[1], None)


def _rms_bwd(dov, yv, g):
    r = lax.rsqrt(jnp.mean(yv * yv, axis=-1, keepdims=True) + EPS)
    z = dov * g
    yr = yv * r
    return r * (z - yr * jnp.mean(yr * z, axis=-1, keepdims=True)), jnp.sum(dov * yr, axis=0, keepdims=True)


def _norm_bwd(name, dout, y, g, resid, out_dtype, below=None):
    S, Dm = y.shape
    ts = _row_tile(S)
    has_resid = resid is not None
    chained = below is not None
    produced = isinstance(dout, tuple)
    kind = dout[2] if produced else None
    a_parts = (dout[0] if isinstance(dout[0], list) else [dout[0]]) if produced else []
    n_a = len(a_parts)

    def body(*refs):
        refs = list(refs)
        if produced and n_a == 1:
            dov = _product(refs[0], refs[1], kind, 0, Dm)
            refs = refs[1:]
        elif produced:
            w_ref = refs[n_a]
            dov, k0 = None, 0
            for a_ref in refs[:n_a]:
                k1 = k0 + a_ref.shape[1]
                part = lax.dot_general(a_ref[...], w_ref[:, k0:k1], NT, preferred_element_type=F32)
                dov = part if dov is None else dov + part
                k0 = k1
            refs = refs[n_a:]
        else:
            dov = refs[0][...]
        y_ref, g_ref = refs[1:3]
        pos = 3
        r_ref = refs[pos] if has_resid else None
        pos += has_resid
        if chained:
            f_ref, gf_ref = refs[pos:pos + 2]
            pos += 2
        dy_ref, dg_ref = refs[pos:pos + 2]
        i = pl.program_id(0)
        dy, dg = _rms_bwd(dov, y_ref[...], g_ref[...])
        if has_resid:
            dy = dy + r_ref[...]
        dy_ref[...] = dy.astype(out_dtype)

        @pl.when(i == 0)
        def _():
            for ref in refs[pos + 1::2]:
                ref[...] = jnp.zeros_like(ref)

        dg_ref[...] += dg
        if chained:
            df_ref, dgf_ref = refs[pos + 2:pos + 4]
            df, dgf = _rms_bwd(dy, f_ref[...], gf_ref[...])
            df_ref[...] = df.astype(BF16)
            dgf_ref[...] += dgf

    row = pl.BlockSpec((ts, Dm), lambda i: (i, 0))
    vec = pl.BlockSpec((1, Dm), lambda i: (0, 0))
    if produced:
        assert n_a == 1 or kind == "nt"
        ins = a_parts + [dout[1]]
        specs = [pl.BlockSpec((ts, a.shape[1]), lambda i: (i, 0)) for a in a_parts] + [_resident(dout[1])]
    else:
        ins = [dout]
        specs = [row]
    ins += [y, g] + ([resid] if has_resid else []) + (list(below) if chained else [])
    specs += [row, vec] + ([row] if has_resid else []) + ([row, vec] if chained else [])
    vec_shape = jax.ShapeDtypeStruct((1, Dm), F32)
    return pl.pallas_call(
        body, name=name, grid=(S // ts,), in_specs=specs, out_specs=[row, vec] + ([row, vec] if chained else []),
        out_shape=[jax.ShapeDtypeStruct((S, Dm), out_dtype), vec_shape]
        + ([jax.ShapeDtypeStruct((S, Dm), BF16), vec_shape] if chained else []),
        compiler_params=_params("arbitrary"))(*ins)


def _loss_fwd_bwd(y, t):
    S, Dm = y.shape
    ts = _row_tile(S)

    def body(y_ref, t_ref, dy_ref, acc_ref):
        i = pl.program_id(0)
        e = y_ref[...] - t_ref[...]
        dy_ref[...] = e * (1.0 / Dm)

        @pl.when(i == 0)
        def _():
            acc_ref[...] = jnp.zeros_like(acc_ref)

        s = jnp.sum(jnp.sum(e * e, axis=1, keepdims=True), axis=0, keepdims=True)
        acc_ref[...] += s

    row = pl.BlockSpec((ts, Dm), lambda i: (i, 0))
    return pl.pallas_call(
        body, name="loss", grid=(S // ts,), in_specs=[row, row],
        out_specs=[row, pl.BlockSpec((8, LANES), lambda i: (0, 0))],
        out_shape=[jax.ShapeDtypeStruct((S, Dm), F32), jax.ShapeDtypeStruct((8, LANES), F32)],
        compiler_params=_params("arbitrary"))(y, t)


def _log_sigmoid(x):
    return jnp.minimum(x, 0.0) - jnp.log(1.0 + jnp.exp(-jnp.abs(x)))


def _gate_fwd(uf, bpad):
    S = uf.shape[0]
    T = _row_tile(S)

    def body(f_ref, b_ref, c_ref, carry):
        i = pl.program_id(0)

        @pl.when(i == 0)
        def _():
            carry[...] = jnp.zeros_like(carry)

        lf = _log_sigmoid(f_ref[...] + b_ref[...])
        r = lax.broadcasted_iota(jnp.int32, (T, T), 0)
        cidx = lax.broadcasted_iota(jnp.int32, (T, T), 1)
        tri = (cidx <= r).astype(F32)
        c = lax.dot_general(tri, lf, NN, precision=lax.Precision.HIGHEST, preferred_element_type=F32)
        c_ref[...] = c + carry[0:1, :]
        carry[...] = carry[...] + jnp.sum(lf, axis=0, keepdims=True)

    return pl.pallas_call(
        body, name="gate_fwd", grid=(S // T,),
        in_specs=[pl.BlockSpec((T, LANES), lambda i: (i, 4)), pl.BlockSpec((1, LANES), lambda i: (0, 0))],
        out_specs=pl.BlockSpec((T, LANES), lambda i: (i, 0)),
        out_shape=jax.ShapeDtypeStruct((S, LANES), F32),
        scratch_shapes=[pltpu.VMEM((8, LANES), F32)], compiler_params=_params("arbitrary"))(uf, bpad)


def _gate_bwd(dc, uf, bpad):
    S = uf.shape[0]
    T = _row_tile(S)
    nb = S // T

    def body(dc_ref, f_ref, b_ref, df_ref, db_ref, carry):
        i = pl.program_id(0)

        @pl.when(i == 0)
        def _():
            carry[...] = jnp.zeros_like(carry)
            db_ref[...] = jnp.zeros_like(db_ref)

        dcv = dc_ref[...]
        r = lax.broadcasted_iota(jnp.int32, (T, T), 0)
        cidx = lax.broadcasted_iota(jnp.int32, (T, T), 1)
        tri = (cidx >= r).astype(F32)
        dlf = lax.dot_general(tri, dcv, NN, precision=lax.Precision.HIGHEST, preferred_element_type=F32)
        dlf = dlf + carry[0:1, :]
        carry[...] = carry[...] + jnp.sum(dcv, axis=0, keepdims=True)
        fg = f_ref[...] + b_ref[...]
        dfg = dlf / (1.0 + jnp.exp(fg))
        df_ref[...] = dfg.astype(BF16)
        db_ref[...] += jnp.sum(dfg, axis=0, keepdims=True)

    return pl.pallas_call(
        body, name="gate_bwd", grid=(nb,),
        in_specs=[pl.BlockSpec((T, LANES), lambda i: (nb - 1 - i, 0)),
                  pl.BlockSpec((T, LANES), lambda i: (nb - 1 - i, 4)),
                  pl.BlockSpec((1, LANES), lambda i: (0, 0))],
        out_specs=[pl.BlockSpec((T, LANES), lambda i: (nb - 1 - i, 0)), pl.BlockSpec((1, LANES), lambda i: (0, 0))],
        out_shape=[jax.ShapeDtypeStruct((S, LANES), BF16), jax.ShapeDtypeStruct((1, LANES), F32)],
        scratch_shapes=[pltpu.VMEM((8, LANES), F32)], compiler_params=_params("arbitrary"))(dc, uf, bpad)


FOX_CHUNK = 32
FOX_CHUNK_BWD = 64
HEAD_PAIRS = FOX_HEADS // 2
PAIR = 2


def _masked(s, row0, col0, diagonal):
    if diagonal:
        row = row0 + lax.broadcasted_iota(jnp.int32, s.shape, 0)
        col = col0 + lax.broadcasted_iota(jnp.int32, s.shape, 1)
        s = jnp.where(col <= row, s, -jnp.inf)
    return s


def _causal_pairs(n, query_major):
    if query_major:
        pairs = [(q, k) for q in range(n) for k in range(q + 1)]
    else:
        pairs = [(q, k) for k in range(n) for q in range(k, n)]
    return (jnp.asarray([p[0] for p in pairs], jnp.int32), jnp.asarray([p[1] for p in pairs], jnp.int32))


def _lane_block(b):
    return slice(b * LANES, (b + 1) * LANES)


def _fold(op, xs):
    acc = xs[0]
    for x in xs[1:]:
        acc = op(acc, x)
    return acc


def _head_lanes(hh):
    lane = lax.broadcasted_iota(jnp.int32, (1, LANES), 1)
    return (lane < FOX_HEAD_DIM) if hh == 0 else (lane >= FOX_HEAD_DIM)


def _pick(first_head, a, b):
    return jnp.where(first_head, a, b)


def _fox_fwd(qkv, cT, comm):
    S = qkv.shape[0]
    t = _row_tile(S)
    n = S // t
    nc = len(comm)
    scale = 1.0 / math.sqrt(FOX_HEAD_DIM)
    chunk = min(FOX_CHUNK, t)
    per_head = 4
    q_tab, k_tab = _causal_pairs(n, True)
    steps = q_tab.shape[0]

    def body(qt_ref, kt_ref, q_ref, k_ref, v_ref, c_ref, *rest):
        comm_in = rest[:nc]
        o_ref, ob_ref, lse_ref = rest[nc:nc + 3]
        comm_out = rest[nc + 3:2 * nc + 3]
        scr = rest[2 * nc + 3:2 * nc + 3 + PAIR * per_head]
        sems = rest[2 * nc + 3 + PAIR * per_head:]
        hp = pl.program_id(0)
        step_id = pl.program_id(1)
        qi = qt_ref[step_id]
        ki = kt_ref[step_id]

        if nc:
            @pl.when((hp == 0) & (step_id == 0))
            def _():
                _Gather(comm_in, comm_out, *sems).start()

            @pl.when((hp == HEAD_PAIRS - 1) & (step_id == 0))
            def _():
                _Gather(comm_in, comm_out, *sems).pass_on()

        @pl.when(ki == 0)
        def _():
            for hh in range(PAIR):
                m_s, l_s, a_s, acc_s = scr[hh * per_head:hh * per_head + 4]
                m_s[...] = jnp.full_like(m_s, -jnp.inf)
                l_s[...] = jnp.zeros_like(l_s)
                acc_s[...] = jnp.zeros_like(acc_s)

        def step(diagonal):
            q2 = q_ref[...] * scale
            k2 = k_ref[...]
            v2 = v_ref[...]
            scores = []
            for hh in range(PAIR):
                qm = jnp.where(_head_lanes(hh), q2, jnp.zeros_like(q2))
                scores.append(lax.dot_general(qm, k2, NT, preferred_element_type=F32))
            for hh in range(PAIR):
                m_s, l_s, a_s, acc_s = scr[hh * per_head:(hh + 1) * per_head]
                s_s = scores[hh]
                hi_rows, lo_rows = [], []
                for r in range(t // chunk):
                    rows = slice(r * chunk, (r + 1) * chunk)
                    blocks = [_masked(s_s[rows, _lane_block(b)] - c_ref[hh, :, _lane_block(b)], r * chunk,
                                      b * LANES, diagonal) for b in range(t // LANES)]
                    m_prev = m_s[rows, :]
                    m_new = jnp.maximum(m_prev, jnp.max(_fold(jnp.maximum, blocks), axis=1, keepdims=True))
                    alpha = jnp.exp(m_prev - m_new)
                    ps = [jnp.exp(blk - m_new) for blk in blocks]
                    l_s[rows, :] = alpha * l_s[rows, :] + jnp.sum(_fold(jnp.add, ps), axis=1, keepdims=True)
                    m_s[rows, :] = m_new
                    a_s[rows, :] = alpha
                    his = [p.astype(BF16) for p in ps]
                    hi_rows.append(jnp.concatenate(his, axis=1))
                    lo_rows.append(jnp.concatenate([(p - h.astype(F32)).astype(BF16) for p, h in zip(ps, his)],
                                                   axis=1))
                pv = (lax.dot_general(jnp.concatenate(hi_rows, axis=0), v2, NN, preferred_element_type=F32)
                      + lax.dot_general(jnp.concatenate(lo_rows, axis=0), v2, NN, preferred_element_type=F32))
                acc_s[...] = a_s[...] * acc_s[...] + pv

        @pl.when(ki < qi)
        def _():
            step(False)

        @pl.when(ki == qi)
        def _():
            step(True)
            heads = []
            for hh in range(PAIR):
                m_s, l_s, a_s, acc_s = scr[hh * per_head:hh * per_head + 4]
                heads.append(acc_s[...] / l_s[...])
                lse_ref[hh] = m_s[...] + jnp.log(l_s[...])
            o2 = _pick(_head_lanes(0), heads[0], heads[1])
            o_ref[...] = o2
            ob_ref[...] = o2.astype(BF16)

        if nc:
            @pl.when((hp == HEAD_PAIRS - 1) & (step_id == steps - 1))
            def _():
                _Gather(comm_in, comm_out, *sems).finish()

    def q_cols(first_block):
        return pl.BlockSpec((t, LANES), lambda h, s, qt, kt: (qt[s], first_block + h))

    def k_cols(first_block):
        return pl.BlockSpec((t, LANES), lambda h, s, qt, kt: (kt[s], first_block + h))

    any_spec = pl.BlockSpec(memory_space=pl.ANY)
    head_scratch = [pltpu.VMEM((t, LANES), F32)] * per_head
    grid_spec = pltpu.PrefetchScalarGridSpec(
        num_scalar_prefetch=2, grid=(HEAD_PAIRS, steps),
        in_specs=[q_cols(0), k_cols(HEAD_PAIRS), k_cols(2 * HEAD_PAIRS),
                  pl.BlockSpec((PAIR, 1, t), lambda h, s, qt, kt: (h, 0, kt[s]))] + [any_spec] * nc,
        out_specs=[q_cols(0), q_cols(0),
                   pl.BlockSpec((PAIR, t, LANES), lambda h, s, qt, kt: (h, qt[s], 0))] + [any_spec] * nc,
        scratch_shapes=head_scratch * PAIR + _comm_scratch(nc))
    return pl.pallas_call(
        body, name="fox_fwd", grid_spec=grid_spec,
        out_shape=[jax.ShapeDtypeStruct((S, FOX_WIDTH), F32), jax.ShapeDtypeStruct((S, FOX_WIDTH), BF16),
                   jax.ShapeDtypeStruct((FOX_HEADS, S, LANES), F32)] + _comm_shapes(comm),
        compiler_params=_params("arbitrary", "arbitrary"))(q_tab, k_tab, qkv, qkv, qkv, cT, *comm)


def _fox_bwd(qkv, cT, o, lse, do, comm):
    S = qkv.shape[0]
    t = _row_tile(S)
    n = S // t
    nc = len(comm)
    scale = 1.0 / math.sqrt(FOX_HEAD_DIM)
    chunk = min(FOX_CHUNK_BWD, t)
    per_head = 2
    q_tab, k_tab = _causal_pairs(n, False)
    steps = q_tab.shape[0]

    def body(qt_ref, kt_ref, q_ref, k_ref, v_ref, c_ref, o_ref, do_ref, lse_ref, *rest):
        comm_in = rest[:nc]
        dq_ref, dk_ref, dv_ref, dc_ref = rest[nc:nc + 4]
        comm_out = rest[nc + 4:2 * nc + 4]
        dq_s, dk_s, dv_s = rest[2 * nc + 4:2 * nc + 7]
        scr = rest[2 * nc + 7:2 * nc + 7 + PAIR * per_head]
        sems = rest[2 * nc + 7 + PAIR * per_head:]
        hp = pl.program_id(0)
        step_id = pl.program_id(1)
        qi = qt_ref[step_id]
        ki = kt_ref[step_id]

        if nc:
            @pl.when((hp == 0) & (step_id == 0))
            def _():
                for cp in _comm_copies(comm_in, comm_out, *sems):
                    cp.start()

        @pl.when(step_id == 0)
        def _():
            dq_s[...] = jnp.zeros_like(dq_s)

        @pl.when(qi == ki)
        def _():
            dk_s[...] = jnp.zeros_like(dk_s)
            dv_s[...] = jnp.zeros_like(dv_s)
            for hh in range(PAIR):
                dc_s = scr[hh * per_head]
                dc_s[...] = jnp.zeros_like(dc_s)

        def step(diagonal):
            q2 = q_ref[...]
            k2 = k_ref[...]
            v2 = v_ref[...]
            do2 = do_ref[...]
            prod = do2.astype(F32) * o_ref[...]
            grads = []
            for hh in range(PAIR):
                dc_s, delta_s = scr[hh * per_head:(hh + 1) * per_head]
                mine = _head_lanes(hh)
                s_s = lax.dot_general(jnp.where(mine, q2 * scale, jnp.zeros_like(q2)), k2, NT,
                                      preferred_element_type=F32)
                dp_s = lax.dot_general(jnp.where(mine, do2, jnp.zeros_like(do2)), v2, NT, preferred_element_type=F32)
                delta_s[...] = jnp.broadcast_to(jnp.sum(jnp.where(mine, prod, 0.0), axis=1, keepdims=True),
                                                (t, LANES))
                dc8 = [jnp.zeros((8, LANES), F32) for _ in range(t // LANES)]
                p_rows, ds_rows = [], []
                for r in range(t // chunk):
                    rows = slice(r * chunk, (r + 1) * chunk)
                    lse = lse_ref[hh, rows, :]
                    delta = delta_s[rows, :]
                    p_blocks, ds_blocks = [], []
                    for b in range(t // LANES):
                        s = _masked(s_s[rows, _lane_block(b)] - c_ref[hh, :, _lane_block(b)], r * chunk, b * LANES,
                                    diagonal)
                        p = jnp.exp(s - lse)
                        ds = p * (dp_s[rows, _lane_block(b)] - delta)
                        p_blocks.append(p.astype(BF16))
                        ds_blocks.append(ds.astype(BF16))
                        dc8[b] = dc8[b] + jnp.sum(ds.reshape(chunk // 8, 8, LANES), axis=0)
                    p_rows.append(jnp.concatenate(p_blocks, axis=1))
                    ds_rows.append(jnp.concatenate(ds_blocks, axis=1))
                for b in range(t // LANES):
                    dc_s[:, _lane_block(b)] += jnp.sum(dc8[b], axis=0, keepdims=True)
                dsb = jnp.concatenate(ds_rows, axis=0)
                grads.append((lax.dot_general(jnp.concatenate(p_rows, axis=0), do2, TN, preferred_element_type=F32),
                              lax.dot_general(dsb, k2, NN, preferred_element_type=F32),
                              lax.dot_general(dsb, q2, TN, preferred_element_type=F32)))
            first = _head_lanes(0)
            dv_s[...] += _pick(first, grads[0][0], grads[1][0])
            q_rows = pl.ds(pl.multiple_of(qi * t, t), t)
            dq_s[q_rows, :] += _pick(first, grads[0][1], grads[1][1]) * scale
            dk_s[...] += _pick(first, grads[0][2], grads[1][2]) * scale

        @pl.when(qi > ki)
        def _():
            step(False)

        @pl.when(qi == ki)
        def _():
            step(True)

        @pl.when(qi == n - 1)
        def _():
            dk_ref[...] = dk_s[...].astype(BF16)
            dv_ref[...] = dv_s[...].astype(BF16)
            for hh in range(PAIR):
                dc_ref[hh] = -scr[hh * per_head][...]

        @pl.when(step_id == steps - 1)
        def _():
            dq_ref[...] = dq_s[...].astype(BF16)

        if nc:
            @pl.when((hp == HEAD_PAIRS - 1) & (step_id == steps - 1))
            def _():
                for cp in _comm_copies(comm_in, comm_out, *sems):
                    cp.wait()

    def q_side(first_block):
        return pl.BlockSpec((t, LANES), lambda h, s, qt, kt: (qt[s], first_block + h))

    def k_side(first_block):
        return pl.BlockSpec((t, LANES), lambda h, s, qt, kt: (kt[s], first_block + h))

    any_spec = pl.BlockSpec(memory_space=pl.ANY)
    head_scratch = [pltpu.VMEM((1, t), F32), pltpu.VMEM((t, LANES), F32)]
    grad_shape = jax.ShapeDtypeStruct((S, FOX_WIDTH), BF16)
    grid_spec = pltpu.PrefetchScalarGridSpec(
        num_scalar_prefetch=2, grid=(HEAD_PAIRS, steps),
        in_specs=[q_side(0), k_side(HEAD_PAIRS), k_side(2 * HEAD_PAIRS),
                  pl.BlockSpec((PAIR, 1, t), lambda h, s, qt, kt: (h, 0, kt[s])), q_side(0), q_side(0),
                  pl.BlockSpec((PAIR, t, LANES), lambda h, s, qt, kt: (h, qt[s], 0))] + [any_spec] * nc,
        out_specs=[pl.BlockSpec((S, LANES), lambda h, s, qt, kt: (0, h)), k_side(0), k_side(0),
                   pl.BlockSpec((PAIR, 1, t), lambda h, s, qt, kt: (h, 0, kt[s]))] + [any_spec] * nc,
        scratch_shapes=[pltpu.VMEM((S, LANES), F32), pltpu.VMEM((t, LANES), F32), pltpu.VMEM((t, LANES), F32)]
        + head_scratch * PAIR + _comm_scratch(nc))
    return pl.pallas_call(
        body, name="fox_bwd", grid_spec=grid_spec,
        out_shape=[grad_shape, grad_shape, grad_shape, jax.ShapeDtypeStruct((FOX_HEADS, 1, S), F32)]
        + _comm_shapes(comm),
        compiler_params=_params("arbitrary", "arbitrary"))(q_tab, k_tab, qkv, qkv, qkv, cT, o, do, lse, *comm)


def _lanes(g):
    return slice(g * POOL_GROUP_DIM, (g + 1) * POOL_GROUP_DIM)


def _window_sum(e, win, back):
    rows = e.shape[0]
    s = e
    sh = 1
    while sh < win:
        s = s + pltpu.roll(s, sh if back else rows - sh, 0)
        sh *= 2
    return s


def _pooled(u_ref, up_ref, i, g, win, T):
    cur = u_ref[:, _lanes(g)]
    tail = jnp.where(i > 0, up_ref[T - POOL_HALO:T, _lanes(g)], 0.0)
    e = jnp.concatenate([tail, cur], axis=0)
    s = _window_sum(e, win, True)
    t_idx = i * T - POOL_HALO + lax.broadcasted_iota(jnp.int32, (T + POOL_HALO, POOL_GROUP_DIM), 0)
    cnt = jnp.clip(t_idx + 1, 1, win).astype(F32)
    return (s / cnt - e)[POOL_HALO:, :]


def _pool_fwd(uf, pw, ps):
    S = uf.shape[0]
    T = _row_tile(S)

    def body(u_ref, up_ref, w_ref, sc_ref, o_ref):
        i = pl.program_id(0)
        for g, win in enumerate(POOL_WINDOWS):
            pb = _pooled(u_ref, up_ref, i, g, win, T).astype(BF16)
            yv = lax.dot_general(pb, w_ref[g], NN, preferred_element_type=F32)
            o_ref[:, _lanes(g)] = (yv * sc_ref[:, _lanes(g)]).astype(BF16)

    return pl.pallas_call(
        body, name="pool_fwd", grid=(S // T,),
        in_specs=[pl.BlockSpec((T, POOL_WIDTH), lambda i: (i, 0)),
                  pl.BlockSpec((T, POOL_WIDTH), lambda i: (jnp.maximum(i - 1, 0), 0)),
                  pl.BlockSpec((4, POOL_GROUP_DIM, POOL_GROUP_DIM), lambda i: (0, 0, 0)),
                  pl.BlockSpec((1, POOL_WIDTH), lambda i: (0, 0))],
        out_specs=pl.BlockSpec((T, POOL_WIDTH), lambda i: (i, 0)),
        out_shape=jax.ShapeDtypeStruct((S, POOL_WIDTH), BF16), compiler_params=_params("parallel"))(uf, uf, pw, ps)


def _pool_bwd(uf, dpool, pw, ps):
    S = uf.shape[0]
    T = _row_tile(S)
    nb = S // T

    def body(u_ref, up_ref, d_ref, dn_ref, w_ref, sc_ref, du_ref, dw_ref, dsc_ref):
        i = pl.program_id(0)

        @pl.when(i == 0)
        def _():
            dw_ref[...] = jnp.zeros_like(dw_ref)
            dsc_ref[...] = jnp.zeros_like(dsc_ref)

        t_idx = i * T + lax.broadcasted_iota(jnp.int32, (T + POOL_HALO, POOL_GROUP_DIM), 0)
        for g, win in enumerate(POOL_WINDOWS):
            pb = _pooled(u_ref, up_ref, i, g, win, T).astype(BF16)
            w = w_ref[g]
            sc = sc_ref[:, _lanes(g)]
            yv = lax.dot_general(pb, w, NN, preferred_element_type=F32)
            dov = d_ref[:, _lanes(g)]
            dsc_ref[:, _lanes(g)] += jnp.sum(dov * yv, axis=0, keepdims=True)
            head = jnp.where(i < nb - 1, dn_ref[0:POOL_HALO, _lanes(g)], 0.0)
            dyb = (jnp.concatenate([dov, head], axis=0) * sc).astype(BF16)
            dw_ref[g] += lax.dot_general(pb, dyb[:T], TN, preferred_element_type=F32)
            dpl = lax.dot_general(dyb, w, NT, preferred_element_type=F32)
            cnt = jnp.minimum(t_idx + 1, win).astype(F32)
            a = _window_sum(dpl / cnt, win, False)
            du_ref[:, _lanes(g)] = (a - dpl)[:T].astype(BF16)

    return pl.pallas_call(
        body, name="pool_bwd", grid=(nb,),
        in_specs=[pl.BlockSpec((T, POOL_WIDTH), lambda i: (i, 0)),
                  pl.BlockSpec((T, POOL_WIDTH), lambda i: (jnp.maximum(i - 1, 0), 0)),
                  pl.BlockSpec((T, POOL_WIDTH), lambda i: (i, 0)),
                  pl.BlockSpec((T, POOL_WIDTH), lambda i: (jnp.minimum(i + 1, nb - 1), 0)),
                  pl.BlockSpec((4, POOL_GROUP_DIM, POOL_GROUP_DIM), lambda i: (0, 0, 0)),
                  pl.BlockSpec((1, POOL_WIDTH), lambda i: (0, 0))],
        out_specs=[pl.BlockSpec((T, POOL_WIDTH), lambda i: (i, 0)),
                   pl.BlockSpec((4, POOL_GROUP_DIM, POOL_GROUP_DIM), lambda i: (0, 0, 0)),
                   pl.BlockSpec((1, POOL_WIDTH), lambda i: (0, 0))],
        out_shape=[jax.ShapeDtypeStruct((S, POOL_WIDTH), BF16),
                   jax.ShapeDtypeStruct((4, POOL_GROUP_DIM, POOL_GROUP_DIM), F32),
                   jax.ShapeDtypeStruct((1, POOL_WIDTH), F32)],
        compiler_params=_params("arbitrary"))(uf, uf, dpool, dpool, pw, ps)


def _xhead(h):
    return slice(h * X_HEAD_DIM, (h + 1) * X_HEAD_DIM)


def _xvhead(h):
    return slice(D_MODEL + h * X_HEAD_DIM, D_MODEL + (h + 1) * X_HEAD_DIM)


X_CHUNK = 32


def _x_probs(s_ref, rows):
    blocks = [s_ref[rows, _lane_block(b)] * (1.0 / math.sqrt(X_HEAD_DIM)) for b in range(MEM_LEN // LANES)]
    m = jnp.max(_fold(jnp.maximum, blocks), axis=1, keepdims=True)
    es = [jnp.exp(blk - m) for blk in blocks]
    den = jnp.sum(_fold(jnp.add, es), axis=1, keepdims=True)
    return [e / den for e in es]


def _xattn_fwd(q, kv):
    S = q.shape[0]
    t = _row_tile(S)
    chunk = min(X_CHUNK, t)

    def body(q_ref, kv_ref, o_ref):
        for h in range(X_HEADS):
            s = lax.dot_general(q_ref[:, _xhead(h)], kv_ref[:, _xhead(h)], NT, preferred_element_type=F32)
            p_rows = []
            for r in range(t // chunk):
                rows = slice(r * chunk, (r + 1) * chunk)
                p_rows.append(jnp.concatenate([p.astype(BF16) for p in _x_probs(s, rows)], axis=1))
            o_ref[:, _xhead(h)] = lax.dot_general(jnp.concatenate(p_rows, axis=0), kv_ref[:, _xvhead(h)], NN,
                                                  preferred_element_type=F32).astype(BF16)

    return pl.pallas_call(
        body, name="xattn_fwd", grid=(S // t,),
        in_specs=[pl.BlockSpec((t, D_MODEL), lambda i: (i, 0)), pl.BlockSpec((MEM_LEN, 2 * D_MODEL), lambda i: (0, 0))],
        out_specs=pl.BlockSpec((t, D_MODEL), lambda i: (i, 0)),
        out_shape=jax.ShapeDtypeStruct((S, D_MODEL), BF16), compiler_params=_params("parallel"))(q, kv)


def _xattn_bwd(q, kv, do):
    S = q.shape[0]
    t = _row_tile(S)
    nb = S // t
    scale = 1.0 / math.sqrt(X_HEAD_DIM)
    chunk = min(X_CHUNK, t)

    def body(q_ref, kv_ref, do_ref, dq_ref, dkv_ref, acc):
        i = pl.program_id(0)

        @pl.when(i == 0)
        def _():
            acc[...] = jnp.zeros_like(acc)

        for h in range(X_HEADS):
            qh = q_ref[:, _xhead(h)]
            kh = kv_ref[:, _xhead(h)]
            doh = do_ref[:, _xhead(h)]
            s_s = lax.dot_general(qh, kh, NT, preferred_element_type=F32)
            dp_s = lax.dot_general(doh, kv_ref[:, _xvhead(h)], NT, preferred_element_type=F32)
            p_rows, ds_rows = [], []
            for r in range(t // chunk):
                rows = slice(r * chunk, (r + 1) * chunk)
                ps = _x_probs(s_s, rows)
                dps = [dp_s[rows, _lane_block(b)] for b in range(len(ps))]
                inner = jnp.sum(_fold(jnp.add, [dp * p for dp, p in zip(dps, ps)]), axis=1, keepdims=True)
                p_rows.append(jnp.concatenate([p.astype(BF16) for p in ps], axis=1))
                ds_rows.append(jnp.concatenate([(p * (dp - inner)).astype(BF16) for dp, p in zip(dps, ps)], axis=1))
            dsb = jnp.concatenate(ds_rows, axis=0)
            acc[:, _xvhead(h)] += lax.dot_general(jnp.concatenate(p_rows, axis=0), doh, TN,
                                                  preferred_element_type=F32)
            dq_ref[:, _xhead(h)] = (lax.dot_general(dsb, kh, NN, preferred_element_type=F32) * scale).astype(BF16)
            acc[:, _xhead(h)] += lax.dot_general(dsb, qh, TN, preferred_element_type=F32) * scale

        @pl.when(i == nb - 1)
        def _():
            dkv_ref[...] = acc[...].astype(BF16)

    row = pl.BlockSpec((t, D_MODEL), lambda i: (i, 0))
    full = pl.BlockSpec((MEM_LEN, 2 * D_MODEL), lambda i: (0, 0))
    return pl.pallas_call(
        body, name="xattn_bwd", grid=(nb,), in_specs=[row, full, row], out_specs=[row, full],
        out_shape=[jax.ShapeDtypeStruct((S, D_MODEL), BF16), jax.ShapeDtypeStruct((MEM_LEN, 2 * D_MODEL), BF16)],
        scratch_shapes=[pltpu.VMEM((MEM_LEN, 2 * D_MODEL), F32)],
        compiler_params=_params("arbitrary"))(q, kv, do)


def _comm_shapes(arrs):
    return [jax.ShapeDtypeStruct((N_DEV,) + tuple(a.shape[-2:]), a.dtype) for a in arrs]


def _comm_scratch(n):
    if n == 0:
        return []
    return [pltpu.SemaphoreType.DMA((n, N_DEV - 1)), pltpu.SemaphoreType.DMA((n, N_DEV - 1)),
            pltpu.SemaphoreType.DMA((n,))]


def _comm_copies(ins, outs, send_sems, recv_sems, local_sems):
    x, y, c = lax.axis_index("x"), lax.axis_index("y"), lax.axis_index("c")
    me = 4 * x + 2 * y + c
    copies = []
    for w in range(len(ins)):
        src = ins[w] if len(ins[w].shape) == 2 else ins[w].at[me]
        copies.append(pltpu.make_async_copy(src, outs[w].at[me], local_sems.at[w]))
    for k in range(1, N_DEV):
        px = 1 - x if k & 4 else x
        py = 1 - y if k & 2 else y
        pc = 1 - c if k & 1 else c
        peer = 4 * px + 2 * py + pc
        for w in range(len(ins)):
            src = ins[w] if len(ins[w].shape) == 2 else ins[w].at[peer]
            copies.append(pltpu.make_async_remote_copy(
                src_ref=src, dst_ref=outs[w].at[me], send_sem=send_sems.at[w, k - 1],
                recv_sem=recv_sems.at[w, k - 1], device_id=(px, py, pc), device_id_type=pl.DeviceIdType.MESH))
    return copies


class _Gather:
    def __init__(self, ins, outs, send_sems, recv_sems, local_sems):
        x, y, c = lax.axis_index("x"), lax.axis_index("y"), lax.axis_index("c")
        me = 4 * x + 2 * y + c
        sibling = (x, y, 1 - c)
        self.local, self.mine, self.passed = [], [], []
        for w in range(len(ins)):
            def remote(idx, src, slot, dev, w=w):
                return pltpu.make_async_remote_copy(
                    src_ref=src, dst_ref=outs[w].at[slot], send_sem=send_sems.at[w, idx],
                    recv_sem=recv_sems.at[w, idx], device_id=dev, device_id_type=pl.DeviceIdType.MESH)

            self.local.append(pltpu.make_async_copy(ins[w], outs[w].at[me], local_sems.at[w]))
            mine, passed = [remote(0, ins[w], me, sibling)], []
            for j, (fx, fy) in enumerate(((0, 1), (1, 0), (1, 1))):
                px = 1 - x if fx else x
                py = 1 - y if fy else y
                slot = 4 * px + 2 * py + c
                mine.append(remote(1 + j, ins[w], me, (px, py, c)))
                passed.append(remote(4 + j, outs[w].at[slot], slot, sibling))
            self.mine.append(mine)
            self.passed.append(passed)

    def start(self):
        for cp in self.local:
            cp.start()
        for mine in self.mine:
            for cp in mine:
                cp.start()

    def pass_on(self):
        for mine, passed in zip(self.mine, self.passed):
            for j, cp in enumerate(passed):
                mine[1 + j].wait_recv()
                cp.start()

    def finish(self):
        for mine, passed in zip(self.mine, self.passed):
            mine[0].wait_recv()
            for cp in passed:
                cp.wait_recv()
            for cp in mine + passed:
                cp.wait_send()
        for cp in self.local:
            cp.wait()


def _exchange(name, arrs):
    n = len(arrs)
    gather = all(a.ndim == 2 for a in arrs)

    def body(*refs):
        if gather:
            g = _Gather(refs[:n], refs[n:2 * n], *refs[2 * n:])
            g.start()
            g.pass_on()
            g.finish()
            return
        copies = _comm_copies(refs[:n], refs[n:2 * n], *refs[2 * n:])
        for cp in copies:
            cp.start()
        for cp in copies:
            cp.wait()

    any_spec = pl.BlockSpec(memory_space=pl.ANY)
    return pl.pallas_call(
        body, name=name, in_specs=[any_spec] * n, out_specs=[any_spec] * n, out_shape=_comm_shapes(arrs),
        scratch_shapes=_comm_scratch(n))(*arrs)


def _adamw_math(w, g, m, v):
    m = ADAM_B1 * m + (1.0 - ADAM_B1) * g
    v = ADAM_B2 * v + (1.0 - ADAM_B2) * (g * g)
    m_hat = m / (1.0 - ADAM_B1 ** ADAM_STEP)
    v_hat = v / (1.0 - ADAM_B2 ** ADAM_STEP)
    delta = -ADAM_LR * (m_hat / (jnp.sqrt(v_hat) + ADAM_EPS) + ADAM_WD * w)
    return delta, m, v


def _sum_parts(p_ref):
    g = p_ref[0].astype(F32)
    for s in range(1, N_DEV):
        g = g + p_ref[s].astype(F32)
    return g


def _adamw_big(name, w, m, v, parts, tr, comm=()):
    L, R, C = w.shape
    nc = len(comm)
    gather = all(a.ndim == 2 for a in comm)
    nr = R // tr

    def exchange(comm_in, comm_out, sems, begin):
        if gather:
            g = _Gather(comm_in, comm_out, *sems)
            if begin:
                g.start()
            else:
                g.pass_on()
                g.finish()
        else:
            for cp in _comm_copies(comm_in, comm_out, *sems):
                cp.start() if begin else cp.wait()

    def body(w_ref, m_ref, v_ref, *rest):
        p_refs = rest[:L]
        comm_in = rest[L:L + nc]
        g_ref, d_ref, nm_ref, nv_ref = rest[L + nc:L + nc + 4]
        comm_out = rest[L + nc + 4:L + 2 * nc + 4]
        sems = rest[L + 2 * nc + 4:]
        layer = pl.program_id(0)
        if nc:
            @pl.when((layer == 0) & (pl.program_id(1) == 0))
            def _():
                exchange(comm_in, comm_out, sems, True)

        for j in range(L):
            @pl.when(layer == j)
            def _(j=j):
                g = _sum_parts(p_refs[j])
                delta, nm, nv = _adamw_math(w_ref[...], g, m_ref[...], v_ref[...])
                g_ref[...] = g
                d_ref[...] = delta
                nm_ref[...] = nm
                nv_ref[...] = nv

        if nc:
            @pl.when((layer == L - 1) & (pl.program_id(1) == nr - 1))
            def _():
                exchange(comm_in, comm_out, sems, False)

    blk = pl.BlockSpec((None, tr, C), lambda l, i: (l, i, 0))

    def part_spec(j):
        return pl.BlockSpec((N_DEV, tr, C), lambda l, i: (0, jnp.where(l == j, i, 0), 0))

    shp = jax.ShapeDtypeStruct((L, R, C), F32)
    any_spec = pl.BlockSpec(memory_space=pl.ANY)
    return pl.pallas_call(
        body, name=name, grid=(L, nr),
        in_specs=[blk, blk, blk] + [part_spec(j) for j in range(L)] + [any_spec] * nc,
        out_specs=[blk] * 4 + [any_spec] * nc, out_shape=[shp] * 4 + _comm_shapes(comm),
        scratch_shapes=_comm_scratch(nc),
        compiler_params=_params("arbitrary", "arbitrary"))(w, m, v, *parts, *comm)


def _adamw_small(w, m, v, parts):
    R, C = w.shape

    def body(w_ref, m_ref, v_ref, p_ref, g_ref, d_ref, nm_ref, nv_ref):
        g = _sum_parts(p_ref)
        delta, nm, nv = _adamw_math(w_ref[...], g, m_ref[...], v_ref[...])
        g_ref[...] = g
        d_ref[...] = delta
        nm_ref[...] = nm
        nv_ref[...] = nv

    shp = jax.ShapeDtypeStruct((R, C), F32)
    return pl.pallas_call(body, name="adamw_small", out_shape=[shp] * 4,
                          compiler_params=pltpu.CompilerParams(vmem_limit_bytes=VMEM_LIMIT))(w, m, v, parts)


def _vec(a):
    return a.reshape(1, -1)


W_IN_SHARD = IN_COLS // N_DEV
W_IN_ROWS = 272


def _w_in_travel(a):
    pad = [(0, 0)] * (a.ndim - 2) + [(0, W_IN_ROWS - W_IN_SHARD), (0, 0)]
    return jnp.pad(jnp.swapaxes(a, -1, -2), pad)


def _unpack_w_in(g):
    full = jnp.transpose(g[:, :W_IN_SHARD, :], (2, 0, 1)).reshape(D_MODEL, IN_COLS)
    qkv = full[:, :QKV_COLS]
    f = full[:, QKV_COLS:QKV_COLS + FOX_HEADS]
    u = full[:, QKV_COLS + FOX_HEADS:]
    uf = jnp.concatenate([u, f, jnp.zeros((D_MODEL, UF_COLS - POOL_WIDTH - FOX_HEADS), g.dtype)], axis=1)
    return jnp.concatenate([qkv, uf], axis=1)


def _pack_dw_in(dwp):
    qkv = dwp[:, :QKV_COLS]
    u = dwp[:, QKV_COLS:QKV_COLS + POOL_WIDTH]
    f = dwp[:, QKV_COLS + POOL_WIDTH:QKV_COLS + POOL_WIDTH + FOX_HEADS]
    full = jnp.concatenate([qkv, f, u], axis=1)
    return _w_in_travel(jnp.transpose(full.reshape(D_MODEL, N_DEV, W_IN_SHARD), (1, 0, 2)))


REST = ['w_out', 'wq_x', 'wkv_x', 'wo_x', 'w_up', 'w_down']


def _layer_fwd(x0, h1, mem, sp, g_in, shards, g_next):
    S = x0.shape[0]
    sv = {"x0": x0}
    w_inp = _unpack_w_in(g_in)
    qkv, uf = _mm_rows("mm_in", [(h1, w_inp, "nn")],
                       [(BF16, 0, QKV_COLS, "id"), (F32, QKV_COLS, UF_COLS, "id")], piece=UF_COLS)
    c = _gate_fwd(uf, sp["b_forget"])
    cT = jnp.transpose(c[:, :FOX_HEADS]).reshape(FOX_HEADS, 1, S)
    o, ob, lse, *got = _fox_fwd(qkv, cT, shards)
    g_out, g_q, g_kv, g_o, g_up, g_down = got[:6]
    W = dict(inp=w_inp, out=g_out.reshape(D_MODEL, D_MODEL), q=g_q.reshape(D_MODEL, D_MODEL), kv=g_kv,
             o=g_o.reshape(D_MODEL, D_MODEL), up=g_up, down=g_down.reshape(D_FF, D_MODEL))
    pool = _pool_fwd(uf, sp["pool_w"], sp["pool_scale"])
    cat = jnp.concatenate([ob, pool], axis=1)
    mix, x1, h2 = _mm_resid_norm("mm_sq_norm", cat, W["out"], x0, sp["g_mix_post"], sp["g_x_pre"])
    mn = _norm_fwd("norm_mem", mem, sp["g_mem"])
    q2 = _mm1("mm_q", h2, W["q"], "nn", D_MODEL, BF16)
    kv = _mm1("mm_kv", mn, W["kv"], "nn3", 2 * D_MODEL, BF16, piece=2 * D_MODEL // N_DEV)
    o2 = _xattn_fwd(q2, kv)
    xo, x2, h3 = _mm_resid_norm("mm_sq_norm", o2, W["o"], x1, sp["g_x_post"], sp["g_ffn_pre"])
    up, act = _mm_rows("mm_up", [(h3, W["up"], "nn3")], [(BF16, 0, D_FF, "id"), (BF16, 0, D_FF, "relu2")],
                       piece=D_FF // N_DEV)
    y, x3, h_next = _mm_resid_norm("mm_down_norm" if g_next is not None else "mm_down_norm_last", act, W["down"], x2,
                                   sp["g_ffn_post"], g_next)
    sv.update(h1=h1, uf=uf, cT=cT, qkv=qkv, o=o, lse=lse, cat=cat, mix=mix, x1=x1, h2=h2, mn=mn, q2=q2, kv=kv,
              o2=o2, xo=xo, x2=x2, h3=h3, up=up, act=act, y=y)
    return x3, h_next, sv, W, (got[6] if len(got) > 6 else None)


def _layer_bwd(dx3, dy, mem, sv, sp, W, carried, below):
    S = dx3.shape[0]
    gs = {}
    gb = {}
    (dup,) = _mm_rows("mm_dup", [(dy, W["down"], "nt")], [(BF16, 0, D_FF, "drelu2")], extra=sv["up"])
    gb["w_down"] = _mm_tn("mm_dw_down", sv["act"], dy, BF16).reshape(N_DEV, D_FF // N_DEV, D_MODEL)
    gb["w_up"] = _mm_tn("mm_dw_up", sv["h3"], dup, BF16, shard_cols=D_FF // N_DEV)
    dx2, gs["g_ffn_pre"], dxo, gs["g_x_post"] = _norm_bwd(
        "mm_dh3_norm_bwd", (dup, W["up"], "nt3"), sv["x2"], sp["g_ffn_pre"], dx3, F32,
        below=(sv["xo"], sp["g_x_post"]))
    do2 = _mm1("mm_sq_t", dxo, W["o"], "nt", D_MODEL, BF16)
    gb["wo_x"] = _mm_tn("mm_dw_sq", sv["o2"], dxo, BF16).reshape(N_DEV, D_MODEL // N_DEV, D_MODEL)
    dq2, dkvb = _xattn_bwd(sv["q2"], sv["kv"], do2)
    gb["wq_x"] = _mm_tn("mm_dw_sq", sv["h2"], dq2, BF16).reshape(N_DEV, D_MODEL // N_DEV, D_MODEL)
    gb["wkv_x"] = _mm_tn("mm_dw_kv", sv["mn"], dkvb, BF16, shard_cols=2 * D_MODEL // N_DEV)
    dmn = _mm1("mm_dmn", dkvb, W["kv"], "nt3", D_MODEL, F32)
    _, gs["g_mem"] = _norm_bwd("norm_bwd_mem", dmn, mem, sp["g_mem"], None, BF16)
    dx1, gs["g_x_pre"], dmix, gs["g_mix_post"] = _norm_bwd(
        "mm_dh2_norm_bwd", (dq2, W["q"], "nt"), sv["x1"], sp["g_x_pre"], dx2, F32,
        below=(sv["mix"], sp["g_mix_post"]))
    doh, dpool = _mm_rows("mm_dcat", [(dmix, W["out"], "nt")],
                          [(BF16, 0, FOX_WIDTH, "id"), (F32, FOX_WIDTH, POOL_WIDTH, "id")])
    gb["w_out"] = _mm_tn("mm_dw_sq", sv["cat"], dmix, BF16).reshape(N_DEV, D_MODEL // N_DEV, D_MODEL)
    du, gs["pool_w"], gs["pool_scale"] = _pool_bwd(sv["uf"], dpool, sp["pool_w"], sp["pool_scale"])
    dq, dk, dv, dcT, *got = _fox_bwd(sv["qkv"], sv["cT"], sv["o"], sv["lse"], doh, [gb[n] for n in REST] + carried)
    dc = jnp.pad(jnp.transpose(dcT.reshape(FOX_HEADS, S)), ((0, 0), (0, LANES - FOX_HEADS)))
    dfg, db = _gate_bwd(dc, sv["uf"], sp["b_forget"])
    gs["b_forget"] = db[:, :FOX_HEADS]
    dproj = [dq, dk, dv, du, dfg]
    dwp = _mm_tn("mm_dw_in", sv["h1"], dproj, BF16, piece=UF_COLS)
    dh1 = (dproj, W["inp"], "nt")
    if below is None:
        dx0, gs["g_mix_pre"] = _norm_bwd("mm_dh1_norm_bwd_first", dh1, sv["x0"], sp["g_mix_pre"], dx1, F32)
        lower = None
    else:
        dx0, gs["g_mix_pre"], *lower = _norm_bwd("mm_dh1_norm_bwd", dh1, sv["x0"], sp["g_mix_pre"], dx1, F32,
                                                 below=below)
    return dx0, lower, dict(zip(REST, got[:6])), got[6:], _pack_dw_in(dwp), gs


def _small_rows(shape):
    return -(-math.prod(shape) // (8 * LANES)) * 8


def _pack_small(d):
    blocks = []
    for n in SMALL:
        rows = _small_rows(d[n].shape)
        if d[n].shape[-1] == LANES:
            blocks.append(d[n].reshape(rows, LANES))
        else:
            flat = d[n].reshape(-1)
            blocks.append(jnp.pad(flat, (0, rows * LANES - flat.shape[0])).reshape(rows, LANES))
    return jnp.concatenate(blocks, axis=0)


def _unpack_small(packed, like):
    out = {}
    row = 0
    for n in SMALL:
        shape = like[n].shape
        rows = _small_rows(shape)
        block = packed[row:row + rows]
        out[n] = block.reshape(shape) if shape[-1] == LANES else block.reshape(-1)[:math.prod(shape)].reshape(shape)
        row += rows
    return out


def kernel(x, mem, g_mix_pre, w_in, b_forget, pool_w, pool_scale, w_out, g_mix_post, g_x_pre, g_mem, wq_x, wkv_x, wo_x, g_x_post, g_ffn_pre, w_up, w_down, g_ffn_post, loss_target, m_g_mix_pre, m_w_in, m_b_forget, m_pool_w, m_pool_scale, m_w_out, m_g_mix_post, m_g_x_pre, m_g_mem, m_wq_x, m_wkv_x, m_wo_x, m_g_x_post, m_g_ffn_pre, m_w_up, m_w_down, m_g_ffn_post, v_g_mix_pre, v_w_in, v_b_forget, v_pool_w, v_pool_scale, v_w_out, v_g_mix_post, v_g_x_pre, v_g_mem, v_wq_x, v_wkv_x, v_wo_x, v_g_x_post, v_g_ffn_pre, v_w_up, v_w_down, v_g_ffn_post):
    w = dict(g_mix_pre=g_mix_pre, w_in=w_in, b_forget=b_forget, pool_w=pool_w, pool_scale=pool_scale, w_out=w_out,
             g_mix_post=g_mix_post, g_x_pre=g_x_pre, g_mem=g_mem, wq_x=wq_x, wkv_x=wkv_x, wo_x=wo_x,
             g_x_post=g_x_post, g_ffn_pre=g_ffn_pre, w_up=w_up, w_down=w_down, g_ffn_post=g_ffn_post)
    mom = dict(g_mix_pre=m_g_mix_pre, w_in=m_w_in, b_forget=m_b_forget, pool_w=m_pool_w, pool_scale=m_pool_scale,
               w_out=m_w_out, g_mix_post=m_g_mix_post, g_x_pre=m_g_x_pre, g_mem=m_g_mem, wq_x=m_wq_x,
               wkv_x=m_wkv_x, wo_x=m_wo_x, g_x_post=m_g_x_post, g_ffn_pre=m_g_ffn_pre, w_up=m_w_up,
               w_down=m_w_down, g_ffn_post=m_g_ffn_post)
    var = dict(g_mix_pre=v_g_mix_pre, w_in=v_w_in, b_forget=v_b_forget, pool_w=v_pool_w, pool_scale=v_pool_scale,
               w_out=v_w_out, g_mix_post=v_g_mix_post, g_x_pre=v_g_x_pre, g_mem=v_g_mem, wq_x=v_wq_x,
               wkv_x=v_wkv_x, wo_x=v_wo_x, g_x_post=v_g_x_post, g_ffn_pre=v_g_ffn_pre, w_up=v_w_up,
               w_down=v_w_down, g_ffn_post=v_g_ffn_post)
    S = x.shape[1]
    xs = x.reshape(S, D_MODEL)
    mems = mem.reshape(MEM_LEN, D_MODEL)
    target = loss_target.reshape(S, D_MODEL)

    def small_params(l):
        return dict(
            g_mix_pre=_vec(g_mix_pre[l]), g_mix_post=_vec(g_mix_post[l]), g_x_pre=_vec(g_x_pre[l]),
            g_mem=_vec(g_mem[l]), g_x_post=_vec(g_x_post[l]), g_ffn_pre=_vec(g_ffn_pre[l]),
            g_ffn_post=_vec(g_ffn_post[l]), pool_scale=_vec(pool_scale[l]), pool_w=pool_w[l].astype(BF16),
            b_forget=jnp.pad(_vec(b_forget[l]), ((0, 0), (0, LANES - FOX_HEADS))))

    shard = {n: [w[n][l].astype(BF16) for l in range(DEPTH)] for n in REST}
    shard["w_in"] = [_w_in_travel(w_in[l].astype(BF16)) for l in range(DEPTH)]
    sps = [small_params(l) for l in range(DEPTH)]
    saved, weights = [], []
    h = xs
    (g_in,) = _exchange("gather_w_in", [shard["w_in"][0]])
    hn = _norm_fwd("norm_fwd", xs, sps[0]["g_mix_pre"])
    for l in range(DEPTH):
        travelling = [shard[n][l] for n in REST] + ([shard["w_in"][l + 1]] if l + 1 < DEPTH else [])
        g_next = sps[l + 1]["g_mix_pre"] if l + 1 < DEPTH else None
        h, hn, sv, W, g_in = _layer_fwd(h, hn, mems, sps[l], g_in, travelling, g_next)
        saved.append(sv)
        weights.append(W)
    dh, sq = _loss_fwd_bwd(h, target)
    loss = lax.psum(0.5 * sq[0, 0] / D_MODEL, ("x", "y", "c"))

    parts = [dict() for _ in range(DEPTH)]
    small_grads = [None] * DEPTH
    carried = []
    lower = _norm_bwd("norm_bwd_b", dh, saved[-1]["y"], sps[-1]["g_ffn_post"], None, BF16)
    for l in reversed(range(DEPTH)):
        dy, dg_ffn_post = lower
        below = (saved[l - 1]["y"], sps[l - 1]["g_ffn_post"]) if l > 0 else None
        dh, lower, got, got_carried, dw_in, gs = _layer_bwd(dh, dy, mems, saved[l], sps[l], weights[l], carried, below)
        gs["g_ffn_post"] = dg_ffn_post
        parts[l].update(got)
        if got_carried:
            parts[l + 1]["w_in"] = got_carried[0]
        carried = [dw_in]
        small_grads[l] = gs
    grad_x = dh.reshape(1, S, D_MODEL)

    grads, deltas, new_m, new_v = {}, {}, {}, {}
    rows = dict(w_in=128, w_out=128, wq_x=128, wkv_x=256, wo_x=128, w_up=256, w_down=128)
    sg = {n: jnp.stack([small_grads[l][n].reshape(w[n].shape[1:]) for l in range(DEPTH)]) for n in SMALL}
    riders = dict(w_down=carried, w_up=[_pack_small(sg)])
    for n in ["w_down", "w_up", "w_out", "wq_x", "wkv_x", "wo_x", "w_in"]:
        if n == "w_in":
            for l in range(DEPTH):
                parts[l]["w_in"] = jnp.swapaxes(parts[l]["w_in"][:, :W_IN_SHARD, :], 1, 2)
        grads[n], deltas[n], new_m[n], new_v[n], *got = _adamw_big(
            "adamw_" + n, w[n], mom[n], var[n], [parts[l][n] for l in range(DEPTH)], rows[n], riders.get(n, ()))
        if n == "w_down":
            (parts[0]["w_in"],) = got
        elif n == "w_up":
            (sg_parts,) = got
    outs = _adamw_small(_pack_small(w), _pack_small(mom), _pack_small(var), sg_parts)
    for d, packed in zip((grads, deltas, new_m, new_v), outs):
        d.update(_unpack_small(packed, w))

    return (loss, grad_x, *[grads[n] for n in W_NAMES], *[deltas[n] for n in W_NAMES],
            *[new_m[n] for n in W_NAMES], *[new_v[n] for n in W_NAMES])
```

```python
import math

import jax
import jax.numpy as jnp
from jax import lax
from jax.experimental import pallas as pl
from jax.experimental.pallas import tpu as pltpu

F32 = jnp.float32
BF16 = jnp.bfloat16

D_MODEL = 1024
DEPTH = 4
FOX_WIDTH = 512
FOX_HEADS = 8
FOX_HEAD_DIM = 64
POOL_WIDTH = 512
POOL_WINDOWS = (2, 4, 8, 16)
POOL_GROUP_DIM = 128
POOL_HALO = 16
MEM_LEN = 256
X_HEADS = 4
X_HEAD_DIM = 256
D_FF = 4096
EPS = 1e-6
IN_COLS = 2056
QKV_COLS = 3 * FOX_WIDTH
UF_COLS = 640
INP_COLS = QKV_COLS + UF_COLS
N_DEV = 8
LANES = 128

ADAM_LR = 0.001
ADAM_B1 = 0.9
ADAM_B2 = 0.999
ADAM_EPS = 1e-08
ADAM_WD = 0.01
ADAM_STEP = 10

VMEM_LIMIT = 56 * 1024 * 1024

W_NAMES = ['g_mix_pre', 'w_in', 'b_forget', 'pool_w', 'pool_scale', 'w_out', 'g_mix_post', 'g_x_pre', 'g_mem',
           'wq_x', 'wkv_x', 'wo_x', 'g_x_post', 'g_ffn_pre', 'w_up', 'w_down', 'g_ffn_post']
BIG = ['w_in', 'w_out', 'wq_x', 'wkv_x', 'wo_x', 'w_up', 'w_down']
SMALL = [n for n in W_NAMES if n not in BIG]

NN = (((1,), (0,)), ((), ()))
NT = (((1,), (1,)), ((), ()))
TN = (((0,), (0,)), ((), ()))


def _params(*sem):
    return pltpu.CompilerParams(dimension_semantics=sem, vmem_limit_bytes=VMEM_LIMIT)


def _row_tile(s):
    return min(s, 512)


def _product(a_ref, w_ref, kind, c0, pw):
    cols = slice(c0, c0 + pw)
    if kind == "nn":
        return lax.dot_general(a_ref[...], w_ref[:, cols], NN, preferred_element_type=F32)
    if kind == "nt":
        return lax.dot_general(a_ref[...], w_ref[cols, :], NT, preferred_element_type=F32)
    n = w_ref.shape[2]
    if kind == "nn3":
        assert pw == n and c0 % n == 0
        return lax.dot_general(a_ref[...], w_ref[c0 // n], NN, preferred_element_type=F32)
    r = None
    for j in range(w_ref.shape[0]):
        part = lax.dot_general(a_ref[:, j * n:(j + 1) * n], w_ref[j, cols, :], NT, preferred_element_type=F32)
        r = part if r is None else r + part
    return r


def _resident(w):
    return pl.BlockSpec(w.shape, lambda i, nd=w.ndim: (0,) * nd)


def _mm_rows(name, terms, outs, extra=None, piece=1024):
    M = terms[0][0].shape[0]
    tm = _row_tile(M)
    nterm = len(terms)
    n_extra = 0 if extra is None else 1
    groups = {}
    for idx, (_, c0, width, fn) in enumerate(outs):
        groups.setdefault((c0, width), []).append((idx, fn))

    def body(*refs):
        a_refs = refs[0:2 * nterm:2]
        w_refs = refs[1:2 * nterm:2]
        extra_refs = refs[2 * nterm:2 * nterm + n_extra]
        out_refs = refs[2 * nterm + n_extra:]
        for (g0, gw), members in groups.items():
            for c0 in range(g0, g0 + gw, piece):
                pw = min(piece, g0 + gw - c0)
                r = None
                for a_ref, w_ref, (_, w, kind) in zip(a_refs, w_refs, terms):
                    part = _product(a_ref, w_ref, kind, c0, pw)
                    r = part if r is None else r + part
                dst = slice(c0 - g0, c0 - g0 + pw)
                for idx, fn in members:
                    if fn == "relu2":
                        rp = jnp.maximum(r, 0.0)
                        val = rp * rp
                    elif fn == "drelu2":
                        val = r * (2.0 * jnp.maximum(extra_refs[0][:, dst].astype(F32), 0.0))
                    else:
                        val = r
                    out_refs[idx][:, dst] = val.astype(out_refs[idx].dtype)

    in_specs, ins = [], []
    for a, w, _ in terms:
        in_specs.append(pl.BlockSpec((tm, a.shape[1]), lambda i: (i, 0)))
        in_specs.append(pl.BlockSpec(w.shape, lambda i, nd=w.ndim: (0,) * nd))
        ins += [a, w]
    if extra is not None:
        in_specs.append(pl.BlockSpec((tm, extra.shape[1]), lambda i: (i, 0)))
        ins.append(extra)
    res = pl.pallas_call(
        body, name=name, grid=(M // tm,), in_specs=in_specs,
        out_specs=[pl.BlockSpec((tm, width), lambda i: (i, 0)) for _, _, width, _ in outs],
        out_shape=[jax.ShapeDtypeStruct((M, width), dt) for dt, _, width, _ in outs],
        compiler_params=_params("parallel"))(*ins)
    return res


def _mm1(name, a, w, kind, n_cols, dtype, piece=1024):
    return _mm_rows(name, [(a, w, kind)], [(dtype, 0, n_cols, "id")], piece=piece)[0]


def _mm_tn(name, a, b, out_dtype, shard_cols=None, piece=512):
    K, M = a.shape
    b_parts = b if isinstance(b, list) else [b]
    nb = len(b_parts)
    N = sum(p.shape[1] for p in b_parts)
    tk = _row_tile(K)
    nk = K // tk
    piece = shard_cols or min(piece, N)

    def body(a_ref, *rest):
        b_refs = rest[:nb]
        o_ref, acc = rest[nb:]
        k = pl.program_id(0)

        @pl.when(k == 0)
        def _():
            acc[...] = jnp.zeros_like(acc)

        a_t = jnp.transpose(a_ref[...])
        if nb == 1:
            for c0 in range(0, N, piece):
                cols = slice(c0, min(c0 + piece, N))
                acc[:, cols] += lax.dot_general(a_t, b_refs[0][:, cols], NN, preferred_element_type=F32)
        else:
            c0 = 0
            for b_ref in b_refs:
                cols = slice(c0, c0 + b_ref.shape[1])
                acc[:, cols] += lax.dot_general(a_t, b_ref[...], NN, preferred_element_type=F32)
                c0 += b_ref.shape[1]

        @pl.when(k == nk - 1)
        def _():
            for c0 in range(0, N, piece):
                cols = slice(c0, min(c0 + piece, N))
                if shard_cols:
                    o_ref[c0 // piece] = acc[:, cols].astype(o_ref.dtype)
                else:
                    o_ref[:, cols] = acc[:, cols].astype(o_ref.dtype)

    out_dims = (N // shard_cols, M, shard_cols) if shard_cols else (M, N)
    return pl.pallas_call(
        body, name=name, grid=(nk,),
        in_specs=[pl.BlockSpec((tk, M), lambda k: (k, 0))]
        + [pl.BlockSpec((tk, p.shape[1]), lambda k: (k, 0)) for p in b_parts],
        out_specs=pl.BlockSpec(out_dims, lambda k, nd=len(out_dims): (0,) * nd),
        out_shape=jax.ShapeDtypeStruct(out_dims, out_dtype),
        scratch_shapes=[pltpu.VMEM((M, N), F32)],
        compiler_params=_params("arbitrary"))(a, *b_parts)


def _norm_fwd(name, x, g):
    S, Dm = x.shape
    ts = _row_tile(S)

    def body(x_ref, g_ref, h_ref):
        xv = x_ref[...]
        r = lax.rsqrt(jnp.mean(xv * xv, axis=-1, keepdims=True) + EPS)
        h_ref[...] = ((xv * r) * g_ref[...]).astype(BF16)

    return pl.pallas_call(
        body, name=name, grid=(S // ts,),
        in_specs=[pl.BlockSpec((ts, Dm), lambda i: (i, 0)), pl.BlockSpec((1, Dm), lambda i: (0, 0))],
        out_specs=pl.BlockSpec((ts, Dm), lambda i: (i, 0)),
        out_shape=jax.ShapeDtypeStruct((S, Dm), BF16), compiler_params=_params("parallel"))(x, g)


def _mm_resid_norm(name, a, w, x, g, g_next):
    S, Dm = x.shape
    ts = _row_tile(S)
    has_next = g_next is not None

    def body(a_ref, w_ref, x_ref, g_ref, *rest):
        fv = _product(a_ref, w_ref, "nn", 0, Dm)
        r = lax.rsqrt(jnp.mean(fv * fv, axis=-1, keepdims=True) + EPS)
        xn = x_ref[...] + (fv * r) * g_ref[...]
        if has_next:
            gn_ref, f_ref, o_ref, h_ref = rest
            rn = lax.rsqrt(jnp.mean(xn * xn, axis=-1, keepdims=True) + EPS)
            h_ref[...] = ((xn * rn) * gn_ref[...]).astype(BF16)
        else:
            f_ref, o_ref = rest
        f_ref[...] = fv
        o_ref[...] = xn

    row = pl.BlockSpec((ts, Dm), lambda i: (i, 0))
    vec = pl.BlockSpec((1, Dm), lambda i: (0, 0))
    ins = [a, w, x, g] + ([g_next] if has_next else [])
    f32_rows = jax.ShapeDtypeStruct((S, Dm), F32)
    res = pl.pallas_call(
        body, name=name, grid=(S // ts,),
        in_specs=[pl.BlockSpec((ts, a.shape[1]), lambda i: (i, 0)), _resident(w), row, vec] + ([vec] if has_next else []),
        out_specs=[row, row] + ([row] if has_next else []),
        out_shape=[f32_rows, f32_rows] + ([jax.ShapeDtypeStruct((S, Dm), BF16)] if has_next else []),
        compiler_params=_params("parallel"))(*ins)
    return (res[0], res[1], res[2]) if has_next else (res[0], res[1], None)


def _rms_bwd(dov, yv, g):
    r = lax.rsqrt(jnp.mean(yv * yv, axis=-1, keepdims=True) + EPS)
    z = dov * g
    yr = yv * r
    return r * (z - yr * jnp.mean(yr * z, axis=-1, keepdims=True)), jnp.sum(dov * yr, axis=0, keepdims=True)


def _norm_bwd(name, dout, y, g, resid, out_dtype, below=None):
    S, Dm = y.shape
    ts = _row_tile(S)
    has_resid = resid is not None
    chained = below is not None
    produced = isinstance(dout, tuple)
    kind = dout[2] if produced else None
    a_parts = (dout[0] if isinstance(dout[0], list) else [dout[0]]) if produced else []
    n_a = len(a_parts)

    def body(*refs):
        refs = list(refs)
        if produced and n_a == 1:
            dov = _product(refs[0], refs[1], kind, 0, Dm)
            refs = refs[1:]
        elif produced:
            w_ref = refs[n_a]
            dov, k0 = None, 0
            for a_ref in refs[:n_a]:
                k1 = k0 + a_ref.shape[1]
                part = lax.dot_general(a_ref[...], w_ref[:, k0:k1], NT, preferred_element_type=F32)
                dov = part if dov is None else dov + part
                k0 = k1
            refs = refs[n_a:]
        else:
            dov = refs[0][...]
        y_ref, g_ref = refs[1:3]
        pos = 3
        r_ref = refs[pos] if has_resid else None
        pos += has_resid
        if chained:
            f_ref, gf_ref = refs[pos:pos + 2]
            pos += 2
        dy_ref, dg_ref = refs[pos:pos + 2]
        i = pl.program_id(0)
        dy, dg = _rms_bwd(dov, y_ref[...], g_ref[...])
        if has_resid:
            dy = dy + r_ref[...]
        dy_ref[...] = dy.astype(out_dtype)

        @pl.when(i == 0)
        def _():
            for ref in refs[pos + 1::2]:
                ref[...] = jnp.zeros_like(ref)

        dg_ref[...] += dg
        if chained:
            df_ref, dgf_ref = refs[pos + 2:pos + 4]
            df, dgf = _rms_bwd(dy, f_ref[...], gf_ref[...])
            df_ref[...] = df.astype(BF16)
            dgf_ref[...] += dgf

    row = pl.BlockSpec((ts, Dm), lambda i: (i, 0))
    vec = pl.BlockSpec((1, Dm), lambda i: (0, 0))
    if produced:
        assert n_a == 1 or kind == "nt"
        ins = a_parts + [dout[1]]
        specs = [pl.BlockSpec((ts, a.shape[1]), lambda i: (i, 0)) for a in a_parts] + [_resident(dout[1])]
    else:
        ins = [dout]
        specs = [row]
    ins += [y, g] + ([resid] if has_resid else []) + (list(below) if chained else [])
    specs += [row, vec] + ([row] if has_resid else []) + ([row, vec] if chained else [])
    vec_shape = jax.ShapeDtypeStruct((1, Dm), F32)
    return pl.pallas_call(
        body, name=name, grid=(S // ts,), in_specs=specs, out_specs=[row, vec] + ([row, vec] if chained else []),
        out_shape=[jax.ShapeDtypeStruct((S, Dm), out_dtype), vec_shape]
        + ([jax.ShapeDtypeStruct((S, Dm), BF16), vec_shape] if chained else []),
        compiler_params=_params("arbitrary"))(*ins)


def _loss_fwd_bwd(y, t):
    S, Dm = y.shape
    ts = _row_tile(S)

    def body(y_ref, t_ref, dy_ref, acc_ref):
        i = pl.program_id(0)
        e = y_ref[...] - t_ref[...]
        dy_ref[...] = e * (1.0 / Dm)

        @pl.when(i == 0)
        def _():
            acc_ref[...] = jnp.zeros_like(acc_ref)

        s = jnp.sum(jnp.sum(e * e, axis=1, keepdims=True), axis=0, keepdims=True)
        acc_ref[...] += s

    row = pl.BlockSpec((ts, Dm), lambda i: (i, 0))
    return pl.pallas_call(
        body, name="loss", grid=(S // ts,), in_specs=[row, row],
        out_specs=[row, pl.BlockSpec((8, LANES), lambda i: (0, 0))],
        out_shape=[jax.ShapeDtypeStruct((S, Dm), F32), jax.ShapeDtypeStruct((8, LANES), F32)],
        compiler_params=_params("arbitrary"))(y, t)


def _log_sigmoid(x):
    return jnp.minimum(x, 0.0) - jnp.log(1.0 + jnp.exp(-jnp.abs(x)))


def _gate_fwd(uf, bpad):
    S = uf.shape[0]
    T = _row_tile(S)

    def body(f_ref, b_ref, c_ref, carry):
        i = pl.program_id(0)

        @pl.when(i == 0)
        def _():
            carry[...] = jnp.zeros_like(carry)

        lf = _log_sigmoid(f_ref[...] + b_ref[...])
        r = lax.broadcasted_iota(jnp.int32, (T, T), 0)
        cidx = lax.broadcasted_iota(jnp.int32, (T, T), 1)
        tri = (cidx <= r).astype(F32)
        c = lax.dot_general(tri, lf, NN, precision=lax.Precision.HIGHEST, preferred_element_type=F32)
        c_ref[...] = c + carry[0:1, :]
        carry[...] = carry[...] + jnp.sum(lf, axis=0, keepdims=True)

    return pl.pallas_call(
        body, name="gate_fwd", grid=(S // T,),
        in_specs=[pl.BlockSpec((T, LANES), lambda i: (i, 4)), pl.BlockSpec((1, LANES), lambda i: (0, 0))],
        out_specs=pl.BlockSpec((T, LANES), lambda i: (i, 0)),
        out_shape=jax.ShapeDtypeStruct((S, LANES), F32),
        scratch_shapes=[pltpu.VMEM((8, LANES), F32)], compiler_params=_params("arbitrary"))(uf, bpad)


def _gate_bwd(dc, uf, bpad):
    S = uf.shape[0]
    T = _row_tile(S)
    nb = S // T

    def body(dc_ref, f_ref, b_ref, df_ref, db_ref, carry):
        i = pl.program_id(0)

        @pl.when(i == 0)
        def _():
            carry[...] = jnp.zeros_like(carry)
            db_ref[...] = jnp.zeros_like(db_ref)

        dcv = dc_ref[...]
        r = lax.broadcasted_iota(jnp.int32, (T, T), 0)
        cidx = lax.broadcasted_iota(jnp.int32, (T, T), 1)
        tri = (cidx >= r).astype(F32)
        dlf = lax.dot_general(tri, dcv, NN, precision=lax.Precision.HIGHEST, preferred_element_type=F32)
        dlf = dlf + carry[0:1, :]
        carry[...] = carry[...] + jnp.sum(dcv, axis=0, keepdims=True)
        fg = f_ref[...] + b_ref[...]
        dfg = dlf / (1.0 + jnp.exp(fg))
        df_ref[...] = dfg.astype(BF16)
        db_ref[...] += jnp.sum(dfg, axis=0, keepdims=True)

    return pl.pallas_call(
        body, name="gate_bwd", grid=(nb,),
        in_specs=[pl.BlockSpec((T, LANES), lambda i: (nb - 1 - i, 0)),
                  pl.BlockSpec((T, LANES), lambda i: (nb - 1 - i, 4)),
                  pl.BlockSpec((1, LANES), lambda i: (0, 0))],
        out_specs=[pl.BlockSpec((T, LANES), lambda i: (nb - 1 - i, 0)), pl.BlockSpec((1, LANES), lambda i: (0, 0))],
        out_shape=[jax.ShapeDtypeStruct((S, LANES), BF16), jax.ShapeDtypeStruct((1, LANES), F32)],
        scratch_shapes=[pltpu.VMEM((8, LANES), F32)], compiler_params=_params("arbitrary"))(dc, uf, bpad)


FOX_CHUNK = 64
FOX_CHUNK_BWD = 32
HEAD_PAIRS = FOX_HEADS // 2
PAIR = 2


def _masked(s, row0, col0, diagonal):
    if diagonal:
        row = row0 + lax.broadcasted_iota(jnp.int32, s.shape, 0)
        col = col0 + lax.broadcasted_iota(jnp.int32, s.shape, 1)
        s = jnp.where(col <= row, s, -jnp.inf)
    return s


def _causal_pairs(n, query_major):
    if query_major:
        pairs = [(q, k) for q in range(n) for k in range(q + 1)]
    else:
        pairs = [(q, k) for k in range(n) for q in range(k, n)]
    return (jnp.asarray([p[0] for p in pairs], jnp.int32), jnp.asarray([p[1] for p in pairs], jnp.int32))


def _lane_block(b):
    return slice(b * LANES, (b + 1) * LANES)


def _fold(op, xs):
    acc = xs[0]
    for x in xs[1:]:
        acc = op(acc, x)
    return acc


def _head_lanes(hh):
    lane = lax.broadcasted_iota(jnp.int32, (1, LANES), 1)
    return (lane < FOX_HEAD_DIM) if hh == 0 else (lane >= FOX_HEAD_DIM)


def _pick(first_head, a, b):
    return jnp.where(first_head, a, b)


def _fox_fwd(qkv, cT, comm):
    S = qkv.shape[0]
    t = _row_tile(S)
    n = S // t
    nc = len(comm)
    scale = 1.0 / math.sqrt(FOX_HEAD_DIM)
    chunk = min(FOX_CHUNK, t)
    per_head = 4
    q_tab, k_tab = _causal_pairs(n, True)
    steps = q_tab.shape[0]

    def body(qt_ref, kt_ref, q_ref, k_ref, v_ref, c_ref, *rest):
        comm_in = rest[:nc]
        o_ref, ob_ref, lse_ref = rest[nc:nc + 3]
        comm_out = rest[nc + 3:2 * nc + 3]
        scr = rest[2 * nc + 3:2 * nc + 3 + PAIR * per_head]
        sems = rest[2 * nc + 3 + PAIR * per_head:]
        hp = pl.program_id(0)
        step_id = pl.program_id(1)
        qi = qt_ref[step_id]
        ki = kt_ref[step_id]

        if nc:
            @pl.when((hp == 0) & (step_id == 0))
            def _():
                _Gather(comm_in, comm_out, *sems).start()

            @pl.when((hp == HEAD_PAIRS - 1) & (step_id == 0))
            def _():
                _Gather(comm_in, comm_out, *sems).pass_on()

        @pl.when(ki == 0)
        def _():
            for hh in range(PAIR):
                m_s, l_s, a_s, acc_s = scr[hh * per_head:hh * per_head + 4]
                m_s[...] = jnp.full_like(m_s, -jnp.inf)
                l_s[...] = jnp.zeros_like(l_s)
                acc_s[...] = jnp.zeros_like(acc_s)

        def step(diagonal):
            q2 = q_ref[...] * scale
            k2 = k_ref[...]
            v2 = v_ref[...]
            scores = []
            for hh in range(PAIR):
                qm = jnp.where(_head_lanes(hh), q2, jnp.zeros_like(q2))
                scores.append(lax.dot_general(qm, k2, NT, preferred_element_type=F32))
            for hh in range(PAIR):
                m_s, l_s, a_s, acc_s = scr[hh * per_head:(hh + 1) * per_head]
                s_s = scores[hh]
                hi_rows, lo_rows = [], []
                for r in range(t // chunk):
                    rows = slice(r * chunk, (r + 1) * chunk)
                    blocks = [_masked(s_s[rows, _lane_block(b)] - c_ref[hh, :, _lane_block(b)], r * chunk,
                                      b * LANES, diagonal) for b in range(t // LANES)]
                    m_prev = m_s[rows, :]
                    m_new = jnp.maximum(m_prev, jnp.max(_fold(jnp.maximum, blocks), axis=1, keepdims=True))
                    alpha = jnp.exp(m_prev - m_new)
                    ps = [jnp.exp(blk - m_new) for blk in blocks]
                    l_s[rows, :] = alpha * l_s[rows, :] + jnp.sum(_fold(jnp.add, ps), axis=1, keepdims=True)
                    m_s[rows, :] = m_new
                    a_s[rows, :] = alpha
                    his = [p.astype(BF16) for p in ps]
                    hi_rows.append(jnp.concatenate(his, axis=1))
                    lo_rows.append(jnp.concatenate([(p - h.astype(F32)).astype(BF16) for p, h in zip(ps, his)],
                                                   axis=1))
                pv = (lax.dot_general(jnp.concatenate(hi_rows, axis=0), v2, NN, preferred_element_type=F32)
                      + lax.dot_general(jnp.concatenate(lo_rows, axis=0), v2, NN, preferred_element_type=F32))
                acc_s[...] = a_s[...] * acc_s[...] + pv

        @pl.when(ki < qi)
        def _():
            step(False)

        @pl.when(ki == qi)
        def _():
            step(True)
            heads = []
            for hh in range(PAIR):
                m_s, l_s, a_s, acc_s = scr[hh * per_head:hh * per_head + 4]
                heads.append(acc_s[...] / l_s[...])
                lse_ref[hh] = m_s[...] + jnp.log(l_s[...])
            o2 = _pick(_head_lanes(0), heads[0], heads[1])
            o_ref[...] = o2
            ob_ref[...] = o2.astype(BF16)

        if nc:
            @pl.when((hp == HEAD_PAIRS - 1) & (step_id == steps - 1))
            def _():
                _Gather(comm_in, comm_out, *sems).finish()

    def q_cols(first_block):
        return pl.BlockSpec((t, LANES), lambda h, s, qt, kt: (qt[s], first_block + h))

    def k_cols(first_block):
        return pl.BlockSpec((t, LANES), lambda h, s, qt, kt: (kt[s], first_block + h))

    any_spec = pl.BlockSpec(memory_space=pl.ANY)
    head_scratch = [pltpu.VMEM((t, LANES), F32)] * per_head
    grid_spec = pltpu.PrefetchScalarGridSpec(
        num_scalar_prefetch=2, grid=(HEAD_PAIRS, steps),
        in_specs=[q_cols(0), k_cols(HEAD_PAIRS), k_cols(2 * HEAD_PAIRS),
                  pl.BlockSpec((PAIR, 1, t), lambda h, s, qt, kt: (h, 0, kt[s]))] + [any_spec] * nc,
        out_specs=[q_cols(0), q_cols(0),
                   pl.BlockSpec((PAIR, t, LANES), lambda h, s, qt, kt: (h, qt[s], 0))] + [any_spec] * nc,
        scratch_shapes=head_scratch * PAIR + _comm_scratch(nc))
    return pl.pallas_call(
        body, name="fox_fwd", grid_spec=grid_spec,
        out_shape=[jax.ShapeDtypeStruct((S, FOX_WIDTH), F32), jax.ShapeDtypeStruct((S, FOX_WIDTH), BF16),
                   jax.ShapeDtypeStruct((FOX_HEADS, S, LANES), F32)] + _comm_shapes(comm),
        compiler_params=_params("arbitrary", "arbitrary"))(q_tab, k_tab, qkv, qkv, qkv, cT, *comm)


def _fox_bwd(qkv, cT, o, lse, do, comm):
    S = qkv.shape[0]
    t = _row_tile(S)
    n = S // t
    nc = len(comm)
    scale = 1.0 / math.sqrt(FOX_HEAD_DIM)
    chunk = min(FOX_CHUNK_BWD, t)
    per_head = 2
    q_tab, k_tab = _causal_pairs(n, False)
    steps = q_tab.shape[0]

    def body(qt_ref, kt_ref, q_ref, k_ref, v_ref, c_ref, o_ref, do_ref, lse_ref, *rest):
        comm_in = rest[:nc]
        dq_ref, dk_ref, dv_ref, dc_ref = rest[nc:nc + 4]
        comm_out = rest[nc + 4:2 * nc + 4]
        dq_s, dk_s, dv_s = rest[2 * nc + 4:2 * nc + 7]
        scr = rest[2 * nc + 7:2 * nc + 7 + PAIR * per_head]
        sems = rest[2 * nc + 7 + PAIR * per_head:]
        hp = pl.program_id(0)
        step_id = pl.program_id(1)
        qi = qt_ref[step_id]
        ki = kt_ref[step_id]

        if nc:
            @pl.when((hp == 0) & (step_id == 0))
            def _():
                for cp in _comm_copies(comm_in, comm_out, *sems):
                    cp.start()

        @pl.when(step_id == 0)
        def _():
            dq_s[...] = jnp.zeros_like(dq_s)

        @pl.when(qi == ki)
        def _():
            dk_s[...] = jnp.zeros_like(dk_s)
            dv_s[...] = jnp.zeros_like(dv_s)
            for hh in range(PAIR):
                dc_s = scr[hh * per_head]
                dc_s[...] = jnp.zeros_like(dc_s)

        def step(diagonal):
            q2 = q_ref[...]
            k2 = k_ref[...]
            v2 = v_ref[...]
            do2 = do_ref[...]
            prod = do2.astype(F32) * o_ref[...]
            grads = []
            for hh in range(PAIR):
                dc_s, delta_s = scr[hh * per_head:(hh + 1) * per_head]
                mine = _head_lanes(hh)
                s_s = lax.dot_general(jnp.where(mine, q2 * scale, jnp.zeros_like(q2)), k2, NT,
                                      preferred_element_type=F32)
                dp_s = lax.dot_general(jnp.where(mine, do2, jnp.zeros_like(do2)), v2, NT, preferred_element_type=F32)
                delta_s[...] = jnp.broadcast_to(jnp.sum(jnp.where(mine, prod, 0.0), axis=1, keepdims=True),
                                                (t, LANES))
                dc8 = [jnp.zeros((8, LANES), F32) for _ in range(t // LANES)]
                p_rows, ds_rows = [], []
                for r in range(t // chunk):
                    rows = slice(r * chunk, (r + 1) * chunk)
                    lse = lse_ref[hh, rows, :]
                    delta = delta_s[rows, :]
                    p_blocks, ds_blocks = [], []
                    for b in range(t // LANES):
                        s = _masked(s_s[rows, _lane_block(b)] - c_ref[hh, :, _lane_block(b)], r * chunk, b * LANES,
                                    diagonal)
                        p = jnp.exp(s - lse)
                        ds = p * (dp_s[rows, _lane_block(b)] - delta)
                        p_blocks.append(p.astype(BF16))
                        ds_blocks.append(ds.astype(BF16))
                        dc8[b] = dc8[b] + jnp.sum(ds.reshape(chunk // 8, 8, LANES), axis=0)
                    p_rows.append(jnp.concatenate(p_blocks, axis=1))
                    ds_rows.append(jnp.concatenate(ds_blocks, axis=1))
                for b in range(t // LANES):
                    dc_s[:, _lane_block(b)] += jnp.sum(dc8[b], axis=0, keepdims=True)
                dsb = jnp.concatenate(ds_rows, axis=0)
                grads.append((lax.dot_general(jnp.concatenate(p_rows, axis=0), do2, TN, preferred_element_type=F32),
                              lax.dot_general(dsb, k2, NN, preferred_element_type=F32),
                              lax.dot_general(dsb, q2, TN, preferred_element_type=F32)))
            first = _head_lanes(0)
            dv_s[...] += _pick(first, grads[0][0], grads[1][0])
            q_rows = pl.ds(pl.multiple_of(qi * t, t), t)
            dq_s[q_rows, :] += _pick(first, grads[0][1], grads[1][1]) * scale
            dk_s[...] += _pick(first, grads[0][2], grads[1][2]) * scale

        @pl.when(qi > ki)
        def _():
            step(False)

        @pl.when(qi == ki)
        def _():
            step(True)

        @pl.when(qi == n - 1)
        def _():
            dk_ref[...] = dk_s[...].astype(BF16)
            dv_ref[...] = dv_s[...].astype(BF16)
            for hh in range(PAIR):
                dc_ref[hh] = -scr[hh * per_head][...]

        @pl.when(step_id == steps - 1)
        def _():
            dq_ref[...] = dq_s[...].astype(BF16)

        if nc:
            @pl.when((hp == HEAD_PAIRS - 1) & (step_id == steps - 1))
            def _():
                for cp in _comm_copies(comm_in, comm_out, *sems):
                    cp.wait()

    def q_side(first_block):
        return pl.BlockSpec((t, LANES), lambda h, s, qt, kt: (qt[s], first_block + h))

    def k_side(first_block):
        return pl.BlockSpec((t, LANES), lambda h, s, qt, kt: (kt[s], first_block + h))

    any_spec = pl.BlockSpec(memory_space=pl.ANY)
    head_scratch = [pltpu.VMEM((1, t), F32), pltpu.VMEM((t, LANES), F32)]
    grad_shape = jax.ShapeDtypeStruct((S, FOX_WIDTH), BF16)
    grid_spec = pltpu.PrefetchScalarGridSpec(
        num_scalar_prefetch=2, grid=(HEAD_PAIRS, steps),
        in_specs=[q_side(0), k_side(HEAD_PAIRS), k_side(2 * HEAD_PAIRS),
                  pl.BlockSpec((PAIR, 1, t), lambda h, s, qt, kt: (h, 0, kt[s])), q_side(0), q_side(0),
                  pl.BlockSpec((PAIR, t, LANES), lambda h, s, qt, kt: (h, qt[s], 0))] + [any_spec] * nc,
        out_specs=[pl.BlockSpec((S, LANES), lambda h, s, qt, kt: (0, h)), k_side(0), k_side(0),
                   pl.BlockSpec((PAIR, 1, t), lambda h, s, qt, kt: (h, 0, kt[s]))] + [any_spec] * nc,
        scratch_shapes=[pltpu.VMEM((S, LANES), F32), pltpu.VMEM((t, LANES), F32), pltpu.VMEM((t, LANES), F32)]
        + head_scratch * PAIR + _comm_scratch(nc))
    return pl.pallas_call(
        body, name="fox_bwd", grid_spec=grid_spec,
        out_shape=[grad_shape, grad_shape, grad_shape, jax.ShapeDtypeStruct((FOX_HEADS, 1, S), F32)]
        + _comm_shapes(comm),
        compiler_params=_params("arbitrary", "arbitrary"))(q_tab, k_tab, qkv, qkv, qkv, cT, o, do, lse, *comm)


def _lanes(g):
    return slice(g * POOL_GROUP_DIM, (g + 1) * POOL_GROUP_DIM)


def _window_sum(e, win, back):
    rows = e.shape[0]
    s = e
    sh = 1
    while sh < win:
        s = s + pltpu.roll(s, sh if back else rows - sh, 0)
        sh *= 2
    return s


def _pooled(u_ref, up_ref, i, g, win, T):
    cur = u_ref[:, _lanes(g)]
    tail = jnp.where(i > 0, up_ref[T - POOL_HALO:T, _lanes(g)], 0.0)
    e = jnp.concatenate([tail, cur], axis=0)
    s = _window_sum(e, win, True)
    t_idx = i * T - POOL_HALO + lax.broadcasted_iota(jnp.int32, (T + POOL_HALO, POOL_GROUP_DIM), 0)
    cnt = jnp.clip(t_idx + 1, 1, win).astype(F32)
    return (s / cnt - e)[POOL_HALO:, :]


def _pool_fwd(uf, pw, ps):
    S = uf.shape[0]
    T = _row_tile(S)

    def body(u_ref, up_ref, w_ref, sc_ref, o_ref):
        i = pl.program_id(0)
        for g, win in enumerate(POOL_WINDOWS):
            pb = _pooled(u_ref, up_ref, i, g, win, T).astype(BF16)
            yv = lax.dot_general(pb, w_ref[g], NN, preferred_element_type=F32)
            o_ref[:, _lanes(g)] = (yv * sc_ref[:, _lanes(g)]).astype(BF16)

    return pl.pallas_call(
        body, name="pool_fwd", grid=(S // T,),
        in_specs=[pl.BlockSpec((T, POOL_WIDTH), lambda i: (i, 0)),
                  pl.BlockSpec((T, POOL_WIDTH), lambda i: (jnp.maximum(i - 1, 0), 0)),
                  pl.BlockSpec((4, POOL_GROUP_DIM, POOL_GROUP_DIM), lambda i: (0, 0, 0)),
                  pl.BlockSpec((1, POOL_WIDTH), lambda i: (0, 0))],
        out_specs=pl.BlockSpec((T, POOL_WIDTH), lambda i: (i, 0)),
        out_shape=jax.ShapeDtypeStruct((S, POOL_WIDTH), BF16), compiler_params=_params("parallel"))(uf, uf, pw, ps)


def _pool_bwd(uf, dpool, pw, ps):
    S = uf.shape[0]
    T = _row_tile(S)
    nb = S // T

    def body(u_ref, up_ref, d_ref, dn_ref, w_ref, sc_ref, du_ref, dw_ref, dsc_ref):
        i = pl.program_id(0)

        @pl.when(i == 0)
        def _():
            dw_ref[...] = jnp.zeros_like(dw_ref)
            dsc_ref[...] = jnp.zeros_like(dsc_ref)

        t_idx = i * T + lax.broadcasted_iota(jnp.int32, (T + POOL_HALO, POOL_GROUP_DIM), 0)
        for g, win in enumerate(POOL_WINDOWS):
            pb = _pooled(u_ref, up_ref, i, g, win, T).astype(BF16)
            w = w_ref[g]
            sc = sc_ref[:, _lanes(g)]
            yv = lax.dot_general(pb, w, NN, preferred_element_type=F32)
            dov = d_ref[:, _lanes(g)]
            dsc_ref[:, _lanes(g)] += jnp.sum(dov * yv, axis=0, keepdims=True)
            head = jnp.where(i < nb - 1, dn_ref[0:POOL_HALO, _lanes(g)], 0.0)
            dyb = (jnp.concatenate([dov, head], axis=0) * sc).astype(BF16)
            dw_ref[g] += lax.dot_general(pb, dyb[:T], TN, preferred_element_type=F32)
            dpl = lax.dot_general(dyb, w, NT, preferred_element_type=F32)
            cnt = jnp.minimum(t_idx + 1, win).astype(F32)
            a = _window_sum(dpl / cnt, win, False)
            du_ref[:, _lanes(g)] = (a - dpl)[:T].astype(BF16)

    return pl.pallas_call(
        body, name="pool_bwd", grid=(nb,),
        in_specs=[pl.BlockSpec((T, POOL_WIDTH), lambda i: (i, 0)),
                  pl.BlockSpec((T, POOL_WIDTH), lambda i: (jnp.maximum(i - 1, 0), 0)),
                  pl.BlockSpec((T, POOL_WIDTH), lambda i: (i, 0)),
                  pl.BlockSpec((T, POOL_WIDTH), lambda i: (jnp.minimum(i + 1, nb - 1), 0)),
                  pl.BlockSpec((4, POOL_GROUP_DIM, POOL_GROUP_DIM), lambda i: (0, 0, 0)),
                  pl.BlockSpec((1, POOL_WIDTH), lambda i: (0, 0))],
        out_specs=[pl.BlockSpec((T, POOL_WIDTH), lambda i: (i, 0)),
                   pl.BlockSpec((4, POOL_GROUP_DIM, POOL_GROUP_DIM), lambda i: (0, 0, 0)),
                   pl.BlockSpec((1, POOL_WIDTH), lambda i: (0, 0))],
        out_shape=[jax.ShapeDtypeStruct((S, POOL_WIDTH), BF16),
                   jax.ShapeDtypeStruct((4, POOL_GROUP_DIM, POOL_GROUP_DIM), F32),
                   jax.ShapeDtypeStruct((1, POOL_WIDTH), F32)],
        compiler_params=_params("arbitrary"))(uf, uf, dpool, dpool, pw, ps)


def _xhead(h):
    return slice(h * X_HEAD_DIM, (h + 1) * X_HEAD_DIM)


def _xvhead(h):
    return slice(D_MODEL + h * X_HEAD_DIM, D_MODEL + (h + 1) * X_HEAD_DIM)


X_CHUNK = 32


def _x_probs(s_ref, rows):
    blocks = [s_ref[rows, _lane_block(b)] * (1.0 / math.sqrt(X_HEAD_DIM)) for b in range(MEM_LEN // LANES)]
    m = jnp.max(_fold(jnp.maximum, blocks), axis=1, keepdims=True)
    es = [jnp.exp(blk - m) for blk in blocks]
    den = jnp.sum(_fold(jnp.add, es), axis=1, keepdims=True)
    return [e / den for e in es]


def _xattn_fwd(q, kv):
    S = q.shape[0]
    t = _row_tile(S)
    chunk = min(X_CHUNK, t)

    def body(q_ref, kv_ref, o_ref):
        for h in range(X_HEADS):
            s = lax.dot_general(q_ref[:, _xhead(h)], kv_ref[:, _xhead(h)], NT, preferred_element_type=F32)
            p_rows = []
            for r in range(t // chunk):
                rows = slice(r * chunk, (r + 1) * chunk)
                p_rows.append(jnp.concatenate([p.astype(BF16) for p in _x_probs(s, rows)], axis=1))
            o_ref[:, _xhead(h)] = lax.dot_general(jnp.concatenate(p_rows, axis=0), kv_ref[:, _xvhead(h)], NN,
                                                  preferred_element_type=F32).astype(BF16)

    return pl.pallas_call(
        body, name="xattn_fwd", grid=(S // t,),
        in_specs=[pl.BlockSpec((t, D_MODEL), lambda i: (i, 0)), pl.BlockSpec((MEM_LEN, 2 * D_MODEL), lambda i: (0, 0))],
        out_specs=pl.BlockSpec((t, D_MODEL), lambda i: (i, 0)),
        out_shape=jax.ShapeDtypeStruct((S, D_MODEL), BF16), compiler_params=_params("parallel"))(q, kv)


def _xattn_bwd(q, kv, do):
    S = q.shape[0]
    t = _row_tile(S)
    nb = S // t
    scale = 1.0 / math.sqrt(X_HEAD_DIM)
    chunk = min(X_CHUNK, t)

    def body(q_ref, kv_ref, do_ref, dq_ref, dkv_ref, acc):
        i = pl.program_id(0)

        @pl.when(i == 0)
        def _():
            acc[...] = jnp.zeros_like(acc)

        for h in range(X_HEADS):
            qh = q_ref[:, _xhead(h)]
            kh = kv_ref[:, _xhead(h)]
            doh = do_ref[:, _xhead(h)]
            s_s = lax.dot_general(qh, kh, NT, preferred_element_type=F32)
            dp_s = lax.dot_general(doh, kv_ref[:, _xvhead(h)], NT, preferred_element_type=F32)
            p_rows, ds_rows = [], []
            for r in range(t // chunk):
                rows = slice(r * chunk, (r + 1) * chunk)
                ps = _x_probs(s_s, rows)
                dps = [dp_s[rows, _lane_block(b)] for b in range(len(ps))]
                inner = jnp.sum(_fold(jnp.add, [dp * p for dp, p in zip(dps, ps)]), axis=1, keepdims=True)
                p_rows.append(jnp.concatenate([p.astype(BF16) for p in ps], axis=1))
                ds_rows.append(jnp.concatenate([(p * (dp - inner)).astype(BF16) for dp, p in zip(dps, ps)], axis=1))
            dsb = jnp.concatenate(ds_rows, axis=0)
            acc[:, _xvhead(h)] += lax.dot_general(jnp.concatenate(p_rows, axis=0), doh, TN,
                                                  preferred_element_type=F32)
            dq_ref[:, _xhead(h)] = (lax.dot_general(dsb, kh, NN, preferred_element_type=F32) * scale).astype(BF16)
            acc[:, _xhead(h)] += lax.dot_general(dsb, qh, TN, preferred_element_type=F32) * scale

        @pl.when(i == nb - 1)
        def _():
            dkv_ref[...] = acc[...].astype(BF16)

    row = pl.BlockSpec((t, D_MODEL), lambda i: (i, 0))
    full = pl.BlockSpec((MEM_LEN, 2 * D_MODEL), lambda i: (0, 0))
    return pl.pallas_call(
        body, name="xattn_bwd", grid=(nb,), in_specs=[row, full, row], out_specs=[row, full],
        out_shape=[jax.ShapeDtypeStruct((S, D_MODEL), BF16), jax.ShapeDtypeStruct((MEM_LEN, 2 * D_MODEL), BF16)],
        scratch_shapes=[pltpu.VMEM((MEM_LEN, 2 * D_MODEL), F32)],
        compiler_params=_params("arbitrary"))(q, kv, do)


def _comm_shapes(arrs):
    return [jax.ShapeDtypeStruct((N_DEV,) + tuple(a.shape[-2:]), a.dtype) for a in arrs]


def _comm_scratch(n):
    if n == 0:
        return []
    return [pltpu.SemaphoreType.DMA((n, N_DEV - 1)), pltpu.SemaphoreType.DMA((n, N_DEV - 1)),
            pltpu.SemaphoreType.DMA((n,))]


def _comm_copies(ins, outs, send_sems, recv_sems, local_sems):
    x, y, c = lax.axis_index("x"), lax.axis_index("y"), lax.axis_index("c")
    me = 4 * x + 2 * y + c
    copies = []
    for w in range(len(ins)):
        src = ins[w] if len(ins[w].shape) == 2 else ins[w].at[me]
        copies.append(pltpu.make_async_copy(src, outs[w].at[me], local_sems.at[w]))
    for k in range(1, N_DEV):
        px = 1 - x if k & 4 else x
        py = 1 - y if k & 2 else y
        pc = 1 - c if k & 1 else c
        peer = 4 * px + 2 * py + pc
        for w in range(len(ins)):
            src = ins[w] if len(ins[w].shape) == 2 else ins[w].at[peer]
            copies.append(pltpu.make_async_remote_copy(
                src_ref=src, dst_ref=outs[w].at[me], send_sem=send_sems.at[w, k - 1],
                recv_sem=recv_sems.at[w, k - 1], device_id=(px, py, pc), device_id_type=pl.DeviceIdType.MESH))
    return copies


class _Gather:
    def __init__(self, ins, outs, send_sems, recv_sems, local_sems):
        x, y, c = lax.axis_index("x"), lax.axis_index("y"), lax.axis_index("c")
        me = 4 * x + 2 * y + c
        sibling = (x, y, 1 - c)
        self.local, self.mine, self.passed = [], [], []
        for w in range(len(ins)):
            def remote(idx, src, slot, dev, w=w):
                return pltpu.make_async_remote_copy(
                    src_ref=src, dst_ref=outs[w].at[slot], send_sem=send_sems.at[w, idx],
                    recv_sem=recv_sems.at[w, idx], device_id=dev, device_id_type=pl.DeviceIdType.MESH)

            self.local.append(pltpu.make_async_copy(ins[w], outs[w].at[me], local_sems.at[w]))
            mine, passed = [remote(0, ins[w], me, sibling)], []
            for j, (fx, fy) in enumerate(((0, 1), (1, 0), (1, 1))):
                px = 1 - x if fx else x
                py = 1 - y if fy else y
                slot = 4 * px + 2 * py + c
                mine.append(remote(1 + j, ins[w], me, (px, py, c)))
                passed.append(remote(4 + j, outs[w].at[slot], slot, sibling))
            self.mine.append(mine)
            self.passed.append(passed)

    def start(self):
        for cp in self.local:
            cp.start()
        for mine in self.mine:
            for cp in mine:
                cp.start()

    def pass_on(self):
        for mine, passed in zip(self.mine, self.passed):
            for j, cp in enumerate(passed):
                mine[1 + j].wait_recv()
                cp.start()

    def finish(self):
        for mine, passed in zip(self.mine, self.passed):
            mine[0].wait_recv()
            for cp in passed:
                cp.wait_recv()
            for cp in mine + passed:
                cp.wait_send()
        for cp in self.local:
            cp.wait()


def _exchange(name, arrs):
    n = len(arrs)
    gather = all(a.ndim == 2 for a in arrs)

    def body(*refs):
        if gather:
            g = _Gather(refs[:n], refs[n:2 * n], *refs[2 * n:])
            g.start()
            g.pass_on()
            g.finish()
            return
        copies = _comm_copies(refs[:n], refs[n:2 * n], *refs[2 * n:])
        for cp in copies:
            cp.start()
        for cp in copies:
            cp.wait()

    any_spec = pl.BlockSpec(memory_space=pl.ANY)
    return pl.pallas_call(
        body, name=name, in_specs=[any_spec] * n, out_specs=[any_spec] * n, out_shape=_comm_shapes(arrs),
        scratch_shapes=_comm_scratch(n))(*arrs)


def _adamw_math(w, g, m, v):
    m = ADAM_B1 * m + (1.0 - ADAM_B1) * g
    v = ADAM_B2 * v + (1.0 - ADAM_B2) * (g * g)
    m_hat = m / (1.0 - ADAM_B1 ** ADAM_STEP)
    v_hat = v / (1.0 - ADAM_B2 ** ADAM_STEP)
    delta = -ADAM_LR * (m_hat / (jnp.sqrt(v_hat) + ADAM_EPS) + ADAM_WD * w)
    return delta, m, v


def _sum_parts(p_ref):
    g = p_ref[0].astype(F32)
    for s in range(1, N_DEV):
        g = g + p_ref[s].astype(F32)
    return g


def _adamw_big(name, w, m, v, parts, tr, comm=()):
    L, R, C = w.shape
    nc = len(comm)
    gather = all(a.ndim == 2 for a in comm)
    nr = R // tr

    def exchange(comm_in, comm_out, sems, begin):
        if gather:
            g = _Gather(comm_in, comm_out, *sems)
            if begin:
                g.start()
            else:
                g.pass_on()
                g.finish()
        else:
            for cp in _comm_copies(comm_in, comm_out, *sems):
                cp.start() if begin else cp.wait()

    def body(w_ref, m_ref, v_ref, *rest):
        p_refs = rest[:L]
        comm_in = rest[L:L + nc]
        g_ref, d_ref, nm_ref, nv_ref = rest[L + nc:L + nc + 4]
        comm_out = rest[L + nc + 4:L + 2 * nc + 4]
        sems = rest[L + 2 * nc + 4:]
        layer = pl.program_id(0)
        if nc:
            @pl.when((layer == 0) & (pl.program_id(1) == 0))
            def _():
                exchange(comm_in, comm_out, sems, True)

        for j in range(L):
            @pl.when(layer == j)
            def _(j=j):
                g = _sum_parts(p_refs[j])
                delta, nm, nv = _adamw_math(w_ref[...], g, m_ref[...], v_ref[...])
                g_ref[...] = g
                d_ref[...] = delta
                nm_ref[...] = nm
                nv_ref[...] = nv

        if nc:
            @pl.when((layer == L - 1) & (pl.program_id(1) == nr - 1))
            def _():
                exchange(comm_in, comm_out, sems, False)

    blk = pl.BlockSpec((None, tr, C), lambda l, i: (l, i, 0))

    def part_spec(j):
        return pl.BlockSpec((N_DEV, tr, C), lambda l, i: (0, jnp.where(l == j, i, 0), 0))

    shp = jax.ShapeDtypeStruct((L, R, C), F32)
    any_spec = pl.BlockSpec(memory_space=pl.ANY)
    return pl.pallas_call(
        body, name=name, grid=(L, nr),
        in_specs=[blk, blk, blk] + [part_spec(j) for j in range(L)] + [any_spec] * nc,
        out_specs=[blk] * 4 + [any_spec] * nc, out_shape=[shp] * 4 + _comm_shapes(comm),
        scratch_shapes=_comm_scratch(nc),
        compiler_params=_params("arbitrary", "arbitrary"))(w, m, v, *parts, *comm)


def _adamw_small(w, m, v, parts):
    R, C = w.shape

    def body(w_ref, m_ref, v_ref, p_ref, g_ref, d_ref, nm_ref, nv_ref):
        g = _sum_parts(p_ref)
        delta, nm, nv = _adamw_math(w_ref[...], g, m_ref[...], v_ref[...])
        g_ref[...] = g
        d_ref[...] = delta
        nm_ref[...] = nm
        nv_ref[...] = nv

    shp = jax.ShapeDtypeStruct((R, C), F32)
    return pl.pallas_call(body, name="adamw_small", out_shape=[shp] * 4,
                          compiler_params=pltpu.CompilerParams(vmem_limit_bytes=VMEM_LIMIT))(w, m, v, parts)


def _vec(a):
    return a.reshape(1, -1)


W_IN_SHARD = IN_COLS // N_DEV
W_IN_ROWS = 272


def _w_in_travel(a):
    pad = [(0, 0)] * (a.ndim - 2) + [(0, W_IN_ROWS - W_IN_SHARD), (0, 0)]
    return jnp.pad(jnp.swapaxes(a, -1, -2), pad)


def _unpack_w_in(g):
    full = jnp.transpose(g[:, :W_IN_SHARD, :], (2, 0, 1)).reshape(D_MODEL, IN_COLS)
    qkv = full[:, :QKV_COLS]
    f = full[:, QKV_COLS:QKV_COLS + FOX_HEADS]
    u = full[:, QKV_COLS + FOX_HEADS:]
    uf = jnp.concatenate([u, f, jnp.zeros((D_MODEL, UF_COLS - POOL_WIDTH - FOX_HEADS), g.dtype)], axis=1)
    return jnp.concatenate([qkv, uf], axis=1)


def _pack_dw_in(dwp):
    qkv = dwp[:, :QKV_COLS]
    u = dwp[:, QKV_COLS:QKV_COLS + POOL_WIDTH]
    f = dwp[:, QKV_COLS + POOL_WIDTH:QKV_COLS + POOL_WIDTH + FOX_HEADS]
    full = jnp.concatenate([qkv, f, u], axis=1)
    return _w_in_travel(jnp.transpose(full.reshape(D_MODEL, N_DEV, W_IN_SHARD), (1, 0, 2)))


REST = ['w_out', 'wq_x', 'wkv_x', 'wo_x', 'w_up', 'w_down']


def _layer_fwd(x0, h1, mem, sp, g_in, shards, g_next):
    S = x0.shape[0]
    sv = {"x0": x0}
    w_inp = _unpack_w_in(g_in)
    qkv, uf = _mm_rows("mm_in", [(h1, w_inp, "nn")],
                       [(BF16, 0, QKV_COLS, "id"), (F32, QKV_COLS, UF_COLS, "id")], piece=UF_COLS)
    c = _gate_fwd(uf, sp["b_forget"])
    cT = jnp.transpose(c[:, :FOX_HEADS]).reshape(FOX_HEADS, 1, S)
    o, ob, lse, *got = _fox_fwd(qkv, cT, shards)
    g_out, g_q, g_kv, g_o, g_up, g_down = got[:6]
    W = dict(inp=w_inp, out=g_out.reshape(D_MODEL, D_MODEL), q=g_q.reshape(D_MODEL, D_MODEL), kv=g_kv,
             o=g_o.reshape(D_MODEL, D_MODEL), up=g_up, down=g_down.reshape(D_FF, D_MODEL))
    pool = _pool_fwd(uf, sp["pool_w"], sp["pool_scale"])
    cat = jnp.concatenate([ob, pool], axis=1)
    mix, x1, h2 = _mm_resid_norm("mm_sq_norm", cat, W["out"], x0, sp["g_mix_post"], sp["g_x_pre"])
    mn = _norm_fwd("norm_mem", mem, sp["g_mem"])
    q2 = _mm1("mm_q", h2, W["q"], "nn", D_MODEL, BF16)
    kv = _mm1("mm_kv", mn, W["kv"], "nn3", 2 * D_MODEL, BF16, piece=2 * D_MODEL // N_DEV)
    o2 = _xattn_fwd(q2, kv)
    xo, x2, h3 = _mm_resid_norm("mm_sq_norm", o2, W["o"], x1, sp["g_x_post"], sp["g_ffn_pre"])
    up, act = _mm_rows("mm_up", [(h3, W["up"], "nn3")], [(BF16, 0, D_FF, "id"), (BF16, 0, D_FF, "relu2")],
                       piece=D_FF // N_DEV)
    y, x3, h_next = _mm_resid_norm("mm_down_norm" if g_next is not None else "mm_down_norm_last", act, W["down"], x2,
                                   sp["g_ffn_post"], g_next)
    sv.update(h1=h1, uf=uf, cT=cT, qkv=qkv, o=o, lse=lse, cat=cat, mix=mix, x1=x1, h2=h2, mn=mn, q2=q2, kv=kv,
              o2=o2, xo=xo, x2=x2, h3=h3, up=up, act=act, y=y)
    return x3, h_next, sv, W, (got[6] if len(got) > 6 else None)


def _layer_bwd(dx3, dy, mem, sv, sp, W, carried, below):
    S = dx3.shape[0]
    gs = {}
    gb = {}
    (dup,) = _mm_rows("mm_dup", [(dy, W["down"], "nt")], [(BF16, 0, D_FF, "drelu2")], extra=sv["up"])
    gb["w_down"] = _mm_tn("mm_dw_down", sv["act"], dy, BF16).reshape(N_DEV, D_FF // N_DEV, D_MODEL)
    gb["w_up"] = _mm_tn("mm_dw_up", sv["h3"], dup, BF16, shard_cols=D_FF // N_DEV)
    dx2, gs["g_ffn_pre"], dxo, gs["g_x_post"] = _norm_bwd(
        "mm_dh3_norm_bwd", (dup, W["up"], "nt3"), sv["x2"], sp["g_ffn_pre"], dx3, F32,
        below=(sv["xo"], sp["g_x_post"]))
    do2 = _mm1("mm_sq_t", dxo, W["o"], "nt", D_MODEL, BF16)
    gb["wo_x"] = _mm_tn("mm_dw_sq", sv["o2"], dxo, BF16).reshape(N_DEV, D_MODEL // N_DEV, D_MODEL)
    dq2, dkvb = _xattn_bwd(sv["q2"], sv["kv"], do2)
    gb["wq_x"] = _mm_tn("mm_dw_sq", sv["h2"], dq2, BF16).reshape(N_DEV, D_MODEL // N_DEV, D_MODEL)
    gb["wkv_x"] = _mm_tn("mm_dw_kv", sv["mn"], dkvb, BF16, shard_cols=2 * D_MODEL // N_DEV)
    dmn = _mm1("mm_dmn", dkvb, W["kv"], "nt3", D_MODEL, F32)
    _, gs["g_mem"] = _norm_bwd("norm_bwd_mem", dmn, mem, sp["g_mem"], None, BF16)
    dx1, gs["g_x_pre"], dmix, gs["g_mix_post"] = _norm_bwd(
        "mm_dh2_norm_bwd", (dq2, W["q"], "nt"), sv["x1"], sp["g_x_pre"], dx2, F32,
        below=(sv["mix"], sp["g_mix_post"]))
    doh, dpool = _mm_rows("mm_dcat", [(dmix, W["out"], "nt")],
                          [(BF16, 0, FOX_WIDTH, "id"), (F32, FOX_WIDTH, POOL_WIDTH, "id")])
    gb["w_out"] = _mm_tn("mm_dw_sq", sv["cat"], dmix, BF16).reshape(N_DEV, D_MODEL // N_DEV, D_MODEL)
    du, gs["pool_w"], gs["pool_scale"] = _pool_bwd(sv["uf"], dpool, sp["pool_w"], sp["pool_scale"])
    dq, dk, dv, dcT, *got = _fox_bwd(sv["qkv"], sv["cT"], sv["o"], sv["lse"], doh, [gb[n] for n in REST] + carried)
    dc = jnp.pad(jnp.transpose(dcT.reshape(FOX_HEADS, S)), ((0, 0), (0, LANES - FOX_HEADS)))
    dfg, db = _gate_bwd(dc, sv["uf"], sp["b_forget"])
    gs["b_forget"] = db[:, :FOX_HEADS]
    dproj = [dq, dk, dv, du, dfg]
    dwp = _mm_tn("mm_dw_in", sv["h1"], dproj, BF16, piece=UF_COLS)
    dh1 = (dproj, W["inp"], "nt")
    if below is None:
        dx0, gs["g_mix_pre"] = _norm_bwd("mm_dh1_norm_bwd_first", dh1, sv["x0"], sp["g_mix_pre"], dx1, F32)
        lower = None
    else:
        dx0, gs["g_mix_pre"], *lower = _norm_bwd("mm_dh1_norm_bwd", dh1, sv["x0"], sp["g_mix_pre"], dx1, F32,
                                                 below=below)
    return dx0, lower, dict(zip(REST, got[:6])), got[6:], _pack_dw_in(dwp), gs


def _small_rows(shape):
    return -(-math.prod(shape) // (8 * LANES)) * 8


def _pack_small(d):
    blocks = []
    for n in SMALL:
        rows = _small_rows(d[n].shape)
        if d[n].shape[-1] == LANES:
            blocks.append(d[n].reshape(rows, LANES))
        else:
            flat = d[n].reshape(-1)
            blocks.append(jnp.pad(flat, (0, rows * LANES - flat.shape[0])).reshape(rows, LANES))
    return jnp.concatenate(blocks, axis=0)


def _unpack_small(packed, like):
    out = {}
    row = 0
    for n in SMALL:
        shape = like[n].shape
        rows = _small_rows(shape)
        block = packed[row:row + rows]
        out[n] = block.reshape(shape) if shape[-1] == LANES else block.reshape(-1)[:math.prod(shape)].reshape(shape)
        row += rows
    return out


def kernel(x, mem, g_mix_pre, w_in, b_forget, pool_w, pool_scale, w_out, g_mix_post, g_x_pre, g_mem, wq_x, wkv_x, wo_x, g_x_post, g_ffn_pre, w_up, w_down, g_ffn_post, loss_target, m_g_mix_pre, m_w_in, m_b_forget, m_pool_w, m_pool_scale, m_w_out, m_g_mix_post, m_g_x_pre, m_g_mem, m_wq_x, m_wkv_x, m_wo_x, m_g_x_post, m_g_ffn_pre, m_w_up, m_w_down, m_g_ffn_post, v_g_mix_pre, v_w_in, v_b_forget, v_pool_w, v_pool_scale, v_w_out, v_g_mix_post, v_g_x_pre, v_g_mem, v_wq_x, v_wkv_x, v_wo_x, v_g_x_post, v_g_ffn_pre, v_w_up, v_w_down, v_g_ffn_post):
    w = dict(g_mix_pre=g_mix_pre, w_in=w_in, b_forget=b_forget, pool_w=pool_w, pool_scale=pool_scale, w_out=w_out,
             g_mix_post=g_mix_post, g_x_pre=g_x_pre, g_mem=g_mem, wq_x=wq_x, wkv_x=wkv_x, wo_x=wo_x,
             g_x_post=g_x_post, g_ffn_pre=g_ffn_pre, w_up=w_up, w_down=w_down, g_ffn_post=g_ffn_post)
    mom = dict(g_mix_pre=m_g_mix_pre, w_in=m_w_in, b_forget=m_b_forget, pool_w=m_pool_w, pool_scale=m_pool_scale,
               w_out=m_w_out, g_mix_post=m_g_mix_post, g_x_pre=m_g_x_pre, g_mem=m_g_mem, wq_x=m_wq_x,
               wkv_x=m_wkv_x, wo_x=m_wo_x, g_x_post=m_g_x_post, g_ffn_pre=m_g_ffn_pre, w_up=m_w_up,
               w_down=m_w_down, g_ffn_post=m_g_ffn_post)
    var = dict(g_mix_pre=v_g_mix_pre, w_in=v_w_in, b_forget=v_b_forget, pool_w=v_pool_w, pool_scale=v_pool_scale,
               w_out=v_w_out, g_mix_post=v_g_mix_post, g_x_pre=v_g_x_pre, g_mem=v_g_mem, wq_x=v_wq_x,
               wkv_x=v_wkv_x, wo_x=v_wo_x, g_x_post=v_g_x_post, g_ffn_pre=v_g_ffn_pre, w_up=v_w_up,
               w_down=v_w_down, g_ffn_post=v_g_ffn_post)
    S = x.shape[1]
    xs = x.reshape(S, D_MODEL)
    mems = mem.reshape(MEM_LEN, D_MODEL)
    target = loss_target.reshape(S, D_MODEL)

    def small_params(l):
        return dict(
            g_mix_pre=_vec(g_mix_pre[l]), g_mix_post=_vec(g_mix_post[l]), g_x_pre=_vec(g_x_pre[l]),
            g_mem=_vec(g_mem[l]), g_x_post=_vec(g_x_post[l]), g_ffn_pre=_vec(g_ffn_pre[l]),
            g_ffn_post=_vec(g_ffn_post[l]), pool_scale=_vec(pool_scale[l]), pool_w=pool_w[l].astype(BF16),
            b_forget=jnp.pad(_vec(b_forget[l]), ((0, 0), (0, LANES - FOX_HEADS))))

    shard = {n: [w[n][l].astype(BF16) for l in range(DEPTH)] for n in REST}
    shard["w_in"] = [_w_in_travel(w_in[l].astype(BF16)) for l in range(DEPTH)]
    sps = [small_params(l) for l in range(DEPTH)]
    saved, weights = [], []
    h = xs
    (g_in,) = _exchange("gather_w_in", [shard["w_in"][0]])
    hn = _norm_fwd("norm_fwd", xs, sps[0]["g_mix_pre"])
    for l in range(DEPTH):
        travelling = [shard[n][l] for n in REST] + ([shard["w_in"][l + 1]] if l + 1 < DEPTH else [])
        g_next = sps[l + 1]["g_mix_pre"] if l + 1 < DEPTH else None
        h, hn, sv, W, g_in = _layer_fwd(h, hn, mems, sps[l], g_in, travelling, g_next)
        saved.append(sv)
        weights.append(W)
    dh, sq = _loss_fwd_bwd(h, target)
    loss = lax.psum(0.5 * sq[0, 0] / D_MODEL, ("x", "y", "c"))

    parts = [dict() for _ in range(DEPTH)]
    small_grads = [None] * DEPTH
    carried = []
    lower = _norm_bwd("norm_bwd_b", dh, saved[-1]["y"], sps[-1]["g_ffn_post"], None, BF16)
    for l in reversed(range(DEPTH)):
        dy, dg_ffn_post = lower
        below = (saved[l - 1]["y"], sps[l - 1]["g_ffn_post"]) if l > 0 else None
        dh, lower, got, got_carried, dw_in, gs = _layer_bwd(dh, dy, mems, saved[l], sps[l], weights[l], carried, below)
        gs["g_ffn_post"] = dg_ffn_post
        parts[l].update(got)
        if got_carried:
            parts[l + 1]["w_in"] = got_carried[0]
        carried = [dw_in]
        small_grads[l] = gs
    grad_x = dh.reshape(1, S, D_MODEL)

    grads, deltas, new_m, new_v = {}, {}, {}, {}
    rows = dict(w_in=128, w_out=128, wq_x=128, wkv_x=256, wo_x=128, w_up=256, w_down=128)
    sg = {n: jnp.stack([small_grads[l][n].reshape(w[n].shape[1:]) for l in range(DEPTH)]) for n in SMALL}
    riders = dict(w_down=carried, w_up=[_pack_small(sg)])
    for n in ["w_down", "w_up", "w_out", "wq_x", "wkv_x", "wo_x", "w_in"]:
        if n == "w_in":
            for l in range(DEPTH):
                parts[l]["w_in"] = jnp.swapaxes(parts[l]["w_in"][:, :W_IN_SHARD, :], 1, 2)
        grads[n], deltas[n], new_m[n], new_v[n], *got = _adamw_big(
            "adamw_" + n, w[n], mom[n], var[n], [parts[l][n] for l in range(DEPTH)], rows[n], riders.get(n, ()))
        if n == "w_down":
            (parts[0]["w_in"],) = got
        elif n == "w_up":
            (sg_parts,) = got
    outs = _adamw_small(_pack_small(w), _pack_small(mom), _pack_small(var), sg_parts)
    for d, packed in zip((grads, deltas, new_m, new_v), outs):
        d.update(_unpack_small(packed, w))

    return (loss, grad_x, *[grads[n] for n in W_NAMES], *[deltas[n] for n in W_NAMES],
            *[new_m[n] for n in W_NAMES], *[new_v[n] for n in W_NAMES])
```

```python
import math

import jax
import jax.numpy as jnp
from jax import lax
from jax.experimental import pallas as pl
from jax.experimental.pallas import tpu as pltpu

F32 = jnp.float32
BF16 = jnp.bfloat16

D_MODEL = 1024
DEPTH = 4
FOX_WIDTH = 512
FOX_HEADS = 8
FOX_HEAD_DIM = 64
POOL_WIDTH = 512
POOL_WINDOWS = (2, 4, 8, 16)
POOL_GROUP_DIM = 128
POOL_HALO = 16
MEM_LEN = 256
X_HEADS = 4
X_HEAD_DIM = 256
D_FF = 4096
EPS = 1e-6
IN_COLS = 2056
QKV_COLS = 3 * FOX_WIDTH
UF_COLS = 640
INP_COLS = QKV_COLS + UF_COLS
N_DEV = 8
LANES = 128

ADAM_LR = 0.001
ADAM_B1 = 0.9
ADAM_B2 = 0.999
ADAM_EPS = 1e-08
ADAM_WD = 0.01
ADAM_STEP = 10

VMEM_LIMIT = 56 * 1024 * 1024

W_NAMES = ['g_mix_pre', 'w_in', 'b_forget', 'pool_w', 'pool_scale', 'w_out', 'g_mix_post', 'g_x_pre', 'g_mem',
           'wq_x', 'wkv_x', 'wo_x', 'g_x_post', 'g_ffn_pre', 'w_up', 'w_down', 'g_ffn_post']
BIG = ['w_in', 'w_out', 'wq_x', 'wkv_x', 'wo_x', 'w_up', 'w_down']
SMALL = [n for n in W_NAMES if n not in BIG]

NN = (((1,), (0,)), ((), ()))
NT = (((1,), (1,)), ((), ()))
TN = (((0,), (0,)), ((), ()))


def _params(*sem):
    return pltpu.CompilerParams(dimension_semantics=sem, vmem_limit_bytes=VMEM_LIMIT)


def _row_tile(s):
    return min(s, 512)


def _product(a_ref, w_ref, kind, c0, pw):
    cols = slice(c0, c0 + pw)
    if kind == "nn":
        return lax.dot_general(a_ref[...], w_ref[:, cols], NN, preferred_element_type=F32)
    if kind == "nt":
        return lax.dot_general(a_ref[...], w_ref[cols, :], NT, preferred_element_type=F32)
    n = w_ref.shape[2]
    if kind == "nn3":
        assert pw == n and c0 % n == 0
        return lax.dot_general(a_ref[...], w_ref[c0 // n], NN, preferred_element_type=F32)
    r = None
    for j in range(w_ref.shape[0]):
        part = lax.dot_general(a_ref[:, j * n:(j + 1) * n], w_ref[j, cols, :], NT, preferred_element_type=F32)
        r = part if r is None else r + part
    return r


def _resident(w):
    return pl.BlockSpec(w.shape, lambda i, nd=w.ndim: (0,) * nd)


def _mm_rows(name, terms, outs, extra=None, piece=1024):
    M = terms[0][0].shape[0]
    tm = _row_tile(M)
    nterm = len(terms)
    n_extra = 0 if extra is None else 1
    groups = {}
    for idx, (_, c0, width, fn) in enumerate(outs):
        groups.setdefault((c0, width), []).append((idx, fn))

    def body(*refs):
        a_refs = refs[0:2 * nterm:2]
        w_refs = refs[1:2 * nterm:2]
        extra_refs = refs[2 * nterm:2 * nterm + n_extra]
        out_refs = refs[2 * nterm + n_extra:]
        for (g0, gw), members in groups.items():
            for c0 in range(g0, g0 + gw, piece):
                pw = min(piece, g0 + gw - c0)
                r = None
                for a_ref, w_ref, (_, w, kind) in zip(a_refs, w_refs, terms):
                    part = _product(a_ref, w_ref, kind, c0, pw)
                    r = part if r is None else r + part
                dst = slice(c0 - g0, c0 - g0 + pw)
                for idx, fn in members:
                    if fn == "relu2":
                        rp = jnp.maximum(r, 0.0)
                        val = rp * rp
                    elif fn == "drelu2":
                        val = r * (2.0 * jnp.maximum(extra_refs[0][:, dst].astype(F32), 0.0))
                    else:
                        val = r
                    out_refs[idx][:, dst] = val.astype(out_refs[idx].dtype)

    in_specs, ins = [], []
    for a, w, _ in terms:
        in_specs.append(pl.BlockSpec((tm, a.shape[1]), lambda i: (i, 0)))
        in_specs.append(pl.BlockSpec(w.shape, lambda i, nd=w.ndim: (0,) * nd))
        ins += [a, w]
    if extra is not None:
        in_specs.append(pl.BlockSpec((tm, extra.shape[1]), lambda i: (i, 0)))
        ins.append(extra)
    res = pl.pallas_call(
        body, name=name, grid=(M // tm,), in_specs=in_specs,
        out_specs=[pl.BlockSpec((tm, width), lambda i: (i, 0)) for _, _, width, _ in outs],
        out_shape=[jax.ShapeDtypeStruct((M, width), dt) for dt, _, width, _ in outs],
        compiler_params=_params("parallel"))(*ins)
    return res


def _mm1(name, a, w, kind, n_cols, dtype, piece=1024):
    return _mm_rows(name, [(a, w, kind)], [(dtype, 0, n_cols, "id")], piece=piece)[0]


def _mm_tn(name, a, b, out_dtype, shard_cols=None, piece=512):
    K, M = a.shape
    b_parts = b if isinstance(b, list) else [b]
    nb = len(b_parts)
    N = sum(p.shape[1] for p in b_parts)
    tk = _row_tile(K)
    nk = K // tk
    piece = shard_cols or min(piece, N)

    def body(a_ref, *rest):
        b_refs = rest[:nb]
        o_ref, acc = rest[nb:]
        k = pl.program_id(0)

        @pl.when(k == 0)
        def _():
            acc[...] = jnp.zeros_like(acc)

        a_t = jnp.transpose(a_ref[...])
        if nb == 1:
            for c0 in range(0, N, piece):
                cols = slice(c0, min(c0 + piece, N))
                acc[:, cols] += lax.dot_general(a_t, b_refs[0][:, cols], NN, preferred_element_type=F32)
        else:
            c0 = 0
            for b_ref in b_refs:
                cols = slice(c0, c0 + b_ref.shape[1])
                acc[:, cols] += lax.dot_general(a_t, b_ref[...], NN, preferred_element_type=F32)
                c0 += b_ref.shape[1]

        @pl.when(k == nk - 1)
        def _():
            for c0 in range(0, N, piece):
                cols = slice(c0, min(c0 + piece, N))
                if shard_cols:
                    o_ref[c0 // piece] = acc[:, cols].astype(o_ref.dtype)
                else:
                    o_ref[:, cols] = acc[:, cols].astype(o_ref.dtype)

    out_dims = (N // shard_cols, M, shard_cols) if shard_cols else (M, N)
    return pl.pallas_call(
        body, name=name, grid=(nk,),
        in_specs=[pl.BlockSpec((tk, M), lambda k: (k, 0))]
        + [pl.BlockSpec((tk, p.shape[1]), lambda k: (k, 0)) for p in b_parts],
        out_specs=pl.BlockSpec(out_dims, lambda k, nd=len(out_dims): (0,) * nd),
        out_shape=jax.ShapeDtypeStruct(out_dims, out_dtype),
        scratch_shapes=[pltpu.VMEM((M, N), F32)],
        compiler_params=_params("arbitrary"))(a, *b_parts)


def _mm_tn_rows(name, a_parts, b, out_dtype):
    K = b.shape[0]
    N = b.shape[1]
    na = len(a_parts)
    M = sum(p.shape[1] for p in a_parts)
    tk = _row_tile(K)
    nk = K // tk

    def body(*refs):
        a_refs = refs[:na]
        b_ref, o_ref, acc = refs[na:]
        k = pl.program_id(0)

        @pl.when(k == 0)
        def _():
            acc[...] = jnp.zeros_like(acc)

        bv = b_ref[...]
        r0 = 0
        for a_ref in a_refs:
            rows = slice(r0, r0 + a_ref.shape[1])
            acc[rows, :] += lax.dot_general(jnp.transpose(a_ref[...]), bv, NN, preferred_element_type=F32)
            r0 += a_ref.shape[1]

        @pl.when(k == nk - 1)
        def _():
            o_ref[...] = acc[...].astype(o_ref.dtype)

    return pl.pallas_call(
        body, name=name, grid=(nk,),
        in_specs=[pl.BlockSpec((tk, p.shape[1]), lambda k: (k, 0)) for p in a_parts]
        + [pl.BlockSpec((tk, N), lambda k: (k, 0))],
        out_specs=pl.BlockSpec((M, N), lambda k: (0, 0)), out_shape=jax.ShapeDtypeStruct((M, N), out_dtype),
        scratch_shapes=[pltpu.VMEM((M, N), F32)], compiler_params=_params("arbitrary"))(*a_parts, b)


def _norm_fwd(name, x, g):
    S, Dm = x.shape
    ts = _row_tile(S)

    def body(x_ref, g_ref, h_ref):
        xv = x_ref[...]
        r = lax.rsqrt(jnp.mean(xv * xv, axis=-1, keepdims=True) + EPS)
        h_ref[...] = ((xv * r) * g_ref[...]).astype(BF16)

    return pl.pallas_call(
        body, name=name, grid=(S // ts,),
        in_specs=[pl.BlockSpec((ts, Dm), lambda i: (i, 0)), pl.BlockSpec((1, Dm), lambda i: (0, 0))],
        out_specs=pl.BlockSpec((ts, Dm), lambda i: (i, 0)),
        out_shape=jax.ShapeDtypeStruct((S, Dm), BF16), compiler_params=_params("parallel"))(x, g)


def _mm_resid_norm(name, a, w, x, g, g_next):
    S, Dm = x.shape
    ts = _row_tile(S)
    has_next = g_next is not None

    def body(a_ref, w_ref, x_ref, g_ref, *rest):
        fv = _product(a_ref, w_ref, "nn", 0, Dm)
        r = lax.rsqrt(jnp.mean(fv * fv, axis=-1, keepdims=True) + EPS)
        xn = x_ref[...] + (fv * r) * g_ref[...]
        if has_next:
            gn_ref, f_ref, o_ref, h_ref = rest
            rn = lax.rsqrt(jnp.mean(xn * xn, axis=-1, keepdims=True) + EPS)
            h_ref[...] = ((xn * rn) * gn_ref[...]).astype(BF16)
        else:
            f_ref, o_ref = rest
        f_ref[...] = fv
        o_ref[...] = xn

    row = pl.BlockSpec((ts, Dm), lambda i: (i, 0))
    vec = pl.BlockSpec((1, Dm), lambda i: (0, 0))
    ins = [a, w, x, g] + ([g_next] if has_next else [])
    f32_rows = jax.ShapeDtypeStruct((S, Dm), F32)
    res = pl.pallas_call(
        body, name=name, grid=(S // ts,),
        in_specs=[pl.BlockSpec((ts, a.shape[1]), lambda i: (i, 0)), _resident(w), row, vec] + ([vec] if has_next else []),
        out_specs=[row, row] + ([row] if has_next else []),
        out_shape=[f32_rows, f32_rows] + ([jax.ShapeDtypeStruct((S, Dm), BF16)] if has_next else []),
        compiler_params=_params("parallel"))(*ins)
    return (res[0], res[1], res[2]) if has_next else (res[0], res[1], None)


def _rms_bwd(dov, yv, g):
    r = lax.rsqrt(jnp.mean(yv * yv, axis=-1, keepdims=True) + EPS)
    z = dov * g
    yr = yv * r
    return r * (z - yr * jnp.mean(yr * z, axis=-1, keepdims=True)), jnp.sum(dov * yr, axis=0, keepdims=True)


def _norm_bwd(name, dout, y, g, resid, out_dtype, below=None):
    S, Dm = y.shape
    ts = _row_tile(S)
    has_resid = resid is not None
    chained = below is not None
    produced = isinstance(dout, tuple)
    kind = dout[2] if produced else None
    a_parts = (dout[0] if isinstance(dout[0], list) else [dout[0]]) if produced else []
    n_a = len(a_parts)

    def body(*refs):
        refs = list(refs)
        if produced and n_a == 1:
            dov = _product(refs[0], refs[1], kind, 0, Dm)
            refs = refs[1:]
        elif produced:
            w_ref = refs[n_a]
            dov, k0 = None, 0
            for a_ref in refs[:n_a]:
                k1 = k0 + a_ref.shape[1]
                if kind == "nt":
                    part = lax.dot_general(a_ref[...], w_ref[:, k0:k1], NT, preferred_element_type=F32)
                else:
                    part = lax.dot_general(a_ref[...], w_ref[k0:k1, :], NN, preferred_element_type=F32)
                dov = part if dov is None else dov + part
                k0 = k1
            refs = refs[n_a:]
        else:
            dov = refs[0][...]
        y_ref, g_ref = refs[1:3]
        pos = 3
        r_ref = refs[pos] if has_resid else None
        pos += has_resid
        if chained:
            f_ref, gf_ref = refs[pos:pos + 2]
            pos += 2
        dy_ref, dg_ref = refs[pos:pos + 2]
        i = pl.program_id(0)
        dy, dg = _rms_bwd(dov, y_ref[...], g_ref[...])
        if has_resid:
            dy = dy + r_ref[...]
        dy_ref[...] = dy.astype(out_dtype)

        @pl.when(i == 0)
        def _():
            for ref in refs[pos + 1::2]:
                ref[...] = jnp.zeros_like(ref)

        dg_ref[...] += dg
        if chained:
            df_ref, dgf_ref = refs[pos + 2:pos + 4]
            df, dgf = _rms_bwd(dy, f_ref[...], gf_ref[...])
            df_ref[...] = df.astype(BF16)
            dgf_ref[...] += dgf

    row = pl.BlockSpec((ts, Dm), lambda i: (i, 0))
    vec = pl.BlockSpec((1, Dm), lambda i: (0, 0))
    if produced:
        assert n_a == 1 or kind in ("nt", "nn")
        ins = a_parts + [dout[1]]
        specs = [pl.BlockSpec((ts, a.shape[1]), lambda i: (i, 0)) for a in a_parts] + [_resident(dout[1])]
    else:
        ins = [dout]
        specs = [row]
    ins += [y, g] + ([resid] if has_resid else []) + (list(below) if chained else [])
    specs += [row, vec] + ([row] if has_resid else []) + ([row, vec] if chained else [])
    vec_shape = jax.ShapeDtypeStruct((1, Dm), F32)
    return pl.pallas_call(
        body, name=name, grid=(S // ts,), in_specs=specs, out_specs=[row, vec] + ([row, vec] if chained else []),
        out_shape=[jax.ShapeDtypeStruct((S, Dm), out_dtype), vec_shape]
        + ([jax.ShapeDtypeStruct((S, Dm), BF16), vec_shape] if chained else []),
        compiler_params=_params("arbitrary"))(*ins)


def _loss_fwd_bwd(y, t):
    S, Dm = y.shape
    ts = _row_tile(S)

    def body(y_ref, t_ref, dy_ref, acc_ref):
        i = pl.program_id(0)
        e = y_ref[...] - t_ref[...]
        dy_ref[...] = e * (1.0 / Dm)

        @pl.when(i == 0)
        def _():
            acc_ref[...] = jnp.zeros_like(acc_ref)

        s = jnp.sum(jnp.sum(e * e, axis=1, keepdims=True), axis=0, keepdims=True)
        acc_ref[...] += s

    row = pl.BlockSpec((ts, Dm), lambda i: (i, 0))
    return pl.pallas_call(
        body, name="loss", grid=(S // ts,), in_specs=[row, row],
        out_specs=[row, pl.BlockSpec((8, LANES), lambda i: (0, 0))],
        out_shape=[jax.ShapeDtypeStruct((S, Dm), F32), jax.ShapeDtypeStruct((8, LANES), F32)],
        compiler_params=_params("arbitrary"))(y, t)


def _log_sigmoid(x):
    return jnp.minimum(x, 0.0) - jnp.log(1.0 + jnp.exp(-jnp.abs(x)))


def _gate_fwd(uf, bpad):
    S = uf.shape[0]
    T = _row_tile(S)

    def body(f_ref, b_ref, c_ref, carry):
        i = pl.program_id(0)

        @pl.when(i == 0)
        def _():
            carry[...] = jnp.zeros_like(carry)

        lf = _log_sigmoid(f_ref[...] + b_ref[...])
        r = lax.broadcasted_iota(jnp.int32, (T, T), 0)
        cidx = lax.broadcasted_iota(jnp.int32, (T, T), 1)
        tri = (cidx <= r).astype(F32)
        c = lax.dot_general(tri, lf, NN, precision=lax.Precision.HIGHEST, preferred_element_type=F32)
        c_ref[...] = c + carry[0:1, :]
        carry[...] = carry[...] + jnp.sum(lf, axis=0, keepdims=True)

    return pl.pallas_call(
        body, name="gate_fwd", grid=(S // T,),
        in_specs=[pl.BlockSpec((T, LANES), lambda i: (i, 4)), pl.BlockSpec((1, LANES), lambda i: (0, 0))],
        out_specs=pl.BlockSpec((T, LANES), lambda i: (i, 0)),
        out_shape=jax.ShapeDtypeStruct((S, LANES), F32),
        scratch_shapes=[pltpu.VMEM((8, LANES), F32)], compiler_params=_params("arbitrary"))(uf, bpad)


def _gate_bwd(dc, uf, bpad):
    S = uf.shape[0]
    T = _row_tile(S)
    nb = S // T

    def body(dc_ref, f_ref, b_ref, df_ref, db_ref, carry):
        i = pl.program_id(0)

        @pl.when(i == 0)
        def _():
            carry[...] = jnp.zeros_like(carry)
            db_ref[...] = jnp.zeros_like(db_ref)

        dcv = dc_ref[...]
        r = lax.broadcasted_iota(jnp.int32, (T, T), 0)
        cidx = lax.broadcasted_iota(jnp.int32, (T, T), 1)
        tri = (cidx >= r).astype(F32)
        dlf = lax.dot_general(tri, dcv, NN, precision=lax.Precision.HIGHEST, preferred_element_type=F32)
        dlf = dlf + carry[0:1, :]
        carry[...] = carry[...] + jnp.sum(dcv, axis=0, keepdims=True)
        fg = f_ref[...] + b_ref[...]
        dfg = dlf / (1.0 + jnp.exp(fg))
        df_ref[...] = dfg.astype(BF16)
        db_ref[...] += jnp.sum(dfg, axis=0, keepdims=True)

    return pl.pallas_call(
        body, name="gate_bwd", grid=(nb,),
        in_specs=[pl.BlockSpec((T, LANES), lambda i: (nb - 1 - i, 0)),
                  pl.BlockSpec((T, LANES), lambda i: (nb - 1 - i, 4)),
                  pl.BlockSpec((1, LANES), lambda i: (0, 0))],
        out_specs=[pl.BlockSpec((T, LANES), lambda i: (nb - 1 - i, 0)), pl.BlockSpec((1, LANES), lambda i: (0, 0))],
        out_shape=[jax.ShapeDtypeStruct((S, LANES), BF16), jax.ShapeDtypeStruct((1, LANES), F32)],
        scratch_shapes=[pltpu.VMEM((8, LANES), F32)], compiler_params=_params("arbitrary"))(dc, uf, bpad)


FOX_CHUNK = 32
FOX_CHUNK_BWD = 64
HEAD_PAIRS = FOX_HEADS // 2
PAIR = 2


def _masked(s, row0, col0, diagonal):
    if diagonal:
        row = row0 + lax.broadcasted_iota(jnp.int32, s.shape, 0)
        col = col0 + lax.broadcasted_iota(jnp.int32, s.shape, 1)
        s = jnp.where(col <= row, s, -jnp.inf)
    return s


def _causal_pairs(n, query_major):
    if query_major:
        pairs = [(q, k) for q in range(n) for k in range(q + 1)]
    else:
        pairs = [(q, k) for k in range(n) for q in range(k, n)]
    return (jnp.asarray([p[0] for p in pairs], jnp.int32), jnp.asarray([p[1] for p in pairs], jnp.int32))


def _lane_block(b):
    return slice(b * LANES, (b + 1) * LANES)


def _fold(op, xs):
    acc = xs[0]
    for x in xs[1:]:
        acc = op(acc, x)
    return acc


def _head_lanes(hh):
    lane = lax.broadcasted_iota(jnp.int32, (1, LANES), 1)
    return (lane < FOX_HEAD_DIM) if hh == 0 else (lane >= FOX_HEAD_DIM)


def _pick(first_head, a, b):
    return jnp.where(first_head, a, b)


def _fox_fwd(qkv, cT, comm):
    S = qkv.shape[0]
    t = _row_tile(S)
    n = S // t
    nc = len(comm)
    scale = 1.0 / math.sqrt(FOX_HEAD_DIM)
    chunk = min(FOX_CHUNK, t)
    per_head = 4
    q_tab, k_tab = _causal_pairs(n, True)
    steps = q_tab.shape[0]

    def body(qt_ref, kt_ref, q_ref, k_ref, v_ref, c_ref, *rest):
        comm_in = rest[:nc]
        o_ref, ob_ref, lse_ref = rest[nc:nc + 3]
        comm_out = rest[nc + 3:2 * nc + 3]
        scr = rest[2 * nc + 3:2 * nc + 3 + PAIR * per_head]
        sems = rest[2 * nc + 3 + PAIR * per_head:]
        hp = pl.program_id(0)
        step_id = pl.program_id(1)
        qi = qt_ref[step_id]
        ki = kt_ref[step_id]

        if nc:
            @pl.when((hp == 0) & (step_id == 0))
            def _():
                _Gather(comm_in, comm_out, *sems).start()

            @pl.when((hp == HEAD_PAIRS - 1) & (step_id == 0))
            def _():
                _Gather(comm_in, comm_out, *sems).pass_on()

        @pl.when(ki == 0)
        def _():
            for hh in range(PAIR):
                m_s, l_s, a_s, acc_s = scr[hh * per_head:hh * per_head + 4]
                m_s[...] = jnp.full_like(m_s, -jnp.inf)
                l_s[...] = jnp.zeros_like(l_s)
                acc_s[...] = jnp.zeros_like(acc_s)

        def step(diagonal):
            q2 = q_ref[...] * scale
            k2 = k_ref[...]
            v2 = v_ref[...]
            scores = []
            for hh in range(PAIR):
                qm = jnp.where(_head_lanes(hh), q2, jnp.zeros_like(q2))
                scores.append(lax.dot_general(qm, k2, NT, preferred_element_type=F32))
            for hh in range(PAIR):
                m_s, l_s, a_s, acc_s = scr[hh * per_head:(hh + 1) * per_head]
                s_s = scores[hh]
                hi_rows, lo_rows = [], []
                for r in range(t // chunk):
                    rows = slice(r * chunk, (r + 1) * chunk)
                    blocks = [_masked(s_s[rows, _lane_block(b)] - c_ref[hh, :, _lane_block(b)], r * chunk,
                                      b * LANES, diagonal) for b in range(t // LANES)]
                    m_prev = m_s[rows, :]
                    m_new = jnp.maximum(m_prev, jnp.max(_fold(jnp.maximum, blocks), axis=1, keepdims=True))
                    alpha = jnp.exp(m_prev - m_new)
                    ps = [jnp.exp(blk - m_new) for blk in blocks]
                    l_s[rows, :] = alpha * l_s[rows, :] + jnp.sum(_fold(jnp.add, ps), axis=1, keepdims=True)
                    m_s[rows, :] = m_new
                    a_s[rows, :] = alpha
                    his = [p.astype(BF16) for p in ps]
                    hi_rows.append(jnp.concatenate(his, axis=1))
                    lo_rows.append(jnp.concatenate([(p - h.astype(F32)).astype(BF16) for p, h in zip(ps, his)],
                                                   axis=1))
                pv = (lax.dot_general(jnp.concatenate(hi_rows, axis=0), v2, NN, preferred_element_type=F32)
                      + lax.dot_general(jnp.concatenate(lo_rows, axis=0), v2, NN, preferred_element_type=F32))
                acc_s[...] = a_s[...] * acc_s[...] + pv

        @pl.when(ki < qi)
        def _():
            step(False)

        @pl.when(ki == qi)
        def _():
            step(True)
            heads = []
            for hh in range(PAIR):
                m_s, l_s, a_s, acc_s = scr[hh * per_head:hh * per_head + 4]
                heads.append(acc_s[...] / l_s[...])
                lse_ref[hh] = m_s[...] + jnp.log(l_s[...])
            o2 = _pick(_head_lanes(0), heads[0], heads[1])
            o_ref[...] = o2
            ob_ref[...] = o2.astype(BF16)

        if nc:
            @pl.when((hp == HEAD_PAIRS - 1) & (step_id == steps - 1))
            def _():
                _Gather(comm_in, comm_out, *sems).finish()

    def q_cols(first_block):
        return pl.BlockSpec((t, LANES), lambda h, s, qt, kt: (qt[s], first_block + h))

    def k_cols(first_block):
        return pl.BlockSpec((t, LANES), lambda h, s, qt, kt: (kt[s], first_block + h))

    any_spec = pl.BlockSpec(memory_space=pl.ANY)
    head_scratch = [pltpu.VMEM((t, LANES), F32)] * per_head
    grid_spec = pltpu.PrefetchScalarGridSpec(
        num_scalar_prefetch=2, grid=(HEAD_PAIRS, steps),
        in_specs=[q_cols(0), k_cols(HEAD_PAIRS), k_cols(2 * HEAD_PAIRS),
                  pl.BlockSpec((PAIR, 1, t), lambda h, s, qt, kt: (h, 0, kt[s]))] + [any_spec] * nc,
        out_specs=[q_cols(0), q_cols(0),
                   pl.BlockSpec((PAIR, t, LANES), lambda h, s, qt, kt: (h, qt[s], 0))] + [any_spec] * nc,
        scratch_shapes=head_scratch * PAIR + _comm_scratch(nc))
    return pl.pallas_call(
        body, name="fox_fwd", grid_spec=grid_spec,
        out_shape=[jax.ShapeDtypeStruct((S, FOX_WIDTH), F32), jax.ShapeDtypeStruct((S, FOX_WIDTH), BF16),
                   jax.ShapeDtypeStruct((FOX_HEADS, S, LANES), F32)] + _comm_shapes(comm),
        compiler_params=_params("arbitrary", "arbitrary"))(q_tab, k_tab, qkv, qkv, qkv, cT, *comm)


def _fox_bwd(qkv, cT, o, lse, do, comm):
    S = qkv.shape[0]
    t = _row_tile(S)
    n = S // t
    nc = len(comm)
    scale = 1.0 / math.sqrt(FOX_HEAD_DIM)
    chunk = min(FOX_CHUNK_BWD, t)
    per_head = 2
    q_tab, k_tab = _causal_pairs(n, False)
    steps = q_tab.shape[0]

    def body(qt_ref, kt_ref, q_ref, k_ref, v_ref, c_ref, o_ref, do_ref, lse_ref, *rest):
        comm_in = rest[:nc]
        dq_ref, dk_ref, dv_ref, dc_ref = rest[nc:nc + 4]
        comm_out = rest[nc + 4:2 * nc + 4]
        dq_s, dk_s, dv_s = rest[2 * nc + 4:2 * nc + 7]
        scr = rest[2 * nc + 7:2 * nc + 7 + PAIR * per_head]
        sems = rest[2 * nc + 7 + PAIR * per_head:]
        hp = pl.program_id(0)
        step_id = pl.program_id(1)
        qi = qt_ref[step_id]
        ki = kt_ref[step_id]

        if nc:
            @pl.when((hp == 0) & (step_id == 0))
            def _():
                for cp in _comm_copies(comm_in, comm_out, *sems):
                    cp.start()

        @pl.when(step_id == 0)
        def _():
            dq_s[...] = jnp.zeros_like(dq_s)

        @pl.when(qi == ki)
        def _():
            dk_s[...] = jnp.zeros_like(dk_s)
            dv_s[...] = jnp.zeros_like(dv_s)
            for hh in range(PAIR):
                dc_s = scr[hh * per_head]
                dc_s[...] = jnp.zeros_like(dc_s)

        def step(diagonal):
            q2 = q_ref[...]
            k2 = k_ref[...]
            v2 = v_ref[...]
            do2 = do_ref[...]
            prod = do2.astype(F32) * o_ref[...]
            grads = []
            for hh in range(PAIR):
                dc_s, delta_s = scr[hh * per_head:(hh + 1) * per_head]
                mine = _head_lanes(hh)
                s_s = lax.dot_general(jnp.where(mine, q2 * scale, jnp.zeros_like(q2)), k2, NT,
                                      preferred_element_type=F32)
                dp_s = lax.dot_general(jnp.where(mine, do2, jnp.zeros_like(do2)), v2, NT, preferred_element_type=F32)
                delta_s[...] = jnp.broadcast_to(jnp.sum(jnp.where(mine, prod, 0.0), axis=1, keepdims=True),
                                                (t, LANES))
                dc8 = [jnp.zeros((8, LANES), F32) for _ in range(t // LANES)]
                p_rows, ds_rows = [], []
                for r in range(t // chunk):
                    rows = slice(r * chunk, (r + 1) * chunk)
                    lse = lse_ref[hh, rows, :]
                    delta = delta_s[rows, :]
                    p_blocks, ds_blocks = [], []
                    for b in range(t // LANES):
                        s = _masked(s_s[rows, _lane_block(b)] - c_ref[hh, :, _lane_block(b)], r * chunk, b * LANES,
                                    diagonal)
                        p = jnp.exp(s - lse)
                        ds = p * (dp_s[rows, _lane_block(b)] - delta)
                        p_blocks.append(p.astype(BF16))
                        ds_blocks.append(ds.astype(BF16))
                        dc8[b] = dc8[b] + jnp.sum(ds.reshape(chunk // 8, 8, LANES), axis=0)
                    p_rows.append(jnp.concatenate(p_blocks, axis=1))
                    ds_rows.append(jnp.concatenate(ds_blocks, axis=1))
                for b in range(t // LANES):
                    dc_s[:, _lane_block(b)] += jnp.sum(dc8[b], axis=0, keepdims=True)
                dsb = jnp.concatenate(ds_rows, axis=0)
                grads.append((lax.dot_general(jnp.concatenate(p_rows, axis=0), do2, TN, preferred_element_type=F32),
                              lax.dot_general(dsb, k2, NN, preferred_element_type=F32),
                              lax.dot_general(dsb, q2, TN, preferred_element_type=F32)))
            first = _head_lanes(0)
            dv_s[...] += _pick(first, grads[0][0], grads[1][0])
            q_rows = pl.ds(pl.multiple_of(qi * t, t), t)
            dq_s[q_rows, :] += _pick(first, grads[0][1], grads[1][1]) * scale
            dk_s[...] += _pick(first, grads[0][2], grads[1][2]) * scale

        @pl.when(qi > ki)
        def _():
            step(False)

        @pl.when(qi == ki)
        def _():
            step(True)

        @pl.when(qi == n - 1)
        def _():
            dk_ref[...] = dk_s[...].astype(BF16)
            dv_ref[...] = dv_s[...].astype(BF16)
            for hh in range(PAIR):
                dc_ref[hh] = -scr[hh * per_head][...]

        @pl.when(step_id == steps - 1)
        def _():
            dq_ref[...] = dq_s[...].astype(BF16)

        if nc:
            @pl.when((hp == HEAD_PAIRS - 1) & (step_id == steps - 1))
            def _():
                for cp in _comm_copies(comm_in, comm_out, *sems):
                    cp.wait()

    def q_side(first_block):
        return pl.BlockSpec((t, LANES), lambda h, s, qt, kt: (qt[s], first_block + h))

    def k_side(first_block):
        return pl.BlockSpec((t, LANES), lambda h, s, qt, kt: (kt[s], first_block + h))

    any_spec = pl.BlockSpec(memory_space=pl.ANY)
    head_scratch = [pltpu.VMEM((1, t), F32), pltpu.VMEM((t, LANES), F32)]
    grad_shape = jax.ShapeDtypeStruct((S, FOX_WIDTH), BF16)
    grid_spec = pltpu.PrefetchScalarGridSpec(
        num_scalar_prefetch=2, grid=(HEAD_PAIRS, steps),
        in_specs=[q_side(0), k_side(HEAD_PAIRS), k_side(2 * HEAD_PAIRS),
                  pl.BlockSpec((PAIR, 1, t), lambda h, s, qt, kt: (h, 0, kt[s])), q_side(0), q_side(0),
                  pl.BlockSpec((PAIR, t, LANES), lambda h, s, qt, kt: (h, qt[s], 0))] + [any_spec] * nc,
        out_specs=[pl.BlockSpec((S, LANES), lambda h, s, qt, kt: (0, h)), k_side(0), k_side(0),
                   pl.BlockSpec((PAIR, 1, t), lambda h, s, qt, kt: (h, 0, kt[s]))] + [any_spec] * nc,
        scratch_shapes=[pltpu.VMEM((S, LANES), F32), pltpu.VMEM((t, LANES), F32), pltpu.VMEM((t, LANES), F32)]
        + head_scratch * PAIR + _comm_scratch(nc))
    return pl.pallas_call(
        body, name="fox_bwd", grid_spec=grid_spec,
        out_shape=[grad_shape, grad_shape, grad_shape, jax.ShapeDtypeStruct((FOX_HEADS, 1, S), F32)]
        + _comm_shapes(comm),
        compiler_params=_params("arbitrary", "arbitrary"))(q_tab, k_tab, qkv, qkv, qkv, cT, o, do, lse, *comm)


def _lanes(g):
    return slice(g * POOL_GROUP_DIM, (g + 1) * POOL_GROUP_DIM)


def _window_sum(e, win, back):
    rows = e.shape[0]
    s = e
    sh = 1
    while sh < win:
        s = s + pltpu.roll(s, sh if back else rows - sh, 0)
        sh *= 2
    return s


def _pooled(u_ref, up_ref, i, g, win, T):
    cur = u_ref[:, _lanes(g)]
    tail = jnp.where(i > 0, up_ref[T - POOL_HALO:T, _lanes(g)], 0.0)
    e = jnp.concatenate([tail, cur], axis=0)
    s = _window_sum(e, win, True)
    t_idx = i * T - POOL_HALO + lax.broadcasted_iota(jnp.int32, (T + POOL_HALO, POOL_GROUP_DIM), 0)
    cnt = jnp.clip(t_idx + 1, 1, win).astype(F32)
    return (s / cnt - e)[POOL_HALO:, :]


def _pool_fwd(uf, pw, ps):
    S = uf.shape[0]
    T = _row_tile(S)

    def body(u_ref, up_ref, w_ref, sc_ref, o_ref):
        i = pl.program_id(0)
        for g, win in enumerate(POOL_WINDOWS):
            pb = _pooled(u_ref, up_ref, i, g, win, T).astype(BF16)
            yv = lax.dot_general(pb, w_ref[g], NN, preferred_element_type=F32)
            o_ref[:, _lanes(g)] = (yv * sc_ref[:, _lanes(g)]).astype(BF16)

    return pl.pallas_call(
        body, name="pool_fwd", grid=(S // T,),
        in_specs=[pl.BlockSpec((T, POOL_WIDTH), lambda i: (i, 0)),
                  pl.BlockSpec((T, POOL_WIDTH), lambda i: (jnp.maximum(i - 1, 0), 0)),
                  pl.BlockSpec((4, POOL_GROUP_DIM, POOL_GROUP_DIM), lambda i: (0, 0, 0)),
                  pl.BlockSpec((1, POOL_WIDTH), lambda i: (0, 0))],
        out_specs=pl.BlockSpec((T, POOL_WIDTH), lambda i: (i, 0)),
        out_shape=jax.ShapeDtypeStruct((S, POOL_WIDTH), BF16), compiler_params=_params("parallel"))(uf, uf, pw, ps)


def _pool_bwd(uf, dpool, pw, ps):
    S = uf.shape[0]
    T = _row_tile(S)
    nb = S // T

    def body(u_ref, up_ref, d_ref, dn_ref, w_ref, sc_ref, du_ref, dw_ref, dsc_ref):
        i = pl.program_id(0)

        @pl.when(i == 0)
        def _():
            dw_ref[...] = jnp.zeros_like(dw_ref)
            dsc_ref[...] = jnp.zeros_like(dsc_ref)

        t_idx = i * T + lax.broadcasted_iota(jnp.int32, (T + POOL_HALO, POOL_GROUP_DIM), 0)
        for g, win in enumerate(POOL_WINDOWS):
            pb = _pooled(u_ref, up_ref, i, g, win, T).astype(BF16)
            w = w_ref[g]
            sc = sc_ref[:, _lanes(g)]
            yv = lax.dot_general(pb, w, NN, preferred_element_type=F32)
            dov = d_ref[:, _lanes(g)]
            dsc_ref[:, _lanes(g)] += jnp.sum(dov * yv, axis=0, keepdims=True)
            head = jnp.where(i < nb - 1, dn_ref[0:POOL_HALO, _lanes(g)], 0.0)
            dyb = (jnp.concatenate([dov, head], axis=0) * sc).astype(BF16)
            dw_ref[g] += lax.dot_general(pb, dyb[:T], TN, preferred_element_type=F32)
            dpl = lax.dot_general(dyb, w, NT, preferred_element_type=F32)
            cnt = jnp.minimum(t_idx + 1, win).astype(F32)
            a = _window_sum(dpl / cnt, win, False)
            du_ref[:, _lanes(g)] = (a - dpl)[:T].astype(BF16)

    return pl.pallas_call(
        body, name="pool_bwd", grid=(nb,),
        in_specs=[pl.BlockSpec((T, POOL_WIDTH), lambda i: (i, 0)),
                  pl.BlockSpec((T, POOL_WIDTH), lambda i: (jnp.maximum(i - 1, 0), 0)),
                  pl.BlockSpec((T, POOL_WIDTH), lambda i: (i, 0)),
                  pl.BlockSpec((T, POOL_WIDTH), lambda i: (jnp.minimum(i + 1, nb - 1), 0)),
                  pl.BlockSpec((4, POOL_GROUP_DIM, POOL_GROUP_DIM), lambda i: (0, 0, 0)),
                  pl.BlockSpec((1, POOL_WIDTH), lambda i: (0, 0))],
        out_specs=[pl.BlockSpec((T, POOL_WIDTH), lambda i: (i, 0)),
                   pl.BlockSpec((4, POOL_GROUP_DIM, POOL_GROUP_DIM), lambda i: (0, 0, 0)),
                   pl.BlockSpec((1, POOL_WIDTH), lambda i: (0, 0))],
        out_shape=[jax.ShapeDtypeStruct((S, POOL_WIDTH), BF16),
                   jax.ShapeDtypeStruct((4, POOL_GROUP_DIM, POOL_GROUP_DIM), F32),
                   jax.ShapeDtypeStruct((1, POOL_WIDTH), F32)],
        compiler_params=_params("arbitrary"))(uf, uf, dpool, dpool, pw, ps)


def _xhead(h):
    return slice(h * X_HEAD_DIM, (h + 1) * X_HEAD_DIM)


def _xvhead(h):
    return slice(D_MODEL + h * X_HEAD_DIM, D_MODEL + (h + 1) * X_HEAD_DIM)


X_CHUNK = 32


def _x_probs(s_ref, rows):
    blocks = [s_ref[rows, _lane_block(b)] * (1.0 / math.sqrt(X_HEAD_DIM)) for b in range(MEM_LEN // LANES)]
    m = jnp.max(_fold(jnp.maximum, blocks), axis=1, keepdims=True)
    es = [jnp.exp(blk - m) for blk in blocks]
    den = jnp.sum(_fold(jnp.add, es), axis=1, keepdims=True)
    return [e / den for e in es]


def _xattn_fwd(q, kv):
    S = q.shape[0]
    t = _row_tile(S)
    chunk = min(X_CHUNK, t)

    def body(q_ref, kv_ref, o_ref):
        for h in range(X_HEADS):
            s = lax.dot_general(q_ref[:, _xhead(h)], kv_ref[:, _xhead(h)], NT, preferred_element_type=F32)
            p_rows = []
            for r in range(t // chunk):
                rows = slice(r * chunk, (r + 1) * chunk)
                p_rows.append(jnp.concatenate([p.astype(BF16) for p in _x_probs(s, rows)], axis=1))
            o_ref[:, _xhead(h)] = lax.dot_general(jnp.concatenate(p_rows, axis=0), kv_ref[:, _xvhead(h)], NN,
                                                  preferred_element_type=F32).astype(BF16)

    return pl.pallas_call(
        body, name="xattn_fwd", grid=(S // t,),
        in_specs=[pl.BlockSpec((t, D_MODEL), lambda i: (i, 0)), pl.BlockSpec((MEM_LEN, 2 * D_MODEL), lambda i: (0, 0))],
        out_specs=pl.BlockSpec((t, D_MODEL), lambda i: (i, 0)),
        out_shape=jax.ShapeDtypeStruct((S, D_MODEL), BF16), compiler_params=_params("parallel"))(q, kv)


def _xattn_bwd(q, kv, do):
    S = q.shape[0]
    t = _row_tile(S)
    nb = S // t
    scale = 1.0 / math.sqrt(X_HEAD_DIM)
    chunk = min(X_CHUNK, t)

    def body(q_ref, kv_ref, do_ref, dq_ref, dkv_ref, acc):
        i = pl.program_id(0)

        @pl.when(i == 0)
        def _():
            acc[...] = jnp.zeros_like(acc)

        for h in range(X_HEADS):
            qh = q_ref[:, _xhead(h)]
            kh = kv_ref[:, _xhead(h)]
            doh = do_ref[:, _xhead(h)]
            s_s = lax.dot_general(qh, kh, NT, preferred_element_type=F32)
            dp_s = lax.dot_general(doh, kv_ref[:, _xvhead(h)], NT, preferred_element_type=F32)
            p_rows, ds_rows = [], []
            for r in range(t // chunk):
                rows = slice(r * chunk, (r + 1) * chunk)
                ps = _x_probs(s_s, rows)
                dps = [dp_s[rows, _lane_block(b)] for b in range(len(ps))]
                inner = jnp.sum(_fold(jnp.add, [dp * p for dp, p in zip(dps, ps)]), axis=1, keepdims=True)
                p_rows.append(jnp.concatenate([p.astype(BF16) for p in ps], axis=1))
                ds_rows.append(jnp.concatenate([(p * (dp - inner)).astype(BF16) for dp, p in zip(dps, ps)], axis=1))
            dsb = jnp.concatenate(ds_rows, axis=0)
            acc[:, _xvhead(h)] += lax.dot_general(jnp.concatenate(p_rows, axis=0), doh, TN,
                                                  preferred_element_type=F32)
            dq_ref[:, _xhead(h)] = (lax.dot_general(dsb, kh, NN, preferred_element_type=F32) * scale).astype(BF16)
            acc[:, _xhead(h)] += lax.dot_general(dsb, qh, TN, preferred_element_type=F32) * scale

        @pl.when(i == nb - 1)
        def _():
            dkv_ref[...] = acc[...].astype(BF16)

    row = pl.BlockSpec((t, D_MODEL), lambda i: (i, 0))
    full = pl.BlockSpec((MEM_LEN, 2 * D_MODEL), lambda i: (0, 0))
    return pl.pallas_call(
        body, name="xattn_bwd", grid=(nb,), in_specs=[row, full, row], out_specs=[row, full],
        out_shape=[jax.ShapeDtypeStruct((S, D_MODEL), BF16), jax.ShapeDtypeStruct((MEM_LEN, 2 * D_MODEL), BF16)],
        scratch_shapes=[pltpu.VMEM((MEM_LEN, 2 * D_MODEL), F32)],
        compiler_params=_params("arbitrary"))(q, kv, do)


def _comm_shapes(arrs):
    return [jax.ShapeDtypeStruct((N_DEV,) + tuple(a.shape[-2:]), a.dtype) for a in arrs]


def _comm_scratch(n):
    if n == 0:
        return []
    return [pltpu.SemaphoreType.DMA((n, N_DEV - 1)), pltpu.SemaphoreType.DMA((n, N_DEV - 1)),
            pltpu.SemaphoreType.DMA((n,))]


def _comm_copies(ins, outs, send_sems, recv_sems, local_sems):
    x, y, c = lax.axis_index("x"), lax.axis_index("y"), lax.axis_index("c")
    me = 4 * x + 2 * y + c
    copies = []
    for w in range(len(ins)):
        src = ins[w] if len(ins[w].shape) == 2 else ins[w].at[me]
        copies.append(pltpu.make_async_copy(src, outs[w].at[me], local_sems.at[w]))
    for k in range(1, N_DEV):
        px = 1 - x if k & 4 else x
        py = 1 - y if k & 2 else y
        pc = 1 - c if k & 1 else c
        peer = 4 * px + 2 * py + pc
        for w in range(len(ins)):
            src = ins[w] if len(ins[w].shape) == 2 else ins[w].at[peer]
            copies.append(pltpu.make_async_remote_copy(
                src_ref=src, dst_ref=outs[w].at[me], send_sem=send_sems.at[w, k - 1],
                recv_sem=recv_sems.at[w, k - 1], device_id=(px, py, pc), device_id_type=pl.DeviceIdType.MESH))
    return copies


class _Gather:
    def __init__(self, ins, outs, send_sems, recv_sems, local_sems):
        x, y, c = lax.axis_index("x"), lax.axis_index("y"), lax.axis_index("c")
        me = 4 * x + 2 * y + c
        sibling = (x, y, 1 - c)
        self.local, self.mine, self.passed = [], [], []
        for w in range(len(ins)):
            def remote(idx, src, slot, dev, w=w):
                return pltpu.make_async_remote_copy(
                    src_ref=src, dst_ref=outs[w].at[slot], send_sem=send_sems.at[w, idx],
                    recv_sem=recv_sems.at[w, idx], device_id=dev, device_id_type=pl.DeviceIdType.MESH)

            self.local.append(pltpu.make_async_copy(ins[w], outs[w].at[me], local_sems.at[w]))
            mine, passed = [remote(0, ins[w], me, sibling)], []
            for j, (fx, fy) in enumerate(((0, 1), (1, 0), (1, 1))):
                px = 1 - x if fx else x
                py = 1 - y if fy else y
                slot = 4 * px + 2 * py + c
                mine.append(remote(1 + j, ins[w], me, (px, py, c)))
                passed.append(remote(4 + j, outs[w].at[slot], slot, sibling))
            self.mine.append(mine)
            self.passed.append(passed)

    def start(self):
        for cp in self.local:
            cp.start()
        for mine in self.mine:
            for cp in mine:
                cp.start()

    def pass_on(self):
        for mine, passed in zip(self.mine, self.passed):
            for j, cp in enumerate(passed):
                mine[1 + j].wait_recv()
                cp.start()

    def finish(self):
        for mine, passed in zip(self.mine, self.passed):
            mine[0].wait_recv()
            for cp in passed:
                cp.wait_recv()
            for cp in mine + passed:
                cp.wait_send()
        for cp in self.local:
            cp.wait()


def _exchange(name, arrs):
    n = len(arrs)
    gather = all(a.ndim == 2 for a in arrs)

    def body(*refs):
        if gather:
            g = _Gather(refs[:n], refs[n:2 * n], *refs[2 * n:])
            g.start()
            g.pass_on()
            g.finish()
            return
        copies = _comm_copies(refs[:n], refs[n:2 * n], *refs[2 * n:])
        for cp in copies:
            cp.start()
        for cp in copies:
            cp.wait()

    any_spec = pl.BlockSpec(memory_space=pl.ANY)
    return pl.pallas_call(
        body, name=name, in_specs=[any_spec] * n, out_specs=[any_spec] * n, out_shape=_comm_shapes(arrs),
        scratch_shapes=_comm_scratch(n))(*arrs)


def _adamw_math(w, g, m, v):
    m = ADAM_B1 * m + (1.0 - ADAM_B1) * g
    v = ADAM_B2 * v + (1.0 - ADAM_B2) * (g * g)
    m_hat = m / (1.0 - ADAM_B1 ** ADAM_STEP)
    v_hat = v / (1.0 - ADAM_B2 ** ADAM_STEP)
    delta = -ADAM_LR * (m_hat / (jnp.sqrt(v_hat) + ADAM_EPS) + ADAM_WD * w)
    return delta, m, v


def _sum_parts(p_ref):
    g = p_ref[0].astype(F32)
    for s in range(1, N_DEV):
        g = g + p_ref[s].astype(F32)
    return g


def _adamw_big(name, w, m, v, parts, tr, comm=()):
    L, R, C = w.shape
    nc = len(comm)
    gather = all(a.ndim == 2 for a in comm)
    nr = R // tr

    def exchange(comm_in, comm_out, sems, begin):
        if gather:
            g = _Gather(comm_in, comm_out, *sems)
            if begin:
                g.start()
            else:
                g.pass_on()
                g.finish()
        else:
            for cp in _comm_copies(comm_in, comm_out, *sems):
                cp.start() if begin else cp.wait()

    def body(w_ref, m_ref, v_ref, *rest):
        p_refs = rest[:L]
        comm_in = rest[L:L + nc]
        g_ref, d_ref, nm_ref, nv_ref = rest[L + nc:L + nc + 4]
        comm_out = rest[L + nc + 4:L + 2 * nc + 4]
        sems = rest[L + 2 * nc + 4:]
        layer = pl.program_id(0)
        if nc:
            @pl.when((layer == 0) & (pl.program_id(1) == 0))
            def _():
                exchange(comm_in, comm_out, sems, True)

        for j in range(L):
            @pl.when(layer == j)
            def _(j=j):
                g = _sum_parts(p_refs[j])
                delta, nm, nv = _adamw_math(w_ref[...], g, m_ref[...], v_ref[...])
                g_ref[...] = g
                d_ref[...] = delta
                nm_ref[...] = nm
                nv_ref[...] = nv

        if nc:
            @pl.when((layer == L - 1) & (pl.program_id(1) == nr - 1))
            def _():
                exchange(comm_in, comm_out, sems, False)

    blk = pl.BlockSpec((None, tr, C), lambda l, i: (l, i, 0))

    def part_spec(j):
        return pl.BlockSpec((N_DEV, tr, C), lambda l, i: (0, jnp.where(l == j, i, 0), 0))

    shp = jax.ShapeDtypeStruct((L, R, C), F32)
    any_spec = pl.BlockSpec(memory_space=pl.ANY)
    return pl.pallas_call(
        body, name=name, grid=(L, nr),
        in_specs=[blk, blk, blk] + [part_spec(j) for j in range(L)] + [any_spec] * nc,
        out_specs=[blk] * 4 + [any_spec] * nc, out_shape=[shp] * 4 + _comm_shapes(comm),
        scratch_shapes=_comm_scratch(nc),
        compiler_params=_params("arbitrary", "arbitrary"))(w, m, v, *parts, *comm)


def _adamw_small(w, m, v, parts):
    R, C = w.shape

    def body(w_ref, m_ref, v_ref, p_ref, g_ref, d_ref, nm_ref, nv_ref):
        g = _sum_parts(p_ref)
        delta, nm, nv = _adamw_math(w_ref[...], g, m_ref[...], v_ref[...])
        g_ref[...] = g
        d_ref[...] = delta
        nm_ref[...] = nm
        nv_ref[...] = nv

    shp = jax.ShapeDtypeStruct((R, C), F32)
    return pl.pallas_call(body, name="adamw_small", out_shape=[shp] * 4,
                          compiler_params=pltpu.CompilerParams(vmem_limit_bytes=VMEM_LIMIT))(w, m, v, parts)


def _vec(a):
    return a.reshape(1, -1)


W_IN_SHARD = IN_COLS // N_DEV
W_IN_ROWS = 272


def _w_in_travel(a):
    pad = [(0, 0)] * (a.ndim - 2) + [(0, W_IN_ROWS - W_IN_SHARD), (0, 0)]
    return jnp.pad(jnp.swapaxes(a, -1, -2), pad)


def _unpack_w_in(g):
    nat = g[:, :W_IN_SHARD, :].reshape(IN_COLS, D_MODEL)
    f = jnp.pad(nat[QKV_COLS:QKV_COLS + FOX_HEADS], ((0, UF_COLS - POOL_WIDTH - FOX_HEADS), (0, 0)))
    return jnp.concatenate([nat[:QKV_COLS], nat[QKV_COLS + FOX_HEADS:], f], axis=0)


def _pack_dw_in(dwp_t):
    nat = jnp.concatenate([dwp_t[:QKV_COLS], dwp_t[QKV_COLS + POOL_WIDTH:QKV_COLS + POOL_WIDTH + FOX_HEADS],
                           dwp_t[QKV_COLS:QKV_COLS + POOL_WIDTH]], axis=0)
    return jnp.pad(nat.reshape(N_DEV, W_IN_SHARD, D_MODEL), ((0, 0), (0, W_IN_ROWS - W_IN_SHARD), (0, 0)))


REST = ['w_out', 'wq_x', 'wkv_x', 'wo_x', 'w_up', 'w_down']


def _layer_fwd(x0, h1, mem, sp, g_in, shards, g_next):
    S = x0.shape[0]
    sv = {"x0": x0}
    w_inp = _unpack_w_in(g_in)
    qkv, uf = _mm_rows("mm_in", [(h1, w_inp, "nt")],
                       [(BF16, 0, QKV_COLS, "id"), (F32, QKV_COLS, UF_COLS, "id")], piece=UF_COLS)
    c = _gate_fwd(uf, sp["b_forget"])
    cT = jnp.transpose(c[:, :FOX_HEADS]).reshape(FOX_HEADS, 1, S)
    o, ob, lse, *got = _fox_fwd(qkv, cT, shards)
    g_out, g_q, g_kv, g_o, g_up, g_down = got[:6]
    W = dict(inp=w_inp, out=g_out.reshape(D_MODEL, D_MODEL), q=g_q.reshape(D_MODEL, D_MODEL), kv=g_kv,
             o=g_o.reshape(D_MODEL, D_MODEL), up=g_up, down=g_down.reshape(D_FF, D_MODEL))
    pool = _pool_fwd(uf, sp["pool_w"], sp["pool_scale"])
    cat = jnp.concatenate([ob, pool], axis=1)
    mix, x1, h2 = _mm_resid_norm("mm_sq_norm", cat, W["out"], x0, sp["g_mix_post"], sp["g_x_pre"])
    mn = _norm_fwd("norm_mem", mem, sp["g_mem"])
    q2 = _mm1("mm_q", h2, W["q"], "nn", D_MODEL, BF16)
    kv = _mm1("mm_kv", mn, W["kv"], "nn3", 2 * D_MODEL, BF16, piece=2 * D_MODEL // N_DEV)
    o2 = _xattn_fwd(q2, kv)
    xo, x2, h3 = _mm_resid_norm("mm_sq_norm", o2, W["o"], x1, sp["g_x_post"], sp["g_ffn_pre"])
    up, act = _mm_rows("mm_up", [(h3, W["up"], "nn3")], [(BF16, 0, D_FF, "id"), (BF16, 0, D_FF, "relu2")],
                       piece=D_FF // N_DEV)
    y, x3, h_next = _mm_resid_norm("mm_down_norm" if g_next is not None else "mm_down_norm_last", act, W["down"], x2,
                                   sp["g_ffn_post"], g_next)
    sv.update(h1=h1, uf=uf, cT=cT, qkv=qkv, o=o, lse=lse, cat=cat, mix=mix, x1=x1, h2=h2, mn=mn, q2=q2, kv=kv,
              o2=o2, xo=xo, x2=x2, h3=h3, up=up, act=act, y=y)
    return x3, h_next, sv, W, (got[6] if len(got) > 6 else None)


def _layer_bwd(dx3, dy, mem, sv, sp, W, carried, below):
    S = dx3.shape[0]
    gs = {}
    gb = {}
    (dup,) = _mm_rows("mm_dup", [(dy, W["down"], "nt")], [(BF16, 0, D_FF, "drelu2")], extra=sv["up"])
    gb["w_down"] = _mm_tn("mm_dw_down", sv["act"], dy, BF16).reshape(N_DEV, D_FF // N_DEV, D_MODEL)
    gb["w_up"] = _mm_tn("mm_dw_up", sv["h3"], dup, BF16, shard_cols=D_FF // N_DEV)
    dx2, gs["g_ffn_pre"], dxo, gs["g_x_post"] = _norm_bwd(
        "mm_dh3_norm_bwd", (dup, W["up"], "nt3"), sv["x2"], sp["g_ffn_pre"], dx3, F32,
        below=(sv["xo"], sp["g_x_post"]))
    do2 = _mm1("mm_sq_t", dxo, W["o"], "nt", D_MODEL, BF16)
    gb["wo_x"] = _mm_tn("mm_dw_sq", sv["o2"], dxo, BF16).reshape(N_DEV, D_MODEL // N_DEV, D_MODEL)
    dq2, dkvb = _xattn_bwd(sv["q2"], sv["kv"], do2)
    gb["wq_x"] = _mm_tn("mm_dw_sq", sv["h2"], dq2, BF16).reshape(N_DEV, D_MODEL // N_DEV, D_MODEL)
    gb["wkv_x"] = _mm_tn("mm_dw_kv", sv["mn"], dkvb, BF16, shard_cols=2 * D_MODEL // N_DEV)
    dmn = _mm1("mm_dmn", dkvb, W["kv"], "nt3", D_MODEL, F32)
    _, gs["g_mem"] = _norm_bwd("norm_bwd_mem", dmn, mem, sp["g_mem"], None, BF16)
    dx1, gs["g_x_pre"], dmix, gs["g_mix_post"] = _norm_bwd(
        "mm_dh2_norm_bwd", (dq2, W["q"], "nt"), sv["x1"], sp["g_x_pre"], dx2, F32,
        below=(sv["mix"], sp["g_mix_post"]))
    doh, dpool = _mm_rows("mm_dcat", [(dmix, W["out"], "nt")],
                          [(BF16, 0, FOX_WIDTH, "id"), (F32, FOX_WIDTH, POOL_WIDTH, "id")])
    gb["w_out"] = _mm_tn("mm_dw_sq", sv["cat"], dmix, BF16).reshape(N_DEV, D_MODEL // N_DEV, D_MODEL)
    du, gs["pool_w"], gs["pool_scale"] = _pool_bwd(sv["uf"], dpool, sp["pool_w"], sp["pool_scale"])
    dq, dk, dv, dcT, *got = _fox_bwd(sv["qkv"], sv["cT"], sv["o"], sv["lse"], doh, [gb[n] for n in REST] + carried)
    dc = jnp.pad(jnp.transpose(dcT.reshape(FOX_HEADS, S)), ((0, 0), (0, LANES - FOX_HEADS)))
    dfg, db = _gate_bwd(dc, sv["uf"], sp["b_forget"])
    gs["b_forget"] = db[:, :FOX_HEADS]
    dproj = [dq, dk, dv, du, dfg]
    dwp = _mm_tn_rows("mm_dw_in", dproj, sv["h1"], BF16)
    dh1 = (dproj, W["inp"], "nn")
    if below is None:
        dx0, gs["g_mix_pre"] = _norm_bwd("mm_dh1_norm_bwd_first", dh1, sv["x0"], sp["g_mix_pre"], dx1, F32)
        lower = None
    else:
        dx0, gs["g_mix_pre"], *lower = _norm_bwd("mm_dh1_norm_bwd", dh1, sv["x0"], sp["g_mix_pre"], dx1, F32,
                                                 below=below)
    return dx0, lower, dict(zip(REST, got[:6])), got[6:], _pack_dw_in(dwp), gs


def _small_rows(shape):
    return -(-math.prod(shape) // (8 * LANES)) * 8


def _pack_small(d):
    blocks = []
    for n in SMALL:
        rows = _small_rows(d[n].shape)
        if d[n].shape[-1] == LANES:
            blocks.append(d[n].reshape(rows, LANES))
        else:
            flat = d[n].reshape(-1)
            blocks.append(jnp.pad(flat, (0, rows * LANES - flat.shape[0])).reshape(rows, LANES))
    return jnp.concatenate(blocks, axis=0)


def _unpack_small(packed, like):
    out = {}
    row = 0
    for n in SMALL:
        shape = like[n].shape
        rows = _small_rows(shape)
        block = packed[row:row + rows]
        out[n] = block.reshape(shape) if shape[-1] == LANES else block.reshape(-1)[:math.prod(shape)].reshape(shape)
        row += rows
    return out


def kernel(x, mem, g_mix_pre, w_in, b_forget, pool_w, pool_scale, w_out, g_mix_post, g_x_pre, g_mem, wq_x, wkv_x, wo_x, g_x_post, g_ffn_pre, w_up, w_down, g_ffn_post, loss_target, m_g_mix_pre, m_w_in, m_b_forget, m_pool_w, m_pool_scale, m_w_out, m_g_mix_post, m_g_x_pre, m_g_mem, m_wq_x, m_wkv_x, m_wo_x, m_g_x_post, m_g_ffn_pre, m_w_up, m_w_down, m_g_ffn_post, v_g_mix_pre, v_w_in, v_b_forget, v_pool_w, v_pool_scale, v_w_out, v_g_mix_post, v_g_x_pre, v_g_mem, v_wq_x, v_wkv_x, v_wo_x, v_g_x_post, v_g_ffn_pre, v_w_up, v_w_down, v_g_ffn_post):
    w = dict(g_mix_pre=g_mix_pre, w_in=w_in, b_forget=b_forget, pool_w=pool_w, pool_scale=pool_scale, w_out=w_out,
             g_mix_post=g_mix_post, g_x_pre=g_x_pre, g_mem=g_mem, wq_x=wq_x, wkv_x=wkv_x, wo_x=wo_x,
             g_x_post=g_x_post, g_ffn_pre=g_ffn_pre, w_up=w_up, w_down=w_down, g_ffn_post=g_ffn_post)
    mom = dict(g_mix_pre=m_g_mix_pre, w_in=m_w_in, b_forget=m_b_forget, pool_w=m_pool_w, pool_scale=m_pool_scale,
               w_out=m_w_out, g_mix_post=m_g_mix_post, g_x_pre=m_g_x_pre, g_mem=m_g_mem, wq_x=m_wq_x,
               wkv_x=m_wkv_x, wo_x=m_wo_x, g_x_post=m_g_x_post, g_ffn_pre=m_g_ffn_pre, w_up=m_w_up,
               w_down=m_w_down, g_ffn_post=m_g_ffn_post)
    var = dict(g_mix_pre=v_g_mix_pre, w_in=v_w_in, b_forget=v_b_forget, pool_w=v_pool_w, pool_scale=v_pool_scale,
               w_out=v_w_out, g_mix_post=v_g_mix_post, g_x_pre=v_g_x_pre, g_mem=v_g_mem, wq_x=v_wq_x,
               wkv_x=v_wkv_x, wo_x=v_wo_x, g_x_post=v_g_x_post, g_ffn_pre=v_g_ffn_pre, w_up=v_w_up,
               w_down=v_w_down, g_ffn_post=v_g_ffn_post)
    S = x.shape[1]
    xs = x.reshape(S, D_MODEL)
    mems = mem.reshape(MEM_LEN, D_MODEL)
    target = loss_target.reshape(S, D_MODEL)

    def small_params(l):
        return dict(
            g_mix_pre=_vec(g_mix_pre[l]), g_mix_post=_vec(g_mix_post[l]), g_x_pre=_vec(g_x_pre[l]),
            g_mem=_vec(g_mem[l]), g_x_post=_vec(g_x_post[l]), g_ffn_pre=_vec(g_ffn_pre[l]),
            g_ffn_post=_vec(g_ffn_post[l]), pool_scale=_vec(pool_scale[l]), pool_w=pool_w[l].astype(BF16),
            b_forget=jnp.pad(_vec(b_forget[l]), ((0, 0), (0, LANES - FOX_HEADS))))

    shard = {n: [w[n][l].astype(BF16) for l in range(DEPTH)] for n in REST}
    shard["w_in"] = [_w_in_travel(w_in[l].astype(BF16)) for l in range(DEPTH)]
    sps = [small_params(l) for l in range(DEPTH)]
    saved, weights = [], []
    h = xs
    (g_in,) = _exchange("gather_w_in", [shard["w_in"][0]])
    hn = _norm_fwd("norm_fwd", xs, sps[0]["g_mix_pre"])
    for l in range(DEPTH):
        travelling = [shard[n][l] for n in REST] + ([shard["w_in"][l + 1]] if l + 1 < DEPTH else [])
        g_next = sps[l + 1]["g_mix_pre"] if l + 1 < DEPTH else None
        h, hn, sv, W, g_in = _layer_fwd(h, hn, mems, sps[l], g_in, travelling, g_next)
        saved.append(sv)
        weights.append(W)
    dh, sq = _loss_fwd_bwd(h, target)
    loss = lax.psum(0.5 * sq[0, 0] / D_MODEL, ("x", "y", "c"))

    parts = [dict() for _ in range(DEPTH)]
    small_grads = [None] * DEPTH
    carried = []
    lower = _norm_bwd("norm_bwd_b", dh, saved[-1]["y"], sps[-1]["g_ffn_post"], None, BF16)
    for l in reversed(range(DEPTH)):
        dy, dg_ffn_post = lower
        below = (saved[l - 1]["y"], sps[l - 1]["g_ffn_post"]) if l > 0 else None
        dh, lower, got, got_carried, dw_in, gs = _layer_bwd(dh, dy, mems, saved[l], sps[l], weights[l], carried, below)
        gs["g_ffn_post"] = dg_ffn_post
        parts[l].update(got)
        if got_carried:
            parts[l + 1]["w_in"] = got_carried[0]
        carried = [dw_in]
        small_grads[l] = gs
    grad_x = dh.reshape(1, S, D_MODEL)

    grads, deltas, new_m, new_v = {}, {}, {}, {}
    rows = dict(w_in=128, w_out=128, wq_x=128, wkv_x=256, wo_x=128, w_up=256, w_down=128)
    sg = {n: jnp.stack([small_grads[l][n].reshape(w[n].shape[1:]) for l in range(DEPTH)]) for n in SMALL}
    riders = dict(w_down=carried, w_up=[_pack_small(sg)])
    for n in ["w_down", "w_up", "w_out", "wq_x", "wkv_x", "wo_x", "w_in"]:
        if n == "w_in":
            for l in range(DEPTH):
                parts[l]["w_in"] = jnp.swapaxes(parts[l]["w_in"][:, :W_IN_SHARD, :], 1, 2)
        grads[n], deltas[n], new_m[n], new_v[n], *got = _adamw_big(
            "adamw_" + n, w[n], mom[n], var[n], [parts[l][n] for l in range(DEPTH)], rows[n], riders.get(n, ()))
        if n == "w_down":
            (parts[0]["w_in"],) = got
        elif n == "w_up":
            (sg_parts,) = got
    outs = _adamw_small(_pack_small(w), _pack_small(mom), _pack_small(var), sg_parts)
    for d, packed in zip((grads, deltas, new_m, new_v), outs):
        d.update(_unpack_small(packed, w))

    return (loss, grad_x, *[grads[n] for n in W_NAMES], *[deltas[n] for n in W_NAMES],
            *[new_m[n] for n in W_NAMES], *[new_v[n] for n in W_NAMES])
```

```python
import math

import jax
import jax.numpy as jnp
from jax import lax
from jax.experimental import pallas as pl
from jax.experimental.pallas import tpu as pltpu

F32 = jnp.float32
BF16 = jnp.bfloat16

D_MODEL = 1024
DEPTH = 4
FOX_WIDTH = 512
FOX_HEADS = 8
FOX_HEAD_DIM = 64
POOL_WIDTH = 512
POOL_WINDOWS = (2, 4, 8, 16)
POOL_GROUP_DIM = 128
POOL_HALO = 16
MEM_LEN = 256
X_HEADS = 4
X_HEAD_DIM = 256
D_FF = 4096
EPS = 1e-6
IN_COLS = 2056
QKV_COLS = 3 * FOX_WIDTH
UF_COLS = 640
INP_COLS = QKV_COLS + UF_COLS
N_DEV = 8
LANES = 128

ADAM_LR = 0.001
ADAM_B1 = 0.9
ADAM_B2 = 0.999
ADAM_EPS = 1e-08
ADAM_WD = 0.01
ADAM_STEP = 10

VMEM_LIMIT = 56 * 1024 * 1024

W_NAMES = ['g_mix_pre', 'w_in', 'b_forget', 'pool_w', 'pool_scale', 'w_out', 'g_mix_post', 'g_x_pre', 'g_mem',
           'wq_x', 'wkv_x', 'wo_x', 'g_x_post', 'g_ffn_pre', 'w_up', 'w_down', 'g_ffn_post']
BIG = ['w_in', 'w_out', 'wq_x', 'wkv_x', 'wo_x', 'w_up', 'w_down']
SMALL = [n for n in W_NAMES if n not in BIG]

NN = (((1,), (0,)), ((), ()))
NT = (((1,), (1,)), ((), ()))
TN = (((0,), (0,)), ((), ()))


def _params(*sem):
    return pltpu.CompilerParams(dimension_semantics=sem, vmem_limit_bytes=VMEM_LIMIT)


def _row_tile(s):
    return min(s, 512)


def _product(a_ref, w_ref, kind, c0, pw):
    cols = slice(c0, c0 + pw)
    if kind == "nn":
        return lax.dot_general(a_ref[...], w_ref[:, cols], NN, preferred_element_type=F32)
    if kind == "nt":
        return lax.dot_general(a_ref[...], w_ref[cols, :], NT, preferred_element_type=F32)
    n = w_ref.shape[2]
    if kind == "nn3":
        assert pw == n and c0 % n == 0
        return lax.dot_general(a_ref[...], w_ref[c0 // n], NN, preferred_element_type=F32)
    r = None
    for j in range(w_ref.shape[0]):
        part = lax.dot_general(a_ref[:, j * n:(j + 1) * n], w_ref[j, cols, :], NT, preferred_element_type=F32)
        r = part if r is None else r + part
    return r


def _resident(w):
    return pl.BlockSpec(w.shape, lambda i, nd=w.ndim: (0,) * nd)


def _mm_rows(name, terms, outs, extra=None, piece=1024):
    M = terms[0][0].shape[0]
    tm = _row_tile(M)
    nterm = len(terms)
    n_extra = 0 if extra is None else 1
    groups = {}
    for idx, (_, c0, width, fn) in enumerate(outs):
        groups.setdefault((c0, width), []).append((idx, fn))

    def body(*refs):
        a_refs = refs[0:2 * nterm:2]
        w_refs = refs[1:2 * nterm:2]
        extra_refs = refs[2 * nterm:2 * nterm + n_extra]
        out_refs = refs[2 * nterm + n_extra:]
        for (g0, gw), members in groups.items():
            for c0 in range(g0, g0 + gw, piece):
                pw = min(piece, g0 + gw - c0)
                r = None
                for a_ref, w_ref, (_, w, kind) in zip(a_refs, w_refs, terms):
                    part = _product(a_ref, w_ref, kind, c0, pw)
                    r = part if r is None else r + part
                dst = slice(c0 - g0, c0 - g0 + pw)
                for idx, fn in members:
                    if fn == "relu2":
                        rp = jnp.maximum(r, 0.0)
                        val = rp * rp
                    elif fn == "drelu2":
                        val = r * (2.0 * jnp.maximum(extra_refs[0][:, dst].astype(F32), 0.0))
                    else:
                        val = r
                    out_refs[idx][:, dst] = val.astype(out_refs[idx].dtype)

    in_specs, ins = [], []
    for a, w, _ in terms:
        in_specs.append(pl.BlockSpec((tm, a.shape[1]), lambda i: (i, 0)))
        in_specs.append(pl.BlockSpec(w.shape, lambda i, nd=w.ndim: (0,) * nd))
        ins += [a, w]
    if extra is not None:
        in_specs.append(pl.BlockSpec((tm, extra.shape[1]), lambda i: (i, 0)))
        ins.append(extra)
    res = pl.pallas_call(
        body, name=name, grid=(M // tm,), in_specs=in_specs,
        out_specs=[pl.BlockSpec((tm, width), lambda i: (i, 0)) for _, _, width, _ in outs],
        out_shape=[jax.ShapeDtypeStruct((M, width), dt) for dt, _, width, _ in outs],
        compiler_params=_params("parallel"))(*ins)
    return res


def _mm1(name, a, w, kind, n_cols, dtype, piece=1024):
    return _mm_rows(name, [(a, w, kind)], [(dtype, 0, n_cols, "id")], piece=piece)[0]


def _mm_tn(name, a, b, out_dtype, shard_cols=None, piece=512):
    K, M = a.shape
    b_parts = b if isinstance(b, list) else [b]
    nb = len(b_parts)
    N = sum(p.shape[1] for p in b_parts)
    tk = _row_tile(K)
    nk = K // tk
    piece = shard_cols or min(piece, N)

    def body(a_ref, *rest):
        b_refs = rest[:nb]
        o_ref, acc = rest[nb:]
        k = pl.program_id(0)

        @pl.when(k == 0)
        def _():
            acc[...] = jnp.zeros_like(acc)

        a_t = jnp.transpose(a_ref[...])
        if nb == 1:
            for c0 in range(0, N, piece):
                cols = slice(c0, min(c0 + piece, N))
                acc[:, cols] += lax.dot_general(a_t, b_refs[0][:, cols], NN, preferred_element_type=F32)
        else:
            c0 = 0
            for b_ref in b_refs:
                cols = slice(c0, c0 + b_ref.shape[1])
                acc[:, cols] += lax.dot_general(a_t, b_ref[...], NN, preferred_element_type=F32)
                c0 += b_ref.shape[1]

        @pl.when(k == nk - 1)
        def _():
            for c0 in range(0, N, piece):
                cols = slice(c0, min(c0 + piece, N))
                if shard_cols:
                    o_ref[c0 // piece] = acc[:, cols].astype(o_ref.dtype)
                else:
                    o_ref[:, cols] = acc[:, cols].astype(o_ref.dtype)

    out_dims = (N // shard_cols, M, shard_cols) if shard_cols else (M, N)
    return pl.pallas_call(
        body, name=name, grid=(nk,),
        in_specs=[pl.BlockSpec((tk, M), lambda k: (k, 0))]
        + [pl.BlockSpec((tk, p.shape[1]), lambda k: (k, 0)) for p in b_parts],
        out_specs=pl.BlockSpec(out_dims, lambda k, nd=len(out_dims): (0,) * nd),
        out_shape=jax.ShapeDtypeStruct(out_dims, out_dtype),
        scratch_shapes=[pltpu.VMEM((M, N), F32)],
        compiler_params=_params("arbitrary"))(a, *b_parts)


def _mm_tn_rows(name, a_parts, b, out_dtype):
    K = b.shape[0]
    N = b.shape[1]
    na = len(a_parts)
    M = sum(p.shape[1] for p in a_parts)
    tk = _row_tile(K)
    nk = K // tk

    def body(*refs):
        a_refs = refs[:na]
        b_ref, o_ref, acc = refs[na:]
        k = pl.program_id(0)

        @pl.when(k == 0)
        def _():
            acc[...] = jnp.zeros_like(acc)

        bv = b_ref[...]
        r0 = 0
        for a_ref in a_refs:
            rows = slice(r0, r0 + a_ref.shape[1])
            acc[rows, :] += lax.dot_general(jnp.transpose(a_ref[...]), bv, NN, preferred_element_type=F32)
            r0 += a_ref.shape[1]

        @pl.when(k == nk - 1)
        def _():
            o_ref[...] = acc[...].astype(o_ref.dtype)

    return pl.pallas_call(
        body, name=name, grid=(nk,),
        in_specs=[pl.BlockSpec((tk, p.shape[1]), lambda k: (k, 0)) for p in a_parts]
        + [pl.BlockSpec((tk, N), lambda k: (k, 0))],
        out_specs=pl.BlockSpec((M, N), lambda k: (0, 0)), out_shape=jax.ShapeDtypeStruct((M, N), out_dtype),
        scratch_shapes=[pltpu.VMEM((M, N), F32)], compiler_params=_params("arbitrary"))(*a_parts, b)


def _norm_fwd(name, x, g):
    S, Dm = x.shape
    ts = _row_tile(S)

    def body(x_ref, g_ref, h_ref):
        xv = x_ref[...]
        r = lax.rsqrt(jnp.mean(xv * xv, axis=-1, keepdims=True) + EPS)
        h_ref[...] = ((xv * r) * g_ref[...]).astype(BF16)

    return pl.pallas_call(
        body, name=name, grid=(S // ts,),
        in_specs=[pl.BlockSpec((ts, Dm), lambda i: (i, 0)), pl.BlockSpec((1, Dm), lambda i: (0, 0))],
        out_specs=pl.BlockSpec((ts, Dm), lambda i: (i, 0)),
        out_shape=jax.ShapeDtypeStruct((S, Dm), BF16), compiler_params=_params("parallel"))(x, g)


def _mm_resid_norm(name, a, w, x, g, g_next):
    S, Dm = x.shape
    ts = _row_tile(S)
    has_next = g_next is not None

    def body(a_ref, w_ref, x_ref, g_ref, *rest):
        fv = _product(a_ref, w_ref, "nn", 0, Dm)
        r = lax.rsqrt(jnp.mean(fv * fv, axis=-1, keepdims=True) + EPS)
        xn = x_ref[...] + (fv * r) * g_ref[...]
        if has_next:
            gn_ref, f_ref, o_ref, h_ref = rest
            rn = lax.rsqrt(jnp.mean(xn * xn, axis=-1, keepdims=True) + EPS)
            h_ref[...] = ((xn * rn) * gn_ref[...]).astype(BF16)
        else:
            f_ref, o_ref = rest
        f_ref[...] = fv
        o_ref[...] = xn

    row = pl.BlockSpec((ts, Dm), lambda i: (i, 0))
    vec = pl.BlockSpec((1, Dm), lambda i: (0, 0))
    ins = [a, w, x, g] + ([g_next] if has_next else [])
    f32_rows = jax.ShapeDtypeStruct((S, Dm), F32)
    res = pl.pallas_call(
        body, name=name, grid=(S // ts,),
        in_specs=[pl.BlockSpec((ts, a.shape[1]), lambda i: (i, 0)), _resident(w), row, vec] + ([vec] if has_next else []),
        out_specs=[row, row] + ([row] if has_next else []),
        out_shape=[f32_rows, f32_rows] + ([jax.ShapeDtypeStruct((S, Dm), BF16)] if has_next else []),
        compiler_params=_params("parallel"))(*ins)
    return (res[0], res[1], res[2]) if has_next else (res[0], res[1], None)


def _rms_bwd(dov, yv, g):
    r = lax.rsqrt(jnp.mean(yv * yv, axis=-1, keepdims=True) + EPS)
    z = dov * g
    yr = yv * r
    return r * (z - yr * jnp.mean(yr * z, axis=-1, keepdims=True)), jnp.sum(dov * yr, axis=0, keepdims=True)


def _norm_bwd(name, dout, y, g, resid, out_dtype, below=None):
    S, Dm = y.shape
    ts = _row_tile(S)
    has_resid = resid is not None
    chained = below is not None
    produced = isinstance(dout, tuple)
    kind = dout[2] if produced else None
    a_parts = (dout[0] if isinstance(dout[0], list) else [dout[0]]) if produced else []
    n_a = len(a_parts)

    def body(*refs):
        refs = list(refs)
        if produced and n_a == 1:
            dov = _product(refs[0], refs[1], kind, 0, Dm)
            refs = refs[1:]
        elif produced:
            w_ref = refs[n_a]
            dov, k0 = None, 0
            for a_ref in refs[:n_a]:
                k1 = k0 + a_ref.shape[1]
                if kind == "nt":
                    part = lax.dot_general(a_ref[...], w_ref[:, k0:k1], NT, preferred_element_type=F32)
                else:
                    part = lax.dot_general(a_ref[...], w_ref[k0:k1, :], NN, preferred_element_type=F32)
                dov = part if dov is None else dov + part
                k0 = k1
            refs = refs[n_a:]
        else:
            dov = refs[0][...]
        y_ref, g_ref = refs[1:3]
        pos = 3
        r_ref = refs[pos] if has_resid else None
        pos += has_resid
        if chained:
            f_ref, gf_ref = refs[pos:pos + 2]
            pos += 2
        dy_ref, dg_ref = refs[pos:pos + 2]
        i = pl.program_id(0)
        dy, dg = _rms_bwd(dov, y_ref[...], g_ref[...])
        if has_resid:
            dy = dy + r_ref[...]
        dy_ref[...] = dy.astype(out_dtype)

        @pl.when(i == 0)
        def _():
            for ref in refs[pos + 1::2]:
                ref[...] = jnp.zeros_like(ref)

        dg_ref[...] += dg
        if chained:
            df_ref, dgf_ref = refs[pos + 2:pos + 4]
            df, dgf = _rms_bwd(dy, f_ref[...], gf_ref[...])
            df_ref[...] = df.astype(BF16)
            dgf_ref[...] += dgf

    row = pl.BlockSpec((ts, Dm), lambda i: (i, 0))
    vec = pl.BlockSpec((1, Dm), lambda i: (0, 0))
    if produced:
        assert n_a == 1 or kind in ("nt", "nn")
        ins = a_parts + [dout[1]]
        specs = [pl.BlockSpec((ts, a.shape[1]), lambda i: (i, 0)) for a in a_parts] + [_resident(dout[1])]
    else:
        ins = [dout]
        specs = [row]
    ins += [y, g] + ([resid] if has_resid else []) + (list(below) if chained else [])
    specs += [row, vec] + ([row] if has_resid else []) + ([row, vec] if chained else [])
    vec_shape = jax.ShapeDtypeStruct((1, Dm), F32)
    return pl.pallas_call(
        body, name=name, grid=(S // ts,), in_specs=specs, out_specs=[row, vec] + ([row, vec] if chained else []),
        out_shape=[jax.ShapeDtypeStruct((S, Dm), out_dtype), vec_shape]
        + ([jax.ShapeDtypeStruct((S, Dm), BF16), vec_shape] if chained else []),
        compiler_params=_params("arbitrary"))(*ins)


def _loss_fwd_bwd(y, t):
    S, Dm = y.shape
    ts = _row_tile(S)

    def body(y_ref, t_ref, dy_ref, acc_ref):
        i = pl.program_id(0)
        e = y_ref[...] - t_ref[...]
        dy_ref[...] = e * (1.0 / Dm)

        @pl.when(i == 0)
        def _():
            acc_ref[...] = jnp.zeros_like(acc_ref)

        s = jnp.sum(jnp.sum(e * e, axis=1, keepdims=True), axis=0, keepdims=True)
        acc_ref[...] += s

    row = pl.BlockSpec((ts, Dm), lambda i: (i, 0))
    return pl.pallas_call(
        body, name="loss", grid=(S // ts,), in_specs=[row, row],
        out_specs=[row, pl.BlockSpec((8, LANES), lambda i: (0, 0))],
        out_shape=[jax.ShapeDtypeStruct((S, Dm), F32), jax.ShapeDtypeStruct((8, LANES), F32)],
        compiler_params=_params("arbitrary"))(y, t)


def _log_sigmoid(x):
    return jnp.minimum(x, 0.0) - jnp.log(1.0 + jnp.exp(-jnp.abs(x)))


def _gate_fwd(uf, bpad):
    S = uf.shape[0]
    T = _row_tile(S)

    def body(f_ref, b_ref, c_ref, carry):
        i = pl.program_id(0)

        @pl.when(i == 0)
        def _():
            carry[...] = jnp.zeros_like(carry)

        lf = _log_sigmoid(f_ref[...] + b_ref[...])
        r = lax.broadcasted_iota(jnp.int32, (T, T), 0)
        cidx = lax.broadcasted_iota(jnp.int32, (T, T), 1)
        tri = (cidx <= r).astype(F32)
        c = lax.dot_general(tri, lf, NN, precision=lax.Precision.HIGHEST, preferred_element_type=F32)
        c_ref[...] = c + carry[0:1, :]
        carry[...] = carry[...] + jnp.sum(lf, axis=0, keepdims=True)

    return pl.pallas_call(
        body, name="gate_fwd", grid=(S // T,),
        in_specs=[pl.BlockSpec((T, LANES), lambda i: (i, 4)), pl.BlockSpec((1, LANES), lambda i: (0, 0))],
        out_specs=pl.BlockSpec((T, LANES), lambda i: (i, 0)),
        out_shape=jax.ShapeDtypeStruct((S, LANES), F32),
        scratch_shapes=[pltpu.VMEM((8, LANES), F32)], compiler_params=_params("arbitrary"))(uf, bpad)


def _gate_bwd(dc, uf, bpad):
    S = uf.shape[0]
    T = _row_tile(S)
    nb = S // T

    def body(dc_ref, f_ref, b_ref, df_ref, db_ref, carry):
        i = pl.program_id(0)

        @pl.when(i == 0)
        def _():
            carry[...] = jnp.zeros_like(carry)
            db_ref[...] = jnp.zeros_like(db_ref)

        dcv = dc_ref[...]
        r = lax.broadcasted_iota(jnp.int32, (T, T), 0)
        cidx = lax.broadcasted_iota(jnp.int32, (T, T), 1)
        tri = (cidx >= r).astype(F32)
        dlf = lax.dot_general(tri, dcv, NN, precision=lax.Precision.HIGHEST, preferred_element_type=F32)
        dlf = dlf + carry[0:1, :]
        carry[...] = carry[...] + jnp.sum(dcv, axis=0, keepdims=True)
        fg = f_ref[...] + b_ref[...]
        dfg = dlf / (1.0 + jnp.exp(fg))
        df_ref[...] = dfg.astype(BF16)
        db_ref[...] += jnp.sum(dfg, axis=0, keepdims=True)

    return pl.pallas_call(
        body, name="gate_bwd", grid=(nb,),
        in_specs=[pl.BlockSpec((T, LANES), lambda i: (nb - 1 - i, 0)),
                  pl.BlockSpec((T, LANES), lambda i: (nb - 1 - i, 4)),
                  pl.BlockSpec((1, LANES), lambda i: (0, 0))],
        out_specs=[pl.BlockSpec((T, LANES), lambda i: (nb - 1 - i, 0)), pl.BlockSpec((1, LANES), lambda i: (0, 0))],
        out_shape=[jax.ShapeDtypeStruct((S, LANES), BF16), jax.ShapeDtypeStruct((1, LANES), F32)],
        scratch_shapes=[pltpu.VMEM((8, LANES), F32)], compiler_params=_params("arbitrary"))(dc, uf, bpad)


FOX_CHUNK = 32
FOX_CHUNK_BWD = 64
HEAD_PAIRS = FOX_HEADS // 2
PAIR = 2


def _masked(s, row0, col0, diagonal):
    if diagonal:
        row = row0 + lax.broadcasted_iota(jnp.int32, s.shape, 0)
        col = col0 + lax.broadcasted_iota(jnp.int32, s.shape, 1)
        s = jnp.where(col <= row, s, -jnp.inf)
    return s


def _causal_pairs(n, query_major):
    if query_major:
        pairs = [(q, k) for q in range(n) for k in range(q + 1)]
    else:
        pairs = [(q, k) for k in range(n) for q in range(k, n)]
    return (jnp.asarray([p[0] for p in pairs], jnp.int32), jnp.asarray([p[1] for p in pairs], jnp.int32))


def _lane_block(b):
    return slice(b * LANES, (b + 1) * LANES)


def _fold(op, xs):
    acc = xs[0]
    for x in xs[1:]:
        acc = op(acc, x)
    return acc


def _head_lanes(hh):
    lane = lax.broadcasted_iota(jnp.int32, (1, LANES), 1)
    return (lane < FOX_HEAD_DIM) if hh == 0 else (lane >= FOX_HEAD_DIM)


def _pick(first_head, a, b):
    return jnp.where(first_head, a, b)


def _fox_fwd(qkv, cT, comm):
    S = qkv.shape[0]
    t = _row_tile(S)
    n = S // t
    nc = len(comm)
    scale = 1.0 / math.sqrt(FOX_HEAD_DIM)
    chunk = min(FOX_CHUNK, t)
    per_head = 4
    q_tab, k_tab = _causal_pairs(n, True)
    steps = q_tab.shape[0]

    def body(qt_ref, kt_ref, q_ref, k_ref, v_ref, c_ref, *rest):
        comm_in = rest[:nc]
        o_ref, ob_ref, lse_ref = rest[nc:nc + 3]
        comm_out = rest[nc + 3:2 * nc + 3]
        scr = rest[2 * nc + 3:2 * nc + 3 + PAIR * per_head]
        sems = rest[2 * nc + 3 + PAIR * per_head:]
        hp = pl.program_id(0)
        step_id = pl.program_id(1)
        qi = qt_ref[step_id]
        ki = kt_ref[step_id]

        if nc:
            @pl.when((hp == 0) & (step_id == 0))
            def _():
                _Gather(comm_in, comm_out, *sems).start()

            @pl.when((hp == HEAD_PAIRS - 1) & (step_id == 0))
            def _():
                _Gather(comm_in, comm_out, *sems).pass_on()

        @pl.when(ki == 0)
        def _():
            for hh in range(PAIR):
                m_s, l_s, a_s, acc_s = scr[hh * per_head:hh * per_head + 4]
                m_s[...] = jnp.full_like(m_s, -jnp.inf)
                l_s[...] = jnp.zeros_like(l_s)
                acc_s[...] = jnp.zeros_like(acc_s)

        def step(diagonal):
            q2 = q_ref[...] * scale
            k2 = k_ref[...]
            v2 = v_ref[...]
            scores = []
            for hh in range(PAIR):
                qm = jnp.where(_head_lanes(hh), q2, jnp.zeros_like(q2))
                scores.append(lax.dot_general(qm, k2, NT, preferred_element_type=F32))
            for hh in range(PAIR):
                m_s, l_s, a_s, acc_s = scr[hh * per_head:(hh + 1) * per_head]
                s_s = scores[hh]
                hi_rows, lo_rows = [], []
                for r in range(t // chunk):
                    rows = slice(r * chunk, (r + 1) * chunk)
                    blocks = [_masked(s_s[rows, _lane_block(b)] - c_ref[hh, :, _lane_block(b)], r * chunk,
                                      b * LANES, diagonal) for b in range(t // LANES)]
                    m_prev = m_s[rows, :]
                    m_new = jnp.maximum(m_prev, jnp.max(_fold(jnp.maximum, blocks), axis=1, keepdims=True))
                    alpha = jnp.exp(m_prev - m_new)
                    ps = [jnp.exp(blk - m_new) for blk in blocks]
                    l_s[rows, :] = alpha * l_s[rows, :] + jnp.sum(_fold(jnp.add, ps), axis=1, keepdims=True)
                    m_s[rows, :] = m_new
                    a_s[rows, :] = alpha
                    his = [p.astype(BF16) for p in ps]
                    hi_rows.append(jnp.concatenate(his, axis=1))
                    lo_rows.append(jnp.concatenate([(p - h.astype(F32)).astype(BF16) for p, h in zip(ps, his)],
                                                   axis=1))
                pv = (lax.dot_general(jnp.concatenate(hi_rows, axis=0), v2, NN, preferred_element_type=F32)
                      + lax.dot_general(jnp.concatenate(lo_rows, axis=0), v2, NN, preferred_element_type=F32))
                acc_s[...] = a_s[...] * acc_s[...] + pv

        @pl.when(ki < qi)
        def _():
            step(False)

        @pl.when(ki == qi)
        def _():
            step(True)
            heads = []
            for hh in range(PAIR):
                m_s, l_s, a_s, acc_s = scr[hh * per_head:hh * per_head + 4]
                heads.append(acc_s[...] / l_s[...])
                lse_ref[hh] = m_s[...] + jnp.log(l_s[...])
            o2 = _pick(_head_lanes(0), heads[0], heads[1])
            o_ref[...] = o2
            ob_ref[...] = o2.astype(BF16)

        if nc:
            @pl.when((hp == HEAD_PAIRS - 1) & (step_id == steps - 1))
            def _():
                _Gather(comm_in, comm_out, *sems).finish()

    def q_cols(first_block):
        return pl.BlockSpec((t, LANES), lambda h, s, qt, kt: (qt[s], first_block + h))

    def k_cols(first_block):
        return pl.BlockSpec((t, LANES), lambda h, s, qt, kt: (kt[s], first_block + h))

    any_spec = pl.BlockSpec(memory_space=pl.ANY)
    head_scratch = [pltpu.VMEM((t, LANES), F32)] * per_head
    grid_spec = pltpu.PrefetchScalarGridSpec(
        num_scalar_prefetch=2, grid=(HEAD_PAIRS, steps),
        in_specs=[q_cols(0), k_cols(HEAD_PAIRS), k_cols(2 * HEAD_PAIRS),
                  pl.BlockSpec((PAIR, 1, t), lambda h, s, qt, kt: (h, 0, kt[s]))] + [any_spec] * nc,
        out_specs=[q_cols(0), q_cols(0),
                   pl.BlockSpec((PAIR, t, LANES), lambda h, s, qt, kt: (h, qt[s], 0))] + [any_spec] * nc,
        scratch_shapes=head_scratch * PAIR + _comm_scratch(nc))
    return pl.pallas_call(
        body, name="fox_fwd", grid_spec=grid_spec,
        out_shape=[jax.ShapeDtypeStruct((S, FOX_WIDTH), F32), jax.ShapeDtypeStruct((S, FOX_WIDTH), BF16),
                   jax.ShapeDtypeStruct((FOX_HEADS, S, LANES), F32)] + _comm_shapes(comm),
        compiler_params=_params("arbitrary", "arbitrary"))(q_tab, k_tab, qkv, qkv, qkv, cT, *comm)


def _fox_bwd(qkv, cT, o, lse, do, comm):
    S = qkv.shape[0]
    t = _row_tile(S)
    n = S // t
    nc = len(comm)
    scale = 1.0 / math.sqrt(FOX_HEAD_DIM)
    chunk = min(FOX_CHUNK_BWD, t)
    per_head = 2
    q_tab, k_tab = _causal_pairs(n, False)
    steps = q_tab.shape[0]

    def body(qt_ref, kt_ref, q_ref, k_ref, v_ref, c_ref, o_ref, do_ref, lse_ref, *rest):
        comm_in = rest[:nc]
        dq_ref, dk_ref, dv_ref, dc_ref = rest[nc:nc + 4]
        comm_out = rest[nc + 4:2 * nc + 4]
        dq_s, dk_s, dv_s = rest[2 * nc + 4:2 * nc + 7]
        scr = rest[2 * nc + 7:2 * nc + 7 + PAIR * per_head]
        sems = rest[2 * nc + 7 + PAIR * per_head:]
        hp = pl.program_id(0)
        step_id = pl.program_id(1)
        qi = qt_ref[step_id]
        ki = kt_ref[step_id]

        if nc:
            @pl.when((hp == 0) & (step_id == 0))
            def _():
                for cp in _comm_copies(comm_in, comm_out, *sems):
                    cp.start()

        @pl.when(step_id == 0)
        def _():
            dq_s[...] = jnp.zeros_like(dq_s)

        @pl.when(qi == ki)
        def _():
            dk_s[...] = jnp.zeros_like(dk_s)
            dv_s[...] = jnp.zeros_like(dv_s)
            for hh in range(PAIR):
                dc_s = scr[hh * per_head]
                dc_s[...] = jnp.zeros_like(dc_s)

        def step(diagonal):
            q2 = q_ref[...]
            k2 = k_ref[...]
            v2 = v_ref[...]
            do2 = do_ref[...]
            prod = do2.astype(F32) * o_ref[...]
            grads = []
            for hh in range(PAIR):
                dc_s, delta_s = scr[hh * per_head:(hh + 1) * per_head]
                mine = _head_lanes(hh)
                s_s = lax.dot_general(jnp.where(mine, q2 * scale, jnp.zeros_like(q2)), k2, NT,
                                      preferred_element_type=F32)
                dp_s = lax.dot_general(jnp.where(mine, do2, jnp.zeros_like(do2)), v2, NT, preferred_element_type=F32)
                delta_s[...] = jnp.broadcast_to(jnp.sum(jnp.where(mine, prod, 0.0), axis=1, keepdims=True),
                                                (t, LANES))
                dc8 = [jnp.zeros((8, LANES), F32) for _ in range(t // LANES)]
                p_rows, ds_rows = [], []
                for r in range(t // chunk):
                    rows = slice(r * chunk, (r + 1) * chunk)
                    lse = lse_ref[hh, rows, :]
                    delta = delta_s[rows, :]
                    p_blocks, ds_blocks = [], []
                    for b in range(t // LANES):
                        s = _masked(s_s[rows, _lane_block(b)] - c_ref[hh, :, _lane_block(b)], r * chunk, b * LANES,
                                    diagonal)
                        p = jnp.exp(s - lse)
                        ds = p * (dp_s[rows, _lane_block(b)] - delta)
                        p_blocks.append(p.astype(BF16))
                        ds_blocks.append(ds.astype(BF16))
                        dc8[b] = dc8[b] + jnp.sum(ds.reshape(chunk // 8, 8, LANES), axis=0)
                    p_rows.append(jnp.concatenate(p_blocks, axis=1))
                    ds_rows.append(jnp.concatenate(ds_blocks, axis=1))
                for b in range(t // LANES):
                    dc_s[:, _lane_block(b)] += jnp.sum(dc8[b], axis=0, keepdims=True)
                dsb = jnp.concatenate(ds_rows, axis=0)
                grads.append((lax.dot_general(jnp.concatenate(p_rows, axis=0), do2, TN, preferred_element_type=F32),
                              lax.dot_general(dsb, k2, NN, preferred_element_type=F32),
                              lax.dot_general(dsb, q2, TN, preferred_element_type=F32)))
            first = _head_lanes(0)
            dv_s[...] += _pick(first, grads[0][0], grads[1][0])
            q_rows = pl.ds(pl.multiple_of(qi * t, t), t)
            dq_s[q_rows, :] += _pick(first, grads[0][1], grads[1][1]) * scale
            dk_s[...] += _pick(first, grads[0][2], grads[1][2]) * scale

        @pl.when(qi > ki)
        def _():
            step(False)

        @pl.when(qi == ki)
        def _():
            step(True)

        @pl.when(qi == n - 1)
        def _():
            dk_ref[...] = dk_s[...].astype(BF16)
            dv_ref[...] = dv_s[...].astype(BF16)
            for hh in range(PAIR):
                dc_ref[hh] = -scr[hh * per_head][...]

        @pl.when(step_id == steps - 1)
        def _():
            dq_ref[...] = dq_s[...].astype(BF16)

        if nc:
            @pl.when((hp == HEAD_PAIRS - 1) & (step_id == steps - 1))
            def _():
                for cp in _comm_copies(comm_in, comm_out, *sems):
                    cp.wait()

    def q_side(first_block):
        return pl.BlockSpec((t, LANES), lambda h, s, qt, kt: (qt[s], first_block + h))

    def k_side(first_block):
        return pl.BlockSpec((t, LANES), lambda h, s, qt, kt: (kt[s], first_block + h))

    any_spec = pl.BlockSpec(memory_space=pl.ANY)
    head_scratch = [pltpu.VMEM((1, t), F32), pltpu.VMEM((t, LANES), F32)]
    grad_shape = jax.ShapeDtypeStruct((S, FOX_WIDTH), BF16)
    grid_spec = pltpu.PrefetchScalarGridSpec(
        num_scalar_prefetch=2, grid=(HEAD_PAIRS, steps),
        in_specs=[q_side(0), k_side(HEAD_PAIRS), k_side(2 * HEAD_PAIRS),
                  pl.BlockSpec((PAIR, 1, t), lambda h, s, qt, kt: (h, 0, kt[s])), q_side(0), q_side(0),
                  pl.BlockSpec((PAIR, t, LANES), lambda h, s, qt, kt: (h, qt[s], 0))] + [any_spec] * nc,
        out_specs=[pl.BlockSpec((S, LANES), lambda h, s, qt, kt: (0, h)), k_side(0), k_side(0),
                   pl.BlockSpec((PAIR, 1, t), lambda h, s, qt, kt: (h, 0, kt[s]))] + [any_spec] * nc,
        scratch_shapes=[pltpu.VMEM((S, LANES), F32), pltpu.VMEM((t, LANES), F32), pltpu.VMEM((t, LANES), F32)]
        + head_scratch * PAIR + _comm_scratch(nc))
    return pl.pallas_call(
        body, name="fox_bwd", grid_spec=grid_spec,
        out_shape=[grad_shape, grad_shape, grad_shape, jax.ShapeDtypeStruct((FOX_HEADS, 1, S), F32)]
        + _comm_shapes(comm),
        compiler_params=_params("arbitrary", "arbitrary"))(q_tab, k_tab, qkv, qkv, qkv, cT, o, do, lse, *comm)


def _lanes(g):
    return slice(g * POOL_GROUP_DIM, (g + 1) * POOL_GROUP_DIM)


def _window_sum(e, win, back):
    rows = e.shape[0]
    s = e
    sh = 1
    while sh < win:
        s = s + pltpu.roll(s, sh if back else rows - sh, 0)
        sh *= 2
    return s


def _pooled(u_ref, up_ref, i, g, win, T):
    cur = u_ref[:, _lanes(g)]
    tail = jnp.where(i > 0, up_ref[T - POOL_HALO:T, _lanes(g)], 0.0)
    e = jnp.concatenate([tail, cur], axis=0)
    s = _window_sum(e, win, True)
    t_idx = i * T - POOL_HALO + lax.broadcasted_iota(jnp.int32, (T + POOL_HALO, POOL_GROUP_DIM), 0)
    cnt = jnp.clip(t_idx + 1, 1, win).astype(F32)
    return (s / cnt - e)[POOL_HALO:, :]


def _pool_fwd(uf, pw, ps):
    S = uf.shape[0]
    T = _row_tile(S)

    def body(u_ref, up_ref, w_ref, sc_ref, o_ref):
        i = pl.program_id(0)
        for g, win in enumerate(POOL_WINDOWS):
            pb = _pooled(u_ref, up_ref, i, g, win, T).astype(BF16)
            yv = lax.dot_general(pb, w_ref[g], NN, preferred_element_type=F32)
            o_ref[:, _lanes(g)] = (yv * sc_ref[:, _lanes(g)]).astype(BF16)

    return pl.pallas_call(
        body, name="pool_fwd", grid=(S // T,),
        in_specs=[pl.BlockSpec((T, POOL_WIDTH), lambda i: (i, 0)),
                  pl.BlockSpec((T, POOL_WIDTH), lambda i: (jnp.maximum(i - 1, 0), 0)),
                  pl.BlockSpec((4, POOL_GROUP_DIM, POOL_GROUP_DIM), lambda i: (0, 0, 0)),
                  pl.BlockSpec((1, POOL_WIDTH), lambda i: (0, 0))],
        out_specs=pl.BlockSpec((T, POOL_WIDTH), lambda i: (i, 0)),
        out_shape=jax.ShapeDtypeStruct((S, POOL_WIDTH), BF16), compiler_params=_params("parallel"))(uf, uf, pw, ps)


def _pool_bwd(uf, dpool, pw, ps):
    S = uf.shape[0]
    T = _row_tile(S)
    nb = S // T

    def body(u_ref, up_ref, d_ref, dn_ref, w_ref, sc_ref, du_ref, dw_ref, dsc_ref):
        i = pl.program_id(0)

        @pl.when(i == 0)
        def _():
            dw_ref[...] = jnp.zeros_like(dw_ref)
            dsc_ref[...] = jnp.zeros_like(dsc_ref)

        t_idx = i * T + lax.broadcasted_iota(jnp.int32, (T + POOL_HALO, POOL_GROUP_DIM), 0)
        for g, win in enumerate(POOL_WINDOWS):
            pb = _pooled(u_ref, up_ref, i, g, win, T).astype(BF16)
            w = w_ref[g]
            sc = sc_ref[:, _lanes(g)]
            yv = lax.dot_general(pb, w, NN, preferred_element_type=F32)
            dov = d_ref[:, _lanes(g)]
            dsc_ref[:, _lanes(g)] += jnp.sum(dov * yv, axis=0, keepdims=True)
            head = jnp.where(i < nb - 1, dn_ref[0:POOL_HALO, _lanes(g)], 0.0)
            dyb = (jnp.concatenate([dov, head], axis=0) * sc).astype(BF16)
            dw_ref[g] += lax.dot_general(pb, dyb[:T], TN, preferred_element_type=F32)
            dpl = lax.dot_general(dyb, w, NT, preferred_element_type=F32)
            cnt = jnp.minimum(t_idx + 1, win).astype(F32)
            a = _window_sum(dpl / cnt, win, False)
            du_ref[:, _lanes(g)] = (a - dpl)[:T].astype(BF16)

    return pl.pallas_call(
        body, name="pool_bwd", grid=(nb,),
        in_specs=[pl.BlockSpec((T, POOL_WIDTH), lambda i: (i, 0)),
                  pl.BlockSpec((T, POOL_WIDTH), lambda i: (jnp.maximum(i - 1, 0), 0)),
                  pl.BlockSpec((T, POOL_WIDTH), lambda i: (i, 0)),
                  pl.BlockSpec((T, POOL_WIDTH), lambda i: (jnp.minimum(i + 1, nb - 1), 0)),
                  pl.BlockSpec((4, POOL_GROUP_DIM, POOL_GROUP_DIM), lambda i: (0, 0, 0)),
                  pl.BlockSpec((1, POOL_WIDTH), lambda i: (0, 0))],
        out_specs=[pl.BlockSpec((T, POOL_WIDTH), lambda i: (i, 0)),
                   pl.BlockSpec((4, POOL_GROUP_DIM, POOL_GROUP_DIM), lambda i: (0, 0, 0)),
                   pl.BlockSpec((1, POOL_WIDTH), lambda i: (0, 0))],
        out_shape=[jax.ShapeDtypeStruct((S, POOL_WIDTH), BF16),
                   jax.ShapeDtypeStruct((4, POOL_GROUP_DIM, POOL_GROUP_DIM), F32),
                   jax.ShapeDtypeStruct((1, POOL_WIDTH), F32)],
        compiler_params=_params("arbitrary"))(uf, uf, dpool, dpool, pw, ps)


def _xhead(h):
    return slice(h * X_HEAD_DIM, (h + 1) * X_HEAD_DIM)


def _xvhead(h):
    return slice(D_MODEL + h * X_HEAD_DIM, D_MODEL + (h + 1) * X_HEAD_DIM)


X_CHUNK = 32


def _x_probs(s_ref, rows):
    blocks = [s_ref[rows, _lane_block(b)] * (1.0 / math.sqrt(X_HEAD_DIM)) for b in range(MEM_LEN // LANES)]
    m = jnp.max(_fold(jnp.maximum, blocks), axis=1, keepdims=True)
    es = [jnp.exp(blk - m) for blk in blocks]
    den = jnp.sum(_fold(jnp.add, es), axis=1, keepdims=True)
    return [e / den for e in es]


def _xattn_fwd(q, kv):
    S = q.shape[0]
    t = _row_tile(S)
    chunk = min(X_CHUNK, t)

    def body(q_ref, kv_ref, o_ref):
        for h in range(X_HEADS):
            s = lax.dot_general(q_ref[:, _xhead(h)], kv_ref[:, _xhead(h)], NT, preferred_element_type=F32)
            p_rows = []
            for r in range(t // chunk):
                rows = slice(r * chunk, (r + 1) * chunk)
                p_rows.append(jnp.concatenate([p.astype(BF16) for p in _x_probs(s, rows)], axis=1))
            o_ref[:, _xhead(h)] = lax.dot_general(jnp.concatenate(p_rows, axis=0), kv_ref[:, _xvhead(h)], NN,
                                                  preferred_element_type=F32).astype(BF16)

    return pl.pallas_call(
        body, name="xattn_fwd", grid=(S // t,),
        in_specs=[pl.BlockSpec((t, D_MODEL), lambda i: (i, 0)), pl.BlockSpec((MEM_LEN, 2 * D_MODEL), lambda i: (0, 0))],
        out_specs=pl.BlockSpec((t, D_MODEL), lambda i: (i, 0)),
        out_shape=jax.ShapeDtypeStruct((S, D_MODEL), BF16), compiler_params=_params("parallel"))(q, kv)


def _xattn_bwd(q, kv, do):
    S = q.shape[0]
    t = _row_tile(S)
    nb = S // t
    scale = 1.0 / math.sqrt(X_HEAD_DIM)
    chunk = min(X_CHUNK, t)

    def body(q_ref, kv_ref, do_ref, dq_ref, dkv_ref, acc):
        i = pl.program_id(0)

        @pl.when(i == 0)
        def _():
            acc[...] = jnp.zeros_like(acc)

        for h in range(X_HEADS):
            qh = q_ref[:, _xhead(h)]
            kh = kv_ref[:, _xhead(h)]
            doh = do_ref[:, _xhead(h)]
            s_s = lax.dot_general(qh, kh, NT, preferred_element_type=F32)
            dp_s = lax.dot_general(doh, kv_ref[:, _xvhead(h)], NT, preferred_element_type=F32)
            p_rows, ds_rows = [], []
            for r in range(t // chunk):
                rows = slice(r * chunk, (r + 1) * chunk)
                ps = _x_probs(s_s, rows)
                dps = [dp_s[rows, _lane_block(b)] for b in range(len(ps))]
                inner = jnp.sum(_fold(jnp.add, [dp * p for dp, p in zip(dps, ps)]), axis=1, keepdims=True)
                p_rows.append(jnp.concatenate([p.astype(BF16) for p in ps], axis=1))
                ds_rows.append(jnp.concatenate([(p * (dp - inner)).astype(BF16) for dp, p in zip(dps, ps)], axis=1))
            dsb = jnp.concatenate(ds_rows, axis=0)
            acc[:, _xvhead(h)] += lax.dot_general(jnp.concatenate(p_rows, axis=0), doh, TN,
                                                  preferred_element_type=F32)
            dq_ref[:, _xhead(h)] = (lax.dot_general(dsb, kh, NN, preferred_element_type=F32) * scale).astype(BF16)
            acc[:, _xhead(h)] += lax.dot_general(dsb, qh, TN, preferred_element_type=F32) * scale

        @pl.when(i == nb - 1)
        def _():
            dkv_ref[...] = acc[...].astype(BF16)

    row = pl.BlockSpec((t, D_MODEL), lambda i: (i, 0))
    full = pl.BlockSpec((MEM_LEN, 2 * D_MODEL), lambda i: (0, 0))
    return pl.pallas_call(
        body, name="xattn_bwd", grid=(nb,), in_specs=[row, full, row], out_specs=[row, full],
        out_shape=[jax.ShapeDtypeStruct((S, D_MODEL), BF16), jax.ShapeDtypeStruct((MEM_LEN, 2 * D_MODEL), BF16)],
        scratch_shapes=[pltpu.VMEM((MEM_LEN, 2 * D_MODEL), F32)],
        compiler_params=_params("arbitrary"))(q, kv, do)


def _comm_shapes(arrs):
    return [jax.ShapeDtypeStruct((N_DEV,) + tuple(a.shape[-2:]), a.dtype) for a in arrs]


def _comm_scratch(n):
    if n == 0:
        return []
    return [pltpu.SemaphoreType.DMA((n, N_DEV - 1)), pltpu.SemaphoreType.DMA((n, N_DEV - 1)),
            pltpu.SemaphoreType.DMA((n,))]


def _comm_copies(ins, outs, send_sems, recv_sems, local_sems):
    x, y, c = lax.axis_index("x"), lax.axis_index("y"), lax.axis_index("c")
    me = 4 * x + 2 * y + c
    copies = []
    for w in range(len(ins)):
        src = ins[w] if len(ins[w].shape) == 2 else ins[w].at[me]
        copies.append(pltpu.make_async_copy(src, outs[w].at[me], local_sems.at[w]))
    for k in range(1, N_DEV):
        px = 1 - x if k & 4 else x
        py = 1 - y if k & 2 else y
        pc = 1 - c if k & 1 else c
        peer = 4 * px + 2 * py + pc
        for w in range(len(ins)):
            src = ins[w] if len(ins[w].shape) == 2 else ins[w].at[peer]
            copies.append(pltpu.make_async_remote_copy(
                src_ref=src, dst_ref=outs[w].at[me], send_sem=send_sems.at[w, k - 1],
                recv_sem=recv_sems.at[w, k - 1], device_id=(px, py, pc), device_id_type=pl.DeviceIdType.MESH))
    return copies


class _Gather:
    def __init__(self, ins, outs, send_sems, recv_sems, local_sems):
        x, y, c = lax.axis_index("x"), lax.axis_index("y"), lax.axis_index("c")
        me = 4 * x + 2 * y + c
        sibling = (x, y, 1 - c)
        self.local, self.mine, self.passed = [], [], []
        for w in range(len(ins)):
            def remote(idx, src, slot, dev, w=w):
                return pltpu.make_async_remote_copy(
                    src_ref=src, dst_ref=outs[w].at[slot], send_sem=send_sems.at[w, idx],
                    recv_sem=recv_sems.at[w, idx], device_id=dev, device_id_type=pl.DeviceIdType.MESH)

            self.local.append(pltpu.make_async_copy(ins[w], outs[w].at[me], local_sems.at[w]))
            mine, passed = [remote(0, ins[w], me, sibling)], []
            for j, (fx, fy) in enumerate(((0, 1), (1, 0), (1, 1))):
                px = 1 - x if fx else x
                py = 1 - y if fy else y
                slot = 4 * px + 2 * py + c
                mine.append(remote(1 + j, ins[w], me, (px, py, c)))
                passed.append(remote(4 + j, outs[w].at[slot], slot, sibling))
            self.mine.append(mine)
            self.passed.append(passed)

    def start(self):
        for cp in self.local:
            cp.start()
        for mine in self.mine:
            for cp in mine:
                cp.start()

    def pass_on(self):
        for mine, passed in zip(self.mine, self.passed):
            for j, cp in enumerate(passed):
                mine[1 + j].wait_recv()
                cp.start()

    def finish(self):
        for mine, passed in zip(self.mine, self.passed):
            mine[0].wait_recv()
            for cp in passed:
                cp.wait_recv()
            for cp in mine + passed:
                cp.wait_send()
        for cp in self.local:
            cp.wait()


def _exchange(name, arrs):
    n = len(arrs)
    gather = all(a.ndim == 2 for a in arrs)

    def body(*refs):
        if gather:
            g = _Gather(refs[:n], refs[n:2 * n], *refs[2 * n:])
            g.start()
            g.pass_on()
            g.finish()
            return
        copies = _comm_copies(refs[:n], refs[n:2 * n], *refs[2 * n:])
        for cp in copies:
            cp.start()
        for cp in copies:
            cp.wait()

    any_spec = pl.BlockSpec(memory_space=pl.ANY)
    return pl.pallas_call(
        body, name=name, in_specs=[any_spec] * n, out_specs=[any_spec] * n, out_shape=_comm_shapes(arrs),
        scratch_shapes=_comm_scratch(n))(*arrs)


def _adamw_math(w, g, m, v):
    m = ADAM_B1 * m + (1.0 - ADAM_B1) * g
    v = ADAM_B2 * v + (1.0 - ADAM_B2) * (g * g)
    m_hat = m / (1.0 - ADAM_B1 ** ADAM_STEP)
    v_hat = v / (1.0 - ADAM_B2 ** ADAM_STEP)
    delta = -ADAM_LR * (m_hat / (jnp.sqrt(v_hat) + ADAM_EPS) + ADAM_WD * w)
    return delta, m, v


def _sum_parts(p_ref):
    g = p_ref[0].astype(F32)
    for s in range(1, N_DEV):
        g = g + p_ref[s].astype(F32)
    return g


def _adamw_big(name, w, m, v, parts, tr, comm=()):
    L, R, C = w.shape
    nc = len(comm)
    gather = all(a.ndim == 2 for a in comm)
    nr = R // tr

    def exchange(comm_in, comm_out, sems, begin):
        if gather:
            g = _Gather(comm_in, comm_out, *sems)
            if begin:
                g.start()
            else:
                g.pass_on()
                g.finish()
        else:
            for cp in _comm_copies(comm_in, comm_out, *sems):
                cp.start() if begin else cp.wait()

    def body(w_ref, m_ref, v_ref, *rest):
        p_refs = rest[:L]
        comm_in = rest[L:L + nc]
        g_ref, d_ref, nm_ref, nv_ref = rest[L + nc:L + nc + 4]
        comm_out = rest[L + nc + 4:L + 2 * nc + 4]
        sems = rest[L + 2 * nc + 4:]
        layer = pl.program_id(0)
        if nc:
            @pl.when((layer == 0) & (pl.program_id(1) == 0))
            def _():
                exchange(comm_in, comm_out, sems, True)

        for j in range(L):
            @pl.when(layer == j)
            def _(j=j):
                g = _sum_parts(p_refs[j])
                delta, nm, nv = _adamw_math(w_ref[...], g, m_ref[...], v_ref[...])
                g_ref[...] = g
                d_ref[...] = delta
                nm_ref[...] = nm
                nv_ref[...] = nv

        if nc:
            @pl.when((layer == L - 1) & (pl.program_id(1) == nr - 1))
            def _():
                exchange(comm_in, comm_out, sems, False)

    blk = pl.BlockSpec((None, tr, C), lambda l, i: (l, i, 0))

    def part_spec(j):
        return pl.BlockSpec((N_DEV, tr, C), lambda l, i: (0, jnp.where(l == j, i, 0), 0))

    shp = jax.ShapeDtypeStruct((L, R, C), F32)
    any_spec = pl.BlockSpec(memory_space=pl.ANY)
    return pl.pallas_call(
        body, name=name, grid=(L, nr),
        in_specs=[blk, blk, blk] + [part_spec(j) for j in range(L)] + [any_spec] * nc,
        out_specs=[blk] * 4 + [any_spec] * nc, out_shape=[shp] * 4 + _comm_shapes(comm),
        scratch_shapes=_comm_scratch(nc),
        compiler_params=_params("arbitrary", "arbitrary"))(w, m, v, *parts, *comm)


def _adamw_small(w, m, v, parts):
    R, C = w.shape

    def body(w_ref, m_ref, v_ref, p_ref, g_ref, d_ref, nm_ref, nv_ref):
        g = _sum_parts(p_ref)
        delta, nm, nv = _adamw_math(w_ref[...], g, m_ref[...], v_ref[...])
        g_ref[...] = g
        d_ref[...] = delta
        nm_ref[...] = nm
        nv_ref[...] = nv

    shp = jax.ShapeDtypeStruct((R, C), F32)
    return pl.pallas_call(body, name="adamw_small", out_shape=[shp] * 4,
                          compiler_params=pltpu.CompilerParams(vmem_limit_bytes=VMEM_LIMIT))(w, m, v, parts)


def _vec(a):
    return a.reshape(1, -1)


W_IN_SHARD = IN_COLS // N_DEV
W_IN_ROWS = 272


def _w_in_travel(a):
    pad = [(0, 0)] * (a.ndim - 2) + [(0, W_IN_ROWS - W_IN_SHARD), (0, 0)]
    return jnp.pad(jnp.swapaxes(a, -1, -2), pad)


def _unpack_w_in(g):
    nat = g[:, :W_IN_SHARD, :].reshape(IN_COLS, D_MODEL)
    f = jnp.pad(nat[QKV_COLS:QKV_COLS + FOX_HEADS], ((0, UF_COLS - POOL_WIDTH - FOX_HEADS), (0, 0)))
    return jnp.concatenate([nat[:QKV_COLS], nat[QKV_COLS + FOX_HEADS:], f], axis=0)


def _pack_dw_in(dwp_t):
    nat = jnp.concatenate([dwp_t[:QKV_COLS], dwp_t[QKV_COLS + POOL_WIDTH:QKV_COLS + POOL_WIDTH + FOX_HEADS],
                           dwp_t[QKV_COLS:QKV_COLS + POOL_WIDTH]], axis=0)
    return jnp.pad(nat.reshape(N_DEV, W_IN_SHARD, D_MODEL), ((0, 0), (0, W_IN_ROWS - W_IN_SHARD), (0, 0)))


REST = ['w_out', 'wq_x', 'wkv_x', 'wo_x', 'w_up', 'w_down']


def _layer_fwd(x0, h1, mem, sp, g_in, shards, g_next):
    S = x0.shape[0]
    sv = {"x0": x0}
    w_inp = _unpack_w_in(g_in)
    qkv, uf = _mm_rows("mm_in", [(h1, w_inp, "nt")],
                       [(BF16, 0, QKV_COLS, "id"), (F32, QKV_COLS, UF_COLS, "id")], piece=UF_COLS)
    c = _gate_fwd(uf, sp["b_forget"])
    cT = jnp.transpose(c[:, :FOX_HEADS]).reshape(FOX_HEADS, 1, S)
    o, ob, lse, *got = _fox_fwd(qkv, cT, shards)
    g_out, g_q, g_kv, g_o, g_up, g_down = got[:6]
    W = dict(inp=w_inp, out=g_out.reshape(D_MODEL, D_MODEL), q=g_q.reshape(D_MODEL, D_MODEL), kv=g_kv,
             o=g_o.reshape(D_MODEL, D_MODEL), up=g_up, down=g_down.reshape(D_FF, D_MODEL))
    pool = _pool_fwd(uf, sp["pool_w"], sp["pool_scale"])
    cat = jnp.concatenate([ob, pool], axis=1)
    mix, x1, h2 = _mm_resid_norm("mm_sq_norm", cat, W["out"], x0, sp["g_mix_post"], sp["g_x_pre"])
    mn = _norm_fwd("norm_mem", mem, sp["g_mem"])
    q2 = _mm1("mm_q", h2, W["q"], "nn", D_MODEL, BF16)
    kv = _mm1("mm_kv", mn, W["kv"], "nn3", 2 * D_MODEL, BF16, piece=2 * D_MODEL // N_DEV)
    o2 = _xattn_fwd(q2, kv)
    xo, x2, h3 = _mm_resid_norm("mm_sq_norm", o2, W["o"], x1, sp["g_x_post"], sp["g_ffn_pre"])
    up, act = _mm_rows("mm_up", [(h3, W["up"], "nn3")], [(BF16, 0, D_FF, "id"), (BF16, 0, D_FF, "relu2")],
                       piece=D_FF // N_DEV)
    y, x3, h_next = _mm_resid_norm("mm_down_norm" if g_next is not None else "mm_down_norm_last", act, W["down"], x2,
                                   sp["g_ffn_post"], g_next)
    sv.update(h1=h1, uf=uf, cT=cT, qkv=qkv, o=o, lse=lse, cat=cat, mix=mix, x1=x1, h2=h2, mn=mn, q2=q2, kv=kv,
              o2=o2, xo=xo, x2=x2, h3=h3, up=up, act=act, y=y)
    return x3, h_next, sv, W, (got[6] if len(got) > 6 else None)


def _layer_bwd(dx3, dy, mem, sv, sp, W, carried, below):
    S = dx3.shape[0]
    gs = {}
    gb = {}
    (dup,) = _mm_rows("mm_dup", [(dy, W["down"], "nt")], [(BF16, 0, D_FF, "drelu2")], extra=sv["up"])
    gb["w_down"] = _mm_tn("mm_dw_down", sv["act"], dy, BF16).reshape(N_DEV, D_FF // N_DEV, D_MODEL)
    gb["w_up"] = _mm_tn("mm_dw_up", sv["h3"], dup, BF16, shard_cols=D_FF // N_DEV)
    dx2, gs["g_ffn_pre"], dxo, gs["g_x_post"] = _norm_bwd(
        "mm_dh3_norm_bwd", (dup, W["up"], "nt3"), sv["x2"], sp["g_ffn_pre"], dx3, F32,
        below=(sv["xo"], sp["g_x_post"]))
    do2 = _mm1("mm_sq_t", dxo, W["o"], "nt", D_MODEL, BF16)
    gb["wo_x"] = _mm_tn("mm_dw_sq", sv["o2"], dxo, BF16).reshape(N_DEV, D_MODEL // N_DEV, D_MODEL)
    dq2, dkvb = _xattn_bwd(sv["q2"], sv["kv"], do2)
    gb["wq_x"] = _mm_tn("mm_dw_sq", sv["h2"], dq2, BF16).reshape(N_DEV, D_MODEL // N_DEV, D_MODEL)
    gb["wkv_x"] = _mm_tn("mm_dw_kv", sv["mn"], dkvb, BF16, shard_cols=2 * D_MODEL // N_DEV)
    dmn = _mm1("mm_dmn", dkvb, W["kv"], "nt3", D_MODEL, F32)
    _, gs["g_mem"] = _norm_bwd("norm_bwd_mem", dmn, mem, sp["g_mem"], None, BF16)
    dx1, gs["g_x_pre"], dmix, gs["g_mix_post"] = _norm_bwd(
        "mm_dh2_norm_bwd", (dq2, W["q"], "nt"), sv["x1"], sp["g_x_pre"], dx2, F32,
        below=(sv["mix"], sp["g_mix_post"]))
    doh, dpool = _mm_rows("mm_dcat", [(dmix, W["out"], "nt")],
                          [(BF16, 0, FOX_WIDTH, "id"), (F32, FOX_WIDTH, POOL_WIDTH, "id")])
    gb["w_out"] = _mm_tn("mm_dw_sq", sv["cat"], dmix, BF16).reshape(N_DEV, D_MODEL // N_DEV, D_MODEL)
    du, gs["pool_w"], gs["pool_scale"] = _pool_bwd(sv["uf"], dpool, sp["pool_w"], sp["pool_scale"])
    dq, dk, dv, dcT, *got = _fox_bwd(sv["qkv"], sv["cT"], sv["o"], sv["lse"], doh, [gb[n] for n in REST] + carried)
    dc = jnp.pad(jnp.transpose(dcT.reshape(FOX_HEADS, S)), ((0, 0), (0, LANES - FOX_HEADS)))
    dfg, db = _gate_bwd(dc, sv["uf"], sp["b_forget"])
    gs["b_forget"] = db[:, :FOX_HEADS]
    dproj = [dq, dk, dv, du, dfg]
    dwp = _mm_tn_rows("mm_dw_in", dproj, sv["h1"], BF16)
    dh1 = (dproj, W["inp"], "nn")
    if below is None:
        dx0, gs["g_mix_pre"] = _norm_bwd("mm_dh1_norm_bwd_first", dh1, sv["x0"], sp["g_mix_pre"], dx1, F32)
        lower = None
    else:
        dx0, gs["g_mix_pre"], *lower = _norm_bwd("mm_dh1_norm_bwd", dh1, sv["x0"], sp["g_mix_pre"], dx1, F32,
                                                 below=below)
    return dx0, lower, dict(zip(REST, got[:6])), got[6:], _pack_dw_in(dwp), gs


def _small_rows(shape):
    return -(-math.prod(shape) // (8 * LANES)) * 8


def _pack_small(d):
    blocks = []
    for n in SMALL:
        rows = _small_rows(d[n].shape)
        if d[n].shape[-1] == LANES:
            blocks.append(d[n].reshape(rows, LANES))
        else:
            flat = d[n].reshape(-1)
            blocks.append(jnp.pad(flat, (0, rows * LANES - flat.shape[0])).reshape(rows, LANES))
    return jnp.concatenate(blocks, axis=0)


def _unpack_small(packed, like):
    out = {}
    row = 0
    for n in SMALL:
        shape = like[n].shape
        rows = _small_rows(shape)
        block = packed[row:row + rows]
        out[n] = block.reshape(shape) if shape[-1] == LANES else block.reshape(-1)[:math.prod(shape)].reshape(shape)
        row += rows
    return out


def kernel(x, mem, g_mix_pre, w_in, b_forget, pool_w, pool_scale, w_out, g_mix_post, g_x_pre, g_mem, wq_x, wkv_x, wo_x, g_x_post, g_ffn_pre, w_up, w_down, g_ffn_post, loss_target, m_g_mix_pre, m_w_in, m_b_forget, m_pool_w, m_pool_scale, m_w_out, m_g_mix_post, m_g_x_pre, m_g_mem, m_wq_x, m_wkv_x, m_wo_x, m_g_x_post, m_g_ffn_pre, m_w_up, m_w_down, m_g_ffn_post, v_g_mix_pre, v_w_in, v_b_forget, v_pool_w, v_pool_scale, v_w_out, v_g_mix_post, v_g_x_pre, v_g_mem, v_wq_x, v_wkv_x, v_wo_x, v_g_x_post, v_g_ffn_pre, v_w_up, v_w_down, v_g_ffn_post):
    w = dict(g_mix_pre=g_mix_pre, w_in=w_in, b_forget=b_forget, pool_w=pool_w, pool_scale=pool_scale, w_out=w_out,
             g_mix_post=g_mix_post, g_x_pre=g_x_pre, g_mem=g_mem, wq_x=wq_x, wkv_x=wkv_x, wo_x=wo_x,
             g_x_post=g_x_post, g_ffn_pre=g_ffn_pre, w_up=w_up, w_down=w_down, g_ffn_post=g_ffn_post)
    mom = dict(g_mix_pre=m_g_mix_pre, w_in=m_w_in, b_forget=m_b_forget, pool_w=m_pool_w, pool_scale=m_pool_scale,
               w_out=m_w_out, g_mix_post=m_g_mix_post, g_x_pre=m_g_x_pre, g_mem=m_g_mem, wq_x=m_wq_x,
               wkv_x=m_wkv_x, wo_x=m_wo_x, g_x_post=m_g_x_post, g_ffn_pre=m_g_ffn_pre, w_up=m_w_up,
               w_down=m_w_down, g_ffn_post=m_g_ffn_post)
    var = dict(g_mix_pre=v_g_mix_pre, w_in=v_w_in, b_forget=v_b_forget, pool_w=v_pool_w, pool_scale=v_pool_scale,
               w_out=v_w_out, g_mix_post=v_g_mix_post, g_x_pre=v_g_x_pre, g_mem=v_g_mem, wq_x=v_wq_x,
               wkv_x=v_wkv_x, wo_x=v_wo_x, g_x_post=v_g_x_post, g_ffn_pre=v_g_ffn_pre, w_up=v_w_up,
               w_down=v_w_down, g_ffn_post=v_g_ffn_post)
    S = x.shape[1]
    xs = x.reshape(S, D_MODEL)
    mems = mem.reshape(MEM_LEN, D_MODEL)
    target = loss_target.reshape(S, D_MODEL)

    def small_params(l):
        return dict(
            g_mix_pre=_vec(g_mix_pre[l]), g_mix_post=_vec(g_mix_post[l]), g_x_pre=_vec(g_x_pre[l]),
            g_mem=_vec(g_mem[l]), g_x_post=_vec(g_x_post[l]), g_ffn_pre=_vec(g_ffn_pre[l]),
            g_ffn_post=_vec(g_ffn_post[l]), pool_scale=_vec(pool_scale[l]), pool_w=pool_w[l].astype(BF16),
            b_forget=jnp.pad(_vec(b_forget[l]), ((0, 0), (0, LANES - FOX_HEADS))))

    shard = {n: [w[n][l].astype(BF16) for l in range(DEPTH)] for n in REST}
    shard["w_in"] = [_w_in_travel(w_in[l].astype(BF16)) for l in range(DEPTH)]
    sps = [small_params(l) for l in range(DEPTH)]
    saved, weights = [], []
    h = xs
    (g_in,) = _exchange("gather_w_in", [shard["w_in"][0]])
    hn = _norm_fwd("norm_fwd", xs, sps[0]["g_mix_pre"])
    for l in range(DEPTH):
        travelling = [shard[n][l] for n in REST] + ([shard["w_in"][l + 1]] if l + 1 < DEPTH else [])
        g_next = sps[l + 1]["g_mix_pre"] if l + 1 < DEPTH else None
        h, hn, sv, W, g_in = _layer_fwd(h, hn, mems, sps[l], g_in, travelling, g_next)
        saved.append(sv)
        weights.append(W)
    dh, sq = _loss_fwd_bwd(h, target)
    loss = lax.psum(0.5 * sq[0, 0] / D_MODEL, ("x", "y", "c"))

    parts = [dict() for _ in range(DEPTH)]
    small_grads = [None] * DEPTH
    carried = []
    lower = _norm_bwd("norm_bwd_b", dh, saved[-1]["y"], sps[-1]["g_ffn_post"], None, BF16)
    for l in reversed(range(DEPTH)):
        dy, dg_ffn_post = lower
        below = (saved[l - 1]["y"], sps[l - 1]["g_ffn_post"]) if l > 0 else None
        dh, lower, got, got_carried, dw_in, gs = _layer_bwd(dh, dy, mems, saved[l], sps[l], weights[l], carried, below)
        gs["g_ffn_post"] = dg_ffn_post
        parts[l].update(got)
        if got_carried:
            parts[l + 1]["w_in"] = got_carried[0]
        carried = [dw_in]
        small_grads[l] = gs
    grad_x = dh.reshape(1, S, D_MODEL)

    grads, deltas, new_m, new_v = {}, {}, {}, {}
    rows = dict(w_in=128, w_out=128, wq_x=128, wkv_x=256, wo_x=128, w_up=256, w_down=128)
    sg = {n: jnp.stack([small_grads[l][n].reshape(w[n].shape[1:]) for l in range(DEPTH)]) for n in SMALL}
    riders = dict(wq_x=carried, wo_x=[_pack_small(sg)])
    for n in ["wq_x", "wo_x", "w_down", "w_up", "w_out", "wkv_x", "w_in"]:
        if n == "w_in":
            for l in range(DEPTH):
                parts[l]["w_in"] = jnp.swapaxes(parts[l]["w_in"][:, :W_IN_SHARD, :], 1, 2)
        grads[n], deltas[n], new_m[n], new_v[n], *got = _adamw_big(
            "adamw_" + n, w[n], mom[n], var[n], [parts[l][n] for l in range(DEPTH)], rows[n], riders.get(n, ()))
        if n == "wq_x":
            (parts[0]["w_in"],) = got
        elif n == "wo_x":
            (sg_parts,) = got
    outs = _adamw_small(_pack_small(w), _pack_small(mom), _pack_small(var), sg_parts)
    for d, packed in zip((grads, deltas, new_m, new_v), outs):
        d.update(_unpack_small(packed, w))

    return (loss, grad_x, *[grads[n] for n in W_NAMES], *[deltas[n] for n in W_NAMES],
            *[new_m[n] for n in W_NAMES], *[new_v[n] for n in W_NAMES])
```

```python
import math

import jax
import jax.numpy as jnp
from jax import lax
from jax.experimental import pallas as pl
from jax.experimental.pallas import tpu as pltpu

F32 = jnp.float32
BF16 = jnp.bfloat16

D_MODEL = 1024
DEPTH = 4
FOX_WIDTH = 512
FOX_HEADS = 8
FOX_HEAD_DIM = 64
POOL_WIDTH = 512
POOL_WINDOWS = (2, 4, 8, 16)
POOL_GROUP_DIM = 128
POOL_HALO = 16
MEM_LEN = 256
X_HEADS = 4
X_HEAD_DIM = 256
D_FF = 4096
EPS = 1e-6
IN_COLS = 2056
QKV_COLS = 3 * FOX_WIDTH
UF_COLS = 640
INP_COLS = QKV_COLS + UF_COLS
N_DEV = 8
LANES = 128

ADAM_LR = 0.001
ADAM_B1 = 0.9
ADAM_B2 = 0.999
ADAM_EPS = 1e-08
ADAM_WD = 0.01
ADAM_STEP = 10

VMEM_LIMIT = 56 * 1024 * 1024

W_NAMES = ['g_mix_pre', 'w_in', 'b_forget', 'pool_w', 'pool_scale', 'w_out', 'g_mix_post', 'g_x_pre', 'g_mem',
           'wq_x', 'wkv_x', 'wo_x', 'g_x_post', 'g_ffn_pre', 'w_up', 'w_down', 'g_ffn_post']
BIG = ['w_in', 'w_out', 'wq_x', 'wkv_x', 'wo_x', 'w_up', 'w_down']
SMALL = [n for n in W_NAMES if n not in BIG]

NN = (((1,), (0,)), ((), ()))
NT = (((1,), (1,)), ((), ()))
TN = (((0,), (0,)), ((), ()))


def _params(*sem):
    return pltpu.CompilerParams(dimension_semantics=sem, vmem_limit_bytes=VMEM_LIMIT)


def _row_tile(s):
    return min(s, 512)


def _product(a_ref, w_ref, kind, c0, pw):
    cols = slice(c0, c0 + pw)
    if kind == "nn":
        return lax.dot_general(a_ref[...], w_ref[:, cols], NN, preferred_element_type=F32)
    if kind == "nt":
        return lax.dot_general(a_ref[...], w_ref[cols, :], NT, preferred_element_type=F32)
    n = w_ref.shape[2]
    if kind == "nn3":
        assert pw == n and c0 % n == 0
        return lax.dot_general(a_ref[...], w_ref[c0 // n], NN, preferred_element_type=F32)
    r = None
    for j in range(w_ref.shape[0]):
        part = lax.dot_general(a_ref[:, j * n:(j + 1) * n], w_ref[j, cols, :], NT, preferred_element_type=F32)
        r = part if r is None else r + part
    return r


def _resident(w):
    return pl.BlockSpec(w.shape, lambda i, nd=w.ndim: (0,) * nd)


def _mm_rows(name, terms, outs, extra=None, piece=1024):
    M = terms[0][0].shape[0]
    tm = _row_tile(M)
    nterm = len(terms)
    n_extra = 0 if extra is None else 1
    groups = {}
    for idx, (_, c0, width, fn) in enumerate(outs):
        groups.setdefault((c0, width), []).append((idx, fn))

    def body(*refs):
        a_refs = refs[0:2 * nterm:2]
        w_refs = refs[1:2 * nterm:2]
        extra_refs = refs[2 * nterm:2 * nterm + n_extra]
        out_refs = refs[2 * nterm + n_extra:]
        for (g0, gw), members in groups.items():
            for c0 in range(g0, g0 + gw, piece):
                pw = min(piece, g0 + gw - c0)
                r = None
                for a_ref, w_ref, (_, w, kind) in zip(a_refs, w_refs, terms):
                    part = _product(a_ref, w_ref, kind, c0, pw)
                    r = part if r is None else r + part
                dst = slice(c0 - g0, c0 - g0 + pw)
                for idx, fn in members:
                    if fn == "relu2":
                        rp = jnp.maximum(r, 0.0)
                        val = rp * rp
                    elif fn == "drelu2":
                        val = r * (2.0 * jnp.maximum(extra_refs[0][:, dst].astype(F32), 0.0))
                    else:
                        val = r
                    out_refs[idx][:, dst] = val.astype(out_refs[idx].dtype)

    in_specs, ins = [], []
    for a, w, _ in terms:
        in_specs.append(pl.BlockSpec((tm, a.shape[1]), lambda i: (i, 0)))
        in_specs.append(pl.BlockSpec(w.shape, lambda i, nd=w.ndim: (0,) * nd))
        ins += [a, w]
    if extra is not None:
        in_specs.append(pl.BlockSpec((tm, extra.shape[1]), lambda i: (i, 0)))
        ins.append(extra)
    res = pl.pallas_call(
        body, name=name, grid=(M // tm,), in_specs=in_specs,
        out_specs=[pl.BlockSpec((tm, width), lambda i: (i, 0)) for _, _, width, _ in outs],
        out_shape=[jax.ShapeDtypeStruct((M, width), dt) for dt, _, width, _ in outs],
        compiler_params=_params("parallel"))(*ins)
    return res


def _mm1(name, a, w, kind, n_cols, dtype, piece=1024):
    return _mm_rows(name, [(a, w, kind)], [(dtype, 0, n_cols, "id")], piece=piece)[0]


def _mm_tn(name, a, b, out_dtype, shard_cols=None, piece=512):
    K, M = a.shape
    b_parts = b if isinstance(b, list) else [b]
    nb = len(b_parts)
    N = sum(p.shape[1] for p in b_parts)
    tk = _row_tile(K)
    nk = K // tk
    piece = shard_cols or min(piece, N)

    def body(a_ref, *rest):
        b_refs = rest[:nb]
        o_ref, acc = rest[nb:]
        k = pl.program_id(0)

        @pl.when(k == 0)
        def _():
            acc[...] = jnp.zeros_like(acc)

        a_t = jnp.transpose(a_ref[...])
        if nb == 1:
            for c0 in range(0, N, piece):
                cols = slice(c0, min(c0 + piece, N))
                acc[:, cols] += lax.dot_general(a_t, b_refs[0][:, cols], NN, preferred_element_type=F32)
        else:
            c0 = 0
            for b_ref in b_refs:
                cols = slice(c0, c0 + b_ref.shape[1])
                acc[:, cols] += lax.dot_general(a_t, b_ref[...], NN, preferred_element_type=F32)
                c0 += b_ref.shape[1]

        @pl.when(k == nk - 1)
        def _():
            for c0 in range(0, N, piece):
                cols = slice(c0, min(c0 + piece, N))
                if shard_cols:
                    o_ref[c0 // piece] = acc[:, cols].astype(o_ref.dtype)
                else:
                    o_ref[:, cols] = acc[:, cols].astype(o_ref.dtype)

    out_dims = (N // shard_cols, M, shard_cols) if shard_cols else (M, N)
    return pl.pallas_call(
        body, name=name, grid=(nk,),
        in_specs=[pl.BlockSpec((tk, M), lambda k: (k, 0))]
        + [pl.BlockSpec((tk, p.shape[1]), lambda k: (k, 0)) for p in b_parts],
        out_specs=pl.BlockSpec(out_dims, lambda k, nd=len(out_dims): (0,) * nd),
        out_shape=jax.ShapeDtypeStruct(out_dims, out_dtype),
        scratch_shapes=[pltpu.VMEM((M, N), F32)],
        compiler_params=_params("arbitrary"))(a, *b_parts)


def _mm_tn_rows(name, a_parts, b, out_dtype):
    K = b.shape[0]
    N = b.shape[1]
    na = len(a_parts)
    M = sum(p.shape[1] for p in a_parts)
    tk = _row_tile(K)
    nk = K // tk

    def body(*refs):
        a_refs = refs[:na]
        b_ref, o_ref, acc = refs[na:]
        k = pl.program_id(0)

        @pl.when(k == 0)
        def _():
            acc[...] = jnp.zeros_like(acc)

        bv = b_ref[...]
        r0 = 0
        for a_ref in a_refs:
            rows = slice(r0, r0 + a_ref.shape[1])
            acc[rows, :] += lax.dot_general(jnp.transpose(a_ref[...]), bv, NN, preferred_element_type=F32)
            r0 += a_ref.shape[1]

        @pl.when(k == nk - 1)
        def _():
            o_ref[...] = acc[...].astype(o_ref.dtype)

    return pl.pallas_call(
        body, name=name, grid=(nk,),
        in_specs=[pl.BlockSpec((tk, p.shape[1]), lambda k: (k, 0)) for p in a_parts]
        + [pl.BlockSpec((tk, N), lambda k: (k, 0))],
        out_specs=pl.BlockSpec((M, N), lambda k: (0, 0)), out_shape=jax.ShapeDtypeStruct((M, N), out_dtype),
        scratch_shapes=[pltpu.VMEM((M, N), F32)], compiler_params=_params("arbitrary"))(*a_parts, b)


def _norm_fwd(name, x, g):
    S, Dm = x.shape
    ts = _row_tile(S)

    def body(x_ref, g_ref, h_ref):
        xv = x_ref[...]
        r = lax.rsqrt(jnp.mean(xv * xv, axis=-1, keepdims=True) + EPS)
        h_ref[...] = ((xv * r) * g_ref[...]).astype(BF16)

    return pl.pallas_call(
        body, name=name, grid=(S // ts,),
        in_specs=[pl.BlockSpec((ts, Dm), lambda i: (i, 0)), pl.BlockSpec((1, Dm), lambda i: (0, 0))],
        out_specs=pl.BlockSpec((ts, Dm), lambda i: (i, 0)),
        out_shape=jax.ShapeDtypeStruct((S, Dm), BF16), compiler_params=_params("parallel"))(x, g)


def _mm_resid_norm(name, a, w, x, g, g_next):
    S, Dm = x.shape
    ts = _row_tile(S)
    has_next = g_next is not None

    def body(a_ref, w_ref, x_ref, g_ref, *rest):
        fv = _product(a_ref, w_ref, "nn", 0, Dm)
        r = lax.rsqrt(jnp.mean(fv * fv, axis=-1, keepdims=True) + EPS)
        xn = x_ref[...] + (fv * r) * g_ref[...]
        if has_next:
            gn_ref, f_ref, o_ref, h_ref = rest
            rn = lax.rsqrt(jnp.mean(xn * xn, axis=-1, keepdims=True) + EPS)
            h_ref[...] = ((xn * rn) * gn_ref[...]).astype(BF16)
        else:
            f_ref, o_ref = rest
        f_ref[...] = fv
        o_ref[...] = xn

    row = pl.BlockSpec((ts, Dm), lambda i: (i, 0))
    vec = pl.BlockSpec((1, Dm), lambda i: (0, 0))
    ins = [a, w, x, g] + ([g_next] if has_next else [])
    f32_rows = jax.ShapeDtypeStruct((S, Dm), F32)
    res = pl.pallas_call(
        body, name=name, grid=(S // ts,),
        in_specs=[pl.BlockSpec((ts, a.shape[1]), lambda i: (i, 0)), _resident(w), row, vec] + ([vec] if has_next else []),
        out_specs=[row, row] + ([row] if has_next else []),
        out_shape=[f32_rows, f32_rows] + ([jax.ShapeDtypeStruct((S, Dm), BF16)] if has_next else []),
        compiler_params=_params("parallel"))(*ins)
    return (res[0], res[1], res[2]) if has_next else (res[0], res[1], None)


def _rms_bwd(dov, yv, g):
    r = lax.rsqrt(jnp.mean(yv * yv, axis=-1, keepdims=True) + EPS)
    z = dov * g
    yr = yv * r
    return r * (z - yr * jnp.mean(yr * z, axis=-1, keepdims=True)), jnp.sum(dov * yr, axis=0, keepdims=True)


def _norm_bwd(name, dout, y, g, resid, out_dtype, below=None):
    S, Dm = y.shape
    ts = _row_tile(S)
    has_resid = resid is not None
    chained = below is not None
    produced = isinstance(dout, tuple)
    kind = dout[2] if produced else None
    a_parts = (dout[0] if isinstance(dout[0], list) else [dout[0]]) if produced else []
    n_a = len(a_parts)

    def body(*refs):
        refs = list(refs)
        if produced and n_a == 1:
            dov = _product(refs[0], refs[1], kind, 0, Dm)
            refs = refs[1:]
        elif produced:
            w_ref = refs[n_a]
            dov, k0 = None, 0
            for a_ref in refs[:n_a]:
                k1 = k0 + a_ref.shape[1]
                if kind == "nt":
                    part = lax.dot_general(a_ref[...], w_ref[:, k0:k1], NT, preferred_element_type=F32)
                else:
                    part = lax.dot_general(a_ref[...], w_ref[k0:k1, :], NN, preferred_element_type=F32)
                dov = part if dov is None else dov + part
                k0 = k1
            refs = refs[n_a:]
        else:
            dov = refs[0][...]
        y_ref, g_ref = refs[1:3]
        pos = 3
        r_ref = refs[pos] if has_resid else None
        pos += has_resid
        if chained:
            f_ref, gf_ref = refs[pos:pos + 2]
            pos += 2
        dy_ref, dg_ref = refs[pos:pos + 2]
        i = pl.program_id(0)
        dy, dg = _rms_bwd(dov, y_ref[...], g_ref[...])
        if has_resid:
            dy = dy + r_ref[...]
        dy_ref[...] = dy.astype(out_dtype)

        @pl.when(i == 0)
        def _():
            for ref in refs[pos + 1::2]:
                ref[...] = jnp.zeros_like(ref)

        dg_ref[...] += dg
        if chained:
            df_ref, dgf_ref = refs[pos + 2:pos + 4]
            df, dgf = _rms_bwd(dy, f_ref[...], gf_ref[...])
            df_ref[...] = df.astype(BF16)
            dgf_ref[...] += dgf

    row = pl.BlockSpec((ts, Dm), lambda i: (i, 0))
    vec = pl.BlockSpec((1, Dm), lambda i: (0, 0))
    if produced:
        assert n_a == 1 or kind in ("nt", "nn")
        ins = a_parts + [dout[1]]
        specs = [pl.BlockSpec((ts, a.shape[1]), lambda i: (i, 0)) for a in a_parts] + [_resident(dout[1])]
    else:
        ins = [dout]
        specs = [row]
    ins += [y, g] + ([resid] if has_resid else []) + (list(below) if chained else [])
    specs += [row, vec] + ([row] if has_resid else []) + ([row, vec] if chained else [])
    vec_shape = jax.ShapeDtypeStruct((1, Dm), F32)
    return pl.pallas_call(
        body, name=name, grid=(S // ts,), in_specs=specs, out_specs=[row, vec] + ([row, vec] if chained else []),
        out_shape=[jax.ShapeDtypeStruct((S, Dm), out_dtype), vec_shape]
        + ([jax.ShapeDtypeStruct((S, Dm), BF16), vec_shape] if chained else []),
        compiler_params=_params("arbitrary"))(*ins)


def _loss_fwd_bwd(y, t):
    S, Dm = y.shape
    ts = _row_tile(S)

    def body(y_ref, t_ref, dy_ref, acc_ref):
        i = pl.program_id(0)
        e = y_ref[...] - t_ref[...]
        dy_ref[...] = e * (1.0 / Dm)

        @pl.when(i == 0)
        def _():
            acc_ref[...] = jnp.zeros_like(acc_ref)

        s = jnp.sum(jnp.sum(e * e, axis=1, keepdims=True), axis=0, keepdims=True)
        acc_ref[...] += s

    row = pl.BlockSpec((ts, Dm), lambda i: (i, 0))
    return pl.pallas_call(
        body, name="loss", grid=(S // ts,), in_specs=[row, row],
        out_specs=[row, pl.BlockSpec((8, LANES), lambda i: (0, 0))],
        out_shape=[jax.ShapeDtypeStruct((S, Dm), F32), jax.ShapeDtypeStruct((8, LANES), F32)],
        compiler_params=_params("arbitrary"))(y, t)


def _log_sigmoid(x):
    return jnp.minimum(x, 0.0) - jnp.log(1.0 + jnp.exp(-jnp.abs(x)))


def _gate_fwd(uf, bpad):
    S = uf.shape[0]
    T = _row_tile(S)

    def body(f_ref, b_ref, c_ref, carry):
        i = pl.program_id(0)

        @pl.when(i == 0)
        def _():
            carry[...] = jnp.zeros_like(carry)

        lf = _log_sigmoid(f_ref[...] + b_ref[...])
        r = lax.broadcasted_iota(jnp.int32, (T, T), 0)
        cidx = lax.broadcasted_iota(jnp.int32, (T, T), 1)
        tri = (cidx <= r).astype(F32)
        c = lax.dot_general(tri, lf, NN, precision=lax.Precision.HIGHEST, preferred_element_type=F32)
        c_ref[...] = c + carry[0:1, :]
        carry[...] = carry[...] + jnp.sum(lf, axis=0, keepdims=True)

    return pl.pallas_call(
        body, name="gate_fwd", grid=(S // T,),
        in_specs=[pl.BlockSpec((T, LANES), lambda i: (i, 4)), pl.BlockSpec((1, LANES), lambda i: (0, 0))],
        out_specs=pl.BlockSpec((T, LANES), lambda i: (i, 0)),
        out_shape=jax.ShapeDtypeStruct((S, LANES), F32),
        scratch_shapes=[pltpu.VMEM((8, LANES), F32)], compiler_params=_params("arbitrary"))(uf, bpad)


def _gate_bwd(dc, uf, bpad):
    S = uf.shape[0]
    T = _row_tile(S)
    nb = S // T

    def body(dc_ref, f_ref, b_ref, df_ref, db_ref, carry):
        i = pl.program_id(0)

        @pl.when(i == 0)
        def _():
            carry[...] = jnp.zeros_like(carry)
            db_ref[...] = jnp.zeros_like(db_ref)

        dcv = dc_ref[...]
        r = lax.broadcasted_iota(jnp.int32, (T, T), 0)
        cidx = lax.broadcasted_iota(jnp.int32, (T, T), 1)
        tri = (cidx >= r).astype(F32)
        dlf = lax.dot_general(tri, dcv, NN, precision=lax.Precision.HIGHEST, preferred_element_type=F32)
        dlf = dlf + carry[0:1, :]
        carry[...] = carry[...] + jnp.sum(dcv, axis=0, keepdims=True)
        fg = f_ref[...] + b_ref[...]
        dfg = dlf / (1.0 + jnp.exp(fg))
        df_ref[...] = dfg.astype(BF16)
        db_ref[...] += jnp.sum(dfg, axis=0, keepdims=True)

    return pl.pallas_call(
        body, name="gate_bwd", grid=(nb,),
        in_specs=[pl.BlockSpec((T, LANES), lambda i: (nb - 1 - i, 0)),
                  pl.BlockSpec((T, LANES), lambda i: (nb - 1 - i, 4)),
                  pl.BlockSpec((1, LANES), lambda i: (0, 0))],
        out_specs=[pl.BlockSpec((T, LANES), lambda i: (nb - 1 - i, 0)), pl.BlockSpec((1, LANES), lambda i: (0, 0))],
        out_shape=[jax.ShapeDtypeStruct((S, LANES), BF16), jax.ShapeDtypeStruct((1, LANES), F32)],
        scratch_shapes=[pltpu.VMEM((8, LANES), F32)], compiler_params=_params("arbitrary"))(dc, uf, bpad)


FOX_CHUNK = 32
FOX_CHUNK_BWD = 64
HEAD_PAIRS = FOX_HEADS // 2
PAIR = 2


def _masked(s, row0, col0, diagonal):
    if diagonal:
        row = row0 + lax.broadcasted_iota(jnp.int32, s.shape, 0)
        col = col0 + lax.broadcasted_iota(jnp.int32, s.shape, 1)
        s = jnp.where(col <= row, s, -jnp.inf)
    return s


def _causal_pairs(n, query_major):
    if query_major:
        pairs = [(q, k) for q in range(n) for k in range(q + 1)]
    else:
        pairs = [(q, k) for k in range(n) for q in range(k, n)]
    return (jnp.asarray([p[0] for p in pairs], jnp.int32), jnp.asarray([p[1] for p in pairs], jnp.int32))


def _lane_block(b):
    return slice(b * LANES, (b + 1) * LANES)


def _fold(op, xs):
    acc = xs[0]
    for x in xs[1:]:
        acc = op(acc, x)
    return acc


def _head_lanes(hh):
    lane = lax.broadcasted_iota(jnp.int32, (1, LANES), 1)
    return (lane < FOX_HEAD_DIM) if hh == 0 else (lane >= FOX_HEAD_DIM)


def _pick(first_head, a, b):
    return jnp.where(first_head, a, b)


def _fox_fwd(qkv, cT, comm):
    S = qkv.shape[0]
    t = _row_tile(S)
    n = S // t
    nc = len(comm)
    scale = 1.0 / math.sqrt(FOX_HEAD_DIM)
    chunk = min(FOX_CHUNK, t)
    per_head = 4
    q_tab, k_tab = _causal_pairs(n, True)
    steps = q_tab.shape[0]

    def body(qt_ref, kt_ref, q_ref, k_ref, v_ref, c_ref, *rest):
        comm_in = rest[:nc]
        o_ref, ob_ref, lse_ref = rest[nc:nc + 3]
        comm_out = rest[nc + 3:2 * nc + 3]
        scr = rest[2 * nc + 3:2 * nc + 3 + PAIR * per_head]
        sems = rest[2 * nc + 3 + PAIR * per_head:]
        hp = pl.program_id(0)
        step_id = pl.program_id(1)
        qi = qt_ref[step_id]
        ki = kt_ref[step_id]

        if nc:
            @pl.when((hp == 0) & (step_id == 0))
            def _():
                _Gather(comm_in, comm_out, *sems).start()

            @pl.when((hp == HEAD_PAIRS - 1) & (step_id == 0))
            def _():
                _Gather(comm_in, comm_out, *sems).pass_on()

        @pl.when(ki == 0)
        def _():
            for hh in range(PAIR):
                m_s, l_s, a_s, acc_s = scr[hh * per_head:hh * per_head + 4]
                m_s[...] = jnp.full_like(m_s, -jnp.inf)
                l_s[...] = jnp.zeros_like(l_s)
                acc_s[...] = jnp.zeros_like(acc_s)

        def step(diagonal):
            q2 = q_ref[...] * scale
            k2 = k_ref[...]
            v2 = v_ref[...]
            scores = []
            for hh in range(PAIR):
                qm = jnp.where(_head_lanes(hh), q2, jnp.zeros_like(q2))
                scores.append(lax.dot_general(qm, k2, NT, preferred_element_type=F32))
            for hh in range(PAIR):
                m_s, l_s, a_s, acc_s = scr[hh * per_head:(hh + 1) * per_head]
                s_s = scores[hh]
                hi_rows, lo_rows = [], []
                for r in range(t // chunk):
                    rows = slice(r * chunk, (r + 1) * chunk)
                    blocks = [_masked(s_s[rows, _lane_block(b)] - c_ref[hh, :, _lane_block(b)], r * chunk,
                                      b * LANES, diagonal) for b in range(t // LANES)]
                    m_prev = m_s[rows, :]
                    m_new = jnp.maximum(m_prev, jnp.max(_fold(jnp.maximum, blocks), axis=1, keepdims=True))
                    alpha = jnp.exp(m_prev - m_new)
                    ps = [jnp.exp(blk - m_new) for blk in blocks]
                    l_s[rows, :] = alpha * l_s[rows, :] + jnp.sum(_fold(jnp.add, ps), axis=1, keepdims=True)
                    m_s[rows, :] = m_new
                    a_s[rows, :] = alpha
                    his = [p.astype(BF16) for p in ps]
                    hi_rows.append(jnp.concatenate(his, axis=1))
                    lo_rows.append(jnp.concatenate([(p - h.astype(F32)).astype(BF16) for p, h in zip(ps, his)],
                                                   axis=1))
                pv = (lax.dot_general(jnp.concatenate(hi_rows, axis=0), v2, NN, preferred_element_type=F32)
                      + lax.dot_general(jnp.concatenate(lo_rows, axis=0), v2, NN, preferred_element_type=F32))
                acc_s[...] = a_s[...] * acc_s[...] + pv

        @pl.when(ki < qi)
        def _():
            step(False)

        @pl.when(ki == qi)
        def _():
            step(True)
            heads = []
            for hh in range(PAIR):
                m_s, l_s, a_s, acc_s = scr[hh * per_head:hh * per_head + 4]
                heads.append(acc_s[...] / l_s[...])
                lse_ref[hh] = m_s[...] + jnp.log(l_s[...])
            o2 = _pick(_head_lanes(0), heads[0], heads[1])
            o_ref[...] = o2
            ob_ref[...] = o2.astype(BF16)

        if nc:
            @pl.when((hp == HEAD_PAIRS - 1) & (step_id == steps - 1))
            def _():
                _Gather(comm_in, comm_out, *sems).finish()

    def q_cols(first_block):
        return pl.BlockSpec((t, LANES), lambda h, s, qt, kt: (qt[s], first_block + h))

    def k_cols(first_block):
        return pl.BlockSpec((t, LANES), lambda h, s, qt, kt: (kt[s], first_block + h))

    any_spec = pl.BlockSpec(memory_space=pl.ANY)
    head_scratch = [pltpu.VMEM((t, LANES), F32)] * per_head
    grid_spec = pltpu.PrefetchScalarGridSpec(
        num_scalar_prefetch=2, grid=(HEAD_PAIRS, steps),
        in_specs=[q_cols(0), k_cols(HEAD_PAIRS), k_cols(2 * HEAD_PAIRS),
                  pl.BlockSpec((PAIR, 1, t), lambda h, s, qt, kt: (h, 0, kt[s]))] + [any_spec] * nc,
        out_specs=[q_cols(0), q_cols(0),
                   pl.BlockSpec((PAIR, t, LANES), lambda h, s, qt, kt: (h, qt[s], 0))] + [any_spec] * nc,
        scratch_shapes=head_scratch * PAIR + _comm_scratch(nc))
    return pl.pallas_call(
        body, name="fox_fwd", grid_spec=grid_spec,
        out_shape=[jax.ShapeDtypeStruct((S, FOX_WIDTH), F32), jax.ShapeDtypeStruct((S, FOX_WIDTH), BF16),
                   jax.ShapeDtypeStruct((FOX_HEADS, S, LANES), F32)] + _comm_shapes(comm),
        compiler_params=_params("arbitrary", "arbitrary"))(q_tab, k_tab, qkv, qkv, qkv, cT, *comm)


def _fox_bwd(qkv, cT, o, lse, do, comm):
    S = qkv.shape[0]
    t = _row_tile(S)
    n = S // t
    nc = len(comm)
    scale = 1.0 / math.sqrt(FOX_HEAD_DIM)
    chunk = min(FOX_CHUNK_BWD, t)
    per_head = 2
    q_tab, k_tab = _causal_pairs(n, False)
    steps = q_tab.shape[0]

    def body(qt_ref, kt_ref, q_ref, k_ref, v_ref, c_ref, o_ref, do_ref, lse_ref, *rest):
        comm_in = rest[:nc]
        dq_ref, dk_ref, dv_ref, dc_ref = rest[nc:nc + 4]
        comm_out = rest[nc + 4:2 * nc + 4]
        dq_s, dk_s, dv_s = rest[2 * nc + 4:2 * nc + 7]
        scr = rest[2 * nc + 7:2 * nc + 7 + PAIR * per_head]
        sems = rest[2 * nc + 7 + PAIR * per_head:]
        hp = pl.program_id(0)
        step_id = pl.program_id(1)
        qi = qt_ref[step_id]
        ki = kt_ref[step_id]

        if nc:
            @pl.when((hp == 0) & (step_id == 0))
            def _():
                for cp in _comm_copies(comm_in, comm_out, *sems):
                    cp.start()

        @pl.when(step_id == 0)
        def _():
            dq_s[...] = jnp.zeros_like(dq_s)

        @pl.when(qi == ki)
        def _():
            dk_s[...] = jnp.zeros_like(dk_s)
            dv_s[...] = jnp.zeros_like(dv_s)
            for hh in range(PAIR):
                dc_s = scr[hh * per_head]
                dc_s[...] = jnp.zeros_like(dc_s)

        def step(diagonal):
            q2 = q_ref[...]
            k2 = k_ref[...]
            v2 = v_ref[...]
            do2 = do_ref[...]
            prod = do2.astype(F32) * o_ref[...]
            grads = []
            for hh in range(PAIR):
                dc_s, delta_s = scr[hh * per_head:(hh + 1) * per_head]
                mine = _head_lanes(hh)
                s_s = lax.dot_general(jnp.where(mine, q2 * scale, jnp.zeros_like(q2)), k2, NT,
                                      preferred_element_type=F32)
                dp_s = lax.dot_general(jnp.where(mine, do2, jnp.zeros_like(do2)), v2, NT, preferred_element_type=F32)
                delta_s[...] = jnp.broadcast_to(jnp.sum(jnp.where(mine, prod, 0.0), axis=1, keepdims=True),
                                                (t, LANES))
                dc8 = [jnp.zeros((8, LANES), F32) for _ in range(t // LANES)]
                p_rows, ds_rows = [], []
                for r in range(t // chunk):
                    rows = slice(r * chunk, (r + 1) * chunk)
                    lse = lse_ref[hh, rows, :]
                    delta = delta_s[rows, :]
                    p_blocks, ds_blocks = [], []
                    for b in range(t // LANES):
                        s = _masked(s_s[rows, _lane_block(b)] - c_ref[hh, :, _lane_block(b)], r * chunk, b * LANES,
                                    diagonal)
                        p = jnp.exp(s - lse)
                        ds = p * (dp_s[rows, _lane_block(b)] - delta)
                        p_blocks.append(p.astype(BF16))
                        ds_blocks.append(ds.astype(BF16))
                        dc8[b] = dc8[b] + jnp.sum(ds.reshape(chunk // 8, 8, LANES), axis=0)
                    p_rows.append(jnp.concatenate(p_blocks, axis=1))
                    ds_rows.append(jnp.concatenate(ds_blocks, axis=1))
                for b in range(t // LANES):
                    dc_s[:, _lane_block(b)] += jnp.sum(dc8[b], axis=0, keepdims=True)
                dsb = jnp.concatenate(ds_rows, axis=0)
                grads.append((lax.dot_general(jnp.concatenate(p_rows, axis=0), do2, TN, preferred_element_type=F32),
                              lax.dot_general(dsb, k2, NN, preferred_element_type=F32),
                              lax.dot_general(dsb, q2, TN, preferred_element_type=F32)))
            first = _head_lanes(0)
            dv_s[...] += _pick(first, grads[0][0], grads[1][0])
            q_rows = pl.ds(pl.multiple_of(qi * t, t), t)
            dq_s[q_rows, :] += _pick(first, grads[0][1], grads[1][1]) * scale
            dk_s[...] += _pick(first, grads[0][2], grads[1][2]) * scale

        @pl.when(qi > ki)
        def _():
            step(False)

        @pl.when(qi == ki)
        def _():
            step(True)

        @pl.when(qi == n - 1)
        def _():
            dk_ref[...] = dk_s[...].astype(BF16)
            dv_ref[...] = dv_s[...].astype(BF16)
            for hh in range(PAIR):
                dc_ref[hh] = -scr[hh * per_head][...]

        @pl.when(step_id == steps - 1)
        def _():
            dq_ref[...] = dq_s[...].astype(BF16)

        if nc:
            @pl.when((hp == HEAD_PAIRS - 1) & (step_id == steps - 1))
            def _():
                for cp in _comm_copies(comm_in, comm_out, *sems):
                    cp.wait()

    def q_side(first_block):
        return pl.BlockSpec((t, LANES), lambda h, s, qt, kt: (qt[s], first_block + h))

    def k_side(first_block):
        return pl.BlockSpec((t, LANES), lambda h, s, qt, kt: (kt[s], first_block + h))

    any_spec = pl.BlockSpec(memory_space=pl.ANY)
    head_scratch = [pltpu.VMEM((1, t), F32), pltpu.VMEM((t, LANES), F32)]
    grad_shape = jax.ShapeDtypeStruct((S, FOX_WIDTH), BF16)
    grid_spec = pltpu.PrefetchScalarGridSpec(
        num_scalar_prefetch=2, grid=(HEAD_PAIRS, steps),
        in_specs=[q_side(0), k_side(HEAD_PAIRS), k_side(2 * HEAD_PAIRS),
                  pl.BlockSpec((PAIR, 1, t), lambda h, s, qt, kt: (h, 0, kt[s])), q_side(0), q_side(0),
                  pl.BlockSpec((PAIR, t, LANES), lambda h, s, qt, kt: (h, qt[s], 0))] + [any_spec] * nc,
        out_specs=[pl.BlockSpec((S, LANES), lambda h, s, qt, kt: (0, h)), k_side(0), k_side(0),
                   pl.BlockSpec((PAIR, 1, t), lambda h, s, qt, kt: (h, 0, kt[s]))] + [any_spec] * nc,
        scratch_shapes=[pltpu.VMEM((S, LANES), F32), pltpu.VMEM((t, LANES), F32), pltpu.VMEM((t, LANES), F32)]
        + head_scratch * PAIR + _comm_scratch(nc))
    return pl.pallas_call(
        body, name="fox_bwd", grid_spec=grid_spec,
        out_shape=[grad_shape, grad_shape, grad_shape, jax.ShapeDtypeStruct((FOX_HEADS, 1, S), F32)]
        + _comm_shapes(comm),
        compiler_params=_params("arbitrary", "arbitrary"))(q_tab, k_tab, qkv, qkv, qkv, cT, o, do, lse, *comm)


def _lanes(g):
    return slice(g * POOL_GROUP_DIM, (g + 1) * POOL_GROUP_DIM)


def _window_sum(e, win, back):
    rows = e.shape[0]
    s = e
    sh = 1
    while sh < win:
        s = s + pltpu.roll(s, sh if back else rows - sh, 0)
        sh *= 2
    return s


def _pooled(u_ref, up_ref, i, g, win, T):
    cur = u_ref[:, _lanes(g)]
    tail = jnp.where(i > 0, up_ref[T - POOL_HALO:T, _lanes(g)], 0.0)
    e = jnp.concatenate([tail, cur], axis=0)
    s = _window_sum(e, win, True)
    t_idx = i * T - POOL_HALO + lax.broadcasted_iota(jnp.int32, (T + POOL_HALO, POOL_GROUP_DIM), 0)
    cnt = jnp.clip(t_idx + 1, 1, win).astype(F32)
    return (s / cnt - e)[POOL_HALO:, :]


def _pool_fwd(uf, pw, ps):
    S = uf.shape[0]
    T = _row_tile(S)

    def body(u_ref, up_ref, w_ref, sc_ref, o_ref):
        i = pl.program_id(0)
        for g, win in enumerate(POOL_WINDOWS):
            pb = _pooled(u_ref, up_ref, i, g, win, T).astype(BF16)
            yv = lax.dot_general(pb, w_ref[g], NN, preferred_element_type=F32)
            o_ref[:, _lanes(g)] = (yv * sc_ref[:, _lanes(g)]).astype(BF16)

    return pl.pallas_call(
        body, name="pool_fwd", grid=(S // T,),
        in_specs=[pl.BlockSpec((T, POOL_WIDTH), lambda i: (i, 0)),
                  pl.BlockSpec((T, POOL_WIDTH), lambda i: (jnp.maximum(i - 1, 0), 0)),
                  pl.BlockSpec((4, POOL_GROUP_DIM, POOL_GROUP_DIM), lambda i: (0, 0, 0)),
                  pl.BlockSpec((1, POOL_WIDTH), lambda i: (0, 0))],
        out_specs=pl.BlockSpec((T, POOL_WIDTH), lambda i: (i, 0)),
        out_shape=jax.ShapeDtypeStruct((S, POOL_WIDTH), BF16), compiler_params=_params("parallel"))(uf, uf, pw, ps)


def _pool_bwd(uf, dpool, pw, ps):
    S = uf.shape[0]
    T = _row_tile(S)
    nb = S // T

    def body(u_ref, up_ref, d_ref, dn_ref, w_ref, sc_ref, du_ref, dw_ref, dsc_ref):
        i = pl.program_id(0)

        @pl.when(i == 0)
        def _():
            dw_ref[...] = jnp.zeros_like(dw_ref)
            dsc_ref[...] = jnp.zeros_like(dsc_ref)

        t_idx = i * T + lax.broadcasted_iota(jnp.int32, (T + POOL_HALO, POOL_GROUP_DIM), 0)
        for g, win in enumerate(POOL_WINDOWS):
            pb = _pooled(u_ref, up_ref, i, g, win, T).astype(BF16)
            w = w_ref[g]
            sc = sc_ref[:, _lanes(g)]
            yv = lax.dot_general(pb, w, NN, preferred_element_type=F32)
            dov = d_ref[:, _lanes(g)]
            dsc_ref[:, _lanes(g)] += jnp.sum(dov * yv, axis=0, keepdims=True)
            head = jnp.where(i < nb - 1, dn_ref[0:POOL_HALO, _lanes(g)], 0.0)
            dyb = (jnp.concatenate([dov, head], axis=0) * sc).astype(BF16)
            dw_ref[g] += lax.dot_general(pb, dyb[:T], TN, preferred_element_type=F32)
            dpl = lax.dot_general(dyb, w, NT, preferred_element_type=F32)
            cnt = jnp.minimum(t_idx + 1, win).astype(F32)
            a = _window_sum(dpl / cnt, win, False)
            du_ref[:, _lanes(g)] = (a - dpl)[:T].astype(BF16)

    return pl.pallas_call(
        body, name="pool_bwd", grid=(nb,),
        in_specs=[pl.BlockSpec((T, POOL_WIDTH), lambda i: (i, 0)),
                  pl.BlockSpec((T, POOL_WIDTH), lambda i: (jnp.maximum(i - 1, 0), 0)),
                  pl.BlockSpec((T, POOL_WIDTH), lambda i: (i, 0)),
                  pl.BlockSpec((T, POOL_WIDTH), lambda i: (jnp.minimum(i + 1, nb - 1), 0)),
                  pl.BlockSpec((4, POOL_GROUP_DIM, POOL_GROUP_DIM), lambda i: (0, 0, 0)),
                  pl.BlockSpec((1, POOL_WIDTH), lambda i: (0, 0))],
        out_specs=[pl.BlockSpec((T, POOL_WIDTH), lambda i: (i, 0)),
                   pl.BlockSpec((4, POOL_GROUP_DIM, POOL_GROUP_DIM), lambda i: (0, 0, 0)),
                   pl.BlockSpec((1, POOL_WIDTH), lambda i: (0, 0))],
        out_shape=[jax.ShapeDtypeStruct((S, POOL_WIDTH), BF16),
                   jax.ShapeDtypeStruct((4, POOL_GROUP_DIM, POOL_GROUP_DIM), F32),
                   jax.ShapeDtypeStruct((1, POOL_WIDTH), F32)],
        compiler_params=_params("arbitrary"))(uf, uf, dpool, dpool, pw, ps)


def _xhead(h):
    return slice(h * X_HEAD_DIM, (h + 1) * X_HEAD_DIM)


def _xvhead(h):
    return slice(D_MODEL + h * X_HEAD_DIM, D_MODEL + (h + 1) * X_HEAD_DIM)


X_CHUNK = 32


def _x_probs(s_ref, rows):
    blocks = [s_ref[rows, _lane_block(b)] * (1.0 / math.sqrt(X_HEAD_DIM)) for b in range(MEM_LEN // LANES)]
    m = jnp.max(_fold(jnp.maximum, blocks), axis=1, keepdims=True)
    es = [jnp.exp(blk - m) for blk in blocks]
    den = jnp.sum(_fold(jnp.add, es), axis=1, keepdims=True)
    return [e / den for e in es]


def _xattn_fwd(q, kv):
    S = q.shape[0]
    t = _row_tile(S)
    chunk = min(X_CHUNK, t)

    def body(q_ref, kv_ref, o_ref):
        for h in range(X_HEADS):
            s = lax.dot_general(q_ref[:, _xhead(h)], kv_ref[:, _xhead(h)], NT, preferred_element_type=F32)
            p_rows = []
            for r in range(t // chunk):
                rows = slice(r * chunk, (r + 1) * chunk)
                p_rows.append(jnp.concatenate([p.astype(BF16) for p in _x_probs(s, rows)], axis=1))
            o_ref[:, _xhead(h)] = lax.dot_general(jnp.concatenate(p_rows, axis=0), kv_ref[:, _xvhead(h)], NN,
                                                  preferred_element_type=F32).astype(BF16)

    return pl.pallas_call(
        body, name="xattn_fwd", grid=(S // t,),
        in_specs=[pl.BlockSpec((t, D_MODEL), lambda i: (i, 0)), pl.BlockSpec((MEM_LEN, 2 * D_MODEL), lambda i: (0, 0))],
        out_specs=pl.BlockSpec((t, D_MODEL), lambda i: (i, 0)),
        out_shape=jax.ShapeDtypeStruct((S, D_MODEL), BF16), compiler_params=_params("parallel"))(q, kv)


def _xattn_bwd(q, kv, do):
    S = q.shape[0]
    t = _row_tile(S)
    nb = S // t
    scale = 1.0 / math.sqrt(X_HEAD_DIM)
    chunk = min(X_CHUNK, t)

    def body(q_ref, kv_ref, do_ref, dq_ref, dkv_ref, acc):
        i = pl.program_id(0)

        @pl.when(i == 0)
        def _():
            acc[...] = jnp.zeros_like(acc)

        for h in range(X_HEADS):
            qh = q_ref[:, _xhead(h)]
            kh = kv_ref[:, _xhead(h)]
            doh = do_ref[:, _xhead(h)]
            s_s = lax.dot_general(qh, kh, NT, preferred_element_type=F32)
            dp_s = lax.dot_general(doh, kv_ref[:, _xvhead(h)], NT, preferred_element_type=F32)
            p_rows, ds_rows = [], []
            for r in range(t // chunk):
                rows = slice(r * chunk, (r + 1) * chunk)
                ps = _x_probs(s_s, rows)
                dps = [dp_s[rows, _lane_block(b)] for b in range(len(ps))]
                inner = jnp.sum(_fold(jnp.add, [dp * p for dp, p in zip(dps, ps)]), axis=1, keepdims=True)
                p_rows.append(jnp.concatenate([p.astype(BF16) for p in ps], axis=1))
                ds_rows.append(jnp.concatenate([(p * (dp - inner)).astype(BF16) for dp, p in zip(dps, ps)], axis=1))
            dsb = jnp.concatenate(ds_rows, axis=0)
            acc[:, _xvhead(h)] += lax.dot_general(jnp.concatenate(p_rows, axis=0), doh, TN,
                                                  preferred_element_type=F32)
            dq_ref[:, _xhead(h)] = (lax.dot_general(dsb, kh, NN, preferred_element_type=F32) * scale).astype(BF16)
            acc[:, _xhead(h)] += lax.dot_general(dsb, qh, TN, preferred_element_type=F32) * scale

        @pl.when(i == nb - 1)
        def _():
            dkv_ref[...] = acc[...].astype(BF16)

    row = pl.BlockSpec((t, D_MODEL), lambda i: (i, 0))
    full = pl.BlockSpec((MEM_LEN, 2 * D_MODEL), lambda i: (0, 0))
    return pl.pallas_call(
        body, name="xattn_bwd", grid=(nb,), in_specs=[row, full, row], out_specs=[row, full],
        out_shape=[jax.ShapeDtypeStruct((S, D_MODEL), BF16), jax.ShapeDtypeStruct((MEM_LEN, 2 * D_MODEL), BF16)],
        scratch_shapes=[pltpu.VMEM((MEM_LEN, 2 * D_MODEL), F32)],
        compiler_params=_params("arbitrary"))(q, kv, do)


def _comm_shapes(arrs):
    return [jax.ShapeDtypeStruct((N_DEV,) + tuple(a.shape[-2:]), a.dtype) for a in arrs]


def _comm_scratch(n):
    if n == 0:
        return []
    return [pltpu.SemaphoreType.DMA((n, N_DEV - 1)), pltpu.SemaphoreType.DMA((n, N_DEV - 1)),
            pltpu.SemaphoreType.DMA((n,))]


def _comm_copies(ins, outs, send_sems, recv_sems, local_sems):
    x, y, c = lax.axis_index("x"), lax.axis_index("y"), lax.axis_index("c")
    me = 4 * x + 2 * y + c
    copies = []
    for w in range(len(ins)):
        src = ins[w] if len(ins[w].shape) == 2 else ins[w].at[me]
        copies.append(pltpu.make_async_copy(src, outs[w].at[me], local_sems.at[w]))
    for k in range(1, N_DEV):
        px = 1 - x if k & 4 else x
        py = 1 - y if k & 2 else y
        pc = 1 - c if k & 1 else c
        peer = 4 * px + 2 * py + pc
        for w in range(len(ins)):
            src = ins[w] if len(ins[w].shape) == 2 else ins[w].at[peer]
            copies.append(pltpu.make_async_remote_copy(
                src_ref=src, dst_ref=outs[w].at[me], send_sem=send_sems.at[w, k - 1],
                recv_sem=recv_sems.at[w, k - 1], device_id=(px, py, pc), device_id_type=pl.DeviceIdType.MESH))
    return copies


class _Gather:
    def __init__(self, ins, outs, send_sems, recv_sems, local_sems):
        x, y, c = lax.axis_index("x"), lax.axis_index("y"), lax.axis_index("c")
        me = 4 * x + 2 * y + c
        sibling = (x, y, 1 - c)
        self.local, self.mine, self.passed = [], [], []
        for w in range(len(ins)):
            def remote(idx, src, slot, dev, w=w):
                return pltpu.make_async_remote_copy(
                    src_ref=src, dst_ref=outs[w].at[slot], send_sem=send_sems.at[w, idx],
                    recv_sem=recv_sems.at[w, idx], device_id=dev, device_id_type=pl.DeviceIdType.MESH)

            self.local.append(pltpu.make_async_copy(ins[w], outs[w].at[me], local_sems.at[w]))
            mine, passed = [remote(0, ins[w], me, sibling)], []
            for j, (fx, fy) in enumerate(((0, 1), (1, 0), (1, 1))):
                px = 1 - x if fx else x
                py = 1 - y if fy else y
                slot = 4 * px + 2 * py + c
                mine.append(remote(1 + j, ins[w], me, (px, py, c)))
                passed.append(remote(4 + j, outs[w].at[slot], slot, sibling))
            self.mine.append(mine)
            self.passed.append(passed)

    def start(self):
        for cp in self.local:
            cp.start()
        for mine in self.mine:
            for cp in mine:
                cp.start()

    def pass_on(self):
        for mine, passed in zip(self.mine, self.passed):
            for j, cp in enumerate(passed):
                mine[1 + j].wait_recv()
                cp.start()

    def finish(self):
        for mine, passed in zip(self.mine, self.passed):
            mine[0].wait_recv()
            for cp in passed:
                cp.wait_recv()
            for cp in mine + passed:
                cp.wait_send()
        for cp in self.local:
            cp.wait()


def _exchange(name, arrs):
    n = len(arrs)
    gather = all(a.ndim == 2 for a in arrs)

    def body(*refs):
        if gather:
            g = _Gather(refs[:n], refs[n:2 * n], *refs[2 * n:])
            g.start()
            g.pass_on()
            g.finish()
            return
        copies = _comm_copies(refs[:n], refs[n:2 * n], *refs[2 * n:])
        for cp in copies:
            cp.start()
        for cp in copies:
            cp.wait()

    any_spec = pl.BlockSpec(memory_space=pl.ANY)
    return pl.pallas_call(
        body, name=name, in_specs=[any_spec] * n, out_specs=[any_spec] * n, out_shape=_comm_shapes(arrs),
        scratch_shapes=_comm_scratch(n))(*arrs)


def _adamw_math(w, g, m, v):
    m = ADAM_B1 * m + (1.0 - ADAM_B1) * g
    v = ADAM_B2 * v + (1.0 - ADAM_B2) * (g * g)
    m_hat = m / (1.0 - ADAM_B1 ** ADAM_STEP)
    v_hat = v / (1.0 - ADAM_B2 ** ADAM_STEP)
    delta = -ADAM_LR * (m_hat / (jnp.sqrt(v_hat) + ADAM_EPS) + ADAM_WD * w)
    return delta, m, v


def _sum_parts(p_ref):
    g = p_ref[0].astype(F32)
    for s in range(1, N_DEV):
        g = g + p_ref[s].astype(F32)
    return g


def _adamw_big(name, w, m, v, parts, tr, comm=()):
    L, R, C = w.shape
    nc = len(comm)
    gathered = [i for i, a in enumerate(comm) if a.ndim == 2]
    direct = [i for i, a in enumerate(comm) if a.ndim != 2]
    nr = R // tr

    def exchange(comm_in, comm_out, sems, begin):
        if gathered:
            g = _Gather([comm_in[i] for i in gathered], [comm_out[i] for i in gathered], *sems[:3])
            if begin:
                g.start()
            else:
                g.pass_on()
                g.finish()
        if direct:
            for cp in _comm_copies([comm_in[i] for i in direct], [comm_out[i] for i in direct], *sems[-3:]):
                cp.start() if begin else cp.wait()

    def body(w_ref, m_ref, v_ref, *rest):
        p_refs = rest[:L]
        comm_in = rest[L:L + nc]
        g_ref, d_ref, nm_ref, nv_ref = rest[L + nc:L + nc + 4]
        comm_out = rest[L + nc + 4:L + 2 * nc + 4]
        sems = rest[L + 2 * nc + 4:]
        layer = pl.program_id(0)
        if nc:
            @pl.when((layer == 0) & (pl.program_id(1) == 0))
            def _():
                exchange(comm_in, comm_out, sems, True)

        for j in range(L):
            @pl.when(layer == j)
            def _(j=j):
                g = _sum_parts(p_refs[j])
                delta, nm, nv = _adamw_math(w_ref[...], g, m_ref[...], v_ref[...])
                g_ref[...] = g
                d_ref[...] = delta
                nm_ref[...] = nm
                nv_ref[...] = nv

        if nc:
            @pl.when((layer == L - 1) & (pl.program_id(1) == nr - 1))
            def _():
                exchange(comm_in, comm_out, sems, False)

    blk = pl.BlockSpec((None, tr, C), lambda l, i: (l, i, 0))

    def part_spec(j):
        return pl.BlockSpec((N_DEV, tr, C), lambda l, i: (0, jnp.where(l == j, i, 0), 0))

    shp = jax.ShapeDtypeStruct((L, R, C), F32)
    any_spec = pl.BlockSpec(memory_space=pl.ANY)
    return pl.pallas_call(
        body, name=name, grid=(L, nr),
        in_specs=[blk, blk, blk] + [part_spec(j) for j in range(L)] + [any_spec] * nc,
        out_specs=[blk] * 4 + [any_spec] * nc, out_shape=[shp] * 4 + _comm_shapes(comm),
        scratch_shapes=_comm_scratch(len(gathered)) + _comm_scratch(len(direct)),
        compiler_params=_params("arbitrary", "arbitrary"))(w, m, v, *parts, *comm)


def _adamw_small(w, m, v, parts):
    R, C = w.shape

    def body(w_ref, m_ref, v_ref, p_ref, g_ref, d_ref, nm_ref, nv_ref):
        g = _sum_parts(p_ref)
        delta, nm, nv = _adamw_math(w_ref[...], g, m_ref[...], v_ref[...])
        g_ref[...] = g
        d_ref[...] = delta
        nm_ref[...] = nm
        nv_ref[...] = nv

    shp = jax.ShapeDtypeStruct((R, C), F32)
    return pl.pallas_call(body, name="adamw_small", out_shape=[shp] * 4,
                          compiler_params=pltpu.CompilerParams(vmem_limit_bytes=VMEM_LIMIT))(w, m, v, parts)


def _vec(a):
    return a.reshape(1, -1)


W_IN_SHARD = IN_COLS // N_DEV
W_IN_ROWS = 272


def _w_in_travel(a):
    pad = [(0, 0)] * (a.ndim - 2) + [(0, W_IN_ROWS - W_IN_SHARD), (0, 0)]
    return jnp.pad(jnp.swapaxes(a, -1, -2), pad)


def _unpack_w_in(g):
    nat = g[:, :W_IN_SHARD, :].reshape(IN_COLS, D_MODEL)
    f = jnp.pad(nat[QKV_COLS:QKV_COLS + FOX_HEADS], ((0, UF_COLS - POOL_WIDTH - FOX_HEADS), (0, 0)))
    return jnp.concatenate([nat[:QKV_COLS], nat[QKV_COLS + FOX_HEADS:], f], axis=0)


def _pack_dw_in(dwp_t):
    nat = jnp.concatenate([dwp_t[:QKV_COLS], dwp_t[QKV_COLS + POOL_WIDTH:QKV_COLS + POOL_WIDTH + FOX_HEADS],
                           dwp_t[QKV_COLS:QKV_COLS + POOL_WIDTH]], axis=0)
    return jnp.pad(nat.reshape(N_DEV, W_IN_SHARD, D_MODEL), ((0, 0), (0, W_IN_ROWS - W_IN_SHARD), (0, 0)))


REST = ['w_out', 'wq_x', 'wkv_x', 'wo_x', 'w_up', 'w_down']


def _layer_fwd(x0, h1, mem, sp, g_in, shards, g_next):
    S = x0.shape[0]
    sv = {"x0": x0}
    w_inp = _unpack_w_in(g_in)
    qkv, uf = _mm_rows("mm_in", [(h1, w_inp, "nt")],
                       [(BF16, 0, QKV_COLS, "id"), (F32, QKV_COLS, UF_COLS, "id")], piece=UF_COLS)
    c = _gate_fwd(uf, sp["b_forget"])
    cT = jnp.transpose(c[:, :FOX_HEADS]).reshape(FOX_HEADS, 1, S)
    o, ob, lse, *got = _fox_fwd(qkv, cT, shards)
    g_out, g_q, g_kv, g_o, g_up, g_down = got[:6]
    W = dict(inp=w_inp, out=g_out.reshape(D_MODEL, D_MODEL), q=g_q.reshape(D_MODEL, D_MODEL), kv=g_kv,
             o=g_o.reshape(D_MODEL, D_MODEL), up=g_up, down=g_down.reshape(D_FF, D_MODEL))
    pool = _pool_fwd(uf, sp["pool_w"], sp["pool_scale"])
    cat = jnp.concatenate([ob, pool], axis=1)
    mix, x1, h2 = _mm_resid_norm("mm_sq_norm", cat, W["out"], x0, sp["g_mix_post"], sp["g_x_pre"])
    mn = _norm_fwd("norm_mem", mem, sp["g_mem"])
    q2 = _mm1("mm_q", h2, W["q"], "nn", D_MODEL, BF16)
    kv = _mm1("mm_kv", mn, W["kv"], "nn3", 2 * D_MODEL, BF16, piece=2 * D_MODEL // N_DEV)
    o2 = _xattn_fwd(q2, kv)
    xo, x2, h3 = _mm_resid_norm("mm_sq_norm", o2, W["o"], x1, sp["g_x_post"], sp["g_ffn_pre"])
    up, act = _mm_rows("mm_up", [(h3, W["up"], "nn3")], [(BF16, 0, D_FF, "id"), (BF16, 0, D_FF, "relu2")],
                       piece=D_FF // N_DEV)
    y, x3, h_next = _mm_resid_norm("mm_down_norm" if g_next is not None else "mm_down_norm_last", act, W["down"], x2,
                                   sp["g_ffn_post"], g_next)
    sv.update(h1=h1, uf=uf, cT=cT, qkv=qkv, o=o, lse=lse, cat=cat, mix=mix, x1=x1, h2=h2, mn=mn, q2=q2, kv=kv,
              o2=o2, xo=xo, x2=x2, h3=h3, up=up, act=act, y=y)
    return x3, h_next, sv, W, (got[6] if len(got) > 6 else None)


def _layer_bwd(dx3, dy, mem, sv, sp, W, carried, below):
    S = dx3.shape[0]
    gs = {}
    gb = {}
    (dup,) = _mm_rows("mm_dup", [(dy, W["down"], "nt")], [(BF16, 0, D_FF, "drelu2")], extra=sv["up"])
    gb["w_down"] = _mm_tn("mm_dw_down", sv["act"], dy, BF16).reshape(N_DEV, D_FF // N_DEV, D_MODEL)
    gb["w_up"] = _mm_tn("mm_dw_up", sv["h3"], dup, BF16, shard_cols=D_FF // N_DEV)
    dx2, gs["g_ffn_pre"], dxo, gs["g_x_post"] = _norm_bwd(
        "mm_dh3_norm_bwd", (dup, W["up"], "nt3"), sv["x2"], sp["g_ffn_pre"], dx3, F32,
        below=(sv["xo"], sp["g_x_post"]))
    do2 = _mm1("mm_sq_t", dxo, W["o"], "nt", D_MODEL, BF16)
    gb["wo_x"] = _mm_tn("mm_dw_sq", sv["o2"], dxo, BF16).reshape(N_DEV, D_MODEL // N_DEV, D_MODEL)
    dq2, dkvb = _xattn_bwd(sv["q2"], sv["kv"], do2)
    gb["wq_x"] = _mm_tn("mm_dw_sq", sv["h2"], dq2, BF16).reshape(N_DEV, D_MODEL // N_DEV, D_MODEL)
    gb["wkv_x"] = _mm_tn("mm_dw_kv", sv["mn"], dkvb, BF16, shard_cols=2 * D_MODEL // N_DEV)
    dmn = _mm1("mm_dmn", dkvb, W["kv"], "nt3", D_MODEL, F32)
    _, gs["g_mem"] = _norm_bwd("norm_bwd_mem", dmn, mem, sp["g_mem"], None, BF16)
    dx1, gs["g_x_pre"], dmix, gs["g_mix_post"] = _norm_bwd(
        "mm_dh2_norm_bwd", (dq2, W["q"], "nt"), sv["x1"], sp["g_x_pre"], dx2, F32,
        below=(sv["mix"], sp["g_mix_post"]))
    doh, dpool = _mm_rows("mm_dcat", [(dmix, W["out"], "nt")],
                          [(BF16, 0, FOX_WIDTH, "id"), (F32, FOX_WIDTH, POOL_WIDTH, "id")])
    gb["w_out"] = _mm_tn("mm_dw_sq", sv["cat"], dmix, BF16).reshape(N_DEV, D_MODEL // N_DEV, D_MODEL)
    du, gs["pool_w"], gs["pool_scale"] = _pool_bwd(sv["uf"], dpool, sp["pool_w"], sp["pool_scale"])
    dq, dk, dv, dcT, *got = _fox_bwd(sv["qkv"], sv["cT"], sv["o"], sv["lse"], doh, [gb[n] for n in REST] + carried)
    dc = jnp.pad(jnp.transpose(dcT.reshape(FOX_HEADS, S)), ((0, 0), (0, LANES - FOX_HEADS)))
    dfg, db = _gate_bwd(dc, sv["uf"], sp["b_forget"])
    gs["b_forget"] = db[:, :FOX_HEADS]
    dproj = [dq, dk, dv, du, dfg]
    dwp = _mm_tn_rows("mm_dw_in", dproj, sv["h1"], BF16)
    dh1 = (dproj, W["inp"], "nn")
    if below is None:
        dx0, gs["g_mix_pre"] = _norm_bwd("mm_dh1_norm_bwd_first", dh1, sv["x0"], sp["g_mix_pre"], dx1, F32)
        lower = None
    else:
        dx0, gs["g_mix_pre"], *lower = _norm_bwd("mm_dh1_norm_bwd", dh1, sv["x0"], sp["g_mix_pre"], dx1, F32,
                                                 below=below)
    return dx0, lower, dict(zip(REST, got[:6])), got[6:], _pack_dw_in(dwp), gs


def _small_rows(shape):
    return -(-math.prod(shape) // (8 * LANES)) * 8


def _pack_small(d):
    blocks = []
    for n in SMALL:
        rows = _small_rows(d[n].shape)
        if d[n].shape[-1] == LANES:
            blocks.append(d[n].reshape(rows, LANES))
        else:
            flat = d[n].reshape(-1)
            blocks.append(jnp.pad(flat, (0, rows * LANES - flat.shape[0])).reshape(rows, LANES))
    return jnp.concatenate(blocks, axis=0)


def _unpack_small(packed, like):
    out = {}
    row = 0
    for n in SMALL:
        shape = like[n].shape
        rows = _small_rows(shape)
        block = packed[row:row + rows]
        out[n] = block.reshape(shape) if shape[-1] == LANES else block.reshape(-1)[:math.prod(shape)].reshape(shape)
        row += rows
    return out


def kernel(x, mem, g_mix_pre, w_in, b_forget, pool_w, pool_scale, w_out, g_mix_post, g_x_pre, g_mem, wq_x, wkv_x, wo_x, g_x_post, g_ffn_pre, w_up, w_down, g_ffn_post, loss_target, m_g_mix_pre, m_w_in, m_b_forget, m_pool_w, m_pool_scale, m_w_out, m_g_mix_post, m_g_x_pre, m_g_mem, m_wq_x, m_wkv_x, m_wo_x, m_g_x_post, m_g_ffn_pre, m_w_up, m_w_down, m_g_ffn_post, v_g_mix_pre, v_w_in, v_b_forget, v_pool_w, v_pool_scale, v_w_out, v_g_mix_post, v_g_x_pre, v_g_mem, v_wq_x, v_wkv_x, v_wo_x, v_g_x_post, v_g_ffn_pre, v_w_up, v_w_down, v_g_ffn_post):
    w = dict(g_mix_pre=g_mix_pre, w_in=w_in, b_forget=b_forget, pool_w=pool_w, pool_scale=pool_scale, w_out=w_out,
             g_mix_post=g_mix_post, g_x_pre=g_x_pre, g_mem=g_mem, wq_x=wq_x, wkv_x=wkv_x, wo_x=wo_x,
             g_x_post=g_x_post, g_ffn_pre=g_ffn_pre, w_up=w_up, w_down=w_down, g_ffn_post=g_ffn_post)
    mom = dict(g_mix_pre=m_g_mix_pre, w_in=m_w_in, b_forget=m_b_forget, pool_w=m_pool_w, pool_scale=m_pool_scale,
               w_out=m_w_out, g_mix_post=m_g_mix_post, g_x_pre=m_g_x_pre, g_mem=m_g_mem, wq_x=m_wq_x,
               wkv_x=m_wkv_x, wo_x=m_wo_x, g_x_post=m_g_x_post, g_ffn_pre=m_g_ffn_pre, w_up=m_w_up,
               w_down=m_w_down, g_ffn_post=m_g_ffn_post)
    var = dict(g_mix_pre=v_g_mix_pre, w_in=v_w_in, b_forget=v_b_forget, pool_w=v_pool_w, pool_scale=v_pool_scale,
               w_out=v_w_out, g_mix_post=v_g_mix_post, g_x_pre=v_g_x_pre, g_mem=v_g_mem, wq_x=v_wq_x,
               wkv_x=v_wkv_x, wo_x=v_wo_x, g_x_post=v_g_x_post, g_ffn_pre=v_g_ffn_pre, w_up=v_w_up,
               w_down=v_w_down, g_ffn_post=v_g_ffn_post)
    S = x.shape[1]
    xs = x.reshape(S, D_MODEL)
    mems = mem.reshape(MEM_LEN, D_MODEL)
    target = loss_target.reshape(S, D_MODEL)

    def small_params(l):
        return dict(
            g_mix_pre=_vec(g_mix_pre[l]), g_mix_post=_vec(g_mix_post[l]), g_x_pre=_vec(g_x_pre[l]),
            g_mem=_vec(g_mem[l]), g_x_post=_vec(g_x_post[l]), g_ffn_pre=_vec(g_ffn_pre[l]),
            g_ffn_post=_vec(g_ffn_post[l]), pool_scale=_vec(pool_scale[l]), pool_w=pool_w[l].astype(BF16),
            b_forget=jnp.pad(_vec(b_forget[l]), ((0, 0), (0, LANES - FOX_HEADS))))

    shard = {n: [w[n][l].astype(BF16) for l in range(DEPTH)] for n in REST}
    shard["w_in"] = [_w_in_travel(w_in[l].astype(BF16)) for l in range(DEPTH)]
    sps = [small_params(l) for l in range(DEPTH)]
    saved, weights = [], []
    h = xs
    (g_in,) = _exchange("gather_w_in", [shard["w_in"][0]])
    hn = _norm_fwd("norm_fwd", xs, sps[0]["g_mix_pre"])
    for l in range(DEPTH):
        travelling = [shard[n][l] for n in REST] + ([shard["w_in"][l + 1]] if l + 1 < DEPTH else [])
        g_next = sps[l + 1]["g_mix_pre"] if l + 1 < DEPTH else None
        h, hn, sv, W, g_in = _layer_fwd(h, hn, mems, sps[l], g_in, travelling, g_next)
        saved.append(sv)
        weights.append(W)
    dh, sq = _loss_fwd_bwd(h, target)
    loss = lax.psum(0.5 * sq[0, 0] / D_MODEL, ("x", "y", "c"))

    parts = [dict() for _ in range(DEPTH)]
    small_grads = [None] * DEPTH
    carried = []
    lower = _norm_bwd("norm_bwd_b", dh, saved[-1]["y"], sps[-1]["g_ffn_post"], None, BF16)
    for l in reversed(range(DEPTH)):
        dy, dg_ffn_post = lower
        below = (saved[l - 1]["y"], sps[l - 1]["g_ffn_post"]) if l > 0 else None
        dh, lower, got, got_carried, dw_in, gs = _layer_bwd(dh, dy, mems, saved[l], sps[l], weights[l], carried, below)
        gs["g_ffn_post"] = dg_ffn_post
        parts[l].update(got)
        if got_carried:
            parts[l + 1]["w_in"] = got_carried[0]
        carried = [dw_in]
        small_grads[l] = gs
    grad_x = dh.reshape(1, S, D_MODEL)

    grads, deltas, new_m, new_v = {}, {}, {}, {}
    rows = dict(w_in=128, w_out=128, wq_x=128, wkv_x=256, wo_x=128, w_up=256, w_down=128)
    sg = {n: jnp.stack([small_grads[l][n].reshape(w[n].shape[1:]) for l in range(DEPTH)]) for n in SMALL}
    riders = dict(w_down=carried + [_pack_small(sg)])
    for n in ["w_down", "w_up", "w_out", "wq_x", "wkv_x", "wo_x", "w_in"]:
        if n == "w_in":
            for l in range(DEPTH):
                parts[l]["w_in"] = jnp.swapaxes(parts[l]["w_in"][:, :W_IN_SHARD, :], 1, 2)
        grads[n], deltas[n], new_m[n], new_v[n], *got = _adamw_big(
            "adamw_" + n, w[n], mom[n], var[n], [parts[l][n] for l in range(DEPTH)], rows[n], riders.get(n, ()))
        if n == "w_down":
            parts[0]["w_in"], sg_parts = got
    outs = _adamw_small(_pack_small(w), _pack_small(mom), _pack_small(var), sg_parts)
    for d, packed in zip((grads, deltas, new_m, new_v), outs):
        d.update(_unpack_small(packed, w))

    return (loss, grad_x, *[grads[n] for n in W_NAMES], *[deltas[n] for n in W_NAMES],
            *[new_m[n] for n in W_NAMES], *[new_v[n] for n in W_NAMES])
```

```python
import math

import jax
import jax.numpy as jnp
from jax import lax
from jax.experimental import pallas as pl
from jax.experimental.pallas import tpu as pltpu

F32 = jnp.float32
BF16 = jnp.bfloat16

D_MODEL = 1024
DEPTH = 4
FOX_WIDTH = 512
FOX_HEADS = 8
FOX_HEAD_DIM = 64
POOL_WIDTH = 512
POOL_WINDOWS = (2, 4, 8, 16)
POOL_GROUP_DIM = 128
POOL_HALO = 16
MEM_LEN = 256
X_HEADS = 4
X_HEAD_DIM = 256
D_FF = 4096
EPS = 1e-6
IN_COLS = 2056
QKV_COLS = 3 * FOX_WIDTH
UF_COLS = 640
INP_COLS = QKV_COLS + UF_COLS
N_DEV = 8
LANES = 128

ADAM_LR = 0.001
ADAM_B1 = 0.9
ADAM_B2 = 0.999
ADAM_EPS = 1e-08
ADAM_WD = 0.01
ADAM_STEP = 10

VMEM_LIMIT = 56 * 1024 * 1024

W_NAMES = ['g_mix_pre', 'w_in', 'b_forget', 'pool_w', 'pool_scale', 'w_out', 'g_mix_post', 'g_x_pre', 'g_mem',
           'wq_x', 'wkv_x', 'wo_x', 'g_x_post', 'g_ffn_pre', 'w_up', 'w_down', 'g_ffn_post']
BIG = ['w_in', 'w_out', 'wq_x', 'wkv_x', 'wo_x', 'w_up', 'w_down']
SMALL = [n for n in W_NAMES if n not in BIG]

NN = (((1,), (0,)), ((), ()))
NT = (((1,), (1,)), ((), ()))
TN = (((0,), (0,)), ((), ()))


def _params(*sem):
    return pltpu.CompilerParams(dimension_semantics=sem, vmem_limit_bytes=VMEM_LIMIT)


def _row_tile(s):
    return min(s, 512)


def _product(a_ref, w_ref, kind, c0, pw):
    cols = slice(c0, c0 + pw)
    if kind == "nn":
        return lax.dot_general(a_ref[...], w_ref[:, cols], NN, preferred_element_type=F32)
    if kind == "nt":
        return lax.dot_general(a_ref[...], w_ref[cols, :], NT, preferred_element_type=F32)
    n = w_ref.shape[2]
    if kind == "nn3":
        assert pw == n and c0 % n == 0
        return lax.dot_general(a_ref[...], w_ref[c0 // n], NN, preferred_element_type=F32)
    r = None
    for j in range(w_ref.shape[0]):
        part = lax.dot_general(a_ref[:, j * n:(j + 1) * n], w_ref[j, cols, :], NT, preferred_element_type=F32)
        r = part if r is None else r + part
    return r


def _resident(w):
    return pl.BlockSpec(w.shape, lambda i, nd=w.ndim: (0,) * nd, pipeline_mode=pl.Buffered(1))


def _mm_rows(name, terms, outs, extra=None, piece=1024):
    M = terms[0][0].shape[0]
    tm = min(M, 2 * _row_tile(M))
    nterm = len(terms)
    n_extra = 0 if extra is None else 1
    groups = {}
    for idx, (_, c0, width, fn) in enumerate(outs):
        groups.setdefault((c0, width), []).append((idx, fn))

    def body(*refs):
        a_refs = refs[0:2 * nterm:2]
        w_refs = refs[1:2 * nterm:2]
        extra_refs = refs[2 * nterm:2 * nterm + n_extra]
        out_refs = refs[2 * nterm + n_extra:]
        for (g0, gw), members in groups.items():
            for c0 in range(g0, g0 + gw, piece):
                pw = min(piece, g0 + gw - c0)
                r = None
                for a_ref, w_ref, (_, w, kind) in zip(a_refs, w_refs, terms):
                    part = _product(a_ref, w_ref, kind, c0, pw)
                    r = part if r is None else r + part
                dst = slice(c0 - g0, c0 - g0 + pw)
                for idx, fn in members:
                    if fn == "relu2":
                        rp = jnp.maximum(r, 0.0)
                        val = rp * rp
                    elif fn == "drelu2":
                        val = r * (2.0 * jnp.maximum(extra_refs[0][:, dst].astype(F32), 0.0))
                    else:
                        val = r
                    out_refs[idx][:, dst] = val.astype(out_refs[idx].dtype)

    in_specs, ins = [], []
    for a, w, _ in terms:
        in_specs.append(pl.BlockSpec((tm, a.shape[1]), lambda i: (i, 0)))
        in_specs.append(_resident(w))
        ins += [a, w]
    if extra is not None:
        in_specs.append(pl.BlockSpec((tm, extra.shape[1]), lambda i: (i, 0)))
        ins.append(extra)
    res = pl.pallas_call(
        body, name=name, grid=(M // tm,), in_specs=in_specs,
        out_specs=[pl.BlockSpec((tm, width), lambda i: (i, 0)) for _, _, width, _ in outs],
        out_shape=[jax.ShapeDtypeStruct((M, width), dt) for dt, _, width, _ in outs],
        compiler_params=_params("parallel"))(*ins)
    return res


def _mm1(name, a, w, kind, n_cols, dtype, piece=1024):
    return _mm_rows(name, [(a, w, kind)], [(dtype, 0, n_cols, "id")], piece=piece)[0]


def _mm_tn(name, a, b, out_dtype, shard_cols=None, piece=512):
    K, M = a.shape
    b_parts = b if isinstance(b, list) else [b]
    nb = len(b_parts)
    N = sum(p.shape[1] for p in b_parts)
    tk = _row_tile(K)
    nk = K // tk
    piece = shard_cols or min(piece, N)

    def body(a_ref, *rest):
        b_refs = rest[:nb]
        o_ref, acc = rest[nb:]
        k = pl.program_id(0)

        @pl.when(k == 0)
        def _():
            acc[...] = jnp.zeros_like(acc)

        a_t = jnp.transpose(a_ref[...])
        if nb == 1:
            for c0 in range(0, N, piece):
                cols = slice(c0, min(c0 + piece, N))
                acc[:, cols] += lax.dot_general(a_t, b_refs[0][:, cols], NN, preferred_element_type=F32)
        else:
            c0 = 0
            for b_ref in b_refs:
                cols = slice(c0, c0 + b_ref.shape[1])
                acc[:, cols] += lax.dot_general(a_t, b_ref[...], NN, preferred_element_type=F32)
                c0 += b_ref.shape[1]

        @pl.when(k == nk - 1)
        def _():
            for c0 in range(0, N, piece):
                cols = slice(c0, min(c0 + piece, N))
                if shard_cols:
                    o_ref[c0 // piece] = acc[:, cols].astype(o_ref.dtype)
                else:
                    o_ref[:, cols] = acc[:, cols].astype(o_ref.dtype)

    out_dims = (N // shard_cols, M, shard_cols) if shard_cols else (M, N)
    return pl.pallas_call(
        body, name=name, grid=(nk,),
        in_specs=[pl.BlockSpec((tk, M), lambda k: (k, 0))]
        + [pl.BlockSpec((tk, p.shape[1]), lambda k: (k, 0)) for p in b_parts],
        out_specs=pl.BlockSpec(out_dims, lambda k, nd=len(out_dims): (0,) * nd),
        out_shape=jax.ShapeDtypeStruct(out_dims, out_dtype),
        scratch_shapes=[pltpu.VMEM((M, N), F32)],
        compiler_params=_params("arbitrary"))(a, *b_parts)


def _mm_tn_rows(name, a_parts, b, out_dtype):
    K = b.shape[0]
    N = b.shape[1]
    na = len(a_parts)
    M = sum(p.shape[1] for p in a_parts)
    tk = _row_tile(K)
    nk = K // tk

    def body(*refs):
        a_refs = refs[:na]
        b_ref, o_ref, acc = refs[na:]
        k = pl.program_id(0)

        @pl.when(k == 0)
        def _():
            acc[...] = jnp.zeros_like(acc)

        bv = b_ref[...]
        r0 = 0
        for a_ref in a_refs:
            rows = slice(r0, r0 + a_ref.shape[1])
            acc[rows, :] += lax.dot_general(jnp.transpose(a_ref[...]), bv, NN, preferred_element_type=F32)
            r0 += a_ref.shape[1]

        @pl.when(k == nk - 1)
        def _():
            o_ref[...] = acc[...].astype(o_ref.dtype)

    return pl.pallas_call(
        body, name=name, grid=(nk,),
        in_specs=[pl.BlockSpec((tk, p.shape[1]), lambda k: (k, 0)) for p in a_parts]
        + [pl.BlockSpec((tk, N), lambda k: (k, 0))],
        out_specs=pl.BlockSpec((M, N), lambda k: (0, 0)), out_shape=jax.ShapeDtypeStruct((M, N), out_dtype),
        scratch_shapes=[pltpu.VMEM((M, N), F32)], compiler_params=_params("arbitrary"))(*a_parts, b)


def _norm_fwd(name, x, g):
    S, Dm = x.shape
    ts = _row_tile(S)

    def body(x_ref, g_ref, h_ref):
        xv = x_ref[...]
        r = lax.rsqrt(jnp.mean(xv * xv, axis=-1, keepdims=True) + EPS)
        h_ref[...] = ((xv * r) * g_ref[...]).astype(BF16)

    return pl.pallas_call(
        body, name=name, grid=(S // ts,),
        in_specs=[pl.BlockSpec((ts, Dm), lambda i: (i, 0)), pl.BlockSpec((1, Dm), lambda i: (0, 0))],
        out_specs=pl.BlockSpec((ts, Dm), lambda i: (i, 0)),
        out_shape=jax.ShapeDtypeStruct((S, Dm), BF16), compiler_params=_params("parallel"))(x, g)


def _mm_resid_norm(name, a, w, x, g, g_next):
    S, Dm = x.shape
    ts = _row_tile(S)
    has_next = g_next is not None

    def body(a_ref, w_ref, x_ref, g_ref, *rest):
        fv = _product(a_ref, w_ref, "nn", 0, Dm)
        r = lax.rsqrt(jnp.mean(fv * fv, axis=-1, keepdims=True) + EPS)
        xn = x_ref[...] + (fv * r) * g_ref[...]
        if has_next:
            gn_ref, f_ref, o_ref, h_ref = rest
            rn = lax.rsqrt(jnp.mean(xn * xn, axis=-1, keepdims=True) + EPS)
            h_ref[...] = ((xn * rn) * gn_ref[...]).astype(BF16)
        else:
            f_ref, o_ref = rest
        f_ref[...] = fv
        o_ref[...] = xn

    row = pl.BlockSpec((ts, Dm), lambda i: (i, 0))
    vec = pl.BlockSpec((1, Dm), lambda i: (0, 0))
    ins = [a, w, x, g] + ([g_next] if has_next else [])
    f32_rows = jax.ShapeDtypeStruct((S, Dm), F32)
    res = pl.pallas_call(
        body, name=name, grid=(S // ts,),
        in_specs=[pl.BlockSpec((ts, a.shape[1]), lambda i: (i, 0)), _resident(w), row, vec] + ([vec] if has_next else []),
        out_specs=[row, row] + ([row] if has_next else []),
        out_shape=[f32_rows, f32_rows] + ([jax.ShapeDtypeStruct((S, Dm), BF16)] if has_next else []),
        compiler_params=_params("parallel"))(*ins)
    return (res[0], res[1], res[2]) if has_next else (res[0], res[1], None)


def _rms_bwd(dov, yv, g):
    r = lax.rsqrt(jnp.mean(yv * yv, axis=-1, keepdims=True) + EPS)
    z = dov * g
    yr = yv * r
    return r * (z - yr * jnp.mean(yr * z, axis=-1, keepdims=True)), jnp.sum(dov * yr, axis=0, keepdims=True)


def _norm_bwd(name, dout, y, g, resid, out_dtype, below=None):
    S, Dm = y.shape
    ts = _row_tile(S)
    has_resid = resid is not None
    chained = below is not None
    produced = isinstance(dout, tuple)
    kind = dout[2] if produced else None
    a_parts = (dout[0] if isinstance(dout[0], list) else [dout[0]]) if produced else []
    n_a = len(a_parts)

    def body(*refs):
        refs = list(refs)
        if produced and n_a == 1:
            dov = _product(refs[0], refs[1], kind, 0, Dm)
            refs = refs[1:]
        elif produced:
            w_ref = refs[n_a]
            dov, k0 = None, 0
            for a_ref in refs[:n_a]:
                k1 = k0 + a_ref.shape[1]
                if kind == "nt":
                    part = lax.dot_general(a_ref[...], w_ref[:, k0:k1], NT, preferred_element_type=F32)
                else:
                    part = lax.dot_general(a_ref[...], w_ref[k0:k1, :], NN, preferred_element_type=F32)
                dov = part if dov is None else dov + part
                k0 = k1
            refs = refs[n_a:]
        else:
            dov = refs[0][...]
        y_ref, g_ref = refs[1:3]
        pos = 3
        r_ref = refs[pos] if has_resid else None
        pos += has_resid
        if chained:
            f_ref, gf_ref = refs[pos:pos + 2]
            pos += 2
        dy_ref, dg_ref = refs[pos:pos + 2]
        i = pl.program_id(0)
        dy, dg = _rms_bwd(dov, y_ref[...], g_ref[...])
        if has_resid:
            dy = dy + r_ref[...]
        dy_ref[...] = dy.astype(out_dtype)

        @pl.when(i == 0)
        def _():
            for ref in refs[pos + 1::2]:
                ref[...] = jnp.zeros_like(ref)

        dg_ref[...] += dg
        if chained:
            df_ref, dgf_ref = refs[pos + 2:pos + 4]
            df, dgf = _rms_bwd(dy, f_ref[...], gf_ref[...])
            df_ref[...] = df.astype(BF16)
            dgf_ref[...] += dgf

    row = pl.BlockSpec((ts, Dm), lambda i: (i, 0))
    vec = pl.BlockSpec((1, Dm), lambda i: (0, 0))
    if produced:
        assert n_a == 1 or kind in ("nt", "nn")
        ins = a_parts + [dout[1]]
        specs = [pl.BlockSpec((ts, a.shape[1]), lambda i: (i, 0)) for a in a_parts] + [_resident(dout[1])]
    else:
        ins = [dout]
        specs = [row]
    ins += [y, g] + ([resid] if has_resid else []) + (list(below) if chained else [])
    specs += [row, vec] + ([row] if has_resid else []) + ([row, vec] if chained else [])
    vec_shape = jax.ShapeDtypeStruct((1, Dm), F32)
    return pl.pallas_call(
        body, name=name, grid=(S // ts,), in_specs=specs, out_specs=[row, vec] + ([row, vec] if chained else []),
        out_shape=[jax.ShapeDtypeStruct((S, Dm), out_dtype), vec_shape]
        + ([jax.ShapeDtypeStruct((S, Dm), BF16), vec_shape] if chained else []),
        compiler_params=_params("arbitrary"))(*ins)


def _loss_fwd_bwd(y, t):
    S, Dm = y.shape
    ts = _row_tile(S)

    def body(y_ref, t_ref, dy_ref, acc_ref):
        i = pl.program_id(0)
        e = y_ref[...] - t_ref[...]
        dy_ref[...] = e * (1.0 / Dm)

        @pl.when(i == 0)
        def _():
            acc_ref[...] = jnp.zeros_like(acc_ref)

        s = jnp.sum(jnp.sum(e * e, axis=1, keepdims=True), axis=0, keepdims=True)
        acc_ref[...] += s

    row = pl.BlockSpec((ts, Dm), lambda i: (i, 0))
    return pl.pallas_call(
        body, name="loss", grid=(S // ts,), in_specs=[row, row],
        out_specs=[row, pl.BlockSpec((8, LANES), lambda i: (0, 0))],
        out_shape=[jax.ShapeDtypeStruct((S, Dm), F32), jax.ShapeDtypeStruct((8, LANES), F32)],
        compiler_params=_params("arbitrary"))(y, t)


def _log_sigmoid(x):
    return jnp.minimum(x, 0.0) - jnp.log(1.0 + jnp.exp(-jnp.abs(x)))


def _gate_fwd(uf, bpad):
    S = uf.shape[0]
    T = _row_tile(S)

    def body(f_ref, b_ref, c_ref, carry):
        i = pl.program_id(0)

        @pl.when(i == 0)
        def _():
            carry[...] = jnp.zeros_like(carry)

        lf = _log_sigmoid(f_ref[...] + b_ref[...])
        r = lax.broadcasted_iota(jnp.int32, (T, T), 0)
        cidx = lax.broadcasted_iota(jnp.int32, (T, T), 1)
        tri = (cidx <= r).astype(F32)
        c = lax.dot_general(tri, lf, NN, precision=lax.Precision.HIGHEST, preferred_element_type=F32)
        c_ref[...] = c + carry[0:1, :]
        carry[...] = carry[...] + jnp.sum(lf, axis=0, keepdims=True)

    return pl.pallas_call(
        body, name="gate_fwd", grid=(S // T,),
        in_specs=[pl.BlockSpec((T, LANES), lambda i: (i, 4)), pl.BlockSpec((1, LANES), lambda i: (0, 0))],
        out_specs=pl.BlockSpec((T, LANES), lambda i: (i, 0)),
        out_shape=jax.ShapeDtypeStruct((S, LANES), F32),
        scratch_shapes=[pltpu.VMEM((8, LANES), F32)], compiler_params=_params("arbitrary"))(uf, bpad)


def _gate_bwd(dc, uf, bpad):
    S = uf.shape[0]
    T = _row_tile(S)
    nb = S // T

    def body(dc_ref, f_ref, b_ref, df_ref, db_ref, carry):
        i = pl.program_id(0)

        @pl.when(i == 0)
        def _():
            carry[...] = jnp.zeros_like(carry)
            db_ref[...] = jnp.zeros_like(db_ref)

        dcv = dc_ref[...]
        r = lax.broadcasted_iota(jnp.int32, (T, T), 0)
        cidx = lax.broadcasted_iota(jnp.int32, (T, T), 1)
        tri = (cidx >= r).astype(F32)
        dlf = lax.dot_general(tri, dcv, NN, precision=lax.Precision.HIGHEST, preferred_element_type=F32)
        dlf = dlf + carry[0:1, :]
        carry[...] = carry[...] + jnp.sum(dcv, axis=0, keepdims=True)
        fg = f_ref[...] + b_ref[...]
        dfg = dlf / (1.0 + jnp.exp(fg))
        df_ref[...] = dfg.astype(BF16)
        db_ref[...] += jnp.sum(dfg, axis=0, keepdims=True)

    return pl.pallas_call(
        body, name="gate_bwd", grid=(nb,),
        in_specs=[pl.BlockSpec((T, LANES), lambda i: (nb - 1 - i, 0)),
                  pl.BlockSpec((T, LANES), lambda i: (nb - 1 - i, 4)),
                  pl.BlockSpec((1, LANES), lambda i: (0, 0))],
        out_specs=[pl.BlockSpec((T, LANES), lambda i: (nb - 1 - i, 0)), pl.BlockSpec((1, LANES), lambda i: (0, 0))],
        out_shape=[jax.ShapeDtypeStruct((S, LANES), BF16), jax.ShapeDtypeStruct((1, LANES), F32)],
        scratch_shapes=[pltpu.VMEM((8, LANES), F32)], compiler_params=_params("arbitrary"))(dc, uf, bpad)


FOX_CHUNK = 32
FOX_CHUNK_BWD = 64
HEAD_PAIRS = FOX_HEADS // 2
PAIR = 2


def _masked(s, row0, col0, diagonal):
    if diagonal:
        row = row0 + lax.broadcasted_iota(jnp.int32, s.shape, 0)
        col = col0 + lax.broadcasted_iota(jnp.int32, s.shape, 1)
        s = jnp.where(col <= row, s, -jnp.inf)
    return s


def _causal_pairs(n, query_major):
    if query_major:
        pairs = [(q, k) for q in range(n) for k in range(q + 1)]
    else:
        pairs = [(q, k) for k in range(n) for q in range(k, n)]
    return (jnp.asarray([p[0] for p in pairs], jnp.int32), jnp.asarray([p[1] for p in pairs], jnp.int32))


def _lane_block(b):
    return slice(b * LANES, (b + 1) * LANES)


def _fold(op, xs):
    acc = xs[0]
    for x in xs[1:]:
        acc = op(acc, x)
    return acc


def _head_lanes(hh):
    lane = lax.broadcasted_iota(jnp.int32, (1, LANES), 1)
    return (lane < FOX_HEAD_DIM) if hh == 0 else (lane >= FOX_HEAD_DIM)


def _pick(first_head, a, b):
    return jnp.where(first_head, a, b)


def _fox_fwd(qkv, cT, comm):
    S = qkv.shape[0]
    t = _row_tile(S)
    n = S // t
    nc = len(comm)
    scale = 1.0 / math.sqrt(FOX_HEAD_DIM)
    chunk = min(FOX_CHUNK, t)
    per_head = 4
    q_tab, k_tab = _causal_pairs(n, True)
    steps = q_tab.shape[0]

    def body(qt_ref, kt_ref, q_ref, k_ref, v_ref, c_ref, *rest):
        comm_in = rest[:nc]
        o_ref, ob_ref, lse_ref = rest[nc:nc + 3]
        comm_out = rest[nc + 3:2 * nc + 3]
        scr = rest[2 * nc + 3:2 * nc + 3 + PAIR * per_head]
        sems = rest[2 * nc + 3 + PAIR * per_head:]
        hp = pl.program_id(0)
        step_id = pl.program_id(1)
        qi = qt_ref[step_id]
        ki = kt_ref[step_id]

        if nc:
            @pl.when((hp == 0) & (step_id == 0))
            def _():
                _Gather(comm_in, comm_out, *sems).start()

            @pl.when((hp == HEAD_PAIRS - 1) & (step_id == 0))
            def _():
                _Gather(comm_in, comm_out, *sems).pass_on()

        @pl.when(ki == 0)
        def _():
            for hh in range(PAIR):
                m_s, l_s, a_s, acc_s = scr[hh * per_head:hh * per_head + 4]
                m_s[...] = jnp.full_like(m_s, -jnp.inf)
                l_s[...] = jnp.zeros_like(l_s)
                acc_s[...] = jnp.zeros_like(acc_s)

        def step(diagonal):
            q2 = q_ref[...] * scale
            k2 = k_ref[...]
            v2 = v_ref[...]
            scores = []
            for hh in range(PAIR):
                qm = jnp.where(_head_lanes(hh), q2, jnp.zeros_like(q2))
                scores.append(lax.dot_general(qm, k2, NT, preferred_element_type=F32))
            for hh in range(PAIR):
                m_s, l_s, a_s, acc_s = scr[hh * per_head:(hh + 1) * per_head]
                s_s = scores[hh]
                hi_rows, lo_rows = [], []
                for r in range(t // chunk):
                    rows = slice(r * chunk, (r + 1) * chunk)
                    blocks = [_masked(s_s[rows, _lane_block(b)] - c_ref[hh, :, _lane_block(b)], r * chunk,
                                      b * LANES, diagonal) for b in range(t // LANES)]
                    m_prev = m_s[rows, :]
                    m_new = jnp.maximum(m_prev, jnp.max(_fold(jnp.maximum, blocks), axis=1, keepdims=True))
                    alpha = jnp.exp(m_prev - m_new)
                    ps = [jnp.exp(blk - m_new) for blk in blocks]
                    l_s[rows, :] = alpha * l_s[rows, :] + jnp.sum(_fold(jnp.add, ps), axis=1, keepdims=True)
                    m_s[rows, :] = m_new
                    a_s[rows, :] = alpha
                    his = [p.astype(BF16) for p in ps]
                    hi_rows.append(jnp.concatenate(his, axis=1))
                    lo_rows.append(jnp.concatenate([(p - h.astype(F32)).astype(BF16) for p, h in zip(ps, his)],
                                                   axis=1))
                pv = (lax.dot_general(jnp.concatenate(hi_rows, axis=0), v2, NN, preferred_element_type=F32)
                      + lax.dot_general(jnp.concatenate(lo_rows, axis=0), v2, NN, preferred_element_type=F32))
                acc_s[...] = a_s[...] * acc_s[...] + pv

        @pl.when(ki < qi)
        def _():
            step(False)

        @pl.when(ki == qi)
        def _():
            step(True)
            heads = []
            for hh in range(PAIR):
                m_s, l_s, a_s, acc_s = scr[hh * per_head:hh * per_head + 4]
                heads.append(acc_s[...] / l_s[...])
                lse_ref[hh] = m_s[...] + jnp.log(l_s[...])
            o2 = _pick(_head_lanes(0), heads[0], heads[1])
            o_ref[...] = o2
            ob_ref[...] = o2.astype(BF16)

        if nc:
            @pl.when((hp == HEAD_PAIRS - 1) & (step_id == steps - 1))
            def _():
                _Gather(comm_in, comm_out, *sems).finish()

    def q_cols(first_block):
        return pl.BlockSpec((t, LANES), lambda h, s, qt, kt: (qt[s], first_block + h))

    def k_cols(first_block):
        return pl.BlockSpec((t, LANES), lambda h, s, qt, kt: (kt[s], first_block + h))

    any_spec = pl.BlockSpec(memory_space=pl.ANY)
    head_scratch = [pltpu.VMEM((t, LANES), F32)] * per_head
    grid_spec = pltpu.PrefetchScalarGridSpec(
        num_scalar_prefetch=2, grid=(HEAD_PAIRS, steps),
        in_specs=[q_cols(0), k_cols(HEAD_PAIRS), k_cols(2 * HEAD_PAIRS),
                  pl.BlockSpec((PAIR, 1, t), lambda h, s, qt, kt: (h, 0, kt[s]))] + [any_spec] * nc,
        out_specs=[q_cols(0), q_cols(0),
                   pl.BlockSpec((PAIR, t, LANES), lambda h, s, qt, kt: (h, qt[s], 0))] + [any_spec] * nc,
        scratch_shapes=head_scratch * PAIR + _comm_scratch(nc))
    return pl.pallas_call(
        body, name="fox_fwd", grid_spec=grid_spec,
        out_shape=[jax.ShapeDtypeStruct((S, FOX_WIDTH), F32), jax.ShapeDtypeStruct((S, FOX_WIDTH), BF16),
                   jax.ShapeDtypeStruct((FOX_HEADS, S, LANES), F32)] + _comm_shapes(comm),
        compiler_params=_params("arbitrary", "arbitrary"))(q_tab, k_tab, qkv, qkv, qkv, cT, *comm)


def _fox_bwd(qkv, cT, o, lse, do, comm):
    S = qkv.shape[0]
    t = _row_tile(S)
    n = S // t
    nc = len(comm)
    scale = 1.0 / math.sqrt(FOX_HEAD_DIM)
    chunk = min(FOX_CHUNK_BWD, t)
    per_head = 2
    q_tab, k_tab = _causal_pairs(n, False)
    steps = q_tab.shape[0]

    def body(qt_ref, kt_ref, q_ref, k_ref, v_ref, c_ref, o_ref, do_ref, lse_ref, *rest):
        comm_in = rest[:nc]
        dq_ref, dk_ref, dv_ref, dc_ref = rest[nc:nc + 4]
        comm_out = rest[nc + 4:2 * nc + 4]
        dq_s, dk_s, dv_s = rest[2 * nc + 4:2 * nc + 7]
        scr = rest[2 * nc + 7:2 * nc + 7 + PAIR * per_head]
        sems = rest[2 * nc + 7 + PAIR * per_head:]
        hp = pl.program_id(0)
        step_id = pl.program_id(1)
        qi = qt_ref[step_id]
        ki = kt_ref[step_id]

        if nc:
            @pl.when((hp == 0) & (step_id == 0))
            def _():
                for cp in _comm_copies(comm_in, comm_out, *sems):
                    cp.start()

        @pl.when(step_id == 0)
        def _():
            dq_s[...] = jnp.zeros_like(dq_s)

        @pl.when(qi == ki)
        def _():
            dk_s[...] = jnp.zeros_like(dk_s)
            dv_s[...] = jnp.zeros_like(dv_s)
            for hh in range(PAIR):
                dc_s = scr[hh * per_head]
                dc_s[...] = jnp.zeros_like(dc_s)

        def step(diagonal):
            q2 = q_ref[...]
            k2 = k_ref[...]
            v2 = v_ref[...]
            do2 = do_ref[...]
            prod = do2.astype(F32) * o_ref[...]
            grads = []
            for hh in range(PAIR):
                dc_s, delta_s = scr[hh * per_head:(hh + 1) * per_head]
                mine = _head_lanes(hh)
                s_s = lax.dot_general(jnp.where(mine, q2 * scale, jnp.zeros_like(q2)), k2, NT,
                                      preferred_element_type=F32)
                dp_s = lax.dot_general(jnp.where(mine, do2, jnp.zeros_like(do2)), v2, NT, preferred_element_type=F32)
                delta_s[...] = jnp.broadcast_to(jnp.sum(jnp.where(mine, prod, 0.0), axis=1, keepdims=True),
                                                (t, LANES))
                dc8 = [jnp.zeros((8, LANES), F32) for _ in range(t // LANES)]
                p_rows, ds_rows = [], []
                for r in range(t // chunk):
                    rows = slice(r * chunk, (r + 1) * chunk)
                    lse = lse_ref[hh, rows, :]
                    delta = delta_s[rows, :]
                    p_blocks, ds_blocks = [], []
                    for b in range(t // LANES):
                        s = _masked(s_s[rows, _lane_block(b)] - c_ref[hh, :, _lane_block(b)], r * chunk, b * LANES,
                                    diagonal)
                        p = jnp.exp(s - lse)
                        ds = p * (dp_s[rows, _lane_block(b)] - delta)
                        p_blocks.append(p.astype(BF16))
                        ds_blocks.append(ds.astype(BF16))
                        dc8[b] = dc8[b] + jnp.sum(ds.reshape(chunk // 8, 8, LANES), axis=0)
                    p_rows.append(jnp.concatenate(p_blocks, axis=1))
                    ds_rows.append(jnp.concatenate(ds_blocks, axis=1))
                for b in range(t // LANES):
                    dc_s[:, _lane_block(b)] += jnp.sum(dc8[b], axis=0, keepdims=True)
                dsb = jnp.concatenate(ds_rows, axis=0)
                grads.append((lax.dot_general(jnp.concatenate(p_rows, axis=0), do2, TN, preferred_element_type=F32),
                              lax.dot_general(dsb, k2, NN, preferred_element_type=F32),
                              lax.dot_general(dsb, q2, TN, preferred_element_type=F32)))
            first = _head_lanes(0)
            dv_s[...] += _pick(first, grads[0][0], grads[1][0])
            q_rows = pl.ds(pl.multiple_of(qi * t, t), t)
            dq_s[q_rows, :] += _pick(first, grads[0][1], grads[1][1]) * scale
            dk_s[...] += _pick(first, grads[0][2], grads[1][2]) * scale

        @pl.when(qi > ki)
        def _():
            step(False)

        @pl.when(qi == ki)
        def _():
            step(True)

        @pl.when(qi == n - 1)
        def _():
            dk_ref[...] = dk_s[...].astype(BF16)
            dv_ref[...] = dv_s[...].astype(BF16)
            for hh in range(PAIR):
                dc_ref[hh] = -scr[hh * per_head][...]

        @pl.when(step_id == steps - 1)
        def _():
            dq_ref[...] = dq_s[...].astype(BF16)

        if nc:
            @pl.when((hp == HEAD_PAIRS - 1) & (step_id == steps - 1))
            def _():
                for cp in _comm_copies(comm_in, comm_out, *sems):
                    cp.wait()

    def q_side(first_block):
        return pl.BlockSpec((t, LANES), lambda h, s, qt, kt: (qt[s], first_block + h))

    def k_side(first_block):
        return pl.BlockSpec((t, LANES), lambda h, s, qt, kt: (kt[s], first_block + h))

    any_spec = pl.BlockSpec(memory_space=pl.ANY)
    head_scratch = [pltpu.VMEM((1, t), F32), pltpu.VMEM((t, LANES), F32)]
    grad_shape = jax.ShapeDtypeStruct((S, FOX_WIDTH), BF16)
    grid_spec = pltpu.PrefetchScalarGridSpec(
        num_scalar_prefetch=2, grid=(HEAD_PAIRS, steps),
        in_specs=[q_side(0), k_side(HEAD_PAIRS), k_side(2 * HEAD_PAIRS),
                  pl.BlockSpec((PAIR, 1, t), lambda h, s, qt, kt: (h, 0, kt[s])), q_side(0), q_side(0),
                  pl.BlockSpec((PAIR, t, LANES), lambda h, s, qt, kt: (h, qt[s], 0))] + [any_spec] * nc,
        out_specs=[pl.BlockSpec((S, LANES), lambda h, s, qt, kt: (0, h)), k_side(0), k_side(0),
                   pl.BlockSpec((PAIR, 1, t), lambda h, s, qt, kt: (h, 0, kt[s]))] + [any_spec] * nc,
        scratch_shapes=[pltpu.VMEM((S, LANES), F32), pltpu.VMEM((t, LANES), F32), pltpu.VMEM((t, LANES), F32)]
        + head_scratch * PAIR + _comm_scratch(nc))
    return pl.pallas_call(
        body, name="fox_bwd", grid_spec=grid_spec,
        out_shape=[grad_shape, grad_shape, grad_shape, jax.ShapeDtypeStruct((FOX_HEADS, 1, S), F32)]
        + _comm_shapes(comm),
        compiler_params=_params("arbitrary", "arbitrary"))(q_tab, k_tab, qkv, qkv, qkv, cT, o, do, lse, *comm)


def _lanes(g):
    return slice(g * POOL_GROUP_DIM, (g + 1) * POOL_GROUP_DIM)


def _window_sum(e, win, back):
    rows = e.shape[0]
    s = e
    sh = 1
    while sh < win:
        s = s + pltpu.roll(s, sh if back else rows - sh, 0)
        sh *= 2
    return s


def _pooled(u_ref, up_ref, i, g, win, T):
    cur = u_ref[:, _lanes(g)]
    tail = jnp.where(i > 0, up_ref[T - POOL_HALO:T, _lanes(g)], 0.0)
    e = jnp.concatenate([tail, cur], axis=0)
    s = _window_sum(e, win, True)
    t_idx = i * T - POOL_HALO + lax.broadcasted_iota(jnp.int32, (T + POOL_HALO, POOL_GROUP_DIM), 0)
    cnt = jnp.clip(t_idx + 1, 1, win).astype(F32)
    return (s / cnt - e)[POOL_HALO:, :]


def _pool_fwd(uf, pw, ps):
    S = uf.shape[0]
    T = _row_tile(S)

    def body(u_ref, up_ref, w_ref, sc_ref, o_ref):
        i = pl.program_id(0)
        for g, win in enumerate(POOL_WINDOWS):
            pb = _pooled(u_ref, up_ref, i, g, win, T).astype(BF16)
            yv = lax.dot_general(pb, w_ref[g], NN, preferred_element_type=F32)
            o_ref[:, _lanes(g)] = (yv * sc_ref[:, _lanes(g)]).astype(BF16)

    return pl.pallas_call(
        body, name="pool_fwd", grid=(S // T,),
        in_specs=[pl.BlockSpec((T, POOL_WIDTH), lambda i: (i, 0)),
                  pl.BlockSpec((T, POOL_WIDTH), lambda i: (jnp.maximum(i - 1, 0), 0)),
                  pl.BlockSpec((4, POOL_GROUP_DIM, POOL_GROUP_DIM), lambda i: (0, 0, 0)),
                  pl.BlockSpec((1, POOL_WIDTH), lambda i: (0, 0))],
        out_specs=pl.BlockSpec((T, POOL_WIDTH), lambda i: (i, 0)),
        out_shape=jax.ShapeDtypeStruct((S, POOL_WIDTH), BF16), compiler_params=_params("parallel"))(uf, uf, pw, ps)


def _pool_bwd(uf, dpool, pw, ps):
    S = uf.shape[0]
    T = _row_tile(S)
    nb = S // T

    def body(u_ref, up_ref, d_ref, dn_ref, w_ref, sc_ref, du_ref, dw_ref, dsc_ref):
        i = pl.program_id(0)

        @pl.when(i == 0)
        def _():
            dw_ref[...] = jnp.zeros_like(dw_ref)
            dsc_ref[...] = jnp.zeros_like(dsc_ref)

        t_idx = i * T + lax.broadcasted_iota(jnp.int32, (T + POOL_HALO, POOL_GROUP_DIM), 0)
        for g, win in enumerate(POOL_WINDOWS):
            pb = _pooled(u_ref, up_ref, i, g, win, T).astype(BF16)
            w = w_ref[g]
            sc = sc_ref[:, _lanes(g)]
            yv = lax.dot_general(pb, w, NN, preferred_element_type=F32)
            dov = d_ref[:, _lanes(g)]
            dsc_ref[:, _lanes(g)] += jnp.sum(dov * yv, axis=0, keepdims=True)
            head = jnp.where(i < nb - 1, dn_ref[0:POOL_HALO, _lanes(g)], 0.0)
            dyb = (jnp.concatenate([dov, head], axis=0) * sc).astype(BF16)
            dw_ref[g] += lax.dot_general(pb, dyb[:T], TN, preferred_element_type=F32)
            dpl = lax.dot_general(dyb, w, NT, preferred_element_type=F32)
            cnt = jnp.minimum(t_idx + 1, win).astype(F32)
            a = _window_sum(dpl / cnt, win, False)
            du_ref[:, _lanes(g)] = (a - dpl)[:T].astype(BF16)

    return pl.pallas_call(
        body, name="pool_bwd", grid=(nb,),
        in_specs=[pl.BlockSpec((T, POOL_WIDTH), lambda i: (i, 0)),
                  pl.BlockSpec((T, POOL_WIDTH), lambda i: (jnp.maximum(i - 1, 0), 0)),
                  pl.BlockSpec((T, POOL_WIDTH), lambda i: (i, 0)),
                  pl.BlockSpec((T, POOL_WIDTH), lambda i: (jnp.minimum(i + 1, nb - 1), 0)),
                  pl.BlockSpec((4, POOL_GROUP_DIM, POOL_GROUP_DIM), lambda i: (0, 0, 0)),
                  pl.BlockSpec((1, POOL_WIDTH), lambda i: (0, 0))],
        out_specs=[pl.BlockSpec((T, POOL_WIDTH), lambda i: (i, 0)),
                   pl.BlockSpec((4, POOL_GROUP_DIM, POOL_GROUP_DIM), lambda i: (0, 0, 0)),
                   pl.BlockSpec((1, POOL_WIDTH), lambda i: (0, 0))],
        out_shape=[jax.ShapeDtypeStruct((S, POOL_WIDTH), BF16),
                   jax.ShapeDtypeStruct((4, POOL_GROUP_DIM, POOL_GROUP_DIM), F32),
                   jax.ShapeDtypeStruct((1, POOL_WIDTH), F32)],
        compiler_params=_params("arbitrary"))(uf, uf, dpool, dpool, pw, ps)


def _xhead(h):
    return slice(h * X_HEAD_DIM, (h + 1) * X_HEAD_DIM)


def _xvhead(h):
    return slice(D_MODEL + h * X_HEAD_DIM, D_MODEL + (h + 1) * X_HEAD_DIM)


X_CHUNK = 32


def _x_probs(s_ref, rows):
    blocks = [s_ref[rows, _lane_block(b)] * (1.0 / math.sqrt(X_HEAD_DIM)) for b in range(MEM_LEN // LANES)]
    m = jnp.max(_fold(jnp.maximum, blocks), axis=1, keepdims=True)
    es = [jnp.exp(blk - m) for blk in blocks]
    den = jnp.sum(_fold(jnp.add, es), axis=1, keepdims=True)
    return [e / den for e in es]


def _xattn_fwd(q, kv):
    S = q.shape[0]
    t = _row_tile(S)
    chunk = min(X_CHUNK, t)

    def body(q_ref, kv_ref, o_ref):
        for h in range(X_HEADS):
            s = lax.dot_general(q_ref[:, _xhead(h)], kv_ref[:, _xhead(h)], NT, preferred_element_type=F32)
            p_rows = []
            for r in range(t // chunk):
                rows = slice(r * chunk, (r + 1) * chunk)
                p_rows.append(jnp.concatenate([p.astype(BF16) for p in _x_probs(s, rows)], axis=1))
            o_ref[:, _xhead(h)] = lax.dot_general(jnp.concatenate(p_rows, axis=0), kv_ref[:, _xvhead(h)], NN,
                                                  preferred_element_type=F32).astype(BF16)

    return pl.pallas_call(
        body, name="xattn_fwd", grid=(S // t,),
        in_specs=[pl.BlockSpec((t, D_MODEL), lambda i: (i, 0)), pl.BlockSpec((MEM_LEN, 2 * D_MODEL), lambda i: (0, 0))],
        out_specs=pl.BlockSpec((t, D_MODEL), lambda i: (i, 0)),
        out_shape=jax.ShapeDtypeStruct((S, D_MODEL), BF16), compiler_params=_params("parallel"))(q, kv)


def _xattn_bwd(q, kv, do):
    S = q.shape[0]
    t = _row_tile(S)
    nb = S // t
    scale = 1.0 / math.sqrt(X_HEAD_DIM)
    chunk = min(X_CHUNK, t)

    def body(q_ref, kv_ref, do_ref, dq_ref, dkv_ref, acc):
        i = pl.program_id(0)

        @pl.when(i == 0)
        def _():
            acc[...] = jnp.zeros_like(acc)

        for h in range(X_HEADS):
            qh = q_ref[:, _xhead(h)]
            kh = kv_ref[:, _xhead(h)]
            doh = do_ref[:, _xhead(h)]
            s_s = lax.dot_general(qh, kh, NT, preferred_element_type=F32)
            dp_s = lax.dot_general(doh, kv_ref[:, _xvhead(h)], NT, preferred_element_type=F32)
            p_rows, ds_rows = [], []
            for r in range(t // chunk):
                rows = slice(r * chunk, (r + 1) * chunk)
                ps = _x_probs(s_s, rows)
                dps = [dp_s[rows, _lane_block(b)] for b in range(len(ps))]
                inner = jnp.sum(_fold(jnp.add, [dp * p for dp, p in zip(dps, ps)]), axis=1, keepdims=True)
                p_rows.append(jnp.concatenate([p.astype(BF16) for p in ps], axis=1))
                ds_rows.append(jnp.concatenate([(p * (dp - inner)).astype(BF16) for dp, p in zip(dps, ps)], axis=1))
            dsb = jnp.concatenate(ds_rows, axis=0)
            acc[:, _xvhead(h)] += lax.dot_general(jnp.concatenate(p_rows, axis=0), doh, TN,
                                                  preferred_element_type=F32)
            dq_ref[:, _xhead(h)] = (lax.dot_general(dsb, kh, NN, preferred_element_type=F32) * scale).astype(BF16)
            acc[:, _xhead(h)] += lax.dot_general(dsb, qh, TN, preferred_element_type=F32) * scale

        @pl.when(i == nb - 1)
        def _():
            dkv_ref[...] = acc[...].astype(BF16)

    row = pl.BlockSpec((t, D_MODEL), lambda i: (i, 0))
    full = pl.BlockSpec((MEM_LEN, 2 * D_MODEL), lambda i: (0, 0))
    return pl.pallas_call(
        body, name="xattn_bwd", grid=(nb,), in_specs=[row, full, row], out_specs=[row, full],
        out_shape=[jax.ShapeDtypeStruct((S, D_MODEL), BF16), jax.ShapeDtypeStruct((MEM_LEN, 2 * D_MODEL), BF16)],
        scratch_shapes=[pltpu.VMEM((MEM_LEN, 2 * D_MODEL), F32)],
        compiler_params=_params("arbitrary"))(q, kv, do)


def _comm_shapes(arrs):
    return [jax.ShapeDtypeStruct((N_DEV,) + tuple(a.shape[-2:]), a.dtype) for a in arrs]


def _comm_scratch(n):
    if n == 0:
        return []
    return [pltpu.SemaphoreType.DMA((n, N_DEV - 1)), pltpu.SemaphoreType.DMA((n, N_DEV - 1)),
            pltpu.SemaphoreType.DMA((n,))]


def _comm_copies(ins, outs, send_sems, recv_sems, local_sems):
    x, y, c = lax.axis_index("x"), lax.axis_index("y"), lax.axis_index("c")
    me = 4 * x + 2 * y + c
    copies = []
    for w in range(len(ins)):
        src = ins[w] if len(ins[w].shape) == 2 else ins[w].at[me]
        copies.append(pltpu.make_async_copy(src, outs[w].at[me], local_sems.at[w]))
    for k in range(1, N_DEV):
        px = 1 - x if k & 4 else x
        py = 1 - y if k & 2 else y
        pc = 1 - c if k & 1 else c
        peer = 4 * px + 2 * py + pc
        for w in range(len(ins)):
            src = ins[w] if len(ins[w].shape) == 2 else ins[w].at[peer]
            copies.append(pltpu.make_async_remote_copy(
                src_ref=src, dst_ref=outs[w].at[me], send_sem=send_sems.at[w, k - 1],
                recv_sem=recv_sems.at[w, k - 1], device_id=(px, py, pc), device_id_type=pl.DeviceIdType.MESH))
    return copies


class _Gather:
    def __init__(self, ins, outs, send_sems, recv_sems, local_sems):
        x, y, c = lax.axis_index("x"), lax.axis_index("y"), lax.axis_index("c")
        me = 4 * x + 2 * y + c
        sibling = (x, y, 1 - c)
        self.local, self.mine, self.passed = [], [], []
        for w in range(len(ins)):
            def remote(idx, src, slot, dev, w=w):
                return pltpu.make_async_remote_copy(
                    src_ref=src, dst_ref=outs[w].at[slot], send_sem=send_sems.at[w, idx],
                    recv_sem=recv_sems.at[w, idx], device_id=dev, device_id_type=pl.DeviceIdType.MESH)

            self.local.append(pltpu.make_async_copy(ins[w], outs[w].at[me], local_sems.at[w]))
            mine, passed = [remote(0, ins[w], me, sibling)], []
            for j, (fx, fy) in enumerate(((0, 1), (1, 0), (1, 1))):
                px = 1 - x if fx else x
                py = 1 - y if fy else y
                slot = 4 * px + 2 * py + c
                mine.append(remote(1 + j, ins[w], me, (px, py, c)))
                passed.append(remote(4 + j, outs[w].at[slot], slot, sibling))
            self.mine.append(mine)
            self.passed.append(passed)

    def start(self):
        for cp in self.local:
            cp.start()
        for mine in self.mine:
            for cp in mine:
                cp.start()

    def pass_on(self):
        for mine, passed in zip(self.mine, self.passed):
            for j, cp in enumerate(passed):
                mine[1 + j].wait_recv()
                cp.start()

    def finish(self):
        for mine, passed in zip(self.mine, self.passed):
            mine[0].wait_recv()
            for cp in passed:
                cp.wait_recv()
            for cp in mine + passed:
                cp.wait_send()
        for cp in self.local:
            cp.wait()


def _exchange(name, arrs):
    n = len(arrs)
    gather = all(a.ndim == 2 for a in arrs)

    def body(*refs):
        if gather:
            g = _Gather(refs[:n], refs[n:2 * n], *refs[2 * n:])
            g.start()
            g.pass_on()
            g.finish()
            return
        copies = _comm_copies(refs[:n], refs[n:2 * n], *refs[2 * n:])
        for cp in copies:
            cp.start()
        for cp in copies:
            cp.wait()

    any_spec = pl.BlockSpec(memory_space=pl.ANY)
    return pl.pallas_call(
        body, name=name, in_specs=[any_spec] * n, out_specs=[any_spec] * n, out_shape=_comm_shapes(arrs),
        scratch_shapes=_comm_scratch(n))(*arrs)


def _adamw_math(w, g, m, v):
    m = ADAM_B1 * m + (1.0 - ADAM_B1) * g
    v = ADAM_B2 * v + (1.0 - ADAM_B2) * (g * g)
    m_hat = m / (1.0 - ADAM_B1 ** ADAM_STEP)
    v_hat = v / (1.0 - ADAM_B2 ** ADAM_STEP)
    delta = -ADAM_LR * (m_hat / (jnp.sqrt(v_hat) + ADAM_EPS) + ADAM_WD * w)
    return delta, m, v


def _sum_parts(p_ref):
    g = p_ref[0].astype(F32)
    for s in range(1, N_DEV):
        g = g + p_ref[s].astype(F32)
    return g


def _adamw_big(name, w, m, v, parts, tr, comm=()):
    L, R, C = w.shape
    nc = len(comm)
    gather = all(a.ndim == 2 for a in comm)
    nr = R // tr

    def exchange(comm_in, comm_out, sems, begin):
        if gather:
            g = _Gather(comm_in, comm_out, *sems)
            if begin:
                g.start()
            else:
                g.pass_on()
                g.finish()
        else:
            for cp in _comm_copies(comm_in, comm_out, *sems):
                cp.start() if begin else cp.wait()

    def body(w_ref, m_ref, v_ref, *rest):
        p_refs = rest[:L]
        comm_in = rest[L:L + nc]
        g_ref, d_ref, nm_ref, nv_ref = rest[L + nc:L + nc + 4]
        comm_out = rest[L + nc + 4:L + 2 * nc + 4]
        sems = rest[L + 2 * nc + 4:]
        layer = pl.program_id(0)
        if nc:
            @pl.when((layer == 0) & (pl.program_id(1) == 0))
            def _():
                exchange(comm_in, comm_out, sems, True)

        for j in range(L):
            @pl.when(layer == j)
            def _(j=j):
                g = _sum_parts(p_refs[j])
                delta, nm, nv = _adamw_math(w_ref[...], g, m_ref[...], v_ref[...])
                g_ref[...] = g
                d_ref[...] = delta
                nm_ref[...] = nm
                nv_ref[...] = nv

        if nc:
            @pl.when((layer == L - 1) & (pl.program_id(1) == nr - 1))
            def _():
                exchange(comm_in, comm_out, sems, False)

    blk = pl.BlockSpec((None, tr, C), lambda l, i: (l, i, 0))

    def part_spec(j):
        return pl.BlockSpec((N_DEV, tr, C), lambda l, i: (0, jnp.where(l == j, i, 0), 0))

    shp = jax.ShapeDtypeStruct((L, R, C), F32)
    any_spec = pl.BlockSpec(memory_space=pl.ANY)
    return pl.pallas_call(
        body, name=name, grid=(L, nr),
        in_specs=[blk, blk, blk] + [part_spec(j) for j in range(L)] + [any_spec] * nc,
        out_specs=[blk] * 4 + [any_spec] * nc, out_shape=[shp] * 4 + _comm_shapes(comm),
        scratch_shapes=_comm_scratch(nc),
        compiler_params=_params("arbitrary", "arbitrary"))(w, m, v, *parts, *comm)


def _adamw_small(w, m, v, parts):
    R, C = w.shape

    def body(w_ref, m_ref, v_ref, p_ref, g_ref, d_ref, nm_ref, nv_ref):
        g = _sum_parts(p_ref)
        delta, nm, nv = _adamw_math(w_ref[...], g, m_ref[...], v_ref[...])
        g_ref[...] = g
        d_ref[...] = delta
        nm_ref[...] = nm
        nv_ref[...] = nv

    shp = jax.ShapeDtypeStruct((R, C), F32)
    return pl.pallas_call(body, name="adamw_small", out_shape=[shp] * 4,
                          compiler_params=pltpu.CompilerParams(vmem_limit_bytes=VMEM_LIMIT))(w, m, v, parts)


def _vec(a):
    return a.reshape(1, -1)


W_IN_SHARD = IN_COLS // N_DEV
W_IN_ROWS = 272


def _w_in_travel(a):
    pad = [(0, 0)] * (a.ndim - 2) + [(0, W_IN_ROWS - W_IN_SHARD), (0, 0)]
    return jnp.pad(jnp.swapaxes(a, -1, -2), pad)


def _unpack_w_in(g):
    nat = g[:, :W_IN_SHARD, :].reshape(IN_COLS, D_MODEL)
    f = jnp.pad(nat[QKV_COLS:QKV_COLS + FOX_HEADS], ((0, UF_COLS - POOL_WIDTH - FOX_HEADS), (0, 0)))
    return jnp.concatenate([nat[:QKV_COLS], nat[QKV_COLS + FOX_HEADS:], f], axis=0)


def _pack_dw_in(dwp_t):
    nat = jnp.concatenate([dwp_t[:QKV_COLS], dwp_t[QKV_COLS + POOL_WIDTH:QKV_COLS + POOL_WIDTH + FOX_HEADS],
                           dwp_t[QKV_COLS:QKV_COLS + POOL_WIDTH]], axis=0)
    return jnp.pad(nat.reshape(N_DEV, W_IN_SHARD, D_MODEL), ((0, 0), (0, W_IN_ROWS - W_IN_SHARD), (0, 0)))


REST = ['w_out', 'wq_x', 'wkv_x', 'wo_x', 'w_up', 'w_down']


def _layer_fwd(x0, h1, mem, sp, g_in, shards, g_next):
    S = x0.shape[0]
    sv = {"x0": x0}
    w_inp = _unpack_w_in(g_in)
    qkv, uf = _mm_rows("mm_in", [(h1, w_inp, "nt")],
                       [(BF16, 0, QKV_COLS, "id"), (F32, QKV_COLS, UF_COLS, "id")], piece=UF_COLS)
    c = _gate_fwd(uf, sp["b_forget"])
    cT = jnp.transpose(c[:, :FOX_HEADS]).reshape(FOX_HEADS, 1, S)
    o, ob, lse, *got = _fox_fwd(qkv, cT, shards)
    g_out, g_q, g_kv, g_o, g_up, g_down = got[:6]
    W = dict(inp=w_inp, out=g_out.reshape(D_MODEL, D_MODEL), q=g_q.reshape(D_MODEL, D_MODEL), kv=g_kv,
             o=g_o.reshape(D_MODEL, D_MODEL), up=g_up, down=g_down.reshape(D_FF, D_MODEL))
    pool = _pool_fwd(uf, sp["pool_w"], sp["pool_scale"])
    cat = jnp.concatenate([ob, pool], axis=1)
    mix, x1, h2 = _mm_resid_norm("mm_sq_norm", cat, W["out"], x0, sp["g_mix_post"], sp["g_x_pre"])
    mn = _norm_fwd("norm_mem", mem, sp["g_mem"])
    q2 = _mm1("mm_q", h2, W["q"], "nn", D_MODEL, BF16)
    kv = _mm1("mm_kv", mn, W["kv"], "nn3", 2 * D_MODEL, BF16, piece=2 * D_MODEL // N_DEV)
    o2 = _xattn_fwd(q2, kv)
    xo, x2, h3 = _mm_resid_norm("mm_sq_norm", o2, W["o"], x1, sp["g_x_post"], sp["g_ffn_pre"])
    up, act = _mm_rows("mm_up", [(h3, W["up"], "nn3")], [(BF16, 0, D_FF, "id"), (BF16, 0, D_FF, "relu2")],
                       piece=D_FF // N_DEV)
    y, x3, h_next = _mm_resid_norm("mm_down_norm" if g_next is not None else "mm_down_norm_last", act, W["down"], x2,
                                   sp["g_ffn_post"], g_next)
    sv.update(h1=h1, uf=uf, cT=cT, qkv=qkv, o=o, lse=lse, cat=cat, mix=mix, x1=x1, h2=h2, mn=mn, q2=q2, kv=kv,
              o2=o2, xo=xo, x2=x2, h3=h3, up=up, act=act, y=y)
    return x3, h_next, sv, W, (got[6] if len(got) > 6 else None)


def _layer_bwd(dx3, dy, mem, sv, sp, W, carried, below):
    S = dx3.shape[0]
    gs = {}
    gb = {}
    (dup,) = _mm_rows("mm_dup", [(dy, W["down"], "nt")], [(BF16, 0, D_FF, "drelu2")], extra=sv["up"])
    gb["w_down"] = _mm_tn("mm_dw_down", sv["act"], dy, BF16).reshape(N_DEV, D_FF // N_DEV, D_MODEL)
    gb["w_up"] = _mm_tn("mm_dw_up", sv["h3"], dup, BF16, shard_cols=D_FF // N_DEV)
    dx2, gs["g_ffn_pre"], dxo, gs["g_x_post"] = _norm_bwd(
        "mm_dh3_norm_bwd", (dup, W["up"], "nt3"), sv["x2"], sp["g_ffn_pre"], dx3, F32,
        below=(sv["xo"], sp["g_x_post"]))
    do2 = _mm1("mm_sq_t", dxo, W["o"], "nt", D_MODEL, BF16)
    gb["wo_x"] = _mm_tn("mm_dw_sq", sv["o2"], dxo, BF16).reshape(N_DEV, D_MODEL // N_DEV, D_MODEL)
    dq2, dkvb = _xattn_bwd(sv["q2"], sv["kv"], do2)
    gb["wq_x"] = _mm_tn("mm_dw_sq", sv["h2"], dq2, BF16).reshape(N_DEV, D_MODEL // N_DEV, D_MODEL)
    gb["wkv_x"] = _mm_tn("mm_dw_kv", sv["mn"], dkvb, BF16, shard_cols=2 * D_MODEL // N_DEV)
    dmn = _mm1("mm_dmn", dkvb, W["kv"], "nt3", D_MODEL, F32)
    _, gs["g_mem"] = _norm_bwd("norm_bwd_mem", dmn, mem, sp["g_mem"], None, BF16)
    dx1, gs["g_x_pre"], dmix, gs["g_mix_post"] = _norm_bwd(
        "mm_dh2_norm_bwd", (dq2, W["q"], "nt"), sv["x1"], sp["g_x_pre"], dx2, F32,
        below=(sv["mix"], sp["g_mix_post"]))
    doh, dpool = _mm_rows("mm_dcat", [(dmix, W["out"], "nt")],
                          [(BF16, 0, FOX_WIDTH, "id"), (F32, FOX_WIDTH, POOL_WIDTH, "id")])
    gb["w_out"] = _mm_tn("mm_dw_sq", sv["cat"], dmix, BF16).reshape(N_DEV, D_MODEL // N_DEV, D_MODEL)
    du, gs["pool_w"], gs["pool_scale"] = _pool_bwd(sv["uf"], dpool, sp["pool_w"], sp["pool_scale"])
    dq, dk, dv, dcT, *got = _fox_bwd(sv["qkv"], sv["cT"], sv["o"], sv["lse"], doh, [gb[n] for n in REST] + carried)
    dc = jnp.pad(jnp.transpose(dcT.reshape(FOX_HEADS, S)), ((0, 0), (0, LANES - FOX_HEADS)))
    dfg, db = _gate_bwd(dc, sv["uf"], sp["b_forget"])
    gs["b_forget"] = db[:, :FOX_HEADS]
    dproj = [dq, dk, dv, du, dfg]
    dwp = _mm_tn_rows("mm_dw_in", dproj, sv["h1"], BF16)
    dh1 = (dproj, W["inp"], "nn")
    if below is None:
        dx0, gs["g_mix_pre"] = _norm_bwd("mm_dh1_norm_bwd_first", dh1, sv["x0"], sp["g_mix_pre"], dx1, F32)
        lower = None
    else:
        dx0, gs["g_mix_pre"], *lower = _norm_bwd("mm_dh1_norm_bwd", dh1, sv["x0"], sp["g_mix_pre"], dx1, F32,
                                                 below=below)
    return dx0, lower, dict(zip(REST, got[:6])), got[6:], _pack_dw_in(dwp), gs


def _small_rows(shape):
    return -(-math.prod(shape) // (8 * LANES)) * 8


def _pack_small(d):
    blocks = []
    for n in SMALL:
        rows = _small_rows(d[n].shape)
        if d[n].shape[-1] == LANES:
            blocks.append(d[n].reshape(rows, LANES))
        else:
            flat = d[n].reshape(-1)
            blocks.append(jnp.pad(flat, (0, rows * LANES - flat.shape[0])).reshape(rows, LANES))
    return jnp.concatenate(blocks, axis=0)


def _unpack_small(packed, like):
    out = {}
    row = 0
    for n in SMALL:
        shape = like[n].shape
        rows = _small_rows(shape)
        block = packed[row:row + rows]
        out[n] = block.reshape(shape) if shape[-1] == LANES else block.reshape(-1)[:math.prod(shape)].reshape(shape)
        row += rows
    return out


def kernel(x, mem, g_mix_pre, w_in, b_forget, pool_w, pool_scale, w_out, g_mix_post, g_x_pre, g_mem, wq_x, wkv_x, wo_x, g_x_post, g_ffn_pre, w_up, w_down, g_ffn_post, loss_target, m_g_mix_pre, m_w_in, m_b_forget, m_pool_w, m_pool_scale, m_w_out, m_g_mix_post, m_g_x_pre, m_g_mem, m_wq_x, m_wkv_x, m_wo_x, m_g_x_post, m_g_ffn_pre, m_w_up, m_w_down, m_g_ffn_post, v_g_mix_pre, v_w_in, v_b_forget, v_pool_w, v_pool_scale, v_w_out, v_g_mix_post, v_g_x_pre, v_g_mem, v_wq_x, v_wkv_x, v_wo_x, v_g_x_post, v_g_ffn_pre, v_w_up, v_w_down, v_g_ffn_post):
    w = dict(g_mix_pre=g_mix_pre, w_in=w_in, b_forget=b_forget, pool_w=pool_w, pool_scale=pool_scale, w_out=w_out,
             g_mix_post=g_mix_post, g_x_pre=g_x_pre, g_mem=g_mem, wq_x=wq_x, wkv_x=wkv_x, wo_x=wo_x,
             g_x_post=g_x_post, g_ffn_pre=g_ffn_pre, w_up=w_up, w_down=w_down, g_ffn_post=g_ffn_post)
    mom = dict(g_mix_pre=m_g_mix_pre, w_in=m_w_in, b_forget=m_b_forget, pool_w=m_pool_w, pool_scale=m_pool_scale,
               w_out=m_w_out, g_mix_post=m_g_mix_post, g_x_pre=m_g_x_pre, g_mem=m_g_mem, wq_x=m_wq_x,
               wkv_x=m_wkv_x, wo_x=m_wo_x, g_x_post=m_g_x_post, g_ffn_pre=m_g_ffn_pre, w_up=m_w_up,
               w_down=m_w_down, g_ffn_post=m_g_ffn_post)
    var = dict(g_mix_pre=v_g_mix_pre, w_in=v_w_in, b_forget=v_b_forget, pool_w=v_pool_w, pool_scale=v_pool_scale,
               w_out=v_w_out, g_mix_post=v_g_mix_post, g_x_pre=v_g_x_pre, g_mem=v_g_mem, wq_x=v_wq_x,
               wkv_x=v_wkv_x, wo_x=v_wo_x, g_x_post=v_g_x_post, g_ffn_pre=v_g_ffn_pre, w_up=v_w_up,
               w_down=v_w_down, g_ffn_post=v_g_ffn_post)
    S = x.shape[1]
    xs = x.reshape(S, D_MODEL)
    mems = mem.reshape(MEM_LEN, D_MODEL)
    target = loss_target.reshape(S, D_MODEL)

    def small_params(l):
        return dict(
            g_mix_pre=_vec(g_mix_pre[l]), g_mix_post=_vec(g_mix_post[l]), g_x_pre=_vec(g_x_pre[l]),
            g_mem=_vec(g_mem[l]), g_x_post=_vec(g_x_post[l]), g_ffn_pre=_vec(g_ffn_pre[l]),
            g_ffn_post=_vec(g_ffn_post[l]), pool_scale=_vec(pool_scale[l]), pool_w=pool_w[l].astype(BF16),
            b_forget=jnp.pad(_vec(b_forget[l]), ((0, 0), (0, LANES - FOX_HEADS))))

    shard = {n: [w[n][l].astype(BF16) for l in range(DEPTH)] for n in REST}
    shard["w_in"] = [_w_in_travel(w_in[l].astype(BF16)) for l in range(DEPTH)]
    sps = [small_params(l) for l in range(DEPTH)]
    saved, weights = [], []
    h = xs
    (g_in,) = _exchange("gather_w_in", [shard["w_in"][0]])
    hn = _norm_fwd("norm_fwd", xs, sps[0]["g_mix_pre"])
    for l in range(DEPTH):
        travelling = [shard[n][l] for n in REST] + ([shard["w_in"][l + 1]] if l + 1 < DEPTH else [])
        g_next = sps[l + 1]["g_mix_pre"] if l + 1 < DEPTH else None
        h, hn, sv, W, g_in = _layer_fwd(h, hn, mems, sps[l], g_in, travelling, g_next)
        saved.append(sv)
        weights.append(W)
    dh, sq = _loss_fwd_bwd(h, target)
    loss = lax.psum(0.5 * sq[0, 0] / D_MODEL, ("x", "y", "c"))

    parts = [dict() for _ in range(DEPTH)]
    small_grads = [None] * DEPTH
    carried = []
    lower = _norm_bwd("norm_bwd_b", dh, saved[-1]["y"], sps[-1]["g_ffn_post"], None, BF16)
    for l in reversed(range(DEPTH)):
        dy, dg_ffn_post = lower
        below = (saved[l - 1]["y"], sps[l - 1]["g_ffn_post"]) if l > 0 else None
        dh, lower, got, got_carried, dw_in, gs = _layer_bwd(dh, dy, mems, saved[l], sps[l], weights[l], carried, below)
        gs["g_ffn_post"] = dg_ffn_post
        parts[l].update(got)
        if got_carried:
            parts[l + 1]["w_in"] = got_carried[0]
        carried = [dw_in]
        small_grads[l] = gs
    grad_x = dh.reshape(1, S, D_MODEL)

    grads, deltas, new_m, new_v = {}, {}, {}, {}
    rows = dict(w_in=128, w_out=128, wq_x=128, wkv_x=256, wo_x=128, w_up=256, w_down=128)
    sg = {n: jnp.stack([small_grads[l][n].reshape(w[n].shape[1:]) for l in range(DEPTH)]) for n in SMALL}
    riders = dict(w_down=carried, w_up=[_pack_small(sg)])
    for n in ["w_down", "w_up", "w_out", "wq_x", "wkv_x", "wo_x", "w_in"]:
        if n == "w_in":
            for l in range(DEPTH):
                parts[l]["w_in"] = jnp.swapaxes(parts[l]["w_in"][:, :W_IN_SHARD, :], 1, 2)
        grads[n], deltas[n], new_m[n], new_v[n], *got = _adamw_big(
            "adamw_" + n, w[n], mom[n], var[n], [parts[l][n] for l in range(DEPTH)], rows[n], riders.get(n, ()))
        if n == "w_down":
            (parts[0]["w_in"],) = got
        elif n == "w_up":
            (sg_parts,) = got
    outs = _adamw_small(_pack_small(w), _pack_small(mom), _pack_small(var), sg_parts)
    for d, packed in zip((grads, deltas, new_m, new_v), outs):
        d.update(_unpack_small(packed, w))

    return (loss, grad_x, *[grads[n] for n in W_NAMES], *[deltas[n] for n in W_NAMES],
            *[new_m[n] for n in W_NAMES], *[new_v[n] for n in W_NAMES])
```

```python
import math

import jax
import jax.numpy as jnp
from jax import lax
from jax.experimental import pallas as pl
from jax.experimental.pallas import tpu as pltpu

F32 = jnp.float32
BF16 = jnp.bfloat16

D_MODEL = 1024
DEPTH = 4
FOX_WIDTH = 512
FOX_HEADS = 8
FOX_HEAD_DIM = 64
POOL_WIDTH = 512
POOL_WINDOWS = (2, 4, 8, 16)
POOL_GROUP_DIM = 128
POOL_HALO = 16
MEM_LEN = 256
X_HEADS = 4
X_HEAD_DIM = 256
D_FF = 4096
EPS = 1e-6
IN_COLS = 2056
QKV_COLS = 3 * FOX_WIDTH
UF_COLS = 640
INP_COLS = QKV_COLS + UF_COLS
N_DEV = 8
LANES = 128

ADAM_LR = 0.001
ADAM_B1 = 0.9
ADAM_B2 = 0.999
ADAM_EPS = 1e-08
ADAM_WD = 0.01
ADAM_STEP = 10

VMEM_LIMIT = 56 * 1024 * 1024

W_NAMES = ['g_mix_pre', 'w_in', 'b_forget', 'pool_w', 'pool_scale', 'w_out', 'g_mix_post', 'g_x_pre', 'g_mem',
           'wq_x', 'wkv_x', 'wo_x', 'g_x_post', 'g_ffn_pre', 'w_up', 'w_down', 'g_ffn_post']
BIG = ['w_in', 'w_out', 'wq_x', 'wkv_x', 'wo_x', 'w_up', 'w_down']
SMALL = [n for n in W_NAMES if n not in BIG]

NN = (((1,), (0,)), ((), ()))
NT = (((1,), (1,)), ((), ()))
TN = (((0,), (0,)), ((), ()))


def _params(*sem):
    return pltpu.CompilerParams(dimension_semantics=sem, vmem_limit_bytes=VMEM_LIMIT)


def _row_tile(s):
    return min(s, 512)


def _product(a_ref, w_ref, kind, c0, pw):
    cols = slice(c0, c0 + pw)
    if kind == "nn":
        return lax.dot_general(a_ref[...], w_ref[:, cols], NN, preferred_element_type=F32)
    if kind == "nt":
        return lax.dot_general(a_ref[...], w_ref[cols, :], NT, preferred_element_type=F32)
    n = w_ref.shape[2]
    if kind == "nn3":
        assert pw == n and c0 % n == 0
        return lax.dot_general(a_ref[...], w_ref[c0 // n], NN, preferred_element_type=F32)
    r = None
    for j in range(w_ref.shape[0]):
        part = lax.dot_general(a_ref[:, j * n:(j + 1) * n], w_ref[j, cols, :], NT, preferred_element_type=F32)
        r = part if r is None else r + part
    return r


def _resident(w):
    return pl.BlockSpec(w.shape, lambda i, nd=w.ndim: (0,) * nd, pipeline_mode=pl.Buffered(1))


def _mm_rows(name, terms, outs, extra=None, piece=1024):
    M = terms[0][0].shape[0]
    tm = min(M, 2 * _row_tile(M)) if sum(o[2] for o in outs) <= D_MODEL else _row_tile(M)
    nterm = len(terms)
    n_extra = 0 if extra is None else 1
    groups = {}
    for idx, (_, c0, width, fn) in enumerate(outs):
        groups.setdefault((c0, width), []).append((idx, fn))

    def body(*refs):
        a_refs = refs[0:2 * nterm:2]
        w_refs = refs[1:2 * nterm:2]
        extra_refs = refs[2 * nterm:2 * nterm + n_extra]
        out_refs = refs[2 * nterm + n_extra:]
        for (g0, gw), members in groups.items():
            for c0 in range(g0, g0 + gw, piece):
                pw = min(piece, g0 + gw - c0)
                r = None
                for a_ref, w_ref, (_, w, kind) in zip(a_refs, w_refs, terms):
                    part = _product(a_ref, w_ref, kind, c0, pw)
                    r = part if r is None else r + part
                dst = slice(c0 - g0, c0 - g0 + pw)
                for idx, fn in members:
                    if fn == "relu2":
                        rp = jnp.maximum(r, 0.0)
                        val = rp * rp
                    elif fn == "drelu2":
                        val = r * (2.0 * jnp.maximum(extra_refs[0][:, dst].astype(F32), 0.0))
                    else:
                        val = r
                    out_refs[idx][:, dst] = val.astype(out_refs[idx].dtype)

    in_specs, ins = [], []
    for a, w, _ in terms:
        in_specs.append(pl.BlockSpec((tm, a.shape[1]), lambda i: (i, 0)))
        in_specs.append(_resident(w))
        ins += [a, w]
    if extra is not None:
        in_specs.append(pl.BlockSpec((tm, extra.shape[1]), lambda i: (i, 0)))
        ins.append(extra)
    res = pl.pallas_call(
        body, name=name, grid=(M // tm,), in_specs=in_specs,
        out_specs=[pl.BlockSpec((tm, width), lambda i: (i, 0)) for _, _, width, _ in outs],
        out_shape=[jax.ShapeDtypeStruct((M, width), dt) for dt, _, width, _ in outs],
        compiler_params=_params("parallel"))(*ins)
    return res


def _mm1(name, a, w, kind, n_cols, dtype, piece=1024):
    return _mm_rows(name, [(a, w, kind)], [(dtype, 0, n_cols, "id")], piece=piece)[0]


def _mm_tn(name, a, b, out_dtype, shard_cols=None, piece=512):
    K, M = a.shape
    b_parts = b if isinstance(b, list) else [b]
    nb = len(b_parts)
    N = sum(p.shape[1] for p in b_parts)
    tk = _row_tile(K)
    nk = K // tk
    piece = shard_cols or min(piece, N)

    def body(a_ref, *rest):
        b_refs = rest[:nb]
        o_ref, acc = rest[nb:]
        k = pl.program_id(0)

        @pl.when(k == 0)
        def _():
            acc[...] = jnp.zeros_like(acc)

        a_t = jnp.transpose(a_ref[...])
        if nb == 1:
            for c0 in range(0, N, piece):
                cols = slice(c0, min(c0 + piece, N))
                acc[:, cols] += lax.dot_general(a_t, b_refs[0][:, cols], NN, preferred_element_type=F32)
        else:
            c0 = 0
            for b_ref in b_refs:
                cols = slice(c0, c0 + b_ref.shape[1])
                acc[:, cols] += lax.dot_general(a_t, b_ref[...], NN, preferred_element_type=F32)
                c0 += b_ref.shape[1]

        @pl.when(k == nk - 1)
        def _():
            for c0 in range(0, N, piece):
                cols = slice(c0, min(c0 + piece, N))
                if shard_cols:
                    o_ref[c0 // piece] = acc[:, cols].astype(o_ref.dtype)
                else:
                    o_ref[:, cols] = acc[:, cols].astype(o_ref.dtype)

    out_dims = (N // shard_cols, M, shard_cols) if shard_cols else (M, N)
    return pl.pallas_call(
        body, name=name, grid=(nk,),
        in_specs=[pl.BlockSpec((tk, M), lambda k: (k, 0))]
        + [pl.BlockSpec((tk, p.shape[1]), lambda k: (k, 0)) for p in b_parts],
        out_specs=pl.BlockSpec(out_dims, lambda k, nd=len(out_dims): (0,) * nd),
        out_shape=jax.ShapeDtypeStruct(out_dims, out_dtype),
        scratch_shapes=[pltpu.VMEM((M, N), F32)],
        compiler_params=_params("arbitrary"))(a, *b_parts)


def _mm_tn_rows(name, a_parts, b, out_dtype):
    K = b.shape[0]
    N = b.shape[1]
    na = len(a_parts)
    M = sum(p.shape[1] for p in a_parts)
    tk = _row_tile(K)
    nk = K // tk

    def body(*refs):
        a_refs = refs[:na]
        b_ref, o_ref, acc = refs[na:]
        k = pl.program_id(0)

        @pl.when(k == 0)
        def _():
            acc[...] = jnp.zeros_like(acc)

        bv = b_ref[...]
        r0 = 0
        for a_ref in a_refs:
            rows = slice(r0, r0 + a_ref.shape[1])
            acc[rows, :] += lax.dot_general(jnp.transpose(a_ref[...]), bv, NN, preferred_element_type=F32)
            r0 += a_ref.shape[1]

        @pl.when(k == nk - 1)
        def _():
            o_ref[...] = acc[...].astype(o_ref.dtype)

    return pl.pallas_call(
        body, name=name, grid=(nk,),
        in_specs=[pl.BlockSpec((tk, p.shape[1]), lambda k: (k, 0)) for p in a_parts]
        + [pl.BlockSpec((tk, N), lambda k: (k, 0))],
        out_specs=pl.BlockSpec((M, N), lambda k: (0, 0)), out_shape=jax.ShapeDtypeStruct((M, N), out_dtype),
        scratch_shapes=[pltpu.VMEM((M, N), F32)], compiler_params=_params("arbitrary"))(*a_parts, b)


def _norm_fwd(name, x, g):
    S, Dm = x.shape
    ts = _row_tile(S)

    def body(x_ref, g_ref, h_ref):
        xv = x_ref[...]
        r = lax.rsqrt(jnp.mean(xv * xv, axis=-1, keepdims=True) + EPS)
        h_ref[...] = ((xv * r) * g_ref[...]).astype(BF16)

    return pl.pallas_call(
        body, name=name, grid=(S // ts,),
        in_specs=[pl.BlockSpec((ts, Dm), lambda i: (i, 0)), pl.BlockSpec((1, Dm), lambda i: (0, 0))],
        out_specs=pl.BlockSpec((ts, Dm), lambda i: (i, 0)),
        out_shape=jax.ShapeDtypeStruct((S, Dm), BF16), compiler_params=_params("parallel"))(x, g)


def _mm_resid_norm(name, a, w, x, g, g_next):
    S, Dm = x.shape
    ts = _row_tile(S)
    has_next = g_next is not None

    def body(a_ref, w_ref, x_ref, g_ref, *rest):
        fv = _product(a_ref, w_ref, "nn", 0, Dm)
        r = lax.rsqrt(jnp.mean(fv * fv, axis=-1, keepdims=True) + EPS)
        xn = x_ref[...] + (fv * r) * g_ref[...]
        if has_next:
            gn_ref, f_ref, o_ref, h_ref = rest
            rn = lax.rsqrt(jnp.mean(xn * xn, axis=-1, keepdims=True) + EPS)
            h_ref[...] = ((xn * rn) * gn_ref[...]).astype(BF16)
        else:
            f_ref, o_ref = rest
        f_ref[...] = fv
        o_ref[...] = xn

    row = pl.BlockSpec((ts, Dm), lambda i: (i, 0))
    vec = pl.BlockSpec((1, Dm), lambda i: (0, 0))
    ins = [a, w, x, g] + ([g_next] if has_next else [])
    f32_rows = jax.ShapeDtypeStruct((S, Dm), F32)
    res = pl.pallas_call(
        body, name=name, grid=(S // ts,),
        in_specs=[pl.BlockSpec((ts, a.shape[1]), lambda i: (i, 0)), _resident(w), row, vec] + ([vec] if has_next else []),
        out_specs=[row, row] + ([row] if has_next else []),
        out_shape=[f32_rows, f32_rows] + ([jax.ShapeDtypeStruct((S, Dm), BF16)] if has_next else []),
        compiler_params=_params("parallel"))(*ins)
    return (res[0], res[1], res[2]) if has_next else (res[0], res[1], None)


def _rms_bwd(dov, yv, g):
    r = lax.rsqrt(jnp.mean(yv * yv, axis=-1, keepdims=True) + EPS)
    z = dov * g
    yr = yv * r
    return r * (z - yr * jnp.mean(yr * z, axis=-1, keepdims=True)), jnp.sum(dov * yr, axis=0, keepdims=True)


def _norm_bwd(name, dout, y, g, resid, out_dtype, below=None):
    S, Dm = y.shape
    ts = _row_tile(S)
    has_resid = resid is not None
    chained = below is not None
    produced = isinstance(dout, tuple)
    kind = dout[2] if produced else None
    a_parts = (dout[0] if isinstance(dout[0], list) else [dout[0]]) if produced else []
    n_a = len(a_parts)

    def body(*refs):
        refs = list(refs)
        if produced and n_a == 1:
            dov = _product(refs[0], refs[1], kind, 0, Dm)
            refs = refs[1:]
        elif produced:
            w_ref = refs[n_a]
            dov, k0 = None, 0
            for a_ref in refs[:n_a]:
                k1 = k0 + a_ref.shape[1]
                if kind == "nt":
                    part = lax.dot_general(a_ref[...], w_ref[:, k0:k1], NT, preferred_element_type=F32)
                else:
                    part = lax.dot_general(a_ref[...], w_ref[k0:k1, :], NN, preferred_element_type=F32)
                dov = part if dov is None else dov + part
                k0 = k1
            refs = refs[n_a:]
        else:
            dov = refs[0][...]
        y_ref, g_ref = refs[1:3]
        pos = 3
        r_ref = refs[pos] if has_resid else None
        pos += has_resid
        if chained:
            f_ref, gf_ref = refs[pos:pos + 2]
            pos += 2
        dy_ref, dg_ref = refs[pos:pos + 2]
        i = pl.program_id(0)
        dy, dg = _rms_bwd(dov, y_ref[...], g_ref[...])
        if has_resid:
            dy = dy + r_ref[...]
        dy_ref[...] = dy.astype(out_dtype)

        @pl.when(i == 0)
        def _():
            for ref in refs[pos + 1::2]:
                ref[...] = jnp.zeros_like(ref)

        dg_ref[...] += dg
        if chained:
            df_ref, dgf_ref = refs[pos + 2:pos + 4]
            df, dgf = _rms_bwd(dy, f_ref[...], gf_ref[...])
            df_ref[...] = df.astype(BF16)
            dgf_ref[...] += dgf

    row = pl.BlockSpec((ts, Dm), lambda i: (i, 0))
    vec = pl.BlockSpec((1, Dm), lambda i: (0, 0))
    if produced:
        assert n_a == 1 or kind in ("nt", "nn")
        ins = a_parts + [dout[1]]
        specs = [pl.BlockSpec((ts, a.shape[1]), lambda i: (i, 0)) for a in a_parts] + [_resident(dout[1])]
    else:
        ins = [dout]
        specs = [row]
    ins += [y, g] + ([resid] if has_resid else []) + (list(below) if chained else [])
    specs += [row, vec] + ([row] if has_resid else []) + ([row, vec] if chained else [])
    vec_shape = jax.ShapeDtypeStruct((1, Dm), F32)
    return pl.pallas_call(
        body, name=name, grid=(S // ts,), in_specs=specs, out_specs=[row, vec] + ([row, vec] if chained else []),
        out_shape=[jax.ShapeDtypeStruct((S, Dm), out_dtype), vec_shape]
        + ([jax.ShapeDtypeStruct((S, Dm), BF16), vec_shape] if chained else []),
        compiler_params=_params("arbitrary"))(*ins)


def _loss_fwd_bwd(y, t):
    S, Dm = y.shape
    ts = _row_tile(S)

    def body(y_ref, t_ref, dy_ref, acc_ref):
        i = pl.program_id(0)
        e = y_ref[...] - t_ref[...]
        dy_ref[...] = e * (1.0 / Dm)

        @pl.when(i == 0)
        def _():
            acc_ref[...] = jnp.zeros_like(acc_ref)

        s = jnp.sum(jnp.sum(e * e, axis=1, keepdims=True), axis=0, keepdims=True)
        acc_ref[...] += s

    row = pl.BlockSpec((ts, Dm), lambda i: (i, 0))
    return pl.pallas_call(
        body, name="loss", grid=(S // ts,), in_specs=[row, row],
        out_specs=[row, pl.BlockSpec((8, LANES), lambda i: (0, 0))],
        out_shape=[jax.ShapeDtypeStruct((S, Dm), F32), jax.ShapeDtypeStruct((8, LANES), F32)],
        compiler_params=_params("arbitrary"))(y, t)


def _log_sigmoid(x):
    return jnp.minimum(x, 0.0) - jnp.log(1.0 + jnp.exp(-jnp.abs(x)))


def _gate_fwd(uf, bpad):
    S = uf.shape[0]
    T = _row_tile(S)

    def body(f_ref, b_ref, c_ref, carry):
        i = pl.program_id(0)

        @pl.when(i == 0)
        def _():
            carry[...] = jnp.zeros_like(carry)

        lf = _log_sigmoid(f_ref[...] + b_ref[...])
        r = lax.broadcasted_iota(jnp.int32, (T, T), 0)
        cidx = lax.broadcasted_iota(jnp.int32, (T, T), 1)
        tri = (cidx <= r).astype(F32)
        c = lax.dot_general(tri, lf, NN, precision=lax.Precision.HIGHEST, preferred_element_type=F32)
        c_ref[...] = c + carry[0:1, :]
        carry[...] = carry[...] + jnp.sum(lf, axis=0, keepdims=True)

    return pl.pallas_call(
        body, name="gate_fwd", grid=(S // T,),
        in_specs=[pl.BlockSpec((T, LANES), lambda i: (i, 4)), pl.BlockSpec((1, LANES), lambda i: (0, 0))],
        out_specs=pl.BlockSpec((T, LANES), lambda i: (i, 0)),
        out_shape=jax.ShapeDtypeStruct((S, LANES), F32),
        scratch_shapes=[pltpu.VMEM((8, LANES), F32)], compiler_params=_params("arbitrary"))(uf, bpad)


def _gate_bwd(dc, uf, bpad):
    S = uf.shape[0]
    T = _row_tile(S)
    nb = S // T

    def body(dc_ref, f_ref, b_ref, df_ref, db_ref, carry):
        i = pl.program_id(0)

        @pl.when(i == 0)
        def _():
            carry[...] = jnp.zeros_like(carry)
            db_ref[...] = jnp.zeros_like(db_ref)

        dcv = dc_ref[...]
        r = lax.broadcasted_iota(jnp.int32, (T, T), 0)
        cidx = lax.broadcasted_iota(jnp.int32, (T, T), 1)
        tri = (cidx >= r).astype(F32)
        dlf = lax.dot_general(tri, dcv, NN, precision=lax.Precision.HIGHEST, preferred_element_type=F32)
        dlf = dlf + carry[0:1, :]
        carry[...] = carry[...] + jnp.sum(dcv, axis=0, keepdims=True)
        fg = f_ref[...] + b_ref[...]
        dfg = dlf / (1.0 + jnp.exp(fg))
        df_ref[...] = dfg.astype(BF16)
        db_ref[...] += jnp.sum(dfg, axis=0, keepdims=True)

    return pl.pallas_call(
        body, name="gate_bwd", grid=(nb,),
        in_specs=[pl.BlockSpec((T, LANES), lambda i: (nb - 1 - i, 0)),
                  pl.BlockSpec((T, LANES), lambda i: (nb - 1 - i, 4)),
                  pl.BlockSpec((1, LANES), lambda i: (0, 0))],
        out_specs=[pl.BlockSpec((T, LANES), lambda i: (nb - 1 - i, 0)), pl.BlockSpec((1, LANES), lambda i: (0, 0))],
        out_shape=[jax.ShapeDtypeStruct((S, LANES), BF16), jax.ShapeDtypeStruct((1, LANES), F32)],
        scratch_shapes=[pltpu.VMEM((8, LANES), F32)], compiler_params=_params("arbitrary"))(dc, uf, bpad)


FOX_CHUNK = 32
FOX_CHUNK_BWD = 64
HEAD_PAIRS = FOX_HEADS // 2
PAIR = 2


def _masked(s, row0, col0, diagonal):
    if diagonal:
        row = row0 + lax.broadcasted_iota(jnp.int32, s.shape, 0)
        col = col0 + lax.broadcasted_iota(jnp.int32, s.shape, 1)
        s = jnp.where(col <= row, s, -jnp.inf)
    return s


def _causal_pairs(n, query_major):
    if query_major:
        pairs = [(q, k) for q in range(n) for k in range(q + 1)]
    else:
        pairs = [(q, k) for k in range(n) for q in range(k, n)]
    return (jnp.asarray([p[0] for p in pairs], jnp.int32), jnp.asarray([p[1] for p in pairs], jnp.int32))


def _lane_block(b):
    return slice(b * LANES, (b + 1) * LANES)


def _fold(op, xs):
    acc = xs[0]
    for x in xs[1:]:
        acc = op(acc, x)
    return acc


def _head_lanes(hh):
    lane = lax.broadcasted_iota(jnp.int32, (1, LANES), 1)
    return (lane < FOX_HEAD_DIM) if hh == 0 else (lane >= FOX_HEAD_DIM)


def _pick(first_head, a, b):
    return jnp.where(first_head, a, b)


def _fox_fwd(qkv, cT, comm):
    S = qkv.shape[0]
    t = _row_tile(S)
    n = S // t
    nc = len(comm)
    scale = 1.0 / math.sqrt(FOX_HEAD_DIM)
    chunk = min(FOX_CHUNK, t)
    per_head = 4
    q_tab, k_tab = _causal_pairs(n, True)
    steps = q_tab.shape[0]

    def body(qt_ref, kt_ref, q_ref, k_ref, v_ref, c_ref, *rest):
        comm_in = rest[:nc]
        o_ref, ob_ref, lse_ref = rest[nc:nc + 3]
        comm_out = rest[nc + 3:2 * nc + 3]
        scr = rest[2 * nc + 3:2 * nc + 3 + PAIR * per_head]
        sems = rest[2 * nc + 3 + PAIR * per_head:]
        hp = pl.program_id(0)
        step_id = pl.program_id(1)
        qi = qt_ref[step_id]
        ki = kt_ref[step_id]

        if nc:
            @pl.when((hp == 0) & (step_id == 0))
            def _():
                _Gather(comm_in, comm_out, *sems).start()

            @pl.when((hp == HEAD_PAIRS - 1) & (step_id == 0))
            def _():
                _Gather(comm_in, comm_out, *sems).pass_on()

        @pl.when(ki == 0)
        def _():
            for hh in range(PAIR):
                m_s, l_s, a_s, acc_s = scr[hh * per_head:hh * per_head + 4]
                m_s[...] = jnp.full_like(m_s, -jnp.inf)
                l_s[...] = jnp.zeros_like(l_s)
                acc_s[...] = jnp.zeros_like(acc_s)

        def step(diagonal):
            q2 = q_ref[...] * scale
            k2 = k_ref[...]
            v2 = v_ref[...]
            scores = []
            for hh in range(PAIR):
                qm = jnp.where(_head_lanes(hh), q2, jnp.zeros_like(q2))
                scores.append(lax.dot_general(qm, k2, NT, preferred_element_type=F32))
            for hh in range(PAIR):
                m_s, l_s, a_s, acc_s = scr[hh * per_head:(hh + 1) * per_head]
                s_s = scores[hh]
                hi_rows, lo_rows = [], []
                for r in range(t // chunk):
                    rows = slice(r * chunk, (r + 1) * chunk)
                    blocks = [_masked(s_s[rows, _lane_block(b)] - c_ref[hh, :, _lane_block(b)], r * chunk,
                                      b * LANES, diagonal) for b in range(t // LANES)]
                    m_prev = m_s[rows, :]
                    m_new = jnp.maximum(m_prev, jnp.max(_fold(jnp.maximum, blocks), axis=1, keepdims=True))
                    alpha = jnp.exp(m_prev - m_new)
                    ps = [jnp.exp(blk - m_new) for blk in blocks]
                    l_s[rows, :] = alpha * l_s[rows, :] + jnp.sum(_fold(jnp.add, ps), axis=1, keepdims=True)
                    m_s[rows, :] = m_new
                    a_s[rows, :] = alpha
                    his = [p.astype(BF16) for p in ps]
                    hi_rows.append(jnp.concatenate(his, axis=1))
                    lo_rows.append(jnp.concatenate([(p - h.astype(F32)).astype(BF16) for p, h in zip(ps, his)],
                                                   axis=1))
                pv = (lax.dot_general(jnp.concatenate(hi_rows, axis=0), v2, NN, preferred_element_type=F32)
                      + lax.dot_general(jnp.concatenate(lo_rows, axis=0), v2, NN, preferred_element_type=F32))
                acc_s[...] = a_s[...] * acc_s[...] + pv

        @pl.when(ki < qi)
        def _():
            step(False)

        @pl.when(ki == qi)
        def _():
            step(True)
            heads = []
            for hh in range(PAIR):
                m_s, l_s, a_s, acc_s = scr[hh * per_head:hh * per_head + 4]
                heads.append(acc_s[...] / l_s[...])
                lse_ref[hh] = m_s[...] + jnp.log(l_s[...])
            o2 = _pick(_head_lanes(0), heads[0], heads[1])
            o_ref[...] = o2
            ob_ref[...] = o2.astype(BF16)

        if nc:
            @pl.when((hp == HEAD_PAIRS - 1) & (step_id == steps - 1))
            def _():
                _Gather(comm_in, comm_out, *sems).finish()

    def q_cols(first_block):
        return pl.BlockSpec((t, LANES), lambda h, s, qt, kt: (qt[s], first_block + h))

    def k_cols(first_block):
        return pl.BlockSpec((t, LANES), lambda h, s, qt, kt: (kt[s], first_block + h))

    any_spec = pl.BlockSpec(memory_space=pl.ANY)
    head_scratch = [pltpu.VMEM((t, LANES), F32)] * per_head
    grid_spec = pltpu.PrefetchScalarGridSpec(
        num_scalar_prefetch=2, grid=(HEAD_PAIRS, steps),
        in_specs=[q_cols(0), k_cols(HEAD_PAIRS), k_cols(2 * HEAD_PAIRS),
                  pl.BlockSpec((PAIR, 1, t), lambda h, s, qt, kt: (h, 0, kt[s]))] + [any_spec] * nc,
        out_specs=[q_cols(0), q_cols(0),
                   pl.BlockSpec((PAIR, t, LANES), lambda h, s, qt, kt: (h, qt[s], 0))] + [any_spec] * nc,
        scratch_shapes=head_scratch * PAIR + _comm_scratch(nc))
    return pl.pallas_call(
        body, name="fox_fwd", grid_spec=grid_spec,
        out_shape=[jax.ShapeDtypeStruct((S, FOX_WIDTH), F32), jax.ShapeDtypeStruct((S, FOX_WIDTH), BF16),
                   jax.ShapeDtypeStruct((FOX_HEADS, S, LANES), F32)] + _comm_shapes(comm),
        compiler_params=_params("arbitrary", "arbitrary"))(q_tab, k_tab, qkv, qkv, qkv, cT, *comm)


def _fox_bwd(qkv, cT, o, lse, do, comm):
    S = qkv.shape[0]
    t = _row_tile(S)
    n = S // t
    nc = len(comm)
    scale = 1.0 / math.sqrt(FOX_HEAD_DIM)
    chunk = min(FOX_CHUNK_BWD, t)
    per_head = 2
    q_tab, k_tab = _causal_pairs(n, False)
    steps = q_tab.shape[0]

    def body(qt_ref, kt_ref, q_ref, k_ref, v_ref, c_ref, o_ref, do_ref, lse_ref, *rest):
        comm_in = rest[:nc]
        dq_ref, dk_ref, dv_ref, dc_ref = rest[nc:nc + 4]
        comm_out = rest[nc + 4:2 * nc + 4]
        dq_s, dk_s, dv_s = rest[2 * nc + 4:2 * nc + 7]
        scr = rest[2 * nc + 7:2 * nc + 7 + PAIR * per_head]
        sems = rest[2 * nc + 7 + PAIR * per_head:]
        hp = pl.program_id(0)
        step_id = pl.program_id(1)
        qi = qt_ref[step_id]
        ki = kt_ref[step_id]

        if nc:
            @pl.when((hp == 0) & (step_id == 0))
            def _():
                for cp in _comm_copies(comm_in, comm_out, *sems):
                    cp.start()

        @pl.when(step_id == 0)
        def _():
            dq_s[...] = jnp.zeros_like(dq_s)

        @pl.when(qi == ki)
        def _():
            dk_s[...] = jnp.zeros_like(dk_s)
            dv_s[...] = jnp.zeros_like(dv_s)
            for hh in range(PAIR):
                dc_s = scr[hh * per_head]
                dc_s[...] = jnp.zeros_like(dc_s)

        def step(diagonal):
            q2 = q_ref[...]
            k2 = k_ref[...]
            v2 = v_ref[...]
            do2 = do_ref[...]
            prod = do2.astype(F32) * o_ref[...]
            grads = []
            for hh in range(PAIR):
                dc_s, delta_s = scr[hh * per_head:(hh + 1) * per_head]
                mine = _head_lanes(hh)
                s_s = lax.dot_general(jnp.where(mine, q2 * scale, jnp.zeros_like(q2)), k2, NT,
                                      preferred_element_type=F32)
                dp_s = lax.dot_general(jnp.where(mine, do2, jnp.zeros_like(do2)), v2, NT, preferred_element_type=F32)
                delta_s[...] = jnp.broadcast_to(jnp.sum(jnp.where(mine, prod, 0.0), axis=1, keepdims=True),
                                                (t, LANES))
                dc8 = [jnp.zeros((8, LANES), F32) for _ in range(t // LANES)]
                p_rows, ds_rows = [], []
                for r in range(t // chunk):
                    rows = slice(r * chunk, (r + 1) * chunk)
                    lse = lse_ref[hh, rows, :]
                    delta = delta_s[rows, :]
                    p_blocks, ds_blocks = [], []
                    for b in range(t // LANES):
                        s = _masked(s_s[rows, _lane_block(b)] - c_ref[hh, :, _lane_block(b)], r * chunk, b * LANES,
                                    diagonal)
                        p = jnp.exp(s - lse)
                        ds = p * (dp_s[rows, _lane_block(b)] - delta)
                        p_blocks.append(p.astype(BF16))
                        ds_blocks.append(ds.astype(BF16))
                        dc8[b] = dc8[b] + jnp.sum(ds.reshape(chunk // 8, 8, LANES), axis=0)
                    p_rows.append(jnp.concatenate(p_blocks, axis=1))
                    ds_rows.append(jnp.concatenate(ds_blocks, axis=1))
                for b in range(t // LANES):
                    dc_s[:, _lane_block(b)] += jnp.sum(dc8[b], axis=0, keepdims=True)
                dsb = jnp.concatenate(ds_rows, axis=0)
                grads.append((lax.dot_general(jnp.concatenate(p_rows, axis=0), do2, TN, preferred_element_type=F32),
                              lax.dot_general(dsb, k2, NN, preferred_element_type=F32),
                              lax.dot_general(dsb, q2, TN, preferred_element_type=F32)))
            first = _head_lanes(0)
            dv_s[...] += _pick(first, grads[0][0], grads[1][0])
            q_rows = pl.ds(pl.multiple_of(qi * t, t), t)
            dq_s[q_rows, :] += _pick(first, grads[0][1], grads[1][1]) * scale
            dk_s[...] += _pick(first, grads[0][2], grads[1][2]) * scale

        @pl.when(qi > ki)
        def _():
            step(False)

        @pl.when(qi == ki)
        def _():
            step(True)

        @pl.when(qi == n - 1)
        def _():
            dk_ref[...] = dk_s[...].astype(BF16)
            dv_ref[...] = dv_s[...].astype(BF16)
            for hh in range(PAIR):
                dc_ref[hh] = -scr[hh * per_head][...]

        @pl.when(step_id == steps - 1)
        def _():
            dq_ref[...] = dq_s[...].astype(BF16)

        if nc:
            @pl.when((hp == HEAD_PAIRS - 1) & (step_id == steps - 1))
            def _():
                for cp in _comm_copies(comm_in, comm_out, *sems):
                    cp.wait()

    def q_side(first_block):
        return pl.BlockSpec((t, LANES), lambda h, s, qt, kt: (qt[s], first_block + h))

    def k_side(first_block):
        return pl.BlockSpec((t, LANES), lambda h, s, qt, kt: (kt[s], first_block + h))

    any_spec = pl.BlockSpec(memory_space=pl.ANY)
    head_scratch = [pltpu.VMEM((1, t), F32), pltpu.VMEM((t, LANES), F32)]
    grad_shape = jax.ShapeDtypeStruct((S, FOX_WIDTH), BF16)
    grid_spec = pltpu.PrefetchScalarGridSpec(
        num_scalar_prefetch=2, grid=(HEAD_PAIRS, steps),
        in_specs=[q_side(0), k_side(HEAD_PAIRS), k_side(2 * HEAD_PAIRS),
                  pl.BlockSpec((PAIR, 1, t), lambda h, s, qt, kt: (h, 0, kt[s])), q_side(0), q_side(0),
                  pl.BlockSpec((PAIR, t, LANES), lambda h, s, qt, kt: (h, qt[s], 0))] + [any_spec] * nc,
        out_specs=[pl.BlockSpec((S, LANES), lambda h, s, qt, kt: (0, h)), k_side(0), k_side(0),
                   pl.BlockSpec((PAIR, 1, t), lambda h, s, qt, kt: (h, 0, kt[s]))] + [any_spec] * nc,
        scratch_shapes=[pltpu.VMEM((S, LANES), F32), pltpu.VMEM((t, LANES), F32), pltpu.VMEM((t, LANES), F32)]
        + head_scratch * PAIR + _comm_scratch(nc))
    return pl.pallas_call(
        body, name="fox_bwd", grid_spec=grid_spec,
        out_shape=[grad_shape, grad_shape, grad_shape, jax.ShapeDtypeStruct((FOX_HEADS, 1, S), F32)]
        + _comm_shapes(comm),
        compiler_params=_params("arbitrary", "arbitrary"))(q_tab, k_tab, qkv, qkv, qkv, cT, o, do, lse, *comm)


def _lanes(g):
    return slice(g * POOL_GROUP_DIM, (g + 1) * POOL_GROUP_DIM)


def _window_sum(e, win, back):
    rows = e.shape[0]
    s = e
    sh = 1
    while sh < win:
        s = s + pltpu.roll(s, sh if back else rows - sh, 0)
        sh *= 2
    return s


def _pooled(u_ref, up_ref, i, g, win, T):
    cur = u_ref[:, _lanes(g)]
    tail = jnp.where(i > 0, up_ref[T - POOL_HALO:T, _lanes(g)], 0.0)
    e = jnp.concatenate([tail, cur], axis=0)
    s = _window_sum(e, win, True)
    t_idx = i * T - POOL_HALO + lax.broadcasted_iota(jnp.int32, (T + POOL_HALO, POOL_GROUP_DIM), 0)
    cnt = jnp.clip(t_idx + 1, 1, win).astype(F32)
    return (s / cnt - e)[POOL_HALO:, :]


def _pool_fwd(uf, pw, ps):
    S = uf.shape[0]
    T = _row_tile(S)

    def body(u_ref, up_ref, w_ref, sc_ref, o_ref):
        i = pl.program_id(0)
        for g, win in enumerate(POOL_WINDOWS):
            pb = _pooled(u_ref, up_ref, i, g, win, T).astype(BF16)
            yv = lax.dot_general(pb, w_ref[g], NN, preferred_element_type=F32)
            o_ref[:, _lanes(g)] = (yv * sc_ref[:, _lanes(g)]).astype(BF16)

    return pl.pallas_call(
        body, name="pool_fwd", grid=(S // T,),
        in_specs=[pl.BlockSpec((T, POOL_WIDTH), lambda i: (i, 0)),
                  pl.BlockSpec((T, POOL_WIDTH), lambda i: (jnp.maximum(i - 1, 0), 0)),
                  pl.BlockSpec((4, POOL_GROUP_DIM, POOL_GROUP_DIM), lambda i: (0, 0, 0)),
                  pl.BlockSpec((1, POOL_WIDTH), lambda i: (0, 0))],
        out_specs=pl.BlockSpec((T, POOL_WIDTH), lambda i: (i, 0)),
        out_shape=jax.ShapeDtypeStruct((S, POOL_WIDTH), BF16), compiler_params=_params("parallel"))(uf, uf, pw, ps)


def _pool_bwd(uf, dpool, pw, ps):
    S = uf.shape[0]
    T = _row_tile(S)
    nb = S // T

    def body(u_ref, up_ref, d_ref, dn_ref, w_ref, sc_ref, du_ref, dw_ref, dsc_ref):
        i = pl.program_id(0)

        @pl.when(i == 0)
        def _():
            dw_ref[...] = jnp.zeros_like(dw_ref)
            dsc_ref[...] = jnp.zeros_like(dsc_ref)

        t_idx = i * T + lax.broadcasted_iota(jnp.int32, (T + POOL_HALO, POOL_GROUP_DIM), 0)
        for g, win in enumerate(POOL_WINDOWS):
            pb = _pooled(u_ref, up_ref, i, g, win, T).astype(BF16)
            w = w_ref[g]
            sc = sc_ref[:, _lanes(g)]
            yv = lax.dot_general(pb, w, NN, preferred_element_type=F32)
            dov = d_ref[:, _lanes(g)]
            dsc_ref[:, _lanes(g)] += jnp.sum(dov * yv, axis=0, keepdims=True)
            head = jnp.where(i < nb - 1, dn_ref[0:POOL_HALO, _lanes(g)], 0.0)
            dyb = (jnp.concatenate([dov, head], axis=0) * sc).astype(BF16)
            dw_ref[g] += lax.dot_general(pb, dyb[:T], TN, preferred_element_type=F32)
            dpl = lax.dot_general(dyb, w, NT, preferred_element_type=F32)
            cnt = jnp.minimum(t_idx + 1, win).astype(F32)
            a = _window_sum(dpl / cnt, win, False)
            du_ref[:, _lanes(g)] = (a - dpl)[:T].astype(BF16)

    return pl.pallas_call(
        body, name="pool_bwd", grid=(nb,),
        in_specs=[pl.BlockSpec((T, POOL_WIDTH), lambda i: (i, 0)),
                  pl.BlockSpec((T, POOL_WIDTH), lambda i: (jnp.maximum(i - 1, 0), 0)),
                  pl.BlockSpec((T, POOL_WIDTH), lambda i: (i, 0)),
                  pl.BlockSpec((T, POOL_WIDTH), lambda i: (jnp.minimum(i + 1, nb - 1), 0)),
                  pl.BlockSpec((4, POOL_GROUP_DIM, POOL_GROUP_DIM), lambda i: (0, 0, 0)),
                  pl.BlockSpec((1, POOL_WIDTH), lambda i: (0, 0))],
        out_specs=[pl.BlockSpec((T, POOL_WIDTH), lambda i: (i, 0)),
                   pl.BlockSpec((4, POOL_GROUP_DIM, POOL_GROUP_DIM), lambda i: (0, 0, 0)),
                   pl.BlockSpec((1, POOL_WIDTH), lambda i: (0, 0))],
        out_shape=[jax.ShapeDtypeStruct((S, POOL_WIDTH), BF16),
                   jax.ShapeDtypeStruct((4, POOL_GROUP_DIM, POOL_GROUP_DIM), F32),
                   jax.ShapeDtypeStruct((1, POOL_WIDTH), F32)],
        compiler_params=_params("arbitrary"))(uf, uf, dpool, dpool, pw, ps)


def _xhead(h):
    return slice(h * X_HEAD_DIM, (h + 1) * X_HEAD_DIM)


def _xvhead(h):
    return slice(D_MODEL + h * X_HEAD_DIM, D_MODEL + (h + 1) * X_HEAD_DIM)


X_CHUNK = 32


def _x_probs(s_ref, rows):
    blocks = [s_ref[rows, _lane_block(b)] * (1.0 / math.sqrt(X_HEAD_DIM)) for b in range(MEM_LEN // LANES)]
    m = jnp.max(_fold(jnp.maximum, blocks), axis=1, keepdims=True)
    es = [jnp.exp(blk - m) for blk in blocks]
    den = jnp.sum(_fold(jnp.add, es), axis=1, keepdims=True)
    return [e / den for e in es]


def _xattn_fwd(q, kv):
    S = q.shape[0]
    t = _row_tile(S)
    chunk = min(X_CHUNK, t)

    def body(q_ref, kv_ref, o_ref):
        for h in range(X_HEADS):
            s = lax.dot_general(q_ref[:, _xhead(h)], kv_ref[:, _xhead(h)], NT, preferred_element_type=F32)
            p_rows = []
            for r in range(t // chunk):
                rows = slice(r * chunk, (r + 1) * chunk)
                p_rows.append(jnp.concatenate([p.astype(BF16) for p in _x_probs(s, rows)], axis=1))
            o_ref[:, _xhead(h)] = lax.dot_general(jnp.concatenate(p_rows, axis=0), kv_ref[:, _xvhead(h)], NN,
                                                  preferred_element_type=F32).astype(BF16)

    return pl.pallas_call(
        body, name="xattn_fwd", grid=(S // t,),
        in_specs=[pl.BlockSpec((t, D_MODEL), lambda i: (i, 0)), pl.BlockSpec((MEM_LEN, 2 * D_MODEL), lambda i: (0, 0))],
        out_specs=pl.BlockSpec((t, D_MODEL), lambda i: (i, 0)),
        out_shape=jax.ShapeDtypeStruct((S, D_MODEL), BF16), compiler_params=_params("parallel"))(q, kv)


def _xattn_bwd(q, kv, do):
    S = q.shape[0]
    t = _row_tile(S)
    nb = S // t
    scale = 1.0 / math.sqrt(X_HEAD_DIM)
    chunk = min(X_CHUNK, t)

    def body(q_ref, kv_ref, do_ref, dq_ref, dkv_ref, acc):
        i = pl.program_id(0)

        @pl.when(i == 0)
        def _():
            acc[...] = jnp.zeros_like(acc)

        for h in range(X_HEADS):
            qh = q_ref[:, _xhead(h)]
            kh = kv_ref[:, _xhead(h)]
            doh = do_ref[:, _xhead(h)]
            s_s = lax.dot_general(qh, kh, NT, preferred_element_type=F32)
            dp_s = lax.dot_general(doh, kv_ref[:, _xvhead(h)], NT, preferred_element_type=F32)
            p_rows, ds_rows = [], []
            for r in range(t // chunk):
                rows = slice(r * chunk, (r + 1) * chunk)
                ps = _x_probs(s_s, rows)
                dps = [dp_s[rows, _lane_block(b)] for b in range(len(ps))]
                inner = jnp.sum(_fold(jnp.add, [dp * p for dp, p in zip(dps, ps)]), axis=1, keepdims=True)
                p_rows.append(jnp.concatenate([p.astype(BF16) for p in ps], axis=1))
                ds_rows.append(jnp.concatenate([(p * (dp - inner)).astype(BF16) for dp, p in zip(dps, ps)], axis=1))
            dsb = jnp.concatenate(ds_rows, axis=0)
            acc[:, _xvhead(h)] += lax.dot_general(jnp.concatenate(p_rows, axis=0), doh, TN,
                                                  preferred_element_type=F32)
            dq_ref[:, _xhead(h)] = (lax.dot_general(dsb, kh, NN, preferred_element_type=F32) * scale).astype(BF16)
            acc[:, _xhead(h)] += lax.dot_general(dsb, qh, TN, preferred_element_type=F32) * scale

        @pl.when(i == nb - 1)
        def _():
            dkv_ref[...] = acc[...].astype(BF16)

    row = pl.BlockSpec((t, D_MODEL), lambda i: (i, 0))
    full = pl.BlockSpec((MEM_LEN, 2 * D_MODEL), lambda i: (0, 0))
    return pl.pallas_call(
        body, name="xattn_bwd", grid=(nb,), in_specs=[row, full, row], out_specs=[row, full],
        out_shape=[jax.ShapeDtypeStruct((S, D_MODEL), BF16), jax.ShapeDtypeStruct((MEM_LEN, 2 * D_MODEL), BF16)],
        scratch_shapes=[pltpu.VMEM((MEM_LEN, 2 * D_MODEL), F32)],
        compiler_params=_params("arbitrary"))(q, kv, do)


def _comm_shapes(arrs):
    return [jax.ShapeDtypeStruct((N_DEV,) + tuple(a.shape[-2:]), a.dtype) for a in arrs]


def _comm_scratch(n):
    if n == 0:
        return []
    return [pltpu.SemaphoreType.DMA((n, N_DEV - 1)), pltpu.SemaphoreType.DMA((n, N_DEV - 1)),
            pltpu.SemaphoreType.DMA((n,))]


def _comm_copies(ins, outs, send_sems, recv_sems, local_sems):
    x, y, c = lax.axis_index("x"), lax.axis_index("y"), lax.axis_index("c")
    me = 4 * x + 2 * y + c
    copies = []
    for w in range(len(ins)):
        src = ins[w] if len(ins[w].shape) == 2 else ins[w].at[me]
        copies.append(pltpu.make_async_copy(src, outs[w].at[me], local_sems.at[w]))
    for k in range(1, N_DEV):
        px = 1 - x if k & 4 else x
        py = 1 - y if k & 2 else y
        pc = 1 - c if k & 1 else c
        peer = 4 * px + 2 * py + pc
        for w in range(len(ins)):
            src = ins[w] if len(ins[w].shape) == 2 else ins[w].at[peer]
            copies.append(pltpu.make_async_remote_copy(
                src_ref=src, dst_ref=outs[w].at[me], send_sem=send_sems.at[w, k - 1],
                recv_sem=recv_sems.at[w, k - 1], device_id=(px, py, pc), device_id_type=pl.DeviceIdType.MESH))
    return copies


class _Gather:
    def __init__(self, ins, outs, send_sems, recv_sems, local_sems):
        x, y, c = lax.axis_index("x"), lax.axis_index("y"), lax.axis_index("c")
        me = 4 * x + 2 * y + c
        sibling = (x, y, 1 - c)
        self.local, self.mine, self.passed = [], [], []
        for w in range(len(ins)):
            def remote(idx, src, slot, dev, w=w):
                return pltpu.make_async_remote_copy(
                    src_ref=src, dst_ref=outs[w].at[slot], send_sem=send_sems.at[w, idx],
                    recv_sem=recv_sems.at[w, idx], device_id=dev, device_id_type=pl.DeviceIdType.MESH)

            self.local.append(pltpu.make_async_copy(ins[w], outs[w].at[me], local_sems.at[w]))
            mine, passed = [remote(0, ins[w], me, sibling)], []
            for j, (fx, fy) in enumerate(((0, 1), (1, 0), (1, 1))):
                px = 1 - x if fx else x
                py = 1 - y if fy else y
                slot = 4 * px + 2 * py + c
                mine.append(remote(1 + j, ins[w], me, (px, py, c)))
                passed.append(remote(4 + j, outs[w].at[slot], slot, sibling))
            self.mine.append(mine)
            self.passed.append(passed)

    def start(self):
        for cp in self.local:
            cp.start()
        for mine in self.mine:
            for cp in mine:
                cp.start()

    def pass_on(self):
        for mine, passed in zip(self.mine, self.passed):
            for j, cp in enumerate(passed):
                mine[1 + j].wait_recv()
                cp.start()

    def finish(self):
        for mine, passed in zip(self.mine, self.passed):
            mine[0].wait_recv()
            for cp in passed:
                cp.wait_recv()
            for cp in mine + passed:
                cp.wait_send()
        for cp in self.local:
            cp.wait()


def _exchange(name, arrs):
    n = len(arrs)
    gather = all(a.ndim == 2 for a in arrs)

    def body(*refs):
        if gather:
            g = _Gather(refs[:n], refs[n:2 * n], *refs[2 * n:])
            g.start()
            g.pass_on()
            g.finish()
            return
        copies = _comm_copies(refs[:n], refs[n:2 * n], *refs[2 * n:])
        for cp in copies:
            cp.start()
        for cp in copies:
            cp.wait()

    any_spec = pl.BlockSpec(memory_space=pl.ANY)
    return pl.pallas_call(
        body, name=name, in_specs=[any_spec] * n, out_specs=[any_spec] * n, out_shape=_comm_shapes(arrs),
        scratch_shapes=_comm_scratch(n))(*arrs)


def _adamw_math(w, g, m, v):
    m = ADAM_B1 * m + (1.0 - ADAM_B1) * g
    v = ADAM_B2 * v + (1.0 - ADAM_B2) * (g * g)
    m_hat = m / (1.0 - ADAM_B1 ** ADAM_STEP)
    v_hat = v / (1.0 - ADAM_B2 ** ADAM_STEP)
    delta = -ADAM_LR * (m_hat / (jnp.sqrt(v_hat) + ADAM_EPS) + ADAM_WD * w)
    return delta, m, v


def _sum_parts(p_ref):
    g = p_ref[0].astype(F32)
    for s in range(1, N_DEV):
        g = g + p_ref[s].astype(F32)
    return g


def _adamw_big(name, w, m, v, parts, tr, comm=()):
    L, R, C = w.shape
    nc = len(comm)
    gather = all(a.ndim == 2 for a in comm)
    nr = R // tr

    def exchange(comm_in, comm_out, sems, begin):
        if gather:
            g = _Gather(comm_in, comm_out, *sems)
            if begin:
                g.start()
            else:
                g.pass_on()
                g.finish()
        else:
            for cp in _comm_copies(comm_in, comm_out, *sems):
                cp.start() if begin else cp.wait()

    def body(w_ref, m_ref, v_ref, *rest):
        p_refs = rest[:L]
        comm_in = rest[L:L + nc]
        g_ref, d_ref, nm_ref, nv_ref = rest[L + nc:L + nc + 4]
        comm_out = rest[L + nc + 4:L + 2 * nc + 4]
        sems = rest[L + 2 * nc + 4:]
        layer = pl.program_id(0)
        if nc:
            @pl.when((layer == 0) & (pl.program_id(1) == 0))
            def _():
                exchange(comm_in, comm_out, sems, True)

        for j in range(L):
            @pl.when(layer == j)
            def _(j=j):
                g = _sum_parts(p_refs[j])
                delta, nm, nv = _adamw_math(w_ref[...], g, m_ref[...], v_ref[...])
                g_ref[...] = g
                d_ref[...] = delta
                nm_ref[...] = nm
                nv_ref[...] = nv

        if nc:
            @pl.when((layer == L - 1) & (pl.program_id(1) == nr - 1))
            def _():
                exchange(comm_in, comm_out, sems, False)

    blk = pl.BlockSpec((None, tr, C), lambda l, i: (l, i, 0))

    def part_spec(j):
        return pl.BlockSpec((N_DEV, tr, C), lambda l, i: (0, jnp.where(l == j, i, 0), 0))

    shp = jax.ShapeDtypeStruct((L, R, C), F32)
    any_spec = pl.BlockSpec(memory_space=pl.ANY)
    return pl.pallas_call(
        body, name=name, grid=(L, nr),
        in_specs=[blk, blk, blk] + [part_spec(j) for j in range(L)] + [any_spec] * nc,
        out_specs=[blk] * 4 + [any_spec] * nc, out_shape=[shp] * 4 + _comm_shapes(comm),
        scratch_shapes=_comm_scratch(nc),
        compiler_params=_params("arbitrary", "arbitrary"))(w, m, v, *parts, *comm)


def _adamw_small(w, m, v, parts):
    R, C = w.shape

    def body(w_ref, m_ref, v_ref, p_ref, g_ref, d_ref, nm_ref, nv_ref):
        g = _sum_parts(p_ref)
        delta, nm, nv = _adamw_math(w_ref[...], g, m_ref[...], v_ref[...])
        g_ref[...] = g
        d_ref[...] = delta
        nm_ref[...] = nm
        nv_ref[...] = nv

    shp = jax.ShapeDtypeStruct((R, C), F32)
    return pl.pallas_call(body, name="adamw_small", out_shape=[shp] * 4,
                          compiler_params=pltpu.CompilerParams(vmem_limit_bytes=VMEM_LIMIT))(w, m, v, parts)


def _vec(a):
    return a.reshape(1, -1)


W_IN_SHARD = IN_COLS // N_DEV
W_IN_ROWS = 272


def _w_in_travel(a):
    pad = [(0, 0)] * (a.ndim - 2) + [(0, W_IN_ROWS - W_IN_SHARD), (0, 0)]
    return jnp.pad(jnp.swapaxes(a, -1, -2), pad)


def _unpack_w_in(g):
    nat = g[:, :W_IN_SHARD, :].reshape(IN_COLS, D_MODEL)
    f = jnp.pad(nat[QKV_COLS:QKV_COLS + FOX_HEADS], ((0, UF_COLS - POOL_WIDTH - FOX_HEADS), (0, 0)))
    return jnp.concatenate([nat[:QKV_COLS], nat[QKV_COLS + FOX_HEADS:], f], axis=0)


def _pack_dw_in(dwp_t):
    nat = jnp.concatenate([dwp_t[:QKV_COLS], dwp_t[QKV_COLS + POOL_WIDTH:QKV_COLS + POOL_WIDTH + FOX_HEADS],
                           dwp_t[QKV_COLS:QKV_COLS + POOL_WIDTH]], axis=0)
    return jnp.pad(nat.reshape(N_DEV, W_IN_SHARD, D_MODEL), ((0, 0), (0, W_IN_ROWS - W_IN_SHARD), (0, 0)))


REST = ['w_out', 'wq_x', 'wkv_x', 'wo_x', 'w_up', 'w_down']


def _layer_fwd(x0, h1, mem, sp, g_in, shards, g_next):
    S = x0.shape[0]
    sv = {"x0": x0}
    w_inp = _unpack_w_in(g_in)
    qkv, uf = _mm_rows("mm_in", [(h1, w_inp, "nt")],
                       [(BF16, 0, QKV_COLS, "id"), (F32, QKV_COLS, UF_COLS, "id")], piece=UF_COLS)
    c = _gate_fwd(uf, sp["b_forget"])
    cT = jnp.transpose(c[:, :FOX_HEADS]).reshape(FOX_HEADS, 1, S)
    o, ob, lse, *got = _fox_fwd(qkv, cT, shards)
    g_out, g_q, g_kv, g_o, g_up, g_down = got[:6]
    W = dict(inp=w_inp, out=g_out.reshape(D_MODEL, D_MODEL), q=g_q.reshape(D_MODEL, D_MODEL), kv=g_kv,
             o=g_o.reshape(D_MODEL, D_MODEL), up=g_up, down=g_down.reshape(D_FF, D_MODEL))
    pool = _pool_fwd(uf, sp["pool_w"], sp["pool_scale"])
    cat = jnp.concatenate([ob, pool], axis=1)
    mix, x1, h2 = _mm_resid_norm("mm_sq_norm", cat, W["out"], x0, sp["g_mix_post"], sp["g_x_pre"])
    mn = _norm_fwd("norm_mem", mem, sp["g_mem"])
    q2 = _mm1("mm_q", h2, W["q"], "nn", D_MODEL, BF16)
    kv = _mm1("mm_kv", mn, W["kv"], "nn3", 2 * D_MODEL, BF16, piece=2 * D_MODEL // N_DEV)
    o2 = _xattn_fwd(q2, kv)
    xo, x2, h3 = _mm_resid_norm("mm_sq_norm", o2, W["o"], x1, sp["g_x_post"], sp["g_ffn_pre"])
    up, act = _mm_rows("mm_up", [(h3, W["up"], "nn3")], [(BF16, 0, D_FF, "id"), (BF16, 0, D_FF, "relu2")],
                       piece=D_FF // N_DEV)
    y, x3, h_next = _mm_resid_norm("mm_down_norm" if g_next is not None else "mm_down_norm_last", act, W["down"], x2,
                                   sp["g_ffn_post"], g_next)
    sv.update(h1=h1, uf=uf, cT=cT, qkv=qkv, o=o, lse=lse, cat=cat, mix=mix, x1=x1, h2=h2, mn=mn, q2=q2, kv=kv,
              o2=o2, xo=xo, x2=x2, h3=h3, up=up, act=act, y=y)
    return x3, h_next, sv, W, (got[6] if len(got) > 6 else None)


def _layer_bwd(dx3, dy, mem, sv, sp, W, carried, below):
    S = dx3.shape[0]
    gs = {}
    gb = {}
    (dup,) = _mm_rows("mm_dup", [(dy, W["down"], "nt")], [(BF16, 0, D_FF, "drelu2")], extra=sv["up"])
    gb["w_down"] = _mm_tn("mm_dw_down", sv["act"], dy, BF16).reshape(N_DEV, D_FF // N_DEV, D_MODEL)
    gb["w_up"] = _mm_tn("mm_dw_up", sv["h3"], dup, BF16, shard_cols=D_FF // N_DEV)
    dx2, gs["g_ffn_pre"], dxo, gs["g_x_post"] = _norm_bwd(
        "mm_dh3_norm_bwd", (dup, W["up"], "nt3"), sv["x2"], sp["g_ffn_pre"], dx3, F32,
        below=(sv["xo"], sp["g_x_post"]))
    do2 = _mm1("mm_sq_t", dxo, W["o"], "nt", D_MODEL, BF16)
    gb["wo_x"] = _mm_tn("mm_dw_sq", sv["o2"], dxo, BF16).reshape(N_DEV, D_MODEL // N_DEV, D_MODEL)
    dq2, dkvb = _xattn_bwd(sv["q2"], sv["kv"], do2)
    gb["wq_x"] = _mm_tn("mm_dw_sq", sv["h2"], dq2, BF16).reshape(N_DEV, D_MODEL // N_DEV, D_MODEL)
    gb["wkv_x"] = _mm_tn("mm_dw_kv", sv["mn"], dkvb, BF16, shard_cols=2 * D_MODEL // N_DEV)
    dmn = _mm1("mm_dmn", dkvb, W["kv"], "nt3", D_MODEL, F32)
    _, gs["g_mem"] = _norm_bwd("norm_bwd_mem", dmn, mem, sp["g_mem"], None, BF16)
    dx1, gs["g_x_pre"], dmix, gs["g_mix_post"] = _norm_bwd(
        "mm_dh2_norm_bwd", (dq2, W["q"], "nt"), sv["x1"], sp["g_x_pre"], dx2, F32,
        below=(sv["mix"], sp["g_mix_post"]))
    doh, dpool = _mm_rows("mm_dcat", [(dmix, W["out"], "nt")],
                          [(BF16, 0, FOX_WIDTH, "id"), (F32, FOX_WIDTH, POOL_WIDTH, "id")])
    gb["w_out"] = _mm_tn("mm_dw_sq", sv["cat"], dmix, BF16).reshape(N_DEV, D_MODEL // N_DEV, D_MODEL)
    du, gs["pool_w"], gs["pool_scale"] = _pool_bwd(sv["uf"], dpool, sp["pool_w"], sp["pool_scale"])
    dq, dk, dv, dcT, *got = _fox_bwd(sv["qkv"], sv["cT"], sv["o"], sv["lse"], doh, [gb[n] for n in REST] + carried)
    dc = jnp.pad(jnp.transpose(dcT.reshape(FOX_HEADS, S)), ((0, 0), (0, LANES - FOX_HEADS)))
    dfg, db = _gate_bwd(dc, sv["uf"], sp["b_forget"])
    gs["b_forget"] = db[:, :FOX_HEADS]
    dproj = [dq, dk, dv, du, dfg]
    dwp = _mm_tn_rows("mm_dw_in", dproj, sv["h1"], BF16)
    dh1 = (dproj, W["inp"], "nn")
    if below is None:
        dx0, gs["g_mix_pre"] = _norm_bwd("mm_dh1_norm_bwd_first", dh1, sv["x0"], sp["g_mix_pre"], dx1, F32)
        lower = None
    else:
        dx0, gs["g_mix_pre"], *lower = _norm_bwd("mm_dh1_norm_bwd", dh1, sv["x0"], sp["g_mix_pre"], dx1, F32,
                                                 below=below)
    return dx0, lower, dict(zip(REST, got[:6])), got[6:], _pack_dw_in(dwp), gs


def _small_rows(shape):
    return -(-math.prod(shape) // (8 * LANES)) * 8


def _pack_small(d):
    blocks = []
    for n in SMALL:
        rows = _small_rows(d[n].shape)
        if d[n].shape[-1] == LANES:
            blocks.append(d[n].reshape(rows, LANES))
        else:
            flat = d[n].reshape(-1)
            blocks.append(jnp.pad(flat, (0, rows * LANES - flat.shape[0])).reshape(rows, LANES))
    return jnp.concatenate(blocks, axis=0)


def _unpack_small(packed, like):
    out = {}
    row = 0
    for n in SMALL:
        shape = like[n].shape
        rows = _small_rows(shape)
        block = packed[row:row + rows]
        out[n] = block.reshape(shape) if shape[-1] == LANES else block.reshape(-1)[:math.prod(shape)].reshape(shape)
        row += rows
    return out


def kernel(x, mem, g_mix_pre, w_in, b_forget, pool_w, pool_scale, w_out, g_mix_post, g_x_pre, g_mem, wq_x, wkv_x, wo_x, g_x_post, g_ffn_pre, w_up, w_down, g_ffn_post, loss_target, m_g_mix_pre, m_w_in, m_b_forget, m_pool_w, m_pool_scale, m_w_out, m_g_mix_post, m_g_x_pre, m_g_mem, m_wq_x, m_wkv_x, m_wo_x, m_g_x_post, m_g_ffn_pre, m_w_up, m_w_down, m_g_ffn_post, v_g_mix_pre, v_w_in, v_b_forget, v_pool_w, v_pool_scale, v_w_out, v_g_mix_post, v_g_x_pre, v_g_mem, v_wq_x, v_wkv_x, v_wo_x, v_g_x_post, v_g_ffn_pre, v_w_up, v_w_down, v_g_ffn_post):
    w = dict(g_mix_pre=g_mix_pre, w_in=w_in, b_forget=b_forget, pool_w=pool_w, pool_scale=pool_scale, w_out=w_out,
             g_mix_post=g_mix_post, g_x_pre=g_x_pre, g_mem=g_mem, wq_x=wq_x, wkv_x=wkv_x, wo_x=wo_x,
             g_x_post=g_x_post, g_ffn_pre=g_ffn_pre, w_up=w_up, w_down=w_down, g_ffn_post=g_ffn_post)
    mom = dict(g_mix_pre=m_g_mix_pre, w_in=m_w_in, b_forget=m_b_forget, pool_w=m_pool_w, pool_scale=m_pool_scale,
               w_out=m_w_out, g_mix_post=m_g_mix_post, g_x_pre=m_g_x_pre, g_mem=m_g_mem, wq_x=m_wq_x,
               wkv_x=m_wkv_x, wo_x=m_wo_x, g_x_post=m_g_x_post, g_ffn_pre=m_g_ffn_pre, w_up=m_w_up,
               w_down=m_w_down, g_ffn_post=m_g_ffn_post)
    var = dict(g_mix_pre=v_g_mix_pre, w_in=v_w_in, b_forget=v_b_forget, pool_w=v_pool_w, pool_scale=v_pool_scale,
               w_out=v_w_out, g_mix_post=v_g_mix_post, g_x_pre=v_g_x_pre, g_mem=v_g_mem, wq_x=v_wq_x,
               wkv_x=v_wkv_x, wo_x=v_wo_x, g_x_post=v_g_x_post, g_ffn_pre=v_g_ffn_pre, w_up=v_w_up,
               w_down=v_w_down, g_ffn_post=v_g_ffn_post)
    S = x.shape[1]
    xs = x.reshape(S, D_MODEL)
    mems = mem.reshape(MEM_LEN, D_MODEL)
    target = loss_target.reshape(S, D_MODEL)

    def small_params(l):
        return dict(
            g_mix_pre=_vec(g_mix_pre[l]), g_mix_post=_vec(g_mix_post[l]), g_x_pre=_vec(g_x_pre[l]),
            g_mem=_vec(g_mem[l]), g_x_post=_vec(g_x_post[l]), g_ffn_pre=_vec(g_ffn_pre[l]),
            g_ffn_post=_vec(g_ffn_post[l]), pool_scale=_vec(pool_scale[l]), pool_w=pool_w[l].astype(BF16),
            b_forget=jnp.pad(_vec(b_forget[l]), ((0, 0), (0, LANES - FOX_HEADS))))

    shard = {n: [w[n][l].astype(BF16) for l in range(DEPTH)] for n in REST}
    shard["w_in"] = [_w_in_travel(w_in[l].astype(BF16)) for l in range(DEPTH)]
    sps = [small_params(l) for l in range(DEPTH)]
    saved, weights = [], []
    h = xs
    (g_in,) = _exchange("gather_w_in", [shard["w_in"][0]])
    hn = _norm_fwd("norm_fwd", xs, sps[0]["g_mix_pre"])
    for l in range(DEPTH):
        travelling = [shard[n][l] for n in REST] + ([shard["w_in"][l + 1]] if l + 1 < DEPTH else [])
        g_next = sps[l + 1]["g_mix_pre"] if l + 1 < DEPTH else None
        h, hn, sv, W, g_in = _layer_fwd(h, hn, mems, sps[l], g_in, travelling, g_next)
        saved.append(sv)
        weights.append(W)
    dh, sq = _loss_fwd_bwd(h, target)
    loss = lax.psum(0.5 * sq[0, 0] / D_MODEL, ("x", "y", "c"))

    parts = [dict() for _ in range(DEPTH)]
    small_grads = [None] * DEPTH
    carried = []
    lower = _norm_bwd("norm_bwd_b", dh, saved[-1]["y"], sps[-1]["g_ffn_post"], None, BF16)
    for l in reversed(range(DEPTH)):
        dy, dg_ffn_post = lower
        below = (saved[l - 1]["y"], sps[l - 1]["g_ffn_post"]) if l > 0 else None
        dh, lower, got, got_carried, dw_in, gs = _layer_bwd(dh, dy, mems, saved[l], sps[l], weights[l], carried, below)
        gs["g_ffn_post"] = dg_ffn_post
        parts[l].update(got)
        if got_carried:
            parts[l + 1]["w_in"] = got_carried[0]
        carried = [dw_in]
        small_grads[l] = gs
    grad_x = dh.reshape(1, S, D_MODEL)

    grads, deltas, new_m, new_v = {}, {}, {}, {}
    rows = dict(w_in=128, w_out=128, wq_x=128, wkv_x=256, wo_x=128, w_up=256, w_down=128)
    sg = {n: jnp.stack([small_grads[l][n].reshape(w[n].shape[1:]) for l in range(DEPTH)]) for n in SMALL}
    riders = dict(w_down=carried, w_up=[_pack_small(sg)])
    for n in ["w_down", "w_up", "w_out", "wq_x", "wkv_x", "wo_x", "w_in"]:
        if n == "w_in":
            for l in range(DEPTH):
                parts[l]["w_in"] = jnp.swapaxes(parts[l]["w_in"][:, :W_IN_SHARD, :], 1, 2)
        grads[n], deltas[n], new_m[n], new_v[n], *got = _adamw_big(
            "adamw_" + n, w[n], mom[n], var[n], [parts[l][n] for l in range(DEPTH)], rows[n], riders.get(n, ()))
        if n == "w_down":
            (parts[0]["w_in"],) = got
        elif n == "w_up":
            (sg_parts,) = got
    outs = _adamw_small(_pack_small(w), _pack_small(mom), _pack_small(var), sg_parts)
    for d, packed in zip((grads, deltas, new_m, new_v), outs):
        d.update(_unpack_small(packed, w))

    return (loss, grad_x, *[grads[n] for n in W_NAMES], *[deltas[n] for n in W_NAMES],
            *[new_m[n] for n in W_NAMES], *[new_v[n] for n in W_NAMES])
```
